```python
import jax, jax.numpy as jnp
from jax import lax
import numpy as np

D_MODEL = 1024
BATCH = 8
SEQ = 2048
DEPTH = 1
DEC_BATCH = 128
DEC_SEQ = 8
PAST_LEN = 16384
PAGE_SIZE = 128

RWKV_HEADS = 8
RWKV_HEAD_DIM = 64
RWKV_WIDTH = RWKV_HEADS * RWKV_HEAD_DIM
DECAY_LORA = 64
A_LORA = 64
GATE_LORA = 128
GMLP_GROUPS = 4
GMLP_GROUP_DIM = 64
GMLP_WIDTH = GMLP_GROUPS * GMLP_GROUP_DIM
CHUNK = 128
MEM_HEADS = 4
MEM_HEAD_DIM = 64
MEM_WIDTH = MEM_HEADS * MEM_HEAD_DIM
N_MEM = 256
N_BRANCH = 3
N_EXPERTS = 32
TOP_K = 4
D_FF_EXPERT = 1024
SWIGLU_ALPHA = 1.702
SWIGLU_LIMIT = 7.0
MOE_BLOCK = 128
RMS_EPS = 1e-5
LN_EPS = 1e-5
GN_EPS = 64e-5

SHIFT_WIDTH = 3 * RWKV_WIDTH + DECAY_LORA + A_LORA + GATE_LORA
OFF_GMLP = SHIFT_WIDTH
OFF_QMEM = OFF_GMLP + 2 * GMLP_WIDTH
OFF_GATE = OFF_QMEM + MEM_WIDTH
IN_WIDTH = OFF_GATE + N_BRANCH * D_MODEL

kernel_name = "hybrid_rwkv7_gmlp_memattn_moe_step"

F32 = jnp.float32


def _rmsnorm(x, g):
    xf = x.astype(F32)
    return (xf * lax.rsqrt(jnp.mean(xf * xf, axis=-1, keepdims=True) + RMS_EPS) * g).astype(x.dtype)


def _wkv_scan(r, w, k, v, kk, a, S0):
    def step(S, inp):
        r_t, w_t, k_t, v_t, kk_t, a_t = inp
        sa = jnp.einsum('bhvk,bhk->bhv', S, -kk_t)
        S = (S * w_t[:, :, None, :] + sa[..., None] * (kk_t * a_t)[:, :, None, :]
             + v_t[..., None] * k_t[:, :, None, :])
        return S, jnp.einsum('bhvk,bhk->bhv', S, r_t)
    xs = tuple(jnp.moveaxis(t, 1, 0) for t in (r, w, k, v, kk, a))
    S, ys = lax.scan(step, S0, xs)
    return jnp.moveaxis(ys, 0, 1), S


def _rwkv_branch(z, shift_prev, S_prev, p):
    B, T, _ = z.shape
    H, N, RW = RWKV_HEADS, RWKV_HEAD_DIM, RWKV_WIDTH
    z_prev = jnp.concatenate([shift_prev[:, None, :].astype(z.dtype), z[:, :-1]], axis=1)
    zs = z + p['mu_shift'] * (z_prev - z)
    r, k, v, zw, za, zg = jnp.split(zs, [RW, 2 * RW, 3 * RW, 3 * RW + DECAY_LORA, 3 * RW + DECAY_LORA + A_LORA], axis=-1)
    logw = -jax.nn.softplus(-(p['w0'] + jnp.tanh(zw) @ p['w_decay_up']).astype(F32)) - 0.5
    decay = jnp.exp(-jnp.exp(logw)).reshape(B, T, H, N)
    a = jax.nn.sigmoid(p['a0'] + za @ p['w_a_up'])
    g = jax.nn.sigmoid(zg) @ p['w_g_up']
    kk = (k * p['k_k']).reshape(B, T, H, N).astype(F32)
    kk = kk / jnp.maximum(jnp.sqrt(jnp.sum(kk * kk, axis=-1, keepdims=True)), 1e-12)
    k = k * (1 + (a - 1) * p['k_a'])
    rh, kh, vh, ah = (t.reshape(B, T, H, N).astype(F32) for t in (r, k, v, a))
    y, S = _wkv_scan(rh, decay, kh, vh, kk, ah, S_prev.astype(F32))
    mu = jnp.mean(y, axis=-1, keepdims=True)
    var = jnp.mean((y - mu) ** 2, axis=-1, keepdims=True)
    y = ((y - mu) * lax.rsqrt(var + GN_EPS)).reshape(B, T, RW) * p['ln_x_g'] + p['ln_x_b']
    bonus = jnp.sum(rh * kh * p['r_k'], axis=-1, keepdims=True) * vh
    y = (y + bonus.reshape(B, T, RW)).astype(z.dtype) * g
    return y, z[:, -1], S


def _gmlp_branch(z, p):
    B, T, _ = z.shape
    u, v = jnp.split(jax.nn.gelu(z, approximate=False), 2, axis=-1)
    vf = v.astype(F32)
    mu = jnp.mean(vf, axis=-1, keepdims=True)
    var = jnp.mean((vf - mu) ** 2, axis=-1, keepdims=True)
    v = ((vf - mu) * lax.rsqrt(var + LN_EPS) * p['gmlp_ln_g'] + p['gmlp_ln_b']).astype(z.dtype)
    L = min(T, CHUNK)
    mask = jnp.tril(jnp.ones((L, L), bool))
    ws = jnp.where(mask, p['w_spatial'][:, :L, :L], 0).astype(v.dtype)
    vc = v.reshape(B, T // L, L, GMLP_GROUPS, GMLP_GROUP_DIM)
    sv = jnp.einsum('gij,bnjgc->bnigc', ws, vc) + p['b_spatial'][:, :L].T[:, :, None]
    return u * sv.reshape(B, T, GMLP_WIDTH), v


def _mem_kv(mem, g, w):
    B, M, _ = mem.shape
    k, v = jnp.split(_rmsnorm(mem, g) @ w, 2, axis=-1)
    return k.reshape(B, M, MEM_HEADS, MEM_HEAD_DIM), v.reshape(B, M, MEM_HEADS, MEM_HEAD_DIM)


def _mem_attn(q, mk, mv):
    B, T, _ = q.shape
    qh = q.reshape(B, T, MEM_HEADS, MEM_HEAD_DIM)
    s = jnp.einsum('bthd,bmhd->bhtm', qh, mk.astype(q.dtype)).astype(F32) * (MEM_HEAD_DIM ** -0.5)
    pr = jax.nn.softmax(s, axis=-1).astype(q.dtype)
    return jnp.einsum('bhtm,bmhd->bthd', pr, mv.astype(q.dtype)).reshape(B, T, MEM_WIDTH)


def _moe(h, p):
    T, D = h.shape
    logits = (h @ p['w_router'] + p['b_router']).astype(F32)
    top_v, top_i = lax.top_k(logits, TOP_K)
    gate = jax.nn.softmax(top_v, axis=-1)
    n = T * TOP_K
    flat_e = top_i.reshape(-1)
    order = jnp.argsort(flat_e)
    e_sorted = flat_e[order]
    tok = (order // TOP_K).astype(jnp.int32)
    counts = jnp.bincount(flat_e, length=N_EXPERTS)
    padded = (counts + MOE_BLOCK - 1) // MOE_BLOCK * MOE_BLOCK
    pad_end = jnp.cumsum(padded)
    start = jnp.cumsum(counts) - counts
    dest = (pad_end - padded)[e_sorted] + jnp.arange(n) - start[e_sorted]
    n_blocks = -(-n // MOE_BLOCK) + N_EXPERTS
    cap = n_blocks * MOE_BLOCK
    src = jnp.full((cap,), T, jnp.int32).at[dest].set(tok)
    h_pad = jnp.concatenate([h, jnp.zeros((1, D), h.dtype)], axis=0)
    xb = h_pad[src].reshape(n_blocks, MOE_BLOCK, D)
    blk_e = jnp.minimum(jnp.searchsorted(pad_end, jnp.arange(n_blocks) * MOE_BLOCK, side='right'), N_EXPERTS - 1)

    def expert_block(args):
        xe, e = args
        zz = xe @ p['w_exp1'][e] + p['b_exp1'][e]
        zg, zl = jnp.split(zz, 2, axis=-1)
        zg = jnp.minimum(zg, SWIGLU_LIMIT)
        zl = jnp.clip(zl, -SWIGLU_LIMIT, SWIGLU_LIMIT)
        act = zg * jax.nn.sigmoid(SWIGLU_ALPHA * zg) * (zl + 1)
        return act @ p['w_exp2'][e] + p['b_exp2'][e]

    yb = lax.map(expert_block, (xb, blk_e)).reshape(cap, D)
    contrib = yb[dest].astype(F32) * gate.reshape(-1)[order][:, None]
    return jnp.zeros((T, D), F32).at[tok].add(contrib).astype(h.dtype)


def _layer(x, shift_prev, S_prev, mem_k, mem_v, p):
    B, T, D = x.shape
    h = _rmsnorm(x, p['g_norm_mix'])
    z = h @ p['w_in']
    o_rwkv, shift_new, S_new = _rwkv_branch(z[..., :SHIFT_WIDTH], shift_prev, S_prev, p)
    o_gmlp, v_rows = _gmlp_branch(z[..., OFF_GMLP:OFF_QMEM], p)
    o_mem = _mem_attn(z[..., OFF_QMEM:OFF_GATE], mem_k, mem_v)
    gates = jax.nn.sigmoid(z[..., OFF_GATE:]).reshape(B, T, N_BRANCH, D)
    merged = (gates[..., 0, :] * (o_rwkv @ p['w_br_rwkv'])
              + gates[..., 1, :] * (o_gmlp @ p['w_br_gmlp'])
              + gates[..., 2, :] * (o_mem @ p['w_br_mem']))
    x = x + merged @ p['w_out']
    x = x + _moe(_rmsnorm(x, p['g_norm_ffn']).reshape(B * T, D), p).reshape(B, T, D)
    return x, shift_new, S_new, v_rows


def setup_inputs(seed: int = 0) -> dict:
    key = jax.random.key(seed)
    ks = iter(jax.random.split(key, 64))

    def nrm(shape, scale):
        return jax.random.normal(next(ks), shape, F32) * scale

    def gain(shape):
        return 1.0 + nrm(shape, 0.02)

    H, N = RWKV_HEADS, RWKV_HEAD_DIM
    w0_base = jnp.tile(jnp.linspace(-6.0, -1.0, N, dtype=F32), H)
    return {
        'x_prompt': nrm((BATCH, SEQ, D_MODEL), 1.0),
        'x_sample': nrm((DEC_BATCH, DEC_SEQ, D_MODEL), 1.0),
        'mem_prompt': nrm((BATCH, N_MEM, D_MODEL), 1.0),
        'state_shift': nrm((DEPTH, DEC_BATCH, SHIFT_WIDTH), 1.0),
        'state_wkv': nrm((DEPTH, DEC_BATCH, H, N, N), 0.3),
        'cache_mem_k': nrm((DEPTH, DEC_BATCH, N_MEM, MEM_HEADS, MEM_HEAD_DIM), 1.0),
        'cache_mem_v': nrm((DEPTH, DEC_BATCH, N_MEM, MEM_HEADS, MEM_HEAD_DIM), 1.0),
        'g_norm_mix': gain((DEPTH, D_MODEL)),
        'w_in': nrm((DEPTH, D_MODEL, IN_WIDTH), D_MODEL ** -0.5),
        'mu_shift': jax.random.uniform(next(ks), (DEPTH, SHIFT_WIDTH), F32, 0.1, 0.9),
        'w0': w0_base[None, :] + nrm((DEPTH, RWKV_WIDTH), 0.1),
        'w_decay_up': nrm((DEPTH, DECAY_LORA, RWKV_WIDTH), 0.1 * DECAY_LORA ** -0.5),
        'a0': nrm((DEPTH, RWKV_WIDTH), 0.1),
        'w_a_up': nrm((DEPTH, A_LORA, RWKV_WIDTH), A_LORA ** -0.5),
        'w_g_up': nrm((DEPTH, GATE_LORA, RWKV_WIDTH), GATE_LORA ** -0.5),
        'k_k': 0.85 + nrm((DEPTH, RWKV_WIDTH), 0.02),
        'k_a': gain((DEPTH, RWKV_WIDTH)),
        'r_k': nrm((DEPTH, H, N), 0.1),
        'ln_x_g': gain((DEPTH, RWKV_WIDTH)),
        'ln_x_b': nrm((DEPTH, RWKV_WIDTH), 0.01),
        'gmlp_ln_g': gain((DEPTH, GMLP_WIDTH)),
        'gmlp_ln_b': nrm((DEPTH, GMLP_WIDTH), 0.01),
        'w_spatial': nrm((DEPTH, GMLP_GROUPS, CHUNK, CHUNK), CHUNK ** -0.5),
        'b_spatial': 1.0 + nrm((DEPTH, GMLP_GROUPS, CHUNK), 0.01),
        'g_norm_mem': gain((DEPTH, D_MODEL)),
        'w_mem_kv': nrm((DEPTH, D_MODEL, 2 * MEM_WIDTH), D_MODEL ** -0.5),
        'w_br_rwkv': nrm((DEPTH, RWKV_WIDTH, D_MODEL), RWKV_WIDTH ** -0.5),
        'w_br_gmlp': nrm((DEPTH, GMLP_WIDTH, D_MODEL), GMLP_WIDTH ** -0.5),
        'w_br_mem': nrm((DEPTH, MEM_WIDTH, D_MODEL), MEM_WIDTH ** -0.5),
        'w_out': nrm((DEPTH, D_MODEL, D_MODEL), D_MODEL ** -0.5),
        'g_norm_ffn': gain((DEPTH, D_MODEL)),
        'w_router': nrm((DEPTH, D_MODEL, N_EXPERTS), D_MODEL ** -0.5),
        'b_router': nrm((DEPTH, N_EXPERTS), 0.01),
        'w_exp1': nrm((DEPTH, N_EXPERTS, D_MODEL, 2 * D_FF_EXPERT), D_MODEL ** -0.5),
        'b_exp1': nrm((DEPTH, N_EXPERTS, 2 * D_FF_EXPERT), 0.01),
        'w_exp2': nrm((DEPTH, N_EXPERTS, D_FF_EXPERT, D_MODEL), D_FF_EXPERT ** -0.5),
        'b_exp2': nrm((DEPTH, N_EXPERTS, D_MODEL), 0.01),
        'g_norm_final': gain((D_MODEL,)),
    }


def reference(x_prompt, x_sample, mem_prompt, state_shift, state_wkv, cache_mem_k, cache_mem_v,
              g_norm_mix, w_in, mu_shift, w0, w_decay_up, a0, w_a_up, w_g_up, k_k, k_a, r_k,
              ln_x_g, ln_x_b, gmlp_ln_g, gmlp_ln_b, w_spatial, b_spatial, g_norm_mem, w_mem_kv,
              w_br_rwkv, w_br_gmlp, w_br_mem, w_out, g_norm_ffn, w_router, b_router,
              w_exp1, b_exp1, w_exp2, b_exp2, g_norm_final):
    xp, xs = x_prompt, x_sample
    Bp = xp.shape[0]
    sp_l, Sp_l, mkp_l, mvp_l, ss_l, Ss_l, vs_l = [], [], [], [], [], [], []
    for l in range(DEPTH):
        p = dict(g_norm_mix=g_norm_mix[l], w_in=w_in[l], mu_shift=mu_shift[l], w0=w0[l],
                 w_decay_up=w_decay_up[l], a0=a0[l], w_a_up=w_a_up[l], w_g_up=w_g_up[l],
                 k_k=k_k[l], k_a=k_a[l], r_k=r_k[l], ln_x_g=ln_x_g[l], ln_x_b=ln_x_b[l],
                 gmlp_ln_g=gmlp_ln_g[l], gmlp_ln_b=gmlp_ln_b[l], w_spatial=w_spatial[l],
                 b_spatial=b_spatial[l], w_br_rwkv=w_br_rwkv[l], w_br_gmlp=w_br_gmlp[l],
                 w_br_mem=w_br_mem[l], w_out=w_out[l], g_norm_ffn=g_norm_ffn[l],
                 w_router=w_router[l], b_router=b_router[l], w_exp1=w_exp1[l], b_exp1=b_exp1[l],
                 w_exp2=w_exp2[l], b_exp2=b_exp2[l])
        mk_p, mv_p = _mem_kv(mem_prompt, g_norm_mem[l], w_mem_kv[l])
        shift0 = jnp.zeros((Bp, SHIFT_WIDTH), xp.dtype)
        S0 = jnp.zeros((Bp, RWKV_HEADS, RWKV_HEAD_DIM, RWKV_HEAD_DIM), F32)
        xp, sp, Sp, _ = _layer(xp, shift0, S0, mk_p, mv_p, p)
        xs, ss, Ss, vs = _layer(xs, state_shift[l], state_wkv[l], cache_mem_k[l], cache_mem_v[l], p)
        sp_l.append(sp); Sp_l.append(Sp.astype(state_wkv.dtype)); mkp_l.append(mk_p); mvp_l.append(mv_p)
        ss_l.append(ss.astype(state_shift.dtype)); Ss_l.append(Ss.astype(state_wkv.dtype)); vs_l.append(vs)
    y_prompt = _rmsnorm(xp, g_norm_final)
    y_sample = _rmsnorm(xs, g_norm_final)
    return (y_prompt, y_sample,
            jnp.stack(sp_l), jnp.stack(Sp_l), jnp.stack(mkp_l), jnp.stack(mvp_l),
            jnp.stack(ss_l), jnp.stack(Ss_l), jnp.stack(vs_l))
```

```python
import functools
import math

import jax
import jax.numpy as jnp
from jax import lax
from jax.experimental import pallas as pl
from jax.experimental.pallas import tpu as pltpu

F32 = jnp.float32
BF16 = jnp.bfloat16

D_MODEL = 1024
RWKV_HEADS = 8
HEAD_DIM = 64
RWKV_WIDTH = RWKV_HEADS * HEAD_DIM
DECAY_LORA = 64
A_LORA = 64
GATE_LORA = 128
GMLP_GROUPS = 4
GMLP_WIDTH = 256
CHUNK = 128
MEM_HEADS = 4
MEM_WIDTH = 256
N_MEM = 256
N_EXPERTS = 32
TOP_K = 4
D_FF = 1024
SWIGLU_ALPHA = 1.702
SWIGLU_LIMIT = 7.0
RMS_EPS = 1e-5
LN_EPS = 1e-5
GN_EPS = 64e-5
SHIFT_WIDTH = 3 * RWKV_WIDTH + DECAY_LORA + A_LORA + GATE_LORA
OFF_GMLP = SHIFT_WIDTH
OFF_QMEM = OFF_GMLP + 2 * GMLP_WIDTH
OFF_GATE = OFF_QMEM + MEM_WIDTH
IN_WIDTH = OFF_GATE + 3 * D_MODEL
LANES = 128

NN = ((1,), (0,))
NT = ((1,), (1,))
TN = ((0,), (0,))

VMEM_LIMIT = 56 * 1024 * 1024


def _params(sem, vmem=VMEM_LIMIT):
    return pltpu.CompilerParams(dimension_semantics=sem, vmem_limit_bytes=vmem)


def _bdot(a, b, dims=NN):
    return lax.dot_general(a.astype(BF16), b.astype(BF16), (dims, ((), ())), preferred_element_type=F32)


def _split(x):
    hi = x.astype(BF16)
    lo = (x - hi.astype(F32)).astype(BF16)
    return hi, lo


def _dot3(a, b, dims=NN):
    dn = (dims, ((), ()))
    ah, al = _split(a)
    bh, bl = _split(b)
    r = lax.dot_general(ah, bh, dn, preferred_element_type=F32)
    r = r + lax.dot_general(al, bh, dn, preferred_element_type=F32)
    return r + lax.dot_general(ah, bl, dn, preferred_element_type=F32)


def _rms(x, g):
    return x * lax.rsqrt(jnp.mean(x * x, axis=-1, keepdims=True) + RMS_EPS) * g


def _in_proj_body(x_ref, g_ref, w_ref, zr_ref, zg_ref, zq_ref, sg_ref):
    h = _rms(x_ref[...], g_ref[...]).astype(BF16)
    zr_ref[...] = jnp.dot(h, w_ref[:, 0:OFF_GMLP], preferred_element_type=F32)
    zg_ref[...] = jnp.dot(h, w_ref[:, OFF_GMLP:OFF_QMEM], preferred_element_type=F32)
    zq_ref[...] = jnp.dot(h, w_ref[:, OFF_QMEM:OFF_GATE], preferred_element_type=F32)
    gates = jnp.dot(h, w_ref[:, OFF_GATE:IN_WIDTH], preferred_element_type=F32)
    sg_ref[...] = jax.nn.sigmoid(gates).astype(BF16)


def _in_proj(x, g, w_bf16, tm=256):
    n = x.shape[0]
    row = lambda i: (i, 0)
    const = lambda i: (0, 0)
    return pl.pallas_call(
        _in_proj_body,
        grid=(n // tm,),
        in_specs=[
            pl.BlockSpec((tm, D_MODEL), row),
            pl.BlockSpec((1, D_MODEL), const),
            pl.BlockSpec((D_MODEL, IN_WIDTH), const, pipeline_mode=pl.Buffered(1)),
        ],
        out_specs=[
            pl.BlockSpec((tm, SHIFT_WIDTH), row),
            pl.BlockSpec((tm, 2 * GMLP_WIDTH), row),
            pl.BlockSpec((tm, MEM_WIDTH), row),
            pl.BlockSpec((tm, 3 * D_MODEL), row),
        ],
        out_shape=[
            jax.ShapeDtypeStruct((n, SHIFT_WIDTH), F32),
            jax.ShapeDtypeStruct((n, 2 * GMLP_WIDTH), F32),
            jax.ShapeDtypeStruct((n, MEM_WIDTH), F32),
            jax.ShapeDtypeStruct((n, 3 * D_MODEL), BF16),
        ],
        compiler_params=_params(("parallel",)),
        name="in_proj",
    )(x, g, w_bf16)


def _mem_kv_body(x_ref, g_ref, w_ref, o_ref):
    h = _rms(x_ref[...], g_ref[...]).astype(BF16)
    o_ref[...] = jnp.dot(h, w_ref[...], preferred_element_type=F32)


def _mem_kv(mem, g, w_bf16, tm=256):
    n = mem.shape[0]
    return pl.pallas_call(
        _mem_kv_body,
        grid=(n // tm,),
        in_specs=[
            pl.BlockSpec((tm, D_MODEL), lambda i: (i, 0)),
            pl.BlockSpec((1, D_MODEL), lambda i: (0, 0)),
            pl.BlockSpec((D_MODEL, 2 * MEM_WIDTH), lambda i: (0, 0)),
        ],
        out_specs=pl.BlockSpec((tm, 2 * MEM_WIDTH), lambda i: (i, 0)),
        out_shape=jax.ShapeDtypeStruct((n, 2 * MEM_WIDTH), F32),
        compiler_params=_params(("parallel",)),
        name="mem_kv",
    )(mem, g, w_bf16)


EXP_M05 = math.exp(-0.5)


def _unit_lower_inverse(low, c):
    rows = lax.broadcasted_iota(jnp.int32, (c, c), 0)
    cols = lax.broadcasted_iota(jnp.int32, (c, c), 1)
    eye = (rows == cols).astype(F32)
    inv = eye - low
    power = low
    for _ in range(int(math.log2(c)) - 1):
        power = _dot3(power, power)
        inv = inv + _dot3(inv, power)
    return inv


def _rwkv_body(zr_ref, shift_ref, s0_ref, mu_ref, w0_ref, wd_ref, a0_ref, wa_ref, wg_ref, kk_ref, ka_ref,
               rk_ref, lng_ref, lnb_ref,
               o_ref, sout_ref,
               carry_ref, state_ref, r_s, k_s, v_s, kk_s, b_s, ld_s, y_s, bon_s, g_s, *, tt, c):
    i = pl.program_id(1)
    rw = RWKV_WIDTH

    @pl.when(i == 0)
    def _():
        carry_ref[...] = shift_ref[0]
        state_ref[...] = s0_ref[0]

    z = zr_ref[0]
    z_prev = pltpu.roll(z, 1, 0)
    row = lax.broadcasted_iota(jnp.int32, z.shape, 0)
    z_prev = jnp.where(row == 0, carry_ref[...], z_prev)
    carry_ref[...] = z[tt - 1:tt, :]
    zs = z + mu_ref[...] * (z_prev - z)
    r = zs[:, 0:rw]
    k = zs[:, rw:2 * rw]
    v = zs[:, 2 * rw:3 * rw]
    zw = zs[:, 3 * rw:3 * rw + DECAY_LORA]
    za = zs[:, 3 * rw + DECAY_LORA:3 * rw + DECAY_LORA + A_LORA]
    zg = zs[:, 3 * rw + DECAY_LORA + A_LORA:SHIFT_WIDTH]
    xw = w0_ref[...] + _bdot(jnp.tanh(zw), wd_ref[...])
    ld_s[...] = -EXP_M05 * jax.nn.sigmoid(xw)
    a = jax.nn.sigmoid(a0_ref[...] + _bdot(za, wa_ref[...]))
    g_s[...] = _bdot(jax.nn.sigmoid(zg), wg_ref[...])
    kk = k * kk_ref[...]
    k = k * (1.0 + (a - 1.0) * ka_ref[...])
    r_s[...] = r
    k_s[...] = k
    v_s[...] = v
    rkk = r * k * rk_ref[...]
    for h in range(RWKV_HEADS):
        hs = slice(h * HEAD_DIM, (h + 1) * HEAD_DIM)
        kkh = kk[:, hs]
        nrm = jnp.sqrt(jnp.sum(kkh * kkh, axis=-1, keepdims=True))
        kkh = kkh / jnp.maximum(nrm, 1e-12)
        kk_s[:, hs] = kkh
        b_s[:, hs] = kkh * a[:, hs]
        bon_s[:, hs] = jnp.sum(rkk[:, hs], axis=-1, keepdims=True) * v[:, hs]

    rows_c = lax.broadcasted_iota(jnp.int32, (c, c), 0)
    cols_c = lax.broadcasted_iota(jnp.int32, (c, c), 1)
    strict = rows_c > cols_c
    incl = rows_c >= cols_c
    tril_ones = incl.astype(BF16)

    def chunk(ci, carry):
        rows = pl.ds(pl.multiple_of(ci * c, c), c)
        ld = ld_s[rows, :]
        ld_hi, ld_lo = _split(ld)
        cum = (jnp.dot(tril_ones, ld_hi, preferred_element_type=F32)
               + jnp.dot(tril_ones, ld_lo, preferred_element_type=F32))
        e_inc = jnp.exp(cum)
        e_exc = jnp.exp(cum - ld)
        e_neg = jnp.exp(-cum)
        rt = r_s[rows, :] * e_inc
        kkt = kk_s[rows, :] * e_exc
        kt = k_s[rows, :] * e_neg
        bt = b_s[rows, :] * e_neg
        g_end = e_inc[c - 1:c, :]
        bc = bt * g_end
        kc = kt * g_end
        vv = v_s[rows, :]
        for h in range(RWKV_HEADS):
            hs = slice(h * HEAD_DIM, (h + 1) * HEAD_DIM)
            x = jnp.concatenate([kkt[:, hs], rt[:, hs]], axis=0)
            zz = jnp.concatenate([bt[:, hs], kt[:, hs]], axis=0)
            amat = _dot3(x, zz, NT)
            l_b = jnp.where(strict, amat[0:c, 0:c], 0.0)
            l_k = jnp.where(strict, amat[0:c, c:2 * c], 0.0)
            a_rb = jnp.where(incl, amat[c:2 * c, 0:c], 0.0)
            a_rk = jnp.where(incl, amat[c:2 * c, c:2 * c], 0.0)
            tinv = _unit_lower_inverse(l_b, c)
            s_h = state_ref[h]
            xs = _dot3(x, s_h, NT)
            v_h = vv[:, hs]
            u = -_dot3(tinv, xs[0:c, :] + _dot3(l_k, v_h))
            y = xs[c:2 * c, :] + _dot3(a_rb, u) + _dot3(a_rk, v_h)
            uv = jnp.concatenate([u, v_h], axis=0)
            bk = jnp.concatenate([bc[:, hs], kc[:, hs]], axis=0)
            state_ref[h] = s_h * g_end[:, hs] + _dot3(uv, bk, TN)
            mean = jnp.mean(y, axis=-1, keepdims=True)
            yc = y - mean
            var = jnp.mean(yc * yc, axis=-1, keepdims=True)
            y_s[rows, hs] = yc * lax.rsqrt(var + GN_EPS)
        return carry

    lax.fori_loop(0, tt // c, chunk, 0)
    o_ref[0] = (y_s[...] * lng_ref[...] + lnb_ref[...] + bon_s[...]) * g_s[...]

    @pl.when(i == pl.num_programs(1) - 1)
    def _():
        sout_ref[0] = state_ref[...]


def _rwkv(zr, shift_prev, s_prev, p, tt, c):
    b, t, _ = zr.shape
    vec = lambda n: pl.BlockSpec((1, n), lambda bi, ti: (0, 0))
    mat = lambda m, n: pl.BlockSpec((m, n), lambda bi, ti: (0, 0))
    rw = RWKV_WIDTH
    scr = lambda: pltpu.VMEM((tt, rw), F32)
    return pl.pallas_call(
        functools.partial(_rwkv_body, tt=tt, c=c),
        grid=(b, t // tt),
        in_specs=[
            pl.BlockSpec((1, tt, SHIFT_WIDTH), lambda bi, ti: (bi, ti, 0)),
            pl.BlockSpec((1, 1, SHIFT_WIDTH), lambda bi, ti: (bi, 0, 0)),
            pl.BlockSpec((1, RWKV_HEADS, HEAD_DIM, HEAD_DIM), lambda bi, ti: (bi, 0, 0, 0)),
            vec(SHIFT_WIDTH), vec(rw), mat(DECAY_LORA, rw), vec(rw), mat(A_LORA, rw), mat(GATE_LORA, rw),
            vec(rw), vec(rw), vec(rw), vec(rw), vec(rw),
        ],
        out_specs=[
            pl.BlockSpec((1, tt, rw), lambda bi, ti: (bi, ti, 0)),
            pl.BlockSpec((1, RWKV_HEADS, HEAD_DIM, HEAD_DIM), lambda bi, ti: (bi, 0, 0, 0)),
        ],
        out_shape=[
            jax.ShapeDtypeStruct((b, t, rw), F32),
            jax.ShapeDtypeStruct((b, RWKV_HEADS, HEAD_DIM, HEAD_DIM), F32),
        ],
        scratch_shapes=[
            pltpu.VMEM((1, SHIFT_WIDTH), F32),
            pltpu.VMEM((RWKV_HEADS, HEAD_DIM, HEAD_DIM), F32),
            scr(), scr(), scr(), scr(), scr(), scr(), scr(), scr(), scr(),
        ],
        compiler_params=_params(("parallel", "arbitrary")),
        name="rwkv",
    )(zr, shift_prev, s_prev, p["mu"], p["w0"], p["wd"], p["a0"], p["wa"], p["wg"], p["kk"], p["ka"],
      p["rk"], p["lng"], p["lnb"])


def _gmlp_body(z_ref, lng_ref, lnb_ref, ws_ref, bs_ref, o_ref, v_ref):
    z = z_ref[...]
    ge = 0.5 * z * (1.0 + lax.erf(z * (1.0 / math.sqrt(2.0))))
    u = ge[:, 0:GMLP_WIDTH]
    v = ge[:, GMLP_WIDTH:2 * GMLP_WIDTH]
    mean = jnp.mean(v, axis=-1, keepdims=True)
    vc = v - mean
    var = jnp.mean(vc * vc, axis=-1, keepdims=True)
    vn = vc * lax.rsqrt(var + LN_EPS) * lng_ref[...] + lnb_ref[...]
    v_ref[...] = vn
    gd = GMLP_WIDTH // GMLP_GROUPS
    for g in range(GMLP_GROUPS):
        gs = slice(g * gd, (g + 1) * gd)
        sv = jnp.dot(ws_ref[g], vn[:, gs].astype(BF16), preferred_element_type=F32) + bs_ref[:, g:g + 1]
        o_ref[:, gs] = u[:, gs] * sv


def _gmlp(zg, lng, lnb, ws_bf16, bs):
    n = zg.shape[0]
    return pl.pallas_call(
        _gmlp_body,
        grid=(n // CHUNK,),
        in_specs=[
            pl.BlockSpec((CHUNK, 2 * GMLP_WIDTH), lambda i: (i, 0)),
            pl.BlockSpec((1, GMLP_WIDTH), lambda i: (0, 0)),
            pl.BlockSpec((1, GMLP_WIDTH), lambda i: (0, 0)),
            pl.BlockSpec((GMLP_GROUPS, CHUNK, CHUNK), lambda i: (0, 0, 0)),
            pl.BlockSpec((CHUNK, GMLP_GROUPS), lambda i: (0, 0)),
        ],
        out_specs=[
            pl.BlockSpec((CHUNK, GMLP_WIDTH), lambda i: (i, 0)),
            pl.BlockSpec((CHUNK, GMLP_WIDTH), lambda i: (i, 0)),
        ],
        out_shape=[
            jax.ShapeDtypeStruct((n, GMLP_WIDTH), F32),
            jax.ShapeDtypeStruct((n, GMLP_WIDTH), F32),
        ],
        compiler_params=_params(("parallel",)),
        name="gmlp",
    )(zg, lng, lnb, ws_bf16, bs)


def _mem_attn_body(q_ref, k_ref, v_ref, o_ref):
    q = q_ref[0]
    mk = k_ref[0]
    mv = v_ref[0]
    hd = MEM_WIDTH // MEM_HEADS
    for h in range(MEM_HEADS):
        hs = slice(h * hd, (h + 1) * hd)
        s = _bdot(q[:, hs], mk[:, hs], NT) * (hd ** -0.5)
        s = s - jnp.max(s, axis=-1, keepdims=True)
        pr = jnp.exp(s)
        pr = pr / jnp.sum(pr, axis=-1, keepdims=True)
        o_ref[0, :, hs] = _bdot(pr, mv[:, hs])


def _mem_attn(q, mk, mv, tt):
    b, t, _ = q.shape
    return pl.pallas_call(
        _mem_attn_body,
        grid=(b, t // tt),
        in_specs=[
            pl.BlockSpec((1, tt, MEM_WIDTH), lambda bi, ti: (bi, ti, 0)),
            pl.BlockSpec((1, N_MEM, MEM_WIDTH), lambda bi, ti: (bi, 0, 0)),
            pl.BlockSpec((1, N_MEM, MEM_WIDTH), lambda bi, ti: (bi, 0, 0)),
        ],
        out_specs=pl.BlockSpec((1, tt, MEM_WIDTH), lambda bi, ti: (bi, ti, 0)),
        out_shape=jax.ShapeDtypeStruct((b, t, MEM_WIDTH), F32),
        compiler_params=_params(("parallel", "parallel")),
        name="mem_attn",
    )(q, mk, mv)


def _merge_body(x_ref, orw_ref, ogm_ref, ome_ref, sg_ref, wbr_ref, wbg_ref, wbm_ref, wout_ref, gffn_ref,
                wrt_ref, brt_ref, x2_ref, h2_ref, lg_ref):
    d = D_MODEL
    merged = sg_ref[:, 0:d].astype(F32) * _bdot(orw_ref[...], wbr_ref[...])
    merged = merged + sg_ref[:, d:2 * d].astype(F32) * _bdot(ogm_ref[...], wbg_ref[...])
    merged = merged + sg_ref[:, 2 * d:3 * d].astype(F32) * _bdot(ome_ref[...], wbm_ref[...])
    x2 = x_ref[...] + _bdot(merged, wout_ref[...])
    x2_ref[...] = x2
    h2 = _rms(x2, gffn_ref[...])
    h2_ref[...] = h2
    lg_ref[...] = _dot3(h2, wrt_ref[...]) + brt_ref[...]


def _merge(x, orw, ogm, ome, sg, wbr, wbg, wbm, wout, gffn, wrt_pad, brt_pad, tm=256):
    n = x.shape[0]
    row = lambda i: (i, 0)
    const = lambda i: (0, 0)
    d = D_MODEL
    return pl.pallas_call(
        _merge_body,
        grid=(n // tm,),
        in_specs=[
            pl.BlockSpec((tm, d), row),
            pl.BlockSpec((tm, RWKV_WIDTH), row),
            pl.BlockSpec((tm, GMLP_WIDTH), row),
            pl.BlockSpec((tm, MEM_WIDTH), row),
            pl.BlockSpec((tm, 3 * d), row),
            pl.BlockSpec((RWKV_WIDTH, d), const),
            pl.BlockSpec((GMLP_WIDTH, d), const),
            pl.BlockSpec((MEM_WIDTH, d), const),
            pl.BlockSpec((d, d), const),
            pl.BlockSpec((1, d), const),
            pl.BlockSpec((d, LANES), const),
            pl.BlockSpec((1, LANES), const),
        ],
        out_specs=[
            pl.BlockSpec((tm, d), row),
            pl.BlockSpec((tm, d), row),
            pl.BlockSpec((tm, LANES), row),
        ],
        out_shape=[
            jax.ShapeDtypeStruct((n, d), F32),
            jax.ShapeDtypeStruct((n, d), F32),
            jax.ShapeDtypeStruct((n, LANES), F32),
        ],
        compiler_params=_params(("parallel",)),
        name="merge",
    )(x, orw, ogm, ome, sg, wbr, wbg, wbm, wout, gffn, wrt_pad, brt_pad)


MOE_BM = 128


def _moe_body(src_ref, blk_e_ref, nused_ref, h_hbm, w1_ref, b1_ref, w2_ref, b2_ref, o_ref, buf, sem):
    j = pl.program_id(0)
    n_used = nused_ref[0]
    bm = MOE_BM

    def row_copy(tok, slot, i):
        return pltpu.make_async_copy(h_hbm.at[pl.ds(tok, 1)], buf.at[slot, pl.ds(i, 1)], sem.at[slot])

    def issue(blk, slot):
        base = blk * bm

        def body(i, carry):
            row_copy(src_ref[base + i], slot, i).start()
            return carry

        lax.fori_loop(0, bm, body, 0, unroll=8)

    slot = lax.rem(j, 2)

    @pl.when(jnp.logical_and(j == 0, n_used > 0))
    def _():
        issue(0, 0)

    @pl.when(j + 1 < n_used)
    def _():
        issue(j + 1, 1 - slot)

    @pl.when(j < n_used)
    def _():
        pltpu.make_async_copy(h_hbm.at[pl.ds(0, bm)], buf.at[slot], sem.at[slot]).wait()
        x = buf[slot].astype(BF16)
        z = jnp.dot(x, w1_ref[0], preferred_element_type=F32) + b1_ref[0]
        zg = jnp.minimum(z[:, 0:D_FF], SWIGLU_LIMIT)
        zl = jnp.clip(z[:, D_FF:2 * D_FF], -SWIGLU_LIMIT, SWIGLU_LIMIT)
        act = zg * jax.nn.sigmoid(SWIGLU_ALPHA * zg) * (zl + 1.0)
        o_ref[...] = jnp.dot(act.astype(BF16), w2_ref[0], preferred_element_type=F32) + b2_ref[0]

    @pl.when(j >= n_used)
    def _():
        o_ref[...] = jnp.zeros_like(o_ref)


def _moe(src, blk_e, n_used, h, w1_bf16, b1, w2_bf16, b2, n_blocks):
    bm = MOE_BM
    d = D_MODEL
    grid_spec = pltpu.PrefetchScalarGridSpec(
        num_scalar_prefetch=3,
        grid=(n_blocks,),
        in_specs=[
            pl.BlockSpec(memory_space=pl.ANY),
            pl.BlockSpec((1, d, 2 * D_FF), lambda j, src, be, nu: (be[j], 0, 0)),
            pl.BlockSpec((1, 1, 2 * D_FF), lambda j, src, be, nu: (be[j], 0, 0)),
            pl.BlockSpec((1, D_FF, d), lambda j, src, be, nu: (be[j], 0, 0)),
            pl.BlockSpec((1, 1, d), lambda j, src, be, nu: (be[j], 0, 0)),
        ],
        out_specs=pl.BlockSpec((bm, d), lambda j, src, be, nu: (j, 0)),
        scratch_shapes=[
            pltpu.VMEM((2, bm, d), F32),
            pltpu.SemaphoreType.DMA((2,)),
        ],
    )
    return pl.pallas_call(
        _moe_body,
        grid_spec=grid_spec,
        out_shape=jax.ShapeDtypeStruct((n_blocks * bm, d), F32),
        compiler_params=_params(("arbitrary",)),
        name="moe_ffn",
    )(src, blk_e, n_used, h, w1_bf16, b1, w2_bf16, b2)


COMBINE_TM = 128


def _combine_body(dest_ref, x2_ref, gate_ref, gfin_ref, yb_hbm, o_ref, buf, sem):
    i = pl.program_id(0)
    nb = pl.num_programs(0)
    tm = COMBINE_TM

    def issue(blk, slot):
        base = blk * (tm * TOP_K)

        def body(r, carry):
            for kk in range(TOP_K):
                d = dest_ref[base + r * TOP_K + kk]
                pltpu.make_async_copy(yb_hbm.at[pl.ds(d, 1)], buf.at[slot, kk, pl.ds(r, 1)], sem.at[slot]).start()
            return carry

        lax.fori_loop(0, tm, body, 0, unroll=2)

    slot = lax.rem(i, 2)

    @pl.when(i == 0)
    def _():
        issue(0, 0)

    @pl.when(i + 1 < nb)
    def _():
        issue(i + 1, 1 - slot)

    acc = x2_ref[...]
    for kk in range(TOP_K):
        pltpu.make_async_copy(yb_hbm.at[pl.ds(0, tm)], buf.at[slot, kk], sem.at[slot]).wait()
    for kk in range(TOP_K):
        acc = acc + gate_ref[:, kk:kk + 1] * buf[slot, kk]
    o_ref[...] = _rms(acc, gfin_ref[...])


def _combine(dest, x2, gate_pad, gfin, yb):
    n, d = x2.shape
    tm = COMBINE_TM
    grid_spec = pltpu.PrefetchScalarGridSpec(
        num_scalar_prefetch=1,
        grid=(n // tm,),
        in_specs=[
            pl.BlockSpec((tm, d), lambda i, dest: (i, 0)),
            pl.BlockSpec((tm, LANES), lambda i, dest: (i, 0)),
            pl.BlockSpec((1, d), lambda i, dest: (0, 0)),
            pl.BlockSpec(memory_space=pl.ANY),
        ],
        out_specs=pl.BlockSpec((tm, d), lambda i, dest: (i, 0)),
        scratch_shapes=[
            pltpu.VMEM((2, TOP_K, tm, d), F32),
            pltpu.SemaphoreType.DMA((2,)),
        ],
    )
    return pl.pallas_call(
        _combine_body,
        grid_spec=grid_spec,
        out_shape=jax.ShapeDtypeStruct((n, d), F32),
        compiler_params=_params(("arbitrary",)),
        name="moe_combine",
    )(dest, x2, gate_pad, gfin, yb)


def _routing(logits, n_blocks):
    t = logits.shape[0]
    bm = MOE_BM
    top_v, top_i = lax.top_k(logits, TOP_K)
    gate = jax.nn.softmax(top_v, axis=-1)
    n = t * TOP_K
    flat_e = top_i.reshape(-1)
    order = jnp.argsort(flat_e)
    e_sorted = flat_e[order]
    counts = jnp.sum(flat_e[:, None] == jnp.arange(N_EXPERTS)[None, :], axis=0).astype(jnp.int32)
    padded = (counts + bm - 1) // bm * bm
    pad_end = jnp.cumsum(padded)
    start = jnp.cumsum(counts) - counts
    dest_sorted = ((pad_end - padded)[e_sorted] + jnp.arange(n, dtype=jnp.int32) - start[e_sorted]).astype(jnp.int32)
    src = jnp.zeros((n_blocks * bm,), jnp.int32).at[dest_sorted].set((order // TOP_K).astype(jnp.int32))
    dest = jnp.zeros((n,), jnp.int32).at[order].set(dest_sorted)
    blk_e = jnp.minimum(jnp.searchsorted(pad_end, jnp.arange(n_blocks) * bm, side="right"), N_EXPERTS - 1)
    n_used = (pad_end[-1] // bm).astype(jnp.int32).reshape(1)
    return src, blk_e.astype(jnp.int32), n_used, dest, gate


def kernel(x_prompt, x_sample, mem_prompt, state_shift, state_wkv, cache_mem_k, cache_mem_v, g_norm_mix, w_in, mu_shift, w0, w_decay_up, a0, w_a_up, w_g_up, k_k, k_a, r_k, ln_x_g, ln_x_b, gmlp_ln_g, gmlp_ln_b, w_spatial, b_spatial, g_norm_mem, w_mem_kv, w_br_rwkv, w_br_gmlp, w_br_mem, w_out, g_norm_ffn, w_router, b_router, w_exp1, b_exp1, w_exp2, b_exp2, g_norm_final):
    bp, tp, d = x_prompt.shape
    bs, ts, _ = x_sample.shape
    n_p, n_s = bp * tp, bs * ts
    n_all = n_p + n_s
    l = 0
    row = lambda a: a.reshape(1, -1)

    x_all = jnp.concatenate([x_prompt.reshape(n_p, d), x_sample.reshape(n_s, d)], axis=0)
    zr, zg, zq, sg = _in_proj(x_all, row(g_norm_mix[l]), w_in[l].astype(BF16))

    kv_p = _mem_kv(mem_prompt.reshape(bp * N_MEM, d), row(g_norm_mem[l]), w_mem_kv[l].astype(BF16))
    mk_p = kv_p[:, :MEM_WIDTH].reshape(bp, N_MEM, MEM_WIDTH)
    mv_p = kv_p[:, MEM_WIDTH:].reshape(bp, N_MEM, MEM_WIDTH)

    rp = dict(mu=row(mu_shift[l]), w0=row(w0[l]), wd=w_decay_up[l].astype(BF16), a0=row(a0[l]),
              wa=w_a_up[l].astype(BF16), wg=w_g_up[l].astype(BF16), kk=row(k_k[l]), ka=row(k_a[l]),
              rk=row(r_k[l]), lng=row(ln_x_g[l]), lnb=row(ln_x_b[l]))
    zr_p = zr[:n_p].reshape(bp, tp, SHIFT_WIDTH)
    zr_s = zr[n_p:].reshape(bs, ts, SHIFT_WIDTH)
    o_rw_p, s_p = _rwkv(zr_p, jnp.zeros((bp, 1, SHIFT_WIDTH), F32),
                        jnp.zeros((bp, RWKV_HEADS, HEAD_DIM, HEAD_DIM), F32), rp, tt=512, c=64)
    o_rw_s, s_s = _rwkv(zr_s, state_shift[l].reshape(bs, 1, SHIFT_WIDTH), state_wkv[l], rp, tt=ts, c=ts)
    o_rw = jnp.concatenate([o_rw_p.reshape(n_p, RWKV_WIDTH), o_rw_s.reshape(n_s, RWKV_WIDTH)], axis=0)

    tri = jnp.tril(jnp.ones((CHUNK, CHUNK), bool))
    ws_p = jnp.where(tri, w_spatial[l], 0.0).astype(BF16)
    bs_p = b_spatial[l].T
    reps = CHUNK // ts
    tri_s = jnp.tril(jnp.ones((ts, ts), bool))
    ws_small = jnp.where(tri_s, w_spatial[l][:, :ts, :ts], 0.0)
    eye = jnp.eye(reps, dtype=F32)
    ws_s = jnp.einsum("ab,gij->gaibj", eye, ws_small).reshape(GMLP_GROUPS, CHUNK, CHUNK).astype(BF16)
    bs_s = jnp.tile(b_spatial[l][:, :ts], (1, reps)).T
    lng, lnb = row(gmlp_ln_g[l]), row(gmlp_ln_b[l])
    o_gm_p, _ = _gmlp(zg[:n_p], lng, lnb, ws_p, bs_p)
    o_gm_s, v_rows_s = _gmlp(zg[n_p:], lng, lnb, ws_s, bs_s)
    o_gm = jnp.concatenate([o_gm_p, o_gm_s], axis=0)

    o_me_p = _mem_attn(zq[:n_p].reshape(bp, tp, MEM_WIDTH), mk_p, mv_p, tt=512)
    o_me_s = _mem_attn(zq[n_p:].reshape(bs, ts, MEM_WIDTH), cache_mem_k[l].reshape(bs, N_MEM, MEM_WIDTH),
                       cache_mem_v[l].reshape(bs, N_MEM, MEM_WIDTH), tt=ts)
    o_me = jnp.concatenate([o_me_p.reshape(n_p, MEM_WIDTH), o_me_s.reshape(n_s, MEM_WIDTH)], axis=0)

    wrt_pad = jnp.zeros((d, LANES), F32).at[:, :N_EXPERTS].set(w_router[l])
    brt_pad = jnp.zeros((1, LANES), F32).at[0, :N_EXPERTS].set(b_router[l])
    x2, h2, logits_pad = _merge(x_all, o_rw, o_gm, o_me, sg, w_br_rwkv[l].astype(BF16),
                                w_br_gmlp[l].astype(BF16), w_br_mem[l].astype(BF16), w_out[l].astype(BF16),
                                row(g_norm_ffn[l]), wrt_pad, brt_pad)

    n_blocks = (n_all * TOP_K) // MOE_BM + N_EXPERTS
    src, blk_e, n_used, dest, gate = _routing(logits_pad[:, :N_EXPERTS], n_blocks)
    yb = _moe(src, blk_e, n_used, h2, w_exp1[l].astype(BF16), b_exp1[l].reshape(N_EXPERTS, 1, 2 * D_FF),
              w_exp2[l].astype(BF16), b_exp2[l].reshape(N_EXPERTS, 1, d), n_blocks)
    gate_pad = jnp.zeros((n_all, LANES), F32).at[:, :TOP_K].set(gate)
    y_all = _combine(dest, x2, gate_pad, row(g_norm_final), yb)

    y_prompt = y_all[:n_p].reshape(bp, tp, d)
    y_sample = y_all[n_p:].reshape(bs, ts, d)
    shift_p = zr_p[:, tp - 1, :][None]
    shift_s = zr_s[:, ts - 1, :][None]
    mk_out = mk_p.reshape(1, bp, N_MEM, MEM_HEADS, MEM_WIDTH // MEM_HEADS)
    mv_out = mv_p.reshape(1, bp, N_MEM, MEM_HEADS, MEM_WIDTH // MEM_HEADS)
    return (y_prompt, y_sample, shift_p, s_p[None], mk_out, mv_out, shift_s, s_s[None],
            v_rows_s.reshape(1, bs, ts, GMLP_WIDTH))
```

```python
import functools
import math

import jax
import jax.numpy as jnp
from jax import lax
from jax.experimental import pallas as pl
from jax.experimental.pallas import tpu as pltpu

F32 = jnp.float32
BF16 = jnp.bfloat16

D_MODEL = 1024
RWKV_HEADS = 8
HEAD_DIM = 64
RWKV_WIDTH = RWKV_HEADS * HEAD_DIM
DECAY_LORA = 64
A_LORA = 64
GATE_LORA = 128
GMLP_GROUPS = 4
GMLP_WIDTH = 256
CHUNK = 128
MEM_HEADS = 4
MEM_WIDTH = 256
N_MEM = 256
N_EXPERTS = 32
TOP_K = 4
D_FF = 1024
SWIGLU_ALPHA = 1.702
SWIGLU_LIMIT = 7.0
RMS_EPS = 1e-5
LN_EPS = 1e-5
GN_EPS = 64e-5
SHIFT_WIDTH = 3 * RWKV_WIDTH + DECAY_LORA + A_LORA + GATE_LORA
OFF_GMLP = SHIFT_WIDTH
OFF_QMEM = OFF_GMLP + 2 * GMLP_WIDTH
OFF_GATE = OFF_QMEM + MEM_WIDTH
IN_WIDTH = OFF_GATE + 3 * D_MODEL
LANES = 128

NN = ((1,), (0,))
NT = ((1,), (1,))
TN = ((0,), (0,))

VMEM_LIMIT = 56 * 1024 * 1024


def _params(sem, vmem=VMEM_LIMIT):
    return pltpu.CompilerParams(dimension_semantics=sem, vmem_limit_bytes=vmem)


def _bdot(a, b, dims=NN):
    return lax.dot_general(a.astype(BF16), b.astype(BF16), (dims, ((), ())), preferred_element_type=F32)


def _split(x):
    hi = x.astype(BF16)
    lo = (x - hi.astype(F32)).astype(BF16)
    return hi, lo


def _dot3(a, b, dims=NN):
    dn = (dims, ((), ()))
    ah, al = _split(a)
    bh, bl = _split(b)
    r = lax.dot_general(ah, bh, dn, preferred_element_type=F32)
    r = r + lax.dot_general(al, bh, dn, preferred_element_type=F32)
    return r + lax.dot_general(ah, bl, dn, preferred_element_type=F32)


def _rms(x, g):
    return x * lax.rsqrt(jnp.mean(x * x, axis=-1, keepdims=True) + RMS_EPS) * g


def _in_proj_body(x_ref, g_ref, w_ref, zr_ref, zg_ref, zq_ref, sg_ref):
    h = _rms(x_ref[...], g_ref[...]).astype(BF16)
    zr_ref[...] = jnp.dot(h, w_ref[:, 0:OFF_GMLP], preferred_element_type=F32)
    zg_ref[...] = jnp.dot(h, w_ref[:, OFF_GMLP:OFF_QMEM], preferred_element_type=F32)
    zq_ref[...] = jnp.dot(h, w_ref[:, OFF_QMEM:OFF_GATE], preferred_element_type=F32)
    gates = jnp.dot(h, w_ref[:, OFF_GATE:IN_WIDTH], preferred_element_type=F32)
    sg_ref[...] = jax.nn.sigmoid(gates).astype(BF16)


def _in_proj(x, g, w_bf16, tm=256):
    n = x.shape[0]
    row = lambda i: (i, 0)
    const = lambda i: (0, 0)
    return pl.pallas_call(
        _in_proj_body,
        grid=(n // tm,),
        in_specs=[
            pl.BlockSpec((tm, D_MODEL), row),
            pl.BlockSpec((1, D_MODEL), const),
            pl.BlockSpec((D_MODEL, IN_WIDTH), const, pipeline_mode=pl.Buffered(1)),
        ],
        out_specs=[
            pl.BlockSpec((tm, SHIFT_WIDTH), row),
            pl.BlockSpec((tm, 2 * GMLP_WIDTH), row),
            pl.BlockSpec((tm, MEM_WIDTH), row),
            pl.BlockSpec((tm, 3 * D_MODEL), row),
        ],
        out_shape=[
            jax.ShapeDtypeStruct((n, SHIFT_WIDTH), F32),
            jax.ShapeDtypeStruct((n, 2 * GMLP_WIDTH), F32),
            jax.ShapeDtypeStruct((n, MEM_WIDTH), F32),
            jax.ShapeDtypeStruct((n, 3 * D_MODEL), BF16),
        ],
        compiler_params=_params(("parallel",)),
        name="in_proj",
    )(x, g, w_bf16)


def _mem_kv_body(x_ref, g_ref, w_ref, o_ref):
    h = _rms(x_ref[...], g_ref[...]).astype(BF16)
    o_ref[...] = jnp.dot(h, w_ref[...], preferred_element_type=F32)


def _mem_kv(mem, g, w_bf16, tm=256):
    n = mem.shape[0]
    return pl.pallas_call(
        _mem_kv_body,
        grid=(n // tm,),
        in_specs=[
            pl.BlockSpec((tm, D_MODEL), lambda i: (i, 0)),
            pl.BlockSpec((1, D_MODEL), lambda i: (0, 0)),
            pl.BlockSpec((D_MODEL, 2 * MEM_WIDTH), lambda i: (0, 0)),
        ],
        out_specs=pl.BlockSpec((tm, 2 * MEM_WIDTH), lambda i: (i, 0)),
        out_shape=jax.ShapeDtypeStruct((n, 2 * MEM_WIDTH), F32),
        compiler_params=_params(("parallel",)),
        name="mem_kv",
    )(mem, g, w_bf16)


EXP_M05 = math.exp(-0.5)


def _dotp(a, b, dims, passes):
    return _dot3(a, b, dims) if passes == 3 else _bdot(a, b, dims)


PREC = dict(a=1, t=1, lkv=1, tw=1, am=1, gh=3, y=1, s=3)


def _unit_lower_inverse(lows, c):
    rows = lax.broadcasted_iota(jnp.int32, (c, c), 0)
    cols = lax.broadcasted_iota(jnp.int32, (c, c), 1)
    eye = (rows == cols).astype(F32)
    invs = [eye - low for low in lows]
    powers = lows
    for _ in range(int(math.log2(c)) - 1):
        powers = [_dotp(pw, pw, NN, PREC["t"]) for pw in powers]
        invs = [inv + _dotp(inv, pw, NN, PREC["t"]) for inv, pw in zip(invs, powers)]
    return invs


def _rwkv_body(zr_ref, shift_ref, s0_ref, mu_ref, w0_ref, wd_ref, a0_ref, wa_ref, wg_ref, kk_ref, ka_ref,
               rk_ref, lng_ref, lnb_ref,
               o_ref, sout_ref,
               carry_ref, state_ref, r_s, k_s, v_s, kk_s, b_s, ld_s, y_s, bon_s, g_s,
               rw_s, y0_s, gm_s, h0_s, *, tt, c):
    i = pl.program_id(1)
    rw = RWKV_WIDTH

    @pl.when(i == 0)
    def _():
        carry_ref[...] = shift_ref[0]
        state_ref[...] = s0_ref[0]

    z = zr_ref[0]
    z_prev = pltpu.roll(z, 1, 0)
    row = lax.broadcasted_iota(jnp.int32, z.shape, 0)
    z_prev = jnp.where(row == 0, carry_ref[...], z_prev)
    carry_ref[...] = z[tt - 1:tt, :]
    zs = z + mu_ref[...] * (z_prev - z)
    r = zs[:, 0:rw]
    k = zs[:, rw:2 * rw]
    v = zs[:, 2 * rw:3 * rw]
    zw = zs[:, 3 * rw:3 * rw + DECAY_LORA]
    za = zs[:, 3 * rw + DECAY_LORA:3 * rw + DECAY_LORA + A_LORA]
    zg = zs[:, 3 * rw + DECAY_LORA + A_LORA:SHIFT_WIDTH]
    xw = w0_ref[...] + _bdot(jnp.tanh(zw), wd_ref[...])
    ld_s[...] = -EXP_M05 * jax.nn.sigmoid(xw)
    a = jax.nn.sigmoid(a0_ref[...] + _bdot(za, wa_ref[...]))
    g_s[...] = _bdot(jax.nn.sigmoid(zg), wg_ref[...])
    kk = k * kk_ref[...]
    k = k * (1.0 + (a - 1.0) * ka_ref[...])
    r_s[...] = r
    k_s[...] = k
    v_s[...] = v
    rkk = r * k * rk_ref[...]
    for h in range(RWKV_HEADS):
        hs = slice(h * HEAD_DIM, (h + 1) * HEAD_DIM)
        kkh = kk[:, hs]
        nrm = jnp.sqrt(jnp.sum(kkh * kkh, axis=-1, keepdims=True))
        kkh = kkh / jnp.maximum(nrm, 1e-12)
        kk_s[:, hs] = kkh
        b_s[:, hs] = kkh * a[:, hs]
        bon_s[:, hs] = jnp.sum(rkk[:, hs], axis=-1, keepdims=True) * v[:, hs]

    rows_c = lax.broadcasted_iota(jnp.int32, (c, c), 0)
    cols_c = lax.broadcasted_iota(jnp.int32, (c, c), 1)
    strict = rows_c > cols_c
    incl = rows_c >= cols_c
    tril_ones = incl.astype(BF16)
    rows_2c = lax.broadcasted_iota(jnp.int32, (c, 2 * c), 0)
    cols_2c = lax.broadcasted_iota(jnp.int32, (c, 2 * c), 1)
    incl2 = rows_2c >= jnp.where(cols_2c >= c, cols_2c - c, cols_2c)
    rows_k =lax.broadcasted_iota(jnp.int32, (HEAD_DIM, HEAD_DIM), 0)
    cols_k = lax.broadcasted_iota(jnp.int32, (HEAD_DIM, HEAD_DIM), 1)
    eye_k = (rows_k == cols_k).astype(F32)
    zeros_cv = jnp.zeros((c, HEAD_DIM), F32)
    heads = [slice(h * HEAD_DIM, (h + 1) * HEAD_DIM) for h in range(RWKV_HEADS)]

    def phase1(ci, carry):
        rows = pl.ds(pl.multiple_of(ci * c, c), c)
        ld = ld_s[rows, :]
        ld_hi, ld_lo = _split(ld)
        cum = (jnp.dot(tril_ones, ld_hi, preferred_element_type=F32)
               + jnp.dot(tril_ones, ld_lo, preferred_element_type=F32))
        e_inc = jnp.exp(cum)
        e_exc = jnp.exp(cum - ld)
        e_neg = jnp.exp(-cum)
        rt = r_s[rows, :] * e_inc
        kkt = kk_s[rows, :] * e_exc
        kt = k_s[rows, :] * e_neg
        bt = b_s[rows, :] * e_neg
        g_end = e_inc[c - 1:c, :]
        bc = bt * g_end
        kc = kt * g_end
        vv = v_s[rows, :]
        x = [jnp.concatenate([kkt[:, s], rt[:, s]], axis=0) for s in heads]
        zz = [jnp.concatenate([bt[:, s], kt[:, s]], axis=0) for s in heads]
        amat = [_dotp(xh, zh, NT, PREC["a"]) for xh, zh in zip(x, zz)]
        l_b = [jnp.where(strict, am[0:c, 0:c], 0.0) for am in amat]
        l_k = [jnp.where(strict, am[0:c, c:2 * c], 0.0) for am in amat]
        a_r = [jnp.where(incl2, am[c:2 * c, :], 0.0) for am in amat]
        tinv = _unit_lower_inverse(l_b, c)
        lkv = [_dotp(lk, vv[:, s], NN, PREC["lkv"]) for lk, s in zip(l_k, heads)]
        wu = [-_dotp(t, jnp.concatenate([kkt[:, s], lv], axis=1), NN, PREC["tw"])
              for t, lv, s in zip(tinv, lkv, heads)]
        m = [jnp.concatenate([w, jnp.concatenate([zeros_cv, vv[:, s]], axis=1)], axis=0)
             for w, s in zip(wu, heads)]
        am2 = [_dotp(ar, mh, NN, PREC["am"]) for ar, mh in zip(a_r, m)]
        gh = [_dotp(mh, jnp.concatenate([bc[:, s], kc[:, s]], axis=0), TN, PREC["gh"])
              for mh, s in zip(m, heads)]
        for h, s in enumerate(heads):
            idx = ci * RWKV_HEADS + h
            rw_s[idx] = rt[:, s] + am2[h][:, 0:HEAD_DIM]
            y0_s[idx] = am2[h][:, HEAD_DIM:2 * HEAD_DIM]
            gm_s[idx] = gh[h][0:HEAD_DIM, :] + eye_k * g_end[:, s]
            h0_s[idx] = gh[h][HEAD_DIM:2 * HEAD_DIM, :]
        return carry

    lax.fori_loop(0, tt // c, phase1, 0)

    def phase2(ci, carry):
        rows = pl.ds(pl.multiple_of(ci * c, c), c)
        for h, s in enumerate(heads):
            idx = ci * RWKV_HEADS + h
            s_h = state_ref[h]
            y = _dotp(rw_s[idx], s_h, NT, PREC["y"]) + y0_s[idx]
            state_ref[h] = _dotp(s_h, gm_s[idx], NN, PREC["s"]) + h0_s[idx]
            mean = jnp.mean(y, axis=-1, keepdims=True)
            yc = y - mean
            var = jnp.mean(yc * yc, axis=-1, keepdims=True)
            y_s[rows, s] = yc * lax.rsqrt(var + GN_EPS)
        return carry

    lax.fori_loop(0, tt // c, phase2, 0)
    o_ref[0] = (y_s[...] * lng_ref[...] + lnb_ref[...] + bon_s[...]) * g_s[...]

    @pl.when(i == pl.num_programs(1) - 1)
    def _():
        sout_ref[0] = state_ref[...]


def _rwkv(zr, shift_prev, s_prev, p, tt, c):
    b, t, _ = zr.shape
    vec = lambda n: pl.BlockSpec((1, n), lambda bi, ti: (0, 0))
    mat = lambda m, n: pl.BlockSpec((m, n), lambda bi, ti: (0, 0))
    rw = RWKV_WIDTH
    scr = lambda: pltpu.VMEM((tt, rw), F32)
    nch = (tt // c) * RWKV_HEADS
    return pl.pallas_call(
        functools.partial(_rwkv_body, tt=tt, c=c),
        grid=(b, t // tt),
        in_specs=[
            pl.BlockSpec((1, tt, SHIFT_WIDTH), lambda bi, ti: (bi, ti, 0)),
            pl.BlockSpec((1, 1, SHIFT_WIDTH), lambda bi, ti: (bi, 0, 0)),
            pl.BlockSpec((1, RWKV_HEADS, HEAD_DIM, HEAD_DIM), lambda bi, ti: (bi, 0, 0, 0)),
            vec(SHIFT_WIDTH), vec(rw), mat(DECAY_LORA, rw), vec(rw), mat(A_LORA, rw), mat(GATE_LORA, rw),
            vec(rw), vec(rw), vec(rw), vec(rw), vec(rw),
        ],
        out_specs=[
            pl.BlockSpec((1, tt, rw), lambda bi, ti: (bi, ti, 0)),
            pl.BlockSpec((1, RWKV_HEADS, HEAD_DIM, HEAD_DIM), lambda bi, ti: (bi, 0, 0, 0)),
        ],
        out_shape=[
            jax.ShapeDtypeStruct((b, t, rw), F32),
            jax.ShapeDtypeStruct((b, RWKV_HEADS, HEAD_DIM, HEAD_DIM), F32),
        ],
        scratch_shapes=[
            pltpu.VMEM((1, SHIFT_WIDTH), F32),
            pltpu.VMEM((RWKV_HEADS, HEAD_DIM, HEAD_DIM), F32),
            scr(), scr(), scr(), scr(), scr(), scr(), scr(), scr(), scr(),
            pltpu.VMEM((nch, c, HEAD_DIM), F32),
            pltpu.VMEM((nch, c, HEAD_DIM), F32),
            pltpu.VMEM((nch, HEAD_DIM, HEAD_DIM), F32),
            pltpu.VMEM((nch, HEAD_DIM, HEAD_DIM), F32),
        ],
        compiler_params=_params(("parallel", "arbitrary")),
        name="rwkv",
    )(zr, shift_prev, s_prev, p["mu"], p["w0"], p["wd"], p["a0"], p["wa"], p["wg"], p["kk"], p["ka"],
      p["rk"], p["lng"], p["lnb"])


def _rwkv_prompt(zr, p):
    b = zr.shape[0]
    return _rwkv(zr, jnp.zeros((b, 1, SHIFT_WIDTH), F32), jnp.zeros((b, RWKV_HEADS, HEAD_DIM, HEAD_DIM), F32),
                 p, tt=512, c=64)


def _rwkv_sample(zr, shift_prev, s_prev, p):
    t = zr.shape[1]
    return _rwkv(zr, shift_prev, s_prev, p, tt=t, c=t)


def _gmlp_body(z_ref, lng_ref, lnb_ref, ws_ref, bs_ref, o_ref, v_ref):
    z = z_ref[...]
    ge = 0.5 * z * (1.0 + lax.erf(z * (1.0 / math.sqrt(2.0))))
    u = ge[:, 0:GMLP_WIDTH]
    v = ge[:, GMLP_WIDTH:2 * GMLP_WIDTH]
    mean = jnp.mean(v, axis=-1, keepdims=True)
    vc = v - mean
    var = jnp.mean(vc * vc, axis=-1, keepdims=True)
    vn = vc * lax.rsqrt(var + LN_EPS) * lng_ref[...] + lnb_ref[...]
    v_ref[...] = vn
    gd = GMLP_WIDTH // GMLP_GROUPS
    for g in range(GMLP_GROUPS):
        gs = slice(g * gd, (g + 1) * gd)
        sv = jnp.dot(ws_ref[g], vn[:, gs].astype(BF16), preferred_element_type=F32) + bs_ref[:, g:g + 1]
        o_ref[:, gs] = u[:, gs] * sv


def _gmlp(zg, lng, lnb, ws_bf16, bs):
    n = zg.shape[0]
    return pl.pallas_call(
        _gmlp_body,
        grid=(n // CHUNK,),
        in_specs=[
            pl.BlockSpec((CHUNK, 2 * GMLP_WIDTH), lambda i: (i, 0)),
            pl.BlockSpec((1, GMLP_WIDTH), lambda i: (0, 0)),
            pl.BlockSpec((1, GMLP_WIDTH), lambda i: (0, 0)),
            pl.BlockSpec((GMLP_GROUPS, CHUNK, CHUNK), lambda i: (0, 0, 0)),
            pl.BlockSpec((CHUNK, GMLP_GROUPS), lambda i: (0, 0)),
        ],
        out_specs=[
            pl.BlockSpec((CHUNK, GMLP_WIDTH), lambda i: (i, 0)),
            pl.BlockSpec((CHUNK, GMLP_WIDTH), lambda i: (i, 0)),
        ],
        out_shape=[
            jax.ShapeDtypeStruct((n, GMLP_WIDTH), F32),
            jax.ShapeDtypeStruct((n, GMLP_WIDTH), F32),
        ],
        compiler_params=_params(("parallel",)),
        name="gmlp",
    )(zg, lng, lnb, ws_bf16, bs)


def _mem_attn_body(q_ref, k_ref, v_ref, o_ref):
    q = q_ref[0]
    mk = k_ref[0]
    mv = v_ref[0]
    hd = MEM_WIDTH // MEM_HEADS
    for h in range(MEM_HEADS):
        hs = slice(h * hd, (h + 1) * hd)
        s = _bdot(q[:, hs], mk[:, hs], NT) * (hd ** -0.5)
        s = s - jnp.max(s, axis=-1, keepdims=True)
        pr = jnp.exp(s)
        pr = pr / jnp.sum(pr, axis=-1, keepdims=True)
        o_ref[0, :, hs] = _bdot(pr, mv[:, hs])


def _mem_attn(q, mk, mv, tt):
    b, t, _ = q.shape
    return pl.pallas_call(
        _mem_attn_body,
        grid=(b, t // tt),
        in_specs=[
            pl.BlockSpec((1, tt, MEM_WIDTH), lambda bi, ti: (bi, ti, 0)),
            pl.BlockSpec((1, N_MEM, MEM_WIDTH), lambda bi, ti: (bi, 0, 0)),
            pl.BlockSpec((1, N_MEM, MEM_WIDTH), lambda bi, ti: (bi, 0, 0)),
        ],
        out_specs=pl.BlockSpec((1, tt, MEM_WIDTH), lambda bi, ti: (bi, ti, 0)),
        out_shape=jax.ShapeDtypeStruct((b, t, MEM_WIDTH), F32),
        compiler_params=_params(("parallel", "parallel")),
        name="mem_attn",
    )(q, mk, mv)


def _merge_body(x_ref, orw_ref, ogm_ref, ome_ref, sg_ref, wbr_ref, wbg_ref, wbm_ref, wout_ref, gffn_ref,
                wrt_ref, brt_ref, x2_ref, h2_ref, lg_ref):
    d = D_MODEL
    merged = sg_ref[:, 0:d].astype(F32) * _bdot(orw_ref[...], wbr_ref[...])
    merged = merged + sg_ref[:, d:2 * d].astype(F32) * _bdot(ogm_ref[...], wbg_ref[...])
    merged = merged + sg_ref[:, 2 * d:3 * d].astype(F32) * _bdot(ome_ref[...], wbm_ref[...])
    x2 = x_ref[...] + _bdot(merged, wout_ref[...])
    x2_ref[...] = x2
    h2 = _rms(x2, gffn_ref[...])
    h2_ref[...] = h2
    lg_ref[...] = _dot3(h2, wrt_ref[...]) + brt_ref[...]


def _merge(x, orw, ogm, ome, sg, wbr, wbg, wbm, wout, gffn, wrt_pad, brt_pad, tm=256):
    n = x.shape[0]
    row = lambda i: (i, 0)
    const = lambda i: (0, 0)
    d = D_MODEL
    return pl.pallas_call(
        _merge_body,
        grid=(n // tm,),
        in_specs=[
            pl.BlockSpec((tm, d), row),
            pl.BlockSpec((tm, RWKV_WIDTH), row),
            pl.BlockSpec((tm, GMLP_WIDTH), row),
            pl.BlockSpec((tm, MEM_WIDTH), row),
            pl.BlockSpec((tm, 3 * d), row),
            pl.BlockSpec((RWKV_WIDTH, d), const),
            pl.BlockSpec((GMLP_WIDTH, d), const),
            pl.BlockSpec((MEM_WIDTH, d), const),
            pl.BlockSpec((d, d), const),
            pl.BlockSpec((1, d), const),
            pl.BlockSpec((d, LANES), const),
            pl.BlockSpec((1, LANES), const),
        ],
        out_specs=[
            pl.BlockSpec((tm, d), row),
            pl.BlockSpec((tm, d), row),
            pl.BlockSpec((tm, LANES), row),
        ],
        out_shape=[
            jax.ShapeDtypeStruct((n, d), F32),
            jax.ShapeDtypeStruct((n, d), F32),
            jax.ShapeDtypeStruct((n, LANES), F32),
        ],
        compiler_params=_params(("parallel",)),
        name="merge",
    )(x, orw, ogm, ome, sg, wbr, wbg, wbm, wout, gffn, wrt_pad, brt_pad)


MOE_BM = 128


def _moe_body(src_ref, blk_e_ref, nused_ref, h_hbm, w1_ref, b1_ref, w2_ref, b2_ref, o_ref, buf, sem):
    j = pl.program_id(0)
    n_used = nused_ref[0]
    bm = MOE_BM

    def row_copy(tok, slot, i):
        return pltpu.make_async_copy(h_hbm.at[pl.ds(tok, 1)], buf.at[slot, pl.ds(i, 1)], sem.at[slot])

    def issue(blk, slot):
        base = blk * bm

        def body(i, carry):
            row_copy(src_ref[base + i], slot, i).start()
            return carry

        lax.fori_loop(0, bm, body, 0, unroll=8)

    slot = lax.rem(j, 2)

    @pl.when(jnp.logical_and(j == 0, n_used > 0))
    def _():
        issue(0, 0)

    @pl.when(j + 1 < n_used)
    def _():
        issue(j + 1, 1 - slot)

    @pl.when(j < n_used)
    def _():
        pltpu.make_async_copy(h_hbm.at[pl.ds(0, bm)], buf.at[slot], sem.at[slot]).wait()
        x = buf[slot].astype(BF16)
        z = jnp.dot(x, w1_ref[0], preferred_element_type=F32) + b1_ref[0]
        zg = jnp.minimum(z[:, 0:D_FF], SWIGLU_LIMIT)
        zl = jnp.clip(z[:, D_FF:2 * D_FF], -SWIGLU_LIMIT, SWIGLU_LIMIT)
        act = zg * jax.nn.sigmoid(SWIGLU_ALPHA * zg) * (zl + 1.0)
        o_ref[...] = jnp.dot(act.astype(BF16), w2_ref[0], preferred_element_type=F32) + b2_ref[0]

    @pl.when(j >= n_used)
    def _():
        o_ref[...] = jnp.zeros_like(o_ref)


def _moe(src, blk_e, n_used, h, w1_bf16, b1, w2_bf16, b2, n_blocks):
    bm = MOE_BM
    d = D_MODEL
    grid_spec = pltpu.PrefetchScalarGridSpec(
        num_scalar_prefetch=3,
        grid=(n_blocks,),
        in_specs=[
            pl.BlockSpec(memory_space=pl.ANY),
            pl.BlockSpec((1, d, 2 * D_FF), lambda j, src, be, nu: (be[j], 0, 0)),
            pl.BlockSpec((1, 1, 2 * D_FF), lambda j, src, be, nu: (be[j], 0, 0)),
            pl.BlockSpec((1, D_FF, d), lambda j, src, be, nu: (be[j], 0, 0)),
            pl.BlockSpec((1, 1, d), lambda j, src, be, nu: (be[j], 0, 0)),
        ],
        out_specs=pl.BlockSpec((bm, d), lambda j, src, be, nu: (j, 0)),
        scratch_shapes=[
            pltpu.VMEM((2, bm, d), F32),
            pltpu.SemaphoreType.DMA((2,)),
        ],
    )
    return pl.pallas_call(
        _moe_body,
        grid_spec=grid_spec,
        out_shape=jax.ShapeDtypeStruct((n_blocks * bm, d), F32),
        compiler_params=_params(("arbitrary",)),
        name="moe_ffn",
    )(src, blk_e, n_used, h, w1_bf16, b1, w2_bf16, b2)


COMBINE_TM = 128


def _combine_body(dest_ref, x2_ref, gate_ref, gfin_ref, yb_hbm, o_ref, buf, sem):
    i = pl.program_id(0)
    nb = pl.num_programs(0)
    tm = COMBINE_TM

    def issue(blk, slot):
        base = blk * (tm * TOP_K)

        def body(r, carry):
            for kk in range(TOP_K):
                d = dest_ref[base + r * TOP_K + kk]
                pltpu.make_async_copy(yb_hbm.at[pl.ds(d, 1)], buf.at[slot, kk, pl.ds(r, 1)], sem.at[slot]).start()
            return carry

        lax.fori_loop(0, tm, body, 0, unroll=2)

    slot = lax.rem(i, 2)

    @pl.when(i == 0)
    def _():
        issue(0, 0)

    @pl.when(i + 1 < nb)
    def _():
        issue(i + 1, 1 - slot)

    acc = x2_ref[...]
    for kk in range(TOP_K):
        pltpu.make_async_copy(yb_hbm.at[pl.ds(0, tm)], buf.at[slot, kk], sem.at[slot]).wait()
    for kk in range(TOP_K):
        acc = acc + gate_ref[:, kk:kk + 1] * buf[slot, kk]
    o_ref[...] = _rms(acc, gfin_ref[...])


def _combine(dest, x2, gate_pad, gfin, yb):
    n, d = x2.shape
    tm = COMBINE_TM
    grid_spec = pltpu.PrefetchScalarGridSpec(
        num_scalar_prefetch=1,
        grid=(n // tm,),
        in_specs=[
            pl.BlockSpec((tm, d), lambda i, dest: (i, 0)),
            pl.BlockSpec((tm, LANES), lambda i, dest: (i, 0)),
            pl.BlockSpec((1, d), lambda i, dest: (0, 0)),
            pl.BlockSpec(memory_space=pl.ANY),
        ],
        out_specs=pl.BlockSpec((tm, d), lambda i, dest: (i, 0)),
        scratch_shapes=[
            pltpu.VMEM((2, TOP_K, tm, d), F32),
            pltpu.SemaphoreType.DMA((2,)),
        ],
    )
    return pl.pallas_call(
        _combine_body,
        grid_spec=grid_spec,
        out_shape=jax.ShapeDtypeStruct((n, d), F32),
        compiler_params=_params(("arbitrary",)),
        name="moe_combine",
    )(dest, x2, gate_pad, gfin, yb)


def _routing(logits, n_blocks):
    t = logits.shape[0]
    bm = MOE_BM
    top_v, top_i = lax.top_k(logits, TOP_K)
    gate = jax.nn.softmax(top_v, axis=-1)
    n = t * TOP_K
    flat_e = top_i.reshape(-1)
    order = jnp.argsort(flat_e)
    e_sorted = flat_e[order]
    counts = jnp.sum(flat_e[:, None] == jnp.arange(N_EXPERTS)[None, :], axis=0).astype(jnp.int32)
    padded = (counts + bm - 1) // bm * bm
    pad_end = jnp.cumsum(padded)
    start = jnp.cumsum(counts) - counts
    dest_sorted = ((pad_end - padded)[e_sorted] + jnp.arange(n, dtype=jnp.int32) - start[e_sorted]).astype(jnp.int32)
    src = jnp.zeros((n_blocks * bm,), jnp.int32).at[dest_sorted].set((order // TOP_K).astype(jnp.int32))
    dest = jnp.zeros((n,), jnp.int32).at[order].set(dest_sorted)
    blk_e = jnp.minimum(jnp.searchsorted(pad_end, jnp.arange(n_blocks) * bm, side="right"), N_EXPERTS - 1)
    n_used = (pad_end[-1] // bm).astype(jnp.int32).reshape(1)
    return src, blk_e.astype(jnp.int32), n_used, dest, gate


def kernel(x_prompt, x_sample, mem_prompt, state_shift, state_wkv, cache_mem_k, cache_mem_v, g_norm_mix, w_in, mu_shift, w0, w_decay_up, a0, w_a_up, w_g_up, k_k, k_a, r_k, ln_x_g, ln_x_b, gmlp_ln_g, gmlp_ln_b, w_spatial, b_spatial, g_norm_mem, w_mem_kv, w_br_rwkv, w_br_gmlp, w_br_mem, w_out, g_norm_ffn, w_router, b_router, w_exp1, b_exp1, w_exp2, b_exp2, g_norm_final):
    bp, tp, d = x_prompt.shape
    bs, ts, _ = x_sample.shape
    n_p, n_s = bp * tp, bs * ts
    n_all = n_p + n_s
    l = 0
    row = lambda a: a.reshape(1, -1)

    x_all = jnp.concatenate([x_prompt.reshape(n_p, d), x_sample.reshape(n_s, d)], axis=0)
    zr, zg, zq, sg = _in_proj(x_all, row(g_norm_mix[l]), w_in[l].astype(BF16))

    kv_p = _mem_kv(mem_prompt.reshape(bp * N_MEM, d), row(g_norm_mem[l]), w_mem_kv[l].astype(BF16))
    mk_p = kv_p[:, :MEM_WIDTH].reshape(bp, N_MEM, MEM_WIDTH)
    mv_p = kv_p[:, MEM_WIDTH:].reshape(bp, N_MEM, MEM_WIDTH)

    rp = dict(mu=row(mu_shift[l]), w0=row(w0[l]), wd=w_decay_up[l].astype(BF16), a0=row(a0[l]),
              wa=w_a_up[l].astype(BF16), wg=w_g_up[l].astype(BF16), kk=row(k_k[l]), ka=row(k_a[l]),
              rk=row(r_k[l]), lng=row(ln_x_g[l]), lnb=row(ln_x_b[l]))
    zr_p = zr[:n_p].reshape(bp, tp, SHIFT_WIDTH)
    zr_s = zr[n_p:].reshape(bs, ts, SHIFT_WIDTH)
    o_rw_p, s_p = _rwkv_prompt(zr_p, rp)
    o_rw_s, s_s = _rwkv_sample(zr_s, state_shift[l].reshape(bs, 1, SHIFT_WIDTH), state_wkv[l], rp)
    o_rw = jnp.concatenate([o_rw_p.reshape(n_p, RWKV_WIDTH), o_rw_s.reshape(n_s, RWKV_WIDTH)], axis=0)

    tri = jnp.tril(jnp.ones((CHUNK, CHUNK), bool))
    ws_p = jnp.where(tri, w_spatial[l], 0.0).astype(BF16)
    bs_p = b_spatial[l].T
    reps = CHUNK // ts
    tri_s = jnp.tril(jnp.ones((ts, ts), bool))
    ws_small = jnp.where(tri_s, w_spatial[l][:, :ts, :ts], 0.0)
    eye = jnp.eye(reps, dtype=F32)
    ws_s = jnp.einsum("ab,gij->gaibj", eye, ws_small).reshape(GMLP_GROUPS, CHUNK, CHUNK).astype(BF16)
    bs_s = jnp.tile(b_spatial[l][:, :ts], (1, reps)).T
    lng, lnb = row(gmlp_ln_g[l]), row(gmlp_ln_b[l])
    o_gm_p, _ = _gmlp(zg[:n_p], lng, lnb, ws_p, bs_p)
    o_gm_s, v_rows_s = _gmlp(zg[n_p:], lng, lnb, ws_s, bs_s)
    o_gm = jnp.concatenate([o_gm_p, o_gm_s], axis=0)

    o_me_p = _mem_attn(zq[:n_p].reshape(bp, tp, MEM_WIDTH), mk_p, mv_p, tt=512)
    o_me_s = _mem_attn(zq[n_p:].reshape(bs, ts, MEM_WIDTH), cache_mem_k[l].reshape(bs, N_MEM, MEM_WIDTH),
                       cache_mem_v[l].reshape(bs, N_MEM, MEM_WIDTH), tt=ts)
    o_me = jnp.concatenate([o_me_p.reshape(n_p, MEM_WIDTH), o_me_s.reshape(n_s, MEM_WIDTH)], axis=0)

    wrt_pad = jnp.zeros((d, LANES), F32).at[:, :N_EXPERTS].set(w_router[l])
    brt_pad = jnp.zeros((1, LANES), F32).at[0, :N_EXPERTS].set(b_router[l])
    x2, h2, logits_pad = _merge(x_all, o_rw, o_gm, o_me, sg, w_br_rwkv[l].astype(BF16),
                                w_br_gmlp[l].astype(BF16), w_br_mem[l].astype(BF16), w_out[l].astype(BF16),
                                row(g_norm_ffn[l]), wrt_pad, brt_pad)

    n_blocks = (n_all * TOP_K) // MOE_BM + N_EXPERTS
    src, blk_e, n_used, dest, gate = _routing(logits_pad[:, :N_EXPERTS], n_blocks)
    yb = _moe(src, blk_e, n_used, h2, w_exp1[l].astype(BF16), b_exp1[l].reshape(N_EXPERTS, 1, 2 * D_FF),
              w_exp2[l].astype(BF16), b_exp2[l].reshape(N_EXPERTS, 1, d), n_blocks)
    gate_pad = jnp.zeros((n_all, LANES), F32).at[:, :TOP_K].set(gate)
    y_all = _combine(dest, x2, gate_pad, row(g_norm_final), yb)

    y_prompt = y_all[:n_p].reshape(bp, tp, d)
    y_sample = y_all[n_p:].reshape(bs, ts, d)
    shift_p = zr_p[:, tp - 1, :][None]
    shift_s = zr_s[:, ts - 1, :][None]
    mk_out = mk_p.reshape(1, bp, N_MEM, MEM_HEADS, MEM_WIDTH // MEM_HEADS)
    mv_out = mv_p.reshape(1, bp, N_MEM, MEM_HEADS, MEM_WIDTH // MEM_HEADS)
    return (y_prompt, y_sample, shift_p, s_p[None], mk_out, mv_out, shift_s, s_s[None],
            v_rows_s.reshape(1, bs, ts, GMLP_WIDTH))
```

```python
import functools
import math

import jax
import jax.numpy as jnp
from jax import lax
from jax.experimental import pallas as pl
from jax.experimental.pallas import tpu as pltpu

F32 = jnp.float32
BF16 = jnp.bfloat16

D_MODEL = 1024
RWKV_HEADS = 8
HEAD_DIM = 64
RWKV_WIDTH = RWKV_HEADS * HEAD_DIM
DECAY_LORA = 64
A_LORA = 64
GATE_LORA = 128
GMLP_GROUPS = 4
GMLP_WIDTH = 256
CHUNK = 128
MEM_HEADS = 4
MEM_WIDTH = 256
N_MEM = 256
N_EXPERTS = 32
TOP_K = 4
D_FF = 1024
SWIGLU_ALPHA = 1.702
SWIGLU_LIMIT = 7.0
RMS_EPS = 1e-5
LN_EPS = 1e-5
GN_EPS = 64e-5
SHIFT_WIDTH = 3 * RWKV_WIDTH + DECAY_LORA + A_LORA + GATE_LORA
OFF_GMLP = SHIFT_WIDTH
OFF_QMEM = OFF_GMLP + 2 * GMLP_WIDTH
OFF_GATE = OFF_QMEM + MEM_WIDTH
IN_WIDTH = OFF_GATE + 3 * D_MODEL
LANES = 128

PROJ_TM = 256
RWKV_TT = 512
RWKV_C = 64
ATTN_TT = 512
MOE_BM = 256
DISPATCH_TM = 256
COMBINE_TM = 128

ROUTE_E = 0
ROUTE_RANK = 4
ROUTE_GATE = 8

NN = ((1,), (0,))
NT = ((1,), (1,))
TN = ((0,), (0,))

VMEM_LIMIT = 56 * 1024 * 1024


def _params(sem, vmem=VMEM_LIMIT):
    return pltpu.CompilerParams(dimension_semantics=sem, vmem_limit_bytes=vmem)


def _bdot(a, b, dims=NN):
    return lax.dot_general(a.astype(BF16), b.astype(BF16), (dims, ((), ())), preferred_element_type=F32)


def _split(x):
    hi = x.astype(BF16)
    lo = (x - hi.astype(F32)).astype(BF16)
    return hi, lo


def _dot3(a, b, dims=NN):
    dn = (dims, ((), ()))
    ah, al = _split(a)
    bh, bl = _split(b)
    r = lax.dot_general(ah, bh, dn, preferred_element_type=F32)
    r = r + lax.dot_general(al, bh, dn, preferred_element_type=F32)
    return r + lax.dot_general(ah, bl, dn, preferred_element_type=F32)


def _rms(x, g):
    return x * lax.rsqrt(jnp.mean(x * x, axis=-1, keepdims=True) + RMS_EPS) * g


def _in_proj_body(x_ref, g_ref, w_ref, zr_ref, zg_ref, zq_ref, sg_ref):
    h = _rms(x_ref[...], g_ref[...]).astype(BF16)
    zr_ref[...] = jnp.dot(h, w_ref[:, 0:OFF_GMLP], preferred_element_type=F32)
    zg_ref[...] = jnp.dot(h, w_ref[:, OFF_GMLP:OFF_QMEM], preferred_element_type=F32)
    zq_ref[...] = jnp.dot(h, w_ref[:, OFF_QMEM:OFF_GATE], preferred_element_type=F32)
    gates = jnp.dot(h, w_ref[:, OFF_GATE:IN_WIDTH], preferred_element_type=F32)
    sg_ref[...] = jax.nn.sigmoid(gates).astype(BF16)


def _in_proj(x, g, w_bf16):
    n = x.shape[0]
    tm = PROJ_TM
    row = lambda i: (i, 0)
    const = lambda i: (0, 0)
    return pl.pallas_call(
        _in_proj_body,
        grid=(n // tm,),
        in_specs=[
            pl.BlockSpec((tm, D_MODEL), row),
            pl.BlockSpec((1, D_MODEL), const),
            pl.BlockSpec((D_MODEL, IN_WIDTH), const, pipeline_mode=pl.Buffered(1)),
        ],
        out_specs=[
            pl.BlockSpec((tm, SHIFT_WIDTH), row),
            pl.BlockSpec((tm, 2 * GMLP_WIDTH), row),
            pl.BlockSpec((tm, MEM_WIDTH), row),
            pl.BlockSpec((tm, 3 * D_MODEL), row),
        ],
        out_shape=[
            jax.ShapeDtypeStruct((n, SHIFT_WIDTH), F32),
            jax.ShapeDtypeStruct((n, 2 * GMLP_WIDTH), F32),
            jax.ShapeDtypeStruct((n, MEM_WIDTH), F32),
            jax.ShapeDtypeStruct((n, 3 * D_MODEL), BF16),
        ],
        compiler_params=_params(("parallel",)),
        name="in_proj",
    )(x, g, w_bf16)


def _mem_kv_body(x_ref, g_ref, w_ref, k_ref, v_ref):
    h = _rms(x_ref[...], g_ref[...]).astype(BF16)
    k_ref[...] = jnp.dot(h, w_ref[:, 0:MEM_WIDTH], preferred_element_type=F32)
    v_ref[...] = jnp.dot(h, w_ref[:, MEM_WIDTH:2 * MEM_WIDTH], preferred_element_type=F32)


def _mem_kv(mem, g, w_bf16):
    n = mem.shape[0]
    tm = PROJ_TM
    return pl.pallas_call(
        _mem_kv_body,
        grid=(n // tm,),
        in_specs=[
            pl.BlockSpec((tm, D_MODEL), lambda i: (i, 0)),
            pl.BlockSpec((1, D_MODEL), lambda i: (0, 0)),
            pl.BlockSpec((D_MODEL, 2 * MEM_WIDTH), lambda i: (0, 0)),
        ],
        out_specs=[pl.BlockSpec((tm, MEM_WIDTH), lambda i: (i, 0))] * 2,
        out_shape=[jax.ShapeDtypeStruct((n, MEM_WIDTH), F32)] * 2,
        compiler_params=_params(("parallel",)),
        name="mem_kv",
    )(mem, g, w_bf16)


EXP_M05 = math.exp(-0.5)


def _dotp(a, b, dims, passes):
    return _dot3(a, b, dims) if passes == 3 else _bdot(a, b, dims)


PREC = dict(a=1, t=1, lkv=1, tw=1, am=1, gh=3, y=1, s=3)


def _unit_lower_inverse(lows, c):
    rows = lax.broadcasted_iota(jnp.int32, (c, c), 0)
    cols = lax.broadcasted_iota(jnp.int32, (c, c), 1)
    eye = (rows == cols).astype(F32)
    invs = [eye - low for low in lows]
    powers = lows
    for _ in range(int(math.log2(c)) - 1):
        powers = [_dotp(pw, pw, NN, PREC["t"]) for pw in powers]
        invs = [inv + _dotp(inv, pw, NN, PREC["t"]) for inv, pw in zip(invs, powers)]
    return invs


def _rwkv_body(zr_ref, shift_ref, s0_ref, mu_ref, w0_ref, wd_ref, a0_ref, wa_ref, wg_ref, kk_ref, ka_ref,
               rk_ref, lng_ref, lnb_ref,
               o_ref, sout_ref,
               carry_ref, state_ref, r_s, k_s, v_s, kk_s, b_s, ld_s, y_s, bon_s, g_s,
               rw_s, y0_s, gm_s, h0_s, *, tt, c):
    i = pl.program_id(1)
    rw = RWKV_WIDTH

    @pl.when(i == 0)
    def _():
        carry_ref[...] = shift_ref[0]
        state_ref[...] = s0_ref[0]

    z = zr_ref[0]
    z_prev = pltpu.roll(z, 1, 0)
    row = lax.broadcasted_iota(jnp.int32, z.shape, 0)
    z_prev = jnp.where(row == 0, carry_ref[...], z_prev)
    carry_ref[...] = z[tt - 1:tt, :]
    zs = z + mu_ref[...] * (z_prev - z)
    r = zs[:, 0:rw]
    k = zs[:, rw:2 * rw]
    v = zs[:, 2 * rw:3 * rw]
    zw = zs[:, 3 * rw:3 * rw + DECAY_LORA]
    za = zs[:, 3 * rw + DECAY_LORA:3 * rw + DECAY_LORA + A_LORA]
    zg = zs[:, 3 * rw + DECAY_LORA + A_LORA:SHIFT_WIDTH]
    xw = w0_ref[...] + _bdot(jnp.tanh(zw), wd_ref[...])
    ld_s[...] = -EXP_M05 * jax.nn.sigmoid(xw)
    a = jax.nn.sigmoid(a0_ref[...] + _bdot(za, wa_ref[...]))
    g_s[...] = _bdot(jax.nn.sigmoid(zg), wg_ref[...])
    kk = k * kk_ref[...]
    k = k * (1.0 + (a - 1.0) * ka_ref[...])
    r_s[...] = r
    k_s[...] = k
    v_s[...] = v
    rkk = r * k * rk_ref[...]
    for h in range(RWKV_HEADS):
        hs = slice(h * HEAD_DIM, (h + 1) * HEAD_DIM)
        kkh = kk[:, hs]
        nrm = jnp.sqrt(jnp.sum(kkh * kkh, axis=-1, keepdims=True))
        kkh = kkh / jnp.maximum(nrm, 1e-12)
        kk_s[:, hs] = kkh
        b_s[:, hs] = kkh * a[:, hs]
        bon_s[:, hs] = jnp.sum(rkk[:, hs], axis=-1, keepdims=True) * v[:, hs]

    rows_c = lax.broadcasted_iota(jnp.int32, (c, c), 0)
    cols_c = lax.broadcasted_iota(jnp.int32, (c, c), 1)
    strict = rows_c > cols_c
    incl = rows_c >= cols_c
    tril_ones = incl.astype(BF16)
    rows_2c = lax.broadcasted_iota(jnp.int32, (c, 2 * c), 0)
    cols_2c = lax.broadcasted_iota(jnp.int32, (c, 2 * c), 1)
    incl2 = rows_2c >= jnp.where(cols_2c >= c, cols_2c - c, cols_2c)
    rows_k = lax.broadcasted_iota(jnp.int32, (HEAD_DIM, HEAD_DIM), 0)
    cols_k = lax.broadcasted_iota(jnp.int32, (HEAD_DIM, HEAD_DIM), 1)
    eye_k = (rows_k == cols_k).astype(F32)
    zeros_cv = jnp.zeros((c, HEAD_DIM), F32)
    heads = [slice(h * HEAD_DIM, (h + 1) * HEAD_DIM) for h in range(RWKV_HEADS)]

    def phase1(ci, carry):
        rows = pl.ds(pl.multiple_of(ci * c, c), c)
        ld = ld_s[rows, :]
        ld_hi, ld_lo = _split(ld)
        cum = (jnp.dot(tril_ones, ld_hi, preferred_element_type=F32)
               + jnp.dot(tril_ones, ld_lo, preferred_element_type=F32))
        e_inc = jnp.exp(cum)
        e_exc = jnp.exp(cum - ld)
        e_neg = jnp.exp(-cum)
        rt = r_s[rows, :] * e_inc
        kkt = kk_s[rows, :] * e_exc
        kt = k_s[rows, :] * e_neg
        bt = b_s[rows, :] * e_neg
        g_end = e_inc[c - 1:c, :]
        bc = bt * g_end
        kc = kt * g_end
        vv = v_s[rows, :]
        x = [jnp.concatenate([kkt[:, s], rt[:, s]], axis=0) for s in heads]
        zz = [jnp.concatenate([bt[:, s], kt[:, s]], axis=0) for s in heads]
        amat = [_dotp(xh, zh, NT, PREC["a"]) for xh, zh in zip(x, zz)]
        l_b = [jnp.where(strict, am[0:c, 0:c], 0.0) for am in amat]
        l_k = [jnp.where(strict, am[0:c, c:2 * c], 0.0) for am in amat]
        a_r = [jnp.where(incl2, am[c:2 * c, :], 0.0) for am in amat]
        tinv = _unit_lower_inverse(l_b, c)
        lkv = [_dotp(lk, vv[:, s], NN, PREC["lkv"]) for lk, s in zip(l_k, heads)]
        wu = [-_dotp(t, jnp.concatenate([kkt[:, s], lv], axis=1), NN, PREC["tw"])
              for t, lv, s in zip(tinv, lkv, heads)]
        m = [jnp.concatenate([w, jnp.concatenate([zeros_cv, vv[:, s]], axis=1)], axis=0)
             for w, s in zip(wu, heads)]
        am2 = [_dotp(ar, mh, NN, PREC["am"]) for ar, mh in zip(a_r, m)]
        gh = [_dotp(mh, jnp.concatenate([bc[:, s], kc[:, s]], axis=0), TN, PREC["gh"])
              for mh, s in zip(m, heads)]
        for h, s in enumerate(heads):
            idx = ci * RWKV_HEADS + h
            rw_s[idx] = rt[:, s] + am2[h][:, 0:HEAD_DIM]
            y0_s[idx] = am2[h][:, HEAD_DIM:2 * HEAD_DIM]
            gm_s[idx] = gh[h][0:HEAD_DIM, :] + eye_k * g_end[:, s]
            h0_s[idx] = gh[h][HEAD_DIM:2 * HEAD_DIM, :]
        return carry

    lax.fori_loop(0, tt // c, phase1, 0)

    def phase2(ci, carry):
        rows = pl.ds(pl.multiple_of(ci * c, c), c)
        for h, s in enumerate(heads):
            idx = ci * RWKV_HEADS + h
            s_h = state_ref[h]
            y = _dotp(rw_s[idx], s_h, NT, PREC["y"]) + y0_s[idx]
            state_ref[h] = _dotp(s_h, gm_s[idx], NN, PREC["s"]) + h0_s[idx]
            mean = jnp.mean(y, axis=-1, keepdims=True)
            yc = y - mean
            var = jnp.mean(yc * yc, axis=-1, keepdims=True)
            y_s[rows, s] = yc * lax.rsqrt(var + GN_EPS)
        return carry

    lax.fori_loop(0, tt // c, phase2, 0)
    o_ref[0] = (y_s[...] * lng_ref[...] + lnb_ref[...] + bon_s[...]) * g_s[...]

    @pl.when(i == pl.num_programs(1) - 1)
    def _():
        sout_ref[0] = state_ref[...]


def _rwkv(zr3, blk_off, b, nt, shift_prev, s_prev, p, tt, c):
    vec = lambda n: pl.BlockSpec((1, n), lambda bi, ti: (0, 0))
    mat = lambda m, n: pl.BlockSpec((m, n), lambda bi, ti: (0, 0))
    rw = RWKV_WIDTH
    scr = lambda: pltpu.VMEM((tt, rw), F32)
    nch = (tt // c) * RWKV_HEADS
    return pl.pallas_call(
        functools.partial(_rwkv_body, tt=tt, c=c),
        grid=(b, nt),
        in_specs=[
            pl.BlockSpec((1, tt, SHIFT_WIDTH), lambda bi, ti: (blk_off + bi * nt + ti, 0, 0)),
            pl.BlockSpec((1, 1, SHIFT_WIDTH), lambda bi, ti: (bi, 0, 0)),
            pl.BlockSpec((1, RWKV_HEADS, HEAD_DIM, HEAD_DIM), lambda bi, ti: (bi, 0, 0, 0)),
            vec(SHIFT_WIDTH), vec(rw), mat(DECAY_LORA, rw), vec(rw), mat(A_LORA, rw), mat(GATE_LORA, rw),
            vec(rw), vec(rw), vec(rw), vec(rw), vec(rw),
        ],
        out_specs=[
            pl.BlockSpec((1, tt, rw), lambda bi, ti: (bi, ti, 0)),
            pl.BlockSpec((1, RWKV_HEADS, HEAD_DIM, HEAD_DIM), lambda bi, ti: (bi, 0, 0, 0)),
        ],
        out_shape=[
            jax.ShapeDtypeStruct((b, nt * tt, rw), F32),
            jax.ShapeDtypeStruct((b, RWKV_HEADS, HEAD_DIM, HEAD_DIM), F32),
        ],
        scratch_shapes=[
            pltpu.VMEM((1, SHIFT_WIDTH), F32),
            pltpu.VMEM((RWKV_HEADS, HEAD_DIM, HEAD_DIM), F32),
            scr(), scr(), scr(), scr(), scr(), scr(), scr(), scr(), scr(),
            pltpu.VMEM((nch, c, HEAD_DIM), F32),
            pltpu.VMEM((nch, c, HEAD_DIM), F32),
            pltpu.VMEM((nch, HEAD_DIM, HEAD_DIM), F32),
            pltpu.VMEM((nch, HEAD_DIM, HEAD_DIM), F32),
        ],
        compiler_params=_params(("parallel", "arbitrary")),
        name="rwkv",
    )(zr3, shift_prev, s_prev, p["mu"], p["w0"], p["wd"], p["a0"], p["wa"], p["wg"], p["kk"], p["ka"],
      p["rk"], p["lng"], p["lnb"])


def _gmlp_body(z_ref, lng_ref, lnb_ref, ws_ref, bs_ref, o_ref, v_ref):
    z = z_ref[...]
    ge = 0.5 * z * (1.0 + lax.erf(z * (1.0 / math.sqrt(2.0))))
    u = ge[:, 0:GMLP_WIDTH]
    v = ge[:, GMLP_WIDTH:2 * GMLP_WIDTH]
    mean = jnp.mean(v, axis=-1, keepdims=True)
    vc = v - mean
    var = jnp.mean(vc * vc, axis=-1, keepdims=True)
    vn = vc * lax.rsqrt(var + LN_EPS) * lng_ref[...] + lnb_ref[...]
    v_ref[...] = vn
    gd = GMLP_WIDTH // GMLP_GROUPS
    for g in range(GMLP_GROUPS):
        gs = slice(g * gd, (g + 1) * gd)
        sv = jnp.dot(ws_ref[g], vn[:, gs].astype(BF16), preferred_element_type=F32) + bs_ref[:, g:g + 1]
        o_ref[:, gs] = u[:, gs] * sv


def _gmlp(zg, blk_off, n_rows, lng, lnb, ws_bf16, bs):
    return pl.pallas_call(
        _gmlp_body,
        grid=(n_rows // CHUNK,),
        in_specs=[
            pl.BlockSpec((CHUNK, 2 * GMLP_WIDTH), lambda i: (blk_off + i, 0)),
            pl.BlockSpec((1, GMLP_WIDTH), lambda i: (0, 0)),
            pl.BlockSpec((1, GMLP_WIDTH), lambda i: (0, 0)),
            pl.BlockSpec((GMLP_GROUPS, CHUNK, CHUNK), lambda i: (0, 0, 0)),
            pl.BlockSpec((CHUNK, GMLP_GROUPS), lambda i: (0, 0)),
        ],
        out_specs=[
            pl.BlockSpec((CHUNK, GMLP_WIDTH), lambda i: (i, 0)),
            pl.BlockSpec((CHUNK, GMLP_WIDTH), lambda i: (i, 0)),
        ],
        out_shape=[
            jax.ShapeDtypeStruct((n_rows, GMLP_WIDTH), F32),
            jax.ShapeDtypeStruct((n_rows, GMLP_WIDTH), F32),
        ],
        compiler_params=_params(("parallel",)),
        name="gmlp",
    )(zg, lng, lnb, ws_bf16, bs)


def _mem_attn_body(q_ref, k_ref, v_ref, o_ref):
    q = q_ref[0]
    mk = k_ref[0]
    mv = v_ref[0]
    hd = MEM_WIDTH // MEM_HEADS
    for h in range(MEM_HEADS):
        hs = slice(h * hd, (h + 1) * hd)
        s = _bdot(q[:, hs], mk[:, hs], NT) * (hd ** -0.5)
        s = s - jnp.max(s, axis=-1, keepdims=True)
        pr = jnp.exp(s)
        pr = pr / jnp.sum(pr, axis=-1, keepdims=True)
        o_ref[0, :, hs] = _bdot(pr, mv[:, hs])


def _mem_attn(q3, blk_off, b, nt, mk, mv, tt):
    return pl.pallas_call(
        _mem_attn_body,
        grid=(b, nt),
        in_specs=[
            pl.BlockSpec((1, tt, MEM_WIDTH), lambda bi, ti: (blk_off + bi * nt + ti, 0, 0)),
            pl.BlockSpec((1, N_MEM, MEM_WIDTH), lambda bi, ti: (bi, 0, 0)),
            pl.BlockSpec((1, N_MEM, MEM_WIDTH), lambda bi, ti: (bi, 0, 0)),
        ],
        out_specs=pl.BlockSpec((1, tt, MEM_WIDTH), lambda bi, ti: (bi, ti, 0)),
        out_shape=jax.ShapeDtypeStruct((b, nt * tt, MEM_WIDTH), F32),
        compiler_params=_params(("parallel", "parallel")),
        name="mem_attn",
    )(q3, mk, mv)


def _merge_body(x_ref, orw_p, orw_s, ogm_p, ogm_s, ome_p, ome_s, sg_ref, wbr_ref, wbg_ref, wbm_ref, wout_ref,
                gffn_ref, wrt_ref, brt_ref, x2_ref, h2_ref, route_ref, cnt_ref, count_s, *, np_tiles):
    i = pl.program_id(0)
    d = D_MODEL
    tm = x_ref.shape[0]
    is_p = i < np_tiles
    orw = jnp.where(is_p, orw_p[...], orw_s[...])
    ogm = jnp.where(is_p, ogm_p[...], ogm_s[...])
    ome = jnp.where(is_p, ome_p[...], ome_s[...])
    merged = sg_ref[:, 0:d].astype(F32) * _bdot(orw, wbr_ref[...])
    merged = merged + sg_ref[:, d:2 * d].astype(F32) * _bdot(ogm, wbg_ref[...])
    merged = merged + sg_ref[:, 2 * d:3 * d].astype(F32) * _bdot(ome, wbm_ref[...])
    x2 = x_ref[...] + _bdot(merged, wout_ref[...])
    x2_ref[...] = x2
    h2 = _rms(x2, gffn_ref[...])
    h2_ref[...] = h2

    @pl.when(i == 0)
    def _():
        count_s[...] = jnp.zeros_like(count_s)

    lane = lax.broadcasted_iota(jnp.int32, (tm, LANES), 1)
    logits = jnp.where(lane < N_EXPERTS, _dot3(h2, wrt_ref[...]) + brt_ref[...], -jnp.inf)
    tops, hots, idxs = [], [], []
    for _ in range(TOP_K):
        top = jnp.max(logits, axis=-1, keepdims=True)
        idx = jnp.min(jnp.where(logits == top, lane, LANES), axis=-1, keepdims=True)
        hot = lane == idx
        logits = jnp.where(hot, -jnp.inf, logits)
        tops.append(top)
        hots.append(hot)
        idxs.append(idx)
    weights = [jnp.exp(t - tops[0]) for t in tops]
    denom = weights[0] + weights[1] + weights[2] + weights[3]
    onehot = jnp.zeros((tm, LANES), F32)
    for hot in hots:
        onehot = onehot + jnp.where(hot, 1.0, 0.0)
    rows_t = lax.broadcasted_iota(jnp.int32, (tm, tm), 0)
    cols_t = lax.broadcasted_iota(jnp.int32, (tm, tm), 1)
    ahead = jnp.where(rows_t > cols_t, 1.0, 0.0).astype(BF16)
    prefix = jnp.dot(ahead, onehot.astype(BF16), preferred_element_type=F32) + count_s[...]
    route = jnp.zeros((tm, LANES), F32)
    for kk in range(TOP_K):
        rank = jnp.sum(jnp.where(hots[kk], prefix, 0.0), axis=-1, keepdims=True)
        route = jnp.where(lane == ROUTE_E + kk, idxs[kk].astype(F32), route)
        route = jnp.where(lane == ROUTE_RANK + kk, rank, route)
        route = jnp.where(lane == ROUTE_GATE + kk, weights[kk] / denom, route)
    route_ref[...] = route
    count_s[...] = count_s[...] + jnp.sum(onehot, axis=0, keepdims=True)
    cnt_ref[...] = jnp.broadcast_to(count_s[...], cnt_ref.shape)


def _merge(x, o_rw, o_gm, o_me, sg, wbr, wbg, wbm, wout, gffn, wrt_pad, brt_pad, n_p):
    n = x.shape[0]
    tm = PROJ_TM
    d = D_MODEL
    np_tiles = n_p // tm
    row = lambda i: (i, 0)
    const = lambda i: (0, 0)
    first = lambda i: (jnp.minimum(i, np_tiles - 1), 0)
    second = lambda i: (jnp.maximum(i - np_tiles, 0), 0)

    def pair(width):
        return [pl.BlockSpec((tm, width), first), pl.BlockSpec((tm, width), second)]

    return pl.pallas_call(
        functools.partial(_merge_body, np_tiles=np_tiles),
        grid=(n // tm,),
        in_specs=[pl.BlockSpec((tm, d), row)] + pair(RWKV_WIDTH) + pair(GMLP_WIDTH) + pair(MEM_WIDTH) + [
            pl.BlockSpec((tm, 3 * d), row),
            pl.BlockSpec((RWKV_WIDTH, d), const),
            pl.BlockSpec((GMLP_WIDTH, d), const),
            pl.BlockSpec((MEM_WIDTH, d), const),
            pl.BlockSpec((d, d), const),
            pl.BlockSpec((1, d), const),
            pl.BlockSpec((d, LANES), const),
            pl.BlockSpec((1, LANES), const),
        ],
        out_specs=[
            pl.BlockSpec((tm, d), row),
            pl.BlockSpec((tm, d), row),
            pl.BlockSpec((tm, LANES), row),
            pl.BlockSpec((8, LANES), const),
        ],
        out_shape=[
            jax.ShapeDtypeStruct((n, d), F32),
            jax.ShapeDtypeStruct((n, d), F32),
            jax.ShapeDtypeStruct((n, LANES), F32),
            jax.ShapeDtypeStruct((8, LANES), F32),
        ],
        scratch_shapes=[pltpu.VMEM((1, LANES), F32)],
        compiler_params=_params(("arbitrary",)),
        name="merge",
    )(x, o_rw[0], o_rw[1], o_gm[0], o_gm[1], o_me[0], o_me[1], sg, wbr, wbg, wbm, wout, gffn, wrt_pad, brt_pad)


N_ZERO_BLOCKS = 2 * N_EXPERTS


def _dispatch_body(dest_ref, zlist_ref, h_ref, xs_hbm, zero_s, sem_z, sem):
    i = pl.program_id(0)
    tm = DISPATCH_TM
    bm = MOE_BM

    @pl.when(i == 0)
    def _():
        zero_s[...] = jnp.zeros_like(zero_s)

        def zero_copy(q):
            start = pl.multiple_of(zlist_ref[q] * bm, bm)
            return pltpu.make_async_copy(zero_s, xs_hbm.at[pl.ds(start, bm)], sem_z)

        def start(q, carry):
            @pl.when(zlist_ref[q] >= 0)
            def _():
                zero_copy(q).start()
            return carry

        def wait(q, carry):
            @pl.when(zlist_ref[q] >= 0)
            def _():
                zero_copy(q).wait()
            return carry

        lax.fori_loop(0, N_ZERO_BLOCKS, start, 0)
        lax.fori_loop(0, N_ZERO_BLOCKS, wait, 0)

    base = i * (tm * TOP_K)

    def body(r, carry):
        for kk in range(TOP_K):
            slot = dest_ref[base + r * TOP_K + kk]
            pltpu.make_async_copy(h_ref.at[pl.ds(r, 1)], xs_hbm.at[pl.ds(slot, 1)], sem).start()
        return carry

    lax.fori_loop(0, tm, body, 0, unroll=2)
    for kk in range(TOP_K):
        pltpu.make_async_copy(h_ref, xs_hbm.at[pl.ds(0, tm)], sem).wait()


def _dispatch(dest, zlist, h, n_blocks):
    n, d = h.shape
    tm = DISPATCH_TM
    grid_spec = pltpu.PrefetchScalarGridSpec(
        num_scalar_prefetch=2,
        grid=(n // tm,),
        in_specs=[pl.BlockSpec((tm, d), lambda i, dest, zl: (i, 0))],
        out_specs=pl.BlockSpec(memory_space=pl.ANY),
        scratch_shapes=[
            pltpu.VMEM((MOE_BM, d), F32),
            pltpu.SemaphoreType.DMA(()),
            pltpu.SemaphoreType.DMA(()),
        ],
    )
    return pl.pallas_call(
        _dispatch_body,
        grid_spec=grid_spec,
        out_shape=jax.ShapeDtypeStruct((n_blocks * MOE_BM, d), F32),
        compiler_params=_params(("arbitrary",)),
        name="moe_dispatch",
    )(dest, zlist, h)


CAST_ROWS = 64


def _moe_body(be_ref, nused_ref, x_ref, w1_ref, b1_ref, w2_ref, b2_ref, o_ref, w1b, w2b):
    j = pl.program_id(0)
    n_used = nused_ref[0]

    @pl.when(j < n_used)
    def _():
        changed = jnp.logical_or(j == 0, be_ref[j] != be_ref[jnp.maximum(j - 1, 0)])

        @pl.when(changed)
        def _():
            def cast1(q, carry):
                rows = pl.ds(pl.multiple_of(q * CAST_ROWS, CAST_ROWS), CAST_ROWS)
                w1b[rows, :] = w1_ref[0, rows, :].astype(BF16)
                return carry

            def cast2(q, carry):
                rows = pl.ds(pl.multiple_of(q * CAST_ROWS, CAST_ROWS), CAST_ROWS)
                w2b[rows, :] = w2_ref[0, rows, :].astype(BF16)
                return carry

            lax.fori_loop(0, D_MODEL // CAST_ROWS, cast1, 0)
            lax.fori_loop(0, D_FF // CAST_ROWS, cast2, 0)

        x = x_ref[...].astype(BF16)
        z = jnp.dot(x, w1b[...], preferred_element_type=F32) + b1_ref[0]
        zg = jnp.minimum(z[:, 0:D_FF], SWIGLU_LIMIT)
        zl = jnp.clip(z[:, D_FF:2 * D_FF], -SWIGLU_LIMIT, SWIGLU_LIMIT)
        act = zg * jax.nn.sigmoid(SWIGLU_ALPHA * zg) * (zl + 1.0)
        o_ref[...] = jnp.dot(act.astype(BF16), w2b[...], preferred_element_type=F32) + b2_ref[0]

    @pl.when(j >= n_used)
    def _():
        o_ref[...] = jnp.zeros_like(o_ref)


def _moe(blk_e, n_used, xs, w1, b1, w2, b2, n_blocks):
    bm = MOE_BM
    d = D_MODEL
    grid_spec = pltpu.PrefetchScalarGridSpec(
        num_scalar_prefetch=2,
        grid=(n_blocks,),
        in_specs=[
            pl.BlockSpec((bm, d), lambda j, be, nu: (j, 0)),
            pl.BlockSpec((1, d, 2 * D_FF), lambda j, be, nu: (be[j], 0, 0)),
            pl.BlockSpec((1, 1, 2 * D_FF), lambda j, be, nu: (be[j], 0, 0)),
            pl.BlockSpec((1, D_FF, d), lambda j, be, nu: (be[j], 0, 0)),
            pl.BlockSpec((1, 1, d), lambda j, be, nu: (be[j], 0, 0)),
        ],
        out_specs=pl.BlockSpec((bm, d), lambda j, be, nu: (j, 0)),
        scratch_shapes=[
            pltpu.VMEM((d, 2 * D_FF), BF16),
            pltpu.VMEM((D_FF, d), BF16),
        ],
    )
    return pl.pallas_call(
        _moe_body,
        grid_spec=grid_spec,
        out_shape=jax.ShapeDtypeStruct((n_blocks * bm, d), F32),
        compiler_params=_params(("arbitrary",)),
        name="moe_ffn",
    )(blk_e, n_used, xs, w1, b1, w2, b2)


def _combine_body(dest_ref, x2_ref, route_ref, gfin_ref, yb_hbm, op_ref, os_ref, buf, sem, *, np_tiles):
    i = pl.program_id(0)
    nb = pl.num_programs(0)
    tm = COMBINE_TM

    def issue(blk, slot):
        base = blk * (tm * TOP_K)

        def body(r, carry):
            for kk in range(TOP_K):
                d = dest_ref[base + r * TOP_K + kk]
                pltpu.make_async_copy(yb_hbm.at[pl.ds(d, 1)], buf.at[slot, kk, pl.ds(r, 1)], sem.at[slot]).start()
            return carry

        lax.fori_loop(0, tm, body, 0, unroll=2)

    slot = lax.rem(i, 2)

    @pl.when(i == 0)
    def _():
        issue(0, 0)

    @pl.when(i + 1 < nb)
    def _():
        issue(i + 1, 1 - slot)

    acc = x2_ref[...]
    for kk in range(TOP_K):
        pltpu.make_async_copy(yb_hbm.at[pl.ds(0, tm)], buf.at[slot, kk], sem.at[slot]).wait()
    for kk in range(TOP_K):
        acc = acc + route_ref[:, ROUTE_GATE + kk:ROUTE_GATE + kk + 1] * buf[slot, kk]
    y = _rms(acc, gfin_ref[...])

    @pl.when(i < np_tiles)
    def _():
        op_ref[...] = y

    @pl.when(i >= np_tiles)
    def _():
        os_ref[...] = y


def _combine(dest, x2, route, gfin, yb, n_p):
    n, d = x2.shape
    tm = COMBINE_TM
    np_tiles = n_p // tm
    grid_spec = pltpu.PrefetchScalarGridSpec(
        num_scalar_prefetch=1,
        grid=(n // tm,),
        in_specs=[
            pl.BlockSpec((tm, d), lambda i, dest: (i, 0)),
            pl.BlockSpec((tm, LANES), lambda i, dest: (i, 0)),
            pl.BlockSpec((1, d), lambda i, dest: (0, 0)),
            pl.BlockSpec(memory_space=pl.ANY),
        ],
        out_specs=[
            pl.BlockSpec((tm, d), lambda i, dest: (jnp.minimum(i, np_tiles - 1), 0)),
            pl.BlockSpec((tm, d), lambda i, dest: (jnp.maximum(i - np_tiles, 0), 0)),
        ],
        scratch_shapes=[
            pltpu.VMEM((2, TOP_K, tm, d), F32),
            pltpu.SemaphoreType.DMA((2,)),
        ],
    )
    return pl.pallas_call(
        functools.partial(_combine_body, np_tiles=np_tiles),
        grid_spec=grid_spec,
        out_shape=[jax.ShapeDtypeStruct((n_p, d), F32), jax.ShapeDtypeStruct((n - n_p, d), F32)],
        compiler_params=_params(("arbitrary",)),
        name="moe_combine",
    )(dest, x2, route, gfin, yb)


def _slot_tables(route, counts_row, n_blocks):
    bm = MOE_BM
    e = route[:, ROUTE_E:ROUTE_E + TOP_K].astype(jnp.int32)
    rank = route[:, ROUTE_RANK:ROUTE_RANK + TOP_K].astype(jnp.int32)
    counts = counts_row[:N_EXPERTS].astype(jnp.int32)
    padded = (counts + bm - 1) // bm * bm
    pad_end = jnp.cumsum(padded)
    pad_start = pad_end - padded
    dest = (jnp.take(pad_start, e) + rank).reshape(-1)
    blk_start = jnp.arange(n_blocks, dtype=jnp.int32) * bm
    blk_e = jnp.minimum(jnp.sum(pad_end[None, :] <= blk_start[:, None], axis=1), N_EXPERTS - 1).astype(jnp.int32)
    n_used = pad_end[-1] // bm
    last_blk = jnp.where(padded > 0, pad_end // bm - 1, -1)
    trailing = n_used + jnp.arange(N_EXPERTS, dtype=jnp.int32)
    trailing = jnp.where(trailing < n_blocks, trailing, -1)
    zlist = jnp.concatenate([last_blk, trailing]).astype(jnp.int32)
    return dest.astype(jnp.int32), blk_e, n_used.astype(jnp.int32).reshape(1), zlist


def kernel(x_prompt, x_sample, mem_prompt, state_shift, state_wkv, cache_mem_k, cache_mem_v, g_norm_mix, w_in, mu_shift, w0, w_decay_up, a0, w_a_up, w_g_up, k_k, k_a, r_k, ln_x_g, ln_x_b, gmlp_ln_g, gmlp_ln_b, w_spatial, b_spatial, g_norm_mem, w_mem_kv, w_br_rwkv, w_br_gmlp, w_br_mem, w_out, g_norm_ffn, w_router, b_router, w_exp1, b_exp1, w_exp2, b_exp2, g_norm_final):
    bp, tp, d = x_prompt.shape
    bs, ts, _ = x_sample.shape
    n_p, n_s = bp * tp, bs * ts
    n_all = n_p + n_s
    l = 0
    row = lambda a: a.reshape(1, -1)

    x_all = jnp.concatenate([x_prompt.reshape(n_p, d), x_sample.reshape(n_s, d)], axis=0)
    zr, zg, zq, sg = _in_proj(x_all, row(g_norm_mix[l]), w_in[l].astype(BF16))

    mk_p, mv_p = _mem_kv(mem_prompt.reshape(bp * N_MEM, d), row(g_norm_mem[l]), w_mem_kv[l].astype(BF16))

    rp = dict(mu=row(mu_shift[l]), w0=row(w0[l]), wd=w_decay_up[l].astype(BF16), a0=row(a0[l]),
              wa=w_a_up[l].astype(BF16), wg=w_g_up[l].astype(BF16), kk=row(k_k[l]), ka=row(k_a[l]),
              rk=row(r_k[l]), lng=row(ln_x_g[l]), lnb=row(ln_x_b[l]))
    o_rw_p, s_p = _rwkv(zr.reshape(n_all // RWKV_TT, RWKV_TT, SHIFT_WIDTH), 0, bp, tp // RWKV_TT,
                        jnp.zeros((bp, 1, SHIFT_WIDTH), F32), jnp.zeros((bp, RWKV_HEADS, HEAD_DIM, HEAD_DIM), F32),
                        rp, tt=RWKV_TT, c=RWKV_C)
    o_rw_s, s_s = _rwkv(zr.reshape(n_all // ts, ts, SHIFT_WIDTH), n_p // ts, bs, 1,
                        state_shift[l].reshape(bs, 1, SHIFT_WIDTH), state_wkv[l], rp, tt=ts, c=ts)

    tri = jnp.tril(jnp.ones((CHUNK, CHUNK), bool))
    ws_p = jnp.where(tri, w_spatial[l], 0.0).astype(BF16)
    bs_p = b_spatial[l].T
    reps = CHUNK // ts
    tri_s = jnp.tril(jnp.ones((ts, ts), bool))
    ws_small = jnp.where(tri_s, w_spatial[l][:, :ts, :ts], 0.0)
    eye = jnp.eye(reps, dtype=F32)
    ws_s = jnp.einsum("ab,gij->gaibj", eye, ws_small).reshape(GMLP_GROUPS, CHUNK, CHUNK).astype(BF16)
    bs_s = jnp.tile(b_spatial[l][:, :ts], (1, reps)).T
    lng, lnb = row(gmlp_ln_g[l]), row(gmlp_ln_b[l])
    o_gm_p, _ = _gmlp(zg, 0, n_p, lng, lnb, ws_p, bs_p)
    o_gm_s, v_rows_s = _gmlp(zg, n_p // CHUNK, n_s, lng, lnb, ws_s, bs_s)

    o_me_p = _mem_attn(zq.reshape(n_all // ATTN_TT, ATTN_TT, MEM_WIDTH), 0, bp, tp // ATTN_TT,
                       mk_p.reshape(bp, N_MEM, MEM_WIDTH), mv_p.reshape(bp, N_MEM, MEM_WIDTH), tt=ATTN_TT)
    o_me_s = _mem_attn(zq.reshape(n_all // ts, ts, MEM_WIDTH), n_p // ts, bs, 1,
                       cache_mem_k[l].reshape(bs, N_MEM, MEM_WIDTH), cache_mem_v[l].reshape(bs, N_MEM, MEM_WIDTH),
                       tt=ts)

    wrt_pad = jnp.zeros((d, LANES), F32).at[:, :N_EXPERTS].set(w_router[l])
    brt_pad = jnp.zeros((1, LANES), F32).at[0, :N_EXPERTS].set(b_router[l])
    x2, h2, route, counts = _merge(
        x_all, (o_rw_p.reshape(n_p, RWKV_WIDTH), o_rw_s.reshape(n_s, RWKV_WIDTH)), (o_gm_p, o_gm_s),
        (o_me_p.reshape(n_p, MEM_WIDTH), o_me_s.reshape(n_s, MEM_WIDTH)), sg,
        w_br_rwkv[l].astype(BF16), w_br_gmlp[l].astype(BF16), w_br_mem[l].astype(BF16), w_out[l].astype(BF16),
        row(g_norm_ffn[l]), wrt_pad, brt_pad, n_p)

    n_assign = n_all * TOP_K
    n_blocks = -(-(n_assign + N_EXPERTS * (MOE_BM - 1)) // MOE_BM)
    dest, blk_e, n_used, zlist = _slot_tables(route, counts[0], n_blocks)
    xs = _dispatch(dest, zlist, h2, n_blocks)
    yb = _moe(blk_e, n_used, xs, w_exp1[l], b_exp1[l].reshape(N_EXPERTS, 1, 2 * D_FF),
              w_exp2[l], b_exp2[l].reshape(N_EXPERTS, 1, d), n_blocks)
    y_p, y_s = _combine(dest, x2, route, row(g_norm_final), yb, n_p)

    shift_p = zr[tp - 1:n_p:tp][None]
    shift_s = zr[n_p + ts - 1::ts][None]
    mk_out = mk_p.reshape(1, bp, N_MEM, MEM_HEADS, MEM_WIDTH // MEM_HEADS)
    mv_out = mv_p.reshape(1, bp, N_MEM, MEM_HEADS, MEM_WIDTH // MEM_HEADS)
    return (y_p.reshape(bp, tp, d), y_s.reshape(bs, ts, d), shift_p, s_p[None], mk_out, mv_out, shift_s, s_s[None],
            v_rows_s.reshape(1, bs, ts, GMLP_WIDTH))
```

```python
import functools
import math

import jax
import jax.numpy as jnp
from jax import lax
from jax.experimental import pallas as pl
from jax.experimental.pallas import tpu as pltpu

F32 = jnp.float32
BF16 = jnp.bfloat16

D_MODEL = 1024
RWKV_HEADS = 8
HEAD_DIM = 64
RWKV_WIDTH = RWKV_HEADS * HEAD_DIM
DECAY_LORA = 64
A_LORA = 64
GATE_LORA = 128
GMLP_GROUPS = 4
GMLP_WIDTH = 256
CHUNK = 128
MEM_HEADS = 4
MEM_WIDTH = 256
N_MEM = 256
N_EXPERTS = 32
TOP_K = 4
D_FF = 1024
SWIGLU_ALPHA = 1.702
SWIGLU_LIMIT = 7.0
RMS_EPS = 1e-5
LN_EPS = 1e-5
GN_EPS = 64e-5
SHIFT_WIDTH = 3 * RWKV_WIDTH + DECAY_LORA + A_LORA + GATE_LORA
OFF_GMLP = SHIFT_WIDTH
OFF_QMEM = OFF_GMLP + 2 * GMLP_WIDTH
OFF_GATE = OFF_QMEM + MEM_WIDTH
IN_WIDTH = OFF_GATE + 3 * D_MODEL
LANES = 128

PROJ_TM = 256
RWKV_TT = 512
RWKV_C = 64
RWKV_SAMPLE_SEQS = 8
ATTN_TT = 512
ATTN_BB = 8
MOE_BM = 512
DISPATCH_TM = 256
COMBINE_TM = 128

ROUTE_E = 0
ROUTE_RANK = 4
ROUTE_GATE = 8

NN = ((1,), (0,))
NT = ((1,), (1,))
TN = ((0,), (0,))

VMEM_LIMIT = 56 * 1024 * 1024


def _params(sem, vmem=VMEM_LIMIT):
    return pltpu.CompilerParams(dimension_semantics=sem, vmem_limit_bytes=vmem)


def _bdot(a, b, dims=NN):
    return lax.dot_general(a.astype(BF16), b.astype(BF16), (dims, ((), ())), preferred_element_type=F32)


def _split(x):
    hi = x.astype(BF16)
    lo = (x - hi.astype(F32)).astype(BF16)
    return hi, lo


def _dot3(a, b, dims=NN):
    dn = (dims, ((), ()))
    ah, al = _split(a)
    bh, bl = _split(b)
    r = lax.dot_general(ah, bh, dn, preferred_element_type=F32)
    r = r + lax.dot_general(al, bh, dn, preferred_element_type=F32)
    return r + lax.dot_general(ah, bl, dn, preferred_element_type=F32)


def _rms(x, g):
    return x * lax.rsqrt(jnp.mean(x * x, axis=-1, keepdims=True) + RMS_EPS) * g


def _sigmoid(x):
    return 0.5 * jnp.tanh(0.5 * x) + 0.5


def _in_proj_body(x_ref, g_ref, w_ref, zr_ref, zg_ref, zq_ref, sg_ref):
    h = _rms(x_ref[...], g_ref[...]).astype(BF16)
    zr_ref[...] = jnp.dot(h, w_ref[:, 0:OFF_GMLP], preferred_element_type=F32)
    zg_ref[...] = jnp.dot(h, w_ref[:, OFF_GMLP:OFF_QMEM], preferred_element_type=F32)
    zq_ref[...] = jnp.dot(h, w_ref[:, OFF_QMEM:OFF_GATE], preferred_element_type=F32)
    gates = jnp.dot(h, w_ref[:, OFF_GATE:IN_WIDTH], preferred_element_type=F32)
    sg_ref[...] = _sigmoid(gates).astype(BF16)


def _in_proj(x, g, w_bf16):
    n = x.shape[0]
    tm = PROJ_TM
    row = lambda i: (i, 0)
    const = lambda i: (0, 0)
    return pl.pallas_call(
        _in_proj_body,
        grid=(n // tm,),
        in_specs=[
            pl.BlockSpec((tm, D_MODEL), row),
            pl.BlockSpec((1, D_MODEL), const),
            pl.BlockSpec((D_MODEL, IN_WIDTH), const, pipeline_mode=pl.Buffered(1)),
        ],
        out_specs=[
            pl.BlockSpec((tm, SHIFT_WIDTH), row),
            pl.BlockSpec((tm, 2 * GMLP_WIDTH), row),
            pl.BlockSpec((tm, MEM_WIDTH), row),
            pl.BlockSpec((tm, 3 * D_MODEL), row),
        ],
        out_shape=[
            jax.ShapeDtypeStruct((n, SHIFT_WIDTH), F32),
            jax.ShapeDtypeStruct((n, 2 * GMLP_WIDTH), F32),
            jax.ShapeDtypeStruct((n, MEM_WIDTH), F32),
            jax.ShapeDtypeStruct((n, 3 * D_MODEL), BF16),
        ],
        compiler_params=_params(("parallel",)),
        name="in_proj",
    )(x, g, w_bf16)


def _mem_kv_body(x_ref, g_ref, w_ref, k_ref, v_ref):
    h = _rms(x_ref[...], g_ref[...]).astype(BF16)
    k_ref[...] = jnp.dot(h, w_ref[:, 0:MEM_WIDTH], preferred_element_type=F32)
    v_ref[...] = jnp.dot(h, w_ref[:, MEM_WIDTH:2 * MEM_WIDTH], preferred_element_type=F32)


def _mem_kv(mem, g, w_bf16):
    n = mem.shape[0]
    tm = PROJ_TM
    return pl.pallas_call(
        _mem_kv_body,
        grid=(n // tm,),
        in_specs=[
            pl.BlockSpec((tm, D_MODEL), lambda i: (i, 0)),
            pl.BlockSpec((1, D_MODEL), lambda i: (0, 0)),
            pl.BlockSpec((D_MODEL, 2 * MEM_WIDTH), lambda i: (0, 0)),
        ],
        out_specs=[pl.BlockSpec((tm, MEM_WIDTH), lambda i: (i, 0))] * 2,
        out_shape=[jax.ShapeDtypeStruct((n, MEM_WIDTH), F32)] * 2,
        compiler_params=_params(("parallel",)),
        name="mem_kv",
    )(mem, g, w_bf16)


EXP_M05 = math.exp(-0.5)


def _dotp(a, b, dims, passes):
    return _dot3(a, b, dims) if passes == 3 else _bdot(a, b, dims)


PREC = dict(a=1, t=1, lkv=1, tw=1, am=1, gh=3, y=1, s=3)


def _unit_lower_inverse(lows, c):
    rows = lax.broadcasted_iota(jnp.int32, (c, c), 0)
    cols = lax.broadcasted_iota(jnp.int32, (c, c), 1)
    eye = (rows == cols).astype(F32)
    invs = [eye - low for low in lows]
    powers = lows
    for _ in range(int(math.log2(c)) - 1):
        powers = [_dotp(pw, pw, NN, PREC["t"]) for pw in powers]
        invs = [inv + _dotp(inv, pw, NN, PREC["t"]) for inv, pw in zip(invs, powers)]
    return invs


def _rwkv_body(zr_ref, shift_ref, s0_ref, mu_ref, w0_ref, wd_ref, a0_ref, wa_ref, wg_ref, kk_ref, ka_ref,
               rk_ref, lng_ref, lnb_ref, hsum_ref,
               o_ref, sout_ref,
               carry_ref, state_ref, r_s, k_s, v_s, kk_s, b_s, ld_s, y_s, bon_s, g_s,
               rw_s, y0_s, gm_s, h0_s, *, tt, c, nseq):
    i = pl.program_id(1)
    rw = RWKV_WIDTH
    nh = RWKV_HEADS

    @pl.when(i == 0)
    def _():
        for q in range(nseq):
            state_ref[q * nh:(q + 1) * nh] = s0_ref[q]

    z = zr_ref[0]
    z_prev = pltpu.roll(z, 1, 0)
    row = lax.broadcasted_iota(jnp.int32, z.shape, 0)
    if nseq == 1:
        @pl.when(i == 0)
        def _():
            carry_ref[...] = shift_ref[0]

        z_prev = jnp.where(row == 0, carry_ref[...], z_prev)
        carry_ref[...] = z[tt - 1:tt, :]
    else:
        first_rows = jnp.concatenate([jnp.broadcast_to(shift_ref[q], (c, SHIFT_WIDTH)) for q in range(nseq)], axis=0)
        z_prev = jnp.where(row % c == 0, first_rows, z_prev)
    zs = z + mu_ref[...] * (z_prev - z)
    r = zs[:, 0:rw]
    k = zs[:, rw:2 * rw]
    v = zs[:, 2 * rw:3 * rw]
    zw = zs[:, 3 * rw:3 * rw + DECAY_LORA]
    za = zs[:, 3 * rw + DECAY_LORA:3 * rw + DECAY_LORA + A_LORA]
    zg = zs[:, 3 * rw + DECAY_LORA + A_LORA:SHIFT_WIDTH]
    xw = w0_ref[...] + _bdot(jnp.tanh(zw), wd_ref[...])
    ld_s[...] = -EXP_M05 * _sigmoid(xw)
    a = _sigmoid(a0_ref[...] + _bdot(za, wa_ref[...]))
    g_s[...] = _bdot(_sigmoid(zg), wg_ref[...])
    kk = k * kk_ref[...]
    k = k * (1.0 + (a - 1.0) * ka_ref[...])
    r_s[...] = r
    k_s[...] = k
    v_s[...] = v

    def head_sum(t):
        return jnp.dot(t.astype(BF16), hsum_ref[...], preferred_element_type=F32)

    kk = kk / jnp.maximum(jnp.sqrt(head_sum(kk * kk)), 1e-12)
    kk_s[...] = kk
    b_s[...] = kk * a
    bon_s[...] = head_sum(r * k * rk_ref[...]) * v

    rows_c = lax.broadcasted_iota(jnp.int32, (c, c), 0)
    cols_c = lax.broadcasted_iota(jnp.int32, (c, c), 1)
    strict = rows_c > cols_c
    incl = rows_c >= cols_c
    tril_ones = incl.astype(BF16)
    rows_2c = lax.broadcasted_iota(jnp.int32, (c, 2 * c), 0)
    cols_2c = lax.broadcasted_iota(jnp.int32, (c, 2 * c), 1)
    incl2 = rows_2c >= jnp.where(cols_2c >= c, cols_2c - c, cols_2c)
    rows_k = lax.broadcasted_iota(jnp.int32, (HEAD_DIM, HEAD_DIM), 0)
    cols_k = lax.broadcasted_iota(jnp.int32, (HEAD_DIM, HEAD_DIM), 1)
    eye_k = (rows_k == cols_k).astype(F32)
    zeros_cv = jnp.zeros((c, HEAD_DIM), F32)
    heads = [slice(h * HEAD_DIM, (h + 1) * HEAD_DIM) for h in range(RWKV_HEADS)]

    n_chunks = tt // c
    group = nseq if nseq > 1 else (2 if n_chunks % 2 == 0 else 1)

    def chunk_rows(ci):
        return pl.ds(ci * c, c) if isinstance(ci, int) else pl.ds(pl.multiple_of(ci * c, c), c)

    def scaled(ci):
        rows = chunk_rows(ci)
        ld = ld_s[rows, :]
        ld_hi, ld_lo = _split(ld)
        cum = (jnp.dot(tril_ones, ld_hi, preferred_element_type=F32)
               + jnp.dot(tril_ones, ld_lo, preferred_element_type=F32))
        e_inc = jnp.exp(cum)
        e_neg = jnp.exp(-cum)
        kt = k_s[rows, :] * e_neg
        bt = b_s[rows, :] * e_neg
        g_end = e_inc[c - 1:c, :]
        return dict(ci=ci, rt=r_s[rows, :] * e_inc, kkt=kk_s[rows, :] * jnp.exp(cum - ld), kt=kt, bt=bt,
                    g_end=g_end, bc=bt * g_end, kc=kt * g_end, vv=v_s[rows, :])

    def phase1(cj, carry):
        chains = [(ch, h, s) for ch in [scaled(cj * group + g) for g in range(group)]
                  for h, s in enumerate(heads)]
        x = [jnp.concatenate([ch["kkt"][:, s], ch["rt"][:, s]], axis=0) for ch, _, s in chains]
        zz = [jnp.concatenate([ch["bt"][:, s], ch["kt"][:, s]], axis=0) for ch, _, s in chains]
        amat = [_dotp(xh, zh, NT, PREC["a"]) for xh, zh in zip(x, zz)]
        l_b = [jnp.where(strict, am[0:c, 0:c], 0.0) for am in amat]
        l_k = [jnp.where(strict, am[0:c, c:2 * c], 0.0) for am in amat]
        a_r = [jnp.where(incl2, am[c:2 * c, :], 0.0) for am in amat]
        tinv = _unit_lower_inverse(l_b, c)
        lkv = [_dotp(lk, ch["vv"][:, s], NN, PREC["lkv"]) for lk, (ch, _, s) in zip(l_k, chains)]
        wu = [-_dotp(t, jnp.concatenate([ch["kkt"][:, s], lv], axis=1), NN, PREC["tw"])
              for t, lv, (ch, _, s) in zip(tinv, lkv, chains)]
        m = [jnp.concatenate([w, jnp.concatenate([zeros_cv, ch["vv"][:, s]], axis=1)], axis=0)
             for w, (ch, _, s) in zip(wu, chains)]
        am2 = [_dotp(ar, mh, NN, PREC["am"]) for ar, mh in zip(a_r, m)]
        gh = [_dotp(mh, jnp.concatenate([ch["bc"][:, s], ch["kc"][:, s]], axis=0), TN, PREC["gh"])
              for mh, (ch, _, s) in zip(m, chains)]
        for q, (ch, h, s) in enumerate(chains):
            idx = ch["ci"] * RWKV_HEADS + h
            rw_s[idx] = ch["rt"][:, s] + am2[q][:, 0:HEAD_DIM]
            y0_s[idx] = am2[q][:, HEAD_DIM:2 * HEAD_DIM]
            gm_s[idx] = gh[q][0:HEAD_DIM, :] + eye_k * ch["g_end"][:, s]
            h0_s[idx] = gh[q][HEAD_DIM:2 * HEAD_DIM, :]
        return carry

    if n_chunks == group:
        phase1(0, 0)
    else:
        lax.fori_loop(0, n_chunks // group, phase1, 0)

    if nseq == 1:
        def phase2(ci, carry):
            rows = chunk_rows(ci)
            for h, s in enumerate(heads):
                idx = ci * nh + h
                s_h = state_ref[h]
                y_s[rows, s] = _dotp(rw_s[idx], s_h, NT, PREC["y"]) + y0_s[idx]
                state_ref[h] = _dotp(s_h, gm_s[idx], NN, PREC["s"]) + h0_s[idx]
            return carry

        lax.fori_loop(0, n_chunks, phase2, 0)
    else:
        pairs = [(q, h, s) for q in range(nseq) for h, s in enumerate(heads)]
        states = [state_ref[q * nh + h] for q, h, _ in pairs]
        ys = [_dotp(rw_s[q * nh + h], st, NT, PREC["y"]) + y0_s[q * nh + h] for (q, h, _), st in zip(pairs, states)]
        new = [_dotp(st, gm_s[q * nh + h], NN, PREC["s"]) + h0_s[q * nh + h] for (q, h, _), st in zip(pairs, states)]
        for (q, h, s), yq, nq in zip(pairs, ys, new):
            y_s[chunk_rows(q), s] = yq
            state_ref[q * nh + h] = nq
    y = y_s[...]
    yc = y - head_sum(y) * (1.0 / HEAD_DIM)
    var = head_sum(yc * yc) * (1.0 / HEAD_DIM)
    yn = yc * lax.rsqrt(var + GN_EPS)
    o_ref[0] = (yn * lng_ref[...] + lnb_ref[...] + bon_s[...]) * g_s[...]

    @pl.when(i == pl.num_programs(1) - 1)
    def _():
        for q in range(nseq):
            sout_ref[q] = state_ref[q * nh:(q + 1) * nh]


def _rwkv(zr3, blk_off, b, nt, shift_prev, s_prev, p, tt, c, nseq):
    vec = lambda n: pl.BlockSpec((1, n), lambda bi, ti: (0, 0))
    mat = lambda m, n: pl.BlockSpec((m, n), lambda bi, ti: (0, 0))
    rw = RWKV_WIDTH
    scr = lambda: pltpu.VMEM((tt, rw), F32)
    nch = (tt // c) * RWKV_HEADS
    return pl.pallas_call(
        functools.partial(_rwkv_body, tt=tt, c=c, nseq=nseq),
        grid=(b // nseq, nt),
        in_specs=[
            pl.BlockSpec((1, tt, SHIFT_WIDTH), lambda bi, ti: (blk_off + bi * nt + ti, 0, 0)),
            pl.BlockSpec((nseq, 1, SHIFT_WIDTH), lambda bi, ti: (bi, 0, 0)),
            pl.BlockSpec((nseq, RWKV_HEADS, HEAD_DIM, HEAD_DIM), lambda bi, ti: (bi, 0, 0, 0)),
            vec(SHIFT_WIDTH), vec(rw), mat(DECAY_LORA, rw), vec(rw), mat(A_LORA, rw), mat(GATE_LORA, rw),
            vec(rw), vec(rw), vec(rw), vec(rw), vec(rw), mat(rw, rw),
        ],
        out_specs=[
            pl.BlockSpec((1, tt, rw), lambda bi, ti: (bi, ti, 0)),
            pl.BlockSpec((nseq, RWKV_HEADS, HEAD_DIM, HEAD_DIM), lambda bi, ti: (bi, 0, 0, 0)),
        ],
        out_shape=[
            jax.ShapeDtypeStruct((b // nseq, nt * tt, rw), F32),
            jax.ShapeDtypeStruct((b, RWKV_HEADS, HEAD_DIM, HEAD_DIM), F32),
        ],
        scratch_shapes=[
            pltpu.VMEM((1, SHIFT_WIDTH), F32),
            pltpu.VMEM((nseq * RWKV_HEADS, HEAD_DIM, HEAD_DIM), F32),
            scr(), scr(), scr(), scr(), scr(), scr(), scr(), scr(), scr(),
            pltpu.VMEM((nch, c, HEAD_DIM), F32),
            pltpu.VMEM((nch, c, HEAD_DIM), F32),
            pltpu.VMEM((nch, HEAD_DIM, HEAD_DIM), F32),
            pltpu.VMEM((nch, HEAD_DIM, HEAD_DIM), F32),
        ],
        compiler_params=_params(("parallel", "arbitrary")),
        name="rwkv",
    )(zr3, shift_prev, s_prev, p["mu"], p["w0"], p["wd"], p["a0"], p["wa"], p["wg"], p["kk"], p["ka"],
      p["rk"], p["lng"], p["lnb"], p["hsum"])


def _gmlp_body(z_ref, lng_ref, lnb_ref, ws_ref, bs_ref, o_ref, v_ref):
    z = z_ref[...]
    ge = 0.5 * z * (1.0 + lax.erf(z * (1.0 / math.sqrt(2.0))))
    u = ge[:, 0:GMLP_WIDTH]
    v = ge[:, GMLP_WIDTH:2 * GMLP_WIDTH]
    mean = jnp.mean(v, axis=-1, keepdims=True)
    vc = v - mean
    var = jnp.mean(vc * vc, axis=-1, keepdims=True)
    vn = vc * lax.rsqrt(var + LN_EPS) * lng_ref[...] + lnb_ref[...]
    v_ref[...] = vn
    gd = GMLP_WIDTH // GMLP_GROUPS
    for g in range(GMLP_GROUPS):
        gs = slice(g * gd, (g + 1) * gd)
        sv = jnp.dot(ws_ref[g], vn[:, gs].astype(BF16), preferred_element_type=F32) + bs_ref[:, g:g + 1]
        o_ref[:, gs] = u[:, gs] * sv


def _gmlp(zg, blk_off, n_rows, lng, lnb, ws_bf16, bs):
    return pl.pallas_call(
        _gmlp_body,
        grid=(n_rows // CHUNK,),
        in_specs=[
            pl.BlockSpec((CHUNK, 2 * GMLP_WIDTH), lambda i: (blk_off + i, 0)),
            pl.BlockSpec((1, GMLP_WIDTH), lambda i: (0, 0)),
            pl.BlockSpec((1, GMLP_WIDTH), lambda i: (0, 0)),
            pl.BlockSpec((GMLP_GROUPS, CHUNK, CHUNK), lambda i: (0, 0, 0)),
            pl.BlockSpec((CHUNK, GMLP_GROUPS), lambda i: (0, 0)),
        ],
        out_specs=[
            pl.BlockSpec((CHUNK, GMLP_WIDTH), lambda i: (i, 0)),
            pl.BlockSpec((CHUNK, GMLP_WIDTH), lambda i: (i, 0)),
        ],
        out_shape=[
            jax.ShapeDtypeStruct((n_rows, GMLP_WIDTH), F32),
            jax.ShapeDtypeStruct((n_rows, GMLP_WIDTH), F32),
        ],
        compiler_params=_params(("parallel",)),
        name="gmlp",
    )(zg, lng, lnb, ws_bf16, bs)


def _mem_attn_body(q_ref, k_ref, v_ref, o_ref, *, bb):
    hd = MEM_WIDTH // MEM_HEADS
    pairs = [(s, slice(h * hd, (h + 1) * hd)) for s in range(bb) for h in range(MEM_HEADS)]
    scores = [_bdot(q_ref[s, :, hs], k_ref[s, :, hs], NT) * (hd ** -0.5) for s, hs in pairs]
    probs = []
    for sc in scores:
        pr = jnp.exp(sc - jnp.max(sc, axis=-1, keepdims=True))
        probs.append(pr / jnp.sum(pr, axis=-1, keepdims=True))
    for (s, hs), pr in zip(pairs, probs):
        o_ref[s, :, hs] = _bdot(pr, v_ref[s, :, hs])


def _mem_attn(q3, blk_off, b, nt, mk, mv, tt, bb):
    return pl.pallas_call(
        functools.partial(_mem_attn_body, bb=bb),
        grid=(b // bb, nt),
        in_specs=[
            pl.BlockSpec((bb, tt, MEM_WIDTH), lambda bi, ti: (blk_off // bb + bi * nt + ti, 0, 0)),
            pl.BlockSpec((bb, N_MEM, MEM_WIDTH), lambda bi, ti: (bi, 0, 0)),
            pl.BlockSpec((bb, N_MEM, MEM_WIDTH), lambda bi, ti: (bi, 0, 0)),
        ],
        out_specs=pl.BlockSpec((bb, tt, MEM_WIDTH), lambda bi, ti: (bi, ti, 0)),
        out_shape=jax.ShapeDtypeStruct((b, nt * tt, MEM_WIDTH), F32),
        compiler_params=_params(("parallel", "parallel")),
        name="mem_attn",
    )(q3, mk, mv)


def _merge_body(x_ref, orw_p, orw_s, ogm_p, ogm_s, ome_p, ome_s, sg_ref, wbr_ref, wbg_ref, wbm_ref, wout_ref,
                gffn_ref, wrt_ref, brt_ref, x2_ref, h2_ref, route_ref, cnt_ref, count_s, *, np_tiles):
    i = pl.program_id(0)
    d = D_MODEL
    tm = x_ref.shape[0]
    is_p = i < np_tiles
    orw = jnp.where(is_p, orw_p[...], orw_s[...])
    ogm = jnp.where(is_p, ogm_p[...], ogm_s[...])
    ome = jnp.where(is_p, ome_p[...], ome_s[...])
    merged = sg_ref[:, 0:d].astype(F32) * _bdot(orw, wbr_ref[...])
    merged = merged + sg_ref[:, d:2 * d].astype(F32) * _bdot(ogm, wbg_ref[...])
    merged = merged + sg_ref[:, 2 * d:3 * d].astype(F32) * _bdot(ome, wbm_ref[...])
    x2 = x_ref[...] + _bdot(merged, wout_ref[...])
    x2_ref[...] = x2
    h2 = _rms(x2, gffn_ref[...])
    h2_ref[...] = h2

    @pl.when(i == 0)
    def _():
        count_s[...] = jnp.zeros_like(count_s)

    lane = lax.broadcasted_iota(jnp.int32, (tm, LANES), 1)
    logits = jnp.where(lane < N_EXPERTS, _dot3(h2, wrt_ref[...]) + brt_ref[...], -jnp.inf)
    tops, hots, idxs = [], [], []
    for _ in range(TOP_K):
        top = jnp.max(logits, axis=-1, keepdims=True)
        idx = jnp.min(jnp.where(logits == top, lane, LANES), axis=-1, keepdims=True)
        hot = lane == idx
        logits = jnp.where(hot, -jnp.inf, logits)
        tops.append(top)
        hots.append(hot)
        idxs.append(idx)
    weights = [jnp.exp(t - tops[0]) for t in tops]
    denom = weights[0] + weights[1] + weights[2] + weights[3]
    onehot = jnp.zeros((tm, LANES), F32)
    for hot in hots:
        onehot = onehot + jnp.where(hot, 1.0, 0.0)
    rows_t = lax.broadcasted_iota(jnp.int32, (tm, tm), 0)
    cols_t = lax.broadcasted_iota(jnp.int32, (tm, tm), 1)
    ahead = jnp.where(rows_t > cols_t, 1.0, 0.0).astype(BF16)
    prefix = jnp.dot(ahead, onehot.astype(BF16), preferred_element_type=F32) + count_s[...]
    route = jnp.zeros((tm, LANES), F32)
    for kk in range(TOP_K):
        rank = jnp.sum(jnp.where(hots[kk], prefix, 0.0), axis=-1, keepdims=True)
        route = jnp.where(lane == ROUTE_E + kk, idxs[kk].astype(F32), route)
        route = jnp.where(lane == ROUTE_RANK + kk, rank, route)
        route = jnp.where(lane == ROUTE_GATE + kk, weights[kk] / denom, route)
    route_ref[...] = route
    count_s[...] = count_s[...] + jnp.sum(onehot, axis=0, keepdims=True)
    cnt_ref[...] = jnp.broadcast_to(count_s[...], cnt_ref.shape)


def _merge(x, o_rw, o_gm, o_me, sg, wbr, wbg, wbm, wout, gffn, wrt_pad, brt_pad, n_p):
    n = x.shape[0]
    tm = PROJ_TM
    d = D_MODEL
    np_tiles = n_p // tm
    row = lambda i: (i, 0)
    const = lambda i: (0, 0)
    first = lambda i: (jnp.minimum(i, np_tiles - 1), 0)
    second = lambda i: (jnp.maximum(i - np_tiles, 0), 0)

    def pair(width):
        return [pl.BlockSpec((tm, width), first), pl.BlockSpec((tm, width), second)]

    return pl.pallas_call(
        functools.partial(_merge_body, np_tiles=np_tiles),
        grid=(n // tm,),
        in_specs=[pl.BlockSpec((tm, d), row)] + pair(RWKV_WIDTH) + pair(GMLP_WIDTH) + pair(MEM_WIDTH) + [
            pl.BlockSpec((tm, 3 * d), row),
            pl.BlockSpec((RWKV_WIDTH, d), const),
            pl.BlockSpec((GMLP_WIDTH, d), const),
            pl.BlockSpec((MEM_WIDTH, d), const),
            pl.BlockSpec((d, d), const),
            pl.BlockSpec((1, d), const),
            pl.BlockSpec((d, LANES), const),
            pl.BlockSpec((1, LANES), const),
        ],
        out_specs=[
            pl.BlockSpec((tm, d), row),
            pl.BlockSpec((tm, d), row),
            pl.BlockSpec((tm, LANES), row),
            pl.BlockSpec((8, LANES), const),
        ],
        out_shape=[
            jax.ShapeDtypeStruct((n, d), F32),
            jax.ShapeDtypeStruct((n, d), F32),
            jax.ShapeDtypeStruct((n, LANES), F32),
            jax.ShapeDtypeStruct((8, LANES), F32),
        ],
        scratch_shapes=[pltpu.VMEM((1, LANES), F32)],
        compiler_params=_params(("arbitrary",)),
        name="merge",
    )(x, o_rw[0], o_rw[1], o_gm[0], o_gm[1], o_me[0], o_me[1], sg, wbr, wbg, wbm, wout, gffn, wrt_pad, brt_pad)


N_ZERO_BLOCKS = 2 * N_EXPERTS


def _dispatch_body(dest_ref, zlist_ref, h_ref, xs_hbm, zero_s, sem_z, sem):
    i = pl.program_id(0)
    tm = DISPATCH_TM
    bm = MOE_BM

    @pl.when(i == 0)
    def _():
        zero_s[...] = jnp.zeros_like(zero_s)

        def zero_copy(q):
            start = pl.multiple_of(zlist_ref[q] * bm, bm)
            return pltpu.make_async_copy(zero_s, xs_hbm.at[pl.ds(start, bm)], sem_z)

        def start(q, carry):
            @pl.when(zlist_ref[q] >= 0)
            def _():
                zero_copy(q).start()
            return carry

        def wait(q, carry):
            @pl.when(zlist_ref[q] >= 0)
            def _():
                zero_copy(q).wait()
            return carry

        lax.fori_loop(0, N_ZERO_BLOCKS, start, 0)
        lax.fori_loop(0, N_ZERO_BLOCKS, wait, 0)

    base = i * (tm * TOP_K)

    def body(r, carry):
        for kk in range(TOP_K):
            slot = dest_ref[base + r * TOP_K + kk]
            pltpu.make_async_copy(h_ref.at[pl.ds(r, 1)], xs_hbm.at[pl.ds(slot, 1)], sem).start()
        return carry

    lax.fori_loop(0, tm, body, 0, unroll=2)
    for kk in range(TOP_K):
        pltpu.make_async_copy(h_ref, xs_hbm.at[pl.ds(0, tm)], sem).wait()


def _dispatch(dest, zlist, h, n_blocks):
    n, d = h.shape
    tm = DISPATCH_TM
    grid_spec = pltpu.PrefetchScalarGridSpec(
        num_scalar_prefetch=2,
        grid=(n // tm,),
        in_specs=[pl.BlockSpec((tm, d), lambda i, dest, zl: (i, 0))],
        out_specs=pl.BlockSpec(memory_space=pl.ANY),
        scratch_shapes=[
            pltpu.VMEM((MOE_BM, d), F32),
            pltpu.SemaphoreType.DMA(()),
            pltpu.SemaphoreType.DMA(()),
        ],
    )
    return pl.pallas_call(
        _dispatch_body,
        grid_spec=grid_spec,
        out_shape=jax.ShapeDtypeStruct((n_blocks * MOE_BM, d), F32),
        compiler_params=_params(("arbitrary",)),
        name="moe_dispatch",
    )(dest, zlist, h)


CAST_ROWS = 64


def _moe_body(be_ref, nused_ref, x_ref, w1_ref, b1_ref, w2_ref, b2_ref, o_ref, w1b, w2b):
    j = pl.program_id(0)
    n_used = nused_ref[0]

    @pl.when(j < n_used)
    def _():
        changed = jnp.logical_or(j == 0, be_ref[j] != be_ref[jnp.maximum(j - 1, 0)])

        @pl.when(changed)
        def _():
            def cast1(q, carry):
                rows = pl.ds(pl.multiple_of(q * CAST_ROWS, CAST_ROWS), CAST_ROWS)
                w1b[rows, :] = w1_ref[0, rows, :].astype(BF16)
                return carry

            def cast2(q, carry):
                rows = pl.ds(pl.multiple_of(q * CAST_ROWS, CAST_ROWS), CAST_ROWS)
                w2b[rows, :] = w2_ref[0, rows, :].astype(BF16)
                return carry

            lax.fori_loop(0, D_MODEL // CAST_ROWS, cast1, 0)
            lax.fori_loop(0, D_FF // CAST_ROWS, cast2, 0)

        x = x_ref[...].astype(BF16)
        z = jnp.dot(x, w1b[...], preferred_element_type=F32) + b1_ref[0]
        zg = jnp.minimum(z[:, 0:D_FF], SWIGLU_LIMIT)
        zl = jnp.clip(z[:, D_FF:2 * D_FF], -SWIGLU_LIMIT, SWIGLU_LIMIT)
        act = zg * _sigmoid(SWIGLU_ALPHA * zg) * (zl + 1.0)
        o_ref[...] = jnp.dot(act.astype(BF16), w2b[...], preferred_element_type=F32) + b2_ref[0]

    @pl.when(j >= n_used)
    def _():
        o_ref[...] = jnp.zeros_like(o_ref)


def _moe(blk_e, n_used, xs, w1, b1, w2, b2, n_blocks):
    bm = MOE_BM
    d = D_MODEL
    grid_spec = pltpu.PrefetchScalarGridSpec(
        num_scalar_prefetch=2,
        grid=(n_blocks,),
        in_specs=[
            pl.BlockSpec((bm, d), lambda j, be, nu: (j, 0)),
            pl.BlockSpec((1, d, 2 * D_FF), lambda j, be, nu: (be[j], 0, 0)),
            pl.BlockSpec((1, 1, 2 * D_FF), lambda j, be, nu: (be[j], 0, 0)),
            pl.BlockSpec((1, D_FF, d), lambda j, be, nu: (be[j], 0, 0)),
            pl.BlockSpec((1, 1, d), lambda j, be, nu: (be[j], 0, 0)),
        ],
        out_specs=pl.BlockSpec((bm, d), lambda j, be, nu: (j, 0)),
        scratch_shapes=[
            pltpu.VMEM((d, 2 * D_FF), BF16),
            pltpu.VMEM((D_FF, d), BF16),
        ],
    )
    return pl.pallas_call(
        _moe_body,
        grid_spec=grid_spec,
        out_shape=jax.ShapeDtypeStruct((n_blocks * bm, d), F32),
        compiler_params=_params(("arbitrary",)),
        name="moe_ffn",
    )(blk_e, n_used, xs, w1, b1, w2, b2)


def _combine_body(dest_ref, x2_ref, route_ref, gfin_ref, yb_hbm, op_ref, os_ref, buf, sem, *, np_tiles):
    i = pl.program_id(0)
    nb = pl.num_programs(0)
    tm = COMBINE_TM

    def issue(blk, slot):
        base = blk * (tm * TOP_K)

        def body(r, carry):
            for kk in range(TOP_K):
                d = dest_ref[base + r * TOP_K + kk]
                pltpu.make_async_copy(yb_hbm.at[pl.ds(d, 1)], buf.at[slot, kk, pl.ds(r, 1)], sem.at[slot]).start()
            return carry

        lax.fori_loop(0, tm, body, 0, unroll=2)

    slot = lax.rem(i, 2)

    @pl.when(i == 0)
    def _():
        issue(0, 0)

    @pl.when(i + 1 < nb)
    def _():
        issue(i + 1, 1 - slot)

    acc = x2_ref[...]
    for kk in range(TOP_K):
        pltpu.make_async_copy(yb_hbm.at[pl.ds(0, tm)], buf.at[slot, kk], sem.at[slot]).wait()
    for kk in range(TOP_K):
        acc = acc + route_ref[:, ROUTE_GATE + kk:ROUTE_GATE + kk + 1] * buf[slot, kk]
    y = _rms(acc, gfin_ref[...])

    @pl.when(i < np_tiles)
    def _():
        op_ref[...] = y

    @pl.when(i >= np_tiles)
    def _():
        os_ref[...] = y


def _combine(dest, x2, route, gfin, yb, n_p):
    n, d = x2.shape
    tm = COMBINE_TM
    np_tiles = n_p // tm
    grid_spec = pltpu.PrefetchScalarGridSpec(
        num_scalar_prefetch=1,
        grid=(n // tm,),
        in_specs=[
            pl.BlockSpec((tm, d), lambda i, dest: (i, 0)),
            pl.BlockSpec((tm, LANES), lambda i, dest: (i, 0)),
            pl.BlockSpec((1, d), lambda i, dest: (0, 0)),
            pl.BlockSpec(memory_space=pl.ANY),
        ],
        out_specs=[
            pl.BlockSpec((tm, d), lambda i, dest: (jnp.minimum(i, np_tiles - 1), 0)),
            pl.BlockSpec((tm, d), lambda i, dest: (jnp.maximum(i - np_tiles, 0), 0)),
        ],
        scratch_shapes=[
            pltpu.VMEM((2, TOP_K, tm, d), F32),
            pltpu.SemaphoreType.DMA((2,)),
        ],
    )
    return pl.pallas_call(
        functools.partial(_combine_body, np_tiles=np_tiles),
        grid_spec=grid_spec,
        out_shape=[jax.ShapeDtypeStruct((n_p, d), F32), jax.ShapeDtypeStruct((n - n_p, d), F32)],
        compiler_params=_params(("arbitrary",)),
        name="moe_combine",
    )(dest, x2, route, gfin, yb)


def _slot_tables(route, counts_row, n_blocks):
    bm = MOE_BM
    e = route[:, ROUTE_E:ROUTE_E + TOP_K].astype(jnp.int32)
    rank = route[:, ROUTE_RANK:ROUTE_RANK + TOP_K].astype(jnp.int32)
    counts = counts_row[:N_EXPERTS].astype(jnp.int32)
    padded = (counts + bm - 1) // bm * bm
    pad_end = jnp.cumsum(padded)
    pad_start = pad_end - padded
    dest = (jnp.take(pad_start, e) + rank).reshape(-1)
    blk_start = jnp.arange(n_blocks, dtype=jnp.int32) * bm
    blk_e = jnp.minimum(jnp.sum(pad_end[None, :] <= blk_start[:, None], axis=1), N_EXPERTS - 1).astype(jnp.int32)
    n_used = pad_end[-1] // bm
    last_blk = jnp.where(padded > 0, pad_end // bm - 1, -1)
    trailing = n_used + jnp.arange(N_EXPERTS, dtype=jnp.int32)
    trailing = jnp.where(trailing < n_blocks, trailing, -1)
    zlist = jnp.concatenate([last_blk, trailing]).astype(jnp.int32)
    return dest.astype(jnp.int32), blk_e, n_used.astype(jnp.int32).reshape(1), zlist


def kernel(x_prompt, x_sample, mem_prompt, state_shift, state_wkv, cache_mem_k, cache_mem_v, g_norm_mix, w_in, mu_shift, w0, w_decay_up, a0, w_a_up, w_g_up, k_k, k_a, r_k, ln_x_g, ln_x_b, gmlp_ln_g, gmlp_ln_b, w_spatial, b_spatial, g_norm_mem, w_mem_kv, w_br_rwkv, w_br_gmlp, w_br_mem, w_out, g_norm_ffn, w_router, b_router, w_exp1, b_exp1, w_exp2, b_exp2, g_norm_final):
    bp, tp, d = x_prompt.shape
    bs, ts, _ = x_sample.shape
    n_p, n_s = bp * tp, bs * ts
    n_all = n_p + n_s
    l = 0
    row = lambda a: a.reshape(1, -1)

    x_all = jnp.concatenate([x_prompt.reshape(n_p, d), x_sample.reshape(n_s, d)], axis=0)
    zr, zg, zq, sg = _in_proj(x_all, row(g_norm_mix[l]), w_in[l].astype(BF16))

    mk_p, mv_p = _mem_kv(mem_prompt.reshape(bp * N_MEM, d), row(g_norm_mem[l]), w_mem_kv[l].astype(BF16))

    rp = dict(mu=row(mu_shift[l]), w0=row(w0[l]), wd=w_decay_up[l].astype(BF16), a0=row(a0[l]),
              wa=w_a_up[l].astype(BF16), wg=w_g_up[l].astype(BF16), kk=row(k_k[l]), ka=row(k_a[l]),
              rk=row(r_k[l]), lng=row(ln_x_g[l]), lnb=row(ln_x_b[l]))
    head_of = jnp.arange(RWKV_WIDTH, dtype=jnp.int32) // HEAD_DIM
    rp["hsum"] = (head_of[:, None] == head_of[None, :]).astype(BF16)
    o_rw_p, s_p = _rwkv(zr.reshape(n_all // RWKV_TT, RWKV_TT, SHIFT_WIDTH), 0, bp, tp // RWKV_TT,
                        jnp.zeros((bp, 1, SHIFT_WIDTH), F32), jnp.zeros((bp, RWKV_HEADS, HEAD_DIM, HEAD_DIM), F32),
                        rp, tt=RWKV_TT, c=RWKV_C, nseq=1)
    tile_s = RWKV_SAMPLE_SEQS * ts
    o_rw_s, s_s = _rwkv(zr.reshape(n_all // tile_s, tile_s, SHIFT_WIDTH), n_p // tile_s, bs, 1,
                        state_shift[l].reshape(bs, 1, SHIFT_WIDTH), state_wkv[l], rp, tt=tile_s, c=ts,
                        nseq=RWKV_SAMPLE_SEQS)

    tri = jnp.tril(jnp.ones((CHUNK, CHUNK), bool))
    ws_p = jnp.where(tri, w_spatial[l], 0.0).astype(BF16)
    bs_p = b_spatial[l].T
    reps = CHUNK // ts
    tri_s = jnp.tril(jnp.ones((ts, ts), bool))
    ws_small = jnp.where(tri_s, w_spatial[l][:, :ts, :ts], 0.0)
    eye = jnp.eye(reps, dtype=F32)
    ws_s = jnp.einsum("ab,gij->gaibj", eye, ws_small).reshape(GMLP_GROUPS, CHUNK, CHUNK).astype(BF16)
    bs_s = jnp.tile(b_spatial[l][:, :ts], (1, reps)).T
    lng, lnb = row(gmlp_ln_g[l]), row(gmlp_ln_b[l])
    o_gm_p, _ = _gmlp(zg, 0, n_p, lng, lnb, ws_p, bs_p)
    o_gm_s, v_rows_s = _gmlp(zg, n_p // CHUNK, n_s, lng, lnb, ws_s, bs_s)

    o_me_p = _mem_attn(zq.reshape(n_all // ATTN_TT, ATTN_TT, MEM_WIDTH), 0, bp, tp // ATTN_TT,
                       mk_p.reshape(bp, N_MEM, MEM_WIDTH), mv_p.reshape(bp, N_MEM, MEM_WIDTH), tt=ATTN_TT, bb=1)
    o_me_s = _mem_attn(zq.reshape(n_all // ts, ts, MEM_WIDTH), n_p // ts, bs, 1,
                       cache_mem_k[l].reshape(bs, N_MEM, MEM_WIDTH), cache_mem_v[l].reshape(bs, N_MEM, MEM_WIDTH),
                       tt=ts, bb=ATTN_BB)

    wrt_pad = jnp.zeros((d, LANES), F32).at[:, :N_EXPERTS].set(w_router[l])
    brt_pad = jnp.zeros((1, LANES), F32).at[0, :N_EXPERTS].set(b_router[l])
    x2, h2, route, counts = _merge(
        x_all, (o_rw_p.reshape(n_p, RWKV_WIDTH), o_rw_s.reshape(n_s, RWKV_WIDTH)), (o_gm_p, o_gm_s),
        (o_me_p.reshape(n_p, MEM_WIDTH), o_me_s.reshape(n_s, MEM_WIDTH)), sg,
        w_br_rwkv[l].astype(BF16), w_br_gmlp[l].astype(BF16), w_br_mem[l].astype(BF16), w_out[l].astype(BF16),
        row(g_norm_ffn[l]), wrt_pad, brt_pad, n_p)

    n_assign = n_all * TOP_K
    n_blocks = -(-(n_assign + N_EXPERTS * (MOE_BM - 1)) // MOE_BM)
    dest, blk_e, n_used, zlist = _slot_tables(route, counts[0], n_blocks)
    xs = _dispatch(dest, zlist, h2, n_blocks)
    yb = _moe(blk_e, n_used, xs, w_exp1[l], b_exp1[l].reshape(N_EXPERTS, 1, 2 * D_FF),
              w_exp2[l], b_exp2[l].reshape(N_EXPERTS, 1, d), n_blocks)
    y_p, y_s = _combine(dest, x2, route, row(g_norm_final), yb, n_p)

    shift_p = zr[tp - 1:n_p:tp][None]
    shift_s = zr[n_p + ts - 1::ts][None]
    mk_out = mk_p.reshape(1, bp, N_MEM, MEM_HEADS, MEM_WIDTH // MEM_HEADS)
    mv_out = mv_p.reshape(1, bp, N_MEM, MEM_HEADS, MEM_WIDTH // MEM_HEADS)
    return (y_p.reshape(bp, tp, d), y_s.reshape(bs, ts, d), shift_p, s_p[None], mk_out, mv_out, shift_s, s_s[None],
            v_rows_s.reshape(1, bs, ts, GMLP_WIDTH))
```

```python
import functools
import math

import jax
import jax.numpy as jnp
from jax import lax
from jax.experimental import pallas as pl
from jax.experimental.pallas import tpu as pltpu

F32 = jnp.float32
BF16 = jnp.bfloat16

D_MODEL = 1024
RWKV_HEADS = 8
HEAD_DIM = 64
RWKV_WIDTH = RWKV_HEADS * HEAD_DIM
DECAY_LORA = 64
A_LORA = 64
GATE_LORA = 128
GMLP_GROUPS = 4
GMLP_WIDTH = 256
CHUNK = 128
MEM_HEADS = 4
MEM_WIDTH = 256
N_MEM = 256
N_EXPERTS = 32
TOP_K = 4
D_FF = 1024
SWIGLU_ALPHA = 1.702
SWIGLU_LIMIT = 7.0
RMS_EPS = 1e-5
LN_EPS = 1e-5
GN_EPS = 64e-5
SHIFT_WIDTH = 3 * RWKV_WIDTH + DECAY_LORA + A_LORA + GATE_LORA
OFF_GMLP = SHIFT_WIDTH
OFF_QMEM = OFF_GMLP + 2 * GMLP_WIDTH
OFF_GATE = OFF_QMEM + MEM_WIDTH
IN_WIDTH = OFF_GATE + 3 * D_MODEL
LANES = 128

PROJ_TM = 256
RWKV_TT = 512
RWKV_C = 64
RWKV_SAMPLE_SEQS = 8
ATTN_TT = 512
ATTN_BB = 8
MOE_BM = 512
DISPATCH_TM = 256
COMBINE_TM = 128

ROUTE_E = 0
ROUTE_RANK = 4
ROUTE_GATE = 8

NN = ((1,), (0,))
NT = ((1,), (1,))
TN = ((0,), (0,))

VMEM_LIMIT = 56 * 1024 * 1024


def _params(sem, vmem=VMEM_LIMIT):
    return pltpu.CompilerParams(dimension_semantics=sem, vmem_limit_bytes=vmem)


def _bdot(a, b, dims=NN):
    return lax.dot_general(a.astype(BF16), b.astype(BF16), (dims, ((), ())), preferred_element_type=F32)


def _split(x):
    hi = x.astype(BF16)
    lo = (x - hi.astype(F32)).astype(BF16)
    return hi, lo


def _dot3(a, b, dims=NN):
    dn = (dims, ((), ()))
    ah, al = _split(a)
    bh, bl = _split(b)
    r = lax.dot_general(ah, bh, dn, preferred_element_type=F32)
    r = r + lax.dot_general(al, bh, dn, preferred_element_type=F32)
    return r + lax.dot_general(ah, bl, dn, preferred_element_type=F32)


def _rms(x, g):
    return x * lax.rsqrt(jnp.mean(x * x, axis=-1, keepdims=True) + RMS_EPS) * g


def _sigmoid(x):
    return 0.5 * jnp.tanh(0.5 * x) + 0.5


def _in_proj_body(xp_ref, xs_ref, g_ref, w_ref, zr_ref, zg_ref, zq_ref, sg_ref, *, np_tiles):
    x = jnp.where(pl.program_id(0) < np_tiles, xp_ref[...], xs_ref[...])
    h = _rms(x, g_ref[...]).astype(BF16)
    zr_ref[...] = jnp.dot(h, w_ref[:, 0:OFF_GMLP], preferred_element_type=F32)
    zg_ref[...] = jnp.dot(h, w_ref[:, OFF_GMLP:OFF_QMEM], preferred_element_type=F32)
    zq_ref[...] = jnp.dot(h, w_ref[:, OFF_QMEM:OFF_GATE], preferred_element_type=F32)
    gates = jnp.dot(h, w_ref[:, OFF_GATE:IN_WIDTH], preferred_element_type=F32)
    sg_ref[...] = _sigmoid(gates).astype(BF16)


def _in_proj(x_p, x_s, g, w_bf16):
    n_p = x_p.shape[0]
    n = n_p + x_s.shape[0]
    tm = PROJ_TM
    np_tiles = n_p // tm
    row = lambda i: (i, 0)
    const = lambda i: (0, 0)
    return pl.pallas_call(
        functools.partial(_in_proj_body, np_tiles=np_tiles),
        grid=(n // tm,),
        in_specs=[
            pl.BlockSpec((tm, D_MODEL), lambda i: (jnp.minimum(i, np_tiles - 1), 0)),
            pl.BlockSpec((tm, D_MODEL), lambda i: (jnp.maximum(i - np_tiles, 0), 0)),
            pl.BlockSpec((1, D_MODEL), const),
            pl.BlockSpec((D_MODEL, IN_WIDTH), const, pipeline_mode=pl.Buffered(1)),
        ],
        out_specs=[
            pl.BlockSpec((tm, SHIFT_WIDTH), row),
            pl.BlockSpec((tm, 2 * GMLP_WIDTH), row),
            pl.BlockSpec((tm, MEM_WIDTH), row),
            pl.BlockSpec((tm, 3 * D_MODEL), row),
        ],
        out_shape=[
            jax.ShapeDtypeStruct((n, SHIFT_WIDTH), F32),
            jax.ShapeDtypeStruct((n, 2 * GMLP_WIDTH), F32),
            jax.ShapeDtypeStruct((n, MEM_WIDTH), F32),
            jax.ShapeDtypeStruct((n, 3 * D_MODEL), BF16),
        ],
        compiler_params=_params(("parallel",)),
        name="in_proj",
    )(x_p, x_s, g, w_bf16)


def _mem_kv_body(x_ref, g_ref, w_ref, k_ref, v_ref):
    h = _rms(x_ref[...], g_ref[...]).astype(BF16)
    k_ref[...] = jnp.dot(h, w_ref[:, 0:MEM_WIDTH], preferred_element_type=F32)
    v_ref[...] = jnp.dot(h, w_ref[:, MEM_WIDTH:2 * MEM_WIDTH], preferred_element_type=F32)


def _mem_kv(mem, g, w_bf16):
    n = mem.shape[0]
    tm = PROJ_TM
    return pl.pallas_call(
        _mem_kv_body,
        grid=(n // tm,),
        in_specs=[
            pl.BlockSpec((tm, D_MODEL), lambda i: (i, 0)),
            pl.BlockSpec((1, D_MODEL), lambda i: (0, 0)),
            pl.BlockSpec((D_MODEL, 2 * MEM_WIDTH), lambda i: (0, 0)),
        ],
        out_specs=[pl.BlockSpec((tm, MEM_WIDTH), lambda i: (i, 0))] * 2,
        out_shape=[jax.ShapeDtypeStruct((n, MEM_WIDTH), F32)] * 2,
        compiler_params=_params(("parallel",)),
        name="mem_kv",
    )(mem, g, w_bf16)


EXP_M05 = math.exp(-0.5)


def _dotp(a, b, dims, passes):
    return _dot3(a, b, dims) if passes == 3 else _bdot(a, b, dims)


PREC = dict(a=1, t=1, lkv=1, tw=1, am=1, gh=3, y=1, s=3)


def _unit_lower_inverse(lows, c):
    rows = lax.broadcasted_iota(jnp.int32, (c, c), 0)
    cols = lax.broadcasted_iota(jnp.int32, (c, c), 1)
    eye = (rows == cols).astype(F32)
    invs = [eye - low for low in lows]
    powers = lows
    for _ in range(int(math.log2(c)) - 1):
        powers = [_dotp(pw, pw, NN, PREC["t"]) for pw in powers]
        invs = [inv + _dotp(inv, pw, NN, PREC["t"]) for inv, pw in zip(invs, powers)]
    return invs


def _rwkv_body(zr_ref, shift_ref, s0_ref, mu_ref, w0_ref, wd_ref, a0_ref, wa_ref, wg_ref, kk_ref, ka_ref,
               rk_ref, lng_ref, lnb_ref, hsum_ref,
               o_ref, sout_ref, shout_ref,
               carry_ref, state_ref, r_s, k_s, v_s, kk_s, b_s, ld_s, y_s, bon_s, g_s,
               rw_s, y0_s, gm_s, h0_s, *, tt, c, nseq):
    i = pl.program_id(1)
    rw = RWKV_WIDTH
    nh = RWKV_HEADS

    @pl.when(i == 0)
    def _():
        for q in range(nseq):
            state_ref[q * nh:(q + 1) * nh] = s0_ref[q]

    z = zr_ref[0]
    z_prev = pltpu.roll(z, 1, 0)
    row = lax.broadcasted_iota(jnp.int32, z.shape, 0)
    if nseq == 1:
        @pl.when(i == 0)
        def _():
            carry_ref[...] = shift_ref[0]

        z_prev = jnp.where(row == 0, carry_ref[...], z_prev)
        carry_ref[...] = z[tt - 1:tt, :]
    else:
        first_rows = jnp.concatenate([jnp.broadcast_to(shift_ref[q], (c, SHIFT_WIDTH)) for q in range(nseq)], axis=0)
        z_prev = jnp.where(row % c == 0, first_rows, z_prev)
    zs = z + mu_ref[...] * (z_prev - z)
    r = zs[:, 0:rw]
    k = zs[:, rw:2 * rw]
    v = zs[:, 2 * rw:3 * rw]
    zw = zs[:, 3 * rw:3 * rw + DECAY_LORA]
    za = zs[:, 3 * rw + DECAY_LORA:3 * rw + DECAY_LORA + A_LORA]
    zg = zs[:, 3 * rw + DECAY_LORA + A_LORA:SHIFT_WIDTH]
    xw = w0_ref[...] + _bdot(jnp.tanh(zw), wd_ref[...])
    ld_s[...] = -EXP_M05 * _sigmoid(xw)
    a = _sigmoid(a0_ref[...] + _bdot(za, wa_ref[...]))
    g_s[...] = _bdot(_sigmoid(zg), wg_ref[...])
    kk = k * kk_ref[...]
    k = k * (1.0 + (a - 1.0) * ka_ref[...])
    r_s[...] = r
    k_s[...] = k
    v_s[...] = v

    def head_sum(t):
        return jnp.dot(t.astype(BF16), hsum_ref[...], preferred_element_type=F32)

    kk = kk / jnp.maximum(jnp.sqrt(head_sum(kk * kk)), 1e-12)
    kk_s[...] = kk
    b_s[...] = kk * a
    bon_s[...] = head_sum(r * k * rk_ref[...]) * v

    rows_c = lax.broadcasted_iota(jnp.int32, (c, c), 0)
    cols_c = lax.broadcasted_iota(jnp.int32, (c, c), 1)
    strict = rows_c > cols_c
    incl = rows_c >= cols_c
    tril_ones = incl.astype(BF16)
    rows_2c = lax.broadcasted_iota(jnp.int32, (c, 2 * c), 0)
    cols_2c = lax.broadcasted_iota(jnp.int32, (c, 2 * c), 1)
    incl2 = rows_2c >= jnp.where(cols_2c >= c, cols_2c - c, cols_2c)
    rows_k = lax.broadcasted_iota(jnp.int32, (HEAD_DIM, HEAD_DIM), 0)
    cols_k = lax.broadcasted_iota(jnp.int32, (HEAD_DIM, HEAD_DIM), 1)
    eye_k = (rows_k == cols_k).astype(F32)
    zeros_cv = jnp.zeros((c, HEAD_DIM), F32)
    heads = [slice(h * HEAD_DIM, (h + 1) * HEAD_DIM) for h in range(RWKV_HEADS)]

    n_chunks = tt // c
    group = nseq if nseq > 1 else (2 if n_chunks % 2 == 0 else 1)

    def chunk_rows(ci):
        return pl.ds(ci * c, c) if isinstance(ci, int) else pl.ds(pl.multiple_of(ci * c, c), c)

    def scaled(ci):
        rows = chunk_rows(ci)
        ld = ld_s[rows, :]
        ld_hi, ld_lo = _split(ld)
        cum = (jnp.dot(tril_ones, ld_hi, preferred_element_type=F32)
               + jnp.dot(tril_ones, ld_lo, preferred_element_type=F32))
        e_inc = jnp.exp(cum)
        e_neg = jnp.exp(-cum)
        kt = k_s[rows, :] * e_neg
        bt = b_s[rows, :] * e_neg
        g_end = e_inc[c - 1:c, :]
        return dict(ci=ci, rt=r_s[rows, :] * e_inc, kkt=kk_s[rows, :] * jnp.exp(cum - ld), kt=kt, bt=bt,
                    g_end=g_end, bc=bt * g_end, kc=kt * g_end, vv=v_s[rows, :])

    def phase1(cj, carry):
        chains = [(ch, h, s) for ch in [scaled(cj * group + g) for g in range(group)]
                  for h, s in enumerate(heads)]
        x = [jnp.concatenate([ch["kkt"][:, s], ch["rt"][:, s]], axis=0) for ch, _, s in chains]
        zz = [jnp.concatenate([ch["bt"][:, s], ch["kt"][:, s]], axis=0) for ch, _, s in chains]
        amat = [_dotp(xh, zh, NT, PREC["a"]) for xh, zh in zip(x, zz)]
        l_b = [jnp.where(strict, am[0:c, 0:c], 0.0) for am in amat]
        l_k = [jnp.where(strict, am[0:c, c:2 * c], 0.0) for am in amat]
        a_r = [jnp.where(incl2, am[c:2 * c, :], 0.0) for am in amat]
        tinv = _unit_lower_inverse(l_b, c)
        lkv = [_dotp(lk, ch["vv"][:, s], NN, PREC["lkv"]) for lk, (ch, _, s) in zip(l_k, chains)]
        wu = [-_dotp(t, jnp.concatenate([ch["kkt"][:, s], lv], axis=1), NN, PREC["tw"])
              for t, lv, (ch, _, s) in zip(tinv, lkv, chains)]
        m = [jnp.concatenate([w, jnp.concatenate([zeros_cv, ch["vv"][:, s]], axis=1)], axis=0)
             for w, (ch, _, s) in zip(wu, chains)]
        am2 = [_dotp(ar, mh, NN, PREC["am"]) for ar, mh in zip(a_r, m)]
        gh = [_dotp(mh, jnp.concatenate([ch["bc"][:, s], ch["kc"][:, s]], axis=0), TN, PREC["gh"])
              for mh, (ch, _, s) in zip(m, chains)]
        for q, (ch, h, s) in enumerate(chains):
            idx = ch["ci"] * RWKV_HEADS + h
            rw_s[idx] = ch["rt"][:, s] + am2[q][:, 0:HEAD_DIM]
            y0_s[idx] = am2[q][:, HEAD_DIM:2 * HEAD_DIM]
            gm_s[idx] = gh[q][0:HEAD_DIM, :] + eye_k * ch["g_end"][:, s]
            h0_s[idx] = gh[q][HEAD_DIM:2 * HEAD_DIM, :]
        return carry

    if n_chunks == group:
        phase1(0, 0)
    else:
        lax.fori_loop(0, n_chunks // group, phase1, 0)

    if nseq == 1:
        def phase2(ci, carry):
            rows = chunk_rows(ci)
            for h, s in enumerate(heads):
                idx = ci * nh + h
                s_h = state_ref[h]
                y_s[rows, s] = _dotp(rw_s[idx], s_h, NT, PREC["y"]) + y0_s[idx]
                state_ref[h] = _dotp(s_h, gm_s[idx], NN, PREC["s"]) + h0_s[idx]
            return carry

        lax.fori_loop(0, n_chunks, phase2, 0)
    else:
        pairs = [(q, h, s) for q in range(nseq) for h, s in enumerate(heads)]
        states = [state_ref[q * nh + h] for q, h, _ in pairs]
        ys = [_dotp(rw_s[q * nh + h], st, NT, PREC["y"]) + y0_s[q * nh + h] for (q, h, _), st in zip(pairs, states)]
        new = [_dotp(st, gm_s[q * nh + h], NN, PREC["s"]) + h0_s[q * nh + h] for (q, h, _), st in zip(pairs, states)]
        for (q, h, s), yq, nq in zip(pairs, ys, new):
            y_s[chunk_rows(q), s] = yq
            state_ref[q * nh + h] = nq
    y = y_s[...]
    yc = y - head_sum(y) * (1.0 / HEAD_DIM)
    var = head_sum(yc * yc) * (1.0 / HEAD_DIM)
    yn = yc * lax.rsqrt(var + GN_EPS)
    o_ref[0] = (yn * lng_ref[...] + lnb_ref[...] + bon_s[...]) * g_s[...]

    @pl.when(i == pl.num_programs(1) - 1)
    def _():
        for q in range(nseq):
            sout_ref[q] = state_ref[q * nh:(q + 1) * nh]
            last = tt - 1 if nseq == 1 else (q + 1) * c - 1
            shout_ref[q] = zr_ref[0, last:last + 1, :]


def _rwkv(zr3, blk_off, b, nt, shift_prev, s_prev, p, tt, c, nseq):
    vec = lambda n: pl.BlockSpec((1, n), lambda bi, ti: (0, 0))
    mat = lambda m, n: pl.BlockSpec((m, n), lambda bi, ti: (0, 0))
    rw = RWKV_WIDTH
    scr = lambda: pltpu.VMEM((tt, rw), F32)
    nch = (tt // c) * RWKV_HEADS
    return pl.pallas_call(
        functools.partial(_rwkv_body, tt=tt, c=c, nseq=nseq),
        grid=(b // nseq, nt),
        in_specs=[
            pl.BlockSpec((1, tt, SHIFT_WIDTH), lambda bi, ti: (blk_off + bi * nt + ti, 0, 0)),
            pl.BlockSpec((nseq, 1, SHIFT_WIDTH), lambda bi, ti: (bi, 0, 0)),
            pl.BlockSpec((nseq, RWKV_HEADS, HEAD_DIM, HEAD_DIM), lambda bi, ti: (bi, 0, 0, 0)),
            vec(SHIFT_WIDTH), vec(rw), mat(DECAY_LORA, rw), vec(rw), mat(A_LORA, rw), mat(GATE_LORA, rw),
            vec(rw), vec(rw), vec(rw), vec(rw), vec(rw), mat(rw, rw),
        ],
        out_specs=[
            pl.BlockSpec((1, tt, rw), lambda bi, ti: (bi, ti, 0)),
            pl.BlockSpec((nseq, RWKV_HEADS, HEAD_DIM, HEAD_DIM), lambda bi, ti: (bi, 0, 0, 0)),
            pl.BlockSpec((nseq, 1, SHIFT_WIDTH), lambda bi, ti: (bi, 0, 0)),
        ],
        out_shape=[
            jax.ShapeDtypeStruct((b // nseq, nt * tt, rw), F32),
            jax.ShapeDtypeStruct((b, RWKV_HEADS, HEAD_DIM, HEAD_DIM), F32),
            jax.ShapeDtypeStruct((b, 1, SHIFT_WIDTH), F32),
        ],
        scratch_shapes=[
            pltpu.VMEM((1, SHIFT_WIDTH), F32),
            pltpu.VMEM((nseq * RWKV_HEADS, HEAD_DIM, HEAD_DIM), F32),
            scr(), scr(), scr(), scr(), scr(), scr(), scr(), scr(), scr(),
            pltpu.VMEM((nch, c, HEAD_DIM), F32),
            pltpu.VMEM((nch, c, HEAD_DIM), F32),
            pltpu.VMEM((nch, HEAD_DIM, HEAD_DIM), F32),
            pltpu.VMEM((nch, HEAD_DIM, HEAD_DIM), F32),
        ],
        compiler_params=_params(("parallel", "arbitrary")),
        name="rwkv",
    )(zr3, shift_prev, s_prev, p["mu"], p["w0"], p["wd"], p["a0"], p["wa"], p["wg"], p["kk"], p["ka"],
      p["rk"], p["lng"], p["lnb"], p["hsum"])


def _gmlp_body(z_ref, lng_ref, lnb_ref, ws_ref, bs_ref, o_ref, v_ref):
    z = z_ref[...]
    ge = 0.5 * z * (1.0 + lax.erf(z * (1.0 / math.sqrt(2.0))))
    u = ge[:, 0:GMLP_WIDTH]
    v = ge[:, GMLP_WIDTH:2 * GMLP_WIDTH]
    mean = jnp.mean(v, axis=-1, keepdims=True)
    vc = v - mean
    var = jnp.mean(vc * vc, axis=-1, keepdims=True)
    vn = vc * lax.rsqrt(var + LN_EPS) * lng_ref[...] + lnb_ref[...]
    v_ref[...] = vn
    gd = GMLP_WIDTH // GMLP_GROUPS
    for g in range(GMLP_GROUPS):
        gs = slice(g * gd, (g + 1) * gd)
        sv = jnp.dot(ws_ref[g], vn[:, gs].astype(BF16), preferred_element_type=F32) + bs_ref[:, g:g + 1]
        o_ref[:, gs] = u[:, gs] * sv


def _gmlp(zg, blk_off, n_rows, lng, lnb, ws_bf16, bs):
    return pl.pallas_call(
        _gmlp_body,
        grid=(n_rows // CHUNK,),
        in_specs=[
            pl.BlockSpec((CHUNK, 2 * GMLP_WIDTH), lambda i: (blk_off + i, 0)),
            pl.BlockSpec((1, GMLP_WIDTH), lambda i: (0, 0)),
            pl.BlockSpec((1, GMLP_WIDTH), lambda i: (0, 0)),
            pl.BlockSpec((GMLP_GROUPS, CHUNK, CHUNK), lambda i: (0, 0, 0)),
            pl.BlockSpec((CHUNK, GMLP_GROUPS), lambda i: (0, 0)),
        ],
        out_specs=[
            pl.BlockSpec((CHUNK, GMLP_WIDTH), lambda i: (i, 0)),
            pl.BlockSpec((CHUNK, GMLP_WIDTH), lambda i: (i, 0)),
        ],
        out_shape=[
            jax.ShapeDtypeStruct((n_rows, GMLP_WIDTH), F32),
            jax.ShapeDtypeStruct((n_rows, GMLP_WIDTH), F32),
        ],
        compiler_params=_params(("parallel",)),
        name="gmlp",
    )(zg, lng, lnb, ws_bf16, bs)


def _mem_attn_body(q_ref, k_ref, v_ref, o_ref, *, bb, feature_major):
    hd = MEM_WIDTH // MEM_HEADS
    pairs = [(s, slice(h * hd, (h + 1) * hd)) for s in range(bb) for h in range(MEM_HEADS)]
    if feature_major:
        scores = [_bdot(q_ref[s, :, hs], k_ref[s, hs, :], NN) * (hd ** -0.5) for s, hs in pairs]
    else:
        scores = [_bdot(q_ref[s, :, hs], k_ref[s, :, hs], NT) * (hd ** -0.5) for s, hs in pairs]
    probs = []
    for sc in scores:
        pr = jnp.exp(sc - jnp.max(sc, axis=-1, keepdims=True))
        probs.append(pr / jnp.sum(pr, axis=-1, keepdims=True))
    for (s, hs), pr in zip(pairs, probs):
        if feature_major:
            o_ref[s, :, hs] = _bdot(pr, v_ref[s, hs, :], NT)
        else:
            o_ref[s, :, hs] = _bdot(pr, v_ref[s, :, hs])


def _mem_attn(q3, blk_off, b, nt, mk, mv, tt, bb, feature_major):
    return pl.pallas_call(
        functools.partial(_mem_attn_body, bb=bb, feature_major=feature_major),
        grid=(b // bb, nt),
        in_specs=[
            pl.BlockSpec((bb, tt, MEM_WIDTH), lambda bi, ti: (blk_off // bb + bi * nt + ti, 0, 0)),
            pl.BlockSpec((bb, N_MEM, MEM_WIDTH), lambda bi, ti: (bi, 0, 0)),
            pl.BlockSpec((bb, N_MEM, MEM_WIDTH), lambda bi, ti: (bi, 0, 0)),
        ],
        out_specs=pl.BlockSpec((bb, tt, MEM_WIDTH), lambda bi, ti: (bi, ti, 0)),
        out_shape=jax.ShapeDtypeStruct((b, nt * tt, MEM_WIDTH), F32),
        compiler_params=_params(("parallel", "parallel")),
        name="mem_attn",
    )(q3, mk, mv)


def _merge_body(x_p, x_s, orw_p, orw_s, ogm_p, ogm_s, ome_p, ome_s, sg_ref, wbr_ref, wbg_ref, wbm_ref, wout_ref,
                gffn_ref, wrt_ref, brt_ref, x2_ref, h2_ref, route_ref, route_t_ref, cnt_ref, count_s, *, np_tiles):
    i = pl.program_id(0)
    d = D_MODEL
    tm = x_p.shape[0]
    is_p = i < np_tiles
    x = jnp.where(is_p, x_p[...], x_s[...])
    orw = jnp.where(is_p, orw_p[...], orw_s[...])
    ogm = jnp.where(is_p, ogm_p[...], ogm_s[...])
    ome = jnp.where(is_p, ome_p[...], ome_s[...])
    merged = sg_ref[:, 0:d].astype(F32) * _bdot(orw, wbr_ref[...])
    merged = merged + sg_ref[:, d:2 * d].astype(F32) * _bdot(ogm, wbg_ref[...])
    merged = merged + sg_ref[:, 2 * d:3 * d].astype(F32) * _bdot(ome, wbm_ref[...])
    x2 = x + _bdot(merged, wout_ref[...])
    x2_ref[...] = x2
    h2 = _rms(x2, gffn_ref[...])
    h2_ref[...] = h2

    @pl.when(i == 0)
    def _():
        count_s[...] = jnp.zeros_like(count_s)

    lane = lax.broadcasted_iota(jnp.int32, (tm, LANES), 1)
    logits = jnp.where(lane < N_EXPERTS, _dot3(h2, wrt_ref[...]) + brt_ref[...], -jnp.inf)
    tops, hots, idxs = [], [], []
    for _ in range(TOP_K):
        top = jnp.max(logits, axis=-1, keepdims=True)
        idx = jnp.min(jnp.where(logits == top, lane, LANES), axis=-1, keepdims=True)
        hot = lane == idx
        logits = jnp.where(hot, -jnp.inf, logits)
        tops.append(top)
        hots.append(hot)
        idxs.append(idx)
    weights = [jnp.exp(t - tops[0]) for t in tops]
    denom = weights[0] + weights[1] + weights[2] + weights[3]
    onehot = jnp.zeros((tm, LANES), F32)
    for hot in hots:
        onehot = onehot + jnp.where(hot, 1.0, 0.0)
    rows_t = lax.broadcasted_iota(jnp.int32, (tm, tm), 0)
    cols_t = lax.broadcasted_iota(jnp.int32, (tm, tm), 1)
    ahead = jnp.where(rows_t > cols_t, 1.0, 0.0).astype(BF16)
    prefix = jnp.dot(ahead, onehot.astype(BF16), preferred_element_type=F32) + count_s[...]
    route = jnp.zeros((tm, LANES), F32)
    for kk in range(TOP_K):
        rank = jnp.sum(jnp.where(hots[kk], prefix, 0.0), axis=-1, keepdims=True)
        route = jnp.where(lane == ROUTE_E + kk, idxs[kk].astype(F32), route)
        route = jnp.where(lane == ROUTE_RANK + kk, rank, route)
        route = jnp.where(lane == ROUTE_GATE + kk, weights[kk] / denom, route)
    route_ref[...] = route
    route_t_ref[...] = route.T
    count_s[...] = count_s[...] + jnp.sum(onehot, axis=0, keepdims=True)
    cnt_ref[...] = jnp.broadcast_to(count_s[...], cnt_ref.shape)


def _merge(x, o_rw, o_gm, o_me, sg, wbr, wbg, wbm, wout, gffn, wrt_pad, brt_pad):
    n_p = x[0].shape[0]
    n = n_p + x[1].shape[0]
    tm = PROJ_TM
    d = D_MODEL
    np_tiles = n_p // tm
    row = lambda i: (i, 0)
    const = lambda i: (0, 0)
    first = lambda i: (jnp.minimum(i, np_tiles - 1), 0)
    second = lambda i: (jnp.maximum(i - np_tiles, 0), 0)

    def pair(width):
        return [pl.BlockSpec((tm, width), first), pl.BlockSpec((tm, width), second)]

    return pl.pallas_call(
        functools.partial(_merge_body, np_tiles=np_tiles),
        grid=(n // tm,),
        in_specs=pair(d) + pair(RWKV_WIDTH) + pair(GMLP_WIDTH) + pair(MEM_WIDTH) + [
            pl.BlockSpec((tm, 3 * d), row),
            pl.BlockSpec((RWKV_WIDTH, d), const),
            pl.BlockSpec((GMLP_WIDTH, d), const),
            pl.BlockSpec((MEM_WIDTH, d), const),
            pl.BlockSpec((d, d), const),
            pl.BlockSpec((1, d), const),
            pl.BlockSpec((d, LANES), const),
            pl.BlockSpec((1, LANES), const),
        ],
        out_specs=[
            pl.BlockSpec((tm, d), row),
            pl.BlockSpec((tm, d), row),
            pl.BlockSpec((tm, LANES), row),
            pl.BlockSpec((LANES, tm), lambda i: (0, i)),
            pl.BlockSpec((8, LANES), const),
        ],
        out_shape=[
            jax.ShapeDtypeStruct((n, d), F32),
            jax.ShapeDtypeStruct((n, d), F32),
            jax.ShapeDtypeStruct((n, LANES), F32),
            jax.ShapeDtypeStruct((LANES, n), F32),
            jax.ShapeDtypeStruct((8, LANES), F32),
        ],
        scratch_shapes=[pltpu.VMEM((1, LANES), F32)],
        compiler_params=_params(("arbitrary",)),
        name="merge",
    )(x[0], x[1], o_rw[0], o_rw[1], o_gm[0], o_gm[1], o_me[0], o_me[1], sg, wbr, wbg, wbm, wout, gffn, wrt_pad,
      brt_pad)


N_ZERO_BLOCKS = 2 * N_EXPERTS


def _dispatch_body(dest_ref, zlist_ref, h_ref, xs_hbm, zero_s, sem_z, sem):
    i = pl.program_id(0)
    tm = DISPATCH_TM
    bm = MOE_BM

    @pl.when(i == 0)
    def _():
        zero_s[...] = jnp.zeros_like(zero_s)

        def zero_copy(q):
            start = pl.multiple_of(zlist_ref[q] * bm, bm)
            return pltpu.make_async_copy(zero_s, xs_hbm.at[pl.ds(start, bm)], sem_z)

        def start(q, carry):
            @pl.when(zlist_ref[q] >= 0)
            def _():
                zero_copy(q).start()
            return carry

        def wait(q, carry):
            @pl.when(zlist_ref[q] >= 0)
            def _():
                zero_copy(q).wait()
            return carry

        lax.fori_loop(0, N_ZERO_BLOCKS, start, 0)
        lax.fori_loop(0, N_ZERO_BLOCKS, wait, 0)

    base = i * tm
    n_tok = pl.num_programs(0) * tm

    def body(r, carry):
        for kk in range(TOP_K):
            slot = dest_ref[kk * n_tok + base + r]
            pltpu.make_async_copy(h_ref.at[pl.ds(r, 1)], xs_hbm.at[pl.ds(slot, 1)], sem).start()
        return carry

    lax.fori_loop(0, tm, body, 0, unroll=2)
    for kk in range(TOP_K):
        pltpu.make_async_copy(h_ref, xs_hbm.at[pl.ds(0, tm)], sem).wait()


def _dispatch(dest, zlist, h, n_blocks):
    n, d = h.shape
    tm = DISPATCH_TM
    grid_spec = pltpu.PrefetchScalarGridSpec(
        num_scalar_prefetch=2,
        grid=(n // tm,),
        in_specs=[pl.BlockSpec((tm, d), lambda i, dest, zl: (i, 0))],
        out_specs=pl.BlockSpec(memory_space=pl.ANY),
        scratch_shapes=[
            pltpu.VMEM((MOE_BM, d), F32),
            pltpu.SemaphoreType.DMA(()),
            pltpu.SemaphoreType.DMA(()),
        ],
    )
    return pl.pallas_call(
        _dispatch_body,
        grid_spec=grid_spec,
        out_shape=jax.ShapeDtypeStruct((n_blocks * MOE_BM, d), F32),
        compiler_params=_params(("arbitrary",)),
        name="moe_dispatch",
    )(dest, zlist, h)


CAST_ROWS = 64


def _moe_body(be_ref, nused_ref, x_ref, w1_ref, b1_ref, w2_ref, b2_ref, o_ref, w1b, w2b):
    j = pl.program_id(0)
    n_used = nused_ref[0]

    @pl.when(j < n_used)
    def _():
        changed = jnp.logical_or(j == 0, be_ref[j] != be_ref[jnp.maximum(j - 1, 0)])

        @pl.when(changed)
        def _():
            def cast1(q, carry):
                rows = pl.ds(pl.multiple_of(q * CAST_ROWS, CAST_ROWS), CAST_ROWS)
                w1b[rows, :] = w1_ref[0, rows, :].astype(BF16)
                return carry

            def cast2(q, carry):
                rows = pl.ds(pl.multiple_of(q * CAST_ROWS, CAST_ROWS), CAST_ROWS)
                w2b[rows, :] = w2_ref[0, rows, :].astype(BF16)
                return carry

            lax.fori_loop(0, D_MODEL // CAST_ROWS, cast1, 0)
            lax.fori_loop(0, D_FF // CAST_ROWS, cast2, 0)

        x = x_ref[...].astype(BF16)
        z = jnp.dot(x, w1b[...], preferred_element_type=F32) + b1_ref[0]
        zg = jnp.minimum(z[:, 0:D_FF], SWIGLU_LIMIT)
        zl = jnp.clip(z[:, D_FF:2 * D_FF], -SWIGLU_LIMIT, SWIGLU_LIMIT)
        act = zg * _sigmoid(SWIGLU_ALPHA * zg) * (zl + 1.0)
        o_ref[...] = jnp.dot(act.astype(BF16), w2b[...], preferred_element_type=F32) + b2_ref[0]

    @pl.when(j >= n_used)
    def _():
        o_ref[...] = jnp.zeros_like(o_ref)


def _moe(blk_e, n_used, xs, w1, b1, w2, b2, n_blocks):
    bm = MOE_BM
    d = D_MODEL
    grid_spec = pltpu.PrefetchScalarGridSpec(
        num_scalar_prefetch=2,
        grid=(n_blocks,),
        in_specs=[
            pl.BlockSpec((bm, d), lambda j, be, nu: (j, 0)),
            pl.BlockSpec((1, d, 2 * D_FF), lambda j, be, nu: (be[j], 0, 0)),
            pl.BlockSpec((1, 1, 2 * D_FF), lambda j, be, nu: (be[j], 0, 0)),
            pl.BlockSpec((1, D_FF, d), lambda j, be, nu: (be[j], 0, 0)),
            pl.BlockSpec((1, 1, d), lambda j, be, nu: (be[j], 0, 0)),
        ],
        out_specs=pl.BlockSpec((bm, d), lambda j, be, nu: (j, 0)),
        scratch_shapes=[
            pltpu.VMEM((d, 2 * D_FF), BF16),
            pltpu.VMEM((D_FF, d), BF16),
        ],
    )
    return pl.pallas_call(
        _moe_body,
        grid_spec=grid_spec,
        out_shape=jax.ShapeDtypeStruct((n_blocks * bm, d), F32),
        compiler_params=_params(("arbitrary",)),
        name="moe_ffn",
    )(blk_e, n_used, xs, w1, b1, w2, b2)


def _combine_body(dest_ref, x2_ref, route_ref, gfin_ref, yb_hbm, op_ref, os_ref, buf, sem, *, np_tiles):
    i = pl.program_id(0)
    nb = pl.num_programs(0)
    tm = COMBINE_TM

    def issue(blk, slot):
        base = blk * tm
        n_tok = nb * tm

        def body(r, carry):
            for kk in range(TOP_K):
                d = dest_ref[kk * n_tok + base + r]
                pltpu.make_async_copy(yb_hbm.at[pl.ds(d, 1)], buf.at[slot, kk, pl.ds(r, 1)], sem.at[slot]).start()
            return carry

        lax.fori_loop(0, tm, body, 0, unroll=2)

    slot = lax.rem(i, 2)

    @pl.when(i == 0)
    def _():
        issue(0, 0)

    @pl.when(i + 1 < nb)
    def _():
        issue(i + 1, 1 - slot)

    acc = x2_ref[...]
    for kk in range(TOP_K):
        pltpu.make_async_copy(yb_hbm.at[pl.ds(0, tm)], buf.at[slot, kk], sem.at[slot]).wait()
    for kk in range(TOP_K):
        acc = acc + route_ref[:, ROUTE_GATE + kk:ROUTE_GATE + kk + 1] * buf[slot, kk]
    y = _rms(acc, gfin_ref[...])

    @pl.when(i < np_tiles)
    def _():
        op_ref[...] = y

    @pl.when(i >= np_tiles)
    def _():
        os_ref[...] = y


def _combine(dest, x2, route, gfin, yb, n_p):
    n, d = x2.shape
    tm = COMBINE_TM
    np_tiles = n_p // tm
    grid_spec = pltpu.PrefetchScalarGridSpec(
        num_scalar_prefetch=1,
        grid=(n // tm,),
        in_specs=[
            pl.BlockSpec((tm, d), lambda i, dest: (i, 0)),
            pl.BlockSpec((tm, LANES), lambda i, dest: (i, 0)),
            pl.BlockSpec((1, d), lambda i, dest: (0, 0)),
            pl.BlockSpec(memory_space=pl.ANY),
        ],
        out_specs=[
            pl.BlockSpec((tm, d), lambda i, dest: (jnp.minimum(i, np_tiles - 1), 0)),
            pl.BlockSpec((tm, d), lambda i, dest: (jnp.maximum(i - np_tiles, 0), 0)),
        ],
        scratch_shapes=[
            pltpu.VMEM((2, TOP_K, tm, d), F32),
            pltpu.SemaphoreType.DMA((2,)),
        ],
    )
    return pl.pallas_call(
        functools.partial(_combine_body, np_tiles=np_tiles),
        grid_spec=grid_spec,
        out_shape=[jax.ShapeDtypeStruct((n_p, d), F32), jax.ShapeDtypeStruct((n - n_p, d), F32)],
        compiler_params=_params(("arbitrary",)),
        name="moe_combine",
    )(dest, x2, route, gfin, yb)


def _slot_tables(route_t, counts_row, n_blocks):
    bm = MOE_BM
    e = route_t[ROUTE_E:ROUTE_E + TOP_K].astype(jnp.int32)
    rank = route_t[ROUTE_RANK:ROUTE_RANK + TOP_K].astype(jnp.int32)
    counts = counts_row[:N_EXPERTS].astype(jnp.int32)
    padded = (counts + bm - 1) // bm * bm
    pad_end = jnp.cumsum(padded)
    pad_start = pad_end - padded
    dest = (jnp.take(pad_start, e) + rank).reshape(-1)
    blk_start = jnp.arange(n_blocks, dtype=jnp.int32) * bm
    blk_e = jnp.minimum(jnp.sum(pad_end[None, :] <= blk_start[:, None], axis=1), N_EXPERTS - 1).astype(jnp.int32)
    n_used = pad_end[-1] // bm
    last_blk = jnp.where(padded > 0, pad_end // bm - 1, -1)
    trailing = n_used + jnp.arange(N_EXPERTS, dtype=jnp.int32)
    trailing = jnp.where(trailing < n_blocks, trailing, -1)
    zlist = jnp.concatenate([last_blk, trailing]).astype(jnp.int32)
    return dest.astype(jnp.int32), blk_e, n_used.astype(jnp.int32).reshape(1), zlist


def kernel(x_prompt, x_sample, mem_prompt, state_shift, state_wkv, cache_mem_k, cache_mem_v, g_norm_mix, w_in, mu_shift, w0, w_decay_up, a0, w_a_up, w_g_up, k_k, k_a, r_k, ln_x_g, ln_x_b, gmlp_ln_g, gmlp_ln_b, w_spatial, b_spatial, g_norm_mem, w_mem_kv, w_br_rwkv, w_br_gmlp, w_br_mem, w_out, g_norm_ffn, w_router, b_router, w_exp1, b_exp1, w_exp2, b_exp2, g_norm_final):
    bp, tp, d = x_prompt.shape
    bs, ts, _ = x_sample.shape
    n_p, n_s = bp * tp, bs * ts
    n_all = n_p + n_s
    l = 0
    row = lambda a: a.reshape(1, -1)

    x_pair = (x_prompt.reshape(n_p, d), x_sample.reshape(n_s, d))
    zr, zg, zq, sg = _in_proj(x_pair[0], x_pair[1], row(g_norm_mix[l]), w_in[l].astype(BF16))

    mk_p, mv_p = _mem_kv(mem_prompt.reshape(bp * N_MEM, d), row(g_norm_mem[l]), w_mem_kv[l].astype(BF16))

    rp = dict(mu=row(mu_shift[l]), w0=row(w0[l]), wd=w_decay_up[l].astype(BF16), a0=row(a0[l]),
              wa=w_a_up[l].astype(BF16), wg=w_g_up[l].astype(BF16), kk=row(k_k[l]), ka=row(k_a[l]),
              rk=row(r_k[l]), lng=row(ln_x_g[l]), lnb=row(ln_x_b[l]))
    head_of = jnp.arange(RWKV_WIDTH, dtype=jnp.int32) // HEAD_DIM
    rp["hsum"] = (head_of[:, None] == head_of[None, :]).astype(BF16)
    o_rw_p, s_p, shift_p = _rwkv(zr.reshape(n_all // RWKV_TT, RWKV_TT, SHIFT_WIDTH), 0, bp, tp // RWKV_TT,
                        jnp.zeros((bp, 1, SHIFT_WIDTH), F32), jnp.zeros((bp, RWKV_HEADS, HEAD_DIM, HEAD_DIM), F32),
                        rp, tt=RWKV_TT, c=RWKV_C, nseq=1)
    tile_s = RWKV_SAMPLE_SEQS * ts
    o_rw_s, s_s, shift_s = _rwkv(zr.reshape(n_all // tile_s, tile_s, SHIFT_WIDTH), n_p // tile_s, bs, 1,
                        state_shift[l].reshape(bs, 1, SHIFT_WIDTH), state_wkv[l], rp, tt=tile_s, c=ts,
                        nseq=RWKV_SAMPLE_SEQS)

    tri = jnp.tril(jnp.ones((CHUNK, CHUNK), bool))
    ws_p = jnp.where(tri, w_spatial[l], 0.0).astype(BF16)
    bs_p = b_spatial[l].T
    reps = CHUNK // ts
    tri_s = jnp.tril(jnp.ones((ts, ts), bool))
    ws_small = jnp.where(tri_s, w_spatial[l][:, :ts, :ts], 0.0)
    eye = jnp.eye(reps, dtype=F32)
    ws_s = jnp.einsum("ab,gij->gaibj", eye, ws_small).reshape(GMLP_GROUPS, CHUNK, CHUNK).astype(BF16)
    bs_s = jnp.tile(b_spatial[l][:, :ts], (1, reps)).T
    lng, lnb = row(gmlp_ln_g[l]), row(gmlp_ln_b[l])
    o_gm_p, _ = _gmlp(zg, 0, n_p, lng, lnb, ws_p, bs_p)
    o_gm_s, v_rows_s = _gmlp(zg, n_p // CHUNK, n_s, lng, lnb, ws_s, bs_s)

    o_me_p = _mem_attn(zq.reshape(n_all // ATTN_TT, ATTN_TT, MEM_WIDTH), 0, bp, tp // ATTN_TT,
                       mk_p.reshape(bp, N_MEM, MEM_WIDTH), mv_p.reshape(bp, N_MEM, MEM_WIDTH), tt=ATTN_TT, bb=1,
                       feature_major=False)
    mk_s = jnp.transpose(cache_mem_k[l].reshape(bs, N_MEM, MEM_WIDTH), (0, 2, 1))
    mv_s = jnp.transpose(cache_mem_v[l].reshape(bs, N_MEM, MEM_WIDTH), (0, 2, 1))
    o_me_s = _mem_attn(zq.reshape(n_all // ts, ts, MEM_WIDTH), n_p // ts, bs, 1, mk_s, mv_s, tt=ts, bb=ATTN_BB,
                       feature_major=True)

    wrt_pad = jnp.zeros((d, LANES), F32).at[:, :N_EXPERTS].set(w_router[l])
    brt_pad = jnp.zeros((1, LANES), F32).at[0, :N_EXPERTS].set(b_router[l])
    x2, h2, route, route_t, counts = _merge(
        x_pair, (o_rw_p.reshape(n_p, RWKV_WIDTH), o_rw_s.reshape(n_s, RWKV_WIDTH)), (o_gm_p, o_gm_s),
        (o_me_p.reshape(n_p, MEM_WIDTH), o_me_s.reshape(n_s, MEM_WIDTH)), sg,
        w_br_rwkv[l].astype(BF16), w_br_gmlp[l].astype(BF16), w_br_mem[l].astype(BF16), w_out[l].astype(BF16),
        row(g_norm_ffn[l]), wrt_pad, brt_pad)

    n_assign = n_all * TOP_K
    n_blocks = -(-(n_assign + N_EXPERTS * (MOE_BM - 1)) // MOE_BM)
    dest, blk_e, n_used, zlist = _slot_tables(route_t, counts[0], n_blocks)
    xs = _dispatch(dest, zlist, h2, n_blocks)
    yb = _moe(blk_e, n_used, xs, w_exp1[l], b_exp1[l].reshape(N_EXPERTS, 1, 2 * D_FF),
              w_exp2[l], b_exp2[l].reshape(N_EXPERTS, 1, d), n_blocks)
    y_p, y_s = _combine(dest, x2, route, row(g_norm_final), yb, n_p)

    mk_out = mk_p.reshape(1, bp, N_MEM, MEM_HEADS, MEM_WIDTH // MEM_HEADS)
    mv_out = mv_p.reshape(1, bp, N_MEM, MEM_HEADS, MEM_WIDTH // MEM_HEADS)
    return (y_p.reshape(bp, tp, d), y_s.reshape(bs, ts, d), shift_p.reshape(1, bp, SHIFT_WIDTH), s_p[None], mk_out,
            mv_out, shift_s.reshape(1, bs, SHIFT_WIDTH), s_s[None],
            v_rows_s.reshape(1, bs, ts, GMLP_WIDTH))
```

```python
import functools
import math

import jax
import jax.numpy as jnp
from jax import lax
from jax.experimental import pallas as pl
from jax.experimental.pallas import tpu as pltpu

F32 = jnp.float32
BF16 = jnp.bfloat16

D_MODEL = 1024
RWKV_HEADS = 8
HEAD_DIM = 64
RWKV_WIDTH = RWKV_HEADS * HEAD_DIM
DECAY_LORA = 64
A_LORA = 64
GATE_LORA = 128
GMLP_GROUPS = 4
GMLP_WIDTH = 256
CHUNK = 128
MEM_HEADS = 4
MEM_WIDTH = 256
N_MEM = 256
N_EXPERTS = 32
TOP_K = 4
D_FF = 1024
SWIGLU_ALPHA = 1.702
SWIGLU_LIMIT = 7.0
RMS_EPS = 1e-5
LN_EPS = 1e-5
GN_EPS = 64e-5
SHIFT_WIDTH = 3 * RWKV_WIDTH + DECAY_LORA + A_LORA + GATE_LORA
OFF_GMLP = SHIFT_WIDTH
OFF_QMEM = OFF_GMLP + 2 * GMLP_WIDTH
OFF_GATE = OFF_QMEM + MEM_WIDTH
IN_WIDTH = OFF_GATE + 3 * D_MODEL
LANES = 128
ROW_SUB = D_MODEL // LANES

PROJ_TM = 256
RWKV_TT = 512
RWKV_C = 64
RWKV_SAMPLE_SEQS = 8
ATTN_TT = 512
ATTN_BB = 8
MOE_BM = 512
DISPATCH_TM = 256
COMBINE_TM = 128

ROUTE_E = 0
ROUTE_RANK = 4
ROUTE_GATE = 8

NN = ((1,), (0,))
NT = ((1,), (1,))
TN = ((0,), (0,))

VMEM_LIMIT = 56 * 1024 * 1024


def _params(sem, vmem=VMEM_LIMIT):
    return pltpu.CompilerParams(dimension_semantics=sem, vmem_limit_bytes=vmem)


def _bdot(a, b, dims=NN):
    return lax.dot_general(a.astype(BF16), b.astype(BF16), (dims, ((), ())), preferred_element_type=F32)


def _split(x):
    hi = x.astype(BF16)
    lo = (x - hi.astype(F32)).astype(BF16)
    return hi, lo


def _dot3(a, b, dims=NN):
    dn = (dims, ((), ()))
    ah, al = _split(a)
    bh, bl = _split(b)
    r = lax.dot_general(ah, bh, dn, preferred_element_type=F32)
    r = r + lax.dot_general(al, bh, dn, preferred_element_type=F32)
    return r + lax.dot_general(ah, bl, dn, preferred_element_type=F32)


def _rms(x, g):
    return x * lax.rsqrt(jnp.mean(x * x, axis=-1, keepdims=True) + RMS_EPS) * g


def _store_row_tiles(ref, x):
    m = x.shape[0]
    for j in range(ROW_SUB):
        ref[pl.ds(j, m, stride=ROW_SUB), :] = x[:, j * LANES:(j + 1) * LANES]


def _load_row_tiles(ref, m):
    return jnp.concatenate([ref[pl.ds(j, m, stride=ROW_SUB), :] for j in range(ROW_SUB)], axis=1)


def _sigmoid(x):
    return 0.5 * jnp.tanh(0.5 * x) + 0.5


def _in_proj_body(xp_ref, xs_ref, g_ref, w_ref, zr_ref, zg_ref, zq_ref, sg_ref, *, np_tiles):
    x = jnp.where(pl.program_id(0) < np_tiles, xp_ref[...], xs_ref[...])
    h = _rms(x, g_ref[...]).astype(BF16)
    zr_ref[...] = jnp.dot(h, w_ref[:, 0:OFF_GMLP], preferred_element_type=F32)
    zg_ref[...] = jnp.dot(h, w_ref[:, OFF_GMLP:OFF_QMEM], preferred_element_type=F32)
    zq_ref[...] = jnp.dot(h, w_ref[:, OFF_QMEM:OFF_GATE], preferred_element_type=F32)
    gates = jnp.dot(h, w_ref[:, OFF_GATE:IN_WIDTH], preferred_element_type=F32)
    sg_ref[...] = _sigmoid(gates).astype(BF16)


def _in_proj(x_p, x_s, g, w_bf16):
    n_p = x_p.shape[0]
    n = n_p + x_s.shape[0]
    tm = PROJ_TM
    np_tiles = n_p // tm
    row = lambda i: (i, 0)
    const = lambda i: (0, 0)
    return pl.pallas_call(
        functools.partial(_in_proj_body, np_tiles=np_tiles),
        grid=(n // tm,),
        in_specs=[
            pl.BlockSpec((tm, D_MODEL), lambda i: (jnp.minimum(i, np_tiles - 1), 0)),
            pl.BlockSpec((tm, D_MODEL), lambda i: (jnp.maximum(i - np_tiles, 0), 0)),
            pl.BlockSpec((1, D_MODEL), const),
            pl.BlockSpec((D_MODEL, IN_WIDTH), const, pipeline_mode=pl.Buffered(1)),
        ],
        out_specs=[
            pl.BlockSpec((tm, SHIFT_WIDTH), row),
            pl.BlockSpec((tm, 2 * GMLP_WIDTH), row),
            pl.BlockSpec((tm, MEM_WIDTH), row),
            pl.BlockSpec((tm, 3 * D_MODEL), row),
        ],
        out_shape=[
            jax.ShapeDtypeStruct((n, SHIFT_WIDTH), F32),
            jax.ShapeDtypeStruct((n, 2 * GMLP_WIDTH), F32),
            jax.ShapeDtypeStruct((n, MEM_WIDTH), F32),
            jax.ShapeDtypeStruct((n, 3 * D_MODEL), BF16),
        ],
        compiler_params=_params(("parallel",)),
        name="in_proj",
    )(x_p, x_s, g, w_bf16)


def _mem_kv_body(x_ref, g_ref, w_ref, k_ref, v_ref):
    h = _rms(x_ref[...], g_ref[...]).astype(BF16)
    k_ref[...] = jnp.dot(h, w_ref[:, 0:MEM_WIDTH], preferred_element_type=F32)
    v_ref[...] = jnp.dot(h, w_ref[:, MEM_WIDTH:2 * MEM_WIDTH], preferred_element_type=F32)


def _mem_kv(mem, g, w_bf16):
    n = mem.shape[0]
    tm = PROJ_TM
    return pl.pallas_call(
        _mem_kv_body,
        grid=(n // tm,),
        in_specs=[
            pl.BlockSpec((tm, D_MODEL), lambda i: (i, 0)),
            pl.BlockSpec((1, D_MODEL), lambda i: (0, 0)),
            pl.BlockSpec((D_MODEL, 2 * MEM_WIDTH), lambda i: (0, 0)),
        ],
        out_specs=[pl.BlockSpec((tm, MEM_WIDTH), lambda i: (i, 0))] * 2,
        out_shape=[jax.ShapeDtypeStruct((n, MEM_WIDTH), F32)] * 2,
        compiler_params=_params(("parallel",)),
        name="mem_kv",
    )(mem, g, w_bf16)


EXP_M05 = math.exp(-0.5)


def _dotp(a, b, dims, passes):
    return _dot3(a, b, dims) if passes == 3 else _bdot(a, b, dims)


PREC = dict(a=1, t=1, lkv=1, tw=1, am=1, gh=3, y=1, s=3)


def _unit_lower_inverse(lows, c):
    rows = lax.broadcasted_iota(jnp.int32, (c, c), 0)
    cols = lax.broadcasted_iota(jnp.int32, (c, c), 1)
    eye = (rows == cols).astype(F32)
    invs = [eye - low for low in lows]
    powers = lows
    for _ in range(int(math.log2(c)) - 1):
        powers = [_dotp(pw, pw, NN, PREC["t"]) for pw in powers]
        invs = [inv + _dotp(inv, pw, NN, PREC["t"]) for inv, pw in zip(invs, powers)]
    return invs


def _rwkv_body(zr_ref, shift_ref, s0_ref, mu_ref, w0_ref, wd_ref, a0_ref, wa_ref, wg_ref, kk_ref, ka_ref,
               rk_ref, lng_ref, lnb_ref, hsum_ref,
               o_ref, sout_ref, shout_ref,
               carry_ref, state_ref, r_s, k_s, v_s, kk_s, b_s, ld_s, y_s, bon_s, g_s,
               rw_s, y0_s, gm_s, h0_s, *, tt, c, nseq):
    i = pl.program_id(1)
    rw = RWKV_WIDTH
    nh = RWKV_HEADS

    @pl.when(i == 0)
    def _():
        for q in range(nseq):
            state_ref[q * nh:(q + 1) * nh] = s0_ref[q]

    z = zr_ref[0]
    z_prev = pltpu.roll(z, 1, 0)
    row = lax.broadcasted_iota(jnp.int32, z.shape, 0)
    if nseq == 1:
        @pl.when(i == 0)
        def _():
            carry_ref[...] = shift_ref[0]

        z_prev = jnp.where(row == 0, carry_ref[...], z_prev)
        carry_ref[...] = z[tt - 1:tt, :]
    else:
        first_rows = jnp.concatenate([jnp.broadcast_to(shift_ref[q], (c, SHIFT_WIDTH)) for q in range(nseq)], axis=0)
        z_prev = jnp.where(row % c == 0, first_rows, z_prev)
    zs = z + mu_ref[...] * (z_prev - z)
    r = zs[:, 0:rw]
    k = zs[:, rw:2 * rw]
    v = zs[:, 2 * rw:3 * rw]
    zw = zs[:, 3 * rw:3 * rw + DECAY_LORA]
    za = zs[:, 3 * rw + DECAY_LORA:3 * rw + DECAY_LORA + A_LORA]
    zg = zs[:, 3 * rw + DECAY_LORA + A_LORA:SHIFT_WIDTH]
    xw = w0_ref[...] + _bdot(jnp.tanh(zw), wd_ref[...])
    ld_s[...] = -EXP_M05 * _sigmoid(xw)
    a = _sigmoid(a0_ref[...] + _bdot(za, wa_ref[...]))
    g_s[...] = _bdot(_sigmoid(zg), wg_ref[...])
    kk = k * kk_ref[...]
    k = k * (1.0 + (a - 1.0) * ka_ref[...])
    r_s[...] = r
    k_s[...] = k
    v_s[...] = v

    def head_sum(t):
        return jnp.dot(t.astype(BF16), hsum_ref[...], preferred_element_type=F32)

    kk = kk / jnp.maximum(jnp.sqrt(head_sum(kk * kk)), 1e-12)
    kk_s[...] = kk
    b_s[...] = kk * a
    bon_s[...] = head_sum(r * k * rk_ref[...]) * v

    rows_c = lax.broadcasted_iota(jnp.int32, (c, c), 0)
    cols_c = lax.broadcasted_iota(jnp.int32, (c, c), 1)
    strict = rows_c > cols_c
    incl = rows_c >= cols_c
    tril_ones = incl.astype(BF16)
    rows_2c = lax.broadcasted_iota(jnp.int32, (c, 2 * c), 0)
    cols_2c = lax.broadcasted_iota(jnp.int32, (c, 2 * c), 1)
    incl2 = rows_2c >= jnp.where(cols_2c >= c, cols_2c - c, cols_2c)
    rows_k = lax.broadcasted_iota(jnp.int32, (HEAD_DIM, HEAD_DIM), 0)
    cols_k = lax.broadcasted_iota(jnp.int32, (HEAD_DIM, HEAD_DIM), 1)
    eye_k = (rows_k == cols_k).astype(F32)
    zeros_cv = jnp.zeros((c, HEAD_DIM), F32)
    heads = [slice(h * HEAD_DIM, (h + 1) * HEAD_DIM) for h in range(RWKV_HEADS)]

    n_chunks = tt // c
    group = nseq if nseq > 1 else (2 if n_chunks % 2 == 0 else 1)

    def chunk_rows(ci):
        return pl.ds(ci * c, c) if isinstance(ci, int) else pl.ds(pl.multiple_of(ci * c, c), c)

    def scaled(ci):
        rows = chunk_rows(ci)
        ld = ld_s[rows, :]
        ld_hi, ld_lo = _split(ld)
        cum = (jnp.dot(tril_ones, ld_hi, preferred_element_type=F32)
               + jnp.dot(tril_ones, ld_lo, preferred_element_type=F32))
        e_inc = jnp.exp(cum)
        e_neg = jnp.exp(-cum)
        kt = k_s[rows, :] * e_neg
        bt = b_s[rows, :] * e_neg
        g_end = e_inc[c - 1:c, :]
        return dict(ci=ci, rt=r_s[rows, :] * e_inc, kkt=kk_s[rows, :] * jnp.exp(cum - ld), kt=kt, bt=bt,
                    g_end=g_end, bc=bt * g_end, kc=kt * g_end, vv=v_s[rows, :])

    def phase1(cj, carry):
        chains = [(ch, h, s) for ch in [scaled(cj * group + g) for g in range(group)]
                  for h, s in enumerate(heads)]
        x = [jnp.concatenate([ch["kkt"][:, s], ch["rt"][:, s]], axis=0) for ch, _, s in chains]
        zz = [jnp.concatenate([ch["bt"][:, s], ch["kt"][:, s]], axis=0) for ch, _, s in chains]
        amat = [_dotp(xh, zh, NT, PREC["a"]) for xh, zh in zip(x, zz)]
        l_b = [jnp.where(strict, am[0:c, 0:c], 0.0) for am in amat]
        l_k = [jnp.where(strict, am[0:c, c:2 * c], 0.0) for am in amat]
        a_r = [jnp.where(incl2, am[c:2 * c, :], 0.0) for am in amat]
        tinv = _unit_lower_inverse(l_b, c)
        lkv = [_dotp(lk, ch["vv"][:, s], NN, PREC["lkv"]) for lk, (ch, _, s) in zip(l_k, chains)]
        wu = [-_dotp(t, jnp.concatenate([ch["kkt"][:, s], lv], axis=1), NN, PREC["tw"])
              for t, lv, (ch, _, s) in zip(tinv, lkv, chains)]
        m = [jnp.concatenate([w, jnp.concatenate([zeros_cv, ch["vv"][:, s]], axis=1)], axis=0)
             for w, (ch, _, s) in zip(wu, chains)]
        am2 = [_dotp(ar, mh, NN, PREC["am"]) for ar, mh in zip(a_r, m)]
        gh = [_dotp(mh, jnp.concatenate([ch["bc"][:, s], ch["kc"][:, s]], axis=0), TN, PREC["gh"])
              for mh, (ch, _, s) in zip(m, chains)]
        for q, (ch, h, s) in enumerate(chains):
            idx = ch["ci"] * RWKV_HEADS + h
            rw_s[idx] = ch["rt"][:, s] + am2[q][:, 0:HEAD_DIM]
            y0_s[idx] = am2[q][:, HEAD_DIM:2 * HEAD_DIM]
            gm_s[idx] = gh[q][0:HEAD_DIM, :] + eye_k * ch["g_end"][:, s]
            h0_s[idx] = gh[q][HEAD_DIM:2 * HEAD_DIM, :]
        return carry

    if n_chunks == group:
        phase1(0, 0)
    else:
        lax.fori_loop(0, n_chunks // group, phase1, 0)

    if nseq == 1:
        def phase2(ci, carry):
            rows = chunk_rows(ci)
            for h, s in enumerate(heads):
                idx = ci * nh + h
                s_h = state_ref[h]
                y_s[rows, s] = _dotp(rw_s[idx], s_h, NT, PREC["y"]) + y0_s[idx]
                state_ref[h] = _dotp(s_h, gm_s[idx], NN, PREC["s"]) + h0_s[idx]
            return carry

        lax.fori_loop(0, n_chunks, phase2, 0)
    else:
        pairs = [(q, h, s) for q in range(nseq) for h, s in enumerate(heads)]
        states = [state_ref[q * nh + h] for q, h, _ in pairs]
        ys = [_dotp(rw_s[q * nh + h], st, NT, PREC["y"]) + y0_s[q * nh + h] for (q, h, _), st in zip(pairs, states)]
        new = [_dotp(st, gm_s[q * nh + h], NN, PREC["s"]) + h0_s[q * nh + h] for (q, h, _), st in zip(pairs, states)]
        for (q, h, s), yq, nq in zip(pairs, ys, new):
            y_s[chunk_rows(q), s] = yq
            state_ref[q * nh + h] = nq
    y = y_s[...]
    yc = y - head_sum(y) * (1.0 / HEAD_DIM)
    var = head_sum(yc * yc) * (1.0 / HEAD_DIM)
    yn = yc * lax.rsqrt(var + GN_EPS)
    o_ref[0] = (yn * lng_ref[...] + lnb_ref[...] + bon_s[...]) * g_s[...]

    @pl.when(i == pl.num_programs(1) - 1)
    def _():
        for q in range(nseq):
            sout_ref[q] = state_ref[q * nh:(q + 1) * nh]
            last = tt - 1 if nseq == 1 else (q + 1) * c - 1
            shout_ref[q] = zr_ref[0, last:last + 1, :]


def _rwkv(zr3, blk_off, b, nt, shift_prev, s_prev, p, tt, c, nseq):
    vec = lambda n: pl.BlockSpec((1, n), lambda bi, ti: (0, 0))
    mat = lambda m, n: pl.BlockSpec((m, n), lambda bi, ti: (0, 0))
    rw = RWKV_WIDTH
    scr = lambda: pltpu.VMEM((tt, rw), F32)
    nch = (tt // c) * RWKV_HEADS
    return pl.pallas_call(
        functools.partial(_rwkv_body, tt=tt, c=c, nseq=nseq),
        grid=(b // nseq, nt),
        in_specs=[
            pl.BlockSpec((1, tt, SHIFT_WIDTH), lambda bi, ti: (blk_off + bi * nt + ti, 0, 0)),
            pl.BlockSpec((nseq, 1, SHIFT_WIDTH), lambda bi, ti: (bi, 0, 0)),
            pl.BlockSpec((nseq, RWKV_HEADS, HEAD_DIM, HEAD_DIM), lambda bi, ti: (bi, 0, 0, 0)),
            vec(SHIFT_WIDTH), vec(rw), mat(DECAY_LORA, rw), vec(rw), mat(A_LORA, rw), mat(GATE_LORA, rw),
            vec(rw), vec(rw), vec(rw), vec(rw), vec(rw), mat(rw, rw),
        ],
        out_specs=[
            pl.BlockSpec((1, tt, rw), lambda bi, ti: (bi, ti, 0)),
            pl.BlockSpec((nseq, RWKV_HEADS, HEAD_DIM, HEAD_DIM), lambda bi, ti: (bi, 0, 0, 0)),
            pl.BlockSpec((nseq, 1, SHIFT_WIDTH), lambda bi, ti: (bi, 0, 0)),
        ],
        out_shape=[
            jax.ShapeDtypeStruct((b // nseq, nt * tt, rw), F32),
            jax.ShapeDtypeStruct((b, RWKV_HEADS, HEAD_DIM, HEAD_DIM), F32),
            jax.ShapeDtypeStruct((b, 1, SHIFT_WIDTH), F32),
        ],
        scratch_shapes=[
            pltpu.VMEM((1, SHIFT_WIDTH), F32),
            pltpu.VMEM((nseq * RWKV_HEADS, HEAD_DIM, HEAD_DIM), F32),
            scr(), scr(), scr(), scr(), scr(), scr(), scr(), scr(), scr(),
            pltpu.VMEM((nch, c, HEAD_DIM), F32),
            pltpu.VMEM((nch, c, HEAD_DIM), F32),
            pltpu.VMEM((nch, HEAD_DIM, HEAD_DIM), F32),
            pltpu.VMEM((nch, HEAD_DIM, HEAD_DIM), F32),
        ],
        compiler_params=_params(("parallel", "arbitrary")),
        name="rwkv",
    )(zr3, shift_prev, s_prev, p["mu"], p["w0"], p["wd"], p["a0"], p["wa"], p["wg"], p["kk"], p["ka"],
      p["rk"], p["lng"], p["lnb"], p["hsum"])


def _gmlp_body(z_ref, lng_ref, lnb_ref, ws_ref, bs_ref, o_ref, v_ref):
    z = z_ref[...]
    ge = 0.5 * z * (1.0 + lax.erf(z * (1.0 / math.sqrt(2.0))))
    u = ge[:, 0:GMLP_WIDTH]
    v = ge[:, GMLP_WIDTH:2 * GMLP_WIDTH]
    mean = jnp.mean(v, axis=-1, keepdims=True)
    vc = v - mean
    var = jnp.mean(vc * vc, axis=-1, keepdims=True)
    vn = vc * lax.rsqrt(var + LN_EPS) * lng_ref[...] + lnb_ref[...]
    v_ref[...] = vn
    gd = GMLP_WIDTH // GMLP_GROUPS
    for g in range(GMLP_GROUPS):
        gs = slice(g * gd, (g + 1) * gd)
        sv = jnp.dot(ws_ref[g], vn[:, gs].astype(BF16), preferred_element_type=F32) + bs_ref[:, g:g + 1]
        o_ref[:, gs] = u[:, gs] * sv


def _gmlp(zg, blk_off, n_rows, lng, lnb, ws_bf16, bs):
    return pl.pallas_call(
        _gmlp_body,
        grid=(n_rows // CHUNK,),
        in_specs=[
            pl.BlockSpec((CHUNK, 2 * GMLP_WIDTH), lambda i: (blk_off + i, 0)),
            pl.BlockSpec((1, GMLP_WIDTH), lambda i: (0, 0)),
            pl.BlockSpec((1, GMLP_WIDTH), lambda i: (0, 0)),
            pl.BlockSpec((GMLP_GROUPS, CHUNK, CHUNK), lambda i: (0, 0, 0)),
            pl.BlockSpec((CHUNK, GMLP_GROUPS), lambda i: (0, 0)),
        ],
        out_specs=[
            pl.BlockSpec((CHUNK, GMLP_WIDTH), lambda i: (i, 0)),
            pl.BlockSpec((CHUNK, GMLP_WIDTH), lambda i: (i, 0)),
        ],
        out_shape=[
            jax.ShapeDtypeStruct((n_rows, GMLP_WIDTH), F32),
            jax.ShapeDtypeStruct((n_rows, GMLP_WIDTH), F32),
        ],
        compiler_params=_params(("parallel",)),
        name="gmlp",
    )(zg, lng, lnb, ws_bf16, bs)


def _mem_attn_body(q_ref, k_ref, v_ref, o_ref, *, bb, feature_major):
    hd = MEM_WIDTH // MEM_HEADS
    pairs = [(s, slice(h * hd, (h + 1) * hd)) for s in range(bb) for h in range(MEM_HEADS)]
    if feature_major:
        scores = [_bdot(q_ref[s, :, hs], k_ref[s, hs, :], NN) * (hd ** -0.5) for s, hs in pairs]
    else:
        scores = [_bdot(q_ref[s, :, hs], k_ref[s, :, hs], NT) * (hd ** -0.5) for s, hs in pairs]
    probs = []
    for sc in scores:
        pr = jnp.exp(sc - jnp.max(sc, axis=-1, keepdims=True))
        probs.append(pr / jnp.sum(pr, axis=-1, keepdims=True))
    for (s, hs), pr in zip(pairs, probs):
        if feature_major:
            o_ref[s, :, hs] = _bdot(pr, v_ref[s, hs, :], NT)
        else:
            o_ref[s, :, hs] = _bdot(pr, v_ref[s, :, hs])


def _mem_attn(q3, blk_off, b, nt, mk, mv, tt, bb, feature_major):
    return pl.pallas_call(
        functools.partial(_mem_attn_body, bb=bb, feature_major=feature_major),
        grid=(b // bb, nt),
        in_specs=[
            pl.BlockSpec((bb, tt, MEM_WIDTH), lambda bi, ti: (blk_off // bb + bi * nt + ti, 0, 0)),
            pl.BlockSpec((bb, N_MEM, MEM_WIDTH), lambda bi, ti: (bi, 0, 0)),
            pl.BlockSpec((bb, N_MEM, MEM_WIDTH), lambda bi, ti: (bi, 0, 0)),
        ],
        out_specs=pl.BlockSpec((bb, tt, MEM_WIDTH), lambda bi, ti: (bi, ti, 0)),
        out_shape=jax.ShapeDtypeStruct((b, nt * tt, MEM_WIDTH), F32),
        compiler_params=_params(("parallel", "parallel")),
        name="mem_attn",
    )(q3, mk, mv)


def _merge_body(x_p, x_s, orw_p, orw_s, ogm_p, ogm_s, ome_p, ome_s, sg_ref, wbr_ref, wbg_ref, wbm_ref, wout_ref,
                gffn_ref, wrt_ref, brt_ref, x2_ref, h2_ref, route_ref, route_t_ref, cnt_ref, count_s, *, np_tiles):
    i = pl.program_id(0)
    d = D_MODEL
    tm = x_p.shape[0]
    is_p = i < np_tiles
    x = jnp.where(is_p, x_p[...], x_s[...])
    orw = jnp.where(is_p, orw_p[...], orw_s[...])
    ogm = jnp.where(is_p, ogm_p[...], ogm_s[...])
    ome = jnp.where(is_p, ome_p[...], ome_s[...])
    merged = sg_ref[:, 0:d].astype(F32) * _bdot(orw, wbr_ref[...])
    merged = merged + sg_ref[:, d:2 * d].astype(F32) * _bdot(ogm, wbg_ref[...])
    merged = merged + sg_ref[:, 2 * d:3 * d].astype(F32) * _bdot(ome, wbm_ref[...])
    x2 = x + _bdot(merged, wout_ref[...])
    x2_ref[...] = x2
    h2 = _rms(x2, gffn_ref[...])
    _store_row_tiles(h2_ref, h2)

    @pl.when(i == 0)
    def _():
        count_s[...] = jnp.zeros_like(count_s)

    lane = lax.broadcasted_iota(jnp.int32, (tm, LANES), 1)
    logits = jnp.where(lane < N_EXPERTS, _dot3(h2, wrt_ref[...]) + brt_ref[...], -jnp.inf)
    lane_f = lane.astype(F32)
    tops, hots, idxs = [], [], []
    for _ in range(TOP_K):
        top = jnp.max(logits, axis=-1, keepdims=True)
        idx = jnp.min(jnp.where(logits == top, lane_f, float(LANES)), axis=-1, keepdims=True)
        hot = lane_f == idx
        logits = jnp.where(hot, -jnp.inf, logits)
        tops.append(top)
        hots.append(hot)
        idxs.append(idx)
    weights = [jnp.exp(t - tops[0]) for t in tops]
    denom = weights[0] + weights[1] + weights[2] + weights[3]
    onehot = jnp.zeros((tm, LANES), F32)
    for hot in hots:
        onehot = onehot + jnp.where(hot, 1.0, 0.0)
    rows_t = lax.broadcasted_iota(jnp.int32, (tm, tm), 0)
    cols_t = lax.broadcasted_iota(jnp.int32, (tm, tm), 1)
    ahead = jnp.where(rows_t > cols_t, 1.0, 0.0).astype(BF16)
    prefix = jnp.dot(ahead, onehot.astype(BF16), preferred_element_type=F32) + count_s[...]
    route = jnp.zeros((tm, LANES), F32)
    for kk in range(TOP_K):
        rank = jnp.sum(jnp.where(hots[kk], prefix, 0.0), axis=-1, keepdims=True)
        route = jnp.where(lane == ROUTE_E + kk, idxs[kk], route)
        route = jnp.where(lane == ROUTE_RANK + kk, rank, route)
        route = jnp.where(lane == ROUTE_GATE + kk, weights[kk] / denom, route)
    route_ref[...] = route
    route_t_ref[...] = route.T
    count_s[...] = count_s[...] + jnp.sum(onehot, axis=0, keepdims=True)
    cnt_ref[...] = jnp.broadcast_to(count_s[...], cnt_ref.shape)


def _merge(x, o_rw, o_gm, o_me, sg, wbr, wbg, wbm, wout, gffn, wrt_pad, brt_pad):
    n_p = x[0].shape[0]
    n = n_p + x[1].shape[0]
    tm = PROJ_TM
    d = D_MODEL
    np_tiles = n_p // tm
    row = lambda i: (i, 0)
    const = lambda i: (0, 0)
    first = lambda i: (jnp.minimum(i, np_tiles - 1), 0)
    second = lambda i: (jnp.maximum(i - np_tiles, 0), 0)

    def pair(width):
        return [pl.BlockSpec((tm, width), first), pl.BlockSpec((tm, width), second)]

    return pl.pallas_call(
        functools.partial(_merge_body, np_tiles=np_tiles),
        grid=(n // tm,),
        in_specs=pair(d) + pair(RWKV_WIDTH) + pair(GMLP_WIDTH) + pair(MEM_WIDTH) + [
            pl.BlockSpec((tm, 3 * d), row),
            pl.BlockSpec((RWKV_WIDTH, d), const),
            pl.BlockSpec((GMLP_WIDTH, d), const),
            pl.BlockSpec((MEM_WIDTH, d), const),
            pl.BlockSpec((d, d), const),
            pl.BlockSpec((1, d), const),
            pl.BlockSpec((d, LANES), const),
            pl.BlockSpec((1, LANES), const),
        ],
        out_specs=[
            pl.BlockSpec((tm, d), row),
            pl.BlockSpec((tm * ROW_SUB, LANES), row),
            pl.BlockSpec((tm, LANES), row),
            pl.BlockSpec((LANES, tm), lambda i: (0, i)),
            pl.BlockSpec((8, LANES), const),
        ],
        out_shape=[
            jax.ShapeDtypeStruct((n, d), F32),
            jax.ShapeDtypeStruct((n * ROW_SUB, LANES), F32),
            jax.ShapeDtypeStruct((n, LANES), F32),
            jax.ShapeDtypeStruct((LANES, n), F32),
            jax.ShapeDtypeStruct((8, LANES), F32),
        ],
        scratch_shapes=[pltpu.VMEM((1, LANES), F32)],
        compiler_params=_params(("arbitrary",)),
        name="merge",
    )(x[0], x[1], o_rw[0], o_rw[1], o_gm[0], o_gm[1], o_me[0], o_me[1], sg, wbr, wbg, wbm, wout, gffn, wrt_pad,
      brt_pad)


N_ZERO_BLOCKS = 2 * N_EXPERTS


def _dispatch_body(dest_ref, zlist_ref, h_ref, xs_hbm, zero_s, sem_z, sem):
    i = pl.program_id(0)
    tm = DISPATCH_TM
    bm = MOE_BM

    @pl.when(i == 0)
    def _():
        zero_s[...] = jnp.zeros_like(zero_s)

        def zero_copy(q):
            start = pl.multiple_of(zlist_ref[q] * bm, bm)
            return pltpu.make_async_copy(zero_s, xs_hbm.at[pl.ds(start, bm)], sem_z)

        def start(q, carry):
            @pl.when(zlist_ref[q] >= 0)
            def _():
                zero_copy(q).start()
            return carry

        def wait(q, carry):
            @pl.when(zlist_ref[q] >= 0)
            def _():
                zero_copy(q).wait()
            return carry

        lax.fori_loop(0, N_ZERO_BLOCKS, start, 0)
        lax.fori_loop(0, N_ZERO_BLOCKS, wait, 0)

    base = i * tm
    n_tok = pl.num_programs(0) * tm

    def body(r, carry):
        for kk in range(TOP_K):
            slot = dest_ref[kk * n_tok + base + r]
            pltpu.make_async_copy(h_ref.at[r], xs_hbm.at[slot], sem).start()
        return carry

    lax.fori_loop(0, tm, body, 0, unroll=2)
    for kk in range(TOP_K):
        pltpu.make_async_copy(h_ref, xs_hbm.at[pl.ds(0, tm)], sem).wait()


def _dispatch(dest, zlist, h, n_blocks):
    n = h.shape[0]
    tm = DISPATCH_TM
    grid_spec = pltpu.PrefetchScalarGridSpec(
        num_scalar_prefetch=2,
        grid=(n // tm,),
        in_specs=[pl.BlockSpec((tm, ROW_SUB, LANES), lambda i, dest, zl: (i, 0, 0))],
        out_specs=pl.BlockSpec(memory_space=pl.ANY),
        scratch_shapes=[
            pltpu.VMEM((MOE_BM, ROW_SUB, LANES), F32),
            pltpu.SemaphoreType.DMA(()),
            pltpu.SemaphoreType.DMA(()),
        ],
    )
    return pl.pallas_call(
        _dispatch_body,
        grid_spec=grid_spec,
        out_shape=jax.ShapeDtypeStruct((n_blocks * MOE_BM, ROW_SUB, LANES), F32),
        compiler_params=_params(("arbitrary",)),
        name="moe_dispatch",
    )(dest, zlist, h)


CAST_ROWS = 64


def _moe_body(be_ref, nused_ref, x_ref, w1_ref, b1_ref, w2_ref, b2_ref, o_ref, w1b, w2b):
    j = pl.program_id(0)
    n_used = nused_ref[0]

    @pl.when(j < n_used)
    def _():
        changed = jnp.logical_or(j == 0, be_ref[j] != be_ref[jnp.maximum(j - 1, 0)])

        @pl.when(changed)
        def _():
            def cast1(q, carry):
                rows = pl.ds(pl.multiple_of(q * CAST_ROWS, CAST_ROWS), CAST_ROWS)
                w1b[rows, :] = w1_ref[0, rows, :].astype(BF16)
                return carry

            def cast2(q, carry):
                rows = pl.ds(pl.multiple_of(q * CAST_ROWS, CAST_ROWS), CAST_ROWS)
                w2b[rows, :] = w2_ref[0, rows, :].astype(BF16)
                return carry

            lax.fori_loop(0, D_MODEL // CAST_ROWS, cast1, 0)
            lax.fori_loop(0, D_FF // CAST_ROWS, cast2, 0)

        x = _load_row_tiles(x_ref, MOE_BM).astype(BF16)
        z = jnp.dot(x, w1b[...], preferred_element_type=F32) + b1_ref[0]
        zg = jnp.minimum(z[:, 0:D_FF], SWIGLU_LIMIT)
        zl = jnp.clip(z[:, D_FF:2 * D_FF], -SWIGLU_LIMIT, SWIGLU_LIMIT)
        act = zg * _sigmoid(SWIGLU_ALPHA * zg) * (zl + 1.0)
        _store_row_tiles(o_ref, jnp.dot(act.astype(BF16), w2b[...], preferred_element_type=F32) + b2_ref[0])

    @pl.when(j >= n_used)
    def _():
        o_ref[...] = jnp.zeros_like(o_ref)


def _moe(blk_e, n_used, xs, w1, b1, w2, b2, n_blocks):
    bm = MOE_BM
    d = D_MODEL
    grid_spec = pltpu.PrefetchScalarGridSpec(
        num_scalar_prefetch=2,
        grid=(n_blocks,),
        in_specs=[
            pl.BlockSpec((bm * ROW_SUB, LANES), lambda j, be, nu: (j, 0)),
            pl.BlockSpec((1, d, 2 * D_FF), lambda j, be, nu: (be[j], 0, 0)),
            pl.BlockSpec((1, 1, 2 * D_FF), lambda j, be, nu: (be[j], 0, 0)),
            pl.BlockSpec((1, D_FF, d), lambda j, be, nu: (be[j], 0, 0)),
            pl.BlockSpec((1, 1, d), lambda j, be, nu: (be[j], 0, 0)),
        ],
        out_specs=pl.BlockSpec((bm * ROW_SUB, LANES), lambda j, be, nu: (j, 0)),
        scratch_shapes=[
            pltpu.VMEM((d, 2 * D_FF), BF16),
            pltpu.VMEM((D_FF, d), BF16),
        ],
    )
    return pl.pallas_call(
        _moe_body,
        grid_spec=grid_spec,
        out_shape=jax.ShapeDtypeStruct((n_blocks * bm * ROW_SUB, LANES), F32),
        compiler_params=_params(("arbitrary",)),
        name="moe_ffn",
    )(blk_e, n_used, xs, w1, b1, w2, b2)


def _combine_body(dest_ref, x2_ref, route_ref, gfin_ref, yb_hbm, op_ref, os_ref, buf, sem, *, np_tiles):
    i = pl.program_id(0)
    nb = pl.num_programs(0)
    tm = COMBINE_TM

    def issue(blk, slot):
        base = blk * tm
        n_tok = nb * tm

        def body(r, carry):
            for kk in range(TOP_K):
                d = dest_ref[kk * n_tok + base + r]
                src = pl.ds(pl.multiple_of(d * ROW_SUB, ROW_SUB), ROW_SUB)
                dst = pl.ds(pl.multiple_of(r * ROW_SUB, ROW_SUB), ROW_SUB)
                pltpu.make_async_copy(yb_hbm.at[src], buf.at[slot, kk, dst], sem.at[slot]).start()
            return carry

        lax.fori_loop(0, tm, body, 0, unroll=2)

    slot = lax.rem(i, 2)

    @pl.when(i == 0)
    def _():
        issue(0, 0)

    @pl.when(i + 1 < nb)
    def _():
        issue(i + 1, 1 - slot)

    acc = x2_ref[...]
    for kk in range(TOP_K):
        pltpu.make_async_copy(yb_hbm.at[pl.ds(0, tm * ROW_SUB)], buf.at[slot, kk], sem.at[slot]).wait()
    for kk in range(TOP_K):
        acc = acc + route_ref[:, ROUTE_GATE + kk:ROUTE_GATE + kk + 1] * _load_row_tiles(buf.at[slot, kk], tm)
    y = _rms(acc, gfin_ref[...])

    @pl.when(i < np_tiles)
    def _():
        op_ref[...] = y

    @pl.when(i >= np_tiles)
    def _():
        os_ref[...] = y


def _combine(dest, x2, route, gfin, yb, n_p):
    n, d = x2.shape
    tm = COMBINE_TM
    np_tiles = n_p // tm
    grid_spec = pltpu.PrefetchScalarGridSpec(
        num_scalar_prefetch=1,
        grid=(n // tm,),
        in_specs=[
            pl.BlockSpec((tm, d), lambda i, dest: (i, 0)),
            pl.BlockSpec((tm, LANES), lambda i, dest: (i, 0)),
            pl.BlockSpec((1, d), lambda i, dest: (0, 0)),
            pl.BlockSpec(memory_space=pl.ANY),
        ],
        out_specs=[
            pl.BlockSpec((tm, d), lambda i, dest: (jnp.minimum(i, np_tiles - 1), 0)),
            pl.BlockSpec((tm, d), lambda i, dest: (jnp.maximum(i - np_tiles, 0), 0)),
        ],
        scratch_shapes=[
            pltpu.VMEM((2, TOP_K, tm * ROW_SUB, LANES), F32),
            pltpu.SemaphoreType.DMA((2,)),
        ],
    )
    return pl.pallas_call(
        functools.partial(_combine_body, np_tiles=np_tiles),
        grid_spec=grid_spec,
        out_shape=[jax.ShapeDtypeStruct((n_p, d), F32), jax.ShapeDtypeStruct((n - n_p, d), F32)],
        compiler_params=_params(("arbitrary",)),
        name="moe_combine",
    )(dest, x2, route, gfin, yb)


def _slot_tables(route_t, counts_row, n_blocks):
    bm = MOE_BM
    e = route_t[ROUTE_E:ROUTE_E + TOP_K].astype(jnp.int32)
    rank = route_t[ROUTE_RANK:ROUTE_RANK + TOP_K].astype(jnp.int32)
    counts = counts_row[:N_EXPERTS].astype(jnp.int32)
    padded = (counts + bm - 1) // bm * bm
    pad_end = jnp.cumsum(padded)
    pad_start = pad_end - padded
    experts = jnp.arange(N_EXPERTS, dtype=jnp.int32)[:, None, None]
    dest = (jnp.sum(jnp.where(e[None] == experts, pad_start[:, None, None], 0), axis=0) + rank).reshape(-1)
    blk_start = jnp.arange(n_blocks, dtype=jnp.int32) * bm
    blk_e = jnp.minimum(jnp.sum(pad_end[None, :] <= blk_start[:, None], axis=1), N_EXPERTS - 1).astype(jnp.int32)
    n_used = pad_end[-1] // bm
    last_blk = jnp.where(padded > 0, pad_end // bm - 1, -1)
    trailing = n_used + jnp.arange(N_EXPERTS, dtype=jnp.int32)
    trailing = jnp.where(trailing < n_blocks, trailing, -1)
    zlist = jnp.concatenate([last_blk, trailing]).astype(jnp.int32)
    return dest.astype(jnp.int32), blk_e, n_used.astype(jnp.int32).reshape(1), zlist


def kernel(x_prompt, x_sample, mem_prompt, state_shift, state_wkv, cache_mem_k, cache_mem_v, g_norm_mix, w_in, mu_shift, w0, w_decay_up, a0, w_a_up, w_g_up, k_k, k_a, r_k, ln_x_g, ln_x_b, gmlp_ln_g, gmlp_ln_b, w_spatial, b_spatial, g_norm_mem, w_mem_kv, w_br_rwkv, w_br_gmlp, w_br_mem, w_out, g_norm_ffn, w_router, b_router, w_exp1, b_exp1, w_exp2, b_exp2, g_norm_final):
    bp, tp, d = x_prompt.shape
    bs, ts, _ = x_sample.shape
    n_p, n_s = bp * tp, bs * ts
    n_all = n_p + n_s
    l = 0
    row = lambda a: a.reshape(1, -1)

    x_pair = (x_prompt.reshape(n_p, d), x_sample.reshape(n_s, d))
    zr, zg, zq, sg = _in_proj(x_pair[0], x_pair[1], row(g_norm_mix[l]), w_in[l].astype(BF16))

    mk_p, mv_p = _mem_kv(mem_prompt.reshape(bp * N_MEM, d), row(g_norm_mem[l]), w_mem_kv[l].astype(BF16))

    rp = dict(mu=row(mu_shift[l]), w0=row(w0[l]), wd=w_decay_up[l].astype(BF16), a0=row(a0[l]),
              wa=w_a_up[l].astype(BF16), wg=w_g_up[l].astype(BF16), kk=row(k_k[l]), ka=row(k_a[l]),
              rk=row(r_k[l]), lng=row(ln_x_g[l]), lnb=row(ln_x_b[l]))
    head_of = jnp.arange(RWKV_WIDTH, dtype=jnp.int32) // HEAD_DIM
    rp["hsum"] = (head_of[:, None] == head_of[None, :]).astype(BF16)
    o_rw_p, s_p, shift_p = _rwkv(zr.reshape(n_all // RWKV_TT, RWKV_TT, SHIFT_WIDTH), 0, bp, tp // RWKV_TT,
                        jnp.zeros((bp, 1, SHIFT_WIDTH), F32), jnp.zeros((bp, RWKV_HEADS, HEAD_DIM, HEAD_DIM), F32),
                        rp, tt=RWKV_TT, c=RWKV_C, nseq=1)
    tile_s = RWKV_SAMPLE_SEQS * ts
    o_rw_s, s_s, shift_s = _rwkv(zr.reshape(n_all // tile_s, tile_s, SHIFT_WIDTH), n_p // tile_s, bs, 1,
                        state_shift[l].reshape(bs, 1, SHIFT_WIDTH), state_wkv[l], rp, tt=tile_s, c=ts,
                        nseq=RWKV_SAMPLE_SEQS)

    tri = jnp.tril(jnp.ones((CHUNK, CHUNK), bool))
    ws_p = jnp.where(tri, w_spatial[l], 0.0).astype(BF16)
    bs_p = b_spatial[l].T
    reps = CHUNK // ts
    tri_s = jnp.tril(jnp.ones((ts, ts), bool))
    ws_small = jnp.where(tri_s, w_spatial[l][:, :ts, :ts], 0.0)
    eye = jnp.eye(reps, dtype=F32)
    ws_s = jnp.einsum("ab,gij->gaibj", eye, ws_small).reshape(GMLP_GROUPS, CHUNK, CHUNK).astype(BF16)
    bs_s = jnp.tile(b_spatial[l][:, :ts], (1, reps)).T
    lng, lnb = row(gmlp_ln_g[l]), row(gmlp_ln_b[l])
    o_gm_p, _ = _gmlp(zg, 0, n_p, lng, lnb, ws_p, bs_p)
    o_gm_s, v_rows_s = _gmlp(zg, n_p // CHUNK, n_s, lng, lnb, ws_s, bs_s)

    o_me_p = _mem_attn(zq.reshape(n_all // ATTN_TT, ATTN_TT, MEM_WIDTH), 0, bp, tp // ATTN_TT,
                       mk_p.reshape(bp, N_MEM, MEM_WIDTH), mv_p.reshape(bp, N_MEM, MEM_WIDTH), tt=ATTN_TT, bb=1,
                       feature_major=False)
    mk_s = jnp.transpose(cache_mem_k[l].reshape(bs, N_MEM, MEM_WIDTH), (0, 2, 1))
    mv_s = jnp.transpose(cache_mem_v[l].reshape(bs, N_MEM, MEM_WIDTH), (0, 2, 1))
    o_me_s = _mem_attn(zq.reshape(n_all // ts, ts, MEM_WIDTH), n_p // ts, bs, 1, mk_s, mv_s, tt=ts, bb=ATTN_BB,
                       feature_major=True)

    wrt_pad = jnp.zeros((d, LANES), F32).at[:, :N_EXPERTS].set(w_router[l])
    brt_pad = jnp.zeros((1, LANES), F32).at[0, :N_EXPERTS].set(b_router[l])
    x2, h2, route, route_t, counts = _merge(
        x_pair, (o_rw_p.reshape(n_p, RWKV_WIDTH), o_rw_s.reshape(n_s, RWKV_WIDTH)), (o_gm_p, o_gm_s),
        (o_me_p.reshape(n_p, MEM_WIDTH), o_me_s.reshape(n_s, MEM_WIDTH)), sg,
        w_br_rwkv[l].astype(BF16), w_br_gmlp[l].astype(BF16), w_br_mem[l].astype(BF16), w_out[l].astype(BF16),
        row(g_norm_ffn[l]), wrt_pad, brt_pad)

    n_assign = n_all * TOP_K
    n_blocks = -(-(n_assign + N_EXPERTS * (MOE_BM - 1)) // MOE_BM)
    dest, blk_e, n_used, zlist = _slot_tables(route_t, counts[0], n_blocks)
    xs = _dispatch(dest, zlist, h2.reshape(n_all, ROW_SUB, LANES), n_blocks)
    yb = _moe(blk_e, n_used, xs.reshape(n_blocks * MOE_BM * ROW_SUB, LANES), w_exp1[l], b_exp1[l].reshape(N_EXPERTS, 1, 2 * D_FF),
              w_exp2[l], b_exp2[l].reshape(N_EXPERTS, 1, d), n_blocks)
    y_p, y_s = _combine(dest, x2, route, row(g_norm_final), yb, n_p)

    mk_out = mk_p.reshape(1, bp, N_MEM, MEM_HEADS, MEM_WIDTH // MEM_HEADS)
    mv_out = mv_p.reshape(1, bp, N_MEM, MEM_HEADS, MEM_WIDTH // MEM_HEADS)
    return (y_p.reshape(bp, tp, d), y_s.reshape(bs, ts, d), shift_p.reshape(1, bp, SHIFT_WIDTH), s_p[None], mk_out,
            mv_out, shift_s.reshape(1, bs, SHIFT_WIDTH), s_s[None],
            v_rows_s.reshape(1, bs, ts, GMLP_WIDTH))
```

```python
import functools
import math

import jax
import jax.numpy as jnp
from jax import lax
from jax.experimental import pallas as pl
from jax.experimental.pallas import tpu as pltpu

F32 = jnp.float32
BF16 = jnp.bfloat16

D_MODEL = 1024
RWKV_HEADS = 8
HEAD_DIM = 64
RWKV_WIDTH = RWKV_HEADS * HEAD_DIM
DECAY_LORA = 64
A_LORA = 64
GATE_LORA = 128
GMLP_GROUPS = 4
GMLP_WIDTH = 256
CHUNK = 128
MEM_HEADS = 4
MEM_WIDTH = 256
N_MEM = 256
N_EXPERTS = 32
TOP_K = 4
D_FF = 1024
SWIGLU_ALPHA = 1.702
SWIGLU_LIMIT = 7.0
RMS_EPS = 1e-5
LN_EPS = 1e-5
GN_EPS = 64e-5
SHIFT_WIDTH = 3 * RWKV_WIDTH + DECAY_LORA + A_LORA + GATE_LORA
OFF_GMLP = SHIFT_WIDTH
OFF_QMEM = OFF_GMLP + 2 * GMLP_WIDTH
OFF_GATE = OFF_QMEM + MEM_WIDTH
IN_WIDTH = OFF_GATE + 3 * D_MODEL
LANES = 128
ROW_SUB = D_MODEL // LANES

PROJ_TM = 256
RWKV_TT = 512
RWKV_C = 64
RWKV_SAMPLE_SEQS = 8
GMLP_TILE = 4
ATTN_TT = 512
ATTN_BB = 8
MOE_BM = 512
DISPATCH_TM = 256
COMBINE_TM = 128

ROUTE_E = 0
ROUTE_RANK = 4
ROUTE_GATE = 8

NN = ((1,), (0,))
NT = ((1,), (1,))
TN = ((0,), (0,))

VMEM_LIMIT = 56 * 1024 * 1024


def _params(sem, vmem=VMEM_LIMIT):
    return pltpu.CompilerParams(dimension_semantics=sem, vmem_limit_bytes=vmem)


def _bdot(a, b, dims=NN):
    return lax.dot_general(a.astype(BF16), b.astype(BF16), (dims, ((), ())), preferred_element_type=F32)


def _split(x):
    hi = x.astype(BF16)
    lo = (x - hi.astype(F32)).astype(BF16)
    return hi, lo


def _dot3(a, b, dims=NN):
    dn = (dims, ((), ()))
    ah, al = _split(a)
    bh, bl = _split(b)
    r = lax.dot_general(ah, bh, dn, preferred_element_type=F32)
    r = r + lax.dot_general(al, bh, dn, preferred_element_type=F32)
    return r + lax.dot_general(ah, bl, dn, preferred_element_type=F32)


def _rms(x, g):
    return x * lax.rsqrt(jnp.mean(x * x, axis=-1, keepdims=True) + RMS_EPS) * g


def _store_row_tiles(ref, x):
    m = x.shape[0]
    for j in range(ROW_SUB):
        ref[pl.ds(j, m, stride=ROW_SUB), :] = x[:, j * LANES:(j + 1) * LANES]


def _load_row_tiles(ref, m):
    return jnp.concatenate([ref[pl.ds(j, m, stride=ROW_SUB), :] for j in range(ROW_SUB)], axis=1)


def _sigmoid(x):
    return 0.5 * jnp.tanh(0.5 * x) + 0.5


def _in_proj_body(xp_ref, xs_ref, g_ref, w_ref, zr_ref, zg_ref, zq_ref, sg_ref, *, np_tiles):
    x = jnp.where(pl.program_id(0) < np_tiles, xp_ref[...], xs_ref[...])
    h = _rms(x, g_ref[...]).astype(BF16)
    zr_ref[...] = jnp.dot(h, w_ref[:, 0:OFF_GMLP], preferred_element_type=F32)
    zg_ref[...] = jnp.dot(h, w_ref[:, OFF_GMLP:OFF_QMEM], preferred_element_type=F32)
    zq_ref[...] = jnp.dot(h, w_ref[:, OFF_QMEM:OFF_GATE], preferred_element_type=F32).astype(zq_ref.dtype)
    gates = jnp.dot(h, w_ref[:, OFF_GATE:IN_WIDTH], preferred_element_type=F32)
    sg_ref[...] = _sigmoid(gates).astype(BF16)


def _in_proj(x_p, x_s, g, w_bf16):
    n_p = x_p.shape[0]
    n = n_p + x_s.shape[0]
    tm = PROJ_TM
    np_tiles = n_p // tm
    row = lambda i: (i, 0)
    const = lambda i: (0, 0)
    return pl.pallas_call(
        functools.partial(_in_proj_body, np_tiles=np_tiles),
        grid=(n // tm,),
        in_specs=[
            pl.BlockSpec((tm, D_MODEL), lambda i: (jnp.minimum(i, np_tiles - 1), 0)),
            pl.BlockSpec((tm, D_MODEL), lambda i: (jnp.maximum(i - np_tiles, 0), 0)),
            pl.BlockSpec((1, D_MODEL), const),
            pl.BlockSpec((D_MODEL, IN_WIDTH), const, pipeline_mode=pl.Buffered(1)),
        ],
        out_specs=[
            pl.BlockSpec((tm, SHIFT_WIDTH), row),
            pl.BlockSpec((tm, 2 * GMLP_WIDTH), row),
            pl.BlockSpec((tm, MEM_WIDTH), row),
            pl.BlockSpec((tm, 3 * D_MODEL), row),
        ],
        out_shape=[
            jax.ShapeDtypeStruct((n, SHIFT_WIDTH), F32),
            jax.ShapeDtypeStruct((n, 2 * GMLP_WIDTH), F32),
            jax.ShapeDtypeStruct((n, MEM_WIDTH), BF16),
            jax.ShapeDtypeStruct((n, 3 * D_MODEL), BF16),
        ],
        compiler_params=_params(("parallel",)),
        name="in_proj",
    )(x_p, x_s, g, w_bf16)


def _mem_kv_body(x_ref, g_ref, w_ref, k_ref, v_ref):
    h = _rms(x_ref[...], g_ref[...]).astype(BF16)
    k_ref[...] = jnp.dot(h, w_ref[:, 0:MEM_WIDTH], preferred_element_type=F32)
    v_ref[...] = jnp.dot(h, w_ref[:, MEM_WIDTH:2 * MEM_WIDTH], preferred_element_type=F32)


def _mem_kv(mem, g, w_bf16):
    n = mem.shape[0]
    tm = PROJ_TM
    return pl.pallas_call(
        _mem_kv_body,
        grid=(n // tm,),
        in_specs=[
            pl.BlockSpec((tm, D_MODEL), lambda i: (i, 0)),
            pl.BlockSpec((1, D_MODEL), lambda i: (0, 0)),
            pl.BlockSpec((D_MODEL, 2 * MEM_WIDTH), lambda i: (0, 0)),
        ],
        out_specs=[pl.BlockSpec((tm, MEM_WIDTH), lambda i: (i, 0))] * 2,
        out_shape=[jax.ShapeDtypeStruct((n, MEM_WIDTH), F32)] * 2,
        compiler_params=_params(("parallel",)),
        name="mem_kv",
    )(mem, g, w_bf16)


EXP_M05 = math.exp(-0.5)


def _dotp(a, b, dims, passes):
    return _dot3(a, b, dims) if passes == 3 else _bdot(a, b, dims)


PREC = dict(a=1, t=1, lkv=1, tw=1, am=1, gh=3, y=1, s=3)


def _unit_lower_inverse(lows, c):
    rows = lax.broadcasted_iota(jnp.int32, (c, c), 0)
    cols = lax.broadcasted_iota(jnp.int32, (c, c), 1)
    eye = (rows == cols).astype(F32)
    invs = [eye - low for low in lows]
    powers = lows
    for _ in range(int(math.log2(c)) - 1):
        powers = [_dotp(pw, pw, NN, PREC["t"]) for pw in powers]
        invs = [inv + _dotp(inv, pw, NN, PREC["t"]) for inv, pw in zip(invs, powers)]
    return invs


def _rwkv_body(zr_ref, shift_ref, s0_ref, mu_ref, w0_ref, wd_ref, a0_ref, wa_ref, wg_ref, kk_ref, ka_ref,
               rk_ref, lng_ref, lnb_ref, hsum_ref,
               o_ref, sout_ref, shout_ref,
               carry_ref, state_ref, r_s, k_s, v_s, kk_s, b_s, ld_s, y_s, bon_s, g_s,
               rw_s, y0_s, gm_s, h0_s, *, tt, c, nseq):
    i = pl.program_id(1)
    rw = RWKV_WIDTH
    nh = RWKV_HEADS

    @pl.when(i == 0)
    def _():
        for q in range(nseq):
            state_ref[q * nh:(q + 1) * nh] = s0_ref[q]

    z = zr_ref[0]
    z_prev = pltpu.roll(z, 1, 0)
    row = lax.broadcasted_iota(jnp.int32, z.shape, 0)
    if nseq == 1:
        @pl.when(i == 0)
        def _():
            carry_ref[...] = shift_ref[0]

        z_prev = jnp.where(row == 0, carry_ref[...], z_prev)
        carry_ref[...] = z[tt - 1:tt, :]
    else:
        first_rows = jnp.concatenate([jnp.broadcast_to(shift_ref[q], (c, SHIFT_WIDTH)) for q in range(nseq)], axis=0)
        z_prev = jnp.where(row % c == 0, first_rows, z_prev)
    zs = z + mu_ref[...] * (z_prev - z)
    r = zs[:, 0:rw]
    k = zs[:, rw:2 * rw]
    v = zs[:, 2 * rw:3 * rw]
    zw = zs[:, 3 * rw:3 * rw + DECAY_LORA]
    za = zs[:, 3 * rw + DECAY_LORA:3 * rw + DECAY_LORA + A_LORA]
    zg = zs[:, 3 * rw + DECAY_LORA + A_LORA:SHIFT_WIDTH]
    xw = w0_ref[...] + _bdot(jnp.tanh(zw), wd_ref[...])
    ld_s[...] = -EXP_M05 * _sigmoid(xw)
    a = _sigmoid(a0_ref[...] + _bdot(za, wa_ref[...]))
    g_s[...] = _bdot(_sigmoid(zg), wg_ref[...])
    kk = k * kk_ref[...]
    k = k * (1.0 + (a - 1.0) * ka_ref[...])
    r_s[...] = r
    k_s[...] = k
    v_s[...] = v

    def head_sum(t):
        return jnp.dot(t.astype(BF16), hsum_ref[...], preferred_element_type=F32)

    kk = kk / jnp.maximum(jnp.sqrt(head_sum(kk * kk)), 1e-12)
    kk_s[...] = kk
    b_s[...] = kk * a
    bon_s[...] = head_sum(r * k * rk_ref[...]) * v

    rows_c = lax.broadcasted_iota(jnp.int32, (c, c), 0)
    cols_c = lax.broadcasted_iota(jnp.int32, (c, c), 1)
    strict = rows_c > cols_c
    incl = rows_c >= cols_c
    tril_ones = incl.astype(BF16)
    rows_2c = lax.broadcasted_iota(jnp.int32, (c, 2 * c), 0)
    cols_2c = lax.broadcasted_iota(jnp.int32, (c, 2 * c), 1)
    incl2 = rows_2c >= jnp.where(cols_2c >= c, cols_2c - c, cols_2c)
    rows_k = lax.broadcasted_iota(jnp.int32, (HEAD_DIM, HEAD_DIM), 0)
    cols_k = lax.broadcasted_iota(jnp.int32, (HEAD_DIM, HEAD_DIM), 1)
    eye_k = (rows_k == cols_k).astype(F32)
    zeros_cv = jnp.zeros((c, HEAD_DIM), F32)
    heads = [slice(h * HEAD_DIM, (h + 1) * HEAD_DIM) for h in range(RWKV_HEADS)]

    n_chunks = tt // c
    group = nseq if nseq > 1 else (2 if n_chunks % 2 == 0 else 1)

    def chunk_rows(ci):
        return pl.ds(ci * c, c) if isinstance(ci, int) else pl.ds(pl.multiple_of(ci * c, c), c)

    def scaled(ci):
        rows = chunk_rows(ci)
        ld = ld_s[rows, :]
        ld_hi, ld_lo = _split(ld)
        cum = (jnp.dot(tril_ones, ld_hi, preferred_element_type=F32)
               + jnp.dot(tril_ones, ld_lo, preferred_element_type=F32))
        e_inc = jnp.exp(cum)
        e_neg = jnp.exp(-cum)
        kt = k_s[rows, :] * e_neg
        bt = b_s[rows, :] * e_neg
        g_end = e_inc[c - 1:c, :]
        return dict(ci=ci, rt=r_s[rows, :] * e_inc, kkt=kk_s[rows, :] * jnp.exp(cum - ld), kt=kt, bt=bt,
                    g_end=g_end, bc=bt * g_end, kc=kt * g_end, vv=v_s[rows, :])

    def phase1(cj, carry):
        chains = [(ch, h, s) for ch in [scaled(cj * group + g) for g in range(group)]
                  for h, s in enumerate(heads)]
        x = [jnp.concatenate([ch["kkt"][:, s], ch["rt"][:, s]], axis=0) for ch, _, s in chains]
        zz = [jnp.concatenate([ch["bt"][:, s], ch["kt"][:, s]], axis=0) for ch, _, s in chains]
        amat = [_dotp(xh, zh, NT, PREC["a"]) for xh, zh in zip(x, zz)]
        l_b = [jnp.where(strict, am[0:c, 0:c], 0.0) for am in amat]
        l_k = [jnp.where(strict, am[0:c, c:2 * c], 0.0) for am in amat]
        a_r = [jnp.where(incl2, am[c:2 * c, :], 0.0) for am in amat]
        tinv = _unit_lower_inverse(l_b, c)
        lkv = [_dotp(lk, ch["vv"][:, s], NN, PREC["lkv"]) for lk, (ch, _, s) in zip(l_k, chains)]
        wu = [-_dotp(t, jnp.concatenate([ch["kkt"][:, s], lv], axis=1), NN, PREC["tw"])
              for t, lv, (ch, _, s) in zip(tinv, lkv, chains)]
        m = [jnp.concatenate([w, jnp.concatenate([zeros_cv, ch["vv"][:, s]], axis=1)], axis=0)
             for w, (ch, _, s) in zip(wu, chains)]
        am2 = [_dotp(ar, mh, NN, PREC["am"]) for ar, mh in zip(a_r, m)]
        gh = [_dotp(mh, jnp.concatenate([ch["bc"][:, s], ch["kc"][:, s]], axis=0), TN, PREC["gh"])
              for mh, (ch, _, s) in zip(m, chains)]
        for q, (ch, h, s) in enumerate(chains):
            idx = ch["ci"] * RWKV_HEADS + h
            rw_s[idx] = ch["rt"][:, s] + am2[q][:, 0:HEAD_DIM]
            y0_s[idx] = am2[q][:, HEAD_DIM:2 * HEAD_DIM]
            gm_s[idx] = gh[q][0:HEAD_DIM, :] + eye_k * ch["g_end"][:, s]
            h0_s[idx] = gh[q][HEAD_DIM:2 * HEAD_DIM, :]
        return carry

    if n_chunks == group:
        phase1(0, 0)
    else:
        lax.fori_loop(0, n_chunks // group, phase1, 0)

    if nseq == 1:
        def phase2(ci, carry):
            rows = chunk_rows(ci)
            for h, s in enumerate(heads):
                idx = ci * nh + h
                s_h = state_ref[h]
                y_s[rows, s] = _dotp(rw_s[idx], s_h, NT, PREC["y"]) + y0_s[idx]
                state_ref[h] = _dotp(s_h, gm_s[idx], NN, PREC["s"]) + h0_s[idx]
            return carry

        lax.fori_loop(0, n_chunks, phase2, 0)
    else:
        pairs = [(q, h, s) for q in range(nseq) for h, s in enumerate(heads)]
        states = [state_ref[q * nh + h] for q, h, _ in pairs]
        ys = [_dotp(rw_s[q * nh + h], st, NT, PREC["y"]) + y0_s[q * nh + h] for (q, h, _), st in zip(pairs, states)]
        new = [_dotp(st, gm_s[q * nh + h], NN, PREC["s"]) + h0_s[q * nh + h] for (q, h, _), st in zip(pairs, states)]
        for (q, h, s), yq, nq in zip(pairs, ys, new):
            y_s[chunk_rows(q), s] = yq
            state_ref[q * nh + h] = nq
    y = y_s[...]
    yc = y - head_sum(y) * (1.0 / HEAD_DIM)
    var = head_sum(yc * yc) * (1.0 / HEAD_DIM)
    yn = yc * lax.rsqrt(var + GN_EPS)
    o_ref[0] = ((yn * lng_ref[...] + lnb_ref[...] + bon_s[...]) * g_s[...]).astype(o_ref.dtype)

    @pl.when(i == pl.num_programs(1) - 1)
    def _():
        for q in range(nseq):
            sout_ref[q] = state_ref[q * nh:(q + 1) * nh]
            last = tt - 1 if nseq == 1 else (q + 1) * c - 1
            shout_ref[q] = zr_ref[0, last:last + 1, :]


def _rwkv(zr3, blk_off, b, nt, shift_prev, s_prev, p, tt, c, nseq):
    vec = lambda n: pl.BlockSpec((1, n), lambda bi, ti: (0, 0))
    mat = lambda m, n: pl.BlockSpec((m, n), lambda bi, ti: (0, 0))
    rw = RWKV_WIDTH
    scr = lambda: pltpu.VMEM((tt, rw), F32)
    nch = (tt // c) * RWKV_HEADS
    return pl.pallas_call(
        functools.partial(_rwkv_body, tt=tt, c=c, nseq=nseq),
        grid=(b // nseq, nt),
        in_specs=[
            pl.BlockSpec((1, tt, SHIFT_WIDTH), lambda bi, ti: (blk_off + bi * nt + ti, 0, 0)),
            pl.BlockSpec((nseq, 1, SHIFT_WIDTH), lambda bi, ti: (bi, 0, 0)),
            pl.BlockSpec((nseq, RWKV_HEADS, HEAD_DIM, HEAD_DIM), lambda bi, ti: (bi, 0, 0, 0)),
            vec(SHIFT_WIDTH), vec(rw), mat(DECAY_LORA, rw), vec(rw), mat(A_LORA, rw), mat(GATE_LORA, rw),
            vec(rw), vec(rw), vec(rw), vec(rw), vec(rw), mat(rw, rw),
        ],
        out_specs=[
            pl.BlockSpec((1, tt, rw), lambda bi, ti: (bi, ti, 0)),
            pl.BlockSpec((nseq, RWKV_HEADS, HEAD_DIM, HEAD_DIM), lambda bi, ti: (bi, 0, 0, 0)),
            pl.BlockSpec((nseq, 1, SHIFT_WIDTH), lambda bi, ti: (bi, 0, 0)),
        ],
        out_shape=[
            jax.ShapeDtypeStruct((b // nseq, nt * tt, rw), BF16),
            jax.ShapeDtypeStruct((b, RWKV_HEADS, HEAD_DIM, HEAD_DIM), F32),
            jax.ShapeDtypeStruct((b, 1, SHIFT_WIDTH), F32),
        ],
        scratch_shapes=[
            pltpu.VMEM((1, SHIFT_WIDTH), F32),
            pltpu.VMEM((nseq * RWKV_HEADS, HEAD_DIM, HEAD_DIM), F32),
            scr(), scr(), scr(), scr(), scr(), scr(), scr(), scr(), scr(),
            pltpu.VMEM((nch, c, HEAD_DIM), F32),
            pltpu.VMEM((nch, c, HEAD_DIM), F32),
            pltpu.VMEM((nch, HEAD_DIM, HEAD_DIM), F32),
            pltpu.VMEM((nch, HEAD_DIM, HEAD_DIM), F32),
        ],
        compiler_params=_params(("parallel", "arbitrary")),
        name="rwkv",
    )(zr3, shift_prev, s_prev, p["mu"], p["w0"], p["wd"], p["a0"], p["wa"], p["wg"], p["kk"], p["ka"],
      p["rk"], p["lng"], p["lnb"], p["hsum"])


def _gmlp_body(z_ref, lng_ref, lnb_ref, ws_ref, bs_ref, o_ref, v_ref):
    z = z_ref[...]
    ge = 0.5 * z * (1.0 + lax.erf(z * (1.0 / math.sqrt(2.0))))
    u = ge[:, 0:GMLP_WIDTH]
    v = ge[:, GMLP_WIDTH:2 * GMLP_WIDTH]
    mean = jnp.mean(v, axis=-1, keepdims=True)
    vc = v - mean
    var = jnp.mean(vc * vc, axis=-1, keepdims=True)
    vn = vc * lax.rsqrt(var + LN_EPS) * lng_ref[...] + lnb_ref[...]
    v_ref[...] = vn
    gd = GMLP_WIDTH // GMLP_GROUPS
    for q in range(GMLP_TILE):
        rows = slice(q * CHUNK, (q + 1) * CHUNK)
        for g in range(GMLP_GROUPS):
            gs = slice(g * gd, (g + 1) * gd)
            sv = jnp.dot(ws_ref[g], vn[rows, gs].astype(BF16), preferred_element_type=F32) + bs_ref[:, g:g + 1]
            o_ref[rows, gs] = (u[rows, gs] * sv).astype(o_ref.dtype)


def _gmlp(zg, tile_off, n_rows, lng, lnb, ws_bf16, bs):
    tile = GMLP_TILE * CHUNK
    return pl.pallas_call(
        _gmlp_body,
        grid=(n_rows // tile,),
        in_specs=[
            pl.BlockSpec((tile, 2 * GMLP_WIDTH), lambda i: (tile_off + i, 0)),
            pl.BlockSpec((1, GMLP_WIDTH), lambda i: (0, 0)),
            pl.BlockSpec((1, GMLP_WIDTH), lambda i: (0, 0)),
            pl.BlockSpec((GMLP_GROUPS, CHUNK, CHUNK), lambda i: (0, 0, 0)),
            pl.BlockSpec((CHUNK, GMLP_GROUPS), lambda i: (0, 0)),
        ],
        out_specs=[
            pl.BlockSpec((tile, GMLP_WIDTH), lambda i: (i, 0)),
            pl.BlockSpec((tile, GMLP_WIDTH), lambda i: (i, 0)),
        ],
        out_shape=[
            jax.ShapeDtypeStruct((n_rows, GMLP_WIDTH), BF16),
            jax.ShapeDtypeStruct((n_rows, GMLP_WIDTH), F32),
        ],
        compiler_params=_params(("parallel",)),
        name="gmlp",
    )(zg, lng, lnb, ws_bf16, bs)


def _mem_attn_body(q_ref, k_ref, v_ref, o_ref, *, bb, feature_major):
    hd = MEM_WIDTH // MEM_HEADS
    pairs = [(s, slice(h * hd, (h + 1) * hd)) for s in range(bb) for h in range(MEM_HEADS)]
    if feature_major:
        scores = [_bdot(q_ref[s, :, hs], k_ref[s, hs, :], NN) * (hd ** -0.5) for s, hs in pairs]
    else:
        scores = [_bdot(q_ref[s, :, hs], k_ref[s, :, hs], NT) * (hd ** -0.5) for s, hs in pairs]
    probs = []
    for sc in scores:
        pr = jnp.exp(sc - jnp.max(sc, axis=-1, keepdims=True))
        probs.append(pr / jnp.sum(pr, axis=-1, keepdims=True))
    for (s, hs), pr in zip(pairs, probs):
        if feature_major:
            o_ref[s, :, hs] = _bdot(pr, v_ref[s, hs, :], NT).astype(o_ref.dtype)
        else:
            o_ref[s, :, hs] = _bdot(pr, v_ref[s, :, hs]).astype(o_ref.dtype)


def _mem_attn(q3, blk_off, b, nt, mk, mv, tt, bb, feature_major):
    return pl.pallas_call(
        functools.partial(_mem_attn_body, bb=bb, feature_major=feature_major),
        grid=(b // bb, nt),
        in_specs=[
            pl.BlockSpec((bb, tt, MEM_WIDTH), lambda bi, ti: (blk_off // bb + bi * nt + ti, 0, 0)),
            pl.BlockSpec((bb, N_MEM, MEM_WIDTH), lambda bi, ti: (bi, 0, 0)),
            pl.BlockSpec((bb, N_MEM, MEM_WIDTH), lambda bi, ti: (bi, 0, 0)),
        ],
        out_specs=pl.BlockSpec((bb, tt, MEM_WIDTH), lambda bi, ti: (bi, ti, 0)),
        out_shape=jax.ShapeDtypeStruct((b, nt * tt, MEM_WIDTH), BF16),
        compiler_params=_params(("parallel", "parallel")),
        name="mem_attn",
    )(q3, mk, mv)


def _merge_body(x_p, x_s, orw_p, orw_s, ogm_p, ogm_s, ome_p, ome_s, sg_ref, wbr_ref, wbg_ref, wbm_ref, wout_ref,
                gffn_ref, wrt_ref, brt_ref, x2_ref, h2_ref, route_ref, route_t_ref, cnt_ref, count_s, *, np_tiles):
    i = pl.program_id(0)
    d = D_MODEL
    tm = x_p.shape[0]
    is_p = i < np_tiles
    x = jnp.where(is_p, x_p[...], x_s[...])
    orw = jnp.where(is_p, orw_p[...], orw_s[...])
    ogm = jnp.where(is_p, ogm_p[...], ogm_s[...])
    ome = jnp.where(is_p, ome_p[...], ome_s[...])
    merged = sg_ref[:, 0:d].astype(F32) * _bdot(orw, wbr_ref[...])
    merged = merged + sg_ref[:, d:2 * d].astype(F32) * _bdot(ogm, wbg_ref[...])
    merged = merged + sg_ref[:, 2 * d:3 * d].astype(F32) * _bdot(ome, wbm_ref[...])
    x2 = x + _bdot(merged, wout_ref[...])
    x2_ref[...] = x2
    h2 = _rms(x2, gffn_ref[...])
    _store_row_tiles(h2_ref, h2)

    @pl.when(i == 0)
    def _():
        count_s[...] = jnp.zeros_like(count_s)

    lane = lax.broadcasted_iota(jnp.int32, (tm, LANES), 1)
    logits = jnp.where(lane < N_EXPERTS, _dot3(h2, wrt_ref[...]) + brt_ref[...], -jnp.inf)
    lane_f = lane.astype(F32)
    tops, hots, idxs = [], [], []
    for _ in range(TOP_K):
        top = jnp.max(logits, axis=-1, keepdims=True)
        idx = jnp.min(jnp.where(logits == top, lane_f, float(LANES)), axis=-1, keepdims=True)
        hot = lane_f == idx
        logits = jnp.where(hot, -jnp.inf, logits)
        tops.append(top)
        hots.append(hot)
        idxs.append(idx)
    weights = [jnp.exp(t - tops[0]) for t in tops]
    denom = weights[0] + weights[1] + weights[2] + weights[3]
    onehot = jnp.zeros((tm, LANES), F32)
    for hot in hots:
        onehot = onehot + jnp.where(hot, 1.0, 0.0)
    rows_t = lax.broadcasted_iota(jnp.int32, (tm, tm), 0)
    cols_t = lax.broadcasted_iota(jnp.int32, (tm, tm), 1)
    ahead = jnp.where(rows_t > cols_t, 1.0, 0.0).astype(BF16)
    prefix = jnp.dot(ahead, onehot.astype(BF16), preferred_element_type=F32) + count_s[...]
    route = jnp.zeros((tm, LANES), F32)
    for kk in range(TOP_K):
        rank = jnp.sum(jnp.where(hots[kk], prefix, 0.0), axis=-1, keepdims=True)
        route = jnp.where(lane == ROUTE_E + kk, idxs[kk], route)
        route = jnp.where(lane == ROUTE_RANK + kk, rank, route)
        route = jnp.where(lane == ROUTE_GATE + kk, weights[kk] / denom, route)
    route_ref[...] = route
    route_t_ref[...] = route.T
    count_s[...] = count_s[...] + jnp.sum(onehot, axis=0, keepdims=True)
    cnt_ref[...] = jnp.broadcast_to(count_s[...], cnt_ref.shape)


def _merge(x, o_rw, o_gm, o_me, sg, wbr, wbg, wbm, wout, gffn, wrt_pad, brt_pad):
    n_p = x[0].shape[0]
    n = n_p + x[1].shape[0]
    tm = PROJ_TM
    d = D_MODEL
    np_tiles = n_p // tm
    row = lambda i: (i, 0)
    const = lambda i: (0, 0)
    first = lambda i: (jnp.minimum(i, np_tiles - 1), 0)
    second = lambda i: (jnp.maximum(i - np_tiles, 0), 0)

    def pair(width):
        return [pl.BlockSpec((tm, width), first), pl.BlockSpec((tm, width), second)]

    return pl.pallas_call(
        functools.partial(_merge_body, np_tiles=np_tiles),
        grid=(n // tm,),
        in_specs=pair(d) + pair(RWKV_WIDTH) + pair(GMLP_WIDTH) + pair(MEM_WIDTH) + [
            pl.BlockSpec((tm, 3 * d), row),
            pl.BlockSpec((RWKV_WIDTH, d), const),
            pl.BlockSpec((GMLP_WIDTH, d), const),
            pl.BlockSpec((MEM_WIDTH, d), const),
            pl.BlockSpec((d, d), const),
            pl.BlockSpec((1, d), const),
            pl.BlockSpec((d, LANES), const),
            pl.BlockSpec((1, LANES), const),
        ],
        out_specs=[
            pl.BlockSpec((tm, d), row),
            pl.BlockSpec((tm * ROW_SUB, LANES), row),
            pl.BlockSpec((tm, LANES), row),
            pl.BlockSpec((LANES, tm), lambda i: (0, i)),
            pl.BlockSpec((8, LANES), const),
        ],
        out_shape=[
            jax.ShapeDtypeStruct((n, d), F32),
            jax.ShapeDtypeStruct((n * ROW_SUB, LANES), F32),
            jax.ShapeDtypeStruct((n, LANES), F32),
            jax.ShapeDtypeStruct((LANES, n), F32),
            jax.ShapeDtypeStruct((8, LANES), F32),
        ],
        scratch_shapes=[pltpu.VMEM((1, LANES), F32)],
        compiler_params=_params(("arbitrary",)),
        name="merge",
    )(x[0], x[1], o_rw[0], o_rw[1], o_gm[0], o_gm[1], o_me[0], o_me[1], sg, wbr, wbg, wbm, wout, gffn, wrt_pad,
      brt_pad)


N_ZERO_BLOCKS = 2 * N_EXPERTS


def _dispatch_body(dest_ref, zlist_ref, h_ref, xs_hbm, zero_s, sem_z, sem):
    i = pl.program_id(0)
    tm = DISPATCH_TM
    bm = MOE_BM

    @pl.when(i == 0)
    def _():
        zero_s[...] = jnp.zeros_like(zero_s)

        def zero_copy(q):
            start = pl.multiple_of(zlist_ref[q] * bm, bm)
            return pltpu.make_async_copy(zero_s, xs_hbm.at[pl.ds(start, bm)], sem_z)

        def start(q, carry):
            @pl.when(zlist_ref[q] >= 0)
            def _():
                zero_copy(q).start()
            return carry

        def wait(q, carry):
            @pl.when(zlist_ref[q] >= 0)
            def _():
                zero_copy(q).wait()
            return carry

        lax.fori_loop(0, N_ZERO_BLOCKS, start, 0)
        lax.fori_loop(0, N_ZERO_BLOCKS, wait, 0)

    base = i * tm
    n_tok = pl.num_programs(0) * tm

    def body(r, carry):
        for kk in range(TOP_K):
            slot = dest_ref[kk * n_tok + base + r]
            pltpu.make_async_copy(h_ref.at[r], xs_hbm.at[slot], sem).start(priority=kk % 2)
        return carry

    lax.fori_loop(0, tm, body, 0, unroll=2)
    for kk in range(TOP_K):
        pltpu.make_async_copy(h_ref, xs_hbm.at[pl.ds(0, tm)], sem).wait()


def _dispatch(dest, zlist, h, n_blocks):
    n = h.shape[0]
    tm = DISPATCH_TM
    grid_spec = pltpu.PrefetchScalarGridSpec(
        num_scalar_prefetch=2,
        grid=(n // tm,),
        in_specs=[pl.BlockSpec((tm, ROW_SUB, LANES), lambda i, dest, zl: (i, 0, 0))],
        out_specs=pl.BlockSpec(memory_space=pl.ANY),
        scratch_shapes=[
            pltpu.VMEM((MOE_BM, ROW_SUB, LANES), F32),
            pltpu.SemaphoreType.DMA(()),
            pltpu.SemaphoreType.DMA(()),
        ],
    )
    return pl.pallas_call(
        _dispatch_body,
        grid_spec=grid_spec,
        out_shape=jax.ShapeDtypeStruct((n_blocks * MOE_BM, ROW_SUB, LANES), F32),
        compiler_params=_params(("arbitrary",)),
        name="moe_dispatch",
    )(dest, zlist, h)


CAST_ROWS = 64


def _moe_body(be_ref, nused_ref, x_ref, w1_ref, b1_ref, w2_ref, b2_ref, o_ref, w1b, w2b):
    j = pl.program_id(0)
    n_used = nused_ref[0]

    @pl.when(j < n_used)
    def _():
        changed = jnp.logical_or(j == 0, be_ref[j] != be_ref[jnp.maximum(j - 1, 0)])

        @pl.when(changed)
        def _():
            def cast1(q, carry):
                rows = pl.ds(pl.multiple_of(q * CAST_ROWS, CAST_ROWS), CAST_ROWS)
                w1b[rows, :] = w1_ref[0, rows, :].astype(BF16)
                return carry

            def cast2(q, carry):
                rows = pl.ds(pl.multiple_of(q * CAST_ROWS, CAST_ROWS), CAST_ROWS)
                w2b[rows, :] = w2_ref[0, rows, :].astype(BF16)
                return carry

            lax.fori_loop(0, D_MODEL // CAST_ROWS, cast1, 0)
            lax.fori_loop(0, D_FF // CAST_ROWS, cast2, 0)

        x = _load_row_tiles(x_ref, MOE_BM).astype(BF16)
        z = jnp.dot(x, w1b[...], preferred_element_type=F32) + b1_ref[0]
        zg = jnp.minimum(z[:, 0:D_FF], SWIGLU_LIMIT)
        zl = jnp.clip(z[:, D_FF:2 * D_FF], -SWIGLU_LIMIT, SWIGLU_LIMIT)
        act = zg * _sigmoid(SWIGLU_ALPHA * zg) * (zl + 1.0)
        _store_row_tiles(o_ref, jnp.dot(act.astype(BF16), w2b[...], preferred_element_type=F32) + b2_ref[0])

    @pl.when(j >= n_used)
    def _():
        o_ref[...] = jnp.zeros_like(o_ref)


def _moe(blk_e, n_used, xs, w1, b1, w2, b2, n_blocks):
    bm = MOE_BM
    d = D_MODEL
    grid_spec = pltpu.PrefetchScalarGridSpec(
        num_scalar_prefetch=2,
        grid=(n_blocks,),
        in_specs=[
            pl.BlockSpec((bm * ROW_SUB, LANES), lambda j, be, nu: (j, 0)),
            pl.BlockSpec((1, d, 2 * D_FF), lambda j, be, nu: (be[j], 0, 0)),
            pl.BlockSpec((1, 1, 2 * D_FF), lambda j, be, nu: (be[j], 0, 0)),
            pl.BlockSpec((1, D_FF, d), lambda j, be, nu: (be[j], 0, 0)),
            pl.BlockSpec((1, 1, d), lambda j, be, nu: (be[j], 0, 0)),
        ],
        out_specs=pl.BlockSpec((bm * ROW_SUB, LANES), lambda j, be, nu: (j, 0)),
        scratch_shapes=[
            pltpu.VMEM((d, 2 * D_FF), BF16),
            pltpu.VMEM((D_FF, d), BF16),
        ],
    )
    return pl.pallas_call(
        _moe_body,
        grid_spec=grid_spec,
        out_shape=jax.ShapeDtypeStruct((n_blocks * bm * ROW_SUB, LANES), F32),
        compiler_params=_params(("arbitrary",)),
        name="moe_ffn",
    )(blk_e, n_used, xs, w1, b1, w2, b2)


def _combine_body(dest_ref, x2_ref, route_ref, gfin_ref, yb_hbm, op_ref, os_ref, buf, sem, *, np_tiles):
    i = pl.program_id(0)
    nb = pl.num_programs(0)
    tm = COMBINE_TM

    def issue(blk, slot):
        base = blk * tm
        n_tok = nb * tm

        def body(r, carry):
            for kk in range(TOP_K):
                d = dest_ref[kk * n_tok + base + r]
                src = pl.ds(pl.multiple_of(d * ROW_SUB, ROW_SUB), ROW_SUB)
                dst = pl.ds(pl.multiple_of(r * ROW_SUB, ROW_SUB), ROW_SUB)
                pltpu.make_async_copy(yb_hbm.at[src], buf.at[slot, kk, dst], sem.at[slot]).start(priority=kk % 2)
            return carry

        lax.fori_loop(0, tm, body, 0, unroll=2)

    slot = lax.rem(i, 2)

    @pl.when(i == 0)
    def _():
        issue(0, 0)

    @pl.when(i + 1 < nb)
    def _():
        issue(i + 1, 1 - slot)

    acc = x2_ref[...]
    for kk in range(TOP_K):
        pltpu.make_async_copy(yb_hbm.at[pl.ds(0, tm * ROW_SUB)], buf.at[slot, kk], sem.at[slot]).wait()
    for kk in range(TOP_K):
        acc = acc + route_ref[:, ROUTE_GATE + kk:ROUTE_GATE + kk + 1] * _load_row_tiles(buf.at[slot, kk], tm)
    y = _rms(acc, gfin_ref[...])

    @pl.when(i < np_tiles)
    def _():
        op_ref[...] = y

    @pl.when(i >= np_tiles)
    def _():
        os_ref[...] = y


def _combine(dest, x2, route, gfin, yb, n_p):
    n, d = x2.shape
    tm = COMBINE_TM
    np_tiles = n_p // tm
    grid_spec = pltpu.PrefetchScalarGridSpec(
        num_scalar_prefetch=1,
        grid=(n // tm,),
        in_specs=[
            pl.BlockSpec((tm, d), lambda i, dest: (i, 0)),
            pl.BlockSpec((tm, LANES), lambda i, dest: (i, 0)),
            pl.BlockSpec((1, d), lambda i, dest: (0, 0)),
            pl.BlockSpec(memory_space=pl.ANY),
        ],
        out_specs=[
            pl.BlockSpec((tm, d), lambda i, dest: (jnp.minimum(i, np_tiles - 1), 0)),
            pl.BlockSpec((tm, d), lambda i, dest: (jnp.maximum(i - np_tiles, 0), 0)),
        ],
        scratch_shapes=[
            pltpu.VMEM((2, TOP_K, tm * ROW_SUB, LANES), F32),
            pltpu.SemaphoreType.DMA((2,)),
        ],
    )
    return pl.pallas_call(
        functools.partial(_combine_body, np_tiles=np_tiles),
        grid_spec=grid_spec,
        out_shape=[jax.ShapeDtypeStruct((n_p, d), F32), jax.ShapeDtypeStruct((n - n_p, d), F32)],
        compiler_params=_params(("arbitrary",)),
        name="moe_combine",
    )(dest, x2, route, gfin, yb)


def _slot_tables(route_t, counts_row, n_blocks):
    bm = MOE_BM
    e = route_t[ROUTE_E:ROUTE_E + TOP_K].astype(jnp.int32)
    rank = route_t[ROUTE_RANK:ROUTE_RANK + TOP_K].astype(jnp.int32)
    counts = counts_row[:N_EXPERTS].astype(jnp.int32)
    padded = (counts + bm - 1) // bm * bm
    pad_end = jnp.cumsum(padded)
    pad_start = pad_end - padded
    experts = jnp.arange(N_EXPERTS, dtype=jnp.int32)[:, None, None]
    dest = (jnp.sum(jnp.where(e[None] == experts, pad_start[:, None, None], 0), axis=0) + rank).reshape(-1)
    blk_start = jnp.arange(n_blocks, dtype=jnp.int32) * bm
    blk_e = jnp.minimum(jnp.sum(pad_end[None, :] <= blk_start[:, None], axis=1), N_EXPERTS - 1).astype(jnp.int32)
    n_used = pad_end[-1] // bm
    last_blk = jnp.where(padded > 0, pad_end // bm - 1, -1)
    trailing = n_used + jnp.arange(N_EXPERTS, dtype=jnp.int32)
    trailing = jnp.where(trailing < n_blocks, trailing, -1)
    zlist = jnp.concatenate([last_blk, trailing]).astype(jnp.int32)
    return dest.astype(jnp.int32), blk_e, n_used.astype(jnp.int32).reshape(1), zlist


def kernel(x_prompt, x_sample, mem_prompt, state_shift, state_wkv, cache_mem_k, cache_mem_v, g_norm_mix, w_in, mu_shift, w0, w_decay_up, a0, w_a_up, w_g_up, k_k, k_a, r_k, ln_x_g, ln_x_b, gmlp_ln_g, gmlp_ln_b, w_spatial, b_spatial, g_norm_mem, w_mem_kv, w_br_rwkv, w_br_gmlp, w_br_mem, w_out, g_norm_ffn, w_router, b_router, w_exp1, b_exp1, w_exp2, b_exp2, g_norm_final):
    bp, tp, d = x_prompt.shape
    bs, ts, _ = x_sample.shape
    n_p, n_s = bp * tp, bs * ts
    n_all = n_p + n_s
    l = 0
    row = lambda a: a.reshape(1, -1)

    x_pair = (x_prompt.reshape(n_p, d), x_sample.reshape(n_s, d))
    zr, zg, zq, sg = _in_proj(x_pair[0], x_pair[1], row(g_norm_mix[l]), w_in[l].astype(BF16))

    mk_p, mv_p = _mem_kv(mem_prompt.reshape(bp * N_MEM, d), row(g_norm_mem[l]), w_mem_kv[l].astype(BF16))

    rp = dict(mu=row(mu_shift[l]), w0=row(w0[l]), wd=w_decay_up[l].astype(BF16), a0=row(a0[l]),
              wa=w_a_up[l].astype(BF16), wg=w_g_up[l].astype(BF16), kk=row(k_k[l]), ka=row(k_a[l]),
              rk=row(r_k[l]), lng=row(ln_x_g[l]), lnb=row(ln_x_b[l]))
    head_of = jnp.arange(RWKV_WIDTH, dtype=jnp.int32) // HEAD_DIM
    rp["hsum"] = (head_of[:, None] == head_of[None, :]).astype(BF16)
    o_rw_p, s_p, shift_p = _rwkv(zr.reshape(n_all // RWKV_TT, RWKV_TT, SHIFT_WIDTH), 0, bp, tp // RWKV_TT,
                        jnp.zeros((bp, 1, SHIFT_WIDTH), F32), jnp.zeros((bp, RWKV_HEADS, HEAD_DIM, HEAD_DIM), F32),
                        rp, tt=RWKV_TT, c=RWKV_C, nseq=1)
    tile_s = RWKV_SAMPLE_SEQS * ts
    o_rw_s, s_s, shift_s = _rwkv(zr.reshape(n_all // tile_s, tile_s, SHIFT_WIDTH), n_p // tile_s, bs, 1,
                        state_shift[l].reshape(bs, 1, SHIFT_WIDTH), state_wkv[l], rp, tt=tile_s, c=ts,
                        nseq=RWKV_SAMPLE_SEQS)

    tri = jnp.tril(jnp.ones((CHUNK, CHUNK), bool))
    ws_p = jnp.where(tri, w_spatial[l], 0.0).astype(BF16)
    bs_p = b_spatial[l].T
    reps = CHUNK // ts
    tri_s = jnp.tril(jnp.ones((ts, ts), bool))
    ws_small = jnp.where(tri_s, w_spatial[l][:, :ts, :ts], 0.0)
    eye = jnp.eye(reps, dtype=F32)
    ws_s = jnp.einsum("ab,gij->gaibj", eye, ws_small).reshape(GMLP_GROUPS, CHUNK, CHUNK).astype(BF16)
    bs_s = jnp.tile(b_spatial[l][:, :ts], (1, reps)).T
    lng, lnb = row(gmlp_ln_g[l]), row(gmlp_ln_b[l])
    o_gm_p, _ = _gmlp(zg, 0, n_p, lng, lnb, ws_p, bs_p)
    o_gm_s, v_rows_s = _gmlp(zg, n_p // (GMLP_TILE * CHUNK), n_s, lng, lnb, ws_s, bs_s)

    o_me_p = _mem_attn(zq.reshape(n_all // ATTN_TT, ATTN_TT, MEM_WIDTH), 0, bp, tp // ATTN_TT,
                       mk_p.reshape(bp, N_MEM, MEM_WIDTH), mv_p.reshape(bp, N_MEM, MEM_WIDTH), tt=ATTN_TT, bb=1,
                       feature_major=False)
    mk_s = jnp.transpose(cache_mem_k[l].reshape(bs, N_MEM, MEM_WIDTH), (0, 2, 1))
    mv_s = jnp.transpose(cache_mem_v[l].reshape(bs, N_MEM, MEM_WIDTH), (0, 2, 1))
    o_me_s = _mem_attn(zq.reshape(n_all // ts, ts, MEM_WIDTH), n_p // ts, bs, 1, mk_s, mv_s, tt=ts, bb=ATTN_BB,
                       feature_major=True)

    wrt_pad = jnp.zeros((d, LANES), F32).at[:, :N_EXPERTS].set(w_router[l])
    brt_pad = jnp.zeros((1, LANES), F32).at[0, :N_EXPERTS].set(b_router[l])
    x2, h2, route, route_t, counts = _merge(
        x_pair, (o_rw_p.reshape(n_p, RWKV_WIDTH), o_rw_s.reshape(n_s, RWKV_WIDTH)), (o_gm_p, o_gm_s),
        (o_me_p.reshape(n_p, MEM_WIDTH), o_me_s.reshape(n_s, MEM_WIDTH)), sg,
        w_br_rwkv[l].astype(BF16), w_br_gmlp[l].astype(BF16), w_br_mem[l].astype(BF16), w_out[l].astype(BF16),
        row(g_norm_ffn[l]), wrt_pad, brt_pad)

    n_assign = n_all * TOP_K
    n_blocks = -(-(n_assign + N_EXPERTS * (MOE_BM - 1)) // MOE_BM)
    dest, blk_e, n_used, zlist = _slot_tables(route_t, counts[0], n_blocks)
    xs = _dispatch(dest, zlist, h2.reshape(n_all, ROW_SUB, LANES), n_blocks)
    yb = _moe(blk_e, n_used, xs.reshape(n_blocks * MOE_BM * ROW_SUB, LANES), w_exp1[l], b_exp1[l].reshape(N_EXPERTS, 1, 2 * D_FF),
              w_exp2[l], b_exp2[l].reshape(N_EXPERTS, 1, d), n_blocks)
    y_p, y_s = _combine(dest, x2, route, row(g_norm_final), yb, n_p)

    mk_out = mk_p.reshape(1, bp, N_MEM, MEM_HEADS, MEM_WIDTH // MEM_HEADS)
    mv_out = mv_p.reshape(1, bp, N_MEM, MEM_HEADS, MEM_WIDTH // MEM_HEADS)
    return (y_p.reshape(bp, tp, d), y_s.reshape(bs, ts, d), shift_p.reshape(1, bp, SHIFT_WIDTH), s_p[None], mk_out,
            mv_out, shift_s.reshape(1, bs, SHIFT_WIDTH), s_s[None],
            v_rows_s.reshape(1, bs, ts, GMLP_WIDTH))
```

```python
import functools
import math

import jax
import jax.numpy as jnp
from jax import lax
from jax.experimental import pallas as pl
from jax.experimental.pallas import tpu as pltpu

F32 = jnp.float32
BF16 = jnp.bfloat16

D_MODEL = 1024
RWKV_HEADS = 8
HEAD_DIM = 64
RWKV_WIDTH = RWKV_HEADS * HEAD_DIM
DECAY_LORA = 64
A_LORA = 64
GATE_LORA = 128
GMLP_GROUPS = 4
GMLP_WIDTH = 256
CHUNK = 128
MEM_HEADS = 4
MEM_WIDTH = 256
N_MEM = 256
N_EXPERTS = 32
TOP_K = 4
D_FF = 1024
SWIGLU_ALPHA = 1.702
SWIGLU_LIMIT = 7.0
RMS_EPS = 1e-5
LN_EPS = 1e-5
GN_EPS = 64e-5
SHIFT_WIDTH = 3 * RWKV_WIDTH + DECAY_LORA + A_LORA + GATE_LORA
OFF_GMLP = SHIFT_WIDTH
OFF_QMEM = OFF_GMLP + 2 * GMLP_WIDTH
OFF_GATE = OFF_QMEM + MEM_WIDTH
IN_WIDTH = OFF_GATE + 3 * D_MODEL
LANES = 128
ROW_SUB = D_MODEL // LANES

PROJ_TM = 256
MERGE_TM = 512
RWKV_TT = 512
RWKV_C = 64
RWKV_GROUP = 4
RWKV_SAMPLE_SEQS = 8
GMLP_TILE = 4
ATTN_TT = 512
ATTN_BB = 8
MOE_BM = 512
DISPATCH_TM = 256
COMBINE_TM = 128

ROUTE_E = 0
ROUTE_RANK = 4
ROUTE_GATE = 8

NN = ((1,), (0,))
NT = ((1,), (1,))
TN = ((0,), (0,))

VMEM_LIMIT = 56 * 1024 * 1024


def _params(sem, vmem=VMEM_LIMIT):
    return pltpu.CompilerParams(dimension_semantics=sem, vmem_limit_bytes=vmem)


def _bdot(a, b, dims=NN):
    return lax.dot_general(a.astype(BF16), b.astype(BF16), (dims, ((), ())), preferred_element_type=F32)


def _split(x):
    hi = x.astype(BF16)
    lo = (x - hi.astype(F32)).astype(BF16)
    return hi, lo


def _dot3(a, b, dims=NN):
    dn = (dims, ((), ()))
    ah, al = _split(a)
    bh, bl = _split(b)
    r = lax.dot_general(ah, bh, dn, preferred_element_type=F32)
    r = r + lax.dot_general(al, bh, dn, preferred_element_type=F32)
    return r + lax.dot_general(ah, bl, dn, preferred_element_type=F32)


def _rms(x, g):
    return x * lax.rsqrt(jnp.mean(x * x, axis=-1, keepdims=True) + RMS_EPS) * g


def _store_row_tiles(ref, x):
    m = x.shape[0]
    for j in range(ROW_SUB):
        ref[pl.ds(j, m, stride=ROW_SUB), :] = x[:, j * LANES:(j + 1) * LANES]


def _load_row_tiles(ref, m):
    return jnp.concatenate([ref[pl.ds(j, m, stride=ROW_SUB), :] for j in range(ROW_SUB)], axis=1)


def _sigmoid(x):
    return 0.5 * jnp.tanh(0.5 * x) + 0.5


def _in_proj_body(xp_ref, xs_ref, g_ref, w_ref, zr_ref, zg_ref, zq_ref, sg_ref, *, np_tiles):
    x = jnp.where(pl.program_id(0) < np_tiles, xp_ref[...], xs_ref[...])
    h = _rms(x, g_ref[...]).astype(BF16)
    zr_ref[...] = jnp.dot(h, w_ref[:, 0:OFF_GMLP], preferred_element_type=F32)
    zg_ref[...] = jnp.dot(h, w_ref[:, OFF_GMLP:OFF_QMEM], preferred_element_type=F32)
    zq_ref[...] = jnp.dot(h, w_ref[:, OFF_QMEM:OFF_GATE], preferred_element_type=F32).astype(zq_ref.dtype)
    gates = jnp.dot(h, w_ref[:, OFF_GATE:IN_WIDTH], preferred_element_type=F32)
    sg_ref[...] = _sigmoid(gates).astype(BF16)


def _in_proj(x_p, x_s, g, w_bf16):
    n_p = x_p.shape[0]
    n = n_p + x_s.shape[0]
    tm = PROJ_TM
    np_tiles = n_p // tm
    row = lambda i: (i, 0)
    const = lambda i: (0, 0)
    return pl.pallas_call(
        functools.partial(_in_proj_body, np_tiles=np_tiles),
        grid=(n // tm,),
        in_specs=[
            pl.BlockSpec((tm, D_MODEL), lambda i: (jnp.minimum(i, np_tiles - 1), 0)),
            pl.BlockSpec((tm, D_MODEL), lambda i: (jnp.maximum(i - np_tiles, 0), 0)),
            pl.BlockSpec((1, D_MODEL), const),
            pl.BlockSpec((D_MODEL, IN_WIDTH), const, pipeline_mode=pl.Buffered(1)),
        ],
        out_specs=[
            pl.BlockSpec((tm, SHIFT_WIDTH), row),
            pl.BlockSpec((tm, 2 * GMLP_WIDTH), row),
            pl.BlockSpec((tm, MEM_WIDTH), row),
            pl.BlockSpec((tm, 3 * D_MODEL), row),
        ],
        out_shape=[
            jax.ShapeDtypeStruct((n, SHIFT_WIDTH), F32),
            jax.ShapeDtypeStruct((n, 2 * GMLP_WIDTH), F32),
            jax.ShapeDtypeStruct((n, MEM_WIDTH), BF16),
            jax.ShapeDtypeStruct((n, 3 * D_MODEL), BF16),
        ],
        compiler_params=_params(("parallel",)),
        name="in_proj",
    )(x_p, x_s, g, w_bf16)


def _mem_kv_body(x_ref, g_ref, w_ref, k_ref, v_ref):
    h = _rms(x_ref[...], g_ref[...]).astype(BF16)
    k_ref[...] = jnp.dot(h, w_ref[:, 0:MEM_WIDTH], preferred_element_type=F32)
    v_ref[...] = jnp.dot(h, w_ref[:, MEM_WIDTH:2 * MEM_WIDTH], preferred_element_type=F32)


def _mem_kv(mem, g, w_bf16):
    n = mem.shape[0]
    tm = PROJ_TM
    return pl.pallas_call(
        _mem_kv_body,
        grid=(n // tm,),
        in_specs=[
            pl.BlockSpec((tm, D_MODEL), lambda i: (i, 0)),
            pl.BlockSpec((1, D_MODEL), lambda i: (0, 0)),
            pl.BlockSpec((D_MODEL, 2 * MEM_WIDTH), lambda i: (0, 0)),
        ],
        out_specs=[pl.BlockSpec((tm, MEM_WIDTH), lambda i: (i, 0))] * 2,
        out_shape=[jax.ShapeDtypeStruct((n, MEM_WIDTH), F32)] * 2,
        compiler_params=_params(("parallel",)),
        name="mem_kv",
    )(mem, g, w_bf16)


EXP_M05 = math.exp(-0.5)


def _dotp(a, b, dims, passes):
    return _dot3(a, b, dims) if passes == 3 else _bdot(a, b, dims)


PREC = dict(a=1, t=1, lkv=1, tw=1, am=1, gh=3, y=1, s=3)


def _unit_lower_inverse(lows, c):
    rows = lax.broadcasted_iota(jnp.int32, (c, c), 0)
    cols = lax.broadcasted_iota(jnp.int32, (c, c), 1)
    eye = (rows == cols).astype(F32)
    invs = [eye - low for low in lows]
    powers = lows
    for _ in range(int(math.log2(c)) - 1):
        powers = [_dotp(pw, pw, NN, PREC["t"]) for pw in powers]
        invs = [inv + _dotp(inv, pw, NN, PREC["t"]) for inv, pw in zip(invs, powers)]
    return invs


def _rwkv_body(zr_ref, shift_ref, s0_ref, mu_ref, w0_ref, wd_ref, a0_ref, wa_ref, wg_ref, kk_ref, ka_ref,
               rk_ref, lng_ref, lnb_ref, hsum_ref,
               o_ref, sout_ref, shout_ref,
               carry_ref, state_ref, r_s, k_s, v_s, kk_s, b_s, ld_s, y_s, bon_s, g_s,
               rw_s, y0_s, gm_s, h0_s, *, tt, c, nseq):
    i = pl.program_id(1)
    rw = RWKV_WIDTH
    nh = RWKV_HEADS

    @pl.when(i == 0)
    def _():
        for q in range(nseq):
            state_ref[q * nh:(q + 1) * nh] = s0_ref[q]

    z = zr_ref[0]
    z_prev = pltpu.roll(z, 1, 0)
    row = lax.broadcasted_iota(jnp.int32, z.shape, 0)
    if nseq == 1:
        @pl.when(i == 0)
        def _():
            carry_ref[...] = shift_ref[0]

        z_prev = jnp.where(row == 0, carry_ref[...], z_prev)
        carry_ref[...] = z[tt - 1:tt, :]
    else:
        first_rows = jnp.concatenate([jnp.broadcast_to(shift_ref[q], (c, SHIFT_WIDTH)) for q in range(nseq)], axis=0)
        z_prev = jnp.where(row % c == 0, first_rows, z_prev)
    zs = z + mu_ref[...] * (z_prev - z)
    r = zs[:, 0:rw]
    k = zs[:, rw:2 * rw]
    v = zs[:, 2 * rw:3 * rw]
    zw = zs[:, 3 * rw:3 * rw + DECAY_LORA]
    za = zs[:, 3 * rw + DECAY_LORA:3 * rw + DECAY_LORA + A_LORA]
    zg = zs[:, 3 * rw + DECAY_LORA + A_LORA:SHIFT_WIDTH]
    xw = w0_ref[...] + _bdot(jnp.tanh(zw), wd_ref[...])
    ld_s[...] = -EXP_M05 * _sigmoid(xw)
    a = _sigmoid(a0_ref[...] + _bdot(za, wa_ref[...]))
    g_s[...] = _bdot(_sigmoid(zg), wg_ref[...])
    kk = k * kk_ref[...]
    k = k * (1.0 + (a - 1.0) * ka_ref[...])
    r_s[...] = r
    k_s[...] = k
    v_s[...] = v

    def head_sum(t):
        return jnp.dot(t.astype(BF16), hsum_ref[...], preferred_element_type=F32)

    kk = kk / jnp.maximum(jnp.sqrt(head_sum(kk * kk)), 1e-12)
    kk_s[...] = kk
    b_s[...] = kk * a
    bon_s[...] = head_sum(r * k * rk_ref[...]) * v

    rows_c = lax.broadcasted_iota(jnp.int32, (c, c), 0)
    cols_c = lax.broadcasted_iota(jnp.int32, (c, c), 1)
    strict = rows_c > cols_c
    incl = rows_c >= cols_c
    tril_ones = incl.astype(BF16)
    rows_2c = lax.broadcasted_iota(jnp.int32, (c, 2 * c), 0)
    cols_2c = lax.broadcasted_iota(jnp.int32, (c, 2 * c), 1)
    incl2 = rows_2c >= jnp.where(cols_2c >= c, cols_2c - c, cols_2c)
    rows_k = lax.broadcasted_iota(jnp.int32, (HEAD_DIM, HEAD_DIM), 0)
    cols_k = lax.broadcasted_iota(jnp.int32, (HEAD_DIM, HEAD_DIM), 1)
    eye_k = (rows_k == cols_k).astype(F32)
    zeros_cv = jnp.zeros((c, HEAD_DIM), F32)
    heads = [slice(h * HEAD_DIM, (h + 1) * HEAD_DIM) for h in range(RWKV_HEADS)]

    n_chunks = tt // c
    group = nseq if nseq > 1 else math.gcd(n_chunks, RWKV_GROUP)

    def chunk_rows(ci):
        return pl.ds(ci * c, c) if isinstance(ci, int) else pl.ds(pl.multiple_of(ci * c, c), c)

    def scaled(ci):
        rows = chunk_rows(ci)
        ld = ld_s[rows, :]
        ld_hi, ld_lo = _split(ld)
        cum = (jnp.dot(tril_ones, ld_hi, preferred_element_type=F32)
               + jnp.dot(tril_ones, ld_lo, preferred_element_type=F32))
        e_inc = jnp.exp(cum)
        e_neg = jnp.exp(-cum)
        kt = k_s[rows, :] * e_neg
        bt = b_s[rows, :] * e_neg
        g_end = e_inc[c - 1:c, :]
        return dict(ci=ci, rt=r_s[rows, :] * e_inc, kkt=kk_s[rows, :] * jnp.exp(cum - ld), kt=kt, bt=bt,
                    g_end=g_end, bc=bt * g_end, kc=kt * g_end, vv=v_s[rows, :])

    def phase1(cj, carry):
        chains = [(ch, h, s) for ch in [scaled(cj * group + g) for g in range(group)]
                  for h, s in enumerate(heads)]
        x = [jnp.concatenate([ch["kkt"][:, s], ch["rt"][:, s]], axis=0) for ch, _, s in chains]
        zz = [jnp.concatenate([ch["bt"][:, s], ch["kt"][:, s]], axis=0) for ch, _, s in chains]
        amat = [_dotp(xh, zh, NT, PREC["a"]) for xh, zh in zip(x, zz)]
        l_b = [jnp.where(strict, am[0:c, 0:c], 0.0) for am in amat]
        l_k = [jnp.where(strict, am[0:c, c:2 * c], 0.0) for am in amat]
        a_r = [jnp.where(incl2, am[c:2 * c, :], 0.0) for am in amat]
        tinv = _unit_lower_inverse(l_b, c)
        lkv = [_dotp(lk, ch["vv"][:, s], NN, PREC["lkv"]) for lk, (ch, _, s) in zip(l_k, chains)]
        wu = [-_dotp(t, jnp.concatenate([ch["kkt"][:, s], lv], axis=1), NN, PREC["tw"])
              for t, lv, (ch, _, s) in zip(tinv, lkv, chains)]
        m = [jnp.concatenate([w, jnp.concatenate([zeros_cv, ch["vv"][:, s]], axis=1)], axis=0)
             for w, (ch, _, s) in zip(wu, chains)]
        am2 = [_dotp(ar, mh, NN, PREC["am"]) for ar, mh in zip(a_r, m)]
        gh = [_dotp(mh, jnp.concatenate([ch["bc"][:, s], ch["kc"][:, s]], axis=0), TN, PREC["gh"])
              for mh, (ch, _, s) in zip(m, chains)]
        for q, (ch, h, s) in enumerate(chains):
            idx = ch["ci"] * RWKV_HEADS + h
            rw_s[idx] = ch["rt"][:, s] + am2[q][:, 0:HEAD_DIM]
            y0_s[idx] = am2[q][:, HEAD_DIM:2 * HEAD_DIM]
            gm_s[idx] = gh[q][0:HEAD_DIM, :] + eye_k * ch["g_end"][:, s]
            h0_s[idx] = gh[q][HEAD_DIM:2 * HEAD_DIM, :]
        return carry

    if n_chunks == group:
        phase1(0, 0)
    else:
        lax.fori_loop(0, n_chunks // group, phase1, 0)

    if nseq == 1:
        def phase2(ci, carry):
            rows = chunk_rows(ci)
            for h, s in enumerate(heads):
                idx = ci * nh + h
                s_h = state_ref[h]
                y_s[rows, s] = _dotp(rw_s[idx], s_h, NT, PREC["y"]) + y0_s[idx]
                state_ref[h] = _dotp(s_h, gm_s[idx], NN, PREC["s"]) + h0_s[idx]
            return carry

        lax.fori_loop(0, n_chunks, phase2, 0)
    else:
        pairs = [(q, h, s) for q in range(nseq) for h, s in enumerate(heads)]
        states = [state_ref[q * nh + h] for q, h, _ in pairs]
        ys = [_dotp(rw_s[q * nh + h], st, NT, PREC["y"]) + y0_s[q * nh + h] for (q, h, _), st in zip(pairs, states)]
        new = [_dotp(st, gm_s[q * nh + h], NN, PREC["s"]) + h0_s[q * nh + h] for (q, h, _), st in zip(pairs, states)]
        for (q, h, s), yq, nq in zip(pairs, ys, new):
            y_s[chunk_rows(q), s] = yq
            state_ref[q * nh + h] = nq
    y = y_s[...]
    yc = y - head_sum(y) * (1.0 / HEAD_DIM)
    var = head_sum(yc * yc) * (1.0 / HEAD_DIM)
    yn = yc * lax.rsqrt(var + GN_EPS)
    o_ref[0] = ((yn * lng_ref[...] + lnb_ref[...] + bon_s[...]) * g_s[...]).astype(o_ref.dtype)

    @pl.when(i == pl.num_programs(1) - 1)
    def _():
        for q in range(nseq):
            sout_ref[q] = state_ref[q * nh:(q + 1) * nh]
            last = tt - 1 if nseq == 1 else (q + 1) * c - 1
            shout_ref[q] = zr_ref[0, last:last + 1, :]


def _rwkv(zr3, blk_off, b, nt, shift_prev, s_prev, p, tt, c, nseq):
    vec = lambda n: pl.BlockSpec((1, n), lambda bi, ti: (0, 0))
    mat = lambda m, n: pl.BlockSpec((m, n), lambda bi, ti: (0, 0))
    rw = RWKV_WIDTH
    scr = lambda: pltpu.VMEM((tt, rw), F32)
    nch = (tt // c) * RWKV_HEADS
    return pl.pallas_call(
        functools.partial(_rwkv_body, tt=tt, c=c, nseq=nseq),
        grid=(b // nseq, nt),
        in_specs=[
            pl.BlockSpec((1, tt, SHIFT_WIDTH), lambda bi, ti: (blk_off + bi * nt + ti, 0, 0)),
            pl.BlockSpec((nseq, 1, SHIFT_WIDTH), lambda bi, ti: (bi, 0, 0)),
            pl.BlockSpec((nseq, RWKV_HEADS, HEAD_DIM, HEAD_DIM), lambda bi, ti: (bi, 0, 0, 0)),
            vec(SHIFT_WIDTH), vec(rw), mat(DECAY_LORA, rw), vec(rw), mat(A_LORA, rw), mat(GATE_LORA, rw),
            vec(rw), vec(rw), vec(rw), vec(rw), vec(rw), mat(rw, rw),
        ],
        out_specs=[
            pl.BlockSpec((1, tt, rw), lambda bi, ti: (bi, ti, 0)),
            pl.BlockSpec((nseq, RWKV_HEADS, HEAD_DIM, HEAD_DIM), lambda bi, ti: (bi, 0, 0, 0)),
            pl.BlockSpec((nseq, 1, SHIFT_WIDTH), lambda bi, ti: (bi, 0, 0)),
        ],
        out_shape=[
            jax.ShapeDtypeStruct((b // nseq, nt * tt, rw), BF16),
            jax.ShapeDtypeStruct((b, RWKV_HEADS, HEAD_DIM, HEAD_DIM), F32),
            jax.ShapeDtypeStruct((b, 1, SHIFT_WIDTH), F32),
        ],
        scratch_shapes=[
            pltpu.VMEM((1, SHIFT_WIDTH), F32),
            pltpu.VMEM((nseq * RWKV_HEADS, HEAD_DIM, HEAD_DIM), F32),
            scr(), scr(), scr(), scr(), scr(), scr(), scr(), scr(), scr(),
            pltpu.VMEM((nch, c, HEAD_DIM), F32),
            pltpu.VMEM((nch, c, HEAD_DIM), F32),
            pltpu.VMEM((nch, HEAD_DIM, HEAD_DIM), F32),
            pltpu.VMEM((nch, HEAD_DIM, HEAD_DIM), F32),
        ],
        compiler_params=_params(("parallel", "arbitrary")),
        name="rwkv",
    )(zr3, shift_prev, s_prev, p["mu"], p["w0"], p["wd"], p["a0"], p["wa"], p["wg"], p["kk"], p["ka"],
      p["rk"], p["lng"], p["lnb"], p["hsum"])


def _gmlp_body(z_ref, lng_ref, lnb_ref, ws_ref, bs_ref, o_ref, v_ref):
    z = z_ref[...]
    ge = 0.5 * z * (1.0 + lax.erf(z * (1.0 / math.sqrt(2.0))))
    u = ge[:, 0:GMLP_WIDTH]
    v = ge[:, GMLP_WIDTH:2 * GMLP_WIDTH]
    mean = jnp.mean(v, axis=-1, keepdims=True)
    vc = v - mean
    var = jnp.mean(vc * vc, axis=-1, keepdims=True)
    vn = vc * lax.rsqrt(var + LN_EPS) * lng_ref[...] + lnb_ref[...]
    v_ref[...] = vn
    gd = GMLP_WIDTH // GMLP_GROUPS
    for q in range(GMLP_TILE):
        rows = slice(q * CHUNK, (q + 1) * CHUNK)
        for g in range(GMLP_GROUPS):
            gs = slice(g * gd, (g + 1) * gd)
            sv = jnp.dot(ws_ref[g], vn[rows, gs].astype(BF16), preferred_element_type=F32) + bs_ref[:, g:g + 1]
            o_ref[rows, gs] = (u[rows, gs] * sv).astype(o_ref.dtype)


def _gmlp(zg, tile_off, n_rows, lng, lnb, ws_bf16, bs):
    tile = GMLP_TILE * CHUNK
    return pl.pallas_call(
        _gmlp_body,
        grid=(n_rows // tile,),
        in_specs=[
            pl.BlockSpec((tile, 2 * GMLP_WIDTH), lambda i: (tile_off + i, 0)),
            pl.BlockSpec((1, GMLP_WIDTH), lambda i: (0, 0)),
            pl.BlockSpec((1, GMLP_WIDTH), lambda i: (0, 0)),
            pl.BlockSpec((GMLP_GROUPS, CHUNK, CHUNK), lambda i: (0, 0, 0)),
            pl.BlockSpec((CHUNK, GMLP_GROUPS), lambda i: (0, 0)),
        ],
        out_specs=[
            pl.BlockSpec((tile, GMLP_WIDTH), lambda i: (i, 0)),
            pl.BlockSpec((tile, GMLP_WIDTH), lambda i: (i, 0)),
        ],
        out_shape=[
            jax.ShapeDtypeStruct((n_rows, GMLP_WIDTH), BF16),
            jax.ShapeDtypeStruct((n_rows, GMLP_WIDTH), F32),
        ],
        compiler_params=_params(("parallel",)),
        name="gmlp",
    )(zg, lng, lnb, ws_bf16, bs)


def _mem_attn_body(q_ref, k_ref, v_ref, o_ref, *, bb, feature_major):
    hd = MEM_WIDTH // MEM_HEADS
    pairs = [(s, slice(h * hd, (h + 1) * hd)) for s in range(bb) for h in range(MEM_HEADS)]
    if feature_major:
        scores = [_bdot(q_ref[s, :, hs], k_ref[s, hs, :], NN) * (hd ** -0.5) for s, hs in pairs]
    else:
        scores = [_bdot(q_ref[s, :, hs], k_ref[s, :, hs], NT) * (hd ** -0.5) for s, hs in pairs]
    probs = []
    for sc in scores:
        pr = jnp.exp(sc - jnp.max(sc, axis=-1, keepdims=True))
        probs.append(pr / jnp.sum(pr, axis=-1, keepdims=True))
    for (s, hs), pr in zip(pairs, probs):
        if feature_major:
            o_ref[s, :, hs] = _bdot(pr, v_ref[s, hs, :], NT).astype(o_ref.dtype)
        else:
            o_ref[s, :, hs] = _bdot(pr, v_ref[s, :, hs]).astype(o_ref.dtype)


def _mem_attn(q3, blk_off, b, nt, mk, mv, tt, bb, feature_major):
    return pl.pallas_call(
        functools.partial(_mem_attn_body, bb=bb, feature_major=feature_major),
        grid=(b // bb, nt),
        in_specs=[
            pl.BlockSpec((bb, tt, MEM_WIDTH), lambda bi, ti: (blk_off // bb + bi * nt + ti, 0, 0)),
            pl.BlockSpec((bb, N_MEM, MEM_WIDTH), lambda bi, ti: (bi, 0, 0)),
            pl.BlockSpec((bb, N_MEM, MEM_WIDTH), lambda bi, ti: (bi, 0, 0)),
        ],
        out_specs=pl.BlockSpec((bb, tt, MEM_WIDTH), lambda bi, ti: (bi, ti, 0)),
        out_shape=jax.ShapeDtypeStruct((b, nt * tt, MEM_WIDTH), BF16),
        compiler_params=_params(("parallel", "parallel")),
        name="mem_attn",
    )(q3, mk, mv)


def _merge_body(x_p, x_s, orw_p, orw_s, ogm_p, ogm_s, ome_p, ome_s, sg_ref, wbr_ref, wbg_ref, wbm_ref, wout_ref,
                gffn_ref, wrt_ref, brt_ref, x2_ref, h2_ref, route_ref, route_t_ref, cnt_ref, count_s, *, np_tiles):
    i = pl.program_id(0)
    d = D_MODEL
    tm = x_p.shape[0]
    is_p = i < np_tiles
    x = jnp.where(is_p, x_p[...], x_s[...])
    orw = jnp.where(is_p, orw_p[...], orw_s[...])
    ogm = jnp.where(is_p, ogm_p[...], ogm_s[...])
    ome = jnp.where(is_p, ome_p[...], ome_s[...])
    merged = sg_ref[:, 0:d].astype(F32) * _bdot(orw, wbr_ref[...])
    merged = merged + sg_ref[:, d:2 * d].astype(F32) * _bdot(ogm, wbg_ref[...])
    merged = merged + sg_ref[:, 2 * d:3 * d].astype(F32) * _bdot(ome, wbm_ref[...])
    x2 = x + _bdot(merged, wout_ref[...])
    x2_ref[...] = x2
    h2 = _rms(x2, gffn_ref[...])
    _store_row_tiles(h2_ref, h2)

    @pl.when(i == 0)
    def _():
        count_s[...] = jnp.zeros_like(count_s)

    lane = lax.broadcasted_iota(jnp.int32, (tm, LANES), 1)
    logits = jnp.where(lane < N_EXPERTS, _dot3(h2, wrt_ref[...]) + brt_ref[...], -jnp.inf)
    lane_f = lane.astype(F32)
    tops, hots, idxs = [], [], []
    for _ in range(TOP_K):
        top = jnp.max(logits, axis=-1, keepdims=True)
        idx = jnp.min(jnp.where(logits == top, lane_f, float(LANES)), axis=-1, keepdims=True)
        hot = lane_f == idx
        logits = jnp.where(hot, -jnp.inf, logits)
        tops.append(top)
        hots.append(hot)
        idxs.append(idx)
    weights = [jnp.exp(t - tops[0]) for t in tops]
    denom = weights[0] + weights[1] + weights[2] + weights[3]
    onehot = jnp.zeros((tm, LANES), F32)
    for hot in hots:
        onehot = onehot + jnp.where(hot, 1.0, 0.0)
    rows_t = lax.broadcasted_iota(jnp.int32, (tm, tm), 0)
    cols_t = lax.broadcasted_iota(jnp.int32, (tm, tm), 1)
    ahead = jnp.where(rows_t > cols_t, 1.0, 0.0).astype(BF16)
    prefix = jnp.dot(ahead, onehot.astype(BF16), preferred_element_type=F32) + count_s[...]
    route = jnp.zeros((tm, LANES), F32)
    for kk in range(TOP_K):
        rank = jnp.sum(jnp.where(hots[kk], prefix, 0.0), axis=-1, keepdims=True)
        route = jnp.where(lane == ROUTE_E + kk, idxs[kk], route)
        route = jnp.where(lane == ROUTE_RANK + kk, rank, route)
        route = jnp.where(lane == ROUTE_GATE + kk, weights[kk] / denom, route)
    route_ref[...] = route
    route_t_ref[...] = route.T
    count_s[...] = count_s[...] + jnp.sum(onehot, axis=0, keepdims=True)
    cnt_ref[...] = jnp.broadcast_to(count_s[...], cnt_ref.shape)


def _merge(x, o_rw, o_gm, o_me, sg, wbr, wbg, wbm, wout, gffn, wrt_pad, brt_pad):
    n_p = x[0].shape[0]
    n = n_p + x[1].shape[0]
    tm = MERGE_TM
    d = D_MODEL
    np_tiles = n_p // tm
    row = lambda i: (i, 0)
    const = lambda i: (0, 0)
    first =lambda i: (jnp.minimum(i, np_tiles - 1), 0)
    second = lambda i: (jnp.maximum(i - np_tiles, 0), 0)

    def pair(width):
        return [pl.BlockSpec((tm, width), first), pl.BlockSpec((tm, width), second)]

    return pl.pallas_call(
        functools.partial(_merge_body, np_tiles=np_tiles),
        grid=(n // tm,),
        in_specs=pair(d) + pair(RWKV_WIDTH) + pair(GMLP_WIDTH) + pair(MEM_WIDTH) + [
            pl.BlockSpec((tm, 3 * d), row),
            pl.BlockSpec((RWKV_WIDTH, d), const),
            pl.BlockSpec((GMLP_WIDTH, d), const),
            pl.BlockSpec((MEM_WIDTH, d), const),
            pl.BlockSpec((d, d), const),
            pl.BlockSpec((1, d), const),
            pl.BlockSpec((d, LANES), const),
            pl.BlockSpec((1, LANES), const),
        ],
        out_specs=[
            pl.BlockSpec((tm, d), row),
            pl.BlockSpec((tm * ROW_SUB, LANES), row),
            pl.BlockSpec((tm, LANES), row),
            pl.BlockSpec((LANES, tm), lambda i: (0, i)),
            pl.BlockSpec((8, LANES), const),
        ],
        out_shape=[
            jax.ShapeDtypeStruct((n, d), F32),
            jax.ShapeDtypeStruct((n * ROW_SUB, LANES), F32),
            jax.ShapeDtypeStruct((n, LANES), F32),
            jax.ShapeDtypeStruct((LANES, n), F32),
            jax.ShapeDtypeStruct((8, LANES), F32),
        ],
        scratch_shapes=[pltpu.VMEM((1, LANES), F32)],
        compiler_params=_params(("arbitrary",)),
        name="merge",
    )(x[0], x[1], o_rw[0], o_rw[1], o_gm[0], o_gm[1], o_me[0], o_me[1], sg, wbr, wbg, wbm, wout, gffn, wrt_pad,
      brt_pad)


N_ZERO_BLOCKS = 2 * N_EXPERTS


def _dispatch_body(dest_ref, zlist_ref, h_ref, xs_hbm, zero_s, sem_z, sem):
    i = pl.program_id(0)
    tm = DISPATCH_TM
    bm = MOE_BM

    @pl.when(i == 0)
    def _():
        zero_s[...] = jnp.zeros_like(zero_s)

        def zero_copy(q):
            start = pl.multiple_of(zlist_ref[q] * bm, bm)
            return pltpu.make_async_copy(zero_s, xs_hbm.at[pl.ds(start, bm)], sem_z)

        def start(q, carry):
            @pl.when(zlist_ref[q] >= 0)
            def _():
                zero_copy(q).start()
            return carry

        def wait(q, carry):
            @pl.when(zlist_ref[q] >= 0)
            def _():
                zero_copy(q).wait()
            return carry

        lax.fori_loop(0, N_ZERO_BLOCKS, start, 0)
        lax.fori_loop(0, N_ZERO_BLOCKS, wait, 0)

    base = i * tm
    n_tok = pl.num_programs(0) * tm

    def body(r, carry):
        for kk in range(TOP_K):
            slot = dest_ref[kk * n_tok + base + r]
            pltpu.make_async_copy(h_ref.at[r], xs_hbm.at[slot], sem).start(priority=kk % 2)
        return carry

    lax.fori_loop(0, tm, body, 0, unroll=2)
    for kk in range(TOP_K):
        pltpu.make_async_copy(h_ref, xs_hbm.at[pl.ds(0, tm)], sem).wait()


def _dispatch(dest, zlist, h, n_blocks):
    n = h.shape[0]
    tm = DISPATCH_TM
    grid_spec = pltpu.PrefetchScalarGridSpec(
        num_scalar_prefetch=2,
        grid=(n // tm,),
        in_specs=[pl.BlockSpec((tm, ROW_SUB, LANES), lambda i, dest, zl: (i, 0, 0))],
        out_specs=pl.BlockSpec(memory_space=pl.ANY),
        scratch_shapes=[
            pltpu.VMEM((MOE_BM, ROW_SUB, LANES), F32),
            pltpu.SemaphoreType.DMA(()),
            pltpu.SemaphoreType.DMA(()),
        ],
    )
    return pl.pallas_call(
        _dispatch_body,
        grid_spec=grid_spec,
        out_shape=jax.ShapeDtypeStruct((n_blocks * MOE_BM, ROW_SUB, LANES), F32),
        compiler_params=_params(("arbitrary",)),
        name="moe_dispatch",
    )(dest, zlist, h)


CAST_ROWS = 64


def _moe_body(be_ref, nused_ref, x_ref, w1_ref, b1_ref, w2_ref, b2_ref, o_ref, w1b, w2b):
    j = pl.program_id(0)
    n_used = nused_ref[0]

    @pl.when(j < n_used)
    def _():
        changed = jnp.logical_or(j == 0, be_ref[j] != be_ref[jnp.maximum(j - 1, 0)])

        @pl.when(changed)
        def _():
            def cast1(q, carry):
                rows = pl.ds(pl.multiple_of(q * CAST_ROWS, CAST_ROWS), CAST_ROWS)
                w1b[rows, :] = w1_ref[0, rows, :].astype(BF16)
                return carry

            def cast2(q, carry):
                rows = pl.ds(pl.multiple_of(q * CAST_ROWS, CAST_ROWS), CAST_ROWS)
                w2b[rows, :] = w2_ref[0, rows, :].astype(BF16)
                return carry

            lax.fori_loop(0, D_MODEL // CAST_ROWS, cast1, 0)
            lax.fori_loop(0, D_FF // CAST_ROWS, cast2, 0)

        x = _load_row_tiles(x_ref, MOE_BM).astype(BF16)
        z = jnp.dot(x, w1b[...], preferred_element_type=F32) + b1_ref[0]
        zg = jnp.minimum(z[:, 0:D_FF], SWIGLU_LIMIT)
        zl = jnp.clip(z[:, D_FF:2 * D_FF], -SWIGLU_LIMIT, SWIGLU_LIMIT)
        act = zg * _sigmoid(SWIGLU_ALPHA * zg) * (zl + 1.0)
        _store_row_tiles(o_ref, jnp.dot(act.astype(BF16), w2b[...], preferred_element_type=F32) + b2_ref[0])

    @pl.when(j >= n_used)
    def _():
        o_ref[...] = jnp.zeros_like(o_ref)


def _moe(blk_e, n_used, xs, w1, b1, w2, b2, n_blocks):
    bm = MOE_BM
    d = D_MODEL
    grid_spec = pltpu.PrefetchScalarGridSpec(
        num_scalar_prefetch=2,
        grid=(n_blocks,),
        in_specs=[
            pl.BlockSpec((bm * ROW_SUB, LANES), lambda j, be, nu: (j, 0)),
            pl.BlockSpec((1, d, 2 * D_FF), lambda j, be, nu: (be[j], 0, 0)),
            pl.BlockSpec((1, 1, 2 * D_FF), lambda j, be, nu: (be[j], 0, 0)),
            pl.BlockSpec((1, D_FF, d), lambda j, be, nu: (be[j], 0, 0)),
            pl.BlockSpec((1, 1, d), lambda j, be, nu: (be[j], 0, 0)),
        ],
        out_specs=pl.BlockSpec((bm * ROW_SUB, LANES), lambda j, be, nu: (j, 0)),
        scratch_shapes=[
            pltpu.VMEM((d, 2 * D_FF), BF16),
            pltpu.VMEM((D_FF, d), BF16),
        ],
    )
    return pl.pallas_call(
        _moe_body,
        grid_spec=grid_spec,
        out_shape=jax.ShapeDtypeStruct((n_blocks * bm * ROW_SUB, LANES), F32),
        compiler_params=_params(("arbitrary",)),
        name="moe_ffn",
    )(blk_e, n_used, xs, w1, b1, w2, b2)


def _combine_body(dest_ref, x2_ref, route_ref, gfin_ref, yb_hbm, op_ref, os_ref, buf0, buf1, sem, *, np_steps):
    i = pl.program_id(0)
    n_steps = pl.num_programs(0)
    tm = COMBINE_TM
    n_tok = n_steps * (2 * tm)
    bufs = (buf0, buf1)

    def issue(tile, which):
        base = tile * tm
        for r in range(tm):
            for kk in range(TOP_K):
                slot = dest_ref[kk * n_tok + base + r]
                src = pl.ds(pl.multiple_of(slot * ROW_SUB, ROW_SUB), ROW_SUB)
                pltpu.make_async_copy(yb_hbm.at[src], bufs[which].at[kk, r * ROW_SUB:(r + 1) * ROW_SUB],
                                      sem.at[which]).start(priority=kk % 2)

    def wait(which):
        for kk in range(TOP_K):
            pltpu.make_async_copy(yb_hbm.at[pl.ds(0, tm * ROW_SUB)], bufs[which].at[kk], sem.at[which]).wait()

    def reduce(which, half):
        rows = slice(half * tm, (half + 1) * tm)
        acc = x2_ref[rows, :]
        for kk in range(TOP_K):
            gate = route_ref[rows, ROUTE_GATE + kk:ROUTE_GATE + kk + 1]
            acc = acc + gate * _load_row_tiles(bufs[which].at[kk], tm)
        return _rms(acc, gfin_ref[...])

    @pl.when(i == 0)
    def _():
        issue(0, 0)

    wait(0)
    issue(2 * i + 1, 1)
    y0 = reduce(0, 0)
    wait(1)
    issue(jnp.minimum(2 * i + 2, 2 * n_steps - 1), 0)
    y1 = reduce(1, 1)

    @pl.when(i < np_steps)
    def _():
        op_ref[0:tm, :] = y0
        op_ref[tm:2 * tm, :] = y1

    @pl.when(i >= np_steps)
    def _():
        os_ref[0:tm, :] = y0
        os_ref[tm:2 * tm, :] = y1

    @pl.when(i == n_steps - 1)
    def _():
        wait(0)


def _combine(dest, x2, route, gfin, yb, n_p):
    n, d = x2.shape
    tm = COMBINE_TM
    step = 2 * tm
    np_steps = n_p // step
    grid_spec = pltpu.PrefetchScalarGridSpec(
        num_scalar_prefetch=1,
        grid=(n // step,),
        in_specs=[
            pl.BlockSpec((step, d), lambda i, dest: (i, 0)),
            pl.BlockSpec((step, LANES), lambda i, dest: (i, 0)),
            pl.BlockSpec((1, d), lambda i, dest: (0, 0)),
            pl.BlockSpec(memory_space=pl.ANY),
        ],
        out_specs=[
            pl.BlockSpec((step, d), lambda i, dest: (jnp.minimum(i, np_steps - 1), 0)),
            pl.BlockSpec((step, d), lambda i, dest: (jnp.maximum(i - np_steps, 0), 0)),
        ],
        scratch_shapes=[
            pltpu.VMEM((TOP_K, tm * ROW_SUB, LANES), F32),
            pltpu.VMEM((TOP_K, tm * ROW_SUB, LANES), F32),
            pltpu.SemaphoreType.DMA((2,)),
        ],
    )
    return pl.pallas_call(
        functools.partial(_combine_body, np_steps=np_steps),
        grid_spec=grid_spec,
        out_shape=[jax.ShapeDtypeStruct((n_p, d), F32), jax.ShapeDtypeStruct((n - n_p, d), F32)],
        compiler_params=_params(("arbitrary",)),
        name="moe_combine",
    )(dest, x2, route, gfin, yb)


def _slot_tables(route_t, counts_row, n_blocks):
    bm = MOE_BM
    e = route_t[ROUTE_E:ROUTE_E + TOP_K].astype(jnp.int32)
    rank = route_t[ROUTE_RANK:ROUTE_RANK + TOP_K].astype(jnp.int32)
    counts = counts_row[:N_EXPERTS].astype(jnp.int32)
    padded = (counts + bm - 1) // bm * bm
    pad_end = jnp.cumsum(padded)
    pad_start = pad_end - padded
    experts = jnp.arange(N_EXPERTS, dtype=jnp.int32)[:, None, None]
    dest = (jnp.sum(jnp.where(e[None] == experts, pad_start[:, None, None], 0), axis=0) + rank).reshape(-1)
    blk_start = jnp.arange(n_blocks, dtype=jnp.int32) * bm
    blk_e = jnp.minimum(jnp.sum(pad_end[None, :] <= blk_start[:, None], axis=1), N_EXPERTS - 1).astype(jnp.int32)
    n_used = pad_end[-1] // bm
    last_blk = jnp.where(padded > 0, pad_end // bm - 1, -1)
    trailing = n_used + jnp.arange(N_EXPERTS, dtype=jnp.int32)
    trailing = jnp.where(trailing < n_blocks, trailing, -1)
    zlist = jnp.concatenate([last_blk, trailing]).astype(jnp.int32)
    return dest.astype(jnp.int32), blk_e, n_used.astype(jnp.int32).reshape(1), zlist


def kernel(x_prompt, x_sample, mem_prompt, state_shift, state_wkv, cache_mem_k, cache_mem_v, g_norm_mix, w_in, mu_shift, w0, w_decay_up, a0, w_a_up, w_g_up, k_k, k_a, r_k, ln_x_g, ln_x_b, gmlp_ln_g, gmlp_ln_b, w_spatial, b_spatial, g_norm_mem, w_mem_kv, w_br_rwkv, w_br_gmlp, w_br_mem, w_out, g_norm_ffn, w_router, b_router, w_exp1, b_exp1, w_exp2, b_exp2, g_norm_final):
    bp, tp, d = x_prompt.shape
    bs, ts, _ = x_sample.shape
    n_p, n_s = bp * tp, bs * ts
    n_all = n_p + n_s
    l = 0
    row = lambda a: a.reshape(1, -1)

    x_pair = (x_prompt.reshape(n_p, d), x_sample.reshape(n_s, d))
    zr, zg, zq, sg = _in_proj(x_pair[0], x_pair[1], row(g_norm_mix[l]), w_in[l].astype(BF16))

    mk_p, mv_p = _mem_kv(mem_prompt.reshape(bp * N_MEM, d), row(g_norm_mem[l]), w_mem_kv[l].astype(BF16))

    rp = dict(mu=row(mu_shift[l]), w0=row(w0[l]), wd=w_decay_up[l].astype(BF16), a0=row(a0[l]),
              wa=w_a_up[l].astype(BF16), wg=w_g_up[l].astype(BF16), kk=row(k_k[l]), ka=row(k_a[l]),
              rk=row(r_k[l]), lng=row(ln_x_g[l]), lnb=row(ln_x_b[l]))
    head_of = jnp.arange(RWKV_WIDTH, dtype=jnp.int32) // HEAD_DIM
    rp["hsum"] = (head_of[:, None] == head_of[None, :]).astype(BF16)
    o_rw_p, s_p, shift_p = _rwkv(zr.reshape(n_all // RWKV_TT, RWKV_TT, SHIFT_WIDTH), 0, bp, tp // RWKV_TT,
                        jnp.zeros((bp, 1, SHIFT_WIDTH), F32), jnp.zeros((bp, RWKV_HEADS, HEAD_DIM, HEAD_DIM), F32),
                        rp, tt=RWKV_TT, c=RWKV_C, nseq=1)
    tile_s = RWKV_SAMPLE_SEQS * ts
    o_rw_s, s_s, shift_s = _rwkv(zr.reshape(n_all // tile_s, tile_s, SHIFT_WIDTH), n_p // tile_s, bs, 1,
                        state_shift[l].reshape(bs, 1, SHIFT_WIDTH), state_wkv[l], rp, tt=tile_s, c=ts,
                        nseq=RWKV_SAMPLE_SEQS)

    tri = jnp.tril(jnp.ones((CHUNK, CHUNK), bool))
    ws_p = jnp.where(tri, w_spatial[l], 0.0).astype(BF16)
    bs_p = b_spatial[l].T
    reps = CHUNK // ts
    tri_s = jnp.tril(jnp.ones((ts, ts), bool))
    ws_small = jnp.where(tri_s, w_spatial[l][:, :ts, :ts], 0.0)
    eye = jnp.eye(reps, dtype=F32)
    ws_s = jnp.einsum("ab,gij->gaibj", eye, ws_small).reshape(GMLP_GROUPS, CHUNK, CHUNK).astype(BF16)
    bs_s = jnp.tile(b_spatial[l][:, :ts], (1, reps)).T
    lng, lnb = row(gmlp_ln_g[l]), row(gmlp_ln_b[l])
    o_gm_p, _ = _gmlp(zg, 0, n_p, lng, lnb, ws_p, bs_p)
    o_gm_s, v_rows_s = _gmlp(zg, n_p // (GMLP_TILE * CHUNK), n_s, lng, lnb, ws_s, bs_s)

    o_me_p = _mem_attn(zq.reshape(n_all // ATTN_TT, ATTN_TT, MEM_WIDTH), 0, bp, tp // ATTN_TT,
                       mk_p.reshape(bp, N_MEM, MEM_WIDTH), mv_p.reshape(bp, N_MEM, MEM_WIDTH), tt=ATTN_TT, bb=1,
                       feature_major=False)
    mk_s = jnp.transpose(cache_mem_k[l].reshape(bs, N_MEM, MEM_WIDTH), (0, 2, 1))
    mv_s = jnp.transpose(cache_mem_v[l].reshape(bs, N_MEM, MEM_WIDTH), (0, 2, 1))
    o_me_s = _mem_attn(zq.reshape(n_all // ts, ts, MEM_WIDTH), n_p // ts, bs, 1, mk_s, mv_s, tt=ts, bb=ATTN_BB,
                       feature_major=True)

    wrt_pad = jnp.zeros((d, LANES), F32).at[:, :N_EXPERTS].set(w_router[l])
    brt_pad = jnp.zeros((1, LANES), F32).at[0, :N_EXPERTS].set(b_router[l])
    x2, h2, route, route_t, counts = _merge(
        x_pair, (o_rw_p.reshape(n_p, RWKV_WIDTH), o_rw_s.reshape(n_s, RWKV_WIDTH)), (o_gm_p, o_gm_s),
        (o_me_p.reshape(n_p, MEM_WIDTH), o_me_s.reshape(n_s, MEM_WIDTH)), sg,
        w_br_rwkv[l].astype(BF16), w_br_gmlp[l].astype(BF16), w_br_mem[l].astype(BF16), w_out[l].astype(BF16),
        row(g_norm_ffn[l]), wrt_pad, brt_pad)

    n_assign = n_all * TOP_K
    n_blocks = -(-(n_assign + N_EXPERTS * (MOE_BM - 1)) // MOE_BM)
    dest, blk_e, n_used, zlist = _slot_tables(route_t, counts[0], n_blocks)
    xs = _dispatch(dest, zlist, h2.reshape(n_all, ROW_SUB, LANES), n_blocks)
    yb = _moe(blk_e, n_used, xs.reshape(n_blocks * MOE_BM * ROW_SUB, LANES), w_exp1[l], b_exp1[l].reshape(N_EXPERTS, 1, 2 * D_FF),
              w_exp2[l], b_exp2[l].reshape(N_EXPERTS, 1, d), n_blocks)
    y_p, y_s = _combine(dest, x2, route, row(g_norm_final), yb, n_p)

    mk_out = mk_p.reshape(1, bp, N_MEM, MEM_HEADS, MEM_WIDTH // MEM_HEADS)
    mv_out = mv_p.reshape(1, bp, N_MEM, MEM_HEADS, MEM_WIDTH // MEM_HEADS)
    return (y_p.reshape(bp, tp, d), y_s.reshape(bs, ts, d), shift_p.reshape(1, bp, SHIFT_WIDTH), s_p[None], mk_out,
            mv_out, shift_s.reshape(1, bs, SHIFT_WIDTH), s_s[None],
            v_rows_s.reshape(1, bs, ts, GMLP_WIDTH))
```

```python
import functools
import math

import jax
import jax.numpy as jnp
from jax import lax
from jax.experimental import pallas as pl
from jax.experimental.pallas import tpu as pltpu

F32 = jnp.float32
BF16 = jnp.bfloat16

D_MODEL = 1024
RWKV_HEADS = 8
HEAD_DIM = 64
RWKV_WIDTH = RWKV_HEADS * HEAD_DIM
DECAY_LORA = 64
A_LORA = 64
GATE_LORA = 128
GMLP_GROUPS = 4
GMLP_WIDTH = 256
CHUNK = 128
MEM_HEADS = 4
MEM_WIDTH = 256
N_MEM = 256
N_EXPERTS = 32
TOP_K = 4
D_FF = 1024
SWIGLU_ALPHA = 1.702
SWIGLU_LIMIT = 7.0
RMS_EPS = 1e-5
LN_EPS = 1e-5
GN_EPS = 64e-5
SHIFT_WIDTH = 3 * RWKV_WIDTH + DECAY_LORA + A_LORA + GATE_LORA
OFF_GMLP = SHIFT_WIDTH
OFF_QMEM = OFF_GMLP + 2 * GMLP_WIDTH
OFF_GATE = OFF_QMEM + MEM_WIDTH
IN_WIDTH = OFF_GATE + 3 * D_MODEL
LANES = 128
ROW_SUB = D_MODEL // LANES

PROJ_TM = 256
MERGE_TM = 512
RWKV_TT = 512
RWKV_C = 64
RWKV_GROUP = 4
RWKV_SAMPLE_SEQS = 8
GMLP_TILE = 4
ATTN_TT = 512
ATTN_BB = 8
MOE_BM = 512
DISPATCH_TM = 256
COMBINE_TM = 128

ROUTE_E = 0
ROUTE_RANK = 4
ROUTE_GATE = 8

NN = ((1,), (0,))
NT = ((1,), (1,))
TN = ((0,), (0,))

VMEM_LIMIT = 56 * 1024 * 1024


def _params(sem, vmem=VMEM_LIMIT):
    return pltpu.CompilerParams(dimension_semantics=sem, vmem_limit_bytes=vmem)


def _bdot(a, b, dims=NN):
    return lax.dot_general(a.astype(BF16), b.astype(BF16), (dims, ((), ())), preferred_element_type=F32)


def _split(x):
    hi = x.astype(BF16)
    lo = (x - hi.astype(F32)).astype(BF16)
    return hi, lo


def _dot3(a, b, dims=NN):
    dn = (dims, ((), ()))
    ah, al = _split(a)
    bh, bl = _split(b)
    r = lax.dot_general(ah, bh, dn, preferred_element_type=F32)
    r = r + lax.dot_general(al, bh, dn, preferred_element_type=F32)
    return r + lax.dot_general(ah, bl, dn, preferred_element_type=F32)


def _rms(x, g):
    return x * lax.rsqrt(jnp.mean(x * x, axis=-1, keepdims=True) + RMS_EPS) * g


def _store_row_tiles(ref, x):
    m = x.shape[0]
    for j in range(ROW_SUB):
        ref[pl.ds(j, m, stride=ROW_SUB), :] = x[:, j * LANES:(j + 1) * LANES]


def _load_row_tiles(ref, m):
    return jnp.concatenate([ref[pl.ds(j, m, stride=ROW_SUB), :] for j in range(ROW_SUB)], axis=1)


def _sigmoid(x):
    return 0.5 * jnp.tanh(0.5 * x) + 0.5


def _in_proj_body(xp_ref, xs_ref, g_ref, w_ref, zr_ref, zg_ref, zq_ref, sg_ref, *, np_tiles):
    x = jnp.where(pl.program_id(0) < np_tiles, xp_ref[...], xs_ref[...])
    h = _rms(x, g_ref[...]).astype(BF16)
    zr_ref[...] = jnp.dot(h, w_ref[:, 0:OFF_GMLP], preferred_element_type=F32)
    zg_ref[...] = jnp.dot(h, w_ref[:, OFF_GMLP:OFF_QMEM], preferred_element_type=F32)
    zq_ref[...] = jnp.dot(h, w_ref[:, OFF_QMEM:OFF_GATE], preferred_element_type=F32).astype(zq_ref.dtype)
    gates = jnp.dot(h, w_ref[:, OFF_GATE:IN_WIDTH], preferred_element_type=F32)
    sg_ref[...] = _sigmoid(gates).astype(BF16)


def _in_proj(x_p, x_s, g, w_bf16):
    n_p = x_p.shape[0]
    n = n_p + x_s.shape[0]
    tm = PROJ_TM
    np_tiles = n_p // tm
    row = lambda i: (i, 0)
    const = lambda i: (0, 0)
    return pl.pallas_call(
        functools.partial(_in_proj_body, np_tiles=np_tiles),
        grid=(n // tm,),
        in_specs=[
            pl.BlockSpec((tm, D_MODEL), lambda i: (jnp.minimum(i, np_tiles - 1), 0)),
            pl.BlockSpec((tm, D_MODEL), lambda i: (jnp.maximum(i - np_tiles, 0), 0)),
            pl.BlockSpec((1, D_MODEL), const),
            pl.BlockSpec((D_MODEL, IN_WIDTH), const, pipeline_mode=pl.Buffered(1)),
        ],
        out_specs=[
            pl.BlockSpec((tm, SHIFT_WIDTH), row),
            pl.BlockSpec((tm, 2 * GMLP_WIDTH), row),
            pl.BlockSpec((tm, MEM_WIDTH), row),
            pl.BlockSpec((tm, 3 * D_MODEL), row),
        ],
        out_shape=[
            jax.ShapeDtypeStruct((n, SHIFT_WIDTH), F32),
            jax.ShapeDtypeStruct((n, 2 * GMLP_WIDTH), F32),
            jax.ShapeDtypeStruct((n, MEM_WIDTH), BF16),
            jax.ShapeDtypeStruct((n, 3 * D_MODEL), BF16),
        ],
        compiler_params=_params(("parallel",)),
        name="in_proj",
    )(x_p, x_s, g, w_bf16)


def _mem_kv_body(x_ref, g_ref, w_ref, k_ref, v_ref):
    h = _rms(x_ref[...], g_ref[...]).astype(BF16)
    k_ref[...] = jnp.dot(h, w_ref[:, 0:MEM_WIDTH], preferred_element_type=F32)
    v_ref[...] = jnp.dot(h, w_ref[:, MEM_WIDTH:2 * MEM_WIDTH], preferred_element_type=F32)


def _mem_kv(mem, g, w_bf16):
    n = mem.shape[0]
    tm = PROJ_TM
    return pl.pallas_call(
        _mem_kv_body,
        grid=(n // tm,),
        in_specs=[
            pl.BlockSpec((tm, D_MODEL), lambda i: (i, 0)),
            pl.BlockSpec((1, D_MODEL), lambda i: (0, 0)),
            pl.BlockSpec((D_MODEL, 2 * MEM_WIDTH), lambda i: (0, 0)),
        ],
        out_specs=[pl.BlockSpec((tm, MEM_WIDTH), lambda i: (i, 0))] * 2,
        out_shape=[jax.ShapeDtypeStruct((n, MEM_WIDTH), F32)] * 2,
        compiler_params=_params(("parallel",)),
        name="mem_kv",
    )(mem, g, w_bf16)


EXP_M05 = math.exp(-0.5)


def _dotp(a, b, dims, passes):
    return _dot3(a, b, dims) if passes == 3 else _bdot(a, b, dims)


PREC = dict(a=1, t=1, lkv=1, tw=1, am=1, gh=3, y=1, s=3)


def _unit_lower_inverse(lows, c):
    rows = lax.broadcasted_iota(jnp.int32, (c, c), 0)
    cols = lax.broadcasted_iota(jnp.int32, (c, c), 1)
    eye = (rows == cols).astype(F32)
    invs = [eye - low for low in lows]
    powers = lows
    for _ in range(int(math.log2(c)) - 1):
        powers = [_dotp(pw, pw, NN, PREC["t"]) for pw in powers]
        invs = [inv + _dotp(inv, pw, NN, PREC["t"]) for inv, pw in zip(invs, powers)]
    return invs


def _rwkv_body(zr_ref, shift_ref, s0_ref, mu_ref, w0_ref, wd_ref, a0_ref, wa_ref, wg_ref, kk_ref, ka_ref,
               rk_ref, lng_ref, lnb_ref, hsum_ref,
               o_ref, sout_ref, shout_ref,
               carry_ref, state_ref, r_s, k_s, v_s, kk_s, b_s, ld_s, y_s, bon_s, g_s,
               rw_s, y0_s, gm_s, h0_s, *, tt, c, nseq):
    i = pl.program_id(1)
    rw = RWKV_WIDTH
    nh = RWKV_HEADS

    @pl.when(i == 0)
    def _():
        for q in range(nseq):
            state_ref[q * nh:(q + 1) * nh] = s0_ref[q]

    z = zr_ref[0]
    z_prev = pltpu.roll(z, 1, 0)
    row = lax.broadcasted_iota(jnp.int32, z.shape, 0)
    if nseq == 1:
        @pl.when(i == 0)
        def _():
            carry_ref[...] = shift_ref[0]

        z_prev = jnp.where(row == 0, carry_ref[...], z_prev)
        carry_ref[...] = z[tt - 1:tt, :]
    else:
        first_rows = jnp.concatenate([jnp.broadcast_to(shift_ref[q], (c, SHIFT_WIDTH)) for q in range(nseq)], axis=0)
        z_prev = jnp.where(row % c == 0, first_rows, z_prev)
    zs = z + mu_ref[...] * (z_prev - z)
    r = zs[:, 0:rw]
    k = zs[:, rw:2 * rw]
    v = zs[:, 2 * rw:3 * rw]
    zw = zs[:, 3 * rw:3 * rw + DECAY_LORA]
    za = zs[:, 3 * rw + DECAY_LORA:3 * rw + DECAY_LORA + A_LORA]
    zg = zs[:, 3 * rw + DECAY_LORA + A_LORA:SHIFT_WIDTH]
    xw = w0_ref[...] + _bdot(jnp.tanh(zw), wd_ref[...])
    ld_s[...] = -EXP_M05 * _sigmoid(xw)
    a = _sigmoid(a0_ref[...] + _bdot(za, wa_ref[...]))
    g_s[...] = _bdot(_sigmoid(zg), wg_ref[...])
    kk = k * kk_ref[...]
    k = k * (1.0 + (a - 1.0) * ka_ref[...])
    r_s[...] = r
    k_s[...] = k
    v_s[...] = v

    def head_sum(t):
        return jnp.dot(t.astype(BF16), hsum_ref[...], preferred_element_type=F32)

    kk = kk / jnp.maximum(jnp.sqrt(head_sum(kk * kk)), 1e-12)
    kk_s[...] = kk
    b_s[...] = kk * a
    bon_s[...] = head_sum(r * k * rk_ref[...]) * v

    rows_c = lax.broadcasted_iota(jnp.int32, (c, c), 0)
    cols_c = lax.broadcasted_iota(jnp.int32, (c, c), 1)
    strict = rows_c > cols_c
    incl = rows_c >= cols_c
    tril_ones = incl.astype(BF16)
    rows_2c = lax.broadcasted_iota(jnp.int32, (c, 2 * c), 0)
    cols_2c = lax.broadcasted_iota(jnp.int32, (c, 2 * c), 1)
    incl2 = rows_2c >= jnp.where(cols_2c >= c, cols_2c - c, cols_2c)
    rows_k = lax.broadcasted_iota(jnp.int32, (HEAD_DIM, HEAD_DIM), 0)
    cols_k = lax.broadcasted_iota(jnp.int32, (HEAD_DIM, HEAD_DIM), 1)
    eye_k = (rows_k == cols_k).astype(F32)
    zeros_cv = jnp.zeros((c, HEAD_DIM), F32)
    heads = [slice(h * HEAD_DIM, (h + 1) * HEAD_DIM) for h in range(RWKV_HEADS)]

    n_chunks = tt // c
    group = nseq if nseq > 1 else math.gcd(n_chunks, RWKV_GROUP)

    def chunk_rows(ci):
        return pl.ds(ci * c, c) if isinstance(ci, int) else pl.ds(pl.multiple_of(ci * c, c), c)

    def scaled(ci):
        rows = chunk_rows(ci)
        ld = ld_s[rows, :]
        ld_hi, ld_lo = _split(ld)
        cum = (jnp.dot(tril_ones, ld_hi, preferred_element_type=F32)
               + jnp.dot(tril_ones, ld_lo, preferred_element_type=F32))
        e_inc = jnp.exp(cum)
        e_neg = jnp.exp(-cum)
        kt = k_s[rows, :] * e_neg
        bt = b_s[rows, :] * e_neg
        g_end = e_inc[c - 1:c, :]
        return dict(ci=ci, rt=r_s[rows, :] * e_inc, kkt=kk_s[rows, :] * jnp.exp(cum - ld), kt=kt, bt=bt,
                    g_end=g_end, bc=bt * g_end, kc=kt * g_end, vv=v_s[rows, :])

    def phase1(cj, carry):
        chains = [(ch, h, s) for ch in [scaled(cj * group + g) for g in range(group)]
                  for h, s in enumerate(heads)]
        x = [jnp.concatenate([ch["kkt"][:, s], ch["rt"][:, s]], axis=0) for ch, _, s in chains]
        zz = [jnp.concatenate([ch["bt"][:, s], ch["kt"][:, s]], axis=0) for ch, _, s in chains]
        amat = [_dotp(xh, zh, NT, PREC["a"]) for xh, zh in zip(x, zz)]
        l_b = [jnp.where(strict, am[0:c, 0:c], 0.0) for am in amat]
        l_k = [jnp.where(strict, am[0:c, c:2 * c], 0.0) for am in amat]
        a_r = [jnp.where(incl2, am[c:2 * c, :], 0.0) for am in amat]
        tinv = _unit_lower_inverse(l_b, c)
        lkv = [_dotp(lk, ch["vv"][:, s], NN, PREC["lkv"]) for lk, (ch, _, s) in zip(l_k, chains)]
        wu = [-_dotp(t, jnp.concatenate([ch["kkt"][:, s], lv], axis=1), NN, PREC["tw"])
              for t, lv, (ch, _, s) in zip(tinv, lkv, chains)]
        m = [jnp.concatenate([w, jnp.concatenate([zeros_cv, ch["vv"][:, s]], axis=1)], axis=0)
             for w, (ch, _, s) in zip(wu, chains)]
        am2 = [_dotp(ar, mh, NN, PREC["am"]) for ar, mh in zip(a_r, m)]
        gh = [_dotp(mh, jnp.concatenate([ch["bc"][:, s], ch["kc"][:, s]], axis=0), TN, PREC["gh"])
              for mh, (ch, _, s) in zip(m, chains)]
        for q, (ch, h, s) in enumerate(chains):
            idx = ch["ci"] * RWKV_HEADS + h
            rw_s[idx] = ch["rt"][:, s] + am2[q][:, 0:HEAD_DIM]
            y0_s[idx] = am2[q][:, HEAD_DIM:2 * HEAD_DIM]
            gm_s[idx] = gh[q][0:HEAD_DIM, :] + eye_k * ch["g_end"][:, s]
            h0_s[idx] = gh[q][HEAD_DIM:2 * HEAD_DIM, :]
        return carry

    if n_chunks == group:
        phase1(0, 0)
    else:
        lax.fori_loop(0, n_chunks // group, phase1, 0)

    if nseq == 1:
        def phase2(ci, carry):
            rows = chunk_rows(ci)
            for h, s in enumerate(heads):
                idx = ci * nh + h
                s_h = state_ref[h]
                y_s[rows, s] = _dotp(rw_s[idx], s_h, NT, PREC["y"]) + y0_s[idx]
                state_ref[h] = _dotp(s_h, gm_s[idx], NN, PREC["s"]) + h0_s[idx]
            return carry

        lax.fori_loop(0, n_chunks, phase2, 0)
    else:
        pairs = [(q, h, s) for q in range(nseq) for h, s in enumerate(heads)]
        states = [state_ref[q * nh + h] for q, h, _ in pairs]
        ys = [_dotp(rw_s[q * nh + h], st, NT, PREC["y"]) + y0_s[q * nh + h] for (q, h, _), st in zip(pairs, states)]
        new = [_dotp(st, gm_s[q * nh + h], NN, PREC["s"]) + h0_s[q * nh + h] for (q, h, _), st in zip(pairs, states)]
        for (q, h, s), yq, nq in zip(pairs, ys, new):
            y_s[chunk_rows(q), s] = yq
            state_ref[q * nh + h] = nq
    y = y_s[...]
    yc = y - head_sum(y) * (1.0 / HEAD_DIM)
    var = head_sum(yc * yc) * (1.0 / HEAD_DIM)
    yn = yc * lax.rsqrt(var + GN_EPS)
    o_ref[0] = ((yn * lng_ref[...] + lnb_ref[...] + bon_s[...]) * g_s[...]).astype(o_ref.dtype)

    @pl.when(i == pl.num_programs(1) - 1)
    def _():
        for q in range(nseq):
            sout_ref[q] = state_ref[q * nh:(q + 1) * nh]
            last = tt - 1 if nseq == 1 else (q + 1) * c - 1
            shout_ref[q] = zr_ref[0, last:last + 1, :]


def _rwkv(zr3, blk_off, b, nt, shift_prev, s_prev, p, tt, c, nseq):
    vec = lambda n: pl.BlockSpec((1, n), lambda bi, ti: (0, 0))
    mat = lambda m, n: pl.BlockSpec((m, n), lambda bi, ti: (0, 0))
    rw = RWKV_WIDTH
    scr = lambda: pltpu.VMEM((tt, rw), F32)
    nch = (tt // c) * RWKV_HEADS
    return pl.pallas_call(
        functools.partial(_rwkv_body, tt=tt, c=c, nseq=nseq),
        grid=(b // nseq, nt),
        in_specs=[
            pl.BlockSpec((1, tt, SHIFT_WIDTH), lambda bi, ti: (blk_off + bi * nt + ti, 0, 0)),
            pl.BlockSpec((nseq, 1, SHIFT_WIDTH), lambda bi, ti: (bi, 0, 0)),
            pl.BlockSpec((nseq, RWKV_HEADS, HEAD_DIM, HEAD_DIM), lambda bi, ti: (bi, 0, 0, 0)),
            vec(SHIFT_WIDTH), vec(rw), mat(DECAY_LORA, rw), vec(rw), mat(A_LORA, rw), mat(GATE_LORA, rw),
            vec(rw), vec(rw), vec(rw), vec(rw), vec(rw), mat(rw, rw),
        ],
        out_specs=[
            pl.BlockSpec((1, tt, rw), lambda bi, ti: (bi, ti, 0)),
            pl.BlockSpec((nseq, RWKV_HEADS, HEAD_DIM, HEAD_DIM), lambda bi, ti: (bi, 0, 0, 0)),
            pl.BlockSpec((nseq, 1, SHIFT_WIDTH), lambda bi, ti: (bi, 0, 0)),
        ],
        out_shape=[
            jax.ShapeDtypeStruct((b // nseq, nt * tt, rw), BF16),
            jax.ShapeDtypeStruct((b, RWKV_HEADS, HEAD_DIM, HEAD_DIM), F32),
            jax.ShapeDtypeStruct((b, 1, SHIFT_WIDTH), F32),
        ],
        scratch_shapes=[
            pltpu.VMEM((1, SHIFT_WIDTH), F32),
            pltpu.VMEM((nseq * RWKV_HEADS, HEAD_DIM, HEAD_DIM), F32),
            scr(), scr(), scr(), scr(), scr(), scr(), scr(), scr(), scr(),
            pltpu.VMEM((nch, c, HEAD_DIM), F32),
            pltpu.VMEM((nch, c, HEAD_DIM), F32),
            pltpu.VMEM((nch, HEAD_DIM, HEAD_DIM), F32),
            pltpu.VMEM((nch, HEAD_DIM, HEAD_DIM), F32),
        ],
        compiler_params=_params(("parallel", "arbitrary")),
        name="rwkv",
    )(zr3, shift_prev, s_prev, p["mu"], p["w0"], p["wd"], p["a0"], p["wa"], p["wg"], p["kk"], p["ka"],
      p["rk"], p["lng"], p["lnb"], p["hsum"])


def _gmlp_body(z_ref, lng_ref, lnb_ref, ws_ref, bs_ref, o_ref, v_ref):
    z = z_ref[...]
    ge = 0.5 * z * (1.0 + lax.erf(z * (1.0 / math.sqrt(2.0))))
    u = ge[:, 0:GMLP_WIDTH]
    v = ge[:, GMLP_WIDTH:2 * GMLP_WIDTH]
    mean = jnp.mean(v, axis=-1, keepdims=True)
    vc = v - mean
    var = jnp.mean(vc * vc, axis=-1, keepdims=True)
    vn = vc * lax.rsqrt(var + LN_EPS) * lng_ref[...] + lnb_ref[...]
    v_ref[...] = vn
    gd = GMLP_WIDTH // GMLP_GROUPS
    for q in range(GMLP_TILE):
        rows = slice(q * CHUNK, (q + 1) * CHUNK)
        for g in range(GMLP_GROUPS):
            gs = slice(g * gd, (g + 1) * gd)
            sv = jnp.dot(ws_ref[g], vn[rows, gs].astype(BF16), preferred_element_type=F32) + bs_ref[:, g:g + 1]
            o_ref[rows, gs] = (u[rows, gs] * sv).astype(o_ref.dtype)


def _gmlp(zg, tile_off, n_rows, lng, lnb, ws_bf16, bs):
    tile = GMLP_TILE * CHUNK
    return pl.pallas_call(
        _gmlp_body,
        grid=(n_rows // tile,),
        in_specs=[
            pl.BlockSpec((tile, 2 * GMLP_WIDTH), lambda i: (tile_off + i, 0)),
            pl.BlockSpec((1, GMLP_WIDTH), lambda i: (0, 0)),
            pl.BlockSpec((1, GMLP_WIDTH), lambda i: (0, 0)),
            pl.BlockSpec((GMLP_GROUPS, CHUNK, CHUNK), lambda i: (0, 0, 0)),
            pl.BlockSpec((CHUNK, GMLP_GROUPS), lambda i: (0, 0)),
        ],
        out_specs=[
            pl.BlockSpec((tile, GMLP_WIDTH), lambda i: (i, 0)),
            pl.BlockSpec((tile, GMLP_WIDTH), lambda i: (i, 0)),
        ],
        out_shape=[
            jax.ShapeDtypeStruct((n_rows, GMLP_WIDTH), BF16),
            jax.ShapeDtypeStruct((n_rows, GMLP_WIDTH), F32),
        ],
        compiler_params=_params(("parallel",)),
        name="gmlp",
    )(zg, lng, lnb, ws_bf16, bs)


def _mem_attn_body(q_ref, k_ref, v_ref, o_ref, *, bb, feature_major):
    hd = MEM_WIDTH // MEM_HEADS
    pairs = [(s, slice(h * hd, (h + 1) * hd)) for s in range(bb) for h in range(MEM_HEADS)]
    if feature_major:
        scores = [_bdot(q_ref[s, :, hs], k_ref[s, hs, :], NN) * (hd ** -0.5) for s, hs in pairs]
    else:
        scores = [_bdot(q_ref[s, :, hs], k_ref[s, :, hs], NT) * (hd ** -0.5) for s, hs in pairs]
    probs = []
    for sc in scores:
        pr = jnp.exp(sc - jnp.max(sc, axis=-1, keepdims=True))
        probs.append(pr / jnp.sum(pr, axis=-1, keepdims=True))
    for (s, hs), pr in zip(pairs, probs):
        if feature_major:
            o_ref[s, :, hs] = _bdot(pr, v_ref[s, hs, :], NT).astype(o_ref.dtype)
        else:
            o_ref[s, :, hs] = _bdot(pr, v_ref[s, :, hs]).astype(o_ref.dtype)


def _mem_attn(q3, blk_off, b, nt, mk, mv, tt, bb, feature_major):
    return pl.pallas_call(
        functools.partial(_mem_attn_body, bb=bb, feature_major=feature_major),
        grid=(b // bb, nt),
        in_specs=[
            pl.BlockSpec((bb, tt, MEM_WIDTH), lambda bi, ti: (blk_off // bb + bi * nt + ti, 0, 0)),
            pl.BlockSpec((bb, N_MEM, MEM_WIDTH), lambda bi, ti: (bi, 0, 0)),
            pl.BlockSpec((bb, N_MEM, MEM_WIDTH), lambda bi, ti: (bi, 0, 0)),
        ],
        out_specs=pl.BlockSpec((bb, tt, MEM_WIDTH), lambda bi, ti: (bi, ti, 0)),
        out_shape=jax.ShapeDtypeStruct((b, nt * tt, MEM_WIDTH), BF16),
        compiler_params=_params(("parallel", "parallel")),
        name="mem_attn",
    )(q3, mk, mv)


def _merge_body(x_p, x_s, orw_p, orw_s, ogm_p, ogm_s, ome_p, ome_s, sg_ref, wbr_ref, wbg_ref, wbm_ref, wout_ref,
                gffn_ref, wrt_ref, brt_ref, x2_ref, h2_ref, route_ref, route_t_ref, cnt_ref, count_s, *, np_tiles):
    i = pl.program_id(0)
    d = D_MODEL
    tm = x_p.shape[0]
    is_p = i < np_tiles
    x = jnp.where(is_p, x_p[...], x_s[...])
    orw = jnp.where(is_p, orw_p[...], orw_s[...])
    ogm = jnp.where(is_p, ogm_p[...], ogm_s[...])
    ome = jnp.where(is_p, ome_p[...], ome_s[...])
    merged = sg_ref[:, 0:d].astype(F32) * _bdot(orw, wbr_ref[...])
    merged = merged + sg_ref[:, d:2 * d].astype(F32) * _bdot(ogm, wbg_ref[...])
    merged = merged + sg_ref[:, 2 * d:3 * d].astype(F32) * _bdot(ome, wbm_ref[...])
    x2 = x + _bdot(merged, wout_ref[...])
    x2_ref[...] = x2
    h2 = _rms(x2, gffn_ref[...])
    _store_row_tiles(h2_ref, h2)

    @pl.when(i == 0)
    def _():
        count_s[...] = jnp.zeros_like(count_s)

    lane = lax.broadcasted_iota(jnp.int32, (tm, LANES), 1)
    logits = jnp.where(lane < N_EXPERTS, _dot3(h2, wrt_ref[...]) + brt_ref[...], -jnp.inf)
    lane_f = lane.astype(F32)
    tops, hots, idxs = [], [], []
    for _ in range(TOP_K):
        top = jnp.max(logits, axis=-1, keepdims=True)
        idx = jnp.min(jnp.where(logits == top, lane_f, float(LANES)), axis=-1, keepdims=True)
        hot = lane_f == idx
        logits = jnp.where(hot, -jnp.inf, logits)
        tops.append(top)
        hots.append(hot)
        idxs.append(idx)
    weights = [jnp.exp(t - tops[0]) for t in tops]
    denom = weights[0] + weights[1] + weights[2] + weights[3]
    onehot = jnp.zeros((tm, LANES), F32)
    for hot in hots:
        onehot = onehot + jnp.where(hot, 1.0, 0.0)
    rows_t = lax.broadcasted_iota(jnp.int32, (tm, tm), 0)
    cols_t = lax.broadcasted_iota(jnp.int32, (tm, tm), 1)
    ahead = jnp.where(rows_t > cols_t, 1.0, 0.0).astype(BF16)
    prefix = jnp.dot(ahead, onehot.astype(BF16), preferred_element_type=F32) + count_s[...]
    route = jnp.zeros((tm, LANES), F32)
    for kk in range(TOP_K):
        rank = jnp.sum(jnp.where(hots[kk], prefix, 0.0), axis=-1, keepdims=True)
        route = jnp.where(lane == ROUTE_E + kk, idxs[kk], route)
        route = jnp.where(lane == ROUTE_RANK + kk, rank, route)
        route = jnp.where(lane == ROUTE_GATE + kk, weights[kk] / denom, route)
    route_ref[...] = route
    route_t_ref[...] = route.T
    count_s[...] = count_s[...] + jnp.sum(onehot, axis=0, keepdims=True)
    cnt_ref[...] = jnp.broadcast_to(count_s[...], cnt_ref.shape)


def _merge(x, o_rw, o_gm, o_me, sg, wbr, wbg, wbm, wout, gffn, wrt_pad, brt_pad):
    n_p = x[0].shape[0]
    n = n_p + x[1].shape[0]
    tm = MERGE_TM
    d = D_MODEL
    np_tiles = n_p // tm
    row = lambda i: (i, 0)
    const = lambda i: (0, 0)
    first =lambda i: (jnp.minimum(i, np_tiles - 1), 0)
    second = lambda i: (jnp.maximum(i - np_tiles, 0), 0)

    def pair(width):
        return [pl.BlockSpec((tm, width), first), pl.BlockSpec((tm, width), second)]

    return pl.pallas_call(
        functools.partial(_merge_body, np_tiles=np_tiles),
        grid=(n // tm,),
        in_specs=pair(d) + pair(RWKV_WIDTH) + pair(GMLP_WIDTH) + pair(MEM_WIDTH) + [
            pl.BlockSpec((tm, 3 * d), row),
            pl.BlockSpec((RWKV_WIDTH, d), const),
            pl.BlockSpec((GMLP_WIDTH, d), const),
            pl.BlockSpec((MEM_WIDTH, d), const),
            pl.BlockSpec((d, d), const),
            pl.BlockSpec((1, d), const),
            pl.BlockSpec((d, LANES), const),
            pl.BlockSpec((1, LANES), const),
        ],
        out_specs=[
            pl.BlockSpec((tm, d), row),
            pl.BlockSpec((tm * ROW_SUB, LANES), row),
            pl.BlockSpec((tm, LANES), row),
            pl.BlockSpec((LANES, tm), lambda i: (0, i)),
            pl.BlockSpec((8, LANES), const),
        ],
        out_shape=[
            jax.ShapeDtypeStruct((n, d), F32),
            jax.ShapeDtypeStruct((n * ROW_SUB, LANES), F32),
            jax.ShapeDtypeStruct((n, LANES), F32),
            jax.ShapeDtypeStruct((LANES, n), F32),
            jax.ShapeDtypeStruct((8, LANES), F32),
        ],
        scratch_shapes=[pltpu.VMEM((1, LANES), F32)],
        compiler_params=_params(("arbitrary",)),
        name="merge",
    )(x[0], x[1], o_rw[0], o_rw[1], o_gm[0], o_gm[1], o_me[0], o_me[1], sg, wbr, wbg, wbm, wout, gffn, wrt_pad,
      brt_pad)


N_ZERO_BLOCKS = 2 * N_EXPERTS


def _dispatch_body(dest_ref, zlist_ref, h_ref, xs_hbm, zero_s, sem_z, sem):
    i = pl.program_id(0)
    tm = DISPATCH_TM
    bm = MOE_BM

    @pl.when(i == 0)
    def _():
        zero_s[...] = jnp.zeros_like(zero_s)

        def zero_copy(q):
            start = pl.multiple_of(zlist_ref[q] * bm, bm)
            return pltpu.make_async_copy(zero_s, xs_hbm.at[pl.ds(start, bm)], sem_z)

        def start(q, carry):
            @pl.when(zlist_ref[q] >= 0)
            def _():
                zero_copy(q).start()
            return carry

        def wait(q, carry):
            @pl.when(zlist_ref[q] >= 0)
            def _():
                zero_copy(q).wait()
            return carry

        lax.fori_loop(0, N_ZERO_BLOCKS, start, 0)
        lax.fori_loop(0, N_ZERO_BLOCKS, wait, 0)

    base = i * tm
    n_tok = pl.num_programs(0) * tm

    def body(r, carry):
        for kk in range(TOP_K):
            slot = dest_ref[kk * n_tok + base + r]
            pltpu.make_async_copy(h_ref.at[r], xs_hbm.at[slot], sem).start(priority=kk % 2)
        return carry

    lax.fori_loop(0, tm, body, 0, unroll=2)
    for kk in range(TOP_K):
        pltpu.make_async_copy(h_ref, xs_hbm.at[pl.ds(0, tm)], sem).wait()


def _dispatch(dest, zlist, h, n_blocks):
    n = h.shape[0]
    tm = DISPATCH_TM
    grid_spec = pltpu.PrefetchScalarGridSpec(
        num_scalar_prefetch=2,
        grid=(n // tm,),
        in_specs=[pl.BlockSpec((tm, ROW_SUB, LANES), lambda i, dest, zl: (i, 0, 0))],
        out_specs=pl.BlockSpec(memory_space=pl.ANY),
        scratch_shapes=[
            pltpu.VMEM((MOE_BM, ROW_SUB, LANES), F32),
            pltpu.SemaphoreType.DMA(()),
            pltpu.SemaphoreType.DMA(()),
        ],
    )
    return pl.pallas_call(
        _dispatch_body,
        grid_spec=grid_spec,
        out_shape=jax.ShapeDtypeStruct((n_blocks * MOE_BM, ROW_SUB, LANES), F32),
        compiler_params=_params(("arbitrary",)),
        name="moe_dispatch",
    )(dest, zlist, h)


CAST_ROWS = 64


def _moe_body(be_ref, valid_ref, x_ref, w1_ref, b1_ref, w2_ref, b2_ref, o_ref, w1b, w2b):
    j = pl.program_id(0)
    n_valid = valid_ref[j]
    bm = MOE_BM
    half = bm // 2

    @pl.when(n_valid > 0)
    def _():
        changed = jnp.logical_or(j == 0, be_ref[j] != be_ref[jnp.maximum(j - 1, 0)])

        @pl.when(changed)
        def _():
            def cast1(q, carry):
                rows = pl.ds(pl.multiple_of(q * CAST_ROWS, CAST_ROWS), CAST_ROWS)
                w1b[rows, :] = w1_ref[0, rows, :].astype(BF16)
                return carry

            def cast2(q, carry):
                rows = pl.ds(pl.multiple_of(q * CAST_ROWS, CAST_ROWS), CAST_ROWS)
                w2b[rows, :] = w2_ref[0, rows, :].astype(BF16)
                return carry

            lax.fori_loop(0, D_MODEL // CAST_ROWS, cast1, 0)
            lax.fori_loop(0, D_FF // CAST_ROWS, cast2, 0)

    def ffn(rows):
        x = _load_row_tiles(x_ref, rows).astype(BF16)
        z = jnp.dot(x, w1b[...], preferred_element_type=F32) + b1_ref[0]
        zg = jnp.minimum(z[:, 0:D_FF], SWIGLU_LIMIT)
        zl = jnp.clip(z[:, D_FF:2 * D_FF], -SWIGLU_LIMIT, SWIGLU_LIMIT)
        act = zg * _sigmoid(SWIGLU_ALPHA * zg) * (zl + 1.0)
        _store_row_tiles(o_ref, jnp.dot(act.astype(BF16), w2b[...], preferred_element_type=F32) + b2_ref[0])

    @pl.when(n_valid > half)
    def _():
        ffn(bm)

    @pl.when(jnp.logical_and(n_valid > 0, n_valid <= half))
    def _():
        ffn(half)
        o_ref[half * ROW_SUB:bm * ROW_SUB, :] = jnp.zeros((half * ROW_SUB, LANES), F32)

    @pl.when(n_valid == 0)
    def _():
        o_ref[...] = jnp.zeros_like(o_ref)


def _moe(blk_e, blk_valid, xs, w1, b1, w2, b2, n_blocks):
    bm = MOE_BM
    d = D_MODEL
    grid_spec = pltpu.PrefetchScalarGridSpec(
        num_scalar_prefetch=2,
        grid=(n_blocks,),
        in_specs=[
            pl.BlockSpec((bm * ROW_SUB, LANES), lambda j, be, nu: (j, 0)),
            pl.BlockSpec((1, d, 2 * D_FF), lambda j, be, nu: (be[j], 0, 0)),
            pl.BlockSpec((1, 1, 2 * D_FF), lambda j, be, nu: (be[j], 0, 0)),
            pl.BlockSpec((1, D_FF, d), lambda j, be, nu: (be[j], 0, 0)),
            pl.BlockSpec((1, 1, d), lambda j, be, nu: (be[j], 0, 0)),
        ],
        out_specs=pl.BlockSpec((bm * ROW_SUB, LANES), lambda j, be, nu: (j, 0)),
        scratch_shapes=[
            pltpu.VMEM((d, 2 * D_FF), BF16),
            pltpu.VMEM((D_FF, d), BF16),
        ],
    )
    return pl.pallas_call(
        _moe_body,
        grid_spec=grid_spec,
        out_shape=jax.ShapeDtypeStruct((n_blocks * bm * ROW_SUB, LANES), F32),
        compiler_params=_params(("arbitrary",)),
        name="moe_ffn",
    )(blk_e, blk_valid, xs, w1, b1, w2, b2)


def _combine_body(dest_ref, x2_ref, route_ref, gfin_ref, yb_hbm, op_ref, os_ref, buf0, buf1, sem, *, np_steps):
    i = pl.program_id(0)
    n_steps = pl.num_programs(0)
    tm = COMBINE_TM
    n_tok = n_steps * (2 * tm)
    bufs = (buf0, buf1)

    def issue(tile, which):
        base = tile * tm
        for r in range(tm):
            for kk in range(TOP_K):
                slot = dest_ref[kk * n_tok + base + r]
                src = pl.ds(pl.multiple_of(slot * ROW_SUB, ROW_SUB), ROW_SUB)
                pltpu.make_async_copy(yb_hbm.at[src], bufs[which].at[kk, r * ROW_SUB:(r + 1) * ROW_SUB],
                                      sem.at[which]).start(priority=kk % 2)

    def wait(which):
        for kk in range(TOP_K):
            pltpu.make_async_copy(yb_hbm.at[pl.ds(0, tm * ROW_SUB)], bufs[which].at[kk], sem.at[which]).wait()

    def reduce(which, half):
        rows = slice(half * tm, (half + 1) * tm)
        acc = x2_ref[rows, :]
        for kk in range(TOP_K):
            gate = route_ref[rows, ROUTE_GATE + kk:ROUTE_GATE + kk + 1]
            acc = acc + gate * _load_row_tiles(bufs[which].at[kk], tm)
        return _rms(acc, gfin_ref[...])

    @pl.when(i == 0)
    def _():
        issue(0, 0)

    wait(0)
    issue(2 * i + 1, 1)
    y0 = reduce(0, 0)
    wait(1)
    issue(jnp.minimum(2 * i + 2, 2 * n_steps - 1), 0)
    y1 = reduce(1, 1)

    @pl.when(i < np_steps)
    def _():
        op_ref[0:tm, :] = y0
        op_ref[tm:2 * tm, :] = y1

    @pl.when(i >= np_steps)
    def _():
        os_ref[0:tm, :] = y0
        os_ref[tm:2 * tm, :] = y1

    @pl.when(i == n_steps - 1)
    def _():
        wait(0)


def _combine(dest, x2, route, gfin, yb, n_p):
    n, d = x2.shape
    tm = COMBINE_TM
    step = 2 * tm
    np_steps = n_p // step
    grid_spec = pltpu.PrefetchScalarGridSpec(
        num_scalar_prefetch=1,
        grid=(n // step,),
        in_specs=[
            pl.BlockSpec((step, d), lambda i, dest: (i, 0)),
            pl.BlockSpec((step, LANES), lambda i, dest: (i, 0)),
            pl.BlockSpec((1, d), lambda i, dest: (0, 0)),
            pl.BlockSpec(memory_space=pl.ANY),
        ],
        out_specs=[
            pl.BlockSpec((step, d), lambda i, dest: (jnp.minimum(i, np_steps - 1), 0)),
            pl.BlockSpec((step, d), lambda i, dest: (jnp.maximum(i - np_steps, 0), 0)),
        ],
        scratch_shapes=[
            pltpu.VMEM((TOP_K, tm * ROW_SUB, LANES), F32),
            pltpu.VMEM((TOP_K, tm * ROW_SUB, LANES), F32),
            pltpu.SemaphoreType.DMA((2,)),
        ],
    )
    return pl.pallas_call(
        functools.partial(_combine_body, np_steps=np_steps),
        grid_spec=grid_spec,
        out_shape=[jax.ShapeDtypeStruct((n_p, d), F32), jax.ShapeDtypeStruct((n - n_p, d), F32)],
        compiler_params=_params(("arbitrary",)),
        name="moe_combine",
    )(dest, x2, route, gfin, yb)


def _slot_tables(route_t, counts_row, n_blocks):
    bm = MOE_BM
    e = route_t[ROUTE_E:ROUTE_E + TOP_K].astype(jnp.int32)
    rank = route_t[ROUTE_RANK:ROUTE_RANK + TOP_K].astype(jnp.int32)
    counts = counts_row[:N_EXPERTS].astype(jnp.int32)
    padded = (counts + bm - 1) // bm * bm
    pad_end = jnp.cumsum(padded)
    pad_start = pad_end - padded
    experts = jnp.arange(N_EXPERTS, dtype=jnp.int32)[:, None, None]
    dest = (jnp.sum(jnp.where(e[None] == experts, pad_start[:, None, None], 0), axis=0) + rank).reshape(-1)
    blk_start = jnp.arange(n_blocks, dtype=jnp.int32) * bm
    blk_e = jnp.minimum(jnp.sum(pad_end[None, :] <= blk_start[:, None], axis=1), N_EXPERTS - 1).astype(jnp.int32)
    n_used = pad_end[-1] // bm
    last_blk = jnp.where(padded > 0, pad_end // bm - 1, -1)
    trailing = n_used + jnp.arange(N_EXPERTS, dtype=jnp.int32)
    trailing = jnp.where(trailing < n_blocks, trailing, -1)
    zlist = jnp.concatenate([last_blk, trailing]).astype(jnp.int32)
    blk_valid = jnp.clip(jnp.take(pad_start + counts, blk_e) - blk_start, 0, bm).astype(jnp.int32)
    return dest.astype(jnp.int32), blk_e, blk_valid, zlist


def kernel(x_prompt, x_sample, mem_prompt, state_shift, state_wkv, cache_mem_k, cache_mem_v, g_norm_mix, w_in, mu_shift, w0, w_decay_up, a0, w_a_up, w_g_up, k_k, k_a, r_k, ln_x_g, ln_x_b, gmlp_ln_g, gmlp_ln_b, w_spatial, b_spatial, g_norm_mem, w_mem_kv, w_br_rwkv, w_br_gmlp, w_br_mem, w_out, g_norm_ffn, w_router, b_router, w_exp1, b_exp1, w_exp2, b_exp2, g_norm_final):
    bp, tp, d = x_prompt.shape
    bs, ts, _ = x_sample.shape
    n_p, n_s = bp * tp, bs * ts
    n_all = n_p + n_s
    l = 0
    row = lambda a: a.reshape(1, -1)

    x_pair = (x_prompt.reshape(n_p, d), x_sample.reshape(n_s, d))
    zr, zg, zq, sg = _in_proj(x_pair[0], x_pair[1], row(g_norm_mix[l]), w_in[l].astype(BF16))

    mk_p, mv_p = _mem_kv(mem_prompt.reshape(bp * N_MEM, d), row(g_norm_mem[l]), w_mem_kv[l].astype(BF16))

    rp = dict(mu=row(mu_shift[l]), w0=row(w0[l]), wd=w_decay_up[l].astype(BF16), a0=row(a0[l]),
              wa=w_a_up[l].astype(BF16), wg=w_g_up[l].astype(BF16), kk=row(k_k[l]), ka=row(k_a[l]),
              rk=row(r_k[l]), lng=row(ln_x_g[l]), lnb=row(ln_x_b[l]))
    head_of = jnp.arange(RWKV_WIDTH, dtype=jnp.int32) // HEAD_DIM
    rp["hsum"] = (head_of[:, None] == head_of[None, :]).astype(BF16)
    o_rw_p, s_p, shift_p = _rwkv(zr.reshape(n_all // RWKV_TT, RWKV_TT, SHIFT_WIDTH), 0, bp, tp // RWKV_TT,
                        jnp.zeros((bp, 1, SHIFT_WIDTH), F32), jnp.zeros((bp, RWKV_HEADS, HEAD_DIM, HEAD_DIM), F32),
                        rp, tt=RWKV_TT, c=RWKV_C, nseq=1)
    tile_s = RWKV_SAMPLE_SEQS * ts
    o_rw_s, s_s, shift_s = _rwkv(zr.reshape(n_all // tile_s, tile_s, SHIFT_WIDTH), n_p // tile_s, bs, 1,
                        state_shift[l].reshape(bs, 1, SHIFT_WIDTH), state_wkv[l], rp, tt=tile_s, c=ts,
                        nseq=RWKV_SAMPLE_SEQS)

    tri = jnp.tril(jnp.ones((CHUNK, CHUNK), bool))
    ws_p = jnp.where(tri, w_spatial[l], 0.0).astype(BF16)
    bs_p = b_spatial[l].T
    reps = CHUNK // ts
    tri_s = jnp.tril(jnp.ones((ts, ts), bool))
    ws_small = jnp.where(tri_s, w_spatial[l][:, :ts, :ts], 0.0)
    eye = jnp.eye(reps, dtype=F32)
    ws_s = jnp.einsum("ab,gij->gaibj", eye, ws_small).reshape(GMLP_GROUPS, CHUNK, CHUNK).astype(BF16)
    bs_s = jnp.tile(b_spatial[l][:, :ts], (1, reps)).T
    lng, lnb = row(gmlp_ln_g[l]), row(gmlp_ln_b[l])
    o_gm_p, _ = _gmlp(zg, 0, n_p, lng, lnb, ws_p, bs_p)
    o_gm_s, v_rows_s = _gmlp(zg, n_p // (GMLP_TILE * CHUNK), n_s, lng, lnb, ws_s, bs_s)

    o_me_p = _mem_attn(zq.reshape(n_all // ATTN_TT, ATTN_TT, MEM_WIDTH), 0, bp, tp // ATTN_TT,
                       mk_p.reshape(bp, N_MEM, MEM_WIDTH), mv_p.reshape(bp, N_MEM, MEM_WIDTH), tt=ATTN_TT, bb=1,
                       feature_major=False)
    mk_s = jnp.transpose(cache_mem_k[l].reshape(bs, N_MEM, MEM_WIDTH), (0, 2, 1))
    mv_s = jnp.transpose(cache_mem_v[l].reshape(bs, N_MEM, MEM_WIDTH), (0, 2, 1))
    o_me_s = _mem_attn(zq.reshape(n_all // ts, ts, MEM_WIDTH), n_p // ts, bs, 1, mk_s, mv_s, tt=ts, bb=ATTN_BB,
                       feature_major=True)

    wrt_pad = jnp.zeros((d, LANES), F32).at[:, :N_EXPERTS].set(w_router[l])
    brt_pad = jnp.zeros((1, LANES), F32).at[0, :N_EXPERTS].set(b_router[l])
    x2, h2, route, route_t, counts = _merge(
        x_pair, (o_rw_p.reshape(n_p, RWKV_WIDTH), o_rw_s.reshape(n_s, RWKV_WIDTH)), (o_gm_p, o_gm_s),
        (o_me_p.reshape(n_p, MEM_WIDTH), o_me_s.reshape(n_s, MEM_WIDTH)), sg,
        w_br_rwkv[l].astype(BF16), w_br_gmlp[l].astype(BF16), w_br_mem[l].astype(BF16), w_out[l].astype(BF16),
        row(g_norm_ffn[l]), wrt_pad, brt_pad)

    n_assign = n_all * TOP_K
    n_blocks = -(-(n_assign + N_EXPERTS * (MOE_BM - 1)) // MOE_BM)
    dest, blk_e, blk_valid, zlist = _slot_tables(route_t, counts[0], n_blocks)
    xs = _dispatch(dest, zlist, h2.reshape(n_all, ROW_SUB, LANES), n_blocks)
    yb = _moe(blk_e, blk_valid, xs.reshape(n_blocks * MOE_BM * ROW_SUB, LANES), w_exp1[l],
              b_exp1[l].reshape(N_EXPERTS, 1, 2 * D_FF), w_exp2[l], b_exp2[l].reshape(N_EXPERTS, 1, d), n_blocks)
    y_p, y_s = _combine(dest, x2, route, row(g_norm_final), yb, n_p)

    mk_out = mk_p.reshape(1, bp, N_MEM, MEM_HEADS, MEM_WIDTH // MEM_HEADS)
    mv_out = mv_p.reshape(1, bp, N_MEM, MEM_HEADS, MEM_WIDTH // MEM_HEADS)
    return (y_p.reshape(bp, tp, d), y_s.reshape(bs, ts, d), shift_p.reshape(1, bp, SHIFT_WIDTH), s_p[None], mk_out,
            mv_out, shift_s.reshape(1, bs, SHIFT_WIDTH), s_s[None],
            v_rows_s.reshape(1, bs, ts, GMLP_WIDTH))
```

```python
import functools
import math

import jax
import jax.numpy as jnp
from jax import lax
from jax.experimental import pallas as pl
from jax.experimental.pallas import tpu as pltpu

F32 = jnp.float32
BF16 = jnp.bfloat16

D_MODEL = 1024
RWKV_HEADS = 8
HEAD_DIM = 64
RWKV_WIDTH = RWKV_HEADS * HEAD_DIM
DECAY_LORA = 64
A_LORA = 64
GATE_LORA = 128
GMLP_GROUPS = 4
GMLP_WIDTH = 256
CHUNK = 128
MEM_HEADS = 4
MEM_WIDTH = 256
N_MEM = 256
N_EXPERTS = 32
TOP_K = 4
D_FF = 1024
SWIGLU_ALPHA = 1.702
SWIGLU_LIMIT = 7.0
RMS_EPS = 1e-5
LN_EPS = 1e-5
GN_EPS = 64e-5
SHIFT_WIDTH = 3 * RWKV_WIDTH + DECAY_LORA + A_LORA + GATE_LORA
OFF_GMLP = SHIFT_WIDTH
OFF_QMEM = OFF_GMLP + 2 * GMLP_WIDTH
OFF_GATE = OFF_QMEM + MEM_WIDTH
IN_WIDTH = OFF_GATE + 3 * D_MODEL
LANES = 128
ROW_SUB = D_MODEL // LANES

PROJ_TM = 256
MERGE_TM = 512
RWKV_TT = 512
RWKV_C = 64
RWKV_GROUP = 4
RWKV_SAMPLE_SEQS = 8
GMLP_TILE = 4
ATTN_TT = 512
ATTN_BB = 8
MOE_BM = 512
DISPATCH_TM = 256
COMBINE_TM = 128

ROUTE_E = 0
ROUTE_RANK = 4
ROUTE_GATE = 8

NN = ((1,), (0,))
NT = ((1,), (1,))
TN = ((0,), (0,))

VMEM_LIMIT = 56 * 1024 * 1024


def _params(sem, vmem=VMEM_LIMIT):
    return pltpu.CompilerParams(dimension_semantics=sem, vmem_limit_bytes=vmem)


def _bdot(a, b, dims=NN):
    return lax.dot_general(a.astype(BF16), b.astype(BF16), (dims, ((), ())), preferred_element_type=F32)


def _split(x):
    hi = x.astype(BF16)
    lo = (x - hi.astype(F32)).astype(BF16)
    return hi, lo


def _dot3(a, b, dims=NN):
    dn = (dims, ((), ()))
    ah, al = _split(a)
    bh, bl = _split(b)
    r = lax.dot_general(ah, bh, dn, preferred_element_type=F32)
    r = r + lax.dot_general(al, bh, dn, preferred_element_type=F32)
    return r + lax.dot_general(ah, bl, dn, preferred_element_type=F32)


def _rms(x, g):
    return x * lax.rsqrt(jnp.mean(x * x, axis=-1, keepdims=True) + RMS_EPS) * g


def _store_row_tiles(ref, x):
    m = x.shape[0]
    for j in range(ROW_SUB):
        ref[pl.ds(j, m, stride=ROW_SUB), :] = x[:, j * LANES:(j + 1) * LANES]


def _load_row_tiles(ref, m):
    return jnp.concatenate([ref[pl.ds(j, m, stride=ROW_SUB), :] for j in range(ROW_SUB)], axis=1)


def _sigmoid(x):
    return 0.5 * jnp.tanh(0.5 * x) + 0.5


def _in_proj_body(xp_ref, xs_ref, g_ref, w_ref, zr_ref, zg_ref, zq_ref, sg_ref, *, np_tiles):
    x = jnp.where(pl.program_id(0) < np_tiles, xp_ref[...], xs_ref[...])
    h = _rms(x, g_ref[...]).astype(BF16)
    zr_ref[...] = jnp.dot(h, w_ref[:, 0:OFF_GMLP], preferred_element_type=F32)
    zg_ref[...] = jnp.dot(h, w_ref[:, OFF_GMLP:OFF_QMEM], preferred_element_type=F32)
    zq_ref[...] = jnp.dot(h, w_ref[:, OFF_QMEM:OFF_GATE], preferred_element_type=F32).astype(zq_ref.dtype)
    gates = jnp.dot(h, w_ref[:, OFF_GATE:IN_WIDTH], preferred_element_type=F32)
    sg_ref[...] = _sigmoid(gates).astype(BF16)


def _in_proj(x_p, x_s, g, w_bf16):
    n_p = x_p.shape[0]
    n = n_p + x_s.shape[0]
    tm = PROJ_TM
    np_tiles = n_p // tm
    row = lambda i: (i, 0)
    const = lambda i: (0, 0)
    return pl.pallas_call(
        functools.partial(_in_proj_body, np_tiles=np_tiles),
        grid=(n // tm,),
        in_specs=[
            pl.BlockSpec((tm, D_MODEL), lambda i: (jnp.minimum(i, np_tiles - 1), 0)),
            pl.BlockSpec((tm, D_MODEL), lambda i: (jnp.maximum(i - np_tiles, 0), 0)),
            pl.BlockSpec((1, D_MODEL), const),
            pl.BlockSpec((D_MODEL, IN_WIDTH), const, pipeline_mode=pl.Buffered(1)),
        ],
        out_specs=[
            pl.BlockSpec((tm, SHIFT_WIDTH), row),
            pl.BlockSpec((tm, 2 * GMLP_WIDTH), row),
            pl.BlockSpec((tm, MEM_WIDTH), row),
            pl.BlockSpec((tm, 3 * D_MODEL), row),
        ],
        out_shape=[
            jax.ShapeDtypeStruct((n, SHIFT_WIDTH), F32),
            jax.ShapeDtypeStruct((n, 2 * GMLP_WIDTH), F32),
            jax.ShapeDtypeStruct((n, MEM_WIDTH), BF16),
            jax.ShapeDtypeStruct((n, 3 * D_MODEL), BF16),
        ],
        compiler_params=_params(("parallel",)),
        name="in_proj",
    )(x_p, x_s, g, w_bf16)


def _mem_kv_body(x_ref, g_ref, w_ref, k_ref, v_ref):
    h = _rms(x_ref[...], g_ref[...]).astype(BF16)
    k_ref[...] = jnp.dot(h, w_ref[:, 0:MEM_WIDTH], preferred_element_type=F32)
    v_ref[...] = jnp.dot(h, w_ref[:, MEM_WIDTH:2 * MEM_WIDTH], preferred_element_type=F32)


def _mem_kv(mem, g, w_bf16):
    n = mem.shape[0]
    tm = PROJ_TM
    return pl.pallas_call(
        _mem_kv_body,
        grid=(n // tm,),
        in_specs=[
            pl.BlockSpec((tm, D_MODEL), lambda i: (i, 0)),
            pl.BlockSpec((1, D_MODEL), lambda i: (0, 0)),
            pl.BlockSpec((D_MODEL, 2 * MEM_WIDTH), lambda i: (0, 0)),
        ],
        out_specs=[pl.BlockSpec((tm, MEM_WIDTH), lambda i: (i, 0))] * 2,
        out_shape=[jax.ShapeDtypeStruct((n, MEM_WIDTH), F32)] * 2,
        compiler_params=_params(("parallel",)),
        name="mem_kv",
    )(mem, g, w_bf16)


EXP_M05 = math.exp(-0.5)


def _dotp(a, b, dims, passes):
    return _dot3(a, b, dims) if passes == 3 else _bdot(a, b, dims)


PREC = dict(a=1, t=1, lkv=1, tw=1, am=1, gh=3, y=1, s=3)


def _unit_lower_inverse(lows, c):
    rows = lax.broadcasted_iota(jnp.int32, (c, c), 0)
    cols = lax.broadcasted_iota(jnp.int32, (c, c), 1)
    eye = (rows == cols).astype(F32)
    invs = [eye - low for low in lows]
    powers = lows
    for _ in range(int(math.log2(c)) - 1):
        powers = [_dotp(pw, pw, NN, PREC["t"]) for pw in powers]
        invs = [inv + _dotp(inv, pw, NN, PREC["t"]) for inv, pw in zip(invs, powers)]
    return invs


def _rwkv_body(zr_ref, shift_ref, s0_ref, mu_ref, w0_ref, wd_ref, a0_ref, wa_ref, wg_ref, kk_ref, ka_ref,
               rk_ref, lng_ref, lnb_ref, hsum_ref,
               o_ref, sout_ref, shout_ref,
               carry_ref, state_ref, r_s, k_s, v_s, kk_s, b_s, ld_s, y_s, bon_s, g_s,
               rw_s, y0_s, gm_s, h0_s, *, tt, c, nseq):
    i = pl.program_id(1)
    rw = RWKV_WIDTH
    nh = RWKV_HEADS

    @pl.when(i == 0)
    def _():
        for q in range(nseq):
            state_ref[q * nh:(q + 1) * nh] = s0_ref[q]

    z = zr_ref[0]
    z_prev = pltpu.roll(z, 1, 0)
    row = lax.broadcasted_iota(jnp.int32, z.shape, 0)
    if nseq == 1:
        @pl.when(i == 0)
        def _():
            carry_ref[...] = shift_ref[0]

        z_prev = jnp.where(row == 0, carry_ref[...], z_prev)
        carry_ref[...] = z[tt - 1:tt, :]
    else:
        first_rows = jnp.concatenate([jnp.broadcast_to(shift_ref[q], (c, SHIFT_WIDTH)) for q in range(nseq)], axis=0)
        z_prev = jnp.where(row % c == 0, first_rows, z_prev)
    zs = z + mu_ref[...] * (z_prev - z)
    r = zs[:, 0:rw]
    k = zs[:, rw:2 * rw]
    v = zs[:, 2 * rw:3 * rw]
    zw = zs[:, 3 * rw:3 * rw + DECAY_LORA]
    za = zs[:, 3 * rw + DECAY_LORA:3 * rw + DECAY_LORA + A_LORA]
    zg = zs[:, 3 * rw + DECAY_LORA + A_LORA:SHIFT_WIDTH]
    xw = w0_ref[...] + _bdot(jnp.tanh(zw), wd_ref[...])
    ld_s[...] = -EXP_M05 * _sigmoid(xw)
    a = _sigmoid(a0_ref[...] + _bdot(za, wa_ref[...]))
    g_s[...] = _bdot(_sigmoid(zg), wg_ref[...])
    kk = k * kk_ref[...]
    k = k * (1.0 + (a - 1.0) * ka_ref[...])
    r_s[...] = r
    k_s[...] = k
    v_s[...] = v

    def head_sum(t):
        return jnp.dot(t.astype(BF16), hsum_ref[...], preferred_element_type=F32)

    kk = kk / jnp.maximum(jnp.sqrt(head_sum(kk * kk)), 1e-12)
    kk_s[...] = kk
    b_s[...] = kk * a
    bon_s[...] = head_sum(r * k * rk_ref[...]) * v

    rows_c = lax.broadcasted_iota(jnp.int32, (c, c), 0)
    cols_c = lax.broadcasted_iota(jnp.int32, (c, c), 1)
    strict = rows_c > cols_c
    incl = rows_c >= cols_c
    tril_ones = incl.astype(BF16)
    rows_2c = lax.broadcasted_iota(jnp.int32, (c, 2 * c), 0)
    cols_2c = lax.broadcasted_iota(jnp.int32, (c, 2 * c), 1)
    incl2 = rows_2c >= jnp.where(cols_2c >= c, cols_2c - c, cols_2c)
    rows_k = lax.broadcasted_iota(jnp.int32, (HEAD_DIM, HEAD_DIM), 0)
    cols_k = lax.broadcasted_iota(jnp.int32, (HEAD_DIM, HEAD_DIM), 1)
    eye_k = (rows_k == cols_k).astype(F32)
    zeros_cv = jnp.zeros((c, HEAD_DIM), F32)
    heads = [slice(h * HEAD_DIM, (h + 1) * HEAD_DIM) for h in range(RWKV_HEADS)]

    n_chunks = tt // c
    group = nseq if nseq > 1 else math.gcd(n_chunks, RWKV_GROUP)

    def chunk_rows(ci):
        return pl.ds(ci * c, c) if isinstance(ci, int) else pl.ds(pl.multiple_of(ci * c, c), c)

    def scaled(ci):
        rows = chunk_rows(ci)
        ld = ld_s[rows, :]
        ld_hi, ld_lo = _split(ld)
        cum = (jnp.dot(tril_ones, ld_hi, preferred_element_type=F32)
               + jnp.dot(tril_ones, ld_lo, preferred_element_type=F32))
        e_inc = jnp.exp(cum)
        e_neg = jnp.exp(-cum)
        kt = k_s[rows, :] * e_neg
        bt = b_s[rows, :] * e_neg
        g_end = e_inc[c - 1:c, :]
        return dict(ci=ci, rt=r_s[rows, :] * e_inc, kkt=kk_s[rows, :] * jnp.exp(cum - ld), kt=kt, bt=bt,
                    g_end=g_end, bc=bt * g_end, kc=kt * g_end, vv=v_s[rows, :])

    def phase1(cj, carry):
        chains = [(ch, h, s) for ch in [scaled(cj * group + g) for g in range(group)]
                  for h, s in enumerate(heads)]
        x = [jnp.concatenate([ch["kkt"][:, s], ch["rt"][:, s]], axis=0) for ch, _, s in chains]
        zz = [jnp.concatenate([ch["bt"][:, s], ch["kt"][:, s]], axis=0) for ch, _, s in chains]
        amat = [_dotp(xh, zh, NT, PREC["a"]) for xh, zh in zip(x, zz)]
        l_b = [jnp.where(strict, am[0:c, 0:c], 0.0) for am in amat]
        l_k = [jnp.where(strict, am[0:c, c:2 * c], 0.0) for am in amat]
        a_r = [jnp.where(incl2, am[c:2 * c, :], 0.0) for am in amat]
        tinv = _unit_lower_inverse(l_b, c)
        lkv = [_dotp(lk, ch["vv"][:, s], NN, PREC["lkv"]) for lk, (ch, _, s) in zip(l_k, chains)]
        wu = [-_dotp(t, jnp.concatenate([ch["kkt"][:, s], lv], axis=1), NN, PREC["tw"])
              for t, lv, (ch, _, s) in zip(tinv, lkv, chains)]
        m = [jnp.concatenate([w, jnp.concatenate([zeros_cv, ch["vv"][:, s]], axis=1)], axis=0)
             for w, (ch, _, s) in zip(wu, chains)]
        am2 = [_dotp(ar, mh, NN, PREC["am"]) for ar, mh in zip(a_r, m)]
        gh = [_dotp(mh, jnp.concatenate([ch["bc"][:, s], ch["kc"][:, s]], axis=0), TN, PREC["gh"])
              for mh, (ch, _, s) in zip(m, chains)]
        for q, (ch, h, s) in enumerate(chains):
            idx = ch["ci"] * RWKV_HEADS + h
            rw_s[idx] = ch["rt"][:, s] + am2[q][:, 0:HEAD_DIM]
            y0_s[idx] = am2[q][:, HEAD_DIM:2 * HEAD_DIM]
            gm_s[idx] = gh[q][0:HEAD_DIM, :] + eye_k * ch["g_end"][:, s]
            h0_s[idx] = gh[q][HEAD_DIM:2 * HEAD_DIM, :]
        return carry

    if n_chunks == group:
        phase1(0, 0)
    else:
        lax.fori_loop(0, n_chunks // group, phase1, 0)

    if nseq == 1:
        def phase2(ci, carry):
            rows = chunk_rows(ci)
            for h, s in enumerate(heads):
                idx = ci * nh + h
                s_h = state_ref[h]
                y_s[rows, s] = _dotp(rw_s[idx], s_h, NT, PREC["y"]) + y0_s[idx]
                state_ref[h] = _dotp(s_h, gm_s[idx], NN, PREC["s"]) + h0_s[idx]
            return carry

        lax.fori_loop(0, n_chunks, phase2, 0)
    else:
        pairs = [(q, h, s) for q in range(nseq) for h, s in enumerate(heads)]
        states = [state_ref[q * nh + h] for q, h, _ in pairs]
        ys = [_dotp(rw_s[q * nh + h], st, NT, PREC["y"]) + y0_s[q * nh + h] for (q, h, _), st in zip(pairs, states)]
        new = [_dotp(st, gm_s[q * nh + h], NN, PREC["s"]) + h0_s[q * nh + h] for (q, h, _), st in zip(pairs, states)]
        for (q, h, s), yq, nq in zip(pairs, ys, new):
            y_s[chunk_rows(q), s] = yq
            state_ref[q * nh + h] = nq
    y = y_s[...]
    yc = y - head_sum(y) * (1.0 / HEAD_DIM)
    var = head_sum(yc * yc) * (1.0 / HEAD_DIM)
    yn = yc * lax.rsqrt(var + GN_EPS)
    o_ref[0] = ((yn * lng_ref[...] + lnb_ref[...] + bon_s[...]) * g_s[...]).astype(o_ref.dtype)

    @pl.when(i == pl.num_programs(1) - 1)
    def _():
        for q in range(nseq):
            sout_ref[q] = state_ref[q * nh:(q + 1) * nh]
            last = tt - 1 if nseq == 1 else (q + 1) * c - 1
            shout_ref[q] = zr_ref[0, last:last + 1, :]


def _rwkv(zr3, blk_off, b, nt, shift_prev, s_prev, p, tt, c, nseq):
    vec = lambda n: pl.BlockSpec((1, n), lambda bi, ti: (0, 0))
    mat = lambda m, n: pl.BlockSpec((m, n), lambda bi, ti: (0, 0))
    rw = RWKV_WIDTH
    scr = lambda: pltpu.VMEM((tt, rw), F32)
    nch = (tt // c) * RWKV_HEADS
    return pl.pallas_call(
        functools.partial(_rwkv_body, tt=tt, c=c, nseq=nseq),
        grid=(b // nseq, nt),
        in_specs=[
            pl.BlockSpec((1, tt, SHIFT_WIDTH), lambda bi, ti: (blk_off + bi * nt + ti, 0, 0)),
            pl.BlockSpec((nseq, 1, SHIFT_WIDTH), lambda bi, ti: (bi, 0, 0)),
            pl.BlockSpec((nseq, RWKV_HEADS, HEAD_DIM, HEAD_DIM), lambda bi, ti: (bi, 0, 0, 0)),
            vec(SHIFT_WIDTH), vec(rw), mat(DECAY_LORA, rw), vec(rw), mat(A_LORA, rw), mat(GATE_LORA, rw),
            vec(rw), vec(rw), vec(rw), vec(rw), vec(rw), mat(rw, rw),
        ],
        out_specs=[
            pl.BlockSpec((1, tt, rw), lambda bi, ti: (bi, ti, 0)),
            pl.BlockSpec((nseq, RWKV_HEADS, HEAD_DIM, HEAD_DIM), lambda bi, ti: (bi, 0, 0, 0)),
            pl.BlockSpec((nseq, 1, SHIFT_WIDTH), lambda bi, ti: (bi, 0, 0)),
        ],
        out_shape=[
            jax.ShapeDtypeStruct((b // nseq, nt * tt, rw), BF16),
            jax.ShapeDtypeStruct((b, RWKV_HEADS, HEAD_DIM, HEAD_DIM), F32),
            jax.ShapeDtypeStruct((b, 1, SHIFT_WIDTH), F32),
        ],
        scratch_shapes=[
            pltpu.VMEM((1, SHIFT_WIDTH), F32),
            pltpu.VMEM((nseq * RWKV_HEADS, HEAD_DIM, HEAD_DIM), F32),
            scr(), scr(), scr(), scr(), scr(), scr(), scr(), scr(), scr(),
            pltpu.VMEM((nch, c, HEAD_DIM), F32),
            pltpu.VMEM((nch, c, HEAD_DIM), F32),
            pltpu.VMEM((nch, HEAD_DIM, HEAD_DIM), F32),
            pltpu.VMEM((nch, HEAD_DIM, HEAD_DIM), F32),
        ],
        compiler_params=_params(("parallel", "arbitrary")),
        name="rwkv",
    )(zr3, shift_prev, s_prev, p["mu"], p["w0"], p["wd"], p["a0"], p["wa"], p["wg"], p["kk"], p["ka"],
      p["rk"], p["lng"], p["lnb"], p["hsum"])


def _gmlp_body(z_ref, lng_ref, lnb_ref, ws_ref, bs_ref, o_ref, v_ref):
    z = z_ref[...]
    ge = 0.5 * z * (1.0 + lax.erf(z * (1.0 / math.sqrt(2.0))))
    u = ge[:, 0:GMLP_WIDTH]
    v = ge[:, GMLP_WIDTH:2 * GMLP_WIDTH]
    mean = jnp.mean(v, axis=-1, keepdims=True)
    vc = v - mean
    var = jnp.mean(vc * vc, axis=-1, keepdims=True)
    vn = vc * lax.rsqrt(var + LN_EPS) * lng_ref[...] + lnb_ref[...]
    v_ref[...] = vn
    gd = GMLP_WIDTH // GMLP_GROUPS
    for q in range(GMLP_TILE):
        rows = slice(q * CHUNK, (q + 1) * CHUNK)
        for g in range(GMLP_GROUPS):
            gs = slice(g * gd, (g + 1) * gd)
            sv = jnp.dot(ws_ref[g], vn[rows, gs].astype(BF16), preferred_element_type=F32) + bs_ref[:, g:g + 1]
            o_ref[rows, gs] = (u[rows, gs] * sv).astype(o_ref.dtype)


def _gmlp(zg, tile_off, n_rows, lng, lnb, ws_bf16, bs):
    tile = GMLP_TILE * CHUNK
    return pl.pallas_call(
        _gmlp_body,
        grid=(n_rows // tile,),
        in_specs=[
            pl.BlockSpec((tile, 2 * GMLP_WIDTH), lambda i: (tile_off + i, 0)),
            pl.BlockSpec((1, GMLP_WIDTH), lambda i: (0, 0)),
            pl.BlockSpec((1, GMLP_WIDTH), lambda i: (0, 0)),
            pl.BlockSpec((GMLP_GROUPS, CHUNK, CHUNK), lambda i: (0, 0, 0)),
            pl.BlockSpec((CHUNK, GMLP_GROUPS), lambda i: (0, 0)),
        ],
        out_specs=[
            pl.BlockSpec((tile, GMLP_WIDTH), lambda i: (i, 0)),
            pl.BlockSpec((tile, GMLP_WIDTH), lambda i: (i, 0)),
        ],
        out_shape=[
            jax.ShapeDtypeStruct((n_rows, GMLP_WIDTH), BF16),
            jax.ShapeDtypeStruct((n_rows, GMLP_WIDTH), F32),
        ],
        compiler_params=_params(("parallel",)),
        name="gmlp",
    )(zg, lng, lnb, ws_bf16, bs)


def _mem_attn_body(q_ref, k_ref, v_ref, o_ref, *, bb, feature_major):
    hd = MEM_WIDTH // MEM_HEADS
    pairs = [(s, slice(h * hd, (h + 1) * hd)) for s in range(bb) for h in range(MEM_HEADS)]
    if feature_major:
        scores = [_bdot(q_ref[s, :, hs], k_ref[s, hs, :], NN) * (hd ** -0.5) for s, hs in pairs]
    else:
        scores = [_bdot(q_ref[s, :, hs], k_ref[s, :, hs], NT) * (hd ** -0.5) for s, hs in pairs]
    probs = []
    for sc in scores:
        pr = jnp.exp(sc - jnp.max(sc, axis=-1, keepdims=True))
        probs.append(pr / jnp.sum(pr, axis=-1, keepdims=True))
    for (s, hs), pr in zip(pairs, probs):
        if feature_major:
            o_ref[s, :, hs] = _bdot(pr, v_ref[s, hs, :], NT).astype(o_ref.dtype)
        else:
            o_ref[s, :, hs] = _bdot(pr, v_ref[s, :, hs]).astype(o_ref.dtype)


def _mem_attn(q3, blk_off, b, nt, mk, mv, tt, bb, feature_major):
    return pl.pallas_call(
        functools.partial(_mem_attn_body, bb=bb, feature_major=feature_major),
        grid=(b // bb, nt),
        in_specs=[
            pl.BlockSpec((bb, tt, MEM_WIDTH), lambda bi, ti: (blk_off // bb + bi * nt + ti, 0, 0)),
            pl.BlockSpec((bb, N_MEM, MEM_WIDTH), lambda bi, ti: (bi, 0, 0)),
            pl.BlockSpec((bb, N_MEM, MEM_WIDTH), lambda bi, ti: (bi, 0, 0)),
        ],
        out_specs=pl.BlockSpec((bb, tt, MEM_WIDTH), lambda bi, ti: (bi, ti, 0)),
        out_shape=jax.ShapeDtypeStruct((b, nt * tt, MEM_WIDTH), BF16),
        compiler_params=_params(("parallel", "parallel")),
        name="mem_attn",
    )(q3, mk, mv)


def _merge_body(x_p, x_s, orw_p, orw_s, ogm_p, ogm_s, ome_p, ome_s, sg_ref, wbr_ref, wbg_ref, wbm_ref, wout_ref,
                gffn_ref, wrt_ref, brt_ref, x2_ref, h2_ref, route_ref, route_t_ref, cnt_ref, count_s, *, np_tiles):
    i = pl.program_id(0)
    d = D_MODEL
    tm = x_p.shape[0]
    is_p = i < np_tiles
    x = jnp.where(is_p, x_p[...], x_s[...])
    orw = jnp.where(is_p, orw_p[...], orw_s[...])
    ogm = jnp.where(is_p, ogm_p[...], ogm_s[...])
    ome = jnp.where(is_p, ome_p[...], ome_s[...])
    merged = sg_ref[:, 0:d].astype(F32) * _bdot(orw, wbr_ref[...])
    merged = merged + sg_ref[:, d:2 * d].astype(F32) * _bdot(ogm, wbg_ref[...])
    merged = merged + sg_ref[:, 2 * d:3 * d].astype(F32) * _bdot(ome, wbm_ref[...])
    x2 = x + _bdot(merged, wout_ref[...])
    x2_ref[...] = x2
    h2 = _rms(x2, gffn_ref[...])
    _store_row_tiles(h2_ref, h2)

    @pl.when(i == 0)
    def _():
        count_s[...] = jnp.zeros_like(count_s)

    lane = lax.broadcasted_iota(jnp.int32, (tm, LANES), 1)
    logits = jnp.where(lane < N_EXPERTS, _dot3(h2, wrt_ref[...]) + brt_ref[...], -jnp.inf)
    lane_f = lane.astype(F32)
    tops, hots, idxs = [], [], []
    for _ in range(TOP_K):
        top = jnp.max(logits, axis=-1, keepdims=True)
        idx = jnp.min(jnp.where(logits == top, lane_f, float(LANES)), axis=-1, keepdims=True)
        hot = lane_f == idx
        logits = jnp.where(hot, -jnp.inf, logits)
        tops.append(top)
        hots.append(hot)
        idxs.append(idx)
    weights = [jnp.exp(t - tops[0]) for t in tops]
    denom = weights[0] + weights[1] + weights[2] + weights[3]
    onehot = jnp.zeros((tm, LANES), F32)
    for hot in hots:
        onehot = onehot + jnp.where(hot, 1.0, 0.0)
    rows_t = lax.broadcasted_iota(jnp.int32, (tm, tm), 0)
    cols_t = lax.broadcasted_iota(jnp.int32, (tm, tm), 1)
    ahead = jnp.where(rows_t > cols_t, 1.0, 0.0).astype(BF16)
    prefix = jnp.dot(ahead, onehot.astype(BF16), preferred_element_type=F32) + count_s[...]
    route = jnp.zeros((tm, LANES), F32)
    for kk in range(TOP_K):
        rank = jnp.sum(jnp.where(hots[kk], prefix, 0.0), axis=-1, keepdims=True)
        route = jnp.where(lane == ROUTE_E + kk, idxs[kk], route)
        route = jnp.where(lane == ROUTE_RANK + kk, rank, route)
        route = jnp.where(lane == ROUTE_GATE + kk, weights[kk] / denom, route)
    route_ref[...] = route
    route_t_ref[...] = route.T
    count_s[...] = count_s[...] + jnp.sum(onehot, axis=0, keepdims=True)
    cnt_ref[...] = jnp.broadcast_to(count_s[...], cnt_ref.shape)


def _merge(x, o_rw, o_gm, o_me, sg, wbr, wbg, wbm, wout, gffn, wrt_pad, brt_pad):
    n_p = x[0].shape[0]
    n = n_p + x[1].shape[0]
    tm = MERGE_TM
    d = D_MODEL
    np_tiles = n_p // tm
    row = lambda i: (i, 0)
    const = lambda i: (0, 0)
    first =lambda i: (jnp.minimum(i, np_tiles - 1), 0)
    second = lambda i: (jnp.maximum(i - np_tiles, 0), 0)

    def pair(width):
        return [pl.BlockSpec((tm, width), first), pl.BlockSpec((tm, width), second)]

    return pl.pallas_call(
        functools.partial(_merge_body, np_tiles=np_tiles),
        grid=(n // tm,),
        in_specs=pair(d) + pair(RWKV_WIDTH) + pair(GMLP_WIDTH) + pair(MEM_WIDTH) + [
            pl.BlockSpec((tm, 3 * d), row),
            pl.BlockSpec((RWKV_WIDTH, d), const),
            pl.BlockSpec((GMLP_WIDTH, d), const),
            pl.BlockSpec((MEM_WIDTH, d), const),
            pl.BlockSpec((d, d), const),
            pl.BlockSpec((1, d), const),
            pl.BlockSpec((d, LANES), const),
            pl.BlockSpec((1, LANES), const),
        ],
        out_specs=[
            pl.BlockSpec((tm, d), row),
            pl.BlockSpec((tm * ROW_SUB, LANES), row),
            pl.BlockSpec((tm, LANES), row),
            pl.BlockSpec((LANES, tm), lambda i: (0, i)),
            pl.BlockSpec((8, LANES), const),
        ],
        out_shape=[
            jax.ShapeDtypeStruct((n, d), F32),
            jax.ShapeDtypeStruct((n * ROW_SUB, LANES), F32),
            jax.ShapeDtypeStruct((n, LANES), F32),
            jax.ShapeDtypeStruct((LANES, n), F32),
            jax.ShapeDtypeStruct((8, LANES), F32),
        ],
        scratch_shapes=[pltpu.VMEM((1, LANES), F32)],
        compiler_params=_params(("arbitrary",)),
        name="merge",
    )(x[0], x[1], o_rw[0], o_rw[1], o_gm[0], o_gm[1], o_me[0], o_me[1], sg, wbr, wbg, wbm, wout, gffn, wrt_pad,
      brt_pad)


N_ZERO_BLOCKS = 2 * N_EXPERTS


def _dispatch_body(dest_ref, zlist_ref, h_ref, xs_hbm, zero_s, sem_z, sem):
    i = pl.program_id(0)
    tm = DISPATCH_TM
    bm = MOE_BM

    @pl.when(i == 0)
    def _():
        zero_s[...] = jnp.zeros_like(zero_s)

        def zero_copy(q):
            start = pl.multiple_of(zlist_ref[q] * bm, bm)
            return pltpu.make_async_copy(zero_s, xs_hbm.at[pl.ds(start, bm)], sem_z)

        def start(q, carry):
            @pl.when(zlist_ref[q] >= 0)
            def _():
                zero_copy(q).start()
            return carry

        def wait(q, carry):
            @pl.when(zlist_ref[q] >= 0)
            def _():
                zero_copy(q).wait()
            return carry

        lax.fori_loop(0, N_ZERO_BLOCKS, start, 0)
        lax.fori_loop(0, N_ZERO_BLOCKS, wait, 0)

    base = i * tm
    n_tok = pl.num_programs(0) * tm

    def body(r, carry):
        for kk in range(TOP_K):
            slot = dest_ref[kk * n_tok + base + r]
            pltpu.make_async_copy(h_ref.at[r], xs_hbm.at[slot], sem).start(priority=kk % 2)
        return carry

    lax.fori_loop(0, tm, body, 0, unroll=2)
    for kk in range(TOP_K):
        pltpu.make_async_copy(h_ref, xs_hbm.at[pl.ds(0, tm)], sem).wait()


def _dispatch(dest, zlist, h, n_blocks):
    n = h.shape[0]
    tm = DISPATCH_TM
    grid_spec = pltpu.PrefetchScalarGridSpec(
        num_scalar_prefetch=2,
        grid=(n // tm,),
        in_specs=[pl.BlockSpec((tm, ROW_SUB, LANES), lambda i, dest, zl: (i, 0, 0))],
        out_specs=pl.BlockSpec(memory_space=pl.ANY),
        scratch_shapes=[
            pltpu.VMEM((MOE_BM, ROW_SUB, LANES), F32),
            pltpu.SemaphoreType.DMA(()),
            pltpu.SemaphoreType.DMA(()),
        ],
    )
    return pl.pallas_call(
        _dispatch_body,
        grid_spec=grid_spec,
        out_shape=jax.ShapeDtypeStruct((n_blocks * MOE_BM, ROW_SUB, LANES), F32),
        compiler_params=_params(("arbitrary",)),
        name="moe_dispatch",
    )(dest, zlist, h)


CAST_ROWS = 64


def _moe_body(be_ref, nused_ref, nxt_ref, par_ref, x_ref, w1_hbm, b1_ref, w2_hbm, b2_ref, o_ref,
              w1f, w2f, w1b, w2b, sem):
    j = pl.program_id(0)
    n_used = nused_ref[0]

    def weight_copies(expert, slot):
        return (pltpu.make_async_copy(w1_hbm.at[expert], w1f.at[slot], sem.at[slot, 0]),
                pltpu.make_async_copy(w2_hbm.at[expert], w2f.at[slot], sem.at[slot, 1]))

    @pl.when(j == 0)
    def _():
        for cp in weight_copies(be_ref[0], 0):
            cp.start()

    @pl.when(j < n_used)
    def _():
        slot = par_ref[j]
        changed = jnp.logical_or(j == 0, be_ref[j] != be_ref[jnp.maximum(j - 1, 0)])

        @pl.when(changed)
        def _():
            for cp in weight_copies(be_ref[j], slot):
                cp.wait()

            @pl.when(nxt_ref[j] >= 0)
            def _():
                for cp in weight_copies(nxt_ref[j], 1 - slot):
                    cp.start()

            def cast1(q, carry):
                rows = pl.ds(pl.multiple_of(q * CAST_ROWS, CAST_ROWS), CAST_ROWS)
                w1b[rows, :] = w1f[slot, rows, :].astype(BF16)
                return carry

            def cast2(q, carry):
                rows = pl.ds(pl.multiple_of(q * CAST_ROWS, CAST_ROWS), CAST_ROWS)
                w2b[rows, :] = w2f[slot, rows, :].astype(BF16)
                return carry

            lax.fori_loop(0, D_MODEL // CAST_ROWS, cast1, 0)
            lax.fori_loop(0, D_FF // CAST_ROWS, cast2, 0)

        x = _load_row_tiles(x_ref, MOE_BM).astype(BF16)
        z = jnp.dot(x, w1b[...], preferred_element_type=F32) + b1_ref[0]
        zg = jnp.minimum(z[:, 0:D_FF], SWIGLU_LIMIT)
        zl = jnp.clip(z[:, D_FF:2 * D_FF], -SWIGLU_LIMIT, SWIGLU_LIMIT)
        act = zg * _sigmoid(SWIGLU_ALPHA * zg) * (zl + 1.0)
        _store_row_tiles(o_ref, jnp.dot(act.astype(BF16), w2b[...], preferred_element_type=F32) + b2_ref[0])

    @pl.when(j >= n_used)
    def _():
        o_ref[...] = jnp.zeros_like(o_ref)


def _moe(blk_e, n_used, nxt_e, parity, xs, w1, b1, w2, b2, n_blocks):
    bm = MOE_BM
    d = D_MODEL
    grid_spec = pltpu.PrefetchScalarGridSpec(
        num_scalar_prefetch=4,
        grid=(n_blocks,),
        in_specs=[
            pl.BlockSpec((bm * ROW_SUB, LANES), lambda j, be, nu, nx, pa: (j, 0)),
            pl.BlockSpec(memory_space=pl.ANY),
            pl.BlockSpec((1, 1, 2 * D_FF), lambda j, be, nu, nx, pa: (be[j], 0, 0)),
            pl.BlockSpec(memory_space=pl.ANY),
            pl.BlockSpec((1, 1, d), lambda j, be, nu, nx, pa: (be[j], 0, 0)),
        ],
        out_specs=pl.BlockSpec((bm * ROW_SUB, LANES), lambda j, be, nu, nx, pa: (j, 0)),
        scratch_shapes=[
            pltpu.VMEM((2, d, 2 * D_FF), F32),
            pltpu.VMEM((2, D_FF, d), F32),
            pltpu.VMEM((d, 2 * D_FF), BF16),
            pltpu.VMEM((D_FF, d), BF16),
            pltpu.SemaphoreType.DMA((2, 2)),
        ],
    )
    return pl.pallas_call(
        _moe_body,
        grid_spec=grid_spec,
        out_shape=jax.ShapeDtypeStruct((n_blocks * bm * ROW_SUB, LANES), F32),
        compiler_params=_params(("arbitrary",)),
        name="moe_ffn",
    )(blk_e, n_used, nxt_e, parity, xs, w1, b1, w2, b2)


def _combine_body(dest_ref, x2_ref, route_ref, gfin_ref, yb_hbm, op_ref, os_ref, buf0, buf1, sem, *, np_steps):
    i = pl.program_id(0)
    n_steps = pl.num_programs(0)
    tm = COMBINE_TM
    n_tok = n_steps * (2 * tm)
    bufs = (buf0, buf1)

    def issue(tile, which):
        base = tile * tm
        for r in range(tm):
            for kk in range(TOP_K):
                slot = dest_ref[kk * n_tok + base + r]
                src = pl.ds(pl.multiple_of(slot * ROW_SUB, ROW_SUB), ROW_SUB)
                pltpu.make_async_copy(yb_hbm.at[src], bufs[which].at[kk, r * ROW_SUB:(r + 1) * ROW_SUB],
                                      sem.at[which]).start(priority=kk % 2)

    def wait(which):
        for kk in range(TOP_K):
            pltpu.make_async_copy(yb_hbm.at[pl.ds(0, tm * ROW_SUB)], bufs[which].at[kk], sem.at[which]).wait()

    def reduce(which, half):
        rows = slice(half * tm, (half + 1) * tm)
        acc = x2_ref[rows, :]
        for kk in range(TOP_K):
            gate = route_ref[rows, ROUTE_GATE + kk:ROUTE_GATE + kk + 1]
            acc = acc + gate * _load_row_tiles(bufs[which].at[kk], tm)
        return _rms(acc, gfin_ref[...])

    @pl.when(i == 0)
    def _():
        issue(0, 0)

    wait(0)
    issue(2 * i + 1, 1)
    y0 = reduce(0, 0)
    wait(1)
    issue(jnp.minimum(2 * i + 2, 2 * n_steps - 1), 0)
    y1 = reduce(1, 1)

    @pl.when(i < np_steps)
    def _():
        op_ref[0:tm, :] = y0
        op_ref[tm:2 * tm, :] = y1

    @pl.when(i >= np_steps)
    def _():
        os_ref[0:tm, :] = y0
        os_ref[tm:2 * tm, :] = y1

    @pl.when(i == n_steps - 1)
    def _():
        wait(0)


def _combine(dest, x2, route, gfin, yb, n_p):
    n, d = x2.shape
    tm = COMBINE_TM
    step = 2 * tm
    np_steps = n_p // step
    grid_spec = pltpu.PrefetchScalarGridSpec(
        num_scalar_prefetch=1,
        grid=(n // step,),
        in_specs=[
            pl.BlockSpec((step, d), lambda i, dest: (i, 0)),
            pl.BlockSpec((step, LANES), lambda i, dest: (i, 0)),
            pl.BlockSpec((1, d), lambda i, dest: (0, 0)),
            pl.BlockSpec(memory_space=pl.ANY),
        ],
        out_specs=[
            pl.BlockSpec((step, d), lambda i, dest: (jnp.minimum(i, np_steps - 1), 0)),
            pl.BlockSpec((step, d), lambda i, dest: (jnp.maximum(i - np_steps, 0), 0)),
        ],
        scratch_shapes=[
            pltpu.VMEM((TOP_K, tm * ROW_SUB, LANES), F32),
            pltpu.VMEM((TOP_K, tm * ROW_SUB, LANES), F32),
            pltpu.SemaphoreType.DMA((2,)),
        ],
    )
    return pl.pallas_call(
        functools.partial(_combine_body, np_steps=np_steps),
        grid_spec=grid_spec,
        out_shape=[jax.ShapeDtypeStruct((n_p, d), F32), jax.ShapeDtypeStruct((n - n_p, d), F32)],
        compiler_params=_params(("arbitrary",)),
        name="moe_combine",
    )(dest, x2, route, gfin, yb)


def _slot_tables(route_t, counts_row, n_blocks):
    bm = MOE_BM
    e = route_t[ROUTE_E:ROUTE_E + TOP_K].astype(jnp.int32)
    rank = route_t[ROUTE_RANK:ROUTE_RANK + TOP_K].astype(jnp.int32)
    counts = counts_row[:N_EXPERTS].astype(jnp.int32)
    padded = (counts + bm - 1) // bm * bm
    pad_end = jnp.cumsum(padded)
    pad_start = pad_end - padded
    experts = jnp.arange(N_EXPERTS, dtype=jnp.int32)[:, None, None]
    dest = (jnp.sum(jnp.where(e[None] == experts, pad_start[:, None, None], 0), axis=0) + rank).reshape(-1)
    blk_start = jnp.arange(n_blocks, dtype=jnp.int32) * bm
    blk_e = jnp.minimum(jnp.sum(pad_end[None, :] <= blk_start[:, None], axis=1), N_EXPERTS - 1).astype(jnp.int32)
    n_used = pad_end[-1] // bm
    last_blk = jnp.where(padded > 0, pad_end // bm - 1, -1)
    trailing = n_used + jnp.arange(N_EXPERTS, dtype=jnp.int32)
    trailing = jnp.where(trailing < n_blocks, trailing, -1)
    zlist = jnp.concatenate([last_blk, trailing]).astype(jnp.int32)
    ids = jnp.arange(N_EXPERTS, dtype=jnp.int32)
    present = counts > 0
    later = jnp.where(jnp.logical_and(ids[None, :] > ids[:, None], present[None, :]), ids[None, :], N_EXPERTS)
    next_present = jnp.min(later, axis=1)
    next_present = jnp.where(next_present < N_EXPERTS, next_present, -1)
    nxt_e = jnp.take(next_present, blk_e).astype(jnp.int32)
    parity = (jnp.take(jnp.cumsum(present.astype(jnp.int32)) - 1, blk_e) % 2).astype(jnp.int32)
    return dest.astype(jnp.int32), blk_e, n_used.astype(jnp.int32).reshape(1), nxt_e, parity, zlist


def kernel(x_prompt, x_sample, mem_prompt, state_shift, state_wkv, cache_mem_k, cache_mem_v, g_norm_mix, w_in, mu_shift, w0, w_decay_up, a0, w_a_up, w_g_up, k_k, k_a, r_k, ln_x_g, ln_x_b, gmlp_ln_g, gmlp_ln_b, w_spatial, b_spatial, g_norm_mem, w_mem_kv, w_br_rwkv, w_br_gmlp, w_br_mem, w_out, g_norm_ffn, w_router, b_router, w_exp1, b_exp1, w_exp2, b_exp2, g_norm_final):
    bp, tp, d = x_prompt.shape
    bs, ts, _ = x_sample.shape
    n_p, n_s = bp * tp, bs * ts
    n_all = n_p + n_s
    l = 0
    row = lambda a: a.reshape(1, -1)

    x_pair = (x_prompt.reshape(n_p, d), x_sample.reshape(n_s, d))
    zr, zg, zq, sg = _in_proj(x_pair[0], x_pair[1], row(g_norm_mix[l]), w_in[l].astype(BF16))

    mk_p, mv_p = _mem_kv(mem_prompt.reshape(bp * N_MEM, d), row(g_norm_mem[l]), w_mem_kv[l].astype(BF16))

    rp = dict(mu=row(mu_shift[l]), w0=row(w0[l]), wd=w_decay_up[l].astype(BF16), a0=row(a0[l]),
              wa=w_a_up[l].astype(BF16), wg=w_g_up[l].astype(BF16), kk=row(k_k[l]), ka=row(k_a[l]),
              rk=row(r_k[l]), lng=row(ln_x_g[l]), lnb=row(ln_x_b[l]))
    head_of = jnp.arange(RWKV_WIDTH, dtype=jnp.int32) // HEAD_DIM
    rp["hsum"] = (head_of[:, None] == head_of[None, :]).astype(BF16)
    o_rw_p, s_p, shift_p = _rwkv(zr.reshape(n_all // RWKV_TT, RWKV_TT, SHIFT_WIDTH), 0, bp, tp // RWKV_TT,
                        jnp.zeros((bp, 1, SHIFT_WIDTH), F32), jnp.zeros((bp, RWKV_HEADS, HEAD_DIM, HEAD_DIM), F32),
                        rp, tt=RWKV_TT, c=RWKV_C, nseq=1)
    tile_s = RWKV_SAMPLE_SEQS * ts
    o_rw_s, s_s, shift_s = _rwkv(zr.reshape(n_all // tile_s, tile_s, SHIFT_WIDTH), n_p // tile_s, bs, 1,
                        state_shift[l].reshape(bs, 1, SHIFT_WIDTH), state_wkv[l], rp, tt=tile_s, c=ts,
                        nseq=RWKV_SAMPLE_SEQS)

    tri = jnp.tril(jnp.ones((CHUNK, CHUNK), bool))
    ws_p = jnp.where(tri, w_spatial[l], 0.0).astype(BF16)
    bs_p = b_spatial[l].T
    reps = CHUNK // ts
    tri_s = jnp.tril(jnp.ones((ts, ts), bool))
    ws_small = jnp.where(tri_s, w_spatial[l][:, :ts, :ts], 0.0)
    eye = jnp.eye(reps, dtype=F32)
    ws_s = jnp.einsum("ab,gij->gaibj", eye, ws_small).reshape(GMLP_GROUPS, CHUNK, CHUNK).astype(BF16)
    bs_s = jnp.tile(b_spatial[l][:, :ts], (1, reps)).T
    lng, lnb = row(gmlp_ln_g[l]), row(gmlp_ln_b[l])
    o_gm_p, _ = _gmlp(zg, 0, n_p, lng, lnb, ws_p, bs_p)
    o_gm_s, v_rows_s = _gmlp(zg, n_p // (GMLP_TILE * CHUNK), n_s, lng, lnb, ws_s, bs_s)

    o_me_p = _mem_attn(zq.reshape(n_all // ATTN_TT, ATTN_TT, MEM_WIDTH), 0, bp, tp // ATTN_TT,
                       mk_p.reshape(bp, N_MEM, MEM_WIDTH), mv_p.reshape(bp, N_MEM, MEM_WIDTH), tt=ATTN_TT, bb=1,
                       feature_major=False)
    mk_s = jnp.transpose(cache_mem_k[l].reshape(bs, N_MEM, MEM_WIDTH), (0, 2, 1))
    mv_s = jnp.transpose(cache_mem_v[l].reshape(bs, N_MEM, MEM_WIDTH), (0, 2, 1))
    o_me_s = _mem_attn(zq.reshape(n_all // ts, ts, MEM_WIDTH), n_p // ts, bs, 1, mk_s, mv_s, tt=ts, bb=ATTN_BB,
                       feature_major=True)

    wrt_pad = jnp.zeros((d, LANES), F32).at[:, :N_EXPERTS].set(w_router[l])
    brt_pad = jnp.zeros((1, LANES), F32).at[0, :N_EXPERTS].set(b_router[l])
    x2, h2, route, route_t, counts = _merge(
        x_pair, (o_rw_p.reshape(n_p, RWKV_WIDTH), o_rw_s.reshape(n_s, RWKV_WIDTH)), (o_gm_p, o_gm_s),
        (o_me_p.reshape(n_p, MEM_WIDTH), o_me_s.reshape(n_s, MEM_WIDTH)), sg,
        w_br_rwkv[l].astype(BF16), w_br_gmlp[l].astype(BF16), w_br_mem[l].astype(BF16), w_out[l].astype(BF16),
        row(g_norm_ffn[l]), wrt_pad, brt_pad)

    n_assign = n_all * TOP_K
    n_blocks = -(-(n_assign + N_EXPERTS * (MOE_BM - 1)) // MOE_BM)
    dest, blk_e, n_used, nxt_e, parity, zlist = _slot_tables(route_t, counts[0], n_blocks)
    xs = _dispatch(dest, zlist, h2.reshape(n_all, ROW_SUB, LANES), n_blocks)
    yb = _moe(blk_e, n_used, nxt_e, parity, xs.reshape(n_blocks * MOE_BM * ROW_SUB, LANES), w_exp1[l],
              b_exp1[l].reshape(N_EXPERTS, 1, 2 * D_FF), w_exp2[l], b_exp2[l].reshape(N_EXPERTS, 1, d), n_blocks)
    y_p, y_s = _combine(dest, x2, route, row(g_norm_final), yb, n_p)

    mk_out = mk_p.reshape(1, bp, N_MEM, MEM_HEADS, MEM_WIDTH // MEM_HEADS)
    mv_out = mv_p.reshape(1, bp, N_MEM, MEM_HEADS, MEM_WIDTH // MEM_HEADS)
    return (y_p.reshape(bp, tp, d), y_s.reshape(bs, ts, d), shift_p.reshape(1, bp, SHIFT_WIDTH), s_p[None], mk_out,
            mv_out, shift_s.reshape(1, bs, SHIFT_WIDTH), s_s[None],
            v_rows_s.reshape(1, bs, ts, GMLP_WIDTH))
```

```python
import functools
import math

import jax
import jax.numpy as jnp
from jax import lax
from jax.experimental import pallas as pl
from jax.experimental.pallas import tpu as pltpu

F32 = jnp.float32
BF16 = jnp.bfloat16

D_MODEL = 1024
RWKV_HEADS = 8
HEAD_DIM = 64
RWKV_WIDTH = RWKV_HEADS * HEAD_DIM
DECAY_LORA = 64
A_LORA = 64
GATE_LORA = 128
GMLP_GROUPS = 4
GMLP_WIDTH = 256
CHUNK = 128
MEM_HEADS = 4
MEM_WIDTH = 256
N_MEM = 256
N_EXPERTS = 32
TOP_K = 4
D_FF = 1024
SWIGLU_ALPHA = 1.702
SWIGLU_LIMIT = 7.0
RMS_EPS = 1e-5
LN_EPS = 1e-5
GN_EPS = 64e-5
SHIFT_WIDTH = 3 * RWKV_WIDTH + DECAY_LORA + A_LORA + GATE_LORA
OFF_GMLP = SHIFT_WIDTH
OFF_QMEM = OFF_GMLP + 2 * GMLP_WIDTH
OFF_GATE = OFF_QMEM + MEM_WIDTH
IN_WIDTH = OFF_GATE + 3 * D_MODEL
LANES = 128
ROW_SUB = D_MODEL // LANES

PROJ_TM = 256
MERGE_TM = 512
RWKV_TT = 512
RWKV_C = 64
RWKV_GROUP = 4
RWKV_SAMPLE_SEQS = 8
GMLP_TILE = 4
ATTN_TT = 512
ATTN_BB = 8
MOE_BM = 512
DISPATCH_TM = 256
COMBINE_TM = 128

ROUTE_E = 0
ROUTE_RANK = 4
ROUTE_GATE = 8

NN = ((1,), (0,))
NT = ((1,), (1,))
TN = ((0,), (0,))

VMEM_LIMIT = 56 * 1024 * 1024


def _params(sem, vmem=VMEM_LIMIT):
    return pltpu.CompilerParams(dimension_semantics=sem, vmem_limit_bytes=vmem)


def _bdot(a, b, dims=NN):
    return lax.dot_general(a.astype(BF16), b.astype(BF16), (dims, ((), ())), preferred_element_type=F32)


def _split(x):
    hi = x.astype(BF16)
    lo = (x - hi.astype(F32)).astype(BF16)
    return hi, lo


def _dot3(a, b, dims=NN):
    dn = (dims, ((), ()))
    ah, al = _split(a)
    bh, bl = _split(b)
    r = lax.dot_general(ah, bh, dn, preferred_element_type=F32)
    r = r + lax.dot_general(al, bh, dn, preferred_element_type=F32)
    return r + lax.dot_general(ah, bl, dn, preferred_element_type=F32)


def _rms(x, g):
    return x * lax.rsqrt(jnp.mean(x * x, axis=-1, keepdims=True) + RMS_EPS) * g


def _store_row_tiles(ref, x):
    m = x.shape[0]
    for j in range(ROW_SUB):
        ref[pl.ds(j, m, stride=ROW_SUB), :] = x[:, j * LANES:(j + 1) * LANES]


def _load_row_tiles(ref, m):
    return jnp.concatenate([ref[pl.ds(j, m, stride=ROW_SUB), :] for j in range(ROW_SUB)], axis=1)


def _sigmoid(x):
    return 0.5 * jnp.tanh(0.5 * x) + 0.5


def _in_proj_body(xp_ref, xs_ref, g_ref, w_ref, zr_ref, zg_ref, zq_ref, sg_ref, *, np_tiles):
    x = jnp.where(pl.program_id(0) < np_tiles, xp_ref[...], xs_ref[...])
    h = _rms(x, g_ref[...]).astype(BF16)
    zr_ref[...] = jnp.dot(h, w_ref[:, 0:OFF_GMLP], preferred_element_type=F32)
    zg_ref[...] = jnp.dot(h, w_ref[:, OFF_GMLP:OFF_QMEM], preferred_element_type=F32)
    zq_ref[...] = jnp.dot(h, w_ref[:, OFF_QMEM:OFF_GATE], preferred_element_type=F32).astype(zq_ref.dtype)
    gates = jnp.dot(h, w_ref[:, OFF_GATE:IN_WIDTH], preferred_element_type=F32)
    sg_ref[...] = _sigmoid(gates).astype(BF16)


def _in_proj(x_p, x_s, g, w_bf16):
    n_p = x_p.shape[0]
    n = n_p + x_s.shape[0]
    tm = PROJ_TM
    np_tiles = n_p // tm
    row = lambda i: (i, 0)
    const = lambda i: (0, 0)
    return pl.pallas_call(
        functools.partial(_in_proj_body, np_tiles=np_tiles),
        grid=(n // tm,),
        in_specs=[
            pl.BlockSpec((tm, D_MODEL), lambda i: (jnp.minimum(i, np_tiles - 1), 0)),
            pl.BlockSpec((tm, D_MODEL), lambda i: (jnp.maximum(i - np_tiles, 0), 0)),
            pl.BlockSpec((1, D_MODEL), const),
            pl.BlockSpec((D_MODEL, IN_WIDTH), const, pipeline_mode=pl.Buffered(1)),
        ],
        out_specs=[
            pl.BlockSpec((tm, SHIFT_WIDTH), row),
            pl.BlockSpec((tm, 2 * GMLP_WIDTH), row),
            pl.BlockSpec((tm, MEM_WIDTH), row),
            pl.BlockSpec((tm, 3 * D_MODEL), row),
        ],
        out_shape=[
            jax.ShapeDtypeStruct((n, SHIFT_WIDTH), F32),
            jax.ShapeDtypeStruct((n, 2 * GMLP_WIDTH), F32),
            jax.ShapeDtypeStruct((n, MEM_WIDTH), BF16),
            jax.ShapeDtypeStruct((n, 3 * D_MODEL), BF16),
        ],
        compiler_params=_params(("parallel",)),
        name="in_proj",
    )(x_p, x_s, g, w_bf16)


def _mem_kv_body(x_ref, g_ref, w_ref, k_ref, v_ref):
    h = _rms(x_ref[...], g_ref[...]).astype(BF16)
    k_ref[...] = jnp.dot(h, w_ref[:, 0:MEM_WIDTH], preferred_element_type=F32)
    v_ref[...] = jnp.dot(h, w_ref[:, MEM_WIDTH:2 * MEM_WIDTH], preferred_element_type=F32)


def _mem_kv(mem, g, w_bf16):
    n = mem.shape[0]
    tm = PROJ_TM
    return pl.pallas_call(
        _mem_kv_body,
        grid=(n // tm,),
        in_specs=[
            pl.BlockSpec((tm, D_MODEL), lambda i: (i, 0)),
            pl.BlockSpec((1, D_MODEL), lambda i: (0, 0)),
            pl.BlockSpec((D_MODEL, 2 * MEM_WIDTH), lambda i: (0, 0)),
        ],
        out_specs=[pl.BlockSpec((tm, MEM_WIDTH), lambda i: (i, 0))] * 2,
        out_shape=[jax.ShapeDtypeStruct((n, MEM_WIDTH), F32)] * 2,
        compiler_params=_params(("parallel",)),
        name="mem_kv",
    )(mem, g, w_bf16)


EXP_M05 = math.exp(-0.5)


def _dotp(a, b, dims, passes):
    return _dot3(a, b, dims) if passes == 3 else _bdot(a, b, dims)


PREC = dict(a=1, t=1, lkv=1, tw=1, am=1, gh=3, y=1, s=3)


def _unit_lower_inverse(lows, c):
    rows = lax.broadcasted_iota(jnp.int32, (c, c), 0)
    cols = lax.broadcasted_iota(jnp.int32, (c, c), 1)
    eye = (rows == cols).astype(F32)
    invs = [eye - low for low in lows]
    powers = lows
    for _ in range(int(math.log2(c)) - 1):
        powers = [_dotp(pw, pw, NN, PREC["t"]) for pw in powers]
        invs = [inv + _dotp(inv, pw, NN, PREC["t"]) for inv, pw in zip(invs, powers)]
    return invs


def _rwkv_body(zr_ref, shift_ref, s0_ref, mu_ref, w0_ref, wd_ref, a0_ref, wa_ref, wg_ref, kk_ref, ka_ref,
               rk_ref, lng_ref, lnb_ref, hsum_ref,
               o_ref, sout_ref, shout_ref,
               carry_ref, state_ref, r_s, k_s, v_s, kk_s, b_s, ld_s, y_s, bon_s, g_s,
               rw_s, y0_s, gm_s, h0_s, *, tt, c, nseq):
    i = pl.program_id(1)
    rw = RWKV_WIDTH
    nh = RWKV_HEADS

    @pl.when(i == 0)
    def _():
        for q in range(nseq):
            state_ref[q * nh:(q + 1) * nh] = s0_ref[q]

    z = zr_ref[0]
    z_prev = pltpu.roll(z, 1, 0)
    row = lax.broadcasted_iota(jnp.int32, z.shape, 0)
    if nseq == 1:
        @pl.when(i == 0)
        def _():
            carry_ref[...] = shift_ref[0]

        z_prev = jnp.where(row == 0, carry_ref[...], z_prev)
        carry_ref[...] = z[tt - 1:tt, :]
    else:
        first_rows = jnp.concatenate([jnp.broadcast_to(shift_ref[q], (c, SHIFT_WIDTH)) for q in range(nseq)], axis=0)
        z_prev = jnp.where(row % c == 0, first_rows, z_prev)
    zs = z + mu_ref[...] * (z_prev - z)
    r = zs[:, 0:rw]
    k = zs[:, rw:2 * rw]
    v = zs[:, 2 * rw:3 * rw]
    zw = zs[:, 3 * rw:3 * rw + DECAY_LORA]
    za = zs[:, 3 * rw + DECAY_LORA:3 * rw + DECAY_LORA + A_LORA]
    zg = zs[:, 3 * rw + DECAY_LORA + A_LORA:SHIFT_WIDTH]
    xw = w0_ref[...] + _bdot(jnp.tanh(zw), wd_ref[...])
    ld_s[...] = -EXP_M05 * _sigmoid(xw)
    a = _sigmoid(a0_ref[...] + _bdot(za, wa_ref[...]))
    g_s[...] = _bdot(_sigmoid(zg), wg_ref[...])
    kk = k * kk_ref[...]
    k = k * (1.0 + (a - 1.0) * ka_ref[...])
    r_s[...] = r
    k_s[...] = k
    v_s[...] = v

    def head_sum(t):
        return jnp.dot(t.astype(BF16), hsum_ref[...], preferred_element_type=F32)

    kk = kk / jnp.maximum(jnp.sqrt(head_sum(kk * kk)), 1e-12)
    kk_s[...] = kk
    b_s[...] = kk * a
    bon_s[...] = head_sum(r * k * rk_ref[...]) * v

    rows_c = lax.broadcasted_iota(jnp.int32, (c, c), 0)
    cols_c = lax.broadcasted_iota(jnp.int32, (c, c), 1)
    strict = rows_c > cols_c
    incl = rows_c >= cols_c
    tril_ones = incl.astype(BF16)
    rows_2c = lax.broadcasted_iota(jnp.int32, (c, 2 * c), 0)
    cols_2c = lax.broadcasted_iota(jnp.int32, (c, 2 * c), 1)
    incl2 = rows_2c >= jnp.where(cols_2c >= c, cols_2c - c, cols_2c)
    rows_k = lax.broadcasted_iota(jnp.int32, (HEAD_DIM, HEAD_DIM), 0)
    cols_k = lax.broadcasted_iota(jnp.int32, (HEAD_DIM, HEAD_DIM), 1)
    eye_k = (rows_k == cols_k).astype(F32)
    zeros_cv = jnp.zeros((c, HEAD_DIM), F32)
    heads = [slice(h * HEAD_DIM, (h + 1) * HEAD_DIM) for h in range(RWKV_HEADS)]

    n_chunks = tt // c
    group = nseq if nseq > 1 else math.gcd(n_chunks, RWKV_GROUP)

    def chunk_rows(ci):
        return pl.ds(ci * c, c) if isinstance(ci, int) else pl.ds(pl.multiple_of(ci * c, c), c)

    def scaled(ci):
        rows = chunk_rows(ci)
        ld = ld_s[rows, :]
        ld_hi, ld_lo = _split(ld)
        cum = (jnp.dot(tril_ones, ld_hi, preferred_element_type=F32)
               + jnp.dot(tril_ones, ld_lo, preferred_element_type=F32))
        e_inc = jnp.exp(cum)
        e_neg = jnp.exp(-cum)
        kt = k_s[rows, :] * e_neg
        bt = b_s[rows, :] * e_neg
        g_end = e_inc[c - 1:c, :]
        return dict(ci=ci, rt=r_s[rows, :] * e_inc, kkt=kk_s[rows, :] * jnp.exp(cum - ld), kt=kt, bt=bt,
                    g_end=g_end, bc=bt * g_end, kc=kt * g_end, vv=v_s[rows, :])

    def phase1(cj, carry):
        chains = [(ch, h, s) for ch in [scaled(cj * group + g) for g in range(group)]
                  for h, s in enumerate(heads)]
        x = [jnp.concatenate([ch["kkt"][:, s], ch["rt"][:, s]], axis=0) for ch, _, s in chains]
        zz = [jnp.concatenate([ch["bt"][:, s], ch["kt"][:, s]], axis=0) for ch, _, s in chains]
        amat = [_dotp(xh, zh, NT, PREC["a"]) for xh, zh in zip(x, zz)]
        l_b = [jnp.where(strict, am[0:c, 0:c], 0.0) for am in amat]
        l_k = [jnp.where(strict, am[0:c, c:2 * c], 0.0) for am in amat]
        a_r = [jnp.where(incl2, am[c:2 * c, :], 0.0) for am in amat]
        tinv = _unit_lower_inverse(l_b, c)
        lkv = [_dotp(lk, ch["vv"][:, s], NN, PREC["lkv"]) for lk, (ch, _, s) in zip(l_k, chains)]
        wu = [-_dotp(t, jnp.concatenate([ch["kkt"][:, s], lv], axis=1), NN, PREC["tw"])
              for t, lv, (ch, _, s) in zip(tinv, lkv, chains)]
        m = [jnp.concatenate([w, jnp.concatenate([zeros_cv, ch["vv"][:, s]], axis=1)], axis=0)
             for w, (ch, _, s) in zip(wu, chains)]
        am2 = [_dotp(ar, mh, NN, PREC["am"]) for ar, mh in zip(a_r, m)]
        gh = [_dotp(mh, jnp.concatenate([ch["bc"][:, s], ch["kc"][:, s]], axis=0), TN, PREC["gh"])
              for mh, (ch, _, s) in zip(m, chains)]
        for q, (ch, h, s) in enumerate(chains):
            idx = ch["ci"] * RWKV_HEADS + h
            rw_s[idx] = ch["rt"][:, s] + am2[q][:, 0:HEAD_DIM]
            y0_s[idx] = am2[q][:, HEAD_DIM:2 * HEAD_DIM]
            gm_s[idx] = gh[q][0:HEAD_DIM, :] + eye_k * ch["g_end"][:, s]
            h0_s[idx] = gh[q][HEAD_DIM:2 * HEAD_DIM, :]
        return carry

    if n_chunks == group:
        phase1(0, 0)
    else:
        lax.fori_loop(0, n_chunks // group, phase1, 0)

    if nseq == 1:
        def phase2(ci, carry):
            rows = chunk_rows(ci)
            for h, s in enumerate(heads):
                idx = ci * nh + h
                s_h = state_ref[h]
                y_s[rows, s] = _dotp(rw_s[idx], s_h, NT, PREC["y"]) + y0_s[idx]
                state_ref[h] = _dotp(s_h, gm_s[idx], NN, PREC["s"]) + h0_s[idx]
            return carry

        lax.fori_loop(0, n_chunks, phase2, 0)
    else:
        pairs = [(q, h, s) for q in range(nseq) for h, s in enumerate(heads)]
        states = [state_ref[q * nh + h] for q, h, _ in pairs]
        ys = [_dotp(rw_s[q * nh + h], st, NT, PREC["y"]) + y0_s[q * nh + h] for (q, h, _), st in zip(pairs, states)]
        new = [_dotp(st, gm_s[q * nh + h], NN, PREC["s"]) + h0_s[q * nh + h] for (q, h, _), st in zip(pairs, states)]
        for (q, h, s), yq, nq in zip(pairs, ys, new):
            y_s[chunk_rows(q), s] = yq
            state_ref[q * nh + h] = nq
    y = y_s[...]
    yc = y - head_sum(y) * (1.0 / HEAD_DIM)
    var = head_sum(yc * yc) * (1.0 / HEAD_DIM)
    yn = yc * lax.rsqrt(var + GN_EPS)
    o_ref[0] = ((yn * lng_ref[...] + lnb_ref[...] + bon_s[...]) * g_s[...]).astype(o_ref.dtype)

    @pl.when(i == pl.num_programs(1) - 1)
    def _():
        for q in range(nseq):
            sout_ref[q] = state_ref[q * nh:(q + 1) * nh]
            last = tt - 1 if nseq == 1 else (q + 1) * c - 1
            shout_ref[q] = zr_ref[0, last:last + 1, :]


def _rwkv(zr3, blk_off, b, nt, shift_prev, s_prev, p, tt, c, nseq):
    vec = lambda n: pl.BlockSpec((1, n), lambda bi, ti: (0, 0))
    mat = lambda m, n: pl.BlockSpec((m, n), lambda bi, ti: (0, 0))
    rw = RWKV_WIDTH
    scr = lambda: pltpu.VMEM((tt, rw), F32)
    nch = (tt // c) * RWKV_HEADS
    return pl.pallas_call(
        functools.partial(_rwkv_body, tt=tt, c=c, nseq=nseq),
        grid=(b // nseq, nt),
        in_specs=[
            pl.BlockSpec((1, tt, SHIFT_WIDTH), lambda bi, ti: (blk_off + bi * nt + ti, 0, 0)),
            pl.BlockSpec((nseq, 1, SHIFT_WIDTH), lambda bi, ti: (bi, 0, 0)),
            pl.BlockSpec((nseq, RWKV_HEADS, HEAD_DIM, HEAD_DIM), lambda bi, ti: (bi, 0, 0, 0)),
            vec(SHIFT_WIDTH), vec(rw), mat(DECAY_LORA, rw), vec(rw), mat(A_LORA, rw), mat(GATE_LORA, rw),
            vec(rw), vec(rw), vec(rw), vec(rw), vec(rw), mat(rw, rw),
        ],
        out_specs=[
            pl.BlockSpec((1, tt, rw), lambda bi, ti: (bi, ti, 0)),
            pl.BlockSpec((nseq, RWKV_HEADS, HEAD_DIM, HEAD_DIM), lambda bi, ti: (bi, 0, 0, 0)),
            pl.BlockSpec((nseq, 1, SHIFT_WIDTH), lambda bi, ti: (bi, 0, 0)),
        ],
        out_shape=[
            jax.ShapeDtypeStruct((b // nseq, nt * tt, rw), BF16),
            jax.ShapeDtypeStruct((b, RWKV_HEADS, HEAD_DIM, HEAD_DIM), F32),
            jax.ShapeDtypeStruct((b, 1, SHIFT_WIDTH), F32),
        ],
        scratch_shapes=[
            pltpu.VMEM((1, SHIFT_WIDTH), F32),
            pltpu.VMEM((nseq * RWKV_HEADS, HEAD_DIM, HEAD_DIM), F32),
            scr(), scr(), scr(), scr(), scr(), scr(), scr(), scr(), scr(),
            pltpu.VMEM((nch, c, HEAD_DIM), F32),
            pltpu.VMEM((nch, c, HEAD_DIM), F32),
            pltpu.VMEM((nch, HEAD_DIM, HEAD_DIM), F32),
            pltpu.VMEM((nch, HEAD_DIM, HEAD_DIM), F32),
        ],
        compiler_params=_params(("parallel", "arbitrary")),
        name="rwkv",
    )(zr3, shift_prev, s_prev, p["mu"], p["w0"], p["wd"], p["a0"], p["wa"], p["wg"], p["kk"], p["ka"],
      p["rk"], p["lng"], p["lnb"], p["hsum"])


def _gmlp_body(z_ref, lng_ref, lnb_ref, ws_ref, bs_ref, o_ref, v_ref):
    z = z_ref[...]
    ge = 0.5 * z * (1.0 + lax.erf(z * (1.0 / math.sqrt(2.0))))
    u = ge[:, 0:GMLP_WIDTH]
    v = ge[:, GMLP_WIDTH:2 * GMLP_WIDTH]
    mean = jnp.mean(v, axis=-1, keepdims=True)
    vc = v - mean
    var = jnp.mean(vc * vc, axis=-1, keepdims=True)
    vn = vc * lax.rsqrt(var + LN_EPS) * lng_ref[...] + lnb_ref[...]
    v_ref[...] = vn
    gd = GMLP_WIDTH // GMLP_GROUPS
    for q in range(GMLP_TILE):
        rows = slice(q * CHUNK, (q + 1) * CHUNK)
        for g in range(GMLP_GROUPS):
            gs = slice(g * gd, (g + 1) * gd)
            sv = jnp.dot(ws_ref[g], vn[rows, gs].astype(BF16), preferred_element_type=F32) + bs_ref[:, g:g + 1]
            o_ref[rows, gs] = (u[rows, gs] * sv).astype(o_ref.dtype)


def _gmlp(zg, tile_off, n_rows, lng, lnb, ws_bf16, bs):
    tile = GMLP_TILE * CHUNK
    return pl.pallas_call(
        _gmlp_body,
        grid=(n_rows // tile,),
        in_specs=[
            pl.BlockSpec((tile, 2 * GMLP_WIDTH), lambda i: (tile_off + i, 0)),
            pl.BlockSpec((1, GMLP_WIDTH), lambda i: (0, 0)),
            pl.BlockSpec((1, GMLP_WIDTH), lambda i: (0, 0)),
            pl.BlockSpec((GMLP_GROUPS, CHUNK, CHUNK), lambda i: (0, 0, 0)),
            pl.BlockSpec((CHUNK, GMLP_GROUPS), lambda i: (0, 0)),
        ],
        out_specs=[
            pl.BlockSpec((tile, GMLP_WIDTH), lambda i: (i, 0)),
            pl.BlockSpec((tile, GMLP_WIDTH), lambda i: (i, 0)),
        ],
        out_shape=[
            jax.ShapeDtypeStruct((n_rows, GMLP_WIDTH), BF16),
            jax.ShapeDtypeStruct((n_rows, GMLP_WIDTH), F32),
        ],
        compiler_params=_params(("parallel",)),
        name="gmlp",
    )(zg, lng, lnb, ws_bf16, bs)


def _mem_attn_body(q_ref, k_ref, v_ref, o_ref, *, bb, feature_major):
    hd = MEM_WIDTH // MEM_HEADS
    pairs = [(s, slice(h * hd, (h + 1) * hd)) for s in range(bb) for h in range(MEM_HEADS)]
    if feature_major:
        scores = [_bdot(q_ref[s, :, hs], k_ref[s, hs, :], NN) * (hd ** -0.5) for s, hs in pairs]
    else:
        scores = [_bdot(q_ref[s, :, hs], k_ref[s, :, hs], NT) * (hd ** -0.5) for s, hs in pairs]
    probs = []
    for sc in scores:
        pr = jnp.exp(sc - jnp.max(sc, axis=-1, keepdims=True))
        probs.append(pr / jnp.sum(pr, axis=-1, keepdims=True))
    for (s, hs), pr in zip(pairs, probs):
        if feature_major:
            o_ref[s, :, hs] = _bdot(pr, v_ref[s, hs, :], NT).astype(o_ref.dtype)
        else:
            o_ref[s, :, hs] = _bdot(pr, v_ref[s, :, hs]).astype(o_ref.dtype)


def _mem_attn(q3, blk_off, b, nt, mk, mv, tt, bb, feature_major):
    return pl.pallas_call(
        functools.partial(_mem_attn_body, bb=bb, feature_major=feature_major),
        grid=(b // bb, nt),
        in_specs=[
            pl.BlockSpec((bb, tt, MEM_WIDTH), lambda bi, ti: (blk_off // bb + bi * nt + ti, 0, 0)),
            pl.BlockSpec((bb, N_MEM, MEM_WIDTH), lambda bi, ti: (bi, 0, 0)),
            pl.BlockSpec((bb, N_MEM, MEM_WIDTH), lambda bi, ti: (bi, 0, 0)),
        ],
        out_specs=pl.BlockSpec((bb, tt, MEM_WIDTH), lambda bi, ti: (bi, ti, 0)),
        out_shape=jax.ShapeDtypeStruct((b, nt * tt, MEM_WIDTH), BF16),
        compiler_params=_params(("parallel", "parallel")),
        name="mem_attn",
    )(q3, mk, mv)


def _merge_body(x_p, x_s, orw_p, orw_s, ogm_p, ogm_s, ome_p, ome_s, sg_ref, wbr_ref, wbg_ref, wbm_ref, wout_ref,
                gffn_ref, wrt_ref, brt_ref, x2_ref, h2_ref, route_ref, route_t_ref, cnt_ref, count_s, *, np_tiles):
    i = pl.program_id(0)
    d = D_MODEL
    tm = x_p.shape[0]
    is_p = i < np_tiles
    x = jnp.where(is_p, x_p[...], x_s[...])
    orw = jnp.where(is_p, orw_p[...], orw_s[...])
    ogm = jnp.where(is_p, ogm_p[...], ogm_s[...])
    ome = jnp.where(is_p, ome_p[...], ome_s[...])
    merged = sg_ref[:, 0:d].astype(F32) * _bdot(orw, wbr_ref[...])
    merged = merged + sg_ref[:, d:2 * d].astype(F32) * _bdot(ogm, wbg_ref[...])
    merged = merged + sg_ref[:, 2 * d:3 * d].astype(F32) * _bdot(ome, wbm_ref[...])
    x2 = x + _bdot(merged, wout_ref[...])
    x2_ref[...] = x2
    h2 = _rms(x2, gffn_ref[...])
    _store_row_tiles(h2_ref, h2)

    @pl.when(i == 0)
    def _():
        count_s[...] = jnp.zeros_like(count_s)

    lane = lax.broadcasted_iota(jnp.int32, (tm, LANES), 1)
    logits = jnp.where(lane < N_EXPERTS, _dot3(h2, wrt_ref[...]) + brt_ref[...], -jnp.inf)
    lane_f = lane.astype(F32)
    tops, hots, idxs = [], [], []
    for _ in range(TOP_K):
        top = jnp.max(logits, axis=-1, keepdims=True)
        idx = jnp.min(jnp.where(logits == top, lane_f, float(LANES)), axis=-1, keepdims=True)
        hot = lane_f == idx
        logits = jnp.where(hot, -jnp.inf, logits)
        tops.append(top)
        hots.append(hot)
        idxs.append(idx)
    weights = [jnp.exp(t - tops[0]) for t in tops]
    denom = weights[0] + weights[1] + weights[2] + weights[3]
    onehot = jnp.zeros((tm, LANES), F32)
    for hot in hots:
        onehot = onehot + jnp.where(hot, 1.0, 0.0)
    rows_t = lax.broadcasted_iota(jnp.int32, (tm, tm), 0)
    cols_t = lax.broadcasted_iota(jnp.int32, (tm, tm), 1)
    ahead = jnp.where(rows_t > cols_t, 1.0, 0.0).astype(BF16)
    prefix = jnp.dot(ahead, onehot.astype(BF16), preferred_element_type=F32) + count_s[...]
    route = jnp.zeros((tm, LANES), F32)
    for kk in range(TOP_K):
        rank = jnp.sum(jnp.where(hots[kk], prefix, 0.0), axis=-1, keepdims=True)
        route = jnp.where(lane == ROUTE_E + kk, idxs[kk], route)
        route = jnp.where(lane == ROUTE_RANK + kk, rank, route)
        route = jnp.where(lane == ROUTE_GATE + kk, weights[kk] / denom, route)
    route_ref[...] = route
    route_t_ref[...] = route.T
    count_s[...] = count_s[...] + jnp.sum(onehot, axis=0, keepdims=True)
    cnt_ref[...] = jnp.broadcast_to(count_s[...], cnt_ref.shape)


def _merge(x, o_rw, o_gm, o_me, sg, wbr, wbg, wbm, wout, gffn, wrt_pad, brt_pad):
    n_p = x[0].shape[0]
    n = n_p + x[1].shape[0]
    tm = MERGE_TM
    d = D_MODEL
    np_tiles = n_p // tm
    row = lambda i: (i, 0)
    const = lambda i: (0, 0)
    first =lambda i: (jnp.minimum(i, np_tiles - 1), 0)
    second = lambda i: (jnp.maximum(i - np_tiles, 0), 0)

    def pair(width):
        return [pl.BlockSpec((tm, width), first), pl.BlockSpec((tm, width), second)]

    return pl.pallas_call(
        functools.partial(_merge_body, np_tiles=np_tiles),
        grid=(n // tm,),
        in_specs=pair(d) + pair(RWKV_WIDTH) + pair(GMLP_WIDTH) + pair(MEM_WIDTH) + [
            pl.BlockSpec((tm, 3 * d), row),
            pl.BlockSpec((RWKV_WIDTH, d), const),
            pl.BlockSpec((GMLP_WIDTH, d), const),
            pl.BlockSpec((MEM_WIDTH, d), const),
            pl.BlockSpec((d, d), const),
            pl.BlockSpec((1, d), const),
            pl.BlockSpec((d, LANES), const),
            pl.BlockSpec((1, LANES), const),
        ],
        out_specs=[
            pl.BlockSpec((tm, d), row),
            pl.BlockSpec((tm * ROW_SUB, LANES), row),
            pl.BlockSpec((tm, LANES), row),
            pl.BlockSpec((LANES, tm), lambda i: (0, i)),
            pl.BlockSpec((8, LANES), const),
        ],
        out_shape=[
            jax.ShapeDtypeStruct((n, d), F32),
            jax.ShapeDtypeStruct((n * ROW_SUB, LANES), F32),
            jax.ShapeDtypeStruct((n, LANES), F32),
            jax.ShapeDtypeStruct((LANES, n), F32),
            jax.ShapeDtypeStruct((8, LANES), F32),
        ],
        scratch_shapes=[pltpu.VMEM((1, LANES), F32)],
        compiler_params=_params(("arbitrary",)),
        name="merge",
    )(x[0], x[1], o_rw[0], o_rw[1], o_gm[0], o_gm[1], o_me[0], o_me[1], sg, wbr, wbg, wbm, wout, gffn, wrt_pad,
      brt_pad)


N_ZERO_BLOCKS = 2 * N_EXPERTS


def _dispatch_body(dest_ref, zlist_ref, h_ref, xs_hbm, zero_s, sem_z, sem):
    i = pl.program_id(0)
    tm = DISPATCH_TM
    bm = MOE_BM

    @pl.when(i == 0)
    def _():
        zero_s[...] = jnp.zeros_like(zero_s)

        def zero_copy(q):
            start = pl.multiple_of(zlist_ref[q] * bm, bm)
            return pltpu.make_async_copy(zero_s, xs_hbm.at[pl.ds(start, bm)], sem_z)

        def start(q, carry):
            @pl.when(zlist_ref[q] >= 0)
            def _():
                zero_copy(q).start()
            return carry

        def wait(q, carry):
            @pl.when(zlist_ref[q] >= 0)
            def _():
                zero_copy(q).wait()
            return carry

        lax.fori_loop(0, N_ZERO_BLOCKS, start, 0)
        lax.fori_loop(0, N_ZERO_BLOCKS, wait, 0)

    base = i * tm
    n_tok = pl.num_programs(0) * tm

    def body(r, carry):
        for kk in range(TOP_K):
            slot = dest_ref[kk * n_tok + base + r]
            pltpu.make_async_copy(h_ref.at[r], xs_hbm.at[slot], sem).start(priority=kk % 2)
        return carry

    lax.fori_loop(0, tm, body, 0, unroll=2)
    for kk in range(TOP_K):
        pltpu.make_async_copy(h_ref, xs_hbm.at[pl.ds(0, tm)], sem).wait()


def _dispatch(dest, zlist, h, n_blocks):
    n = h.shape[0]
    tm = DISPATCH_TM
    grid_spec = pltpu.PrefetchScalarGridSpec(
        num_scalar_prefetch=2,
        grid=(n // tm,),
        in_specs=[pl.BlockSpec((tm, ROW_SUB, LANES), lambda i, dest, zl: (i, 0, 0))],
        out_specs=pl.BlockSpec(memory_space=pl.ANY),
        scratch_shapes=[
            pltpu.VMEM((MOE_BM, ROW_SUB, LANES), F32),
            pltpu.SemaphoreType.DMA(()),
            pltpu.SemaphoreType.DMA(()),
        ],
    )
    return pl.pallas_call(
        _dispatch_body,
        grid_spec=grid_spec,
        out_shape=jax.ShapeDtypeStruct((n_blocks * MOE_BM, ROW_SUB, LANES), F32),
        compiler_params=_params(("arbitrary",)),
        name="moe_dispatch",
    )(dest, zlist, h)


CAST_ROWS = 64


def _moe_body(be_ref, nused_ref, nxt_ref, par_ref, x_ref, w1_hbm, b1_ref, w2_hbm, b2_ref, o_ref,
              w1f, w2f, w1b, w2b, sem):
    j = pl.program_id(0)
    n_used = nused_ref[0]

    def weight_copies(expert, slot):
        return (pltpu.make_async_copy(w1_hbm.at[expert], w1f.at[slot], sem.at[slot, 0]),
                pltpu.make_async_copy(w2_hbm.at[expert], w2f.at[slot], sem.at[slot, 1]))

    @pl.when(j == 0)
    def _():
        for cp in weight_copies(be_ref[0], 0):
            cp.start()

    @pl.when(j < n_used)
    def _():
        slot = par_ref[be_ref[j]]
        changed = jnp.logical_or(j == 0, be_ref[j] != be_ref[jnp.maximum(j - 1, 0)])

        @pl.when(changed)
        def _():
            for cp in weight_copies(be_ref[j], slot):
                cp.wait()

            nxt = nxt_ref[be_ref[j]]

            @pl.when(nxt >= 0)
            def _():
                for cp in weight_copies(nxt, 1 - slot):
                    cp.start()

            def cast1(q, carry):
                rows = pl.ds(pl.multiple_of(q * CAST_ROWS, CAST_ROWS), CAST_ROWS)
                w1b[rows, :] = w1f[slot, rows, :].astype(BF16)
                return carry

            def cast2(q, carry):
                rows = pl.ds(pl.multiple_of(q * CAST_ROWS, CAST_ROWS), CAST_ROWS)
                w2b[rows, :] = w2f[slot, rows, :].astype(BF16)
                return carry

            lax.fori_loop(0, D_MODEL // CAST_ROWS, cast1, 0)
            lax.fori_loop(0, D_FF // CAST_ROWS, cast2, 0)

        x = _load_row_tiles(x_ref, MOE_BM).astype(BF16)
        z = jnp.dot(x, w1b[...], preferred_element_type=F32) + b1_ref[0]
        zg = jnp.minimum(z[:, 0:D_FF], SWIGLU_LIMIT)
        zl = jnp.clip(z[:, D_FF:2 * D_FF], -SWIGLU_LIMIT, SWIGLU_LIMIT)
        act = zg * _sigmoid(SWIGLU_ALPHA * zg) * (zl + 1.0)
        _store_row_tiles(o_ref, jnp.dot(act.astype(BF16), w2b[...], preferred_element_type=F32) + b2_ref[0])

    @pl.when(j >= n_used)
    def _():
        o_ref[...] = jnp.zeros_like(o_ref)


def _moe(blk_e, n_used, nxt_e, parity, xs, w1, b1, w2, b2, n_blocks):
    bm = MOE_BM
    d = D_MODEL
    grid_spec = pltpu.PrefetchScalarGridSpec(
        num_scalar_prefetch=4,
        grid=(n_blocks,),
        in_specs=[
            pl.BlockSpec((bm * ROW_SUB, LANES), lambda j, be, nu, nx, pa: (j, 0)),
            pl.BlockSpec(memory_space=pl.ANY),
            pl.BlockSpec((1, 1, 2 * D_FF), lambda j, be, nu, nx, pa: (be[j], 0, 0)),
            pl.BlockSpec(memory_space=pl.ANY),
            pl.BlockSpec((1, 1, d), lambda j, be, nu, nx, pa: (be[j], 0, 0)),
        ],
        out_specs=pl.BlockSpec((bm * ROW_SUB, LANES), lambda j, be, nu, nx, pa: (j, 0)),
        scratch_shapes=[
            pltpu.VMEM((2, d, 2 * D_FF), F32),
            pltpu.VMEM((2, D_FF, d), F32),
            pltpu.VMEM((d, 2 * D_FF), BF16),
            pltpu.VMEM((D_FF, d), BF16),
            pltpu.SemaphoreType.DMA((2, 2)),
        ],
    )
    return pl.pallas_call(
        _moe_body,
        grid_spec=grid_spec,
        out_shape=jax.ShapeDtypeStruct((n_blocks * bm * ROW_SUB, LANES), F32),
        compiler_params=_params(("arbitrary",)),
        name="moe_ffn",
    )(blk_e, n_used, nxt_e, parity, xs, w1, b1, w2, b2)


def _combine_body(dest_ref, x2_ref, route_ref, gfin_ref, yb_hbm, op_ref, os_ref, buf0, buf1, sem, *, np_steps):
    i = pl.program_id(0)
    n_steps = pl.num_programs(0)
    tm = COMBINE_TM
    n_tok = n_steps * (2 * tm)
    bufs = (buf0, buf1)

    def issue(tile, which):
        base = tile * tm
        for r in range(tm):
            for kk in range(TOP_K):
                slot = dest_ref[kk * n_tok + base + r]
                src = pl.ds(pl.multiple_of(slot * ROW_SUB, ROW_SUB), ROW_SUB)
                pltpu.make_async_copy(yb_hbm.at[src], bufs[which].at[kk, r * ROW_SUB:(r + 1) * ROW_SUB],
                                      sem.at[which]).start(priority=kk % 2)

    def wait(which):
        for kk in range(TOP_K):
            pltpu.make_async_copy(yb_hbm.at[pl.ds(0, tm * ROW_SUB)], bufs[which].at[kk], sem.at[which]).wait()

    def reduce(which, half):
        rows = slice(half * tm, (half + 1) * tm)
        acc = x2_ref[rows, :]
        for kk in range(TOP_K):
            gate = route_ref[rows, ROUTE_GATE + kk:ROUTE_GATE + kk + 1]
            acc = acc + gate * _load_row_tiles(bufs[which].at[kk], tm)
        return _rms(acc, gfin_ref[...])

    @pl.when(i == 0)
    def _():
        issue(0, 0)

    wait(0)
    issue(2 * i + 1, 1)
    y0 = reduce(0, 0)
    wait(1)
    issue(jnp.minimum(2 * i + 2, 2 * n_steps - 1), 0)
    y1 = reduce(1, 1)

    @pl.when(i < np_steps)
    def _():
        op_ref[0:tm, :] = y0
        op_ref[tm:2 * tm, :] = y1

    @pl.when(i >= np_steps)
    def _():
        os_ref[0:tm, :] = y0
        os_ref[tm:2 * tm, :] = y1

    @pl.when(i == n_steps - 1)
    def _():
        wait(0)


def _combine(dest, x2, route, gfin, yb, n_p):
    n, d = x2.shape
    tm = COMBINE_TM
    step = 2 * tm
    np_steps = n_p // step
    grid_spec = pltpu.PrefetchScalarGridSpec(
        num_scalar_prefetch=1,
        grid=(n // step,),
        in_specs=[
            pl.BlockSpec((step, d), lambda i, dest: (i, 0)),
            pl.BlockSpec((step, LANES), lambda i, dest: (i, 0)),
            pl.BlockSpec((1, d), lambda i, dest: (0, 0)),
            pl.BlockSpec(memory_space=pl.ANY),
        ],
        out_specs=[
            pl.BlockSpec((step, d), lambda i, dest: (jnp.minimum(i, np_steps - 1), 0)),
            pl.BlockSpec((step, d), lambda i, dest: (jnp.maximum(i - np_steps, 0), 0)),
        ],
        scratch_shapes=[
            pltpu.VMEM((TOP_K, tm * ROW_SUB, LANES), F32),
            pltpu.VMEM((TOP_K, tm * ROW_SUB, LANES), F32),
            pltpu.SemaphoreType.DMA((2,)),
        ],
    )
    return pl.pallas_call(
        functools.partial(_combine_body, np_steps=np_steps),
        grid_spec=grid_spec,
        out_shape=[jax.ShapeDtypeStruct((n_p, d), F32), jax.ShapeDtypeStruct((n - n_p, d), F32)],
        compiler_params=_params(("arbitrary",)),
        name="moe_combine",
    )(dest, x2, route, gfin, yb)


def _slot_tables(route_t, counts_row, n_blocks):
    bm = MOE_BM
    e = route_t[ROUTE_E:ROUTE_E + TOP_K].astype(jnp.int32)
    rank = route_t[ROUTE_RANK:ROUTE_RANK + TOP_K].astype(jnp.int32)
    counts = counts_row[:N_EXPERTS].astype(jnp.int32)
    padded = (counts + bm - 1) // bm * bm
    pad_end = jnp.cumsum(padded)
    pad_start = pad_end - padded
    experts = jnp.arange(N_EXPERTS, dtype=jnp.int32)[:, None, None]
    dest = (jnp.sum(jnp.where(e[None] == experts, pad_start[:, None, None], 0), axis=0) + rank).reshape(-1)
    blk_start = jnp.arange(n_blocks, dtype=jnp.int32) * bm
    blk_e = jnp.minimum(jnp.sum(pad_end[None, :] <= blk_start[:, None], axis=1), N_EXPERTS - 1).astype(jnp.int32)
    n_used = pad_end[-1] // bm
    last_blk = jnp.where(padded > 0, pad_end // bm - 1, -1)
    trailing = n_used + jnp.arange(N_EXPERTS, dtype=jnp.int32)
    trailing = jnp.where(trailing < n_blocks, trailing, -1)
    zlist = jnp.concatenate([last_blk, trailing]).astype(jnp.int32)
    ids = jnp.arange(N_EXPERTS, dtype=jnp.int32)
    present = counts > 0
    later = jnp.where(jnp.logical_and(ids[None, :] > ids[:, None], present[None, :]), ids[None, :], N_EXPERTS)
    next_present = jnp.min(later, axis=1)
    next_present = jnp.where(next_present < N_EXPERTS, next_present, -1)
    nxt_e = next_present.astype(jnp.int32)
    parity = ((jnp.cumsum(present.astype(jnp.int32)) - 1) % 2).astype(jnp.int32)
    return dest.astype(jnp.int32), blk_e, n_used.astype(jnp.int32).reshape(1), nxt_e, parity, zlist


def kernel(x_prompt, x_sample, mem_prompt, state_shift, state_wkv, cache_mem_k, cache_mem_v, g_norm_mix, w_in, mu_shift, w0, w_decay_up, a0, w_a_up, w_g_up, k_k, k_a, r_k, ln_x_g, ln_x_b, gmlp_ln_g, gmlp_ln_b, w_spatial, b_spatial, g_norm_mem, w_mem_kv, w_br_rwkv, w_br_gmlp, w_br_mem, w_out, g_norm_ffn, w_router, b_router, w_exp1, b_exp1, w_exp2, b_exp2, g_norm_final):
    bp, tp, d = x_prompt.shape
    bs, ts, _ = x_sample.shape
    n_p, n_s = bp * tp, bs * ts
    n_all = n_p + n_s
    l = 0
    row = lambda a: a.reshape(1, -1)

    x_pair = (x_prompt.reshape(n_p, d), x_sample.reshape(n_s, d))
    zr, zg, zq, sg = _in_proj(x_pair[0], x_pair[1], row(g_norm_mix[l]), w_in[l].astype(BF16))

    mk_p, mv_p = _mem_kv(mem_prompt.reshape(bp * N_MEM, d), row(g_norm_mem[l]), w_mem_kv[l].astype(BF16))

    rp = dict(mu=row(mu_shift[l]), w0=row(w0[l]), wd=w_decay_up[l].astype(BF16), a0=row(a0[l]),
              wa=w_a_up[l].astype(BF16), wg=w_g_up[l].astype(BF16), kk=row(k_k[l]), ka=row(k_a[l]),
              rk=row(r_k[l]), lng=row(ln_x_g[l]), lnb=row(ln_x_b[l]))
    head_of = jnp.arange(RWKV_WIDTH, dtype=jnp.int32) // HEAD_DIM
    rp["hsum"] = (head_of[:, None] == head_of[None, :]).astype(BF16)
    o_rw_p, s_p, shift_p = _rwkv(zr.reshape(n_all // RWKV_TT, RWKV_TT, SHIFT_WIDTH), 0, bp, tp // RWKV_TT,
                        jnp.zeros((bp, 1, SHIFT_WIDTH), F32), jnp.zeros((bp, RWKV_HEADS, HEAD_DIM, HEAD_DIM), F32),
                        rp, tt=RWKV_TT, c=RWKV_C, nseq=1)
    tile_s = RWKV_SAMPLE_SEQS * ts
    o_rw_s, s_s, shift_s = _rwkv(zr.reshape(n_all // tile_s, tile_s, SHIFT_WIDTH), n_p // tile_s, bs, 1,
                        state_shift[l].reshape(bs, 1, SHIFT_WIDTH), state_wkv[l], rp, tt=tile_s, c=ts,
                        nseq=RWKV_SAMPLE_SEQS)

    tri = jnp.tril(jnp.ones((CHUNK, CHUNK), bool))
    ws_p = jnp.where(tri, w_spatial[l], 0.0).astype(BF16)
    bs_p = b_spatial[l].T
    reps = CHUNK // ts
    tri_s = jnp.tril(jnp.ones((ts, ts), bool))
    ws_small = jnp.where(tri_s, w_spatial[l][:, :ts, :ts], 0.0)
    eye = jnp.eye(reps, dtype=F32)
    ws_s = jnp.einsum("ab,gij->gaibj", eye, ws_small).reshape(GMLP_GROUPS, CHUNK, CHUNK).astype(BF16)
    bs_s = jnp.tile(b_spatial[l][:, :ts], (1, reps)).T
    lng, lnb = row(gmlp_ln_g[l]), row(gmlp_ln_b[l])
    o_gm_p, _ = _gmlp(zg, 0, n_p, lng, lnb, ws_p, bs_p)
    o_gm_s, v_rows_s = _gmlp(zg, n_p // (GMLP_TILE * CHUNK), n_s, lng, lnb, ws_s, bs_s)

    o_me_p = _mem_attn(zq.reshape(n_all // ATTN_TT, ATTN_TT, MEM_WIDTH), 0, bp, tp // ATTN_TT,
                       mk_p.reshape(bp, N_MEM, MEM_WIDTH), mv_p.reshape(bp, N_MEM, MEM_WIDTH), tt=ATTN_TT, bb=1,
                       feature_major=False)
    mk_s = jnp.transpose(cache_mem_k[l].reshape(bs, N_MEM, MEM_WIDTH), (0, 2, 1))
    mv_s = jnp.transpose(cache_mem_v[l].reshape(bs, N_MEM, MEM_WIDTH), (0, 2, 1))
    o_me_s = _mem_attn(zq.reshape(n_all // ts, ts, MEM_WIDTH), n_p // ts, bs, 1, mk_s, mv_s, tt=ts, bb=ATTN_BB,
                       feature_major=True)

    wrt_pad = jnp.zeros((d, LANES), F32).at[:, :N_EXPERTS].set(w_router[l])
    brt_pad = jnp.zeros((1, LANES), F32).at[0, :N_EXPERTS].set(b_router[l])
    x2, h2, route, route_t, counts = _merge(
        x_pair, (o_rw_p.reshape(n_p, RWKV_WIDTH), o_rw_s.reshape(n_s, RWKV_WIDTH)), (o_gm_p, o_gm_s),
        (o_me_p.reshape(n_p, MEM_WIDTH), o_me_s.reshape(n_s, MEM_WIDTH)), sg,
        w_br_rwkv[l].astype(BF16), w_br_gmlp[l].astype(BF16), w_br_mem[l].astype(BF16), w_out[l].astype(BF16),
        row(g_norm_ffn[l]), wrt_pad, brt_pad)

    n_assign = n_all * TOP_K
    n_blocks = -(-(n_assign + N_EXPERTS * (MOE_BM - 1)) // MOE_BM)
    dest, blk_e, n_used, nxt_e, parity, zlist = _slot_tables(route_t, counts[0], n_blocks)
    xs = _dispatch(dest, zlist, h2.reshape(n_all, ROW_SUB, LANES), n_blocks)
    yb = _moe(blk_e, n_used, nxt_e, parity, xs.reshape(n_blocks * MOE_BM * ROW_SUB, LANES), w_exp1[l],
              b_exp1[l].reshape(N_EXPERTS, 1, 2 * D_FF), w_exp2[l], b_exp2[l].reshape(N_EXPERTS, 1, d), n_blocks)
    y_p, y_s = _combine(dest, x2, route, row(g_norm_final), yb, n_p)

    mk_out = mk_p.reshape(1, bp, N_MEM, MEM_HEADS, MEM_WIDTH // MEM_HEADS)
    mv_out = mv_p.reshape(1, bp, N_MEM, MEM_HEADS, MEM_WIDTH // MEM_HEADS)
    return (y_p.reshape(bp, tp, d), y_s.reshape(bs, ts, d), shift_p.reshape(1, bp, SHIFT_WIDTH), s_p[None], mk_out,
            mv_out, shift_s.reshape(1, bs, SHIFT_WIDTH), s_s[None],
            v_rows_s.reshape(1, bs, ts, GMLP_WIDTH))
```

```python
import functools
import math

import jax
import jax.numpy as jnp
from jax import lax
from jax.experimental import pallas as pl
from jax.experimental.pallas import tpu as pltpu

F32 = jnp.float32
BF16 = jnp.bfloat16

D_MODEL = 1024
RWKV_HEADS = 8
HEAD_DIM = 64
RWKV_WIDTH = RWKV_HEADS * HEAD_DIM
DECAY_LORA = 64
A_LORA = 64
GATE_LORA = 128
GMLP_GROUPS = 4
GMLP_WIDTH = 256
CHUNK = 128
MEM_HEADS = 4
MEM_WIDTH = 256
N_MEM = 256
N_EXPERTS = 32
TOP_K = 4
D_FF = 1024
SWIGLU_ALPHA = 1.702
SWIGLU_LIMIT = 7.0
RMS_EPS = 1e-5
LN_EPS = 1e-5
GN_EPS = 64e-5
SHIFT_WIDTH = 3 * RWKV_WIDTH + DECAY_LORA + A_LORA + GATE_LORA
OFF_GMLP = SHIFT_WIDTH
OFF_QMEM = OFF_GMLP + 2 * GMLP_WIDTH
OFF_GATE = OFF_QMEM + MEM_WIDTH
IN_WIDTH = OFF_GATE + 3 * D_MODEL
LANES = 128
ROW_SUB = D_MODEL // LANES

PROJ_TM = 256
MERGE_TM = 512
RWKV_TT = 512
RWKV_C = 64
RWKV_GROUP = 4
RWKV_SAMPLE_SEQS = 8
GMLP_TILE = 4
ATTN_TT = 512
ATTN_BB = 8
MOE_BM = 512
DISPATCH_TM = 256
COMBINE_TM = 128

ROUTE_E = 0
ROUTE_RANK = 4
ROUTE_GATE = 8

NN = ((1,), (0,))
NT = ((1,), (1,))
TN = ((0,), (0,))

VMEM_LIMIT = 56 * 1024 * 1024


def _params(sem, vmem=VMEM_LIMIT):
    return pltpu.CompilerParams(dimension_semantics=sem, vmem_limit_bytes=vmem)


def _bdot(a, b, dims=NN):
    return lax.dot_general(a.astype(BF16), b.astype(BF16), (dims, ((), ())), preferred_element_type=F32)


def _split(x):
    hi = x.astype(BF16)
    lo = (x - hi.astype(F32)).astype(BF16)
    return hi, lo


def _dot3(a, b, dims=NN):
    dn = (dims, ((), ()))
    ah, al = _split(a)
    bh, bl = _split(b)
    r = lax.dot_general(ah, bh, dn, preferred_element_type=F32)
    r = r + lax.dot_general(al, bh, dn, preferred_element_type=F32)
    return r + lax.dot_general(ah, bl, dn, preferred_element_type=F32)


def _rms(x, g):
    return x * lax.rsqrt(jnp.mean(x * x, axis=-1, keepdims=True) + RMS_EPS) * g


def _store_row_tiles(ref, x):
    m = x.shape[0]
    for j in range(ROW_SUB):
        ref[pl.ds(j, m, stride=ROW_SUB), :] = x[:, j * LANES:(j + 1) * LANES]


def _load_row_tiles(ref, m):
    return jnp.concatenate([ref[pl.ds(j, m, stride=ROW_SUB), :] for j in range(ROW_SUB)], axis=1)


def _sigmoid(x):
    return 0.5 * jnp.tanh(0.5 * x) + 0.5


def _in_proj_body(xp_ref, xs_ref, g_ref, w_ref, zr_ref, zg_ref, zq_ref, sg_ref, *, np_tiles):
    x = jnp.where(pl.program_id(0) < np_tiles, xp_ref[...], xs_ref[...])
    h = _rms(x, g_ref[...]).astype(BF16)
    zr_ref[...] = jnp.dot(h, w_ref[:, 0:OFF_GMLP], preferred_element_type=F32)
    zg_ref[...] = jnp.dot(h, w_ref[:, OFF_GMLP:OFF_QMEM], preferred_element_type=F32)
    zq_ref[...] = jnp.dot(h, w_ref[:, OFF_QMEM:OFF_GATE], preferred_element_type=F32).astype(zq_ref.dtype)
    gates = jnp.dot(h, w_ref[:, OFF_GATE:IN_WIDTH], preferred_element_type=F32)
    sg_ref[...] = _sigmoid(gates).astype(BF16)


def _in_proj(x_p, x_s, g, w_bf16):
    n_p = x_p.shape[0]
    n = n_p + x_s.shape[0]
    tm = PROJ_TM
    np_tiles = n_p // tm
    row = lambda i: (i, 0)
    const = lambda i: (0, 0)
    return pl.pallas_call(
        functools.partial(_in_proj_body, np_tiles=np_tiles),
        grid=(n // tm,),
        in_specs=[
            pl.BlockSpec((tm, D_MODEL), lambda i: (jnp.minimum(i, np_tiles - 1), 0)),
            pl.BlockSpec((tm, D_MODEL), lambda i: (jnp.maximum(i - np_tiles, 0), 0)),
            pl.BlockSpec((1, D_MODEL), const),
            pl.BlockSpec((D_MODEL, IN_WIDTH), const, pipeline_mode=pl.Buffered(1)),
        ],
        out_specs=[
            pl.BlockSpec((tm, SHIFT_WIDTH), row),
            pl.BlockSpec((tm, 2 * GMLP_WIDTH), row),
            pl.BlockSpec((tm, MEM_WIDTH), row),
            pl.BlockSpec((tm, 3 * D_MODEL), row),
        ],
        out_shape=[
            jax.ShapeDtypeStruct((n, SHIFT_WIDTH), F32),
            jax.ShapeDtypeStruct((n, 2 * GMLP_WIDTH), F32),
            jax.ShapeDtypeStruct((n, MEM_WIDTH), BF16),
            jax.ShapeDtypeStruct((n, 3 * D_MODEL), BF16),
        ],
        compiler_params=_params(("parallel",)),
        name="in_proj",
    )(x_p, x_s, g, w_bf16)


def _mem_kv_body(x_ref, g_ref, w_ref, k_ref, v_ref):
    h = _rms(x_ref[...], g_ref[...]).astype(BF16)
    k_ref[...] = jnp.dot(h, w_ref[:, 0:MEM_WIDTH], preferred_element_type=F32)
    v_ref[...] = jnp.dot(h, w_ref[:, MEM_WIDTH:2 * MEM_WIDTH], preferred_element_type=F32)


def _mem_kv(mem, g, w_bf16):
    n = mem.shape[0]
    tm = PROJ_TM
    return pl.pallas_call(
        _mem_kv_body,
        grid=(n // tm,),
        in_specs=[
            pl.BlockSpec((tm, D_MODEL), lambda i: (i, 0)),
            pl.BlockSpec((1, D_MODEL), lambda i: (0, 0)),
            pl.BlockSpec((D_MODEL, 2 * MEM_WIDTH), lambda i: (0, 0)),
        ],
        out_specs=[pl.BlockSpec((tm, MEM_WIDTH), lambda i: (i, 0))] * 2,
        out_shape=[jax.ShapeDtypeStruct((n, MEM_WIDTH), F32)] * 2,
        compiler_params=_params(("parallel",)),
        name="mem_kv",
    )(mem, g, w_bf16)


EXP_M05 = math.exp(-0.5)


def _dotp(a, b, dims, passes):
    return _dot3(a, b, dims) if passes == 3 else _bdot(a, b, dims)


PREC = dict(a=1, t=1, lkv=1, tw=1, am=1, gh=1, y=1, s=1)


def _unit_lower_inverse(lows, c):
    rows = lax.broadcasted_iota(jnp.int32, (c, c), 0)
    cols = lax.broadcasted_iota(jnp.int32, (c, c), 1)
    eye = (rows == cols).astype(F32)
    invs = [eye - low for low in lows]
    powers = lows
    for _ in range(int(math.log2(c)) - 1):
        powers = [_dotp(pw, pw, NN, PREC["t"]) for pw in powers]
        invs = [inv + _dotp(inv, pw, NN, PREC["t"]) for inv, pw in zip(invs, powers)]
    return invs


def _rwkv_body(zr_ref, shift_ref, s0_ref, mu_ref, w0_ref, wd_ref, a0_ref, wa_ref, wg_ref, kk_ref, ka_ref,
               rk_ref, lng_ref, lnb_ref, hsum_ref,
               o_ref, sout_ref, shout_ref,
               carry_ref, state_ref, r_s, k_s, v_s, kk_s, b_s, ld_s, y_s, bon_s, g_s,
               rw_s, y0_s, gm_s, h0_s, *, tt, c, nseq):
    i = pl.program_id(1)
    rw = RWKV_WIDTH
    nh = RWKV_HEADS

    @pl.when(i == 0)
    def _():
        for q in range(nseq):
            state_ref[q * nh:(q + 1) * nh] = s0_ref[q]

    z = zr_ref[0]
    z_prev = pltpu.roll(z, 1, 0)
    row = lax.broadcasted_iota(jnp.int32, z.shape, 0)
    if nseq == 1:
        @pl.when(i == 0)
        def _():
            carry_ref[...] = shift_ref[0]

        z_prev = jnp.where(row == 0, carry_ref[...], z_prev)
        carry_ref[...] = z[tt - 1:tt, :]
    else:
        first_rows = jnp.concatenate([jnp.broadcast_to(shift_ref[q], (c, SHIFT_WIDTH)) for q in range(nseq)], axis=0)
        z_prev = jnp.where(row % c == 0, first_rows, z_prev)
    zs = z + mu_ref[...] * (z_prev - z)
    r = zs[:, 0:rw]
    k = zs[:, rw:2 * rw]
    v = zs[:, 2 * rw:3 * rw]
    zw = zs[:, 3 * rw:3 * rw + DECAY_LORA]
    za = zs[:, 3 * rw + DECAY_LORA:3 * rw + DECAY_LORA + A_LORA]
    zg = zs[:, 3 * rw + DECAY_LORA + A_LORA:SHIFT_WIDTH]
    xw = w0_ref[...] + _bdot(jnp.tanh(zw), wd_ref[...])
    ld_s[...] = -EXP_M05 * _sigmoid(xw)
    a = _sigmoid(a0_ref[...] + _bdot(za, wa_ref[...]))
    g_s[...] = _bdot(_sigmoid(zg), wg_ref[...])
    kk = k * kk_ref[...]
    k = k * (1.0 + (a - 1.0) * ka_ref[...])
    r_s[...] = r
    k_s[...] = k
    v_s[...] = v

    def head_sum(t):
        return jnp.dot(t.astype(BF16), hsum_ref[...], preferred_element_type=F32)

    kk = kk / jnp.maximum(jnp.sqrt(head_sum(kk * kk)), 1e-12)
    kk_s[...] = kk
    b_s[...] = kk * a
    bon_s[...] = head_sum(r * k * rk_ref[...]) * v

    rows_c = lax.broadcasted_iota(jnp.int32, (c, c), 0)
    cols_c = lax.broadcasted_iota(jnp.int32, (c, c), 1)
    strict = rows_c > cols_c
    incl = rows_c >= cols_c
    tril_ones = jnp.where(incl, 1.0, 0.0).astype(BF16)
    rows_2c = lax.broadcasted_iota(jnp.int32, (c, 2 * c), 0)
    cols_2c = lax.broadcasted_iota(jnp.int32, (c, 2 * c), 1)
    incl2 = rows_2c >= jnp.where(cols_2c >= c, cols_2c - c, cols_2c)
    rows_k = lax.broadcasted_iota(jnp.int32, (HEAD_DIM, HEAD_DIM), 0)
    cols_k = lax.broadcasted_iota(jnp.int32, (HEAD_DIM, HEAD_DIM), 1)
    eye_k = (rows_k == cols_k).astype(F32)
    zeros_cv = jnp.zeros((c, HEAD_DIM), F32)
    heads = [slice(h * HEAD_DIM, (h + 1) * HEAD_DIM) for h in range(RWKV_HEADS)]

    n_chunks = tt // c
    group = nseq if nseq > 1 else math.gcd(n_chunks, RWKV_GROUP)

    def chunk_rows(ci):
        return pl.ds(ci * c, c) if isinstance(ci, int) else pl.ds(pl.multiple_of(ci * c, c), c)

    def scaled(ci):
        rows = chunk_rows(ci)
        ld = ld_s[rows, :]
        ld_hi, ld_lo = _split(ld)
        cum = (jnp.dot(tril_ones, ld_hi, preferred_element_type=F32)
               + jnp.dot(tril_ones, ld_lo, preferred_element_type=F32))
        e_inc = jnp.exp(cum)
        e_neg = jnp.exp(-cum)
        kt = k_s[rows, :] * e_neg
        bt = b_s[rows, :] * e_neg
        g_end = e_inc[c - 1:c, :]
        return dict(ci=ci, rt=r_s[rows, :] * e_inc, kkt=kk_s[rows, :] * jnp.exp(cum - ld), kt=kt, bt=bt,
                    g_end=g_end, bc=bt * g_end, kc=kt * g_end, vv=v_s[rows, :])

    def phase1(cj, carry):
        chains = [(ch, h, s) for ch in [scaled(cj * group + g) for g in range(group)]
                  for h, s in enumerate(heads)]
        x = [jnp.concatenate([ch["kkt"][:, s], ch["rt"][:, s]], axis=0) for ch, _, s in chains]
        zz = [jnp.concatenate([ch["bt"][:, s], ch["kt"][:, s]], axis=0) for ch, _, s in chains]
        amat = [_dotp(xh, zh, NT, PREC["a"]) for xh, zh in zip(x, zz)]
        l_b = [jnp.where(strict, am[0:c, 0:c], 0.0) for am in amat]
        l_k = [jnp.where(strict, am[0:c, c:2 * c], 0.0) for am in amat]
        a_r = [jnp.where(incl2, am[c:2 * c, :], 0.0) for am in amat]
        tinv = _unit_lower_inverse(l_b, c)
        lkv = [_dotp(lk, ch["vv"][:, s], NN, PREC["lkv"]) for lk, (ch, _, s) in zip(l_k, chains)]
        wu = [-_dotp(t, jnp.concatenate([ch["kkt"][:, s], lv], axis=1), NN, PREC["tw"])
              for t, lv, (ch, _, s) in zip(tinv, lkv, chains)]
        m = [jnp.concatenate([w, jnp.concatenate([zeros_cv, ch["vv"][:, s]], axis=1)], axis=0)
             for w, (ch, _, s) in zip(wu, chains)]
        am2 = [_dotp(ar, mh, NN, PREC["am"]) for ar, mh in zip(a_r, m)]
        gh = [_dotp(mh, jnp.concatenate([ch["bc"][:, s], ch["kc"][:, s]], axis=0), TN, PREC["gh"])
              for mh, (ch, _, s) in zip(m, chains)]
        for q, (ch, h, s) in enumerate(chains):
            idx = ch["ci"] * RWKV_HEADS + h
            rw_s[idx] = ch["rt"][:, s] + am2[q][:, 0:HEAD_DIM]
            y0_s[idx] = am2[q][:, HEAD_DIM:2 * HEAD_DIM]
            gm_s[idx] = gh[q][0:HEAD_DIM, :] + eye_k * ch["g_end"][:, s]
            h0_s[idx] = gh[q][HEAD_DIM:2 * HEAD_DIM, :]
        return carry

    if n_chunks == group:
        phase1(0, 0)
    else:
        lax.fori_loop(0, n_chunks // group, phase1, 0)

    if nseq == 1:
        def phase2(ci, carry):
            rows = chunk_rows(ci)
            for h, s in enumerate(heads):
                idx = ci * nh + h
                s_h = state_ref[h]
                y_s[rows, s] = _dotp(rw_s[idx], s_h, NT, PREC["y"]) + y0_s[idx]
                state_ref[h] = _dotp(s_h, gm_s[idx], NN, PREC["s"]) + h0_s[idx]
            return carry

        lax.fori_loop(0, n_chunks, phase2, 0)
    else:
        pairs = [(q, h, s) for q in range(nseq) for h, s in enumerate(heads)]
        states = [state_ref[q * nh + h] for q, h, _ in pairs]
        ys = [_dotp(rw_s[q * nh + h], st, NT, PREC["y"]) + y0_s[q * nh + h] for (q, h, _), st in zip(pairs, states)]
        new = [_dotp(st, gm_s[q * nh + h], NN, PREC["s"]) + h0_s[q * nh + h] for (q, h, _), st in zip(pairs, states)]
        for (q, h, s), yq, nq in zip(pairs, ys, new):
            y_s[chunk_rows(q), s] = yq
            state_ref[q * nh + h] = nq
    y = y_s[...]
    yc = y - head_sum(y) * (1.0 / HEAD_DIM)
    var = head_sum(yc * yc) * (1.0 / HEAD_DIM)
    yn = yc * lax.rsqrt(var + GN_EPS)
    o_ref[0] = ((yn * lng_ref[...] + lnb_ref[...] + bon_s[...]) * g_s[...]).astype(o_ref.dtype)

    @pl.when(i == pl.num_programs(1) - 1)
    def _():
        for q in range(nseq):
            sout_ref[q] = state_ref[q * nh:(q + 1) * nh]
            last = tt - 1 if nseq == 1 else (q + 1) * c - 1
            shout_ref[q] = zr_ref[0, last:last + 1, :]


def _rwkv(zr3, blk_off, b, nt, shift_prev, s_prev, p, tt, c, nseq):
    vec = lambda n: pl.BlockSpec((1, n), lambda bi, ti: (0, 0))
    mat = lambda m, n: pl.BlockSpec((m, n), lambda bi, ti: (0, 0))
    rw = RWKV_WIDTH
    scr = lambda: pltpu.VMEM((tt, rw), F32)
    nch = (tt // c) * RWKV_HEADS
    return pl.pallas_call(
        functools.partial(_rwkv_body, tt=tt, c=c, nseq=nseq),
        grid=(b // nseq, nt),
        in_specs=[
            pl.BlockSpec((1, tt, SHIFT_WIDTH), lambda bi, ti: (blk_off + bi * nt + ti, 0, 0)),
            pl.BlockSpec((nseq, 1, SHIFT_WIDTH), lambda bi, ti: (bi, 0, 0)),
            pl.BlockSpec((nseq, RWKV_HEADS, HEAD_DIM, HEAD_DIM), lambda bi, ti: (bi, 0, 0, 0)),
            vec(SHIFT_WIDTH), vec(rw), mat(DECAY_LORA, rw), vec(rw), mat(A_LORA, rw), mat(GATE_LORA, rw),
            vec(rw), vec(rw), vec(rw), vec(rw), vec(rw), mat(rw, rw),
        ],
        out_specs=[
            pl.BlockSpec((1, tt, rw), lambda bi, ti: (bi, ti, 0)),
            pl.BlockSpec((nseq, RWKV_HEADS, HEAD_DIM, HEAD_DIM), lambda bi, ti: (bi, 0, 0, 0)),
            pl.BlockSpec((nseq, 1, SHIFT_WIDTH), lambda bi, ti: (bi, 0, 0)),
        ],
        out_shape=[
            jax.ShapeDtypeStruct((b // nseq, nt * tt, rw), BF16),
            jax.ShapeDtypeStruct((b, RWKV_HEADS, HEAD_DIM, HEAD_DIM), F32),
            jax.ShapeDtypeStruct((b, 1, SHIFT_WIDTH), F32),
        ],
        scratch_shapes=[
            pltpu.VMEM((1, SHIFT_WIDTH), F32),
            pltpu.VMEM((nseq * RWKV_HEADS, HEAD_DIM, HEAD_DIM), F32),
            scr(), scr(), scr(), scr(), scr(), scr(), scr(), scr(), scr(),
            pltpu.VMEM((nch, c, HEAD_DIM), F32),
            pltpu.VMEM((nch, c, HEAD_DIM), F32),
            pltpu.VMEM((nch, HEAD_DIM, HEAD_DIM), F32),
            pltpu.VMEM((nch, HEAD_DIM, HEAD_DIM), F32),
        ],
        compiler_params=_params(("parallel", "arbitrary")),
        name="rwkv",
    )(zr3, shift_prev, s_prev, p["mu"], p["w0"], p["wd"], p["a0"], p["wa"], p["wg"], p["kk"], p["ka"],
      p["rk"], p["lng"], p["lnb"], p["hsum"])


def _gmlp_body(z_ref, lng_ref, lnb_ref, ws_ref, bs_ref, o_ref, v_ref):
    z = z_ref[...]
    ge = 0.5 * z * (1.0 + lax.erf(z * (1.0 / math.sqrt(2.0))))
    u = ge[:, 0:GMLP_WIDTH]
    v = ge[:, GMLP_WIDTH:2 * GMLP_WIDTH]
    mean = jnp.mean(v, axis=-1, keepdims=True)
    vc = v - mean
    var = jnp.mean(vc * vc, axis=-1, keepdims=True)
    vn = vc * lax.rsqrt(var + LN_EPS) * lng_ref[...] + lnb_ref[...]
    v_ref[...] = vn
    gd = GMLP_WIDTH // GMLP_GROUPS
    for q in range(GMLP_TILE):
        rows = slice(q * CHUNK, (q + 1) * CHUNK)
        for g in range(GMLP_GROUPS):
            gs = slice(g * gd, (g + 1) * gd)
            sv = jnp.dot(ws_ref[g], vn[rows, gs].astype(BF16), preferred_element_type=F32) + bs_ref[:, g:g + 1]
            o_ref[rows, gs] = (u[rows, gs] * sv).astype(o_ref.dtype)


def _gmlp(zg, tile_off, n_rows, lng, lnb, ws_bf16, bs):
    tile = GMLP_TILE * CHUNK
    return pl.pallas_call(
        _gmlp_body,
        grid=(n_rows // tile,),
        in_specs=[
            pl.BlockSpec((tile, 2 * GMLP_WIDTH), lambda i: (tile_off + i, 0)),
            pl.BlockSpec((1, GMLP_WIDTH), lambda i: (0, 0)),
            pl.BlockSpec((1, GMLP_WIDTH), lambda i: (0, 0)),
            pl.BlockSpec((GMLP_GROUPS, CHUNK, CHUNK), lambda i: (0, 0, 0)),
            pl.BlockSpec((CHUNK, GMLP_GROUPS), lambda i: (0, 0)),
        ],
        out_specs=[
            pl.BlockSpec((tile, GMLP_WIDTH), lambda i: (i, 0)),
            pl.BlockSpec((tile, GMLP_WIDTH), lambda i: (i, 0)),
        ],
        out_shape=[
            jax.ShapeDtypeStruct((n_rows, GMLP_WIDTH), BF16),
            jax.ShapeDtypeStruct((n_rows, GMLP_WIDTH), F32),
        ],
        compiler_params=_params(("parallel",)),
        name="gmlp",
    )(zg, lng, lnb, ws_bf16, bs)


def _mem_attn_body(q_ref, k_ref, v_ref, o_ref, *, bb, feature_major):
    hd = MEM_WIDTH // MEM_HEADS
    pairs = [(s, slice(h * hd, (h + 1) * hd)) for s in range(bb) for h in range(MEM_HEADS)]
    if feature_major:
        scores = [_bdot(q_ref[s, :, hs], k_ref[s, hs, :], NN) * (hd ** -0.5) for s, hs in pairs]
    else:
        scores = [_bdot(q_ref[s, :, hs], k_ref[s, :, hs], NT) * (hd ** -0.5) for s, hs in pairs]
    probs = []
    for sc in scores:
        pr = jnp.exp(sc - jnp.max(sc, axis=-1, keepdims=True))
        probs.append(pr / jnp.sum(pr, axis=-1, keepdims=True))
    for (s, hs), pr in zip(pairs, probs):
        if feature_major:
            o_ref[s, :, hs] = _bdot(pr, v_ref[s, hs, :], NT).astype(o_ref.dtype)
        else:
            o_ref[s, :, hs] = _bdot(pr, v_ref[s, :, hs]).astype(o_ref.dtype)


def _mem_attn(q3, blk_off, b, nt, mk, mv, tt, bb, feature_major):
    return pl.pallas_call(
        functools.partial(_mem_attn_body, bb=bb, feature_major=feature_major),
        grid=(b // bb, nt),
        in_specs=[
            pl.BlockSpec((bb, tt, MEM_WIDTH), lambda bi, ti: (blk_off // bb + bi * nt + ti, 0, 0)),
            pl.BlockSpec((bb, N_MEM, MEM_WIDTH), lambda bi, ti: (bi, 0, 0)),
            pl.BlockSpec((bb, N_MEM, MEM_WIDTH), lambda bi, ti: (bi, 0, 0)),
        ],
        out_specs=pl.BlockSpec((bb, tt, MEM_WIDTH), lambda bi, ti: (bi, ti, 0)),
        out_shape=jax.ShapeDtypeStruct((b, nt * tt, MEM_WIDTH), BF16),
        compiler_params=_params(("parallel", "parallel")),
        name="mem_attn",
    )(q3, mk, mv)


def _merge_body(x_p, x_s, orw_p, orw_s, ogm_p, ogm_s, ome_p, ome_s, sg_ref, wbr_ref, wbg_ref, wbm_ref, wout_ref,
                gffn_ref, wrt_ref, brt_ref, x2_ref, h2_ref, route_ref, route_t_ref, cnt_ref, count_s, *, np_tiles):
    i = pl.program_id(0)
    d = D_MODEL
    tm = x_p.shape[0]
    is_p = i < np_tiles
    x = jnp.where(is_p, x_p[...], x_s[...])
    orw = jnp.where(is_p, orw_p[...], orw_s[...])
    ogm = jnp.where(is_p, ogm_p[...], ogm_s[...])
    ome = jnp.where(is_p, ome_p[...], ome_s[...])
    merged = sg_ref[:, 0:d].astype(F32) * _bdot(orw, wbr_ref[...])
    merged = merged + sg_ref[:, d:2 * d].astype(F32) * _bdot(ogm, wbg_ref[...])
    merged = merged + sg_ref[:, 2 * d:3 * d].astype(F32) * _bdot(ome, wbm_ref[...])
    x2 = x + _bdot(merged, wout_ref[...])
    x2_ref[...] = x2
    h2 = _rms(x2, gffn_ref[...])
    _store_row_tiles(h2_ref, h2)

    @pl.when(i == 0)
    def _():
        count_s[...] = jnp.zeros_like(count_s)

    lane = lax.broadcasted_iota(jnp.int32, (tm, LANES), 1)
    logits = jnp.where(lane < N_EXPERTS, _dot3(h2, wrt_ref[...]) + brt_ref[...], -jnp.inf)
    lane_f = lane.astype(F32)
    tops, hots, idxs = [], [], []
    for _ in range(TOP_K):
        top = jnp.max(logits, axis=-1, keepdims=True)
        idx = jnp.min(jnp.where(logits == top, lane_f, float(LANES)), axis=-1, keepdims=True)
        hot = lane_f == idx
        logits = jnp.where(hot, -jnp.inf, logits)
        tops.append(top)
        hots.append(hot)
        idxs.append(idx)
    weights = [jnp.exp(t - tops[0]) for t in tops]
    denom = weights[0] + weights[1] + weights[2] + weights[3]
    onehot = jnp.zeros((tm, LANES), F32)
    for hot in hots:
        onehot = onehot + jnp.where(hot, 1.0, 0.0)
    rows_t = lax.broadcasted_iota(jnp.int32, (tm, tm), 0)
    cols_t = lax.broadcasted_iota(jnp.int32, (tm, tm), 1)
    ahead = jnp.where(rows_t > cols_t, 1.0, 0.0).astype(BF16)
    prefix = jnp.dot(ahead, onehot.astype(BF16), preferred_element_type=F32) + count_s[...]
    route = jnp.zeros((tm, LANES), F32)
    for kk in range(TOP_K):
        rank = jnp.sum(jnp.where(hots[kk], prefix, 0.0), axis=-1, keepdims=True)
        route = jnp.where(lane == ROUTE_E + kk, idxs[kk], route)
        route = jnp.where(lane == ROUTE_RANK + kk, rank, route)
        route = jnp.where(lane == ROUTE_GATE + kk, weights[kk] / denom, route)
    route_ref[...] = route
    route_t_ref[...] = route.T
    count_s[...] = count_s[...] + jnp.sum(onehot, axis=0, keepdims=True)
    cnt_ref[...] = jnp.broadcast_to(count_s[...], cnt_ref.shape)


def _merge(x, o_rw, o_gm, o_me, sg, wbr, wbg, wbm, wout, gffn, wrt_pad, brt_pad):
    n_p = x[0].shape[0]
    n = n_p + x[1].shape[0]
    tm = MERGE_TM
    d = D_MODEL
    np_tiles = n_p // tm
    row = lambda i: (i, 0)
    const = lambda i: (0, 0)
    first =lambda i: (jnp.minimum(i, np_tiles - 1), 0)
    second = lambda i: (jnp.maximum(i - np_tiles, 0), 0)

    def pair(width):
        return [pl.BlockSpec((tm, width), first), pl.BlockSpec((tm, width), second)]

    return pl.pallas_call(
        functools.partial(_merge_body, np_tiles=np_tiles),
        grid=(n // tm,),
        in_specs=pair(d) + pair(RWKV_WIDTH) + pair(GMLP_WIDTH) + pair(MEM_WIDTH) + [
            pl.BlockSpec((tm, 3 * d), row),
            pl.BlockSpec((RWKV_WIDTH, d), const),
            pl.BlockSpec((GMLP_WIDTH, d), const),
            pl.BlockSpec((MEM_WIDTH, d), const),
            pl.BlockSpec((d, d), const),
            pl.BlockSpec((1, d), const),
            pl.BlockSpec((d, LANES), const),
            pl.BlockSpec((1, LANES), const),
        ],
        out_specs=[
            pl.BlockSpec((tm, d), row),
            pl.BlockSpec((tm * ROW_SUB, LANES), row),
            pl.BlockSpec((tm, LANES), row),
            pl.BlockSpec((LANES, tm), lambda i: (0, i)),
            pl.BlockSpec((8, LANES), const),
        ],
        out_shape=[
            jax.ShapeDtypeStruct((n, d), F32),
            jax.ShapeDtypeStruct((n * ROW_SUB, LANES), F32),
            jax.ShapeDtypeStruct((n, LANES), F32),
            jax.ShapeDtypeStruct((LANES, n), F32),
            jax.ShapeDtypeStruct((8, LANES), F32),
        ],
        scratch_shapes=[pltpu.VMEM((1, LANES), F32)],
        compiler_params=_params(("arbitrary",)),
        name="merge",
    )(x[0], x[1], o_rw[0], o_rw[1], o_gm[0], o_gm[1], o_me[0], o_me[1], sg, wbr, wbg, wbm, wout, gffn, wrt_pad,
      brt_pad)


N_ZERO_BLOCKS = 2 * N_EXPERTS


def _dispatch_body(dest_ref, zlist_ref, h_ref, xs_hbm, zero_s, sem_z, sem):
    i = pl.program_id(0)
    tm = DISPATCH_TM
    bm = MOE_BM

    @pl.when(i == 0)
    def _():
        zero_s[...] = jnp.zeros_like(zero_s)

        def zero_copy(q):
            start = pl.multiple_of(zlist_ref[q] * bm, bm)
            return pltpu.make_async_copy(zero_s, xs_hbm.at[pl.ds(start, bm)], sem_z)

        def start(q, carry):
            @pl.when(zlist_ref[q] >= 0)
            def _():
                zero_copy(q).start()
            return carry

        def wait(q, carry):
            @pl.when(zlist_ref[q] >= 0)
            def _():
                zero_copy(q).wait()
            return carry

        lax.fori_loop(0, N_ZERO_BLOCKS, start, 0)
        lax.fori_loop(0, N_ZERO_BLOCKS, wait, 0)

    base = i * tm
    n_tok = pl.num_programs(0) * tm

    def body(r, carry):
        for kk in range(TOP_K):
            slot = dest_ref[kk * n_tok + base + r]
            pltpu.make_async_copy(h_ref.at[r], xs_hbm.at[slot], sem).start(priority=kk % 2)
        return carry

    lax.fori_loop(0, tm, body, 0, unroll=2)
    for kk in range(TOP_K):
        pltpu.make_async_copy(h_ref, xs_hbm.at[pl.ds(0, tm)], sem).wait()


def _dispatch(dest, zlist, h, n_blocks):
    n = h.shape[0]
    tm = DISPATCH_TM
    grid_spec = pltpu.PrefetchScalarGridSpec(
        num_scalar_prefetch=2,
        grid=(n // tm,),
        in_specs=[pl.BlockSpec((tm, ROW_SUB, LANES), lambda i, dest, zl: (i, 0, 0))],
        out_specs=pl.BlockSpec(memory_space=pl.ANY),
        scratch_shapes=[
            pltpu.VMEM((MOE_BM, ROW_SUB, LANES), F32),
            pltpu.SemaphoreType.DMA(()),
            pltpu.SemaphoreType.DMA(()),
        ],
    )
    return pl.pallas_call(
        _dispatch_body,
        grid_spec=grid_spec,
        out_shape=jax.ShapeDtypeStruct((n_blocks * MOE_BM, ROW_SUB, LANES), F32),
        compiler_params=_params(("arbitrary",)),
        name="moe_dispatch",
    )(dest, zlist, h)


CAST_ROWS = 64


def _moe_body(be_ref, nused_ref, nxt_ref, par_ref, x_ref, w1_hbm, b1_ref, w2_hbm, b2_ref, o_ref,
              w1f, w2f, w1b, w2b, sem):
    j = pl.program_id(0)
    n_used = nused_ref[0]

    def weight_copies(expert, slot):
        return (pltpu.make_async_copy(w1_hbm.at[expert], w1f.at[slot], sem.at[slot, 0]),
                pltpu.make_async_copy(w2_hbm.at[expert], w2f.at[slot], sem.at[slot, 1]))

    @pl.when(j == 0)
    def _():
        for cp in weight_copies(be_ref[0], 0):
            cp.start()

    @pl.when(j < n_used)
    def _():
        slot = par_ref[be_ref[j]]
        changed = jnp.logical_or(j == 0, be_ref[j] != be_ref[jnp.maximum(j - 1, 0)])

        @pl.when(changed)
        def _():
            for cp in weight_copies(be_ref[j], slot):
                cp.wait()

            nxt = nxt_ref[be_ref[j]]

            @pl.when(nxt >= 0)
            def _():
                for cp in weight_copies(nxt, 1 - slot):
                    cp.start()

            def cast1(q, carry):
                rows = pl.ds(pl.multiple_of(q * CAST_ROWS, CAST_ROWS), CAST_ROWS)
                w1b[rows, :] = w1f[slot, rows, :].astype(BF16)
                return carry

            def cast2(q, carry):
                rows = pl.ds(pl.multiple_of(q * CAST_ROWS, CAST_ROWS), CAST_ROWS)
                w2b[rows, :] = w2f[slot, rows, :].astype(BF16)
                return carry

            lax.fori_loop(0, D_MODEL // CAST_ROWS, cast1, 0)
            lax.fori_loop(0, D_FF // CAST_ROWS, cast2, 0)

        x = _load_row_tiles(x_ref, MOE_BM).astype(BF16)
        z = jnp.dot(x, w1b[...], preferred_element_type=F32) + b1_ref[0]
        zg = jnp.minimum(z[:, 0:D_FF], SWIGLU_LIMIT)
        zl = jnp.clip(z[:, D_FF:2 * D_FF], -SWIGLU_LIMIT, SWIGLU_LIMIT)
        act = zg * _sigmoid(SWIGLU_ALPHA * zg) * (zl + 1.0)
        _store_row_tiles(o_ref, jnp.dot(act.astype(BF16), w2b[...], preferred_element_type=F32) + b2_ref[0])

    @pl.when(j >= n_used)
    def _():
        o_ref[...] = jnp.zeros_like(o_ref)


def _moe(blk_e, n_used, nxt_e, parity, xs, w1, b1, w2, b2, n_blocks):
    bm = MOE_BM
    d = D_MODEL
    grid_spec = pltpu.PrefetchScalarGridSpec(
        num_scalar_prefetch=4,
        grid=(n_blocks,),
        in_specs=[
            pl.BlockSpec((bm * ROW_SUB, LANES), lambda j, be, nu, nx, pa: (j, 0)),
            pl.BlockSpec(memory_space=pl.ANY),
            pl.BlockSpec((1, 1, 2 * D_FF), lambda j, be, nu, nx, pa: (be[j], 0, 0)),
            pl.BlockSpec(memory_space=pl.ANY),
            pl.BlockSpec((1, 1, d), lambda j, be, nu, nx, pa: (be[j], 0, 0)),
        ],
        out_specs=pl.BlockSpec((bm * ROW_SUB, LANES), lambda j, be, nu, nx, pa: (j, 0)),
        scratch_shapes=[
            pltpu.VMEM((2, d, 2 * D_FF), F32),
            pltpu.VMEM((2, D_FF, d), F32),
            pltpu.VMEM((d, 2 * D_FF), BF16),
            pltpu.VMEM((D_FF, d), BF16),
            pltpu.SemaphoreType.DMA((2, 2)),
        ],
    )
    return pl.pallas_call(
        _moe_body,
        grid_spec=grid_spec,
        out_shape=jax.ShapeDtypeStruct((n_blocks * bm * ROW_SUB, LANES), F32),
        compiler_params=_params(("arbitrary",)),
        name="moe_ffn",
    )(blk_e, n_used, nxt_e, parity, xs, w1, b1, w2, b2)


def _combine_body(dest_ref, x2_ref, route_ref, gfin_ref, yb_hbm, op_ref, os_ref, buf0, buf1, sem, *, np_steps):
    i = pl.program_id(0)
    n_steps = pl.num_programs(0)
    tm = COMBINE_TM
    n_tok = n_steps * (2 * tm)
    bufs = (buf0, buf1)

    def issue(tile, which):
        base = tile * tm
        for r in range(tm):
            for kk in range(TOP_K):
                slot = dest_ref[kk * n_tok + base + r]
                src = pl.ds(pl.multiple_of(slot * ROW_SUB, ROW_SUB), ROW_SUB)
                pltpu.make_async_copy(yb_hbm.at[src], bufs[which].at[kk, r * ROW_SUB:(r + 1) * ROW_SUB],
                                      sem.at[which]).start(priority=kk % 2)

    def wait(which):
        for kk in range(TOP_K):
            pltpu.make_async_copy(yb_hbm.at[pl.ds(0, tm * ROW_SUB)], bufs[which].at[kk], sem.at[which]).wait()

    def reduce(which, half):
        rows = slice(half * tm, (half + 1) * tm)
        acc = x2_ref[rows, :]
        for kk in range(TOP_K):
            gate = route_ref[rows, ROUTE_GATE + kk:ROUTE_GATE + kk + 1]
            acc = acc + gate * _load_row_tiles(bufs[which].at[kk], tm)
        return _rms(acc, gfin_ref[...])

    @pl.when(i == 0)
    def _():
        issue(0, 0)

    wait(0)
    issue(2 * i + 1, 1)
    y0 = reduce(0, 0)
    wait(1)
    issue(jnp.minimum(2 * i + 2, 2 * n_steps - 1), 0)
    y1 = reduce(1, 1)

    @pl.when(i < np_steps)
    def _():
        op_ref[0:tm, :] = y0
        op_ref[tm:2 * tm, :] = y1

    @pl.when(i >= np_steps)
    def _():
        os_ref[0:tm, :] = y0
        os_ref[tm:2 * tm, :] = y1

    @pl.when(i == n_steps - 1)
    def _():
        wait(0)


def _combine(dest, x2, route, gfin, yb, n_p):
    n, d = x2.shape
    tm = COMBINE_TM
    step = 2 * tm
    np_steps = n_p // step
    grid_spec = pltpu.PrefetchScalarGridSpec(
        num_scalar_prefetch=1,
        grid=(n // step,),
        in_specs=[
            pl.BlockSpec((step, d), lambda i, dest: (i, 0)),
            pl.BlockSpec((step, LANES), lambda i, dest: (i, 0)),
            pl.BlockSpec((1, d), lambda i, dest: (0, 0)),
            pl.BlockSpec(memory_space=pl.ANY),
        ],
        out_specs=[
            pl.BlockSpec((step, d), lambda i, dest: (jnp.minimum(i, np_steps - 1), 0)),
            pl.BlockSpec((step, d), lambda i, dest: (jnp.maximum(i - np_steps, 0), 0)),
        ],
        scratch_shapes=[
            pltpu.VMEM((TOP_K, tm * ROW_SUB, LANES), F32),
            pltpu.VMEM((TOP_K, tm * ROW_SUB, LANES), F32),
            pltpu.SemaphoreType.DMA((2,)),
        ],
    )
    return pl.pallas_call(
        functools.partial(_combine_body, np_steps=np_steps),
        grid_spec=grid_spec,
        out_shape=[jax.ShapeDtypeStruct((n_p, d), F32), jax.ShapeDtypeStruct((n - n_p, d), F32)],
        compiler_params=_params(("arbitrary",)),
        name="moe_combine",
    )(dest, x2, route, gfin, yb)


def _slot_tables(route_t, counts_row, n_blocks):
    bm = MOE_BM
    e = route_t[ROUTE_E:ROUTE_E + TOP_K].astype(jnp.int32)
    rank = route_t[ROUTE_RANK:ROUTE_RANK + TOP_K].astype(jnp.int32)
    counts = counts_row[:N_EXPERTS].astype(jnp.int32)
    padded = (counts + bm - 1) // bm * bm
    pad_end = jnp.cumsum(padded)
    pad_start = pad_end - padded
    experts = jnp.arange(N_EXPERTS, dtype=jnp.int32)[:, None, None]
    dest = (jnp.sum(jnp.where(e[None] == experts, pad_start[:, None, None], 0), axis=0) + rank).reshape(-1)
    blk_start = jnp.arange(n_blocks, dtype=jnp.int32) * bm
    blk_e = jnp.minimum(jnp.sum(pad_end[None, :] <= blk_start[:, None], axis=1), N_EXPERTS - 1).astype(jnp.int32)
    n_used = pad_end[-1] // bm
    last_blk = jnp.where(padded > 0, pad_end // bm - 1, -1)
    trailing = n_used + jnp.arange(N_EXPERTS, dtype=jnp.int32)
    trailing = jnp.where(trailing < n_blocks, trailing, -1)
    zlist = jnp.concatenate([last_blk, trailing]).astype(jnp.int32)
    ids = jnp.arange(N_EXPERTS, dtype=jnp.int32)
    present = counts > 0
    later = jnp.where(jnp.logical_and(ids[None, :] > ids[:, None], present[None, :]), ids[None, :], N_EXPERTS)
    next_present = jnp.min(later, axis=1)
    next_present = jnp.where(next_present < N_EXPERTS, next_present, -1)
    nxt_e = next_present.astype(jnp.int32)
    parity = ((jnp.cumsum(present.astype(jnp.int32)) - 1) % 2).astype(jnp.int32)
    return dest.astype(jnp.int32), blk_e, n_used.astype(jnp.int32).reshape(1), nxt_e, parity, zlist


def kernel(x_prompt, x_sample, mem_prompt, state_shift, state_wkv, cache_mem_k, cache_mem_v, g_norm_mix, w_in, mu_shift, w0, w_decay_up, a0, w_a_up, w_g_up, k_k, k_a, r_k, ln_x_g, ln_x_b, gmlp_ln_g, gmlp_ln_b, w_spatial, b_spatial, g_norm_mem, w_mem_kv, w_br_rwkv, w_br_gmlp, w_br_mem, w_out, g_norm_ffn, w_router, b_router, w_exp1, b_exp1, w_exp2, b_exp2, g_norm_final):
    bp, tp, d = x_prompt.shape
    bs, ts, _ = x_sample.shape
    n_p, n_s = bp * tp, bs * ts
    n_all = n_p + n_s
    l = 0
    row = lambda a: a.reshape(1, -1)

    x_pair = (x_prompt.reshape(n_p, d), x_sample.reshape(n_s, d))
    zr, zg, zq, sg = _in_proj(x_pair[0], x_pair[1], row(g_norm_mix[l]), w_in[l].astype(BF16))

    mk_p, mv_p = _mem_kv(mem_prompt.reshape(bp * N_MEM, d), row(g_norm_mem[l]), w_mem_kv[l].astype(BF16))

    rp = dict(mu=row(mu_shift[l]), w0=row(w0[l]), wd=w_decay_up[l].astype(BF16), a0=row(a0[l]),
              wa=w_a_up[l].astype(BF16), wg=w_g_up[l].astype(BF16), kk=row(k_k[l]), ka=row(k_a[l]),
              rk=row(r_k[l]), lng=row(ln_x_g[l]), lnb=row(ln_x_b[l]))
    head_of = jnp.arange(RWKV_WIDTH, dtype=jnp.int32) // HEAD_DIM
    rp["hsum"] = (head_of[:, None] == head_of[None, :]).astype(BF16)
    o_rw_p, s_p, shift_p = _rwkv(zr.reshape(n_all // RWKV_TT, RWKV_TT, SHIFT_WIDTH), 0, bp, tp // RWKV_TT,
                        jnp.zeros((bp, 1, SHIFT_WIDTH), F32), jnp.zeros((bp, RWKV_HEADS, HEAD_DIM, HEAD_DIM), F32),
                        rp, tt=RWKV_TT, c=RWKV_C, nseq=1)
    tile_s = RWKV_SAMPLE_SEQS * ts
    o_rw_s, s_s, shift_s = _rwkv(zr.reshape(n_all // tile_s, tile_s, SHIFT_WIDTH), n_p // tile_s, bs, 1,
                        state_shift[l].reshape(bs, 1, SHIFT_WIDTH), state_wkv[l], rp, tt=tile_s, c=ts,
                        nseq=RWKV_SAMPLE_SEQS)

    tri = jnp.tril(jnp.ones((CHUNK, CHUNK), bool))
    ws_p = jnp.where(tri, w_spatial[l], 0.0).astype(BF16)
    bs_p = b_spatial[l].T
    reps = CHUNK // ts
    tri_s = jnp.tril(jnp.ones((ts, ts), bool))
    ws_small = jnp.where(tri_s, w_spatial[l][:, :ts, :ts], 0.0)
    eye = jnp.eye(reps, dtype=F32)
    ws_s = jnp.einsum("ab,gij->gaibj", eye, ws_small).reshape(GMLP_GROUPS, CHUNK, CHUNK).astype(BF16)
    bs_s = jnp.tile(b_spatial[l][:, :ts], (1, reps)).T
    lng, lnb = row(gmlp_ln_g[l]), row(gmlp_ln_b[l])
    o_gm_p, _ = _gmlp(zg, 0, n_p, lng, lnb, ws_p, bs_p)
    o_gm_s, v_rows_s = _gmlp(zg, n_p // (GMLP_TILE * CHUNK), n_s, lng, lnb, ws_s, bs_s)

    o_me_p = _mem_attn(zq.reshape(n_all // ATTN_TT, ATTN_TT, MEM_WIDTH), 0, bp, tp // ATTN_TT,
                       mk_p.reshape(bp, N_MEM, MEM_WIDTH), mv_p.reshape(bp, N_MEM, MEM_WIDTH), tt=ATTN_TT, bb=1,
                       feature_major=False)
    mk_s = jnp.transpose(cache_mem_k[l].reshape(bs, N_MEM, MEM_WIDTH), (0, 2, 1))
    mv_s = jnp.transpose(cache_mem_v[l].reshape(bs, N_MEM, MEM_WIDTH), (0, 2, 1))
    o_me_s = _mem_attn(zq.reshape(n_all // ts, ts, MEM_WIDTH), n_p // ts, bs, 1, mk_s, mv_s, tt=ts, bb=ATTN_BB,
                       feature_major=True)

    wrt_pad = jnp.zeros((d, LANES), F32).at[:, :N_EXPERTS].set(w_router[l])
    brt_pad = jnp.zeros((1, LANES), F32).at[0, :N_EXPERTS].set(b_router[l])
    x2, h2, route, route_t, counts = _merge(
        x_pair, (o_rw_p.reshape(n_p, RWKV_WIDTH), o_rw_s.reshape(n_s, RWKV_WIDTH)), (o_gm_p, o_gm_s),
        (o_me_p.reshape(n_p, MEM_WIDTH), o_me_s.reshape(n_s, MEM_WIDTH)), sg,
        w_br_rwkv[l].astype(BF16), w_br_gmlp[l].astype(BF16), w_br_mem[l].astype(BF16), w_out[l].astype(BF16),
        row(g_norm_ffn[l]), wrt_pad, brt_pad)

    n_assign = n_all * TOP_K
    n_blocks = -(-(n_assign + N_EXPERTS * (MOE_BM - 1)) // MOE_BM)
    dest, blk_e, n_used, nxt_e, parity, zlist = _slot_tables(route_t, counts[0], n_blocks)
    xs = _dispatch(dest, zlist, h2.reshape(n_all, ROW_SUB, LANES), n_blocks)
    yb = _moe(blk_e, n_used, nxt_e, parity, xs.reshape(n_blocks * MOE_BM * ROW_SUB, LANES), w_exp1[l],
              b_exp1[l].reshape(N_EXPERTS, 1, 2 * D_FF), w_exp2[l], b_exp2[l].reshape(N_EXPERTS, 1, d), n_blocks)
    y_p, y_s = _combine(dest, x2, route, row(g_norm_final), yb, n_p)

    mk_out = mk_p.reshape(1, bp, N_MEM, MEM_HEADS, MEM_WIDTH // MEM_HEADS)
    mv_out = mv_p.reshape(1, bp, N_MEM, MEM_HEADS, MEM_WIDTH // MEM_HEADS)
    return (y_p.reshape(bp, tp, d), y_s.reshape(bs, ts, d), shift_p.reshape(1, bp, SHIFT_WIDTH), s_p[None], mk_out,
            mv_out, shift_s.reshape(1, bs, SHIFT_WIDTH), s_s[None],
            v_rows_s.reshape(1, bs, ts, GMLP_WIDTH))
```

```python
import functools
import math

import jax
import jax.numpy as jnp
from jax import lax
from jax.experimental import pallas as pl
from jax.experimental.pallas import tpu as pltpu

F32 = jnp.float32
BF16 = jnp.bfloat16

D_MODEL = 1024
RWKV_HEADS = 8
HEAD_DIM = 64
RWKV_WIDTH = RWKV_HEADS * HEAD_DIM
DECAY_LORA = 64
A_LORA = 64
GATE_LORA = 128
GMLP_GROUPS = 4
GMLP_WIDTH = 256
CHUNK = 128
MEM_HEADS = 4
MEM_WIDTH = 256
N_MEM = 256
N_EXPERTS = 32
TOP_K = 4
D_FF = 1024
SWIGLU_ALPHA = 1.702
SWIGLU_LIMIT = 7.0
RMS_EPS = 1e-5
LN_EPS = 1e-5
GN_EPS = 64e-5
SHIFT_WIDTH = 3 * RWKV_WIDTH + DECAY_LORA + A_LORA + GATE_LORA
OFF_GMLP = SHIFT_WIDTH
OFF_QMEM = OFF_GMLP + 2 * GMLP_WIDTH
OFF_GATE = OFF_QMEM + MEM_WIDTH
IN_WIDTH = OFF_GATE + 3 * D_MODEL
LANES = 128
ROW_SUB = D_MODEL // LANES

PROJ_TM = 256
MERGE_TM = 512
RWKV_TT = 512
RWKV_C = 64
RWKV_GROUP = 8
RWKV_SAMPLE_SEQS = 8
GMLP_TILE = 4
ATTN_TT = 512
ATTN_BB = 8
MOE_BM = 512
DISPATCH_TM = 256
COMBINE_TM = 128

ROUTE_E = 0
ROUTE_RANK = 4
ROUTE_GATE = 8

NN = ((1,), (0,))
NT = ((1,), (1,))
TN = ((0,), (0,))

VMEM_LIMIT = 56 * 1024 * 1024


def _params(sem, vmem=VMEM_LIMIT):
    return pltpu.CompilerParams(dimension_semantics=sem, vmem_limit_bytes=vmem)


def _bdot(a, b, dims=NN):
    return lax.dot_general(a.astype(BF16), b.astype(BF16), (dims, ((), ())), preferred_element_type=F32)


def _split(x):
    hi = x.astype(BF16)
    lo = (x - hi.astype(F32)).astype(BF16)
    return hi, lo


def _dot3(a, b, dims=NN):
    dn = (dims, ((), ()))
    ah, al = _split(a)
    bh, bl = _split(b)
    r = lax.dot_general(ah, bh, dn, preferred_element_type=F32)
    r = r + lax.dot_general(al, bh, dn, preferred_element_type=F32)
    return r + lax.dot_general(ah, bl, dn, preferred_element_type=F32)


def _rms(x, g):
    return x * lax.rsqrt(jnp.mean(x * x, axis=-1, keepdims=True) + RMS_EPS) * g


def _store_row_tiles(ref, x):
    m = x.shape[0]
    for j in range(ROW_SUB):
        ref[pl.ds(j, m, stride=ROW_SUB), :] = x[:, j * LANES:(j + 1) * LANES]


def _load_row_tiles(ref, m):
    return jnp.concatenate([ref[pl.ds(j, m, stride=ROW_SUB), :] for j in range(ROW_SUB)], axis=1)


def _sigmoid(x):
    return 0.5 * jnp.tanh(0.5 * x) + 0.5


def _in_proj_body(xp_ref, xs_ref, g_ref, w_ref, zr_ref, zg_ref, zq_ref, sg_ref, *, np_tiles):
    x = jnp.where(pl.program_id(0) < np_tiles, xp_ref[...], xs_ref[...])
    h = _rms(x, g_ref[...]).astype(BF16)
    zr_ref[...] = jnp.dot(h, w_ref[:, 0:OFF_GMLP], preferred_element_type=F32)
    zg_ref[...] = jnp.dot(h, w_ref[:, OFF_GMLP:OFF_QMEM], preferred_element_type=F32)
    zq_ref[...] = jnp.dot(h, w_ref[:, OFF_QMEM:OFF_GATE], preferred_element_type=F32).astype(zq_ref.dtype)
    gates = jnp.dot(h, w_ref[:, OFF_GATE:IN_WIDTH], preferred_element_type=F32)
    sg_ref[...] = _sigmoid(gates).astype(BF16)


def _in_proj(x_p, x_s, g, w_bf16):
    n_p = x_p.shape[0]
    n = n_p + x_s.shape[0]
    tm = PROJ_TM
    np_tiles = n_p // tm
    row = lambda i: (i, 0)
    const = lambda i: (0, 0)
    return pl.pallas_call(
        functools.partial(_in_proj_body, np_tiles=np_tiles),
        grid=(n // tm,),
        in_specs=[
            pl.BlockSpec((tm, D_MODEL), lambda i: (jnp.minimum(i, np_tiles - 1), 0)),
            pl.BlockSpec((tm, D_MODEL), lambda i: (jnp.maximum(i - np_tiles, 0), 0)),
            pl.BlockSpec((1, D_MODEL), const),
            pl.BlockSpec((D_MODEL, IN_WIDTH), const, pipeline_mode=pl.Buffered(1)),
        ],
        out_specs=[
            pl.BlockSpec((tm, SHIFT_WIDTH), row),
            pl.BlockSpec((tm, 2 * GMLP_WIDTH), row),
            pl.BlockSpec((tm, MEM_WIDTH), row),
            pl.BlockSpec((tm, 3 * D_MODEL), row),
        ],
        out_shape=[
            jax.ShapeDtypeStruct((n, SHIFT_WIDTH), F32),
            jax.ShapeDtypeStruct((n, 2 * GMLP_WIDTH), F32),
            jax.ShapeDtypeStruct((n, MEM_WIDTH), BF16),
            jax.ShapeDtypeStruct((n, 3 * D_MODEL), BF16),
        ],
        compiler_params=_params(("parallel",)),
        name="in_proj",
    )(x_p, x_s, g, w_bf16)


def _mem_kv_body(x_ref, g_ref, w_ref, k_ref, v_ref):
    h = _rms(x_ref[...], g_ref[...]).astype(BF16)
    k_ref[...] = jnp.dot(h, w_ref[:, 0:MEM_WIDTH], preferred_element_type=F32)
    v_ref[...] = jnp.dot(h, w_ref[:, MEM_WIDTH:2 * MEM_WIDTH], preferred_element_type=F32)


def _mem_kv(mem, g, w_bf16):
    n = mem.shape[0]
    tm = PROJ_TM
    return pl.pallas_call(
        _mem_kv_body,
        grid=(n // tm,),
        in_specs=[
            pl.BlockSpec((tm, D_MODEL), lambda i: (i, 0)),
            pl.BlockSpec((1, D_MODEL), lambda i: (0, 0)),
            pl.BlockSpec((D_MODEL, 2 * MEM_WIDTH), lambda i: (0, 0)),
        ],
        out_specs=[pl.BlockSpec((tm, MEM_WIDTH), lambda i: (i, 0))] * 2,
        out_shape=[jax.ShapeDtypeStruct((n, MEM_WIDTH), F32)] * 2,
        compiler_params=_params(("parallel",)),
        name="mem_kv",
    )(mem, g, w_bf16)


EXP_M05 = math.exp(-0.5)


def _unit_lower_inverse(lows, c):
    rows = lax.broadcasted_iota(jnp.int32, (c, c), 0)
    cols = lax.broadcasted_iota(jnp.int32, (c, c), 1)
    eye = (rows == cols).astype(F32)
    invs = [eye - low for low in lows]
    powers = lows
    for _ in range(int(math.log2(c)) - 1):
        powers = [_bdot(pw, pw) for pw in powers]
        invs = [inv + _bdot(inv, pw) for inv, pw in zip(invs, powers)]
    return invs


def _rwkv_body(zr_ref, shift_ref, s0_ref, mu_ref, w0_ref, wd_ref, a0_ref, wa_ref, wg_ref, kk_ref, ka_ref,
               rk_ref, lng_ref, lnb_ref, hsum_ref,
               o_ref, sout_ref, shout_ref,
               carry_ref, state_ref, r_s, k_s, v_s, kk_s, b_s, ld_s, y_s, bon_s, g_s,
               rw_s, y0_s, gm_s, h0_s, *, tt, c, nseq):
    i = pl.program_id(1)
    rw = RWKV_WIDTH
    nh = RWKV_HEADS

    @pl.when(i == 0)
    def _():
        for q in range(nseq):
            state_ref[q * nh:(q + 1) * nh] = s0_ref[q]

    z = zr_ref[0]
    z_prev = pltpu.roll(z, 1, 0)
    row = lax.broadcasted_iota(jnp.int32, z.shape, 0)
    if nseq == 1:
        @pl.when(i == 0)
        def _():
            carry_ref[...] = shift_ref[0]

        z_prev = jnp.where(row == 0, carry_ref[...], z_prev)
        carry_ref[...] = z[tt - 1:tt, :]
    else:
        first_rows = jnp.concatenate([jnp.broadcast_to(shift_ref[q], (c, SHIFT_WIDTH)) for q in range(nseq)], axis=0)
        z_prev = jnp.where(row % c == 0, first_rows, z_prev)
    zs = z + mu_ref[...] * (z_prev - z)
    r = zs[:, 0:rw]
    k = zs[:, rw:2 * rw]
    v = zs[:, 2 * rw:3 * rw]
    zw = zs[:, 3 * rw:3 * rw + DECAY_LORA]
    za = zs[:, 3 * rw + DECAY_LORA:3 * rw + DECAY_LORA + A_LORA]
    zg = zs[:, 3 * rw + DECAY_LORA + A_LORA:SHIFT_WIDTH]
    xw = w0_ref[...] + _bdot(jnp.tanh(zw), wd_ref[...])
    ld_s[...] = -EXP_M05 * _sigmoid(xw)
    a = _sigmoid(a0_ref[...] + _bdot(za, wa_ref[...]))
    g_s[...] = _bdot(_sigmoid(zg), wg_ref[...])
    kk = k * kk_ref[...]
    k = k * (1.0 + (a - 1.0) * ka_ref[...])
    r_s[...] = r
    k_s[...] = k
    v_s[...] = v

    def head_sum(t):
        return jnp.dot(t.astype(BF16), hsum_ref[...], preferred_element_type=F32)

    kk = kk / jnp.maximum(jnp.sqrt(head_sum(kk * kk)), 1e-12)
    kk_s[...] = kk
    b_s[...] = kk * a
    bon_s[...] = head_sum(r * k * rk_ref[...]) * v

    rows_c = lax.broadcasted_iota(jnp.int32, (c, c), 0)
    cols_c = lax.broadcasted_iota(jnp.int32, (c, c), 1)
    strict = rows_c > cols_c
    incl = rows_c >= cols_c
    tril_ones = jnp.where(incl, 1.0, 0.0).astype(BF16)
    rows_2c = lax.broadcasted_iota(jnp.int32, (c, 2 * c), 0)
    cols_2c = lax.broadcasted_iota(jnp.int32, (c, 2 * c), 1)
    incl2 = rows_2c >= jnp.where(cols_2c >= c, cols_2c - c, cols_2c)
    rows_k = lax.broadcasted_iota(jnp.int32, (HEAD_DIM, HEAD_DIM), 0)
    cols_k = lax.broadcasted_iota(jnp.int32, (HEAD_DIM, HEAD_DIM), 1)
    eye_k = (rows_k == cols_k).astype(F32)
    zeros_cv = jnp.zeros((c, HEAD_DIM), F32)
    heads = [slice(h * HEAD_DIM, (h + 1) * HEAD_DIM) for h in range(RWKV_HEADS)]

    n_chunks = tt // c
    group = nseq if nseq > 1 else math.gcd(n_chunks, RWKV_GROUP)

    def chunk_rows(ci):
        return pl.ds(ci * c, c) if isinstance(ci, int) else pl.ds(pl.multiple_of(ci * c, c), c)

    def scaled(ci):
        rows = chunk_rows(ci)
        ld = ld_s[rows, :]
        ld_hi, ld_lo = _split(ld)
        cum = (jnp.dot(tril_ones, ld_hi, preferred_element_type=F32)
               + jnp.dot(tril_ones, ld_lo, preferred_element_type=F32))
        e_inc = jnp.exp(cum)
        e_neg = jnp.exp(-cum)
        kt = k_s[rows, :] * e_neg
        bt = b_s[rows, :] * e_neg
        g_end = e_inc[c - 1:c, :]
        return dict(ci=ci, rt=r_s[rows, :] * e_inc, kkt=kk_s[rows, :] * jnp.exp(cum - ld), kt=kt, bt=bt,
                    g_end=g_end, bc=bt * g_end, kc=kt * g_end, vv=v_s[rows, :])

    def phase1(cj, carry):
        chains = [(ch, h, s) for ch in [scaled(cj * group + g) for g in range(group)]
                  for h, s in enumerate(heads)]
        x = [jnp.concatenate([ch["kkt"][:, s], ch["rt"][:, s]], axis=0) for ch, _, s in chains]
        zz = [jnp.concatenate([ch["bt"][:, s], ch["kt"][:, s]], axis=0) for ch, _, s in chains]
        amat = [_bdot(xh, zh, NT) for xh, zh in zip(x, zz)]
        l_b = [jnp.where(strict, am[0:c, 0:c], 0.0) for am in amat]
        l_k = [jnp.where(strict, am[0:c, c:2 * c], 0.0) for am in amat]
        a_r = [jnp.where(incl2, am[c:2 * c, :], 0.0) for am in amat]
        tinv = _unit_lower_inverse(l_b, c)
        lkv = [_bdot(lk, ch["vv"][:, s]) for lk, (ch, _, s) in zip(l_k, chains)]
        wu = [-_bdot(t, jnp.concatenate([ch["kkt"][:, s], lv], axis=1))
              for t, lv, (ch, _, s) in zip(tinv, lkv, chains)]
        m = [jnp.concatenate([w, jnp.concatenate([zeros_cv, ch["vv"][:, s]], axis=1)], axis=0)
             for w, (ch, _, s) in zip(wu, chains)]
        am2 = [_bdot(ar, mh) for ar, mh in zip(a_r, m)]
        gh = [_bdot(mh, jnp.concatenate([ch["bc"][:, s], ch["kc"][:, s]], axis=0), TN)
              for mh, (ch, _, s) in zip(m, chains)]
        for q, (ch, h, s) in enumerate(chains):
            idx = ch["ci"] * RWKV_HEADS + h
            rw_s[idx] = ch["rt"][:, s] + am2[q][:, 0:HEAD_DIM]
            y0_s[idx] = am2[q][:, HEAD_DIM:2 * HEAD_DIM]
            gm_s[idx] = gh[q][0:HEAD_DIM, :] + eye_k * ch["g_end"][:, s]
            h0_s[idx] = gh[q][HEAD_DIM:2 * HEAD_DIM, :]
        return carry

    if n_chunks == group:
        phase1(0, 0)
    else:
        lax.fori_loop(0, n_chunks // group, phase1, 0)

    if nseq == 1:
        def phase2(ci, carry):
            rows = chunk_rows(ci)
            for h, s in enumerate(heads):
                idx = ci * nh + h
                s_h = state_ref[h]
                y_s[rows, s] = _bdot(rw_s[idx], s_h, NT) + y0_s[idx]
                state_ref[h] = _bdot(s_h, gm_s[idx]) + h0_s[idx]
            return carry

        lax.fori_loop(0, n_chunks, phase2, 0)
    else:
        pairs = [(q, h, s) for q in range(nseq) for h, s in enumerate(heads)]
        states = [state_ref[q * nh + h] for q, h, _ in pairs]
        ys = [_bdot(rw_s[q * nh + h], st, NT) + y0_s[q * nh + h] for (q, h, _), st in zip(pairs, states)]
        new = [_bdot(st, gm_s[q * nh + h]) + h0_s[q * nh + h] for (q, h, _), st in zip(pairs, states)]
        for (q, h, s), yq, nq in zip(pairs, ys, new):
            y_s[chunk_rows(q), s] = yq
            state_ref[q * nh + h] = nq
    y = y_s[...]
    yc = y - head_sum(y) * (1.0 / HEAD_DIM)
    var = head_sum(yc * yc) * (1.0 / HEAD_DIM)
    yn = yc * lax.rsqrt(var + GN_EPS)
    o_ref[0] = ((yn * lng_ref[...] + lnb_ref[...] + bon_s[...]) * g_s[...]).astype(o_ref.dtype)

    @pl.when(i == pl.num_programs(1) - 1)
    def _():
        for q in range(nseq):
            sout_ref[q] = state_ref[q * nh:(q + 1) * nh]
            last = tt - 1 if nseq == 1 else (q + 1) * c - 1
            shout_ref[q] = zr_ref[0, last:last + 1, :]


def _rwkv(zr3, blk_off, b, nt, shift_prev, s_prev, p, tt, c, nseq):
    vec = lambda n: pl.BlockSpec((1, n), lambda bi, ti: (0, 0))
    mat = lambda m, n: pl.BlockSpec((m, n), lambda bi, ti: (0, 0))
    rw = RWKV_WIDTH
    scr = lambda: pltpu.VMEM((tt, rw), F32)
    nch = (tt // c) * RWKV_HEADS
    return pl.pallas_call(
        functools.partial(_rwkv_body, tt=tt, c=c, nseq=nseq),
        grid=(b // nseq, nt),
        in_specs=[
            pl.BlockSpec((1, tt, SHIFT_WIDTH), lambda bi, ti: (blk_off + bi * nt + ti, 0, 0)),
            pl.BlockSpec((nseq, 1, SHIFT_WIDTH), lambda bi, ti: (bi, 0, 0)),
            pl.BlockSpec((nseq, RWKV_HEADS, HEAD_DIM, HEAD_DIM), lambda bi, ti: (bi, 0, 0, 0)),
            vec(SHIFT_WIDTH), vec(rw), mat(DECAY_LORA, rw), vec(rw), mat(A_LORA, rw), mat(GATE_LORA, rw),
            vec(rw), vec(rw), vec(rw), vec(rw), vec(rw), mat(rw, rw),
        ],
        out_specs=[
            pl.BlockSpec((1, tt, rw), lambda bi, ti: (bi, ti, 0)),
            pl.BlockSpec((nseq, RWKV_HEADS, HEAD_DIM, HEAD_DIM), lambda bi, ti: (bi, 0, 0, 0)),
            pl.BlockSpec((nseq, 1, SHIFT_WIDTH), lambda bi, ti: (bi, 0, 0)),
        ],
        out_shape=[
            jax.ShapeDtypeStruct((b // nseq, nt * tt, rw), BF16),
            jax.ShapeDtypeStruct((b, RWKV_HEADS, HEAD_DIM, HEAD_DIM), F32),
            jax.ShapeDtypeStruct((b, 1, SHIFT_WIDTH), F32),
        ],
        scratch_shapes=[
            pltpu.VMEM((1, SHIFT_WIDTH), F32),
            pltpu.VMEM((nseq * RWKV_HEADS, HEAD_DIM, HEAD_DIM), F32),
            scr(), scr(), scr(), scr(), scr(), scr(), scr(), scr(), scr(),
            pltpu.VMEM((nch, c, HEAD_DIM), F32),
            pltpu.VMEM((nch, c, HEAD_DIM), F32),
            pltpu.VMEM((nch, HEAD_DIM, HEAD_DIM), F32),
            pltpu.VMEM((nch, HEAD_DIM, HEAD_DIM), F32),
        ],
        compiler_params=_params(("parallel", "arbitrary")),
        name="rwkv",
    )(zr3, shift_prev, s_prev, p["mu"], p["w0"], p["wd"], p["a0"], p["wa"], p["wg"], p["kk"], p["ka"],
      p["rk"], p["lng"], p["lnb"], p["hsum"])


def _gmlp_body(z_ref, lng_ref, lnb_ref, ws_ref, bs_ref, o_ref, v_ref):
    z = z_ref[...]
    ge = 0.5 * z * (1.0 + lax.erf(z * (1.0 / math.sqrt(2.0))))
    u = ge[:, 0:GMLP_WIDTH]
    v = ge[:, GMLP_WIDTH:2 * GMLP_WIDTH]
    mean = jnp.mean(v, axis=-1, keepdims=True)
    vc = v - mean
    var = jnp.mean(vc * vc, axis=-1, keepdims=True)
    vn = vc * lax.rsqrt(var + LN_EPS) * lng_ref[...] + lnb_ref[...]
    v_ref[...] = vn
    gd = GMLP_WIDTH // GMLP_GROUPS
    for q in range(GMLP_TILE):
        rows = slice(q * CHUNK, (q + 1) * CHUNK)
        for g in range(GMLP_GROUPS):
            gs = slice(g * gd, (g + 1) * gd)
            sv = jnp.dot(ws_ref[g], vn[rows, gs].astype(BF16), preferred_element_type=F32) + bs_ref[:, g:g + 1]
            o_ref[rows, gs] = (u[rows, gs] * sv).astype(o_ref.dtype)


def _gmlp(zg, tile_off, n_rows, lng, lnb, ws_bf16, bs):
    tile = GMLP_TILE * CHUNK
    return pl.pallas_call(
        _gmlp_body,
        grid=(n_rows // tile,),
        in_specs=[
            pl.BlockSpec((tile, 2 * GMLP_WIDTH), lambda i: (tile_off + i, 0)),
            pl.BlockSpec((1, GMLP_WIDTH), lambda i: (0, 0)),
            pl.BlockSpec((1, GMLP_WIDTH), lambda i: (0, 0)),
            pl.BlockSpec((GMLP_GROUPS, CHUNK, CHUNK), lambda i: (0, 0, 0)),
            pl.BlockSpec((CHUNK, GMLP_GROUPS), lambda i: (0, 0)),
        ],
        out_specs=[
            pl.BlockSpec((tile, GMLP_WIDTH), lambda i: (i, 0)),
            pl.BlockSpec((tile, GMLP_WIDTH), lambda i: (i, 0)),
        ],
        out_shape=[
            jax.ShapeDtypeStruct((n_rows, GMLP_WIDTH), BF16),
            jax.ShapeDtypeStruct((n_rows, GMLP_WIDTH), F32),
        ],
        compiler_params=_params(("parallel",)),
        name="gmlp",
    )(zg, lng, lnb, ws_bf16, bs)


def _mem_attn_body(q_ref, k_ref, v_ref, o_ref, *, bb, feature_major):
    hd = MEM_WIDTH // MEM_HEADS
    pairs = [(s, slice(h * hd, (h + 1) * hd)) for s in range(bb) for h in range(MEM_HEADS)]
    if feature_major:
        scores = [_bdot(q_ref[s, :, hs], k_ref[s, hs, :], NN) * (hd ** -0.5) for s, hs in pairs]
    else:
        scores = [_bdot(q_ref[s, :, hs], k_ref[s, :, hs], NT) * (hd ** -0.5) for s, hs in pairs]
    probs = []
    for sc in scores:
        pr = jnp.exp(sc - jnp.max(sc, axis=-1, keepdims=True))
        probs.append(pr / jnp.sum(pr, axis=-1, keepdims=True))
    for (s, hs), pr in zip(pairs, probs):
        if feature_major:
            o_ref[s, :, hs] = _bdot(pr, v_ref[s, hs, :], NT).astype(o_ref.dtype)
        else:
            o_ref[s, :, hs] = _bdot(pr, v_ref[s, :, hs]).astype(o_ref.dtype)


def _mem_attn(q3, blk_off, b, nt, mk, mv, tt, bb, feature_major):
    return pl.pallas_call(
        functools.partial(_mem_attn_body, bb=bb, feature_major=feature_major),
        grid=(b // bb, nt),
        in_specs=[
            pl.BlockSpec((bb, tt, MEM_WIDTH), lambda bi, ti: (blk_off // bb + bi * nt + ti, 0, 0)),
            pl.BlockSpec((bb, N_MEM, MEM_WIDTH), lambda bi, ti: (bi, 0, 0)),
            pl.BlockSpec((bb, N_MEM, MEM_WIDTH), lambda bi, ti: (bi, 0, 0)),
        ],
        out_specs=pl.BlockSpec((bb, tt, MEM_WIDTH), lambda bi, ti: (bi, ti, 0)),
        out_shape=jax.ShapeDtypeStruct((b, nt * tt, MEM_WIDTH), BF16),
        compiler_params=_params(("parallel", "parallel")),
        name="mem_attn",
    )(q3, mk, mv)


def _merge_body(x_p, x_s, orw_p, orw_s, ogm_p, ogm_s, ome_p, ome_s, sg_ref, wbr_ref, wbg_ref, wbm_ref, wout_ref,
                gffn_ref, wrt_ref, brt_ref, x2_ref, h2_ref, route_ref, route_t_ref, cnt_ref, count_s, *, np_tiles):
    i = pl.program_id(0)
    d = D_MODEL
    tm = x_p.shape[0]
    is_p = i < np_tiles
    x = jnp.where(is_p, x_p[...], x_s[...])
    orw = jnp.where(is_p, orw_p[...], orw_s[...])
    ogm = jnp.where(is_p, ogm_p[...], ogm_s[...])
    ome = jnp.where(is_p, ome_p[...], ome_s[...])
    merged = sg_ref[:, 0:d].astype(F32) * _bdot(orw, wbr_ref[...])
    merged = merged + sg_ref[:, d:2 * d].astype(F32) * _bdot(ogm, wbg_ref[...])
    merged = merged + sg_ref[:, 2 * d:3 * d].astype(F32) * _bdot(ome, wbm_ref[...])
    x2 = x + _bdot(merged, wout_ref[...])
    x2_ref[...] = x2
    h2 = _rms(x2, gffn_ref[...])
    _store_row_tiles(h2_ref, h2)

    @pl.when(i == 0)
    def _():
        count_s[...] = jnp.zeros_like(count_s)

    lane = lax.broadcasted_iota(jnp.int32, (tm, LANES), 1)
    logits = jnp.where(lane < N_EXPERTS, _dot3(h2, wrt_ref[...]) + brt_ref[...], -jnp.inf)
    lane_f = lane.astype(F32)
    tops, hots, idxs = [], [], []
    for _ in range(TOP_K):
        top = jnp.max(logits, axis=-1, keepdims=True)
        idx = jnp.min(jnp.where(logits == top, lane_f, float(LANES)), axis=-1, keepdims=True)
        hot = lane_f == idx
        logits = jnp.where(hot, -jnp.inf, logits)
        tops.append(top)
        hots.append(hot)
        idxs.append(idx)
    weights = [jnp.exp(t - tops[0]) for t in tops]
    denom = weights[0] + weights[1] + weights[2] + weights[3]
    onehot = jnp.zeros((tm, LANES), F32)
    for hot in hots:
        onehot = onehot + jnp.where(hot, 1.0, 0.0)
    rows_t = lax.broadcasted_iota(jnp.int32, (tm, tm), 0)
    cols_t = lax.broadcasted_iota(jnp.int32, (tm, tm), 1)
    ahead = jnp.where(rows_t > cols_t, 1.0, 0.0).astype(BF16)
    prefix = jnp.dot(ahead, onehot.astype(BF16), preferred_element_type=F32) + count_s[...]
    route = jnp.zeros((tm, LANES), F32)
    for kk in range(TOP_K):
        rank = jnp.sum(jnp.where(hots[kk], prefix, 0.0), axis=-1, keepdims=True)
        route = jnp.where(lane == ROUTE_E + kk, idxs[kk], route)
        route = jnp.where(lane == ROUTE_RANK + kk, rank, route)
        route = jnp.where(lane == ROUTE_GATE + kk, weights[kk] / denom, route)
    route_ref[...] = route
    route_t_ref[...] = route.T
    count_s[...] = count_s[...] + jnp.sum(onehot, axis=0, keepdims=True)
    cnt_ref[...] = jnp.broadcast_to(count_s[...], cnt_ref.shape)


def _merge(x, o_rw, o_gm, o_me, sg, wbr, wbg, wbm, wout, gffn, wrt_pad, brt_pad):
    n_p = x[0].shape[0]
    n = n_p + x[1].shape[0]
    tm = MERGE_TM
    d = D_MODEL
    np_tiles = n_p // tm
    row = lambda i: (i, 0)
    const = lambda i: (0, 0)
    first =lambda i: (jnp.minimum(i, np_tiles - 1), 0)
    second = lambda i: (jnp.maximum(i - np_tiles, 0), 0)

    def pair(width):
        return [pl.BlockSpec((tm, width), first), pl.BlockSpec((tm, width), second)]

    return pl.pallas_call(
        functools.partial(_merge_body, np_tiles=np_tiles),
        grid=(n // tm,),
        in_specs=pair(d) + pair(RWKV_WIDTH) + pair(GMLP_WIDTH) + pair(MEM_WIDTH) + [
            pl.BlockSpec((tm, 3 * d), row),
            pl.BlockSpec((RWKV_WIDTH, d), const),
            pl.BlockSpec((GMLP_WIDTH, d), const),
            pl.BlockSpec((MEM_WIDTH, d), const),
            pl.BlockSpec((d, d), const),
            pl.BlockSpec((1, d), const),
            pl.BlockSpec((d, LANES), const),
            pl.BlockSpec((1, LANES), const),
        ],
        out_specs=[
            pl.BlockSpec((tm, d), row),
            pl.BlockSpec((tm * ROW_SUB, LANES), row),
            pl.BlockSpec((tm, LANES), row),
            pl.BlockSpec((LANES, tm), lambda i: (0, i)),
            pl.BlockSpec((8, LANES), const),
        ],
        out_shape=[
            jax.ShapeDtypeStruct((n, d), F32),
            jax.ShapeDtypeStruct((n * ROW_SUB, LANES), F32),
            jax.ShapeDtypeStruct((n, LANES), F32),
            jax.ShapeDtypeStruct((LANES, n), F32),
            jax.ShapeDtypeStruct((8, LANES), F32),
        ],
        scratch_shapes=[pltpu.VMEM((1, LANES), F32)],
        compiler_params=_params(("arbitrary",)),
        name="merge",
    )(x[0], x[1], o_rw[0], o_rw[1], o_gm[0], o_gm[1], o_me[0], o_me[1], sg, wbr, wbg, wbm, wout, gffn, wrt_pad,
      brt_pad)


N_ZERO_BLOCKS = 2 * N_EXPERTS


def _dispatch_body(dest_ref, zlist_ref, h_ref, xs_hbm, zero_s, sem_z, sem):
    i = pl.program_id(0)
    tm = DISPATCH_TM
    bm = MOE_BM

    @pl.when(i == 0)
    def _():
        zero_s[...] = jnp.zeros_like(zero_s)

        def zero_copy(q):
            start = pl.multiple_of(zlist_ref[q] * bm, bm)
            return pltpu.make_async_copy(zero_s, xs_hbm.at[pl.ds(start, bm)], sem_z)

        def start(q, carry):
            @pl.when(zlist_ref[q] >= 0)
            def _():
                zero_copy(q).start()
            return carry

        def wait(q, carry):
            @pl.when(zlist_ref[q] >= 0)
            def _():
                zero_copy(q).wait()
            return carry

        lax.fori_loop(0, N_ZERO_BLOCKS, start, 0)
        lax.fori_loop(0, N_ZERO_BLOCKS, wait, 0)

    base = i * tm
    n_tok = pl.num_programs(0) * tm

    def body(r, carry):
        for kk in range(TOP_K):
            slot = dest_ref[kk * n_tok + base + r]
            pltpu.make_async_copy(h_ref.at[r], xs_hbm.at[slot], sem).start(priority=kk % 2)
        return carry

    lax.fori_loop(0, tm, body, 0, unroll=2)
    for kk in range(TOP_K):
        pltpu.make_async_copy(h_ref, xs_hbm.at[pl.ds(0, tm)], sem).wait()


def _dispatch(dest, zlist, h, n_blocks):
    n = h.shape[0]
    tm = DISPATCH_TM
    grid_spec = pltpu.PrefetchScalarGridSpec(
        num_scalar_prefetch=2,
        grid=(n // tm,),
        in_specs=[pl.BlockSpec((tm, ROW_SUB, LANES), lambda i, dest, zl: (i, 0, 0))],
        out_specs=pl.BlockSpec(memory_space=pl.ANY),
        scratch_shapes=[
            pltpu.VMEM((MOE_BM, ROW_SUB, LANES), F32),
            pltpu.SemaphoreType.DMA(()),
            pltpu.SemaphoreType.DMA(()),
        ],
    )
    return pl.pallas_call(
        _dispatch_body,
        grid_spec=grid_spec,
        out_shape=jax.ShapeDtypeStruct((n_blocks * MOE_BM, ROW_SUB, LANES), F32),
        compiler_params=_params(("arbitrary",)),
        name="moe_dispatch",
    )(dest, zlist, h)


CAST_ROWS = 64


def _moe_body(be_ref, nused_ref, nxt_ref, par_ref, x_ref, w1_hbm, b1_ref, w2_hbm, b2_ref, o_ref,
              w1f, w2f, w1b, w2b, sem):
    j = pl.program_id(0)
    n_used = nused_ref[0]

    def weight_copies(expert, slot):
        return (pltpu.make_async_copy(w1_hbm.at[expert], w1f.at[slot], sem.at[slot, 0]),
                pltpu.make_async_copy(w2_hbm.at[expert], w2f.at[slot], sem.at[slot, 1]))

    @pl.when(j == 0)
    def _():
        for cp in weight_copies(be_ref[0], 0):
            cp.start()

    @pl.when(j < n_used)
    def _():
        slot = par_ref[be_ref[j]]
        changed = jnp.logical_or(j == 0, be_ref[j] != be_ref[jnp.maximum(j - 1, 0)])

        @pl.when(changed)
        def _():
            for cp in weight_copies(be_ref[j], slot):
                cp.wait()

            nxt = nxt_ref[be_ref[j]]

            @pl.when(nxt >= 0)
            def _():
                for cp in weight_copies(nxt, 1 - slot):
                    cp.start()

            def cast1(q, carry):
                rows = pl.ds(pl.multiple_of(q * CAST_ROWS, CAST_ROWS), CAST_ROWS)
                w1b[rows, :] = w1f[slot, rows, :].astype(BF16)
                return carry

            def cast2(q, carry):
                rows = pl.ds(pl.multiple_of(q * CAST_ROWS, CAST_ROWS), CAST_ROWS)
                w2b[rows, :] = w2f[slot, rows, :].astype(BF16)
                return carry

            lax.fori_loop(0, D_MODEL // CAST_ROWS, cast1, 0)
            lax.fori_loop(0, D_FF // CAST_ROWS, cast2, 0)

        x = _load_row_tiles(x_ref, MOE_BM).astype(BF16)
        z = jnp.dot(x, w1b[...], preferred_element_type=F32) + b1_ref[0]
        zg = jnp.minimum(z[:, 0:D_FF], SWIGLU_LIMIT)
        zl = jnp.clip(z[:, D_FF:2 * D_FF], -SWIGLU_LIMIT, SWIGLU_LIMIT)
        act = zg * _sigmoid(SWIGLU_ALPHA * zg) * (zl + 1.0)
        _store_row_tiles(o_ref, jnp.dot(act.astype(BF16), w2b[...], preferred_element_type=F32) + b2_ref[0])

    @pl.when(j >= n_used)
    def _():
        o_ref[...] = jnp.zeros_like(o_ref)


def _moe(blk_e, n_used, nxt_e, parity, xs, w1, b1, w2, b2, n_blocks):
    bm = MOE_BM
    d = D_MODEL
    grid_spec = pltpu.PrefetchScalarGridSpec(
        num_scalar_prefetch=4,
        grid=(n_blocks,),
        in_specs=[
            pl.BlockSpec((bm * ROW_SUB, LANES), lambda j, be, nu, nx, pa: (j, 0)),
            pl.BlockSpec(memory_space=pl.ANY),
            pl.BlockSpec((1, 1, 2 * D_FF), lambda j, be, nu, nx, pa: (be[j], 0, 0)),
            pl.BlockSpec(memory_space=pl.ANY),
            pl.BlockSpec((1, 1, d), lambda j, be, nu, nx, pa: (be[j], 0, 0)),
        ],
        out_specs=pl.BlockSpec((bm * ROW_SUB, LANES), lambda j, be, nu, nx, pa: (j, 0)),
        scratch_shapes=[
            pltpu.VMEM((2, d, 2 * D_FF), F32),
            pltpu.VMEM((2, D_FF, d), F32),
            pltpu.VMEM((d, 2 * D_FF), BF16),
            pltpu.VMEM((D_FF, d), BF16),
            pltpu.SemaphoreType.DMA((2, 2)),
        ],
    )
    return pl.pallas_call(
        _moe_body,
        grid_spec=grid_spec,
        out_shape=jax.ShapeDtypeStruct((n_blocks * bm * ROW_SUB, LANES), F32),
        compiler_params=_params(("arbitrary",)),
        name="moe_ffn",
    )(blk_e, n_used, nxt_e, parity, xs, w1, b1, w2, b2)


def _combine_body(dest_ref, x2_ref, route_ref, gfin_ref, yb_hbm, op_ref, os_ref, buf0, buf1, sem, *, np_steps):
    i = pl.program_id(0)
    n_steps = pl.num_programs(0)
    tm = COMBINE_TM
    n_tok = n_steps * (2 * tm)
    bufs = (buf0, buf1)

    def issue(tile, which):
        base = tile * tm
        for r in range(tm):
            for kk in range(TOP_K):
                slot = dest_ref[kk * n_tok + base + r]
                src = pl.ds(pl.multiple_of(slot * ROW_SUB, ROW_SUB), ROW_SUB)
                pltpu.make_async_copy(yb_hbm.at[src], bufs[which].at[kk, r * ROW_SUB:(r + 1) * ROW_SUB],
                                      sem.at[which]).start(priority=kk % 2)

    def wait(which):
        for kk in range(TOP_K):
            pltpu.make_async_copy(yb_hbm.at[pl.ds(0, tm * ROW_SUB)], bufs[which].at[kk], sem.at[which]).wait()

    def reduce(which, half):
        rows = slice(half * tm, (half + 1) * tm)
        acc = x2_ref[rows, :]
        for kk in range(TOP_K):
            gate = route_ref[rows, ROUTE_GATE + kk:ROUTE_GATE + kk + 1]
            acc = acc + gate * _load_row_tiles(bufs[which].at[kk], tm)
        return _rms(acc, gfin_ref[...])

    @pl.when(i == 0)
    def _():
        issue(0, 0)

    wait(0)
    issue(2 * i + 1, 1)
    y0 = reduce(0, 0)
    wait(1)
    issue(jnp.minimum(2 * i + 2, 2 * n_steps - 1), 0)
    y1 = reduce(1, 1)

    @pl.when(i < np_steps)
    def _():
        op_ref[0:tm, :] = y0
        op_ref[tm:2 * tm, :] = y1

    @pl.when(i >= np_steps)
    def _():
        os_ref[0:tm, :] = y0
        os_ref[tm:2 * tm, :] = y1

    @pl.when(i == n_steps - 1)
    def _():
        wait(0)


def _combine(dest, x2, route, gfin, yb, n_p):
    n, d = x2.shape
    tm = COMBINE_TM
    step = 2 * tm
    np_steps = n_p // step
    grid_spec = pltpu.PrefetchScalarGridSpec(
        num_scalar_prefetch=1,
        grid=(n // step,),
        in_specs=[
            pl.BlockSpec((step, d), lambda i, dest: (i, 0)),
            pl.BlockSpec((step, LANES), lambda i, dest: (i, 0)),
            pl.BlockSpec((1, d), lambda i, dest: (0, 0)),
            pl.BlockSpec(memory_space=pl.ANY),
        ],
        out_specs=[
            pl.BlockSpec((step, d), lambda i, dest: (jnp.minimum(i, np_steps - 1), 0)),
            pl.BlockSpec((step, d), lambda i, dest: (jnp.maximum(i - np_steps, 0), 0)),
        ],
        scratch_shapes=[
            pltpu.VMEM((TOP_K, tm * ROW_SUB, LANES), F32),
            pltpu.VMEM((TOP_K, tm * ROW_SUB, LANES), F32),
            pltpu.SemaphoreType.DMA((2,)),
        ],
    )
    return pl.pallas_call(
        functools.partial(_combine_body, np_steps=np_steps),
        grid_spec=grid_spec,
        out_shape=[jax.ShapeDtypeStruct((n_p, d), F32), jax.ShapeDtypeStruct((n - n_p, d), F32)],
        compiler_params=_params(("arbitrary",)),
        name="moe_combine",
    )(dest, x2, route, gfin, yb)


def _slot_tables(route_t, counts_row, n_blocks):
    bm = MOE_BM
    e = route_t[ROUTE_E:ROUTE_E + TOP_K].astype(jnp.int32)
    rank = route_t[ROUTE_RANK:ROUTE_RANK + TOP_K].astype(jnp.int32)
    counts = counts_row[:N_EXPERTS].astype(jnp.int32)
    padded = (counts + bm - 1) // bm * bm
    pad_end = jnp.cumsum(padded)
    pad_start = pad_end - padded
    experts = jnp.arange(N_EXPERTS, dtype=jnp.int32)[:, None, None]
    dest = (jnp.sum(jnp.where(e[None] == experts, pad_start[:, None, None], 0), axis=0) + rank).reshape(-1)
    blk_start = jnp.arange(n_blocks, dtype=jnp.int32) * bm
    blk_e = jnp.minimum(jnp.sum(pad_end[None, :] <= blk_start[:, None], axis=1), N_EXPERTS - 1).astype(jnp.int32)
    n_used = pad_end[-1] // bm
    last_blk = jnp.where(padded > 0, pad_end // bm - 1, -1)
    trailing = n_used + jnp.arange(N_EXPERTS, dtype=jnp.int32)
    trailing = jnp.where(trailing < n_blocks, trailing, -1)
    zlist = jnp.concatenate([last_blk, trailing]).astype(jnp.int32)
    ids = jnp.arange(N_EXPERTS, dtype=jnp.int32)
    present = counts > 0
    later = jnp.where(jnp.logical_and(ids[None, :] > ids[:, None], present[None, :]), ids[None, :], N_EXPERTS)
    next_present = jnp.min(later, axis=1)
    next_present = jnp.where(next_present < N_EXPERTS, next_present, -1)
    nxt_e = next_present.astype(jnp.int32)
    parity = ((jnp.cumsum(present.astype(jnp.int32)) - 1) % 2).astype(jnp.int32)
    return dest.astype(jnp.int32), blk_e, n_used.astype(jnp.int32).reshape(1), nxt_e, parity, zlist


def kernel(x_prompt, x_sample, mem_prompt, state_shift, state_wkv, cache_mem_k, cache_mem_v, g_norm_mix, w_in, mu_shift, w0, w_decay_up, a0, w_a_up, w_g_up, k_k, k_a, r_k, ln_x_g, ln_x_b, gmlp_ln_g, gmlp_ln_b, w_spatial, b_spatial, g_norm_mem, w_mem_kv, w_br_rwkv, w_br_gmlp, w_br_mem, w_out, g_norm_ffn, w_router, b_router, w_exp1, b_exp1, w_exp2, b_exp2, g_norm_final):
    bp, tp, d = x_prompt.shape
    bs, ts, _ = x_sample.shape
    n_p, n_s = bp * tp, bs * ts
    n_all = n_p + n_s
    l = 0
    row = lambda a: a.reshape(1, -1)

    x_pair = (x_prompt.reshape(n_p, d), x_sample.reshape(n_s, d))
    zr, zg, zq, sg = _in_proj(x_pair[0], x_pair[1], row(g_norm_mix[l]), w_in[l].astype(BF16))

    mk_p, mv_p = _mem_kv(mem_prompt.reshape(bp * N_MEM, d), row(g_norm_mem[l]), w_mem_kv[l].astype(BF16))

    rp = dict(mu=row(mu_shift[l]), w0=row(w0[l]), wd=w_decay_up[l].astype(BF16), a0=row(a0[l]),
              wa=w_a_up[l].astype(BF16), wg=w_g_up[l].astype(BF16), kk=row(k_k[l]), ka=row(k_a[l]),
              rk=row(r_k[l]), lng=row(ln_x_g[l]), lnb=row(ln_x_b[l]))
    head_of = jnp.arange(RWKV_WIDTH, dtype=jnp.int32) // HEAD_DIM
    rp["hsum"] = (head_of[:, None] == head_of[None, :]).astype(BF16)
    o_rw_p, s_p, shift_p = _rwkv(zr.reshape(n_all // RWKV_TT, RWKV_TT, SHIFT_WIDTH), 0, bp, tp // RWKV_TT,
                        jnp.zeros((bp, 1, SHIFT_WIDTH), F32), jnp.zeros((bp, RWKV_HEADS, HEAD_DIM, HEAD_DIM), F32),
                        rp, tt=RWKV_TT, c=RWKV_C, nseq=1)
    tile_s = RWKV_SAMPLE_SEQS * ts
    o_rw_s, s_s, shift_s = _rwkv(zr.reshape(n_all // tile_s, tile_s, SHIFT_WIDTH), n_p // tile_s, bs, 1,
                        state_shift[l].reshape(bs, 1, SHIFT_WIDTH), state_wkv[l], rp, tt=tile_s, c=ts,
                        nseq=RWKV_SAMPLE_SEQS)

    tri = jnp.tril(jnp.ones((CHUNK, CHUNK), bool))
    ws_p = jnp.where(tri, w_spatial[l], 0.0).astype(BF16)
    bs_p = b_spatial[l].T
    reps = CHUNK // ts
    tri_s = jnp.tril(jnp.ones((ts, ts), bool))
    ws_small = jnp.where(tri_s, w_spatial[l][:, :ts, :ts], 0.0)
    eye = jnp.eye(reps, dtype=F32)
    ws_s = jnp.einsum("ab,gij->gaibj", eye, ws_small).reshape(GMLP_GROUPS, CHUNK, CHUNK).astype(BF16)
    bs_s = jnp.tile(b_spatial[l][:, :ts], (1, reps)).T
    lng, lnb = row(gmlp_ln_g[l]), row(gmlp_ln_b[l])
    o_gm_p, _ = _gmlp(zg, 0, n_p, lng, lnb, ws_p, bs_p)
    o_gm_s, v_rows_s = _gmlp(zg, n_p // (GMLP_TILE * CHUNK), n_s, lng, lnb, ws_s, bs_s)

    o_me_p = _mem_attn(zq.reshape(n_all // ATTN_TT, ATTN_TT, MEM_WIDTH), 0, bp, tp // ATTN_TT,
                       mk_p.reshape(bp, N_MEM, MEM_WIDTH), mv_p.reshape(bp, N_MEM, MEM_WIDTH), tt=ATTN_TT, bb=1,
                       feature_major=False)
    mk_s = jnp.transpose(cache_mem_k[l].reshape(bs, N_MEM, MEM_WIDTH), (0, 2, 1))
    mv_s = jnp.transpose(cache_mem_v[l].reshape(bs, N_MEM, MEM_WIDTH), (0, 2, 1))
    o_me_s = _mem_attn(zq.reshape(n_all // ts, ts, MEM_WIDTH), n_p // ts, bs, 1, mk_s, mv_s, tt=ts, bb=ATTN_BB,
                       feature_major=True)

    wrt_pad = jnp.zeros((d, LANES), F32).at[:, :N_EXPERTS].set(w_router[l])
    brt_pad = jnp.zeros((1, LANES), F32).at[0, :N_EXPERTS].set(b_router[l])
    x2, h2, route, route_t, counts = _merge(
        x_pair, (o_rw_p.reshape(n_p, RWKV_WIDTH), o_rw_s.reshape(n_s, RWKV_WIDTH)), (o_gm_p, o_gm_s),
        (o_me_p.reshape(n_p, MEM_WIDTH), o_me_s.reshape(n_s, MEM_WIDTH)), sg,
        w_br_rwkv[l].astype(BF16), w_br_gmlp[l].astype(BF16), w_br_mem[l].astype(BF16), w_out[l].astype(BF16),
        row(g_norm_ffn[l]), wrt_pad, brt_pad)

    n_assign = n_all * TOP_K
    n_blocks = -(-(n_assign + N_EXPERTS * (MOE_BM - 1)) // MOE_BM)
    dest, blk_e, n_used, nxt_e, parity, zlist = _slot_tables(route_t, counts[0], n_blocks)
    xs = _dispatch(dest, zlist, h2.reshape(n_all, ROW_SUB, LANES), n_blocks)
    yb = _moe(blk_e, n_used, nxt_e, parity, xs.reshape(n_blocks * MOE_BM * ROW_SUB, LANES), w_exp1[l],
              b_exp1[l].reshape(N_EXPERTS, 1, 2 * D_FF), w_exp2[l], b_exp2[l].reshape(N_EXPERTS, 1, d), n_blocks)
    y_p, y_s = _combine(dest, x2, route, row(g_norm_final), yb, n_p)

    mk_out = mk_p.reshape(1, bp, N_MEM, MEM_HEADS, MEM_WIDTH // MEM_HEADS)
    mv_out = mv_p.reshape(1, bp, N_MEM, MEM_HEADS, MEM_WIDTH // MEM_HEADS)
    return (y_p.reshape(bp, tp, d), y_s.reshape(bs, ts, d), shift_p.reshape(1, bp, SHIFT_WIDTH), s_p[None], mk_out,
            mv_out, shift_s.reshape(1, bs, SHIFT_WIDTH), s_s[None],
            v_rows_s.reshape(1, bs, ts, GMLP_WIDTH))
```

```python
import functools
import math

import jax
import jax.numpy as jnp
from jax import lax
from jax.experimental import pallas as pl
from jax.experimental.pallas import tpu as pltpu

F32 = jnp.float32
BF16 = jnp.bfloat16

D_MODEL = 1024
RWKV_HEADS = 8
HEAD_DIM = 64
RWKV_WIDTH = RWKV_HEADS * HEAD_DIM
DECAY_LORA = 64
A_LORA = 64
GATE_LORA = 128
GMLP_GROUPS = 4
GMLP_WIDTH = 256
CHUNK = 128
MEM_HEADS = 4
MEM_WIDTH = 256
N_MEM = 256
N_EXPERTS = 32
TOP_K = 4
D_FF = 1024
SWIGLU_ALPHA = 1.702
SWIGLU_LIMIT = 7.0
RMS_EPS = 1e-5
LN_EPS = 1e-5
GN_EPS = 64e-5
SHIFT_WIDTH = 3 * RWKV_WIDTH + DECAY_LORA + A_LORA + GATE_LORA
OFF_GMLP = SHIFT_WIDTH
OFF_QMEM = OFF_GMLP + 2 * GMLP_WIDTH
OFF_GATE = OFF_QMEM + MEM_WIDTH
IN_WIDTH = OFF_GATE + 3 * D_MODEL
LANES = 128
ROW_SUB = D_MODEL // LANES

PROJ_TM = 512
MERGE_TM = 512
RWKV_TT = 512
RWKV_C = 64
RWKV_GROUP = 8
RWKV_SAMPLE_SEQS = 8
GMLP_TILE = 4
ATTN_TT = 512
ATTN_BB = 8
MOE_BM = 512
DISPATCH_TM = 1024
COMBINE_TM = 128

ROUTE_E = 0
ROUTE_RANK = 4
ROUTE_GATE = 8

NN = ((1,), (0,))
NT = ((1,), (1,))
TN = ((0,), (0,))

VMEM_LIMIT = 56 * 1024 * 1024


def _params(sem, vmem=VMEM_LIMIT):
    return pltpu.CompilerParams(dimension_semantics=sem, vmem_limit_bytes=vmem)


def _bdot(a, b, dims=NN):
    return lax.dot_general(a.astype(BF16), b.astype(BF16), (dims, ((), ())), preferred_element_type=F32)


def _split(x):
    hi = x.astype(BF16)
    lo = (x - hi.astype(F32)).astype(BF16)
    return hi, lo


def _dot3(a, b, dims=NN):
    dn = (dims, ((), ()))
    ah, al = _split(a)
    bh, bl = _split(b)
    r = lax.dot_general(ah, bh, dn, preferred_element_type=F32)
    r = r + lax.dot_general(al, bh, dn, preferred_element_type=F32)
    return r + lax.dot_general(ah, bl, dn, preferred_element_type=F32)


def _rms(x, g):
    return x * lax.rsqrt(jnp.mean(x * x, axis=-1, keepdims=True) + RMS_EPS) * g


def _store_row_tiles(ref, x):
    m = x.shape[0]
    for j in range(ROW_SUB):
        ref[pl.ds(j, m, stride=ROW_SUB), :] = x[:, j * LANES:(j + 1) * LANES]


def _load_row_tiles(ref, m):
    return jnp.concatenate([ref[pl.ds(j, m, stride=ROW_SUB), :] for j in range(ROW_SUB)], axis=1)


def _sigmoid(x):
    return 0.5 * jnp.tanh(0.5 * x) + 0.5


def _in_proj_body(xp_ref, xs_ref, g_ref, w_ref, zr_ref, zg_ref, zq_ref, sg_ref, *, np_tiles):
    x = jnp.where(pl.program_id(0) < np_tiles, xp_ref[...], xs_ref[...])
    h = _rms(x, g_ref[...]).astype(BF16)
    zr_ref[...] = jnp.dot(h, w_ref[:, 0:OFF_GMLP], preferred_element_type=F32)
    zg_ref[...] = jnp.dot(h, w_ref[:, OFF_GMLP:OFF_QMEM], preferred_element_type=F32)
    zq_ref[...] = jnp.dot(h, w_ref[:, OFF_QMEM:OFF_GATE], preferred_element_type=F32).astype(zq_ref.dtype)
    gates = jnp.dot(h, w_ref[:, OFF_GATE:IN_WIDTH], preferred_element_type=F32)
    sg_ref[...] = _sigmoid(gates).astype(BF16)


def _in_proj(x_p, x_s, g, w_bf16):
    n_p = x_p.shape[0]
    n = n_p + x_s.shape[0]
    tm = PROJ_TM
    np_tiles = n_p // tm
    row = lambda i: (i, 0)
    const = lambda i: (0, 0)
    return pl.pallas_call(
        functools.partial(_in_proj_body, np_tiles=np_tiles),
        grid=(n // tm,),
        in_specs=[
            pl.BlockSpec((tm, D_MODEL), lambda i: (jnp.minimum(i, np_tiles - 1), 0)),
            pl.BlockSpec((tm, D_MODEL), lambda i: (jnp.maximum(i - np_tiles, 0), 0)),
            pl.BlockSpec((1, D_MODEL), const),
            pl.BlockSpec((D_MODEL, IN_WIDTH), const, pipeline_mode=pl.Buffered(1)),
        ],
        out_specs=[
            pl.BlockSpec((tm, SHIFT_WIDTH), row),
            pl.BlockSpec((tm, 2 * GMLP_WIDTH), row),
            pl.BlockSpec((tm, MEM_WIDTH), row),
            pl.BlockSpec((tm, 3 * D_MODEL), row),
        ],
        out_shape=[
            jax.ShapeDtypeStruct((n, SHIFT_WIDTH), F32),
            jax.ShapeDtypeStruct((n, 2 * GMLP_WIDTH), F32),
            jax.ShapeDtypeStruct((n, MEM_WIDTH), BF16),
            jax.ShapeDtypeStruct((n, 3 * D_MODEL), BF16),
        ],
        compiler_params=_params(("parallel",)),
        name="in_proj",
    )(x_p, x_s, g, w_bf16)


def _mem_kv_body(x_ref, g_ref, w_ref, k_ref, v_ref):
    h = _rms(x_ref[...], g_ref[...]).astype(BF16)
    k_ref[...] = jnp.dot(h, w_ref[:, 0:MEM_WIDTH], preferred_element_type=F32)
    v_ref[...] = jnp.dot(h, w_ref[:, MEM_WIDTH:2 * MEM_WIDTH], preferred_element_type=F32)


def _mem_kv(mem, g, w_bf16):
    n = mem.shape[0]
    tm = PROJ_TM
    return pl.pallas_call(
        _mem_kv_body,
        grid=(n // tm,),
        in_specs=[
            pl.BlockSpec((tm, D_MODEL), lambda i: (i, 0)),
            pl.BlockSpec((1, D_MODEL), lambda i: (0, 0)),
            pl.BlockSpec((D_MODEL, 2 * MEM_WIDTH), lambda i: (0, 0)),
        ],
        out_specs=[pl.BlockSpec((tm, MEM_WIDTH), lambda i: (i, 0))] * 2,
        out_shape=[jax.ShapeDtypeStruct((n, MEM_WIDTH), F32)] * 2,
        compiler_params=_params(("parallel",)),
        name="mem_kv",
    )(mem, g, w_bf16)


EXP_M05 = math.exp(-0.5)


def _unit_lower_inverse(lows, c):
    rows = lax.broadcasted_iota(jnp.int32, (c, c), 0)
    cols = lax.broadcasted_iota(jnp.int32, (c, c), 1)
    eye = (rows == cols).astype(F32)
    invs = [eye - low for low in lows]
    powers = lows
    for _ in range(int(math.log2(c)) - 1):
        powers = [_bdot(pw, pw) for pw in powers]
        invs = [inv + _bdot(inv, pw) for inv, pw in zip(invs, powers)]
    return invs


def _rwkv_body(zr_ref, shift_ref, s0_ref, mu_ref, w0_ref, wd_ref, a0_ref, wa_ref, wg_ref, kk_ref, ka_ref,
               rk_ref, lng_ref, lnb_ref, hsum_ref,
               o_ref, sout_ref, shout_ref,
               carry_ref, state_ref, r_s, k_s, v_s, kk_s, b_s, ld_s, y_s, bon_s, g_s,
               rw_s, y0_s, gm_s, h0_s, *, tt, c, nseq):
    i = pl.program_id(1)
    rw = RWKV_WIDTH
    nh = RWKV_HEADS

    @pl.when(i == 0)
    def _():
        for q in range(nseq):
            state_ref[q * nh:(q + 1) * nh] = s0_ref[q]

    z = zr_ref[0]
    z_prev = pltpu.roll(z, 1, 0)
    row = lax.broadcasted_iota(jnp.int32, z.shape, 0)
    if nseq == 1:
        @pl.when(i == 0)
        def _():
            carry_ref[...] = shift_ref[0]

        z_prev = jnp.where(row == 0, carry_ref[...], z_prev)
        carry_ref[...] = z[tt - 1:tt, :]
    else:
        first_rows = jnp.concatenate([jnp.broadcast_to(shift_ref[q], (c, SHIFT_WIDTH)) for q in range(nseq)], axis=0)
        z_prev = jnp.where(row % c == 0, first_rows, z_prev)
    zs = z + mu_ref[...] * (z_prev - z)
    r = zs[:, 0:rw]
    k = zs[:, rw:2 * rw]
    v = zs[:, 2 * rw:3 * rw]
    zw = zs[:, 3 * rw:3 * rw + DECAY_LORA]
    za = zs[:, 3 * rw + DECAY_LORA:3 * rw + DECAY_LORA + A_LORA]
    zg = zs[:, 3 * rw + DECAY_LORA + A_LORA:SHIFT_WIDTH]
    xw = w0_ref[...] + _bdot(jnp.tanh(zw), wd_ref[...])
    ld_s[...] = -EXP_M05 * _sigmoid(xw)
    a = _sigmoid(a0_ref[...] + _bdot(za, wa_ref[...]))
    g_s[...] = _bdot(_sigmoid(zg), wg_ref[...])
    kk = k * kk_ref[...]
    k = k * (1.0 + (a - 1.0) * ka_ref[...])
    r_s[...] = r
    k_s[...] = k
    v_s[...] = v

    def head_sum(t):
        return jnp.dot(t.astype(BF16), hsum_ref[...], preferred_element_type=F32)

    kk = kk / jnp.maximum(jnp.sqrt(head_sum(kk * kk)), 1e-12)
    kk_s[...] = kk
    b_s[...] = kk * a
    bon_s[...] = head_sum(r * k * rk_ref[...]) * v

    rows_c = lax.broadcasted_iota(jnp.int32, (c, c), 0)
    cols_c = lax.broadcasted_iota(jnp.int32, (c, c), 1)
    strict = rows_c > cols_c
    incl = rows_c >= cols_c
    tril_ones = jnp.where(incl, 1.0, 0.0).astype(BF16)
    rows_2c = lax.broadcasted_iota(jnp.int32, (c, 2 * c), 0)
    cols_2c = lax.broadcasted_iota(jnp.int32, (c, 2 * c), 1)
    incl2 = rows_2c >= jnp.where(cols_2c >= c, cols_2c - c, cols_2c)
    rows_k = lax.broadcasted_iota(jnp.int32, (HEAD_DIM, HEAD_DIM), 0)
    cols_k = lax.broadcasted_iota(jnp.int32, (HEAD_DIM, HEAD_DIM), 1)
    eye_k = (rows_k == cols_k).astype(F32)
    zeros_cv = jnp.zeros((c, HEAD_DIM), F32)
    heads = [slice(h * HEAD_DIM, (h + 1) * HEAD_DIM) for h in range(RWKV_HEADS)]

    n_chunks = tt // c
    group = nseq if nseq > 1 else math.gcd(n_chunks, RWKV_GROUP)

    def chunk_rows(ci):
        return pl.ds(ci * c, c) if isinstance(ci, int) else pl.ds(pl.multiple_of(ci * c, c), c)

    def scaled(ci):
        rows = chunk_rows(ci)
        ld = ld_s[rows, :]
        ld_hi, ld_lo = _split(ld)
        cum = (jnp.dot(tril_ones, ld_hi, preferred_element_type=F32)
               + jnp.dot(tril_ones, ld_lo, preferred_element_type=F32))
        e_inc = jnp.exp(cum)
        e_neg = jnp.exp(-cum)
        kt = k_s[rows, :] * e_neg
        bt = b_s[rows, :] * e_neg
        g_end = e_inc[c - 1:c, :]
        return dict(ci=ci, rt=r_s[rows, :] * e_inc, kkt=kk_s[rows, :] * jnp.exp(cum - ld), kt=kt, bt=bt,
                    g_end=g_end, bc=bt * g_end, kc=kt * g_end, vv=v_s[rows, :])

    def phase1(cj, carry):
        chains = [(ch, h, s) for ch in [scaled(cj * group + g) for g in range(group)]
                  for h, s in enumerate(heads)]
        x = [jnp.concatenate([ch["kkt"][:, s], ch["rt"][:, s]], axis=0) for ch, _, s in chains]
        zz = [jnp.concatenate([ch["bt"][:, s], ch["kt"][:, s]], axis=0) for ch, _, s in chains]
        amat = [_bdot(xh, zh, NT) for xh, zh in zip(x, zz)]
        l_b = [jnp.where(strict, am[0:c, 0:c], 0.0) for am in amat]
        l_k = [jnp.where(strict, am[0:c, c:2 * c], 0.0) for am in amat]
        a_r = [jnp.where(incl2, am[c:2 * c, :], 0.0) for am in amat]
        tinv = _unit_lower_inverse(l_b, c)
        lkv = [_bdot(lk, ch["vv"][:, s]) for lk, (ch, _, s) in zip(l_k, chains)]
        wu = [-_bdot(t, jnp.concatenate([ch["kkt"][:, s], lv], axis=1))
              for t, lv, (ch, _, s) in zip(tinv, lkv, chains)]
        m = [jnp.concatenate([w, jnp.concatenate([zeros_cv, ch["vv"][:, s]], axis=1)], axis=0)
             for w, (ch, _, s) in zip(wu, chains)]
        am2 = [_bdot(ar, mh) for ar, mh in zip(a_r, m)]
        gh = [_bdot(mh, jnp.concatenate([ch["bc"][:, s], ch["kc"][:, s]], axis=0), TN)
              for mh, (ch, _, s) in zip(m, chains)]
        for q, (ch, h, s) in enumerate(chains):
            idx = ch["ci"] * RWKV_HEADS + h
            rw_s[idx] = ch["rt"][:, s] + am2[q][:, 0:HEAD_DIM]
            y0_s[idx] = am2[q][:, HEAD_DIM:2 * HEAD_DIM]
            gm_s[idx] = gh[q][0:HEAD_DIM, :] + eye_k * ch["g_end"][:, s]
            h0_s[idx] = gh[q][HEAD_DIM:2 * HEAD_DIM, :]
        return carry

    if n_chunks == group:
        phase1(0, 0)
    else:
        lax.fori_loop(0, n_chunks // group, phase1, 0)

    if nseq == 1:
        def phase2(ci, carry):
            rows = chunk_rows(ci)
            for h, s in enumerate(heads):
                idx = ci * nh + h
                s_h = state_ref[h]
                y_s[rows, s] = _bdot(rw_s[idx], s_h, NT) + y0_s[idx]
                state_ref[h] = _bdot(s_h, gm_s[idx]) + h0_s[idx]
            return carry

        lax.fori_loop(0, n_chunks, phase2, 0)
    else:
        pairs = [(q, h, s) for q in range(nseq) for h, s in enumerate(heads)]
        states = [state_ref[q * nh + h] for q, h, _ in pairs]
        ys = [_bdot(rw_s[q * nh + h], st, NT) + y0_s[q * nh + h] for (q, h, _), st in zip(pairs, states)]
        new = [_bdot(st, gm_s[q * nh + h]) + h0_s[q * nh + h] for (q, h, _), st in zip(pairs, states)]
        for (q, h, s), yq, nq in zip(pairs, ys, new):
            y_s[chunk_rows(q), s] = yq
            state_ref[q * nh + h] = nq
    y = y_s[...]
    yc = y - head_sum(y) * (1.0 / HEAD_DIM)
    var = head_sum(yc * yc) * (1.0 / HEAD_DIM)
    yn = yc * lax.rsqrt(var + GN_EPS)
    o_ref[0] = ((yn * lng_ref[...] + lnb_ref[...] + bon_s[...]) * g_s[...]).astype(o_ref.dtype)

    @pl.when(i == pl.num_programs(1) - 1)
    def _():
        for q in range(nseq):
            sout_ref[q] = state_ref[q * nh:(q + 1) * nh]
            last = tt - 1 if nseq == 1 else (q + 1) * c - 1
            shout_ref[q] = zr_ref[0, last:last + 1, :]


def _rwkv(zr3, blk_off, b, nt, shift_prev, s_prev, p, tt, c, nseq):
    vec = lambda n: pl.BlockSpec((1, n), lambda bi, ti: (0, 0))
    mat = lambda m, n: pl.BlockSpec((m, n), lambda bi, ti: (0, 0))
    rw = RWKV_WIDTH
    scr = lambda: pltpu.VMEM((tt, rw), F32)
    nch = (tt // c) * RWKV_HEADS
    return pl.pallas_call(
        functools.partial(_rwkv_body, tt=tt, c=c, nseq=nseq),
        grid=(b // nseq, nt),
        in_specs=[
            pl.BlockSpec((1, tt, SHIFT_WIDTH), lambda bi, ti: (blk_off + bi * nt + ti, 0, 0)),
            pl.BlockSpec((nseq, 1, SHIFT_WIDTH), lambda bi, ti: (bi, 0, 0)),
            pl.BlockSpec((nseq, RWKV_HEADS, HEAD_DIM, HEAD_DIM), lambda bi, ti: (bi, 0, 0, 0)),
            vec(SHIFT_WIDTH), vec(rw), mat(DECAY_LORA, rw), vec(rw), mat(A_LORA, rw), mat(GATE_LORA, rw),
            vec(rw), vec(rw), vec(rw), vec(rw), vec(rw), mat(rw, rw),
        ],
        out_specs=[
            pl.BlockSpec((1, tt, rw), lambda bi, ti: (bi, ti, 0)),
            pl.BlockSpec((nseq, RWKV_HEADS, HEAD_DIM, HEAD_DIM), lambda bi, ti: (bi, 0, 0, 0)),
            pl.BlockSpec((nseq, 1, SHIFT_WIDTH), lambda bi, ti: (bi, 0, 0)),
        ],
        out_shape=[
            jax.ShapeDtypeStruct((b // nseq, nt * tt, rw), BF16),
            jax.ShapeDtypeStruct((b, RWKV_HEADS, HEAD_DIM, HEAD_DIM), F32),
            jax.ShapeDtypeStruct((b, 1, SHIFT_WIDTH), F32),
        ],
        scratch_shapes=[
            pltpu.VMEM((1, SHIFT_WIDTH), F32),
            pltpu.VMEM((nseq * RWKV_HEADS, HEAD_DIM, HEAD_DIM), F32),
            scr(), scr(), scr(), scr(), scr(), scr(), scr(), scr(), scr(),
            pltpu.VMEM((nch, c, HEAD_DIM), F32),
            pltpu.VMEM((nch, c, HEAD_DIM), F32),
            pltpu.VMEM((nch, HEAD_DIM, HEAD_DIM), F32),
            pltpu.VMEM((nch, HEAD_DIM, HEAD_DIM), F32),
        ],
        compiler_params=_params(("parallel", "arbitrary")),
        name="rwkv",
    )(zr3, shift_prev, s_prev, p["mu"], p["w0"], p["wd"], p["a0"], p["wa"], p["wg"], p["kk"], p["ka"],
      p["rk"], p["lng"], p["lnb"], p["hsum"])


def _gmlp_body(z_ref, lng_ref, lnb_ref, ws_ref, bs_ref, o_ref, v_ref):
    z = z_ref[...]
    ge = 0.5 * z * (1.0 + lax.erf(z * (1.0 / math.sqrt(2.0))))
    u = ge[:, 0:GMLP_WIDTH]
    v = ge[:, GMLP_WIDTH:2 * GMLP_WIDTH]
    mean = jnp.mean(v, axis=-1, keepdims=True)
    vc = v - mean
    var = jnp.mean(vc * vc, axis=-1, keepdims=True)
    vn = vc * lax.rsqrt(var + LN_EPS) * lng_ref[...] + lnb_ref[...]
    v_ref[...] = vn
    gd = GMLP_WIDTH // GMLP_GROUPS
    for q in range(GMLP_TILE):
        rows = slice(q * CHUNK, (q + 1) * CHUNK)
        for g in range(GMLP_GROUPS):
            gs = slice(g * gd, (g + 1) * gd)
            sv = jnp.dot(ws_ref[g], vn[rows, gs].astype(BF16), preferred_element_type=F32) + bs_ref[:, g:g + 1]
            o_ref[rows, gs] = (u[rows, gs] * sv).astype(o_ref.dtype)


def _gmlp(zg, tile_off, n_rows, lng, lnb, ws_bf16, bs):
    tile = GMLP_TILE * CHUNK
    return pl.pallas_call(
        _gmlp_body,
        grid=(n_rows // tile,),
        in_specs=[
            pl.BlockSpec((tile, 2 * GMLP_WIDTH), lambda i: (tile_off + i, 0)),
            pl.BlockSpec((1, GMLP_WIDTH), lambda i: (0, 0)),
            pl.BlockSpec((1, GMLP_WIDTH), lambda i: (0, 0)),
            pl.BlockSpec((GMLP_GROUPS, CHUNK, CHUNK), lambda i: (0, 0, 0)),
            pl.BlockSpec((CHUNK, GMLP_GROUPS), lambda i: (0, 0)),
        ],
        out_specs=[
            pl.BlockSpec((tile, GMLP_WIDTH), lambda i: (i, 0)),
            pl.BlockSpec((tile, GMLP_WIDTH), lambda i: (i, 0)),
        ],
        out_shape=[
            jax.ShapeDtypeStruct((n_rows, GMLP_WIDTH), BF16),
            jax.ShapeDtypeStruct((n_rows, GMLP_WIDTH), F32),
        ],
        compiler_params=_params(("parallel",)),
        name="gmlp",
    )(zg, lng, lnb, ws_bf16, bs)


def _mem_attn_body(q_ref, k_ref, v_ref, o_ref, *, bb, feature_major):
    hd = MEM_WIDTH // MEM_HEADS
    pairs = [(s, slice(h * hd, (h + 1) * hd)) for s in range(bb) for h in range(MEM_HEADS)]
    if feature_major:
        scores = [_bdot(q_ref[s, :, hs], k_ref[s, hs, :], NN) * (hd ** -0.5) for s, hs in pairs]
    else:
        scores = [_bdot(q_ref[s, :, hs], k_ref[s, :, hs], NT) * (hd ** -0.5) for s, hs in pairs]
    probs = []
    for sc in scores:
        pr = jnp.exp(sc - jnp.max(sc, axis=-1, keepdims=True))
        probs.append(pr / jnp.sum(pr, axis=-1, keepdims=True))
    for (s, hs), pr in zip(pairs, probs):
        if feature_major:
            o_ref[s, :, hs] = _bdot(pr, v_ref[s, hs, :], NT).astype(o_ref.dtype)
        else:
            o_ref[s, :, hs] = _bdot(pr, v_ref[s, :, hs]).astype(o_ref.dtype)


def _mem_attn(q3, blk_off, b, nt, mk, mv, tt, bb, feature_major):
    return pl.pallas_call(
        functools.partial(_mem_attn_body, bb=bb, feature_major=feature_major),
        grid=(b // bb, nt),
        in_specs=[
            pl.BlockSpec((bb, tt, MEM_WIDTH), lambda bi, ti: (blk_off // bb + bi * nt + ti, 0, 0)),
            pl.BlockSpec((bb, N_MEM, MEM_WIDTH), lambda bi, ti: (bi, 0, 0)),
            pl.BlockSpec((bb, N_MEM, MEM_WIDTH), lambda bi, ti: (bi, 0, 0)),
        ],
        out_specs=pl.BlockSpec((bb, tt, MEM_WIDTH), lambda bi, ti: (bi, ti, 0)),
        out_shape=jax.ShapeDtypeStruct((b, nt * tt, MEM_WIDTH), BF16),
        compiler_params=_params(("parallel", "parallel")),
        name="mem_attn",
    )(q3, mk, mv)


def _merge_body(x_p, x_s, orw_p, orw_s, ogm_p, ogm_s, ome_p, ome_s, sg_ref, wbr_ref, wbg_ref, wbm_ref, wout_ref,
                gffn_ref, wrt_ref, brt_ref, x2_ref, h2_ref, route_ref, route_t_ref, cnt_ref, count_s, *, np_tiles):
    i = pl.program_id(0)
    d = D_MODEL
    tm = x_p.shape[0]
    is_p = i < np_tiles
    x = jnp.where(is_p, x_p[...], x_s[...])
    orw = jnp.where(is_p, orw_p[...], orw_s[...])
    ogm = jnp.where(is_p, ogm_p[...], ogm_s[...])
    ome = jnp.where(is_p, ome_p[...], ome_s[...])
    merged = sg_ref[:, 0:d].astype(F32) * _bdot(orw, wbr_ref[...])
    merged = merged + sg_ref[:, d:2 * d].astype(F32) * _bdot(ogm, wbg_ref[...])
    merged = merged + sg_ref[:, 2 * d:3 * d].astype(F32) * _bdot(ome, wbm_ref[...])
    x2 = x + _bdot(merged, wout_ref[...])
    x2_ref[...] = x2
    h2 = _rms(x2, gffn_ref[...])
    _store_row_tiles(h2_ref, h2)

    @pl.when(i == 0)
    def _():
        count_s[...] = jnp.zeros_like(count_s)

    lane = lax.broadcasted_iota(jnp.int32, (tm, LANES), 1)
    logits = jnp.where(lane < N_EXPERTS, _dot3(h2, wrt_ref[...]) + brt_ref[...], -jnp.inf)
    lane_f = lane.astype(F32)
    tops, hots, idxs = [], [], []
    for _ in range(TOP_K):
        top = jnp.max(logits, axis=-1, keepdims=True)
        idx = jnp.min(jnp.where(logits == top, lane_f, float(LANES)), axis=-1, keepdims=True)
        hot = lane_f == idx
        logits = jnp.where(hot, -jnp.inf, logits)
        tops.append(top)
        hots.append(hot)
        idxs.append(idx)
    weights = [jnp.exp(t - tops[0]) for t in tops]
    denom = weights[0] + weights[1] + weights[2] + weights[3]
    onehot = jnp.zeros((tm, LANES), F32)
    for hot in hots:
        onehot = onehot + jnp.where(hot, 1.0, 0.0)
    rows_t = lax.broadcasted_iota(jnp.int32, (tm, tm), 0)
    cols_t = lax.broadcasted_iota(jnp.int32, (tm, tm), 1)
    ahead = jnp.where(rows_t > cols_t, 1.0, 0.0).astype(BF16)
    prefix = jnp.dot(ahead, onehot.astype(BF16), preferred_element_type=F32) + count_s[...]
    route = jnp.zeros((tm, LANES), F32)
    for kk in range(TOP_K):
        rank = jnp.sum(jnp.where(hots[kk], prefix, 0.0), axis=-1, keepdims=True)
        route = jnp.where(lane == ROUTE_E + kk, idxs[kk], route)
        route = jnp.where(lane == ROUTE_RANK + kk, rank, route)
        route = jnp.where(lane == ROUTE_GATE + kk, weights[kk] / denom, route)
    route_ref[...] = route
    route_t_ref[...] = route.T
    count_s[...] = count_s[...] + jnp.sum(onehot, axis=0, keepdims=True)
    cnt_ref[...] = jnp.broadcast_to(count_s[...], cnt_ref.shape)


def _merge(x, o_rw, o_gm, o_me, sg, wbr, wbg, wbm, wout, gffn, wrt_pad, brt_pad):
    n_p = x[0].shape[0]
    n = n_p + x[1].shape[0]
    tm = MERGE_TM
    d = D_MODEL
    np_tiles = n_p // tm
    row = lambda i: (i, 0)
    const = lambda i: (0, 0)
    first =lambda i: (jnp.minimum(i, np_tiles - 1), 0)
    second = lambda i: (jnp.maximum(i - np_tiles, 0), 0)

    def pair(width):
        return [pl.BlockSpec((tm, width), first), pl.BlockSpec((tm, width), second)]

    return pl.pallas_call(
        functools.partial(_merge_body, np_tiles=np_tiles),
        grid=(n // tm,),
        in_specs=pair(d) + pair(RWKV_WIDTH) + pair(GMLP_WIDTH) + pair(MEM_WIDTH) + [
            pl.BlockSpec((tm, 3 * d), row),
            pl.BlockSpec((RWKV_WIDTH, d), const),
            pl.BlockSpec((GMLP_WIDTH, d), const),
            pl.BlockSpec((MEM_WIDTH, d), const),
            pl.BlockSpec((d, d), const),
            pl.BlockSpec((1, d), const),
            pl.BlockSpec((d, LANES), const),
            pl.BlockSpec((1, LANES), const),
        ],
        out_specs=[
            pl.BlockSpec((tm, d), row),
            pl.BlockSpec((tm * ROW_SUB, LANES), row),
            pl.BlockSpec((tm, LANES), row),
            pl.BlockSpec((LANES, tm), lambda i: (0, i)),
            pl.BlockSpec((8, LANES), const),
        ],
        out_shape=[
            jax.ShapeDtypeStruct((n, d), F32),
            jax.ShapeDtypeStruct((n * ROW_SUB, LANES), F32),
            jax.ShapeDtypeStruct((n, LANES), F32),
            jax.ShapeDtypeStruct((LANES, n), F32),
            jax.ShapeDtypeStruct((8, LANES), F32),
        ],
        scratch_shapes=[pltpu.VMEM((1, LANES), F32)],
        compiler_params=_params(("arbitrary",)),
        name="merge",
    )(x[0], x[1], o_rw[0], o_rw[1], o_gm[0], o_gm[1], o_me[0], o_me[1], sg, wbr, wbg, wbm, wout, gffn, wrt_pad,
      brt_pad)


N_ZERO_BLOCKS = 2 * N_EXPERTS


def _dispatch_body(dest_ref, zlist_ref, h_ref, xs_hbm, zero_s, sem_z, sem):
    i = pl.program_id(0)
    tm = DISPATCH_TM
    bm = MOE_BM

    @pl.when(i == 0)
    def _():
        zero_s[...] = jnp.zeros_like(zero_s)

        def zero_copy(q):
            start = pl.multiple_of(zlist_ref[q] * bm, bm)
            return pltpu.make_async_copy(zero_s, xs_hbm.at[pl.ds(start, bm)], sem_z)

        def start(q, carry):
            @pl.when(zlist_ref[q] >= 0)
            def _():
                zero_copy(q).start()
            return carry

        def wait(q, carry):
            @pl.when(zlist_ref[q] >= 0)
            def _():
                zero_copy(q).wait()
            return carry

        lax.fori_loop(0, N_ZERO_BLOCKS, start, 0)
        lax.fori_loop(0, N_ZERO_BLOCKS, wait, 0)

    base = i * tm
    n_tok = pl.num_programs(0) * tm

    def body(r, carry):
        for kk in range(TOP_K):
            slot = dest_ref[kk * n_tok + base + r]
            pltpu.make_async_copy(h_ref.at[r], xs_hbm.at[slot], sem).start(priority=kk % 2)
        return carry

    lax.fori_loop(0, tm, body, 0, unroll=2)
    for kk in range(TOP_K):
        pltpu.make_async_copy(h_ref, xs_hbm.at[pl.ds(0, tm)], sem).wait()


def _dispatch(dest, zlist, h, n_blocks):
    n = h.shape[0]
    tm = DISPATCH_TM
    grid_spec = pltpu.PrefetchScalarGridSpec(
        num_scalar_prefetch=2,
        grid=(n // tm,),
        in_specs=[pl.BlockSpec((tm, ROW_SUB, LANES), lambda i, dest, zl: (i, 0, 0))],
        out_specs=pl.BlockSpec(memory_space=pl.ANY),
        scratch_shapes=[
            pltpu.VMEM((MOE_BM, ROW_SUB, LANES), F32),
            pltpu.SemaphoreType.DMA(()),
            pltpu.SemaphoreType.DMA(()),
        ],
    )
    return pl.pallas_call(
        _dispatch_body,
        grid_spec=grid_spec,
        out_shape=jax.ShapeDtypeStruct((n_blocks * MOE_BM, ROW_SUB, LANES), F32),
        compiler_params=_params(("arbitrary",)),
        name="moe_dispatch",
    )(dest, zlist, h)


CAST_ROWS = 64


def _moe_body(be_ref, nused_ref, nxt_ref, par_ref, x_ref, w1_hbm, b1_ref, w2_hbm, b2_ref, o_ref,
              w1f, w2f, w1b, w2b, sem):
    j = pl.program_id(0)
    n_used = nused_ref[0]

    def weight_copies(expert, slot):
        return (pltpu.make_async_copy(w1_hbm.at[expert], w1f.at[slot], sem.at[slot, 0]),
                pltpu.make_async_copy(w2_hbm.at[expert], w2f.at[slot], sem.at[slot, 1]))

    @pl.when(j == 0)
    def _():
        for cp in weight_copies(be_ref[0], 0):
            cp.start()

    @pl.when(j < n_used)
    def _():
        slot = par_ref[be_ref[j]]
        changed = jnp.logical_or(j == 0, be_ref[j] != be_ref[jnp.maximum(j - 1, 0)])

        @pl.when(changed)
        def _():
            for cp in weight_copies(be_ref[j], slot):
                cp.wait()

            nxt = nxt_ref[be_ref[j]]

            @pl.when(nxt >= 0)
            def _():
                for cp in weight_copies(nxt, 1 - slot):
                    cp.start()

            def cast1(q, carry):
                rows = pl.ds(pl.multiple_of(q * CAST_ROWS, CAST_ROWS), CAST_ROWS)
                w1b[rows, :] = w1f[slot, rows, :].astype(BF16)
                return carry

            def cast2(q, carry):
                rows = pl.ds(pl.multiple_of(q * CAST_ROWS, CAST_ROWS), CAST_ROWS)
                w2b[rows, :] = w2f[slot, rows, :].astype(BF16)
                return carry

            lax.fori_loop(0, D_MODEL // CAST_ROWS, cast1, 0)
            lax.fori_loop(0, D_FF // CAST_ROWS, cast2, 0)

        x = _load_row_tiles(x_ref, MOE_BM).astype(BF16)
        z = jnp.dot(x, w1b[...], preferred_element_type=F32) + b1_ref[0]
        zg = jnp.minimum(z[:, 0:D_FF], SWIGLU_LIMIT)
        zl = jnp.clip(z[:, D_FF:2 * D_FF], -SWIGLU_LIMIT, SWIGLU_LIMIT)
        act = zg * _sigmoid(SWIGLU_ALPHA * zg) * (zl + 1.0)
        _store_row_tiles(o_ref, jnp.dot(act.astype(BF16), w2b[...], preferred_element_type=F32) + b2_ref[0])

    @pl.when(j >= n_used)
    def _():
        o_ref[...] = jnp.zeros_like(o_ref)


def _moe(blk_e, n_used, nxt_e, parity, xs, w1, b1, w2, b2, n_blocks):
    bm = MOE_BM
    d = D_MODEL
    grid_spec = pltpu.PrefetchScalarGridSpec(
        num_scalar_prefetch=4,
        grid=(n_blocks,),
        in_specs=[
            pl.BlockSpec((bm * ROW_SUB, LANES), lambda j, be, nu, nx, pa: (j, 0)),
            pl.BlockSpec(memory_space=pl.ANY),
            pl.BlockSpec((1, 1, 2 * D_FF), lambda j, be, nu, nx, pa: (be[j], 0, 0)),
            pl.BlockSpec(memory_space=pl.ANY),
            pl.BlockSpec((1, 1, d), lambda j, be, nu, nx, pa: (be[j], 0, 0)),
        ],
        out_specs=pl.BlockSpec((bm * ROW_SUB, LANES), lambda j, be, nu, nx, pa: (j, 0)),
        scratch_shapes=[
            pltpu.VMEM((2, d, 2 * D_FF), F32),
            pltpu.VMEM((2, D_FF, d), F32),
            pltpu.VMEM((d, 2 * D_FF), BF16),
            pltpu.VMEM((D_FF, d), BF16),
            pltpu.SemaphoreType.DMA((2, 2)),
        ],
    )
    return pl.pallas_call(
        _moe_body,
        grid_spec=grid_spec,
        out_shape=jax.ShapeDtypeStruct((n_blocks * bm * ROW_SUB, LANES), F32),
        compiler_params=_params(("arbitrary",)),
        name="moe_ffn",
    )(blk_e, n_used, nxt_e, parity, xs, w1, b1, w2, b2)


def _combine_body(dest_ref, x2_ref, route_ref, gfin_ref, yb_hbm, op_ref, os_ref, buf0, buf1, sem, *, np_steps):
    i = pl.program_id(0)
    n_steps = pl.num_programs(0)
    tm = COMBINE_TM
    n_tok = n_steps * (2 * tm)
    bufs = (buf0, buf1)

    def issue(tile, which):
        base = tile * tm
        for r in range(tm):
            for kk in range(TOP_K):
                slot = dest_ref[kk * n_tok + base + r]
                src = pl.ds(pl.multiple_of(slot * ROW_SUB, ROW_SUB), ROW_SUB)
                pltpu.make_async_copy(yb_hbm.at[src], bufs[which].at[kk, r * ROW_SUB:(r + 1) * ROW_SUB],
                                      sem.at[which]).start(priority=kk % 2)

    def wait(which):
        for kk in range(TOP_K):
            pltpu.make_async_copy(yb_hbm.at[pl.ds(0, tm * ROW_SUB)], bufs[which].at[kk], sem.at[which]).wait()

    def reduce(which, half):
        rows = slice(half * tm, (half + 1) * tm)
        acc = x2_ref[rows, :]
        for kk in range(TOP_K):
            gate = route_ref[rows, ROUTE_GATE + kk:ROUTE_GATE + kk + 1]
            acc = acc + gate * _load_row_tiles(bufs[which].at[kk], tm)
        return _rms(acc, gfin_ref[...])

    @pl.when(i == 0)
    def _():
        issue(0, 0)

    wait(0)
    issue(2 * i + 1, 1)
    y0 = reduce(0, 0)
    wait(1)
    issue(jnp.minimum(2 * i + 2, 2 * n_steps - 1), 0)
    y1 = reduce(1, 1)

    @pl.when(i < np_steps)
    def _():
        op_ref[0:tm, :] = y0
        op_ref[tm:2 * tm, :] = y1

    @pl.when(i >= np_steps)
    def _():
        os_ref[0:tm, :] = y0
        os_ref[tm:2 * tm, :] = y1

    @pl.when(i == n_steps - 1)
    def _():
        wait(0)


def _combine(dest, x2, route, gfin, yb, n_p):
    n, d = x2.shape
    tm = COMBINE_TM
    step = 2 * tm
    np_steps = n_p // step
    grid_spec = pltpu.PrefetchScalarGridSpec(
        num_scalar_prefetch=1,
        grid=(n // step,),
        in_specs=[
            pl.BlockSpec((step, d), lambda i, dest: (i, 0)),
            pl.BlockSpec((step, LANES), lambda i, dest: (i, 0)),
            pl.BlockSpec((1, d), lambda i, dest: (0, 0)),
            pl.BlockSpec(memory_space=pl.ANY),
        ],
        out_specs=[
            pl.BlockSpec((step, d), lambda i, dest: (jnp.minimum(i, np_steps - 1), 0)),
            pl.BlockSpec((step, d), lambda i, dest: (jnp.maximum(i - np_steps, 0), 0)),
        ],
        scratch_shapes=[
            pltpu.VMEM((TOP_K, tm * ROW_SUB, LANES), F32),
            pltpu.VMEM((TOP_K, tm * ROW_SUB, LANES), F32),
            pltpu.SemaphoreType.DMA((2,)),
        ],
    )
    return pl.pallas_call(
        functools.partial(_combine_body, np_steps=np_steps),
        grid_spec=grid_spec,
        out_shape=[jax.ShapeDtypeStruct((n_p, d), F32), jax.ShapeDtypeStruct((n - n_p, d), F32)],
        compiler_params=_params(("arbitrary",)),
        name="moe_combine",
    )(dest, x2, route, gfin, yb)


def _slot_tables(route_t, counts_row, n_blocks):
    bm = MOE_BM
    e = route_t[ROUTE_E:ROUTE_E + TOP_K].astype(jnp.int32)
    rank = route_t[ROUTE_RANK:ROUTE_RANK + TOP_K].astype(jnp.int32)
    counts = counts_row[:N_EXPERTS].astype(jnp.int32)
    padded = (counts + bm - 1) // bm * bm
    pad_end = jnp.cumsum(padded)
    pad_start = pad_end - padded
    experts = jnp.arange(N_EXPERTS, dtype=jnp.int32)[:, None, None]
    dest = (jnp.sum(jnp.where(e[None] == experts, pad_start[:, None, None], 0), axis=0) + rank).reshape(-1)
    blk_start = jnp.arange(n_blocks, dtype=jnp.int32) * bm
    blk_e = jnp.minimum(jnp.sum(pad_end[None, :] <= blk_start[:, None], axis=1), N_EXPERTS - 1).astype(jnp.int32)
    n_used = pad_end[-1] // bm
    last_blk = jnp.where(padded > 0, pad_end // bm - 1, -1)
    trailing = n_used + jnp.arange(N_EXPERTS, dtype=jnp.int32)
    trailing = jnp.where(trailing < n_blocks, trailing, -1)
    zlist = jnp.concatenate([last_blk, trailing]).astype(jnp.int32)
    ids = jnp.arange(N_EXPERTS, dtype=jnp.int32)
    present = counts > 0
    later = jnp.where(jnp.logical_and(ids[None, :] > ids[:, None], present[None, :]), ids[None, :], N_EXPERTS)
    next_present = jnp.min(later, axis=1)
    next_present = jnp.where(next_present < N_EXPERTS, next_present, -1)
    nxt_e = next_present.astype(jnp.int32)
    parity = ((jnp.cumsum(present.astype(jnp.int32)) - 1) % 2).astype(jnp.int32)
    return dest.astype(jnp.int32), blk_e, n_used.astype(jnp.int32).reshape(1), nxt_e, parity, zlist


def kernel(x_prompt, x_sample, mem_prompt, state_shift, state_wkv, cache_mem_k, cache_mem_v, g_norm_mix, w_in, mu_shift, w0, w_decay_up, a0, w_a_up, w_g_up, k_k, k_a, r_k, ln_x_g, ln_x_b, gmlp_ln_g, gmlp_ln_b, w_spatial, b_spatial, g_norm_mem, w_mem_kv, w_br_rwkv, w_br_gmlp, w_br_mem, w_out, g_norm_ffn, w_router, b_router, w_exp1, b_exp1, w_exp2, b_exp2, g_norm_final):
    bp, tp, d = x_prompt.shape
    bs, ts, _ = x_sample.shape
    n_p, n_s = bp * tp, bs * ts
    n_all = n_p + n_s
    l = 0
    row = lambda a: a.reshape(1, -1)

    x_pair = (x_prompt.reshape(n_p, d), x_sample.reshape(n_s, d))
    zr, zg, zq, sg = _in_proj(x_pair[0], x_pair[1], row(g_norm_mix[l]), w_in[l].astype(BF16))

    mk_p, mv_p = _mem_kv(mem_prompt.reshape(bp * N_MEM, d), row(g_norm_mem[l]), w_mem_kv[l].astype(BF16))

    rp = dict(mu=row(mu_shift[l]), w0=row(w0[l]), wd=w_decay_up[l].astype(BF16), a0=row(a0[l]),
              wa=w_a_up[l].astype(BF16), wg=w_g_up[l].astype(BF16), kk=row(k_k[l]), ka=row(k_a[l]),
              rk=row(r_k[l]), lng=row(ln_x_g[l]), lnb=row(ln_x_b[l]))
    head_of = jnp.arange(RWKV_WIDTH, dtype=jnp.int32) // HEAD_DIM
    rp["hsum"] = (head_of[:, None] == head_of[None, :]).astype(BF16)
    o_rw_p, s_p, shift_p = _rwkv(zr.reshape(n_all // RWKV_TT, RWKV_TT, SHIFT_WIDTH), 0, bp, tp // RWKV_TT,
                        jnp.zeros((bp, 1, SHIFT_WIDTH), F32), jnp.zeros((bp, RWKV_HEADS, HEAD_DIM, HEAD_DIM), F32),
                        rp, tt=RWKV_TT, c=RWKV_C, nseq=1)
    tile_s = RWKV_SAMPLE_SEQS * ts
    o_rw_s, s_s, shift_s = _rwkv(zr.reshape(n_all // tile_s, tile_s, SHIFT_WIDTH), n_p // tile_s, bs, 1,
                        state_shift[l].reshape(bs, 1, SHIFT_WIDTH), state_wkv[l], rp, tt=tile_s, c=ts,
                        nseq=RWKV_SAMPLE_SEQS)

    tri = jnp.tril(jnp.ones((CHUNK, CHUNK), bool))
    ws_p = jnp.where(tri, w_spatial[l], 0.0).astype(BF16)
    bs_p = b_spatial[l].T
    reps = CHUNK // ts
    tri_s = jnp.tril(jnp.ones((ts, ts), bool))
    ws_small = jnp.where(tri_s, w_spatial[l][:, :ts, :ts], 0.0)
    eye = jnp.eye(reps, dtype=F32)
    ws_s = jnp.einsum("ab,gij->gaibj", eye, ws_small).reshape(GMLP_GROUPS, CHUNK, CHUNK).astype(BF16)
    bs_s = jnp.tile(b_spatial[l][:, :ts], (1, reps)).T
    lng, lnb = row(gmlp_ln_g[l]), row(gmlp_ln_b[l])
    o_gm_p, _ = _gmlp(zg, 0, n_p, lng, lnb, ws_p, bs_p)
    o_gm_s, v_rows_s = _gmlp(zg, n_p // (GMLP_TILE * CHUNK), n_s, lng, lnb, ws_s, bs_s)

    o_me_p = _mem_attn(zq.reshape(n_all // ATTN_TT, ATTN_TT, MEM_WIDTH), 0, bp, tp // ATTN_TT,
                       mk_p.reshape(bp, N_MEM, MEM_WIDTH), mv_p.reshape(bp, N_MEM, MEM_WIDTH), tt=ATTN_TT, bb=1,
                       feature_major=False)
    mk_s = jnp.transpose(cache_mem_k[l].reshape(bs, N_MEM, MEM_WIDTH), (0, 2, 1))
    mv_s = jnp.transpose(cache_mem_v[l].reshape(bs, N_MEM, MEM_WIDTH), (0, 2, 1))
    o_me_s = _mem_attn(zq.reshape(n_all // ts, ts, MEM_WIDTH), n_p // ts, bs, 1, mk_s, mv_s, tt=ts, bb=ATTN_BB,
                       feature_major=True)

    wrt_pad = jnp.zeros((d, LANES), F32).at[:, :N_EXPERTS].set(w_router[l])
    brt_pad = jnp.zeros((1, LANES), F32).at[0, :N_EXPERTS].set(b_router[l])
    x2, h2, route, route_t, counts = _merge(
        x_pair, (o_rw_p.reshape(n_p, RWKV_WIDTH), o_rw_s.reshape(n_s, RWKV_WIDTH)), (o_gm_p, o_gm_s),
        (o_me_p.reshape(n_p, MEM_WIDTH), o_me_s.reshape(n_s, MEM_WIDTH)), sg,
        w_br_rwkv[l].astype(BF16), w_br_gmlp[l].astype(BF16), w_br_mem[l].astype(BF16), w_out[l].astype(BF16),
        row(g_norm_ffn[l]), wrt_pad, brt_pad)

    n_assign = n_all * TOP_K
    n_blocks = -(-(n_assign + N_EXPERTS * (MOE_BM - 1)) // MOE_BM)
    dest, blk_e, n_used, nxt_e, parity, zlist = _slot_tables(route_t, counts[0], n_blocks)
    xs = _dispatch(dest, zlist, h2.reshape(n_all, ROW_SUB, LANES), n_blocks)
    yb = _moe(blk_e, n_used, nxt_e, parity, xs.reshape(n_blocks * MOE_BM * ROW_SUB, LANES), w_exp1[l],
              b_exp1[l].reshape(N_EXPERTS, 1, 2 * D_FF), w_exp2[l], b_exp2[l].reshape(N_EXPERTS, 1, d), n_blocks)
    y_p, y_s = _combine(dest, x2, route, row(g_norm_final), yb, n_p)

    mk_out = mk_p.reshape(1, bp, N_MEM, MEM_HEADS, MEM_WIDTH // MEM_HEADS)
    mv_out = mv_p.reshape(1, bp, N_MEM, MEM_HEADS, MEM_WIDTH // MEM_HEADS)
    return (y_p.reshape(bp, tp, d), y_s.reshape(bs, ts, d), shift_p.reshape(1, bp, SHIFT_WIDTH), s_p[None], mk_out,
            mv_out, shift_s.reshape(1, bs, SHIFT_WIDTH), s_s[None],
            v_rows_s.reshape(1, bs, ts, GMLP_WIDTH))
```

```python
import functools
import math

import jax
import jax.numpy as jnp
from jax import lax
from jax.experimental import pallas as pl
from jax.experimental.pallas import tpu as pltpu

F32 = jnp.float32
BF16 = jnp.bfloat16

D_MODEL = 1024
RWKV_HEADS = 8
HEAD_DIM = 64
RWKV_WIDTH = RWKV_HEADS * HEAD_DIM
DECAY_LORA = 64
A_LORA = 64
GATE_LORA = 128
GMLP_GROUPS = 4
GMLP_WIDTH = 256
CHUNK = 128
MEM_HEADS = 4
MEM_WIDTH = 256
N_MEM = 256
N_EXPERTS = 32
TOP_K = 4
D_FF = 1024
SWIGLU_ALPHA = 1.702
SWIGLU_LIMIT = 7.0
RMS_EPS = 1e-5
LN_EPS = 1e-5
GN_EPS = 64e-5
SHIFT_WIDTH = 3 * RWKV_WIDTH + DECAY_LORA + A_LORA + GATE_LORA
OFF_GMLP = SHIFT_WIDTH
OFF_QMEM = OFF_GMLP + 2 * GMLP_WIDTH
OFF_GATE = OFF_QMEM + MEM_WIDTH
IN_WIDTH = OFF_GATE + 3 * D_MODEL
LANES = 128
ROW_SUB = D_MODEL // LANES

PROJ_TM = 512
MERGE_TM = 512
RWKV_TT = 512
RWKV_C = 64
RWKV_GROUP = 8
RWKV_SAMPLE_SEQS = 8
GMLP_TILE = 4
ATTN_TT = 512
ATTN_BB = 8
MOE_BM = 512
DISPATCH_TM = 1024
COMBINE_TM = 128

ROUTE_E = 0
ROUTE_RANK = 4
ROUTE_GATE = 8

NN = ((1,), (0,))
NT = ((1,), (1,))
TN = ((0,), (0,))

VMEM_LIMIT = 56 * 1024 * 1024


def _params(sem, vmem=VMEM_LIMIT):
    return pltpu.CompilerParams(dimension_semantics=sem, vmem_limit_bytes=vmem)


def _bdot(a, b, dims=NN):
    return lax.dot_general(a.astype(BF16), b.astype(BF16), (dims, ((), ())), preferred_element_type=F32)


def _split(x):
    hi = x.astype(BF16)
    lo = (x - hi.astype(F32)).astype(BF16)
    return hi, lo


def _dot3(a, b, dims=NN):
    dn = (dims, ((), ()))
    ah, al = _split(a)
    bh, bl = _split(b)
    r = lax.dot_general(ah, bh, dn, preferred_element_type=F32)
    r = r + lax.dot_general(al, bh, dn, preferred_element_type=F32)
    return r + lax.dot_general(ah, bl, dn, preferred_element_type=F32)


def _rms(x, g):
    return x * lax.rsqrt(jnp.mean(x * x, axis=-1, keepdims=True) + RMS_EPS) * g


def _store_row_tiles(ref, x):
    m = x.shape[0]
    for j in range(ROW_SUB):
        ref[pl.ds(j, m, stride=ROW_SUB), :] = x[:, j * LANES:(j + 1) * LANES]


def _load_row_tiles(ref, m):
    return jnp.concatenate([ref[pl.ds(j, m, stride=ROW_SUB), :] for j in range(ROW_SUB)], axis=1)


def _sigmoid(x):
    return 0.5 * jnp.tanh(0.5 * x) + 0.5


def _in_proj_body(xp_ref, xs_ref, g_ref, w_ref, zr_ref, zg_ref, zq_ref, sg_ref, *, np_tiles):
    x = jnp.where(pl.program_id(0) < np_tiles, xp_ref[...], xs_ref[...])
    h = _rms(x, g_ref[...]).astype(BF16)
    zr_ref[...] = jnp.dot(h, w_ref[:, 0:OFF_GMLP], preferred_element_type=F32)
    zg_ref[...] = jnp.dot(h, w_ref[:, OFF_GMLP:OFF_QMEM], preferred_element_type=F32)
    zq_ref[...] = jnp.dot(h, w_ref[:, OFF_QMEM:OFF_GATE], preferred_element_type=F32).astype(zq_ref.dtype)
    gates = jnp.dot(h, w_ref[:, OFF_GATE:IN_WIDTH], preferred_element_type=F32)
    sg_ref[...] = _sigmoid(gates).astype(BF16)


def _in_proj(x_p, x_s, g, w_bf16):
    n_p = x_p.shape[0]
    n = n_p + x_s.shape[0]
    tm = PROJ_TM
    np_tiles = n_p // tm
    row = lambda i: (i, 0)
    const = lambda i: (0, 0)
    return pl.pallas_call(
        functools.partial(_in_proj_body, np_tiles=np_tiles),
        grid=(n // tm,),
        in_specs=[
            pl.BlockSpec((tm, D_MODEL), lambda i: (jnp.minimum(i, np_tiles - 1), 0)),
            pl.BlockSpec((tm, D_MODEL), lambda i: (jnp.maximum(i - np_tiles, 0), 0)),
            pl.BlockSpec((1, D_MODEL), const),
            pl.BlockSpec((D_MODEL, IN_WIDTH), const, pipeline_mode=pl.Buffered(1)),
        ],
        out_specs=[
            pl.BlockSpec((tm, SHIFT_WIDTH), row),
            pl.BlockSpec((tm, 2 * GMLP_WIDTH), row),
            pl.BlockSpec((tm, MEM_WIDTH), row),
            pl.BlockSpec((tm, 3 * D_MODEL), row),
        ],
        out_shape=[
            jax.ShapeDtypeStruct((n, SHIFT_WIDTH), F32),
            jax.ShapeDtypeStruct((n, 2 * GMLP_WIDTH), F32),
            jax.ShapeDtypeStruct((n, MEM_WIDTH), BF16),
            jax.ShapeDtypeStruct((n, 3 * D_MODEL), BF16),
        ],
        compiler_params=_params(("parallel",)),
        name="in_proj",
    )(x_p, x_s, g, w_bf16)


def _mem_kv_body(x_ref, g_ref, w_ref, k_ref, v_ref):
    h = _rms(x_ref[...], g_ref[...]).astype(BF16)
    k_ref[...] = jnp.dot(h, w_ref[:, 0:MEM_WIDTH], preferred_element_type=F32)
    v_ref[...] = jnp.dot(h, w_ref[:, MEM_WIDTH:2 * MEM_WIDTH], preferred_element_type=F32)


def _mem_kv(mem, g, w_bf16):
    n = mem.shape[0]
    tm = PROJ_TM
    return pl.pallas_call(
        _mem_kv_body,
        grid=(n // tm,),
        in_specs=[
            pl.BlockSpec((tm, D_MODEL), lambda i: (i, 0)),
            pl.BlockSpec((1, D_MODEL), lambda i: (0, 0)),
            pl.BlockSpec((D_MODEL, 2 * MEM_WIDTH), lambda i: (0, 0)),
        ],
        out_specs=[pl.BlockSpec((tm, MEM_WIDTH), lambda i: (i, 0))] * 2,
        out_shape=[jax.ShapeDtypeStruct((n, MEM_WIDTH), F32)] * 2,
        compiler_params=_params(("parallel",)),
        name="mem_kv",
    )(mem, g, w_bf16)


EXP_M05 = math.exp(-0.5)


def _unit_lower_inverse(lows, c):
    rows = lax.broadcasted_iota(jnp.int32, (c, c), 0)
    cols = lax.broadcasted_iota(jnp.int32, (c, c), 1)
    eye = (rows == cols).astype(F32)
    invs = [eye - low for low in lows]
    powers = lows
    for _ in range(int(math.log2(c)) - 1):
        powers = [_bdot(pw, pw) for pw in powers]
        invs = [inv + _bdot(inv, pw) for inv, pw in zip(invs, powers)]
    return invs


def _rwkv_body(zr_ref, shift_ref, s0_ref, mu_ref, w0_ref, wd_ref, a0_ref, wa_ref, wg_ref, kk_ref, ka_ref,
               rk_ref, lng_ref, lnb_ref, hsum_ref,
               o_ref, sout_ref, shout_ref,
               carry_ref, state_ref, r_s, k_s, v_s, kk_s, b_s, ld_s, y_s, bon_s, g_s,
               rw_s, y0_s, gm_s, h0_s, *, tt, c, nseq):
    i = pl.program_id(1)
    rw = RWKV_WIDTH
    nh = RWKV_HEADS

    @pl.when(i == 0)
    def _():
        for q in range(nseq):
            state_ref[q * nh:(q + 1) * nh] = s0_ref[q]

    z = zr_ref[0]
    z_prev = pltpu.roll(z, 1, 0)
    row = lax.broadcasted_iota(jnp.int32, z.shape, 0)
    if nseq == 1:
        @pl.when(i == 0)
        def _():
            carry_ref[...] = shift_ref[0]

        z_prev = jnp.where(row == 0, carry_ref[...], z_prev)
        carry_ref[...] = z[tt - 1:tt, :]
    else:
        first_rows = jnp.concatenate([jnp.broadcast_to(shift_ref[q], (c, SHIFT_WIDTH)) for q in range(nseq)], axis=0)
        z_prev = jnp.where(row % c == 0, first_rows, z_prev)
    zs = z + mu_ref[...] * (z_prev - z)
    r = zs[:, 0:rw]
    k = zs[:, rw:2 * rw]
    v = zs[:, 2 * rw:3 * rw]
    zw = zs[:, 3 * rw:3 * rw + DECAY_LORA]
    za = zs[:, 3 * rw + DECAY_LORA:3 * rw + DECAY_LORA + A_LORA]
    zg = zs[:, 3 * rw + DECAY_LORA + A_LORA:SHIFT_WIDTH]
    xw = w0_ref[...] + _bdot(jnp.tanh(zw), wd_ref[...])
    ld_s[...] = -EXP_M05 * _sigmoid(xw)
    a = _sigmoid(a0_ref[...] + _bdot(za, wa_ref[...]))
    g_s[...] = _bdot(_sigmoid(zg), wg_ref[...])
    kk = k * kk_ref[...]
    k = k * (1.0 + (a - 1.0) * ka_ref[...])
    r_s[...] = r
    k_s[...] = k
    v_s[...] = v

    def head_sum(t):
        return jnp.dot(t.astype(BF16), hsum_ref[...], preferred_element_type=F32)

    kk = kk / jnp.maximum(jnp.sqrt(head_sum(kk * kk)), 1e-12)
    kk_s[...] = kk
    b_s[...] = kk * a
    bon_s[...] = head_sum(r * k * rk_ref[...]) * v

    rows_c = lax.broadcasted_iota(jnp.int32, (c, c), 0)
    cols_c = lax.broadcasted_iota(jnp.int32, (c, c), 1)
    strict = rows_c > cols_c
    incl = rows_c >= cols_c
    tril_ones = jnp.where(incl, 1.0, 0.0).astype(BF16)
    rows_2c = lax.broadcasted_iota(jnp.int32, (c, 2 * c), 0)
    cols_2c = lax.broadcasted_iota(jnp.int32, (c, 2 * c), 1)
    incl2 = rows_2c >= jnp.where(cols_2c >= c, cols_2c - c, cols_2c)
    rows_k = lax.broadcasted_iota(jnp.int32, (HEAD_DIM, HEAD_DIM), 0)
    cols_k = lax.broadcasted_iota(jnp.int32, (HEAD_DIM, HEAD_DIM), 1)
    eye_k = (rows_k == cols_k).astype(F32)
    zeros_cv = jnp.zeros((c, HEAD_DIM), F32)
    heads = [slice(h * HEAD_DIM, (h + 1) * HEAD_DIM) for h in range(RWKV_HEADS)]

    n_chunks = tt // c
    group = nseq if nseq > 1 else math.gcd(n_chunks, RWKV_GROUP)

    def chunk_rows(ci):
        return pl.ds(ci * c, c) if isinstance(ci, int) else pl.ds(pl.multiple_of(ci * c, c), c)

    def scaled(ci):
        rows = chunk_rows(ci)
        ld = ld_s[rows, :]
        ld_hi, ld_lo = _split(ld)
        cum = (jnp.dot(tril_ones, ld_hi, preferred_element_type=F32)
               + jnp.dot(tril_ones, ld_lo, preferred_element_type=F32))
        e_inc = jnp.exp(cum)
        e_neg = jnp.exp(-cum)
        kt = k_s[rows, :] * e_neg
        bt = b_s[rows, :] * e_neg
        g_end = e_inc[c - 1:c, :]
        return dict(ci=ci, rt=r_s[rows, :] * e_inc, kkt=kk_s[rows, :] * jnp.exp(cum - ld), kt=kt, bt=bt,
                    g_end=g_end, bc=bt * g_end, kc=kt * g_end, vv=v_s[rows, :])

    def phase1(cj, carry):
        chains = [(ch, h, s) for ch in [scaled(cj * group + g) for g in range(group)]
                  for h, s in enumerate(heads)]
        x = [jnp.concatenate([ch["kkt"][:, s], ch["rt"][:, s]], axis=0) for ch, _, s in chains]
        zz = [jnp.concatenate([ch["bt"][:, s], ch["kt"][:, s]], axis=0) for ch, _, s in chains]
        amat = [_bdot(xh, zh, NT) for xh, zh in zip(x, zz)]
        l_b = [jnp.where(strict, am[0:c, 0:c], 0.0) for am in amat]
        l_k = [jnp.where(strict, am[0:c, c:2 * c], 0.0) for am in amat]
        a_r = [jnp.where(incl2, am[c:2 * c, :], 0.0) for am in amat]
        tinv = _unit_lower_inverse(l_b, c)
        lkv = [_bdot(lk, ch["vv"][:, s]) for lk, (ch, _, s) in zip(l_k, chains)]
        wu = [-_bdot(t, jnp.concatenate([ch["kkt"][:, s], lv], axis=1))
              for t, lv, (ch, _, s) in zip(tinv, lkv, chains)]
        m = [jnp.concatenate([w, jnp.concatenate([zeros_cv, ch["vv"][:, s]], axis=1)], axis=0)
             for w, (ch, _, s) in zip(wu, chains)]
        am2 = [_bdot(ar, mh) for ar, mh in zip(a_r, m)]
        gh = [_bdot(mh, jnp.concatenate([ch["bc"][:, s], ch["kc"][:, s]], axis=0), TN)
              for mh, (ch, _, s) in zip(m, chains)]
        for q, (ch, h, s) in enumerate(chains):
            idx = ch["ci"] * RWKV_HEADS + h
            rw_s[idx] = ch["rt"][:, s] + am2[q][:, 0:HEAD_DIM]
            y0_s[idx] = am2[q][:, HEAD_DIM:2 * HEAD_DIM]
            gm_s[idx] = gh[q][0:HEAD_DIM, :] + eye_k * ch["g_end"][:, s]
            h0_s[idx] = gh[q][HEAD_DIM:2 * HEAD_DIM, :]
        return carry

    if n_chunks == group:
        phase1(0, 0)
    else:
        lax.fori_loop(0, n_chunks // group, phase1, 0)

    if nseq == 1:
        def phase2(ci, carry):
            rows = chunk_rows(ci)
            for h, s in enumerate(heads):
                idx = ci * nh + h
                s_h = state_ref[h]
                y_s[rows, s] = _bdot(rw_s[idx], s_h, NT) + y0_s[idx]
                state_ref[h] = _bdot(s_h, gm_s[idx]) + h0_s[idx]
            return carry

        lax.fori_loop(0, n_chunks, phase2, 0)
    else:
        pairs = [(q, h, s) for q in range(nseq) for h, s in enumerate(heads)]
        states = [state_ref[q * nh + h] for q, h, _ in pairs]
        ys = [_bdot(rw_s[q * nh + h], st, NT) + y0_s[q * nh + h] for (q, h, _), st in zip(pairs, states)]
        new = [_bdot(st, gm_s[q * nh + h]) + h0_s[q * nh + h] for (q, h, _), st in zip(pairs, states)]
        for (q, h, s), yq, nq in zip(pairs, ys, new):
            y_s[chunk_rows(q), s] = yq
            state_ref[q * nh + h] = nq
    y = y_s[...]
    yc = y - head_sum(y) * (1.0 / HEAD_DIM)
    var = head_sum(yc * yc) * (1.0 / HEAD_DIM)
    yn = yc * lax.rsqrt(var + GN_EPS)
    o_ref[0] = ((yn * lng_ref[...] + lnb_ref[...] + bon_s[...]) * g_s[...]).astype(o_ref.dtype)

    @pl.when(i == pl.num_programs(1) - 1)
    def _():
        for q in range(nseq):
            sout_ref[q] = state_ref[q * nh:(q + 1) * nh]
            last = tt - 1 if nseq == 1 else (q + 1) * c - 1
            shout_ref[q] = zr_ref[0, last:last + 1, :]


def _rwkv(zr3, blk_off, b, nt, shift_prev, s_prev, p, tt, c, nseq):
    vec = lambda n: pl.BlockSpec((1, n), lambda bi, ti: (0, 0))
    mat = lambda m, n: pl.BlockSpec((m, n), lambda bi, ti: (0, 0))
    rw = RWKV_WIDTH
    scr = lambda: pltpu.VMEM((tt, rw), F32)
    nch = (tt // c) * RWKV_HEADS
    return pl.pallas_call(
        functools.partial(_rwkv_body, tt=tt, c=c, nseq=nseq),
        grid=(b // nseq, nt),
        in_specs=[
            pl.BlockSpec((1, tt, SHIFT_WIDTH), lambda bi, ti: (blk_off + bi * nt + ti, 0, 0)),
            pl.BlockSpec((nseq, 1, SHIFT_WIDTH), lambda bi, ti: (bi, 0, 0)),
            pl.BlockSpec((nseq, RWKV_HEADS, HEAD_DIM, HEAD_DIM), lambda bi, ti: (bi, 0, 0, 0)),
            vec(SHIFT_WIDTH), vec(rw), mat(DECAY_LORA, rw), vec(rw), mat(A_LORA, rw), mat(GATE_LORA, rw),
            vec(rw), vec(rw), vec(rw), vec(rw), vec(rw), mat(rw, rw),
        ],
        out_specs=[
            pl.BlockSpec((1, tt, rw), lambda bi, ti: (bi, ti, 0)),
            pl.BlockSpec((nseq, RWKV_HEADS, HEAD_DIM, HEAD_DIM), lambda bi, ti: (bi, 0, 0, 0)),
            pl.BlockSpec((nseq, 1, SHIFT_WIDTH), lambda bi, ti: (bi, 0, 0)),
        ],
        out_shape=[
            jax.ShapeDtypeStruct((b // nseq, nt * tt, rw), BF16),
            jax.ShapeDtypeStruct((b, RWKV_HEADS, HEAD_DIM, HEAD_DIM), F32),
            jax.ShapeDtypeStruct((b, 1, SHIFT_WIDTH), F32),
        ],
        scratch_shapes=[
            pltpu.VMEM((1, SHIFT_WIDTH), F32),
            pltpu.VMEM((nseq * RWKV_HEADS, HEAD_DIM, HEAD_DIM), F32),
            scr(), scr(), scr(), scr(), scr(), scr(), scr(), scr(), scr(),
            pltpu.VMEM((nch, c, HEAD_DIM), F32),
            pltpu.VMEM((nch, c, HEAD_DIM), F32),
            pltpu.VMEM((nch, HEAD_DIM, HEAD_DIM), F32),
            pltpu.VMEM((nch, HEAD_DIM, HEAD_DIM), F32),
        ],
        compiler_params=_params(("parallel", "arbitrary")),
        name="rwkv",
    )(zr3, shift_prev, s_prev, p["mu"], p["w0"], p["wd"], p["a0"], p["wa"], p["wg"], p["kk"], p["ka"],
      p["rk"], p["lng"], p["lnb"], p["hsum"])


def _gmlp_body(z_ref, lng_ref, lnb_ref, ws_ref, bs_ref, o_ref, v_ref):
    z = z_ref[...]
    ge = 0.5 * z * (1.0 + lax.erf(z * (1.0 / math.sqrt(2.0))))
    u = ge[:, 0:GMLP_WIDTH]
    v = ge[:, GMLP_WIDTH:2 * GMLP_WIDTH]
    mean = jnp.mean(v, axis=-1, keepdims=True)
    vc = v - mean
    var = jnp.mean(vc * vc, axis=-1, keepdims=True)
    vn = vc * lax.rsqrt(var + LN_EPS) * lng_ref[...] + lnb_ref[...]
    v_ref[...] = vn
    gd = GMLP_WIDTH // GMLP_GROUPS
    for q in range(GMLP_TILE):
        rows = slice(q * CHUNK, (q + 1) * CHUNK)
        for g in range(GMLP_GROUPS):
            gs = slice(g * gd, (g + 1) * gd)
            sv = jnp.dot(ws_ref[g], vn[rows, gs].astype(BF16), preferred_element_type=F32) + bs_ref[:, g:g + 1]
            o_ref[rows, gs] = (u[rows, gs] * sv).astype(o_ref.dtype)


def _gmlp(zg, tile_off, n_rows, lng, lnb, ws_bf16, bs):
    tile = GMLP_TILE * CHUNK
    return pl.pallas_call(
        _gmlp_body,
        grid=(n_rows // tile,),
        in_specs=[
            pl.BlockSpec((tile, 2 * GMLP_WIDTH), lambda i: (tile_off + i, 0)),
            pl.BlockSpec((1, GMLP_WIDTH), lambda i: (0, 0)),
            pl.BlockSpec((1, GMLP_WIDTH), lambda i: (0, 0)),
            pl.BlockSpec((GMLP_GROUPS, CHUNK, CHUNK), lambda i: (0, 0, 0)),
            pl.BlockSpec((CHUNK, GMLP_GROUPS), lambda i: (0, 0)),
        ],
        out_specs=[
            pl.BlockSpec((tile, GMLP_WIDTH), lambda i: (i, 0)),
            pl.BlockSpec((tile, GMLP_WIDTH), lambda i: (i, 0)),
        ],
        out_shape=[
            jax.ShapeDtypeStruct((n_rows, GMLP_WIDTH), BF16),
            jax.ShapeDtypeStruct((n_rows, GMLP_WIDTH), F32),
        ],
        compiler_params=_params(("parallel",)),
        name="gmlp",
    )(zg, lng, lnb, ws_bf16, bs)


def _mem_attn_body(q_ref, k_ref, v_ref, o_ref, *, bb, feature_major):
    hd = MEM_WIDTH // MEM_HEADS
    pairs = [(s, slice(h * hd, (h + 1) * hd)) for s in range(bb) for h in range(MEM_HEADS)]
    if feature_major:
        scores = [_bdot(q_ref[s, :, hs], k_ref[s, hs, :], NN) * (hd ** -0.5) for s, hs in pairs]
    else:
        scores = [_bdot(q_ref[s, :, hs], k_ref[s, :, hs], NT) * (hd ** -0.5) for s, hs in pairs]
    probs = []
    for sc in scores:
        pr = jnp.exp(sc - jnp.max(sc, axis=-1, keepdims=True))
        probs.append(pr / jnp.sum(pr, axis=-1, keepdims=True))
    for (s, hs), pr in zip(pairs, probs):
        if feature_major:
            o_ref[s, :, hs] = _bdot(pr, v_ref[s, hs, :], NT).astype(o_ref.dtype)
        else:
            o_ref[s, :, hs] = _bdot(pr, v_ref[s, :, hs]).astype(o_ref.dtype)


def _mem_attn(q3, blk_off, b, nt, mk, mv, tt, bb, feature_major):
    return pl.pallas_call(
        functools.partial(_mem_attn_body, bb=bb, feature_major=feature_major),
        grid=(b // bb, nt),
        in_specs=[
            pl.BlockSpec((bb, tt, MEM_WIDTH), lambda bi, ti: (blk_off // bb + bi * nt + ti, 0, 0)),
            pl.BlockSpec((bb, N_MEM, MEM_WIDTH), lambda bi, ti: (bi, 0, 0)),
            pl.BlockSpec((bb, N_MEM, MEM_WIDTH), lambda bi, ti: (bi, 0, 0)),
        ],
        out_specs=pl.BlockSpec((bb, tt, MEM_WIDTH), lambda bi, ti: (bi, ti, 0)),
        out_shape=jax.ShapeDtypeStruct((b, nt * tt, MEM_WIDTH), BF16),
        compiler_params=_params(("parallel", "parallel")),
        name="mem_attn",
    )(q3, mk, mv)


def _merge_body(x_p, x_s, orw_p, orw_s, ogm_p, ogm_s, ome_p, ome_s, sg_ref, wbr_ref, wbg_ref, wbm_ref, wout_ref,
                gffn_ref, wrt_ref, brt_ref, x2_ref, h2_ref, route_ref, route_t_ref, cnt_ref, count_s, *, np_tiles):
    i = pl.program_id(0)
    d = D_MODEL
    tm = x_p.shape[0]
    is_p = i < np_tiles
    x = jnp.where(is_p, x_p[...], x_s[...])
    orw = jnp.where(is_p, orw_p[...], orw_s[...])
    ogm = jnp.where(is_p, ogm_p[...], ogm_s[...])
    ome = jnp.where(is_p, ome_p[...], ome_s[...])
    merged = sg_ref[:, 0:d].astype(F32) * _bdot(orw, wbr_ref[...])
    merged = merged + sg_ref[:, d:2 * d].astype(F32) * _bdot(ogm, wbg_ref[...])
    merged = merged + sg_ref[:, 2 * d:3 * d].astype(F32) * _bdot(ome, wbm_ref[...])
    x2 = x + _bdot(merged, wout_ref[...])
    x2_ref[...] = x2
    h2 = _rms(x2, gffn_ref[...])
    _store_row_tiles(h2_ref, h2)

    @pl.when(i == 0)
    def _():
        count_s[...] = jnp.zeros_like(count_s)

    lane = lax.broadcasted_iota(jnp.int32, (tm, LANES), 1)
    logits = jnp.where(lane < N_EXPERTS, _dot3(h2, wrt_ref[...]) + brt_ref[...], -jnp.inf)
    lane_f = lane.astype(F32)
    tops, hots, idxs = [], [], []
    for _ in range(TOP_K):
        top = jnp.max(logits, axis=-1, keepdims=True)
        idx = jnp.min(jnp.where(logits == top, lane_f, float(LANES)), axis=-1, keepdims=True)
        hot = lane_f == idx
        logits = jnp.where(hot, -jnp.inf, logits)
        tops.append(top)
        hots.append(hot)
        idxs.append(idx)
    weights = [jnp.exp(t - tops[0]) for t in tops]
    denom = weights[0] + weights[1] + weights[2] + weights[3]
    onehot = jnp.zeros((tm, LANES), F32)
    for hot in hots:
        onehot = onehot + jnp.where(hot, 1.0, 0.0)
    rows_t = lax.broadcasted_iota(jnp.int32, (tm, tm), 0)
    cols_t = lax.broadcasted_iota(jnp.int32, (tm, tm), 1)
    ahead = jnp.where(rows_t > cols_t, 1.0, 0.0).astype(BF16)
    prefix = jnp.dot(ahead, onehot.astype(BF16), preferred_element_type=F32) + count_s[...]
    route = jnp.zeros((tm, LANES), F32)
    for kk in range(TOP_K):
        rank = jnp.sum(jnp.where(hots[kk], prefix, 0.0), axis=-1, keepdims=True)
        route = jnp.where(lane == ROUTE_E + kk, idxs[kk], route)
        route = jnp.where(lane == ROUTE_RANK + kk, rank, route)
        route = jnp.where(lane == ROUTE_GATE + kk, weights[kk] / denom, route)
    route_ref[...] = route
    route_t_ref[...] = route.T
    count_s[...] = count_s[...] + jnp.sum(onehot, axis=0, keepdims=True)
    cnt_ref[...] = jnp.broadcast_to(count_s[...], cnt_ref.shape)


def _merge(x, o_rw, o_gm, o_me, sg, wbr, wbg, wbm, wout, gffn, wrt_pad, brt_pad):
    n_p = x[0].shape[0]
    n = n_p + x[1].shape[0]
    tm = MERGE_TM
    d = D_MODEL
    np_tiles = n_p // tm
    row = lambda i: (i, 0)
    const = lambda i: (0, 0)
    first =lambda i: (jnp.minimum(i, np_tiles - 1), 0)
    second = lambda i: (jnp.maximum(i - np_tiles, 0), 0)

    def pair(width):
        return [pl.BlockSpec((tm, width), first), pl.BlockSpec((tm, width), second)]

    return pl.pallas_call(
        functools.partial(_merge_body, np_tiles=np_tiles),
        grid=(n // tm,),
        in_specs=pair(d) + pair(RWKV_WIDTH) + pair(GMLP_WIDTH) + pair(MEM_WIDTH) + [
            pl.BlockSpec((tm, 3 * d), row),
            pl.BlockSpec((RWKV_WIDTH, d), const),
            pl.BlockSpec((GMLP_WIDTH, d), const),
            pl.BlockSpec((MEM_WIDTH, d), const),
            pl.BlockSpec((d, d), const),
            pl.BlockSpec((1, d), const),
            pl.BlockSpec((d, LANES), const),
            pl.BlockSpec((1, LANES), const),
        ],
        out_specs=[
            pl.BlockSpec((tm, d), row),
            pl.BlockSpec((tm * ROW_SUB, LANES), row),
            pl.BlockSpec((tm, LANES), row),
            pl.BlockSpec((LANES, tm), lambda i: (0, i)),
            pl.BlockSpec((8, LANES), const),
        ],
        out_shape=[
            jax.ShapeDtypeStruct((n, d), F32),
            jax.ShapeDtypeStruct((n * ROW_SUB, LANES), F32),
            jax.ShapeDtypeStruct((n, LANES), F32),
            jax.ShapeDtypeStruct((LANES, n), F32),
            jax.ShapeDtypeStruct((8, LANES), F32),
        ],
        scratch_shapes=[pltpu.VMEM((1, LANES), F32)],
        compiler_params=_params(("arbitrary",)),
        name="merge",
    )(x[0], x[1], o_rw[0], o_rw[1], o_gm[0], o_gm[1], o_me[0], o_me[1], sg, wbr, wbg, wbm, wout, gffn, wrt_pad,
      brt_pad)


N_ZERO_BLOCKS = 2 * N_EXPERTS


def _dispatch_body(dest_ref, zlist_ref, h_hbm, xs_hbm, zero_s, hbuf, sem_z, sem_in, sem):
    i = pl.program_id(0)
    tm = DISPATCH_TM
    bm = MOE_BM

    @pl.when(i == 0)
    def _():
        zero_s[...] = jnp.zeros_like(zero_s)

        def zero_copy(q):
            start = pl.multiple_of(zlist_ref[q] * bm, bm)
            return pltpu.make_async_copy(zero_s, xs_hbm.at[pl.ds(start, bm)], sem_z)

        def start(q, carry):
            @pl.when(zlist_ref[q] >= 0)
            def _():
                zero_copy(q).start()
            return carry

        def wait(q, carry):
            @pl.when(zlist_ref[q] >= 0)
            def _():
                zero_copy(q).wait()
            return carry

        lax.fori_loop(0, N_ZERO_BLOCKS, start, 0)
        lax.fori_loop(0, N_ZERO_BLOCKS, wait, 0)

    base = i * tm
    n_tok = pl.num_programs(0) * tm

    n_steps = pl.num_programs(0)
    parity = lax.rem(i, 2)
    buf = lax.rem(i, 3)

    def fetch(tile, which):
        return pltpu.make_async_copy(h_hbm.at[pl.ds(pl.multiple_of(tile * tm, tm), tm)], hbuf.at[which],
                                     sem_in.at[which])

    @pl.when(i == 0)
    def _():
        fetch(0, 0).start()

    fetch(i, buf).wait()

    @pl.when(i + 1 < n_steps)
    def _():
        fetch(i + 1, lax.rem(i + 1, 3)).start()

    def body(r, carry):
        for kk in range(TOP_K):
            slot = dest_ref[kk * n_tok + base + r]
            pltpu.make_async_copy(hbuf.at[buf, r], xs_hbm.at[slot], sem.at[parity]).start(priority=kk % 2)
        return carry

    lax.fori_loop(0, tm, body, 0, unroll=2)

    def drain(which):
        for kk in range(TOP_K):
            pltpu.make_async_copy(hbuf.at[0], xs_hbm.at[pl.ds(0, tm)], sem.at[which]).wait()

    @pl.when(i > 0)
    def _():
        drain(1 - parity)

    @pl.when(i == n_steps - 1)
    def _():
        drain(parity)


def _dispatch(dest, zlist, h, n_blocks):
    n = h.shape[0]
    tm = DISPATCH_TM
    grid_spec = pltpu.PrefetchScalarGridSpec(
        num_scalar_prefetch=2,
        grid=(n // tm,),
        in_specs=[pl.BlockSpec(memory_space=pl.ANY)],
        out_specs=pl.BlockSpec(memory_space=pl.ANY),
        scratch_shapes=[
            pltpu.VMEM((MOE_BM, ROW_SUB, LANES), F32),
            pltpu.VMEM((3, tm, ROW_SUB, LANES), F32),
            pltpu.SemaphoreType.DMA(()),
            pltpu.SemaphoreType.DMA((3,)),
            pltpu.SemaphoreType.DMA((2,)),
        ],
    )
    return pl.pallas_call(
        _dispatch_body,
        grid_spec=grid_spec,
        out_shape=jax.ShapeDtypeStruct((n_blocks * MOE_BM, ROW_SUB, LANES), F32),
        compiler_params=_params(("arbitrary",)),
        name="moe_dispatch",
    )(dest, zlist, h)


CAST_ROWS = 64


def _moe_body(be_ref, nused_ref, nxt_ref, par_ref, x_ref, w1_hbm, b1_ref, w2_hbm, b2_ref, o_ref,
              w1f, w2f, w1b, w2b, sem):
    j = pl.program_id(0)
    n_used = nused_ref[0]

    def weight_copies(expert, slot):
        return (pltpu.make_async_copy(w1_hbm.at[expert], w1f.at[slot], sem.at[slot, 0]),
                pltpu.make_async_copy(w2_hbm.at[expert], w2f.at[slot], sem.at[slot, 1]))

    @pl.when(j == 0)
    def _():
        for cp in weight_copies(be_ref[0], 0):
            cp.start()

    @pl.when(j < n_used)
    def _():
        slot = par_ref[be_ref[j]]
        changed = jnp.logical_or(j == 0, be_ref[j] != be_ref[jnp.maximum(j - 1, 0)])

        @pl.when(changed)
        def _():
            for cp in weight_copies(be_ref[j], slot):
                cp.wait()

            nxt = nxt_ref[be_ref[j]]

            @pl.when(nxt >= 0)
            def _():
                for cp in weight_copies(nxt, 1 - slot):
                    cp.start()

            def cast1(q, carry):
                rows = pl.ds(pl.multiple_of(q * CAST_ROWS, CAST_ROWS), CAST_ROWS)
                w1b[rows, :] = w1f[slot, rows, :].astype(BF16)
                return carry

            def cast2(q, carry):
                rows = pl.ds(pl.multiple_of(q * CAST_ROWS, CAST_ROWS), CAST_ROWS)
                w2b[rows, :] = w2f[slot, rows, :].astype(BF16)
                return carry

            lax.fori_loop(0, D_MODEL // CAST_ROWS, cast1, 0)
            lax.fori_loop(0, D_FF // CAST_ROWS, cast2, 0)

        x = _load_row_tiles(x_ref, MOE_BM).astype(BF16)
        z = jnp.dot(x, w1b[...], preferred_element_type=F32) + b1_ref[0]
        zg = jnp.minimum(z[:, 0:D_FF], SWIGLU_LIMIT)
        zl = jnp.clip(z[:, D_FF:2 * D_FF], -SWIGLU_LIMIT, SWIGLU_LIMIT)
        act = zg * _sigmoid(SWIGLU_ALPHA * zg) * (zl + 1.0)
        _store_row_tiles(o_ref, jnp.dot(act.astype(BF16), w2b[...], preferred_element_type=F32) + b2_ref[0])

    @pl.when(j >= n_used)
    def _():
        o_ref[...] = jnp.zeros_like(o_ref)


def _moe(blk_e, n_used, nxt_e, parity, xs, w1, b1, w2, b2, n_blocks):
    bm = MOE_BM
    d = D_MODEL
    grid_spec = pltpu.PrefetchScalarGridSpec(
        num_scalar_prefetch=4,
        grid=(n_blocks,),
        in_specs=[
            pl.BlockSpec((bm * ROW_SUB, LANES), lambda j, be, nu, nx, pa: (j, 0)),
            pl.BlockSpec(memory_space=pl.ANY),
            pl.BlockSpec((1, 1, 2 * D_FF), lambda j, be, nu, nx, pa: (be[j], 0, 0)),
            pl.BlockSpec(memory_space=pl.ANY),
            pl.BlockSpec((1, 1, d), lambda j, be, nu, nx, pa: (be[j], 0, 0)),
        ],
        out_specs=pl.BlockSpec((bm * ROW_SUB, LANES), lambda j, be, nu, nx, pa: (j, 0)),
        scratch_shapes=[
            pltpu.VMEM((2, d, 2 * D_FF), F32),
            pltpu.VMEM((2, D_FF, d), F32),
            pltpu.VMEM((d, 2 * D_FF), BF16),
            pltpu.VMEM((D_FF, d), BF16),
            pltpu.SemaphoreType.DMA((2, 2)),
        ],
    )
    return pl.pallas_call(
        _moe_body,
        grid_spec=grid_spec,
        out_shape=jax.ShapeDtypeStruct((n_blocks * bm * ROW_SUB, LANES), F32),
        compiler_params=_params(("arbitrary",)),
        name="moe_ffn",
    )(blk_e, n_used, nxt_e, parity, xs, w1, b1, w2, b2)


def _combine_body(dest_ref, x2_ref, route_ref, gfin_ref, yb_hbm, op_ref, os_ref, buf0, buf1, sem, *, np_steps):
    i = pl.program_id(0)
    n_steps = pl.num_programs(0)
    tm = COMBINE_TM
    n_tok = n_steps * (2 * tm)
    bufs = (buf0, buf1)

    def issue(tile, which):
        base = tile * tm
        for r in range(tm):
            for kk in range(TOP_K):
                slot = dest_ref[kk * n_tok + base + r]
                src = pl.ds(pl.multiple_of(slot * ROW_SUB, ROW_SUB), ROW_SUB)
                pltpu.make_async_copy(yb_hbm.at[src], bufs[which].at[kk, r * ROW_SUB:(r + 1) * ROW_SUB],
                                      sem.at[which]).start(priority=kk % 2)

    def wait(which):
        for kk in range(TOP_K):
            pltpu.make_async_copy(yb_hbm.at[pl.ds(0, tm * ROW_SUB)], bufs[which].at[kk], sem.at[which]).wait()

    def reduce(which, half):
        rows = slice(half * tm, (half + 1) * tm)
        acc = x2_ref[rows, :]
        for kk in range(TOP_K):
            gate = route_ref[rows, ROUTE_GATE + kk:ROUTE_GATE + kk + 1]
            acc = acc + gate * _load_row_tiles(bufs[which].at[kk], tm)
        return _rms(acc, gfin_ref[...])

    @pl.when(i == 0)
    def _():
        issue(0, 0)

    wait(0)
    issue(2 * i + 1, 1)
    y0 = reduce(0, 0)
    wait(1)
    issue(jnp.minimum(2 * i + 2, 2 * n_steps - 1), 0)
    y1 = reduce(1, 1)

    @pl.when(i < np_steps)
    def _():
        op_ref[0:tm, :] = y0
        op_ref[tm:2 * tm, :] = y1

    @pl.when(i >= np_steps)
    def _():
        os_ref[0:tm, :] = y0
        os_ref[tm:2 * tm, :] = y1

    @pl.when(i == n_steps - 1)
    def _():
        wait(0)


def _combine(dest, x2, route, gfin, yb, n_p):
    n, d = x2.shape
    tm = COMBINE_TM
    step = 2 * tm
    np_steps = n_p // step
    grid_spec = pltpu.PrefetchScalarGridSpec(
        num_scalar_prefetch=1,
        grid=(n // step,),
        in_specs=[
            pl.BlockSpec((step, d), lambda i, dest: (i, 0)),
            pl.BlockSpec((step, LANES), lambda i, dest: (i, 0)),
            pl.BlockSpec((1, d), lambda i, dest: (0, 0)),
            pl.BlockSpec(memory_space=pl.ANY),
        ],
        out_specs=[
            pl.BlockSpec((step, d), lambda i, dest: (jnp.minimum(i, np_steps - 1), 0)),
            pl.BlockSpec((step, d), lambda i, dest: (jnp.maximum(i - np_steps, 0), 0)),
        ],
        scratch_shapes=[
            pltpu.VMEM((TOP_K, tm * ROW_SUB, LANES), F32),
            pltpu.VMEM((TOP_K, tm * ROW_SUB, LANES), F32),
            pltpu.SemaphoreType.DMA((2,)),
        ],
    )
    return pl.pallas_call(
        functools.partial(_combine_body, np_steps=np_steps),
        grid_spec=grid_spec,
        out_shape=[jax.ShapeDtypeStruct((n_p, d), F32), jax.ShapeDtypeStruct((n - n_p, d), F32)],
        compiler_params=_params(("arbitrary",)),
        name="moe_combine",
    )(dest, x2, route, gfin, yb)


def _slot_tables(route_t, counts_row, n_blocks):
    bm = MOE_BM
    e = route_t[ROUTE_E:ROUTE_E + TOP_K].astype(jnp.int32)
    rank = route_t[ROUTE_RANK:ROUTE_RANK + TOP_K].astype(jnp.int32)
    counts = counts_row[:N_EXPERTS].astype(jnp.int32)
    padded = (counts + bm - 1) // bm * bm
    pad_end = jnp.cumsum(padded)
    pad_start = pad_end - padded
    experts = jnp.arange(N_EXPERTS, dtype=jnp.int32)[:, None, None]
    dest = (jnp.sum(jnp.where(e[None] == experts, pad_start[:, None, None], 0), axis=0) + rank).reshape(-1)
    blk_start = jnp.arange(n_blocks, dtype=jnp.int32) * bm
    blk_e = jnp.minimum(jnp.sum(pad_end[None, :] <= blk_start[:, None], axis=1), N_EXPERTS - 1).astype(jnp.int32)
    n_used = pad_end[-1] // bm
    last_blk = jnp.where(padded > 0, pad_end // bm - 1, -1)
    trailing = n_used + jnp.arange(N_EXPERTS, dtype=jnp.int32)
    trailing = jnp.where(trailing < n_blocks, trailing, -1)
    zlist = jnp.concatenate([last_blk, trailing]).astype(jnp.int32)
    ids = jnp.arange(N_EXPERTS, dtype=jnp.int32)
    present = counts > 0
    later = jnp.where(jnp.logical_and(ids[None, :] > ids[:, None], present[None, :]), ids[None, :], N_EXPERTS)
    next_present = jnp.min(later, axis=1)
    next_present = jnp.where(next_present < N_EXPERTS, next_present, -1)
    nxt_e = next_present.astype(jnp.int32)
    parity = ((jnp.cumsum(present.astype(jnp.int32)) - 1) % 2).astype(jnp.int32)
    return dest.astype(jnp.int32), blk_e, n_used.astype(jnp.int32).reshape(1), nxt_e, parity, zlist


def kernel(x_prompt, x_sample, mem_prompt, state_shift, state_wkv, cache_mem_k, cache_mem_v, g_norm_mix, w_in, mu_shift, w0, w_decay_up, a0, w_a_up, w_g_up, k_k, k_a, r_k, ln_x_g, ln_x_b, gmlp_ln_g, gmlp_ln_b, w_spatial, b_spatial, g_norm_mem, w_mem_kv, w_br_rwkv, w_br_gmlp, w_br_mem, w_out, g_norm_ffn, w_router, b_router, w_exp1, b_exp1, w_exp2, b_exp2, g_norm_final):
    bp, tp, d = x_prompt.shape
    bs, ts, _ = x_sample.shape
    n_p, n_s = bp * tp, bs * ts
    n_all = n_p + n_s
    l = 0
    row = lambda a: a.reshape(1, -1)

    x_pair = (x_prompt.reshape(n_p, d), x_sample.reshape(n_s, d))
    zr, zg, zq, sg = _in_proj(x_pair[0], x_pair[1], row(g_norm_mix[l]), w_in[l].astype(BF16))

    mk_p, mv_p = _mem_kv(mem_prompt.reshape(bp * N_MEM, d), row(g_norm_mem[l]), w_mem_kv[l].astype(BF16))

    rp = dict(mu=row(mu_shift[l]), w0=row(w0[l]), wd=w_decay_up[l].astype(BF16), a0=row(a0[l]),
              wa=w_a_up[l].astype(BF16), wg=w_g_up[l].astype(BF16), kk=row(k_k[l]), ka=row(k_a[l]),
              rk=row(r_k[l]), lng=row(ln_x_g[l]), lnb=row(ln_x_b[l]))
    head_of = jnp.arange(RWKV_WIDTH, dtype=jnp.int32) // HEAD_DIM
    rp["hsum"] = (head_of[:, None] == head_of[None, :]).astype(BF16)
    o_rw_p, s_p, shift_p = _rwkv(zr.reshape(n_all // RWKV_TT, RWKV_TT, SHIFT_WIDTH), 0, bp, tp // RWKV_TT,
                        jnp.zeros((bp, 1, SHIFT_WIDTH), F32), jnp.zeros((bp, RWKV_HEADS, HEAD_DIM, HEAD_DIM), F32),
                        rp, tt=RWKV_TT, c=RWKV_C, nseq=1)
    tile_s = RWKV_SAMPLE_SEQS * ts
    o_rw_s, s_s, shift_s = _rwkv(zr.reshape(n_all // tile_s, tile_s, SHIFT_WIDTH), n_p // tile_s, bs, 1,
                        state_shift[l].reshape(bs, 1, SHIFT_WIDTH), state_wkv[l], rp, tt=tile_s, c=ts,
                        nseq=RWKV_SAMPLE_SEQS)

    tri = jnp.tril(jnp.ones((CHUNK, CHUNK), bool))
    ws_p = jnp.where(tri, w_spatial[l], 0.0).astype(BF16)
    bs_p = b_spatial[l].T
    reps = CHUNK // ts
    tri_s = jnp.tril(jnp.ones((ts, ts), bool))
    ws_small = jnp.where(tri_s, w_spatial[l][:, :ts, :ts], 0.0)
    eye = jnp.eye(reps, dtype=F32)
    ws_s = jnp.einsum("ab,gij->gaibj", eye, ws_small).reshape(GMLP_GROUPS, CHUNK, CHUNK).astype(BF16)
    bs_s = jnp.tile(b_spatial[l][:, :ts], (1, reps)).T
    lng, lnb = row(gmlp_ln_g[l]), row(gmlp_ln_b[l])
    o_gm_p, _ = _gmlp(zg, 0, n_p, lng, lnb, ws_p, bs_p)
    o_gm_s, v_rows_s = _gmlp(zg, n_p // (GMLP_TILE * CHUNK), n_s, lng, lnb, ws_s, bs_s)

    o_me_p = _mem_attn(zq.reshape(n_all // ATTN_TT, ATTN_TT, MEM_WIDTH), 0, bp, tp // ATTN_TT,
                       mk_p.reshape(bp, N_MEM, MEM_WIDTH), mv_p.reshape(bp, N_MEM, MEM_WIDTH), tt=ATTN_TT, bb=1,
                       feature_major=False)
    mk_s = jnp.transpose(cache_mem_k[l].reshape(bs, N_MEM, MEM_WIDTH), (0, 2, 1))
    mv_s = jnp.transpose(cache_mem_v[l].reshape(bs, N_MEM, MEM_WIDTH), (0, 2, 1))
    o_me_s = _mem_attn(zq.reshape(n_all // ts, ts, MEM_WIDTH), n_p // ts, bs, 1, mk_s, mv_s, tt=ts, bb=ATTN_BB,
                       feature_major=True)

    wrt_pad = jnp.zeros((d, LANES), F32).at[:, :N_EXPERTS].set(w_router[l])
    brt_pad = jnp.zeros((1, LANES), F32).at[0, :N_EXPERTS].set(b_router[l])
    x2, h2, route, route_t, counts = _merge(
        x_pair, (o_rw_p.reshape(n_p, RWKV_WIDTH), o_rw_s.reshape(n_s, RWKV_WIDTH)), (o_gm_p, o_gm_s),
        (o_me_p.reshape(n_p, MEM_WIDTH), o_me_s.reshape(n_s, MEM_WIDTH)), sg,
        w_br_rwkv[l].astype(BF16), w_br_gmlp[l].astype(BF16), w_br_mem[l].astype(BF16), w_out[l].astype(BF16),
        row(g_norm_ffn[l]), wrt_pad, brt_pad)

    n_assign = n_all * TOP_K
    n_blocks = -(-(n_assign + N_EXPERTS * (MOE_BM - 1)) // MOE_BM)
    dest, blk_e, n_used, nxt_e, parity, zlist = _slot_tables(route_t, counts[0], n_blocks)
    xs = _dispatch(dest, zlist, h2.reshape(n_all, ROW_SUB, LANES), n_blocks)
    yb = _moe(blk_e, n_used, nxt_e, parity, xs.reshape(n_blocks * MOE_BM * ROW_SUB, LANES), w_exp1[l],
              b_exp1[l].reshape(N_EXPERTS, 1, 2 * D_FF), w_exp2[l], b_exp2[l].reshape(N_EXPERTS, 1, d), n_blocks)
    y_p, y_s = _combine(dest, x2, route, row(g_norm_final), yb, n_p)

    mk_out = mk_p.reshape(1, bp, N_MEM, MEM_HEADS, MEM_WIDTH // MEM_HEADS)
    mv_out = mv_p.reshape(1, bp, N_MEM, MEM_HEADS, MEM_WIDTH // MEM_HEADS)
    return (y_p.reshape(bp, tp, d), y_s.reshape(bs, ts, d), shift_p.reshape(1, bp, SHIFT_WIDTH), s_p[None], mk_out,
            mv_out, shift_s.reshape(1, bs, SHIFT_WIDTH), s_s[None],
            v_rows_s.reshape(1, bs, ts, GMLP_WIDTH))
```

```python
import functools
import math

import jax
import jax.numpy as jnp
from jax import lax
from jax.experimental import pallas as pl
from jax.experimental.pallas import tpu as pltpu

F32 = jnp.float32
BF16 = jnp.bfloat16

D_MODEL = 1024
RWKV_HEADS = 8
HEAD_DIM = 64
RWKV_WIDTH = RWKV_HEADS * HEAD_DIM
DECAY_LORA = 64
A_LORA = 64
GATE_LORA = 128
GMLP_GROUPS = 4
GMLP_WIDTH = 256
CHUNK = 128
MEM_HEADS = 4
MEM_WIDTH = 256
N_MEM = 256
N_EXPERTS = 32
TOP_K = 4
D_FF = 1024
SWIGLU_ALPHA = 1.702
SWIGLU_LIMIT = 7.0
RMS_EPS = 1e-5
LN_EPS = 1e-5
GN_EPS = 64e-5
SHIFT_WIDTH = 3 * RWKV_WIDTH + DECAY_LORA + A_LORA + GATE_LORA
OFF_GMLP = SHIFT_WIDTH
OFF_QMEM = OFF_GMLP + 2 * GMLP_WIDTH
OFF_GATE = OFF_QMEM + MEM_WIDTH
IN_WIDTH = OFF_GATE + 3 * D_MODEL
LANES = 128
ROW_SUB = D_MODEL // LANES

PROJ_TM = 512
MERGE_TM = 512
RWKV_TT = 512
RWKV_C = 64
RWKV_GROUP = 8
RWKV_SAMPLE_SEQS = 8
GMLP_TILE = 4
ATTN_TT = 512
ATTN_BB = 8
MOE_BM = 512
DISPATCH_TM = 1024
COMBINE_TM = 128

ROUTE_E = 0
ROUTE_RANK = 4
ROUTE_GATE = 8

NN = ((1,), (0,))
NT = ((1,), (1,))
TN = ((0,), (0,))

VMEM_LIMIT = 56 * 1024 * 1024


def _params(sem, vmem=VMEM_LIMIT):
    return pltpu.CompilerParams(dimension_semantics=sem, vmem_limit_bytes=vmem)


def _bdot(a, b, dims=NN):
    return lax.dot_general(a.astype(BF16), b.astype(BF16), (dims, ((), ())), preferred_element_type=F32)


def _split(x):
    hi = x.astype(BF16)
    lo = (x - hi.astype(F32)).astype(BF16)
    return hi, lo


def _dot3(a, b, dims=NN):
    dn = (dims, ((), ()))
    ah, al = _split(a)
    bh, bl = _split(b)
    r = lax.dot_general(ah, bh, dn, preferred_element_type=F32)
    r = r + lax.dot_general(al, bh, dn, preferred_element_type=F32)
    return r + lax.dot_general(ah, bl, dn, preferred_element_type=F32)


def _rms(x, g):
    return x * lax.rsqrt(jnp.mean(x * x, axis=-1, keepdims=True) + RMS_EPS) * g


def _store_row_tiles(ref, x):
    m = x.shape[0]
    for j in range(ROW_SUB):
        ref[pl.ds(j, m, stride=ROW_SUB), :] = x[:, j * LANES:(j + 1) * LANES]


def _load_row_tiles(ref, m):
    return jnp.concatenate([ref[pl.ds(j, m, stride=ROW_SUB), :] for j in range(ROW_SUB)], axis=1)


def _sigmoid(x):
    return 0.5 * jnp.tanh(0.5 * x) + 0.5


def _in_proj_body(xp_ref, xs_ref, g_ref, w_ref, zr_ref, zg_ref, zq_ref, sg_ref, *, np_tiles):
    x = jnp.where(pl.program_id(0) < np_tiles, xp_ref[...], xs_ref[...])
    h = _rms(x, g_ref[...]).astype(BF16)
    zr_ref[...] = jnp.dot(h, w_ref[:, 0:OFF_GMLP], preferred_element_type=F32)
    zg_ref[...] = jnp.dot(h, w_ref[:, OFF_GMLP:OFF_QMEM], preferred_element_type=F32)
    zq_ref[...] = jnp.dot(h, w_ref[:, OFF_QMEM:OFF_GATE], preferred_element_type=F32).astype(zq_ref.dtype)
    gates = jnp.dot(h, w_ref[:, OFF_GATE:IN_WIDTH], preferred_element_type=F32)
    sg_ref[...] = _sigmoid(gates).astype(BF16)


def _in_proj(x_p, x_s, g, w_bf16):
    n_p = x_p.shape[0]
    n = n_p + x_s.shape[0]
    tm = PROJ_TM
    np_tiles = n_p // tm
    row = lambda i: (i, 0)
    const = lambda i: (0, 0)
    return pl.pallas_call(
        functools.partial(_in_proj_body, np_tiles=np_tiles),
        grid=(n // tm,),
        in_specs=[
            pl.BlockSpec((tm, D_MODEL), lambda i: (jnp.minimum(i, np_tiles - 1), 0)),
            pl.BlockSpec((tm, D_MODEL), lambda i: (jnp.maximum(i - np_tiles, 0), 0)),
            pl.BlockSpec((1, D_MODEL), const),
            pl.BlockSpec((D_MODEL, IN_WIDTH), const, pipeline_mode=pl.Buffered(1)),
        ],
        out_specs=[
            pl.BlockSpec((tm, SHIFT_WIDTH), row),
            pl.BlockSpec((tm, 2 * GMLP_WIDTH), row),
            pl.BlockSpec((tm, MEM_WIDTH), row),
            pl.BlockSpec((tm, 3 * D_MODEL), row),
        ],
        out_shape=[
            jax.ShapeDtypeStruct((n, SHIFT_WIDTH), F32),
            jax.ShapeDtypeStruct((n, 2 * GMLP_WIDTH), F32),
            jax.ShapeDtypeStruct((n, MEM_WIDTH), BF16),
            jax.ShapeDtypeStruct((n, 3 * D_MODEL), BF16),
        ],
        compiler_params=_params(("parallel",)),
        name="in_proj",
    )(x_p, x_s, g, w_bf16)


def _mem_kv_body(x_ref, g_ref, w_ref, k_ref, v_ref):
    h = _rms(x_ref[...], g_ref[...]).astype(BF16)
    k_ref[...] = jnp.dot(h, w_ref[:, 0:MEM_WIDTH], preferred_element_type=F32)
    v_ref[...] = jnp.dot(h, w_ref[:, MEM_WIDTH:2 * MEM_WIDTH], preferred_element_type=F32)


def _mem_kv(mem, g, w_bf16):
    n = mem.shape[0]
    tm = PROJ_TM
    return pl.pallas_call(
        _mem_kv_body,
        grid=(n // tm,),
        in_specs=[
            pl.BlockSpec((tm, D_MODEL), lambda i: (i, 0)),
            pl.BlockSpec((1, D_MODEL), lambda i: (0, 0)),
            pl.BlockSpec((D_MODEL, 2 * MEM_WIDTH), lambda i: (0, 0)),
        ],
        out_specs=[pl.BlockSpec((tm, MEM_WIDTH), lambda i: (i, 0))] * 2,
        out_shape=[jax.ShapeDtypeStruct((n, MEM_WIDTH), F32)] * 2,
        compiler_params=_params(("parallel",)),
        name="mem_kv",
    )(mem, g, w_bf16)


EXP_M05 = math.exp(-0.5)


def _unit_lower_inverse(lows, c):
    rows = lax.broadcasted_iota(jnp.int32, (c, c), 0)
    cols = lax.broadcasted_iota(jnp.int32, (c, c), 1)
    eye = (rows == cols).astype(F32)
    invs = [eye - low for low in lows]
    powers = lows
    for _ in range(int(math.log2(c)) - 1):
        powers = [_bdot(pw, pw) for pw in powers]
        invs = [inv + _bdot(inv, pw) for inv, pw in zip(invs, powers)]
    return invs


def _rwkv_body(zr_ref, shift_ref, s0_ref, mu_ref, w0_ref, wd_ref, a0_ref, wa_ref, wg_ref, kk_ref, ka_ref,
               rk_ref, lng_ref, lnb_ref, hsum_ref,
               o_ref, sout_ref, shout_ref,
               carry_ref, state_ref, r_s, k_s, v_s, kk_s, b_s, ld_s, y_s, bon_s, g_s,
               rw_s, y0_s, gm_s, h0_s, *, tt, c, nseq):
    i = pl.program_id(1)
    rw = RWKV_WIDTH
    nh = RWKV_HEADS

    @pl.when(i == 0)
    def _():
        for q in range(nseq):
            state_ref[q * nh:(q + 1) * nh] = s0_ref[q]

    z = zr_ref[0]
    z_prev = pltpu.roll(z, 1, 0)
    row = lax.broadcasted_iota(jnp.int32, z.shape, 0)
    if nseq == 1:
        @pl.when(i == 0)
        def _():
            carry_ref[...] = shift_ref[0]

        z_prev = jnp.where(row == 0, carry_ref[...], z_prev)
        carry_ref[...] = z[tt - 1:tt, :]
    else:
        first_rows = jnp.concatenate([jnp.broadcast_to(shift_ref[q], (c, SHIFT_WIDTH)) for q in range(nseq)], axis=0)
        z_prev = jnp.where(row % c == 0, first_rows, z_prev)
    zs = z + mu_ref[...] * (z_prev - z)
    r = zs[:, 0:rw]
    k = zs[:, rw:2 * rw]
    v = zs[:, 2 * rw:3 * rw]
    zw = zs[:, 3 * rw:3 * rw + DECAY_LORA]
    za = zs[:, 3 * rw + DECAY_LORA:3 * rw + DECAY_LORA + A_LORA]
    zg = zs[:, 3 * rw + DECAY_LORA + A_LORA:SHIFT_WIDTH]
    xw = w0_ref[...] + _bdot(jnp.tanh(zw), wd_ref[...])
    ld_s[...] = -EXP_M05 * _sigmoid(xw)
    a = _sigmoid(a0_ref[...] + _bdot(za, wa_ref[...]))
    g_s[...] = _bdot(_sigmoid(zg), wg_ref[...])
    kk = k * kk_ref[...]
    k = k * (1.0 + (a - 1.0) * ka_ref[...])
    r_s[...] = r
    k_s[...] = k
    v_s[...] = v

    def head_sum(t):
        return jnp.dot(t.astype(BF16), hsum_ref[...], preferred_element_type=F32)

    kk = kk / jnp.maximum(jnp.sqrt(head_sum(kk * kk)), 1e-12)
    kk_s[...] = kk
    b_s[...] = kk * a
    bon_s[...] = head_sum(r * k * rk_ref[...]) * v

    rows_c = lax.broadcasted_iota(jnp.int32, (c, c), 0)
    cols_c = lax.broadcasted_iota(jnp.int32, (c, c), 1)
    strict = rows_c > cols_c
    incl = rows_c >= cols_c
    tril_ones = jnp.where(incl, 1.0, 0.0).astype(BF16)
    rows_2c = lax.broadcasted_iota(jnp.int32, (c, 2 * c), 0)
    cols_2c = lax.broadcasted_iota(jnp.int32, (c, 2 * c), 1)
    incl2 = rows_2c >= jnp.where(cols_2c >= c, cols_2c - c, cols_2c)
    rows_k = lax.broadcasted_iota(jnp.int32, (HEAD_DIM, HEAD_DIM), 0)
    cols_k = lax.broadcasted_iota(jnp.int32, (HEAD_DIM, HEAD_DIM), 1)
    eye_k = (rows_k == cols_k).astype(F32)
    zeros_cv = jnp.zeros((c, HEAD_DIM), F32)
    heads = [slice(h * HEAD_DIM, (h + 1) * HEAD_DIM) for h in range(RWKV_HEADS)]

    n_chunks = tt // c
    group = nseq if nseq > 1 else math.gcd(n_chunks, RWKV_GROUP)

    def chunk_rows(ci):
        return pl.ds(ci * c, c) if isinstance(ci, int) else pl.ds(pl.multiple_of(ci * c, c), c)

    def scaled(ci):
        rows = chunk_rows(ci)
        ld = ld_s[rows, :]
        ld_hi, ld_lo = _split(ld)
        cum = (jnp.dot(tril_ones, ld_hi, preferred_element_type=F32)
               + jnp.dot(tril_ones, ld_lo, preferred_element_type=F32))
        e_inc = jnp.exp(cum)
        e_neg = jnp.exp(-cum)
        kt = k_s[rows, :] * e_neg
        bt = b_s[rows, :] * e_neg
        g_end = e_inc[c - 1:c, :]
        return dict(ci=ci, rt=r_s[rows, :] * e_inc, kkt=kk_s[rows, :] * jnp.exp(cum - ld), kt=kt, bt=bt,
                    g_end=g_end, bc=bt * g_end, kc=kt * g_end, vv=v_s[rows, :])

    def phase1(cj, carry):
        chains = [(ch, h, s) for ch in [scaled(cj * group + g) for g in range(group)]
                  for h, s in enumerate(heads)]
        x = [jnp.concatenate([ch["kkt"][:, s], ch["rt"][:, s]], axis=0) for ch, _, s in chains]
        zz = [jnp.concatenate([ch["bt"][:, s], ch["kt"][:, s]], axis=0) for ch, _, s in chains]
        amat = [_bdot(xh, zh, NT) for xh, zh in zip(x, zz)]
        l_b = [jnp.where(strict, am[0:c, 0:c], 0.0) for am in amat]
        l_k = [jnp.where(strict, am[0:c, c:2 * c], 0.0) for am in amat]
        a_r = [jnp.where(incl2, am[c:2 * c, :], 0.0) for am in amat]
        tinv = _unit_lower_inverse(l_b, c)
        lkv = [_bdot(lk, ch["vv"][:, s]) for lk, (ch, _, s) in zip(l_k, chains)]
        wu = [-_bdot(t, jnp.concatenate([ch["kkt"][:, s], lv], axis=1))
              for t, lv, (ch, _, s) in zip(tinv, lkv, chains)]
        m = [jnp.concatenate([w, jnp.concatenate([zeros_cv, ch["vv"][:, s]], axis=1)], axis=0)
             for w, (ch, _, s) in zip(wu, chains)]
        am2 = [_bdot(ar, mh) for ar, mh in zip(a_r, m)]
        gh = [_bdot(mh, jnp.concatenate([ch["bc"][:, s], ch["kc"][:, s]], axis=0), TN)
              for mh, (ch, _, s) in zip(m, chains)]
        for q, (ch, h, s) in enumerate(chains):
            idx = ch["ci"] * RWKV_HEADS + h
            rw_s[idx] = ch["rt"][:, s] + am2[q][:, 0:HEAD_DIM]
            y0_s[idx] = am2[q][:, HEAD_DIM:2 * HEAD_DIM]
            gm_s[idx] = gh[q][0:HEAD_DIM, :] + eye_k * ch["g_end"][:, s]
            h0_s[idx] = gh[q][HEAD_DIM:2 * HEAD_DIM, :]
        return carry

    if n_chunks == group:
        phase1(0, 0)
    else:
        lax.fori_loop(0, n_chunks // group, phase1, 0)

    if nseq == 1:
        def phase2(ci, carry):
            rows = chunk_rows(ci)
            for h, s in enumerate(heads):
                idx = ci * nh + h
                s_h = state_ref[h]
                y_s[rows, s] = _bdot(rw_s[idx], s_h, NT) + y0_s[idx]
                state_ref[h] = _bdot(s_h, gm_s[idx]) + h0_s[idx]
            return carry

        lax.fori_loop(0, n_chunks, phase2, 0)
    else:
        pairs = [(q, h, s) for q in range(nseq) for h, s in enumerate(heads)]
        states = [state_ref[q * nh + h] for q, h, _ in pairs]
        ys = [_bdot(rw_s[q * nh + h], st, NT) + y0_s[q * nh + h] for (q, h, _), st in zip(pairs, states)]
        new = [_bdot(st, gm_s[q * nh + h]) + h0_s[q * nh + h] for (q, h, _), st in zip(pairs, states)]
        for (q, h, s), yq, nq in zip(pairs, ys, new):
            y_s[chunk_rows(q), s] = yq
            state_ref[q * nh + h] = nq
    y = y_s[...]
    yc = y - head_sum(y) * (1.0 / HEAD_DIM)
    var = head_sum(yc * yc) * (1.0 / HEAD_DIM)
    yn = yc * lax.rsqrt(var + GN_EPS)
    o_ref[0] = ((yn * lng_ref[...] + lnb_ref[...] + bon_s[...]) * g_s[...]).astype(o_ref.dtype)

    @pl.when(i == pl.num_programs(1) - 1)
    def _():
        for q in range(nseq):
            sout_ref[q] = state_ref[q * nh:(q + 1) * nh]
            last = tt - 1 if nseq == 1 else (q + 1) * c - 1
            shout_ref[q] = zr_ref[0, last:last + 1, :]


def _rwkv(zr3, blk_off, b, nt, shift_prev, s_prev, p, tt, c, nseq):
    vec = lambda n: pl.BlockSpec((1, n), lambda bi, ti: (0, 0))
    mat = lambda m, n: pl.BlockSpec((m, n), lambda bi, ti: (0, 0))
    rw = RWKV_WIDTH
    scr = lambda: pltpu.VMEM((tt, rw), F32)
    nch = (tt // c) * RWKV_HEADS
    return pl.pallas_call(
        functools.partial(_rwkv_body, tt=tt, c=c, nseq=nseq),
        grid=(b // nseq, nt),
        in_specs=[
            pl.BlockSpec((1, tt, SHIFT_WIDTH), lambda bi, ti: (blk_off + bi * nt + ti, 0, 0)),
            pl.BlockSpec((nseq, 1, SHIFT_WIDTH), lambda bi, ti: (bi, 0, 0)),
            pl.BlockSpec((nseq, RWKV_HEADS, HEAD_DIM, HEAD_DIM), lambda bi, ti: (bi, 0, 0, 0)),
            vec(SHIFT_WIDTH), vec(rw), mat(DECAY_LORA, rw), vec(rw), mat(A_LORA, rw), mat(GATE_LORA, rw),
            vec(rw), vec(rw), vec(rw), vec(rw), vec(rw), mat(rw, rw),
        ],
        out_specs=[
            pl.BlockSpec((1, tt, rw), lambda bi, ti: (bi, ti, 0)),
            pl.BlockSpec((nseq, RWKV_HEADS, HEAD_DIM, HEAD_DIM), lambda bi, ti: (bi, 0, 0, 0)),
            pl.BlockSpec((nseq, 1, SHIFT_WIDTH), lambda bi, ti: (bi, 0, 0)),
        ],
        out_shape=[
            jax.ShapeDtypeStruct((b // nseq, nt * tt, rw), BF16),
            jax.ShapeDtypeStruct((b, RWKV_HEADS, HEAD_DIM, HEAD_DIM), F32),
            jax.ShapeDtypeStruct((b, 1, SHIFT_WIDTH), F32),
        ],
        scratch_shapes=[
            pltpu.VMEM((1, SHIFT_WIDTH), F32),
            pltpu.VMEM((nseq * RWKV_HEADS, HEAD_DIM, HEAD_DIM), F32),
            scr(), scr(), scr(), scr(), scr(), scr(), scr(), scr(), scr(),
            pltpu.VMEM((nch, c, HEAD_DIM), F32),
            pltpu.VMEM((nch, c, HEAD_DIM), F32),
            pltpu.VMEM((nch, HEAD_DIM, HEAD_DIM), F32),
            pltpu.VMEM((nch, HEAD_DIM, HEAD_DIM), F32),
        ],
        compiler_params=_params(("parallel", "arbitrary")),
        name="rwkv",
    )(zr3, shift_prev, s_prev, p["mu"], p["w0"], p["wd"], p["a0"], p["wa"], p["wg"], p["kk"], p["ka"],
      p["rk"], p["lng"], p["lnb"], p["hsum"])


def _gmlp_body(z_ref, lng_ref, lnb_ref, ws_ref, bs_ref, o_ref, v_ref):
    z = z_ref[...]
    ge = 0.5 * z * (1.0 + lax.erf(z * (1.0 / math.sqrt(2.0))))
    u = ge[:, 0:GMLP_WIDTH]
    v = ge[:, GMLP_WIDTH:2 * GMLP_WIDTH]
    mean = jnp.mean(v, axis=-1, keepdims=True)
    vc = v - mean
    var = jnp.mean(vc * vc, axis=-1, keepdims=True)
    vn = vc * lax.rsqrt(var + LN_EPS) * lng_ref[...] + lnb_ref[...]
    v_ref[...] = vn
    gd = GMLP_WIDTH // GMLP_GROUPS
    for q in range(GMLP_TILE):
        rows = slice(q * CHUNK, (q + 1) * CHUNK)
        for g in range(GMLP_GROUPS):
            gs = slice(g * gd, (g + 1) * gd)
            sv = jnp.dot(ws_ref[g], vn[rows, gs].astype(BF16), preferred_element_type=F32) + bs_ref[:, g:g + 1]
            o_ref[rows, gs] = (u[rows, gs] * sv).astype(o_ref.dtype)


def _gmlp(zg, tile_off, n_rows, lng, lnb, ws_bf16, bs):
    tile = GMLP_TILE * CHUNK
    return pl.pallas_call(
        _gmlp_body,
        grid=(n_rows // tile,),
        in_specs=[
            pl.BlockSpec((tile, 2 * GMLP_WIDTH), lambda i: (tile_off + i, 0)),
            pl.BlockSpec((1, GMLP_WIDTH), lambda i: (0, 0)),
            pl.BlockSpec((1, GMLP_WIDTH), lambda i: (0, 0)),
            pl.BlockSpec((GMLP_GROUPS, CHUNK, CHUNK), lambda i: (0, 0, 0)),
            pl.BlockSpec((CHUNK, GMLP_GROUPS), lambda i: (0, 0)),
        ],
        out_specs=[
            pl.BlockSpec((tile, GMLP_WIDTH), lambda i: (i, 0)),
            pl.BlockSpec((tile, GMLP_WIDTH), lambda i: (i, 0)),
        ],
        out_shape=[
            jax.ShapeDtypeStruct((n_rows, GMLP_WIDTH), BF16),
            jax.ShapeDtypeStruct((n_rows, GMLP_WIDTH), F32),
        ],
        compiler_params=_params(("parallel",)),
        name="gmlp",
    )(zg, lng, lnb, ws_bf16, bs)


def _mem_attn_body(q_ref, k_ref, v_ref, o_ref, *, bb, feature_major):
    hd = MEM_WIDTH // MEM_HEADS
    pairs = [(s, slice(h * hd, (h + 1) * hd)) for s in range(bb) for h in range(MEM_HEADS)]
    if feature_major:
        scores = [_bdot(q_ref[s, :, hs], k_ref[s, hs, :], NN) * (hd ** -0.5) for s, hs in pairs]
    else:
        scores = [_bdot(q_ref[s, :, hs], k_ref[s, :, hs], NT) * (hd ** -0.5) for s, hs in pairs]
    probs = []
    for sc in scores:
        pr = jnp.exp(sc - jnp.max(sc, axis=-1, keepdims=True))
        probs.append(pr / jnp.sum(pr, axis=-1, keepdims=True))
    for (s, hs), pr in zip(pairs, probs):
        if feature_major:
            o_ref[s, :, hs] = _bdot(pr, v_ref[s, hs, :], NT).astype(o_ref.dtype)
        else:
            o_ref[s, :, hs] = _bdot(pr, v_ref[s, :, hs]).astype(o_ref.dtype)


def _mem_attn(q3, blk_off, b, nt, mk, mv, tt, bb, feature_major):
    return pl.pallas_call(
        functools.partial(_mem_attn_body, bb=bb, feature_major=feature_major),
        grid=(b // bb, nt),
        in_specs=[
            pl.BlockSpec((bb, tt, MEM_WIDTH), lambda bi, ti: (blk_off // bb + bi * nt + ti, 0, 0)),
            pl.BlockSpec((bb, N_MEM, MEM_WIDTH), lambda bi, ti: (bi, 0, 0)),
            pl.BlockSpec((bb, N_MEM, MEM_WIDTH), lambda bi, ti: (bi, 0, 0)),
        ],
        out_specs=pl.BlockSpec((bb, tt, MEM_WIDTH), lambda bi, ti: (bi, ti, 0)),
        out_shape=jax.ShapeDtypeStruct((b, nt * tt, MEM_WIDTH), BF16),
        compiler_params=_params(("parallel", "parallel")),
        name="mem_attn",
    )(q3, mk, mv)


def _merge_body(x_p, x_s, orw_p, orw_s, ogm_p, ogm_s, ome_p, ome_s, sg_ref, wbr_ref, wbg_ref, wbm_ref, wout_ref,
                gffn_ref, wrt_ref, brt_ref, x2_ref, h2_ref, route_ref, route_t_ref, cnt_ref, count_s, *, np_tiles):
    i = pl.program_id(0)
    d = D_MODEL
    tm = x_p.shape[0]
    is_p = i < np_tiles
    x = jnp.where(is_p, x_p[...], x_s[...])
    orw = jnp.where(is_p, orw_p[...], orw_s[...])
    ogm = jnp.where(is_p, ogm_p[...], ogm_s[...])
    ome = jnp.where(is_p, ome_p[...], ome_s[...])
    merged = sg_ref[:, 0:d].astype(F32) * _bdot(orw, wbr_ref[...])
    merged = merged + sg_ref[:, d:2 * d].astype(F32) * _bdot(ogm, wbg_ref[...])
    merged = merged + sg_ref[:, 2 * d:3 * d].astype(F32) * _bdot(ome, wbm_ref[...])
    x2 = x + _bdot(merged, wout_ref[...])
    x2_ref[...] = x2
    h2 = _rms(x2, gffn_ref[...])
    _store_row_tiles(h2_ref, h2)

    @pl.when(i == 0)
    def _():
        count_s[...] = jnp.zeros_like(count_s)

    lane = lax.broadcasted_iota(jnp.int32, (tm, LANES), 1)
    logits = jnp.where(lane < N_EXPERTS, _dot3(h2, wrt_ref[...]) + brt_ref[...], -jnp.inf)
    lane_f = lane.astype(F32)
    tops, hots, idxs = [], [], []
    for _ in range(TOP_K):
        top = jnp.max(logits, axis=-1, keepdims=True)
        idx = jnp.min(jnp.where(logits == top, lane_f, float(LANES)), axis=-1, keepdims=True)
        hot = lane_f == idx
        logits = jnp.where(hot, -jnp.inf, logits)
        tops.append(top)
        hots.append(hot)
        idxs.append(idx)
    weights = [jnp.exp(t - tops[0]) for t in tops]
    denom = weights[0] + weights[1] + weights[2] + weights[3]
    onehot = jnp.zeros((tm, LANES), F32)
    for hot in hots:
        onehot = onehot + jnp.where(hot, 1.0, 0.0)
    rows_t = lax.broadcasted_iota(jnp.int32, (tm, tm), 0)
    cols_t = lax.broadcasted_iota(jnp.int32, (tm, tm), 1)
    ahead = jnp.where(rows_t > cols_t, 1.0, 0.0).astype(BF16)
    prefix = jnp.dot(ahead, onehot.astype(BF16), preferred_element_type=F32) + count_s[...]
    route = jnp.zeros((tm, LANES), F32)
    for kk in range(TOP_K):
        rank = jnp.sum(jnp.where(hots[kk], prefix, 0.0), axis=-1, keepdims=True)
        route = jnp.where(lane == ROUTE_E + kk, idxs[kk], route)
        route = jnp.where(lane == ROUTE_RANK + kk, rank, route)
        route = jnp.where(lane == ROUTE_GATE + kk, weights[kk] / denom, route)
    route_ref[...] = route
    route_t_ref[...] = route.T
    count_s[...] = count_s[...] + jnp.sum(onehot, axis=0, keepdims=True)
    cnt_ref[...] = jnp.broadcast_to(count_s[...], cnt_ref.shape)


def _merge(x, o_rw, o_gm, o_me, sg, wbr, wbg, wbm, wout, gffn, wrt_pad, brt_pad):
    n_p = x[0].shape[0]
    n = n_p + x[1].shape[0]
    tm = MERGE_TM
    d = D_MODEL
    np_tiles = n_p // tm
    row = lambda i: (i, 0)
    const = lambda i: (0, 0)
    first =lambda i: (jnp.minimum(i, np_tiles - 1), 0)
    second = lambda i: (jnp.maximum(i - np_tiles, 0), 0)

    def pair(width):
        return [pl.BlockSpec((tm, width), first), pl.BlockSpec((tm, width), second)]

    return pl.pallas_call(
        functools.partial(_merge_body, np_tiles=np_tiles),
        grid=(n // tm,),
        in_specs=pair(d) + pair(RWKV_WIDTH) + pair(GMLP_WIDTH) + pair(MEM_WIDTH) + [
            pl.BlockSpec((tm, 3 * d), row),
            pl.BlockSpec((RWKV_WIDTH, d), const),
            pl.BlockSpec((GMLP_WIDTH, d), const),
            pl.BlockSpec((MEM_WIDTH, d), const),
            pl.BlockSpec((d, d), const),
            pl.BlockSpec((1, d), const),
            pl.BlockSpec((d, LANES), const),
            pl.BlockSpec((1, LANES), const),
        ],
        out_specs=[
            pl.BlockSpec((tm, d), row),
            pl.BlockSpec((tm * ROW_SUB, LANES), row),
            pl.BlockSpec((tm, LANES), row),
            pl.BlockSpec((LANES, tm), lambda i: (0, i)),
            pl.BlockSpec((8, LANES), const),
        ],
        out_shape=[
            jax.ShapeDtypeStruct((n, d), F32),
            jax.ShapeDtypeStruct((n * ROW_SUB, LANES), F32),
            jax.ShapeDtypeStruct((n, LANES), F32),
            jax.ShapeDtypeStruct((LANES, n), F32),
            jax.ShapeDtypeStruct((8, LANES), F32),
        ],
        scratch_shapes=[pltpu.VMEM((1, LANES), F32)],
        compiler_params=_params(("arbitrary",)),
        name="merge",
    )(x[0], x[1], o_rw[0], o_rw[1], o_gm[0], o_gm[1], o_me[0], o_me[1], sg, wbr, wbg, wbm, wout, gffn, wrt_pad,
      brt_pad)


N_ZERO_BLOCKS = 2 * N_EXPERTS


def _dispatch_body(dest_ref, zlist_ref, h_hbm, xs_hbm, zero_s, hbuf, sem_z, sem_in, sem):
    i = pl.program_id(0)
    tm = DISPATCH_TM
    bm = MOE_BM

    @pl.when(i == 0)
    def _():
        zero_s[...] = jnp.zeros_like(zero_s)

        def zero_copy(q):
            start = pl.multiple_of(zlist_ref[q] * bm, bm)
            return pltpu.make_async_copy(zero_s, xs_hbm.at[pl.ds(start, bm)], sem_z)

        def start(q, carry):
            @pl.when(zlist_ref[q] >= 0)
            def _():
                zero_copy(q).start()
            return carry

        def wait(q, carry):
            @pl.when(zlist_ref[q] >= 0)
            def _():
                zero_copy(q).wait()
            return carry

        lax.fori_loop(0, N_ZERO_BLOCKS, start, 0)
        lax.fori_loop(0, N_ZERO_BLOCKS, wait, 0)

    base = i * tm
    n_tok = pl.num_programs(0) * tm

    n_steps = pl.num_programs(0)
    parity = lax.rem(i, 2)
    buf = lax.rem(i, 3)

    def fetch(tile, which):
        return pltpu.make_async_copy(h_hbm.at[pl.ds(pl.multiple_of(tile * tm, tm), tm)], hbuf.at[which],
                                     sem_in.at[which])

    @pl.when(i == 0)
    def _():
        fetch(0, 0).start()

    fetch(i, buf).wait()

    @pl.when(i + 1 < n_steps)
    def _():
        fetch(i + 1, lax.rem(i + 1, 3)).start()

    def body(r, carry):
        for kk in range(TOP_K):
            slot = dest_ref[kk * n_tok + base + r]
            pltpu.make_async_copy(hbuf.at[buf, r], xs_hbm.at[slot], sem.at[parity]).start(priority=kk % 2)
        return carry

    lax.fori_loop(0, tm, body, 0, unroll=8)

    def drain(which):
        for kk in range(TOP_K):
            pltpu.make_async_copy(hbuf.at[0], xs_hbm.at[pl.ds(0, tm)], sem.at[which]).wait()

    @pl.when(i > 0)
    def _():
        drain(1 - parity)

    @pl.when(i == n_steps - 1)
    def _():
        drain(parity)


def _dispatch(dest, zlist, h, n_blocks):
    n = h.shape[0]
    tm = DISPATCH_TM
    grid_spec = pltpu.PrefetchScalarGridSpec(
        num_scalar_prefetch=2,
        grid=(n // tm,),
        in_specs=[pl.BlockSpec(memory_space=pl.ANY)],
        out_specs=pl.BlockSpec(memory_space=pl.ANY),
        scratch_shapes=[
            pltpu.VMEM((MOE_BM, ROW_SUB, LANES), F32),
            pltpu.VMEM((3, tm, ROW_SUB, LANES), F32),
            pltpu.SemaphoreType.DMA(()),
            pltpu.SemaphoreType.DMA((3,)),
            pltpu.SemaphoreType.DMA((2,)),
        ],
    )
    return pl.pallas_call(
        _dispatch_body,
        grid_spec=grid_spec,
        out_shape=jax.ShapeDtypeStruct((n_blocks * MOE_BM, ROW_SUB, LANES), F32),
        compiler_params=_params(("arbitrary",)),
        name="moe_dispatch",
    )(dest, zlist, h)


CAST_ROWS = 64


def _moe_body(be_ref, nused_ref, nxt_ref, par_ref, x_ref, w1_hbm, b1_ref, w2_hbm, b2_ref, o_ref,
              w1f, w2f, w1b, w2b, sem):
    j = pl.program_id(0)
    n_used = nused_ref[0]

    def weight_copies(expert, slot):
        return (pltpu.make_async_copy(w1_hbm.at[expert], w1f.at[slot], sem.at[slot, 0]),
                pltpu.make_async_copy(w2_hbm.at[expert], w2f.at[slot], sem.at[slot, 1]))

    @pl.when(j == 0)
    def _():
        for cp in weight_copies(be_ref[0], 0):
            cp.start()

    @pl.when(j < n_used)
    def _():
        slot = par_ref[be_ref[j]]
        changed = jnp.logical_or(j == 0, be_ref[j] != be_ref[jnp.maximum(j - 1, 0)])

        @pl.when(changed)
        def _():
            for cp in weight_copies(be_ref[j], slot):
                cp.wait()

            nxt = nxt_ref[be_ref[j]]

            @pl.when(nxt >= 0)
            def _():
                for cp in weight_copies(nxt, 1 - slot):
                    cp.start()

            def cast1(q, carry):
                rows = pl.ds(pl.multiple_of(q * CAST_ROWS, CAST_ROWS), CAST_ROWS)
                w1b[rows, :] = w1f[slot, rows, :].astype(BF16)
                return carry

            def cast2(q, carry):
                rows = pl.ds(pl.multiple_of(q * CAST_ROWS, CAST_ROWS), CAST_ROWS)
                w2b[rows, :] = w2f[slot, rows, :].astype(BF16)
                return carry

            lax.fori_loop(0, D_MODEL // CAST_ROWS, cast1, 0)
            lax.fori_loop(0, D_FF // CAST_ROWS, cast2, 0)

        x = _load_row_tiles(x_ref, MOE_BM).astype(BF16)
        z = jnp.dot(x, w1b[...], preferred_element_type=F32) + b1_ref[0]
        zg = jnp.minimum(z[:, 0:D_FF], SWIGLU_LIMIT)
        zl = jnp.clip(z[:, D_FF:2 * D_FF], -SWIGLU_LIMIT, SWIGLU_LIMIT)
        act = zg * _sigmoid(SWIGLU_ALPHA * zg) * (zl + 1.0)
        _store_row_tiles(o_ref, jnp.dot(act.astype(BF16), w2b[...], preferred_element_type=F32) + b2_ref[0])

    @pl.when(j >= n_used)
    def _():
        o_ref[...] = jnp.zeros_like(o_ref)


def _moe(blk_e, n_used, nxt_e, parity, xs, w1, b1, w2, b2, n_blocks):
    bm = MOE_BM
    d = D_MODEL
    grid_spec = pltpu.PrefetchScalarGridSpec(
        num_scalar_prefetch=4,
        grid=(n_blocks,),
        in_specs=[
            pl.BlockSpec((bm * ROW_SUB, LANES), lambda j, be, nu, nx, pa: (j, 0)),
            pl.BlockSpec(memory_space=pl.ANY),
            pl.BlockSpec((1, 1, 2 * D_FF), lambda j, be, nu, nx, pa: (be[j], 0, 0)),
            pl.BlockSpec(memory_space=pl.ANY),
            pl.BlockSpec((1, 1, d), lambda j, be, nu, nx, pa: (be[j], 0, 0)),
        ],
        out_specs=pl.BlockSpec((bm * ROW_SUB, LANES), lambda j, be, nu, nx, pa: (j, 0)),
        scratch_shapes=[
            pltpu.VMEM((2, d, 2 * D_FF), F32),
            pltpu.VMEM((2, D_FF, d), F32),
            pltpu.VMEM((d, 2 * D_FF), BF16),
            pltpu.VMEM((D_FF, d), BF16),
            pltpu.SemaphoreType.DMA((2, 2)),
        ],
    )
    return pl.pallas_call(
        _moe_body,
        grid_spec=grid_spec,
        out_shape=jax.ShapeDtypeStruct((n_blocks * bm * ROW_SUB, LANES), F32),
        compiler_params=_params(("arbitrary",)),
        name="moe_ffn",
    )(blk_e, n_used, nxt_e, parity, xs, w1, b1, w2, b2)


def _combine_body(dest_ref, x2_ref, route_ref, gfin_ref, yb_hbm, op_ref, os_ref, buf0, buf1, sem, *, np_steps):
    i = pl.program_id(0)
    n_steps = pl.num_programs(0)
    tm = COMBINE_TM
    n_tok = n_steps * (2 * tm)
    bufs = (buf0, buf1)

    def issue(tile, which):
        base = tile * tm
        for r in range(tm):
            for kk in range(TOP_K):
                slot = dest_ref[kk * n_tok + base + r]
                src = pl.ds(pl.multiple_of(slot * ROW_SUB, ROW_SUB), ROW_SUB)
                pltpu.make_async_copy(yb_hbm.at[src], bufs[which].at[kk, r * ROW_SUB:(r + 1) * ROW_SUB],
                                      sem.at[which]).start(priority=kk % 2)

    def wait(which):
        for kk in range(TOP_K):
            pltpu.make_async_copy(yb_hbm.at[pl.ds(0, tm * ROW_SUB)], bufs[which].at[kk], sem.at[which]).wait()

    def reduce(which, half):
        rows = slice(half * tm, (half + 1) * tm)
        acc = x2_ref[rows, :]
        for kk in range(TOP_K):
            gate = route_ref[rows, ROUTE_GATE + kk:ROUTE_GATE + kk + 1]
            acc = acc + gate * _load_row_tiles(bufs[which].at[kk], tm)
        return _rms(acc, gfin_ref[...])

    @pl.when(i == 0)
    def _():
        issue(0, 0)

    wait(0)
    issue(2 * i + 1, 1)
    y0 = reduce(0, 0)
    wait(1)
    issue(jnp.minimum(2 * i + 2, 2 * n_steps - 1), 0)
    y1 = reduce(1, 1)

    @pl.when(i < np_steps)
    def _():
        op_ref[0:tm, :] = y0
        op_ref[tm:2 * tm, :] = y1

    @pl.when(i >= np_steps)
    def _():
        os_ref[0:tm, :] = y0
        os_ref[tm:2 * tm, :] = y1

    @pl.when(i == n_steps - 1)
    def _():
        wait(0)


def _combine(dest, x2, route, gfin, yb, n_p):
    n, d = x2.shape
    tm = COMBINE_TM
    step = 2 * tm
    np_steps = n_p // step
    grid_spec = pltpu.PrefetchScalarGridSpec(
        num_scalar_prefetch=1,
        grid=(n // step,),
        in_specs=[
            pl.BlockSpec((step, d), lambda i, dest: (i, 0)),
            pl.BlockSpec((step, LANES), lambda i, dest: (i, 0)),
            pl.BlockSpec((1, d), lambda i, dest: (0, 0)),
            pl.BlockSpec(memory_space=pl.ANY),
        ],
        out_specs=[
            pl.BlockSpec((step, d), lambda i, dest: (jnp.minimum(i, np_steps - 1), 0)),
            pl.BlockSpec((step, d), lambda i, dest: (jnp.maximum(i - np_steps, 0), 0)),
        ],
        scratch_shapes=[
            pltpu.VMEM((TOP_K, tm * ROW_SUB, LANES), F32),
            pltpu.VMEM((TOP_K, tm * ROW_SUB, LANES), F32),
            pltpu.SemaphoreType.DMA((2,)),
        ],
    )
    return pl.pallas_call(
        functools.partial(_combine_body, np_steps=np_steps),
        grid_spec=grid_spec,
        out_shape=[jax.ShapeDtypeStruct((n_p, d), F32), jax.ShapeDtypeStruct((n - n_p, d), F32)],
        compiler_params=_params(("arbitrary",)),
        name="moe_combine",
    )(dest, x2, route, gfin, yb)


def _slot_tables(route_t, counts_row, n_blocks):
    bm = MOE_BM
    e = route_t[ROUTE_E:ROUTE_E + TOP_K].astype(jnp.int32)
    rank = route_t[ROUTE_RANK:ROUTE_RANK + TOP_K].astype(jnp.int32)
    counts = counts_row[:N_EXPERTS].astype(jnp.int32)
    padded = (counts + bm - 1) // bm * bm
    pad_end = jnp.cumsum(padded)
    pad_start = pad_end - padded
    experts = jnp.arange(N_EXPERTS, dtype=jnp.int32)[:, None, None]
    dest = (jnp.sum(jnp.where(e[None] == experts, pad_start[:, None, None], 0), axis=0) + rank).reshape(-1)
    blk_start = jnp.arange(n_blocks, dtype=jnp.int32) * bm
    blk_e = jnp.minimum(jnp.sum(pad_end[None, :] <= blk_start[:, None], axis=1), N_EXPERTS - 1).astype(jnp.int32)
    n_used = pad_end[-1] // bm
    last_blk = jnp.where(padded > 0, pad_end // bm - 1, -1)
    trailing = n_used + jnp.arange(N_EXPERTS, dtype=jnp.int32)
    trailing = jnp.where(trailing < n_blocks, trailing, -1)
    zlist = jnp.concatenate([last_blk, trailing]).astype(jnp.int32)
    ids = jnp.arange(N_EXPERTS, dtype=jnp.int32)
    present = counts > 0
    later = jnp.where(jnp.logical_and(ids[None, :] > ids[:, None], present[None, :]), ids[None, :], N_EXPERTS)
    next_present = jnp.min(later, axis=1)
    next_present = jnp.where(next_present < N_EXPERTS, next_present, -1)
    nxt_e = next_present.astype(jnp.int32)
    parity = ((jnp.cumsum(present.astype(jnp.int32)) - 1) % 2).astype(jnp.int32)
    return dest.astype(jnp.int32), blk_e, n_used.astype(jnp.int32).reshape(1), nxt_e, parity, zlist


def kernel(x_prompt, x_sample, mem_prompt, state_shift, state_wkv, cache_mem_k, cache_mem_v, g_norm_mix, w_in, mu_shift, w0, w_decay_up, a0, w_a_up, w_g_up, k_k, k_a, r_k, ln_x_g, ln_x_b, gmlp_ln_g, gmlp_ln_b, w_spatial, b_spatial, g_norm_mem, w_mem_kv, w_br_rwkv, w_br_gmlp, w_br_mem, w_out, g_norm_ffn, w_router, b_router, w_exp1, b_exp1, w_exp2, b_exp2, g_norm_final):
    bp, tp, d = x_prompt.shape
    bs, ts, _ = x_sample.shape
    n_p, n_s = bp * tp, bs * ts
    n_all = n_p + n_s
    l = 0
    row = lambda a: a.reshape(1, -1)

    x_pair = (x_prompt.reshape(n_p, d), x_sample.reshape(n_s, d))
    zr, zg, zq, sg = _in_proj(x_pair[0], x_pair[1], row(g_norm_mix[l]), w_in[l].astype(BF16))

    mk_p, mv_p = _mem_kv(mem_prompt.reshape(bp * N_MEM, d), row(g_norm_mem[l]), w_mem_kv[l].astype(BF16))

    rp = dict(mu=row(mu_shift[l]), w0=row(w0[l]), wd=w_decay_up[l].astype(BF16), a0=row(a0[l]),
              wa=w_a_up[l].astype(BF16), wg=w_g_up[l].astype(BF16), kk=row(k_k[l]), ka=row(k_a[l]),
              rk=row(r_k[l]), lng=row(ln_x_g[l]), lnb=row(ln_x_b[l]))
    head_of = jnp.arange(RWKV_WIDTH, dtype=jnp.int32) // HEAD_DIM
    rp["hsum"] = (head_of[:, None] == head_of[None, :]).astype(BF16)
    o_rw_p, s_p, shift_p = _rwkv(zr.reshape(n_all // RWKV_TT, RWKV_TT, SHIFT_WIDTH), 0, bp, tp // RWKV_TT,
                        jnp.zeros((bp, 1, SHIFT_WIDTH), F32), jnp.zeros((bp, RWKV_HEADS, HEAD_DIM, HEAD_DIM), F32),
                        rp, tt=RWKV_TT, c=RWKV_C, nseq=1)
    tile_s = RWKV_SAMPLE_SEQS * ts
    o_rw_s, s_s, shift_s = _rwkv(zr.reshape(n_all // tile_s, tile_s, SHIFT_WIDTH), n_p // tile_s, bs, 1,
                        state_shift[l].reshape(bs, 1, SHIFT_WIDTH), state_wkv[l], rp, tt=tile_s, c=ts,
                        nseq=RWKV_SAMPLE_SEQS)

    tri = jnp.tril(jnp.ones((CHUNK, CHUNK), bool))
    ws_p = jnp.where(tri, w_spatial[l], 0.0).astype(BF16)
    bs_p = b_spatial[l].T
    reps = CHUNK // ts
    tri_s = jnp.tril(jnp.ones((ts, ts), bool))
    ws_small = jnp.where(tri_s, w_spatial[l][:, :ts, :ts], 0.0)
    eye = jnp.eye(reps, dtype=F32)
    ws_s = jnp.einsum("ab,gij->gaibj", eye, ws_small).reshape(GMLP_GROUPS, CHUNK, CHUNK).astype(BF16)
    bs_s = jnp.tile(b_spatial[l][:, :ts], (1, reps)).T
    lng, lnb = row(gmlp_ln_g[l]), row(gmlp_ln_b[l])
    o_gm_p, _ = _gmlp(zg, 0, n_p, lng, lnb, ws_p, bs_p)
    o_gm_s, v_rows_s = _gmlp(zg, n_p // (GMLP_TILE * CHUNK), n_s, lng, lnb, ws_s, bs_s)

    o_me_p = _mem_attn(zq.reshape(n_all // ATTN_TT, ATTN_TT, MEM_WIDTH), 0, bp, tp // ATTN_TT,
                       mk_p.reshape(bp, N_MEM, MEM_WIDTH), mv_p.reshape(bp, N_MEM, MEM_WIDTH), tt=ATTN_TT, bb=1,
                       feature_major=False)
    mk_s = jnp.transpose(cache_mem_k[l].reshape(bs, N_MEM, MEM_WIDTH), (0, 2, 1))
    mv_s = jnp.transpose(cache_mem_v[l].reshape(bs, N_MEM, MEM_WIDTH), (0, 2, 1))
    o_me_s = _mem_attn(zq.reshape(n_all // ts, ts, MEM_WIDTH), n_p // ts, bs, 1, mk_s, mv_s, tt=ts, bb=ATTN_BB,
                       feature_major=True)

    wrt_pad = jnp.zeros((d, LANES), F32).at[:, :N_EXPERTS].set(w_router[l])
    brt_pad = jnp.zeros((1, LANES), F32).at[0, :N_EXPERTS].set(b_router[l])
    x2, h2, route, route_t, counts = _merge(
        x_pair, (o_rw_p.reshape(n_p, RWKV_WIDTH), o_rw_s.reshape(n_s, RWKV_WIDTH)), (o_gm_p, o_gm_s),
        (o_me_p.reshape(n_p, MEM_WIDTH), o_me_s.reshape(n_s, MEM_WIDTH)), sg,
        w_br_rwkv[l].astype(BF16), w_br_gmlp[l].astype(BF16), w_br_mem[l].astype(BF16), w_out[l].astype(BF16),
        row(g_norm_ffn[l]), wrt_pad, brt_pad)

    n_assign = n_all * TOP_K
    n_blocks = -(-(n_assign + N_EXPERTS * (MOE_BM - 1)) // MOE_BM)
    dest, blk_e, n_used, nxt_e, parity, zlist = _slot_tables(route_t, counts[0], n_blocks)
    xs = _dispatch(dest, zlist, h2.reshape(n_all, ROW_SUB, LANES), n_blocks)
    yb = _moe(blk_e, n_used, nxt_e, parity, xs.reshape(n_blocks * MOE_BM * ROW_SUB, LANES), w_exp1[l],
              b_exp1[l].reshape(N_EXPERTS, 1, 2 * D_FF), w_exp2[l], b_exp2[l].reshape(N_EXPERTS, 1, d), n_blocks)
    y_p, y_s = _combine(dest, x2, route, row(g_norm_final), yb, n_p)

    mk_out = mk_p.reshape(1, bp, N_MEM, MEM_HEADS, MEM_WIDTH // MEM_HEADS)
    mv_out = mv_p.reshape(1, bp, N_MEM, MEM_HEADS, MEM_WIDTH // MEM_HEADS)
    return (y_p.reshape(bp, tp, d), y_s.reshape(bs, ts, d), shift_p.reshape(1, bp, SHIFT_WIDTH), s_p[None], mk_out,
            mv_out, shift_s.reshape(1, bs, SHIFT_WIDTH), s_s[None],
            v_rows_s.reshape(1, bs, ts, GMLP_WIDTH))
```

```python
import functools
import math

import jax
import jax.numpy as jnp
from jax import lax
from jax.experimental import pallas as pl
from jax.experimental.pallas import tpu as pltpu

F32 = jnp.float32
BF16 = jnp.bfloat16

D_MODEL = 1024
RWKV_HEADS = 8
HEAD_DIM = 64
RWKV_WIDTH = RWKV_HEADS * HEAD_DIM
DECAY_LORA = 64
A_LORA = 64
GATE_LORA = 128
GMLP_GROUPS = 4
GMLP_WIDTH = 256
CHUNK = 128
MEM_HEADS = 4
MEM_WIDTH = 256
N_MEM = 256
N_EXPERTS = 32
TOP_K = 4
D_FF = 1024
SWIGLU_ALPHA = 1.702
SWIGLU_LIMIT = 7.0
RMS_EPS = 1e-5
LN_EPS = 1e-5
GN_EPS = 64e-5
SHIFT_WIDTH = 3 * RWKV_WIDTH + DECAY_LORA + A_LORA + GATE_LORA
OFF_GMLP = SHIFT_WIDTH
OFF_QMEM = OFF_GMLP + 2 * GMLP_WIDTH
OFF_GATE = OFF_QMEM + MEM_WIDTH
IN_WIDTH = OFF_GATE + 3 * D_MODEL
LANES = 128
ROW_SUB = D_MODEL // LANES

PROJ_TM = 512
MERGE_TM = 512
RWKV_TT = 512
RWKV_C = 64
RWKV_GROUP = 8
RWKV_SAMPLE_SEQS = 16
GMLP_TILE = 8
ATTN_TT = 1024
ATTN_BB = 8
MOE_BM = 512
DISPATCH_TM = 1024
COMBINE_TM = 128

ROUTE_E = 0
ROUTE_RANK = 4
ROUTE_GATE = 8

NN = ((1,), (0,))
NT = ((1,), (1,))
TN = ((0,), (0,))

VMEM_LIMIT = 56 * 1024 * 1024


def _params(sem, vmem=VMEM_LIMIT):
    return pltpu.CompilerParams(dimension_semantics=sem, vmem_limit_bytes=vmem)


def _bdot(a, b, dims=NN):
    return lax.dot_general(a.astype(BF16), b.astype(BF16), (dims, ((), ())), preferred_element_type=F32)


def _split(x):
    hi = x.astype(BF16)
    lo = (x - hi.astype(F32)).astype(BF16)
    return hi, lo


def _dot3(a, b, dims=NN):
    dn = (dims, ((), ()))
    ah, al = _split(a)
    bh, bl = _split(b)
    r = lax.dot_general(ah, bh, dn, preferred_element_type=F32)
    r = r + lax.dot_general(al, bh, dn, preferred_element_type=F32)
    return r + lax.dot_general(ah, bl, dn, preferred_element_type=F32)


def _rms(x, g):
    return x * lax.rsqrt(jnp.mean(x * x, axis=-1, keepdims=True) + RMS_EPS) * g


def _store_row_tiles(ref, x):
    m = x.shape[0]
    for j in range(ROW_SUB):
        ref[pl.ds(j, m, stride=ROW_SUB), :] = x[:, j * LANES:(j + 1) * LANES]


def _load_row_tiles(ref, m):
    return jnp.concatenate([ref[pl.ds(j, m, stride=ROW_SUB), :] for j in range(ROW_SUB)], axis=1)


def _sigmoid(x):
    return 0.5 * jnp.tanh(0.5 * x) + 0.5


def _in_proj_body(xp_ref, xs_ref, g_ref, w_ref, zr_ref, zg_ref, zq_ref, sg_ref, *, np_tiles):
    x = jnp.where(pl.program_id(0) < np_tiles, xp_ref[...], xs_ref[...])
    h = _rms(x, g_ref[...]).astype(BF16)
    zr_ref[...] = jnp.dot(h, w_ref[:, 0:OFF_GMLP], preferred_element_type=F32)
    zg_ref[...] = jnp.dot(h, w_ref[:, OFF_GMLP:OFF_QMEM], preferred_element_type=F32)
    zq_ref[...] = jnp.dot(h, w_ref[:, OFF_QMEM:OFF_GATE], preferred_element_type=F32).astype(zq_ref.dtype)
    gates = jnp.dot(h, w_ref[:, OFF_GATE:IN_WIDTH], preferred_element_type=F32)
    sg_ref[...] = _sigmoid(gates).astype(BF16)


def _in_proj(x_p, x_s, g, w_bf16):
    n_p = x_p.shape[0]
    n = n_p + x_s.shape[0]
    tm = PROJ_TM
    np_tiles = n_p // tm
    row = lambda i: (i, 0)
    const = lambda i: (0, 0)
    return pl.pallas_call(
        functools.partial(_in_proj_body, np_tiles=np_tiles),
        grid=(n // tm,),
        in_specs=[
            pl.BlockSpec((tm, D_MODEL), lambda i: (jnp.minimum(i, np_tiles - 1), 0)),
            pl.BlockSpec((tm, D_MODEL), lambda i: (jnp.maximum(i - np_tiles, 0), 0)),
            pl.BlockSpec((1, D_MODEL), const),
            pl.BlockSpec((D_MODEL, IN_WIDTH), const, pipeline_mode=pl.Buffered(1)),
        ],
        out_specs=[
            pl.BlockSpec((tm, SHIFT_WIDTH), row),
            pl.BlockSpec((tm, 2 * GMLP_WIDTH), row),
            pl.BlockSpec((tm, MEM_WIDTH), row),
            pl.BlockSpec((tm, 3 * D_MODEL), row),
        ],
        out_shape=[
            jax.ShapeDtypeStruct((n, SHIFT_WIDTH), F32),
            jax.ShapeDtypeStruct((n, 2 * GMLP_WIDTH), F32),
            jax.ShapeDtypeStruct((n, MEM_WIDTH), BF16),
            jax.ShapeDtypeStruct((n, 3 * D_MODEL), BF16),
        ],
        compiler_params=_params(("parallel",)),
        name="in_proj",
    )(x_p, x_s, g, w_bf16)


def _mem_kv_body(x_ref, g_ref, w_ref, k_ref, v_ref):
    h = _rms(x_ref[...], g_ref[...]).astype(BF16)
    k_ref[...] = jnp.dot(h, w_ref[:, 0:MEM_WIDTH], preferred_element_type=F32)
    v_ref[...] = jnp.dot(h, w_ref[:, MEM_WIDTH:2 * MEM_WIDTH], preferred_element_type=F32)


def _mem_kv(mem, g, w_bf16):
    n = mem.shape[0]
    tm = PROJ_TM
    return pl.pallas_call(
        _mem_kv_body,
        grid=(n // tm,),
        in_specs=[
            pl.BlockSpec((tm, D_MODEL), lambda i: (i, 0)),
            pl.BlockSpec((1, D_MODEL), lambda i: (0, 0)),
            pl.BlockSpec((D_MODEL, 2 * MEM_WIDTH), lambda i: (0, 0)),
        ],
        out_specs=[pl.BlockSpec((tm, MEM_WIDTH), lambda i: (i, 0))] * 2,
        out_shape=[jax.ShapeDtypeStruct((n, MEM_WIDTH), F32)] * 2,
        compiler_params=_params(("parallel",)),
        name="mem_kv",
    )(mem, g, w_bf16)


EXP_M05 = math.exp(-0.5)


def _unit_lower_inverse(lows, c):
    rows = lax.broadcasted_iota(jnp.int32, (c, c), 0)
    cols = lax.broadcasted_iota(jnp.int32, (c, c), 1)
    eye = (rows == cols).astype(F32)
    invs = [eye - low for low in lows]
    powers = lows
    for _ in range(int(math.log2(c)) - 1):
        powers = [_bdot(pw, pw) for pw in powers]
        invs = [inv + _bdot(inv, pw) for inv, pw in zip(invs, powers)]
    return invs


def _rwkv_body(zr_ref, shift_ref, s0_ref, mu_ref, w0_ref, wd_ref, a0_ref, wa_ref, wg_ref, kk_ref, ka_ref,
               rk_ref, lng_ref, lnb_ref, hsum_ref,
               o_ref, sout_ref, shout_ref,
               carry_ref, state_ref, r_s, k_s, v_s, kk_s, b_s, ld_s, y_s, bon_s, g_s,
               rw_s, y0_s, gm_s, h0_s, *, tt, c, nseq):
    i = pl.program_id(1)
    rw = RWKV_WIDTH
    nh = RWKV_HEADS

    @pl.when(i == 0)
    def _():
        for q in range(nseq):
            state_ref[q * nh:(q + 1) * nh] = s0_ref[q]

    z = zr_ref[0]
    z_prev = pltpu.roll(z, 1, 0)
    row = lax.broadcasted_iota(jnp.int32, z.shape, 0)
    if nseq == 1:
        @pl.when(i == 0)
        def _():
            carry_ref[...] = shift_ref[0]

        z_prev = jnp.where(row == 0, carry_ref[...], z_prev)
        carry_ref[...] = z[tt - 1:tt, :]
    else:
        first_rows = jnp.concatenate([jnp.broadcast_to(shift_ref[q], (c, SHIFT_WIDTH)) for q in range(nseq)], axis=0)
        z_prev = jnp.where(row % c == 0, first_rows, z_prev)
    zs = z + mu_ref[...] * (z_prev - z)
    r = zs[:, 0:rw]
    k = zs[:, rw:2 * rw]
    v = zs[:, 2 * rw:3 * rw]
    zw = zs[:, 3 * rw:3 * rw + DECAY_LORA]
    za = zs[:, 3 * rw + DECAY_LORA:3 * rw + DECAY_LORA + A_LORA]
    zg = zs[:, 3 * rw + DECAY_LORA + A_LORA:SHIFT_WIDTH]
    xw = w0_ref[...] + _bdot(jnp.tanh(zw), wd_ref[...])
    ld_s[...] = -EXP_M05 * _sigmoid(xw)
    a = _sigmoid(a0_ref[...] + _bdot(za, wa_ref[...]))
    g_s[...] = _bdot(_sigmoid(zg), wg_ref[...])
    kk = k * kk_ref[...]
    k = k * (1.0 + (a - 1.0) * ka_ref[...])
    r_s[...] = r
    k_s[...] = k
    v_s[...] = v

    def head_sum(t):
        return jnp.dot(t.astype(BF16), hsum_ref[...], preferred_element_type=F32)

    kk = kk / jnp.maximum(jnp.sqrt(head_sum(kk * kk)), 1e-12)
    kk_s[...] = kk
    b_s[...] = kk * a
    bon_s[...] = head_sum(r * k * rk_ref[...]) * v

    rows_c = lax.broadcasted_iota(jnp.int32, (c, c), 0)
    cols_c = lax.broadcasted_iota(jnp.int32, (c, c), 1)
    strict = rows_c > cols_c
    incl = rows_c >= cols_c
    tril_ones = jnp.where(incl, 1.0, 0.0).astype(BF16)
    rows_2c = lax.broadcasted_iota(jnp.int32, (c, 2 * c), 0)
    cols_2c = lax.broadcasted_iota(jnp.int32, (c, 2 * c), 1)
    incl2 = rows_2c >= jnp.where(cols_2c >= c, cols_2c - c, cols_2c)
    rows_k = lax.broadcasted_iota(jnp.int32, (HEAD_DIM, HEAD_DIM), 0)
    cols_k = lax.broadcasted_iota(jnp.int32, (HEAD_DIM, HEAD_DIM), 1)
    eye_k = (rows_k == cols_k).astype(F32)
    zeros_cv = jnp.zeros((c, HEAD_DIM), F32)
    heads = [slice(h * HEAD_DIM, (h + 1) * HEAD_DIM) for h in range(RWKV_HEADS)]

    n_chunks = tt // c
    group = nseq if nseq > 1 else math.gcd(n_chunks, RWKV_GROUP)

    def chunk_rows(ci):
        return pl.ds(ci * c, c) if isinstance(ci, int) else pl.ds(pl.multiple_of(ci * c, c), c)

    def scaled(ci):
        rows = chunk_rows(ci)
        ld = ld_s[rows, :]
        ld_hi, ld_lo = _split(ld)
        cum = (jnp.dot(tril_ones, ld_hi, preferred_element_type=F32)
               + jnp.dot(tril_ones, ld_lo, preferred_element_type=F32))
        e_inc = jnp.exp(cum)
        e_neg = jnp.exp(-cum)
        kt = k_s[rows, :] * e_neg
        bt = b_s[rows, :] * e_neg
        g_end = e_inc[c - 1:c, :]
        return dict(ci=ci, rt=r_s[rows, :] * e_inc, kkt=kk_s[rows, :] * jnp.exp(cum - ld), kt=kt, bt=bt,
                    g_end=g_end, bc=bt * g_end, kc=kt * g_end, vv=v_s[rows, :])

    def phase1(cj, carry):
        chains = [(ch, h, s) for ch in [scaled(cj * group + g) for g in range(group)]
                  for h, s in enumerate(heads)]
        x = [jnp.concatenate([ch["kkt"][:, s], ch["rt"][:, s]], axis=0) for ch, _, s in chains]
        zz = [jnp.concatenate([ch["bt"][:, s], ch["kt"][:, s]], axis=0) for ch, _, s in chains]
        amat = [_bdot(xh, zh, NT) for xh, zh in zip(x, zz)]
        l_b = [jnp.where(strict, am[0:c, 0:c], 0.0) for am in amat]
        l_k = [jnp.where(strict, am[0:c, c:2 * c], 0.0) for am in amat]
        a_r = [jnp.where(incl2, am[c:2 * c, :], 0.0) for am in amat]
        tinv = _unit_lower_inverse(l_b, c)
        lkv = [_bdot(lk, ch["vv"][:, s]) for lk, (ch, _, s) in zip(l_k, chains)]
        wu = [-_bdot(t, jnp.concatenate([ch["kkt"][:, s], lv], axis=1))
              for t, lv, (ch, _, s) in zip(tinv, lkv, chains)]
        m = [jnp.concatenate([w, jnp.concatenate([zeros_cv, ch["vv"][:, s]], axis=1)], axis=0)
             for w, (ch, _, s) in zip(wu, chains)]
        am2 = [_bdot(ar, mh) for ar, mh in zip(a_r, m)]
        gh = [_bdot(mh, jnp.concatenate([ch["bc"][:, s], ch["kc"][:, s]], axis=0), TN)
              for mh, (ch, _, s) in zip(m, chains)]
        for q, (ch, h, s) in enumerate(chains):
            idx = ch["ci"] * RWKV_HEADS + h
            rw_s[idx] = ch["rt"][:, s] + am2[q][:, 0:HEAD_DIM]
            y0_s[idx] = am2[q][:, HEAD_DIM:2 * HEAD_DIM]
            gm_s[idx] = gh[q][0:HEAD_DIM, :] + eye_k * ch["g_end"][:, s]
            h0_s[idx] = gh[q][HEAD_DIM:2 * HEAD_DIM, :]
        return carry

    if n_chunks == group:
        phase1(0, 0)
    else:
        lax.fori_loop(0, n_chunks // group, phase1, 0)

    if nseq == 1:
        def phase2(ci, carry):
            rows = chunk_rows(ci)
            for h, s in enumerate(heads):
                idx = ci * nh + h
                s_h = state_ref[h]
                y_s[rows, s] = _bdot(rw_s[idx], s_h, NT) + y0_s[idx]
                state_ref[h] = _bdot(s_h, gm_s[idx]) + h0_s[idx]
            return carry

        lax.fori_loop(0, n_chunks, phase2, 0)
    else:
        pairs = [(q, h, s) for q in range(nseq) for h, s in enumerate(heads)]
        states = [state_ref[q * nh + h] for q, h, _ in pairs]
        ys = [_bdot(rw_s[q * nh + h], st, NT) + y0_s[q * nh + h] for (q, h, _), st in zip(pairs, states)]
        new = [_bdot(st, gm_s[q * nh + h]) + h0_s[q * nh + h] for (q, h, _), st in zip(pairs, states)]
        for (q, h, s), yq, nq in zip(pairs, ys, new):
            y_s[chunk_rows(q), s] = yq
            state_ref[q * nh + h] = nq
    y = y_s[...]
    yc = y - head_sum(y) * (1.0 / HEAD_DIM)
    var = head_sum(yc * yc) * (1.0 / HEAD_DIM)
    yn = yc * lax.rsqrt(var + GN_EPS)
    o_ref[0] = ((yn * lng_ref[...] + lnb_ref[...] + bon_s[...]) * g_s[...]).astype(o_ref.dtype)

    @pl.when(i == pl.num_programs(1) - 1)
    def _():
        for q in range(nseq):
            sout_ref[q] = state_ref[q * nh:(q + 1) * nh]
            last = tt - 1 if nseq == 1 else (q + 1) * c - 1
            shout_ref[q] = zr_ref[0, last:last + 1, :]


def _rwkv(zr3, blk_off, b, nt, shift_prev, s_prev, p, tt, c, nseq):
    vec = lambda n: pl.BlockSpec((1, n), lambda bi, ti: (0, 0))
    mat = lambda m, n: pl.BlockSpec((m, n), lambda bi, ti: (0, 0))
    rw = RWKV_WIDTH
    scr = lambda: pltpu.VMEM((tt, rw), F32)
    nch = (tt // c) * RWKV_HEADS
    return pl.pallas_call(
        functools.partial(_rwkv_body, tt=tt, c=c, nseq=nseq),
        grid=(b // nseq, nt),
        in_specs=[
            pl.BlockSpec((1, tt, SHIFT_WIDTH), lambda bi, ti: (blk_off + bi * nt + ti, 0, 0)),
            pl.BlockSpec((nseq, 1, SHIFT_WIDTH), lambda bi, ti: (bi, 0, 0)),
            pl.BlockSpec((nseq, RWKV_HEADS, HEAD_DIM, HEAD_DIM), lambda bi, ti: (bi, 0, 0, 0)),
            vec(SHIFT_WIDTH), vec(rw), mat(DECAY_LORA, rw), vec(rw), mat(A_LORA, rw), mat(GATE_LORA, rw),
            vec(rw), vec(rw), vec(rw), vec(rw), vec(rw), mat(rw, rw),
        ],
        out_specs=[
            pl.BlockSpec((1, tt, rw), lambda bi, ti: (bi, ti, 0)),
            pl.BlockSpec((nseq, RWKV_HEADS, HEAD_DIM, HEAD_DIM), lambda bi, ti: (bi, 0, 0, 0)),
            pl.BlockSpec((nseq, 1, SHIFT_WIDTH), lambda bi, ti: (bi, 0, 0)),
        ],
        out_shape=[
            jax.ShapeDtypeStruct((b // nseq, nt * tt, rw), BF16),
            jax.ShapeDtypeStruct((b, RWKV_HEADS, HEAD_DIM, HEAD_DIM), F32),
            jax.ShapeDtypeStruct((b, 1, SHIFT_WIDTH), F32),
        ],
        scratch_shapes=[
            pltpu.VMEM((1, SHIFT_WIDTH), F32),
            pltpu.VMEM((nseq * RWKV_HEADS, HEAD_DIM, HEAD_DIM), F32),
            scr(), scr(), scr(), scr(), scr(), scr(), scr(), scr(), scr(),
            pltpu.VMEM((nch, c, HEAD_DIM), F32),
            pltpu.VMEM((nch, c, HEAD_DIM), F32),
            pltpu.VMEM((nch, HEAD_DIM, HEAD_DIM), F32),
            pltpu.VMEM((nch, HEAD_DIM, HEAD_DIM), F32),
        ],
        compiler_params=_params(("parallel", "arbitrary")),
        name="rwkv",
    )(zr3, shift_prev, s_prev, p["mu"], p["w0"], p["wd"], p["a0"], p["wa"], p["wg"], p["kk"], p["ka"],
      p["rk"], p["lng"], p["lnb"], p["hsum"])


def _gmlp_body(z_ref, lng_ref, lnb_ref, ws_ref, bs_ref, o_ref, v_ref):
    z = z_ref[...]
    ge = 0.5 * z * (1.0 + lax.erf(z * (1.0 / math.sqrt(2.0))))
    u = ge[:, 0:GMLP_WIDTH]
    v = ge[:, GMLP_WIDTH:2 * GMLP_WIDTH]
    mean = jnp.mean(v, axis=-1, keepdims=True)
    vc = v - mean
    var = jnp.mean(vc * vc, axis=-1, keepdims=True)
    vn = vc * lax.rsqrt(var + LN_EPS) * lng_ref[...] + lnb_ref[...]
    v_ref[...] = vn
    gd = GMLP_WIDTH // GMLP_GROUPS
    for q in range(GMLP_TILE):
        rows = slice(q * CHUNK, (q + 1) * CHUNK)
        for g in range(GMLP_GROUPS):
            gs = slice(g * gd, (g + 1) * gd)
            sv = jnp.dot(ws_ref[g], vn[rows, gs].astype(BF16), preferred_element_type=F32) + bs_ref[:, g:g + 1]
            o_ref[rows, gs] = (u[rows, gs] * sv).astype(o_ref.dtype)


def _gmlp(zg, tile_off, n_rows, lng, lnb, ws_bf16, bs):
    tile = GMLP_TILE * CHUNK
    return pl.pallas_call(
        _gmlp_body,
        grid=(n_rows // tile,),
        in_specs=[
            pl.BlockSpec((tile, 2 * GMLP_WIDTH), lambda i: (tile_off + i, 0)),
            pl.BlockSpec((1, GMLP_WIDTH), lambda i: (0, 0)),
            pl.BlockSpec((1, GMLP_WIDTH), lambda i: (0, 0)),
            pl.BlockSpec((GMLP_GROUPS, CHUNK, CHUNK), lambda i: (0, 0, 0)),
            pl.BlockSpec((CHUNK, GMLP_GROUPS), lambda i: (0, 0)),
        ],
        out_specs=[
            pl.BlockSpec((tile, GMLP_WIDTH), lambda i: (i, 0)),
            pl.BlockSpec((tile, GMLP_WIDTH), lambda i: (i, 0)),
        ],
        out_shape=[
            jax.ShapeDtypeStruct((n_rows, GMLP_WIDTH), BF16),
            jax.ShapeDtypeStruct((n_rows, GMLP_WIDTH), F32),
        ],
        compiler_params=_params(("parallel",)),
        name="gmlp",
    )(zg, lng, lnb, ws_bf16, bs)


def _mem_attn_body(q_ref, k_ref, v_ref, o_ref, *, bb, feature_major):
    hd = MEM_WIDTH // MEM_HEADS
    pairs = [(s, slice(h * hd, (h + 1) * hd)) for s in range(bb) for h in range(MEM_HEADS)]
    if feature_major:
        scores = [_bdot(q_ref[s, :, hs], k_ref[s, hs, :], NN) * (hd ** -0.5) for s, hs in pairs]
    else:
        scores = [_bdot(q_ref[s, :, hs], k_ref[s, :, hs], NT) * (hd ** -0.5) for s, hs in pairs]
    probs = []
    for sc in scores:
        pr = jnp.exp(sc - jnp.max(sc, axis=-1, keepdims=True))
        probs.append(pr / jnp.sum(pr, axis=-1, keepdims=True))
    for (s, hs), pr in zip(pairs, probs):
        if feature_major:
            o_ref[s, :, hs] = _bdot(pr, v_ref[s, hs, :], NT).astype(o_ref.dtype)
        else:
            o_ref[s, :, hs] = _bdot(pr, v_ref[s, :, hs]).astype(o_ref.dtype)


def _mem_attn(q3, blk_off, b, nt, mk, mv, tt, bb, feature_major):
    return pl.pallas_call(
        functools.partial(_mem_attn_body, bb=bb, feature_major=feature_major),
        grid=(b // bb, nt),
        in_specs=[
            pl.BlockSpec((bb, tt, MEM_WIDTH), lambda bi, ti: (blk_off // bb + bi * nt + ti, 0, 0)),
            pl.BlockSpec((bb, N_MEM, MEM_WIDTH), lambda bi, ti: (bi, 0, 0)),
            pl.BlockSpec((bb, N_MEM, MEM_WIDTH), lambda bi, ti: (bi, 0, 0)),
        ],
        out_specs=pl.BlockSpec((bb, tt, MEM_WIDTH), lambda bi, ti: (bi, ti, 0)),
        out_shape=jax.ShapeDtypeStruct((b, nt * tt, MEM_WIDTH), BF16),
        compiler_params=_params(("parallel", "parallel")),
        name="mem_attn",
    )(q3, mk, mv)


def _merge_body(x_p, x_s, orw_p, orw_s, ogm_p, ogm_s, ome_p, ome_s, sg_ref, wbr_ref, wbg_ref, wbm_ref, wout_ref,
                gffn_ref, wrt_ref, brt_ref, x2_ref, h2_ref, route_ref, route_t_ref, cnt_ref, count_s, *, np_tiles):
    i = pl.program_id(0)
    d = D_MODEL
    tm = x_p.shape[0]
    is_p = i < np_tiles
    x = jnp.where(is_p, x_p[...], x_s[...])
    orw = jnp.where(is_p, orw_p[...], orw_s[...])
    ogm = jnp.where(is_p, ogm_p[...], ogm_s[...])
    ome = jnp.where(is_p, ome_p[...], ome_s[...])
    merged = sg_ref[:, 0:d].astype(F32) * _bdot(orw, wbr_ref[...])
    merged = merged + sg_ref[:, d:2 * d].astype(F32) * _bdot(ogm, wbg_ref[...])
    merged = merged + sg_ref[:, 2 * d:3 * d].astype(F32) * _bdot(ome, wbm_ref[...])
    x2 = x + _bdot(merged, wout_ref[...])
    x2_ref[...] = x2
    h2 = _rms(x2, gffn_ref[...])
    _store_row_tiles(h2_ref, h2)

    @pl.when(i == 0)
    def _():
        count_s[...] = jnp.zeros_like(count_s)

    lane = lax.broadcasted_iota(jnp.int32, (tm, LANES), 1)
    logits = jnp.where(lane < N_EXPERTS, _dot3(h2, wrt_ref[...]) + brt_ref[...], -jnp.inf)
    lane_f = lane.astype(F32)
    tops, hots, idxs = [], [], []
    for _ in range(TOP_K):
        top = jnp.max(logits, axis=-1, keepdims=True)
        idx = jnp.min(jnp.where(logits == top, lane_f, float(LANES)), axis=-1, keepdims=True)
        hot = lane_f == idx
        logits = jnp.where(hot, -jnp.inf, logits)
        tops.append(top)
        hots.append(hot)
        idxs.append(idx)
    weights = [jnp.exp(t - tops[0]) for t in tops]
    denom = weights[0] + weights[1] + weights[2] + weights[3]
    onehot = jnp.zeros((tm, LANES), F32)
    for hot in hots:
        onehot = onehot + jnp.where(hot, 1.0, 0.0)
    rows_t = lax.broadcasted_iota(jnp.int32, (tm, tm), 0)
    cols_t = lax.broadcasted_iota(jnp.int32, (tm, tm), 1)
    ahead = jnp.where(rows_t > cols_t, 1.0, 0.0).astype(BF16)
    prefix = jnp.dot(ahead, onehot.astype(BF16), preferred_element_type=F32) + count_s[...]
    route = jnp.zeros((tm, LANES), F32)
    for kk in range(TOP_K):
        rank = jnp.sum(jnp.where(hots[kk], prefix, 0.0), axis=-1, keepdims=True)
        route = jnp.where(lane == ROUTE_E + kk, idxs[kk], route)
        route = jnp.where(lane == ROUTE_RANK + kk, rank, route)
        route = jnp.where(lane == ROUTE_GATE + kk, weights[kk] / denom, route)
    route_ref[...] = route
    route_t_ref[...] = route.T
    count_s[...] = count_s[...] + jnp.sum(onehot, axis=0, keepdims=True)
    cnt_ref[...] = jnp.broadcast_to(count_s[...], cnt_ref.shape)


def _merge(x, o_rw, o_gm, o_me, sg, wbr, wbg, wbm, wout, gffn, wrt_pad, brt_pad):
    n_p = x[0].shape[0]
    n = n_p + x[1].shape[0]
    tm = MERGE_TM
    d = D_MODEL
    np_tiles = n_p // tm
    row = lambda i: (i, 0)
    const = lambda i: (0, 0)
    first =lambda i: (jnp.minimum(i, np_tiles - 1), 0)
    second = lambda i: (jnp.maximum(i - np_tiles, 0), 0)

    def pair(width):
        return [pl.BlockSpec((tm, width), first), pl.BlockSpec((tm, width), second)]

    return pl.pallas_call(
        functools.partial(_merge_body, np_tiles=np_tiles),
        grid=(n // tm,),
        in_specs=pair(d) + pair(RWKV_WIDTH) + pair(GMLP_WIDTH) + pair(MEM_WIDTH) + [
            pl.BlockSpec((tm, 3 * d), row),
            pl.BlockSpec((RWKV_WIDTH, d), const),
            pl.BlockSpec((GMLP_WIDTH, d), const),
            pl.BlockSpec((MEM_WIDTH, d), const),
            pl.BlockSpec((d, d), const),
            pl.BlockSpec((1, d), const),
            pl.BlockSpec((d, LANES), const),
            pl.BlockSpec((1, LANES), const),
        ],
        out_specs=[
            pl.BlockSpec((tm, d), row),
            pl.BlockSpec((tm * ROW_SUB, LANES), row),
            pl.BlockSpec((tm, LANES), row),
            pl.BlockSpec((LANES, tm), lambda i: (0, i)),
            pl.BlockSpec((8, LANES), const),
        ],
        out_shape=[
            jax.ShapeDtypeStruct((n, d), F32),
            jax.ShapeDtypeStruct((n * ROW_SUB, LANES), F32),
            jax.ShapeDtypeStruct((n, LANES), F32),
            jax.ShapeDtypeStruct((LANES, n), F32),
            jax.ShapeDtypeStruct((8, LANES), F32),
        ],
        scratch_shapes=[pltpu.VMEM((1, LANES), F32)],
        compiler_params=_params(("arbitrary",)),
        name="merge",
    )(x[0], x[1], o_rw[0], o_rw[1], o_gm[0], o_gm[1], o_me[0], o_me[1], sg, wbr, wbg, wbm, wout, gffn, wrt_pad,
      brt_pad)


N_ZERO_BLOCKS = 2 * N_EXPERTS


def _dispatch_body(dest_ref, zlist_ref, h_hbm, xs_hbm, zero_s, hbuf, sem_z, sem_in, sem):
    i = pl.program_id(0)
    tm = DISPATCH_TM
    bm = MOE_BM

    @pl.when(i == 0)
    def _():
        zero_s[...] = jnp.zeros_like(zero_s)

        def zero_copy(q):
            start = pl.multiple_of(zlist_ref[q] * bm, bm)
            return pltpu.make_async_copy(zero_s, xs_hbm.at[pl.ds(start, bm)], sem_z)

        def start(q, carry):
            @pl.when(zlist_ref[q] >= 0)
            def _():
                zero_copy(q).start()
            return carry

        def wait(q, carry):
            @pl.when(zlist_ref[q] >= 0)
            def _():
                zero_copy(q).wait()
            return carry

        lax.fori_loop(0, N_ZERO_BLOCKS, start, 0)
        lax.fori_loop(0, N_ZERO_BLOCKS, wait, 0)

    base = i * tm
    n_tok = pl.num_programs(0) * tm

    n_steps = pl.num_programs(0)
    parity = lax.rem(i, 2)
    buf = lax.rem(i, 3)

    def fetch(tile, which):
        return pltpu.make_async_copy(h_hbm.at[pl.ds(pl.multiple_of(tile * tm, tm), tm)], hbuf.at[which],
                                     sem_in.at[which])

    @pl.when(i == 0)
    def _():
        fetch(0, 0).start()

    fetch(i, buf).wait()

    @pl.when(i + 1 < n_steps)
    def _():
        fetch(i + 1, lax.rem(i + 1, 3)).start()

    def body(r, carry):
        for kk in range(TOP_K):
            slot = dest_ref[kk * n_tok + base + r]
            pltpu.make_async_copy(hbuf.at[buf, r], xs_hbm.at[slot], sem.at[parity]).start(priority=kk % 2)
        return carry

    lax.fori_loop(0, tm, body, 0, unroll=8)

    def drain(which):
        for kk in range(TOP_K):
            pltpu.make_async_copy(hbuf.at[0], xs_hbm.at[pl.ds(0, tm)], sem.at[which]).wait()

    @pl.when(i > 0)
    def _():
        drain(1 - parity)

    @pl.when(i == n_steps - 1)
    def _():
        drain(parity)


def _dispatch(dest, zlist, h, n_blocks):
    n = h.shape[0]
    tm = DISPATCH_TM
    grid_spec = pltpu.PrefetchScalarGridSpec(
        num_scalar_prefetch=2,
        grid=(n // tm,),
        in_specs=[pl.BlockSpec(memory_space=pl.ANY)],
        out_specs=pl.BlockSpec(memory_space=pl.ANY),
        scratch_shapes=[
            pltpu.VMEM((MOE_BM, ROW_SUB, LANES), F32),
            pltpu.VMEM((3, tm, ROW_SUB, LANES), F32),
            pltpu.SemaphoreType.DMA(()),
            pltpu.SemaphoreType.DMA((3,)),
            pltpu.SemaphoreType.DMA((2,)),
        ],
    )
    return pl.pallas_call(
        _dispatch_body,
        grid_spec=grid_spec,
        out_shape=jax.ShapeDtypeStruct((n_blocks * MOE_BM, ROW_SUB, LANES), F32),
        compiler_params=_params(("arbitrary",)),
        name="moe_dispatch",
    )(dest, zlist, h)


CAST_ROWS = 64


def _moe_body(be_ref, nused_ref, nxt_ref, par_ref, x_ref, w1_hbm, b1_ref, w2_hbm, b2_ref, o_ref,
              w1f, w2f, w1b, w2b, sem):
    j = pl.program_id(0)
    n_used = nused_ref[0]

    def weight_copies(expert, slot):
        return (pltpu.make_async_copy(w1_hbm.at[expert], w1f.at[slot], sem.at[slot, 0]),
                pltpu.make_async_copy(w2_hbm.at[expert], w2f.at[slot], sem.at[slot, 1]))

    @pl.when(j == 0)
    def _():
        for cp in weight_copies(be_ref[0], 0):
            cp.start()

    @pl.when(j < n_used)
    def _():
        slot = par_ref[be_ref[j]]
        changed = jnp.logical_or(j == 0, be_ref[j] != be_ref[jnp.maximum(j - 1, 0)])

        @pl.when(changed)
        def _():
            for cp in weight_copies(be_ref[j], slot):
                cp.wait()

            nxt = nxt_ref[be_ref[j]]

            @pl.when(nxt >= 0)
            def _():
                for cp in weight_copies(nxt, 1 - slot):
                    cp.start()

            def cast1(q, carry):
                rows = pl.ds(pl.multiple_of(q * CAST_ROWS, CAST_ROWS), CAST_ROWS)
                w1b[rows, :] = w1f[slot, rows, :].astype(BF16)
                return carry

            def cast2(q, carry):
                rows = pl.ds(pl.multiple_of(q * CAST_ROWS, CAST_ROWS), CAST_ROWS)
                w2b[rows, :] = w2f[slot, rows, :].astype(BF16)
                return carry

            lax.fori_loop(0, D_MODEL // CAST_ROWS, cast1, 0)
            lax.fori_loop(0, D_FF // CAST_ROWS, cast2, 0)

        x = _load_row_tiles(x_ref, MOE_BM).astype(BF16)
        z = jnp.dot(x, w1b[...], preferred_element_type=F32) + b1_ref[0]
        zg = jnp.minimum(z[:, 0:D_FF], SWIGLU_LIMIT)
        zl = jnp.clip(z[:, D_FF:2 * D_FF], -SWIGLU_LIMIT, SWIGLU_LIMIT)
        act = zg * _sigmoid(SWIGLU_ALPHA * zg) * (zl + 1.0)
        _store_row_tiles(o_ref, jnp.dot(act.astype(BF16), w2b[...], preferred_element_type=F32) + b2_ref[0])

    @pl.when(j >= n_used)
    def _():
        o_ref[...] = jnp.zeros_like(o_ref)


def _moe(blk_e, n_used, nxt_e, parity, xs, w1, b1, w2, b2, n_blocks):
    bm = MOE_BM
    d = D_MODEL
    grid_spec = pltpu.PrefetchScalarGridSpec(
        num_scalar_prefetch=4,
        grid=(n_blocks,),
        in_specs=[
            pl.BlockSpec((bm * ROW_SUB, LANES), lambda j, be, nu, nx, pa: (j, 0)),
            pl.BlockSpec(memory_space=pl.ANY),
            pl.BlockSpec((1, 1, 2 * D_FF), lambda j, be, nu, nx, pa: (be[j], 0, 0)),
            pl.BlockSpec(memory_space=pl.ANY),
            pl.BlockSpec((1, 1, d), lambda j, be, nu, nx, pa: (be[j], 0, 0)),
        ],
        out_specs=pl.BlockSpec((bm * ROW_SUB, LANES), lambda j, be, nu, nx, pa: (j, 0)),
        scratch_shapes=[
            pltpu.VMEM((2, d, 2 * D_FF), F32),
            pltpu.VMEM((2, D_FF, d), F32),
            pltpu.VMEM((d, 2 * D_FF), BF16),
            pltpu.VMEM((D_FF, d), BF16),
            pltpu.SemaphoreType.DMA((2, 2)),
        ],
    )
    return pl.pallas_call(
        _moe_body,
        grid_spec=grid_spec,
        out_shape=jax.ShapeDtypeStruct((n_blocks * bm * ROW_SUB, LANES), F32),
        compiler_params=_params(("arbitrary",)),
        name="moe_ffn",
    )(blk_e, n_used, nxt_e, parity, xs, w1, b1, w2, b2)


def _combine_body(dest_ref, x2_ref, route_ref, gfin_ref, yb_hbm, op_ref, os_ref, buf0, buf1, sem, *, np_steps):
    i = pl.program_id(0)
    n_steps = pl.num_programs(0)
    tm = COMBINE_TM
    n_tok = n_steps * (2 * tm)
    bufs = (buf0, buf1)

    def issue(tile, which):
        base = tile * tm
        for r in range(tm):
            for kk in range(TOP_K):
                slot = dest_ref[kk * n_tok + base + r]
                src = pl.ds(pl.multiple_of(slot * ROW_SUB, ROW_SUB), ROW_SUB)
                pltpu.make_async_copy(yb_hbm.at[src], bufs[which].at[kk, r * ROW_SUB:(r + 1) * ROW_SUB],
                                      sem.at[which]).start(priority=kk % 2)

    def wait(which):
        for kk in range(TOP_K):
            pltpu.make_async_copy(yb_hbm.at[pl.ds(0, tm * ROW_SUB)], bufs[which].at[kk], sem.at[which]).wait()

    def reduce(which, half):
        rows = slice(half * tm, (half + 1) * tm)
        acc = x2_ref[rows, :]
        for kk in range(TOP_K):
            gate = route_ref[rows, ROUTE_GATE + kk:ROUTE_GATE + kk + 1]
            acc = acc + gate * _load_row_tiles(bufs[which].at[kk], tm)
        return _rms(acc, gfin_ref[...])

    @pl.when(i == 0)
    def _():
        issue(0, 0)

    wait(0)
    issue(2 * i + 1, 1)
    y0 = reduce(0, 0)
    wait(1)
    issue(jnp.minimum(2 * i + 2, 2 * n_steps - 1), 0)
    y1 = reduce(1, 1)

    @pl.when(i < np_steps)
    def _():
        op_ref[0:tm, :] = y0
        op_ref[tm:2 * tm, :] = y1

    @pl.when(i >= np_steps)
    def _():
        os_ref[0:tm, :] = y0
        os_ref[tm:2 * tm, :] = y1

    @pl.when(i == n_steps - 1)
    def _():
        wait(0)


def _combine(dest, x2, route, gfin, yb, n_p):
    n, d = x2.shape
    tm = COMBINE_TM
    step = 2 * tm
    np_steps = n_p // step
    grid_spec = pltpu.PrefetchScalarGridSpec(
        num_scalar_prefetch=1,
        grid=(n // step,),
        in_specs=[
            pl.BlockSpec((step, d), lambda i, dest: (i, 0)),
            pl.BlockSpec((step, LANES), lambda i, dest: (i, 0)),
            pl.BlockSpec((1, d), lambda i, dest: (0, 0)),
            pl.BlockSpec(memory_space=pl.ANY),
        ],
        out_specs=[
            pl.BlockSpec((step, d), lambda i, dest: (jnp.minimum(i, np_steps - 1), 0)),
            pl.BlockSpec((step, d), lambda i, dest: (jnp.maximum(i - np_steps, 0), 0)),
        ],
        scratch_shapes=[
            pltpu.VMEM((TOP_K, tm * ROW_SUB, LANES), F32),
            pltpu.VMEM((TOP_K, tm * ROW_SUB, LANES), F32),
            pltpu.SemaphoreType.DMA((2,)),
        ],
    )
    return pl.pallas_call(
        functools.partial(_combine_body, np_steps=np_steps),
        grid_spec=grid_spec,
        out_shape=[jax.ShapeDtypeStruct((n_p, d), F32), jax.ShapeDtypeStruct((n - n_p, d), F32)],
        compiler_params=_params(("arbitrary",)),
        name="moe_combine",
    )(dest, x2, route, gfin, yb)


def _slot_tables(route_t, counts_row, n_blocks):
    bm = MOE_BM
    e = route_t[ROUTE_E:ROUTE_E + TOP_K].astype(jnp.int32)
    rank = route_t[ROUTE_RANK:ROUTE_RANK + TOP_K].astype(jnp.int32)
    counts = counts_row[:N_EXPERTS].astype(jnp.int32)
    padded = (counts + bm - 1) // bm * bm
    pad_end = jnp.cumsum(padded)
    pad_start = pad_end - padded
    experts = jnp.arange(N_EXPERTS, dtype=jnp.int32)[:, None, None]
    dest = (jnp.sum(jnp.where(e[None] == experts, pad_start[:, None, None], 0), axis=0) + rank).reshape(-1)
    blk_start = jnp.arange(n_blocks, dtype=jnp.int32) * bm
    blk_e = jnp.minimum(jnp.sum(pad_end[None, :] <= blk_start[:, None], axis=1), N_EXPERTS - 1).astype(jnp.int32)
    n_used = pad_end[-1] // bm
    last_blk = jnp.where(padded > 0, pad_end // bm - 1, -1)
    trailing = n_used + jnp.arange(N_EXPERTS, dtype=jnp.int32)
    trailing = jnp.where(trailing < n_blocks, trailing, -1)
    zlist = jnp.concatenate([last_blk, trailing]).astype(jnp.int32)
    ids = jnp.arange(N_EXPERTS, dtype=jnp.int32)
    present = counts > 0
    later = jnp.where(jnp.logical_and(ids[None, :] > ids[:, None], present[None, :]), ids[None, :], N_EXPERTS)
    next_present = jnp.min(later, axis=1)
    next_present = jnp.where(next_present < N_EXPERTS, next_present, -1)
    nxt_e = next_present.astype(jnp.int32)
    parity = ((jnp.cumsum(present.astype(jnp.int32)) - 1) % 2).astype(jnp.int32)
    return dest.astype(jnp.int32), blk_e, n_used.astype(jnp.int32).reshape(1), nxt_e, parity, zlist


def kernel(x_prompt, x_sample, mem_prompt, state_shift, state_wkv, cache_mem_k, cache_mem_v, g_norm_mix, w_in, mu_shift, w0, w_decay_up, a0, w_a_up, w_g_up, k_k, k_a, r_k, ln_x_g, ln_x_b, gmlp_ln_g, gmlp_ln_b, w_spatial, b_spatial, g_norm_mem, w_mem_kv, w_br_rwkv, w_br_gmlp, w_br_mem, w_out, g_norm_ffn, w_router, b_router, w_exp1, b_exp1, w_exp2, b_exp2, g_norm_final):
    bp, tp, d = x_prompt.shape
    bs, ts, _ = x_sample.shape
    n_p, n_s = bp * tp, bs * ts
    n_all = n_p + n_s
    l = 0
    row = lambda a: a.reshape(1, -1)

    x_pair = (x_prompt.reshape(n_p, d), x_sample.reshape(n_s, d))
    zr, zg, zq, sg = _in_proj(x_pair[0], x_pair[1], row(g_norm_mix[l]), w_in[l].astype(BF16))

    mk_p, mv_p = _mem_kv(mem_prompt.reshape(bp * N_MEM, d), row(g_norm_mem[l]), w_mem_kv[l].astype(BF16))

    rp = dict(mu=row(mu_shift[l]), w0=row(w0[l]), wd=w_decay_up[l].astype(BF16), a0=row(a0[l]),
              wa=w_a_up[l].astype(BF16), wg=w_g_up[l].astype(BF16), kk=row(k_k[l]), ka=row(k_a[l]),
              rk=row(r_k[l]), lng=row(ln_x_g[l]), lnb=row(ln_x_b[l]))
    head_of = jnp.arange(RWKV_WIDTH, dtype=jnp.int32) // HEAD_DIM
    rp["hsum"] = (head_of[:, None] == head_of[None, :]).astype(BF16)
    o_rw_p, s_p, shift_p = _rwkv(zr.reshape(n_all // RWKV_TT, RWKV_TT, SHIFT_WIDTH), 0, bp, tp // RWKV_TT,
                        jnp.zeros((bp, 1, SHIFT_WIDTH), F32), jnp.zeros((bp, RWKV_HEADS, HEAD_DIM, HEAD_DIM), F32),
                        rp, tt=RWKV_TT, c=RWKV_C, nseq=1)
    tile_s = RWKV_SAMPLE_SEQS * ts
    o_rw_s, s_s, shift_s = _rwkv(zr.reshape(n_all // tile_s, tile_s, SHIFT_WIDTH), n_p // tile_s, bs, 1,
                        state_shift[l].reshape(bs, 1, SHIFT_WIDTH), state_wkv[l], rp, tt=tile_s, c=ts,
                        nseq=RWKV_SAMPLE_SEQS)

    tri = jnp.tril(jnp.ones((CHUNK, CHUNK), bool))
    ws_p = jnp.where(tri, w_spatial[l], 0.0).astype(BF16)
    bs_p = b_spatial[l].T
    reps = CHUNK // ts
    tri_s = jnp.tril(jnp.ones((ts, ts), bool))
    ws_small = jnp.where(tri_s, w_spatial[l][:, :ts, :ts], 0.0)
    eye = jnp.eye(reps, dtype=F32)
    ws_s = jnp.einsum("ab,gij->gaibj", eye, ws_small).reshape(GMLP_GROUPS, CHUNK, CHUNK).astype(BF16)
    bs_s = jnp.tile(b_spatial[l][:, :ts], (1, reps)).T
    lng, lnb = row(gmlp_ln_g[l]), row(gmlp_ln_b[l])
    o_gm_p, _ = _gmlp(zg, 0, n_p, lng, lnb, ws_p, bs_p)
    o_gm_s, v_rows_s = _gmlp(zg, n_p // (GMLP_TILE * CHUNK), n_s, lng, lnb, ws_s, bs_s)

    o_me_p = _mem_attn(zq.reshape(n_all // ATTN_TT, ATTN_TT, MEM_WIDTH), 0, bp, tp // ATTN_TT,
                       mk_p.reshape(bp, N_MEM, MEM_WIDTH), mv_p.reshape(bp, N_MEM, MEM_WIDTH), tt=ATTN_TT, bb=1,
                       feature_major=False)
    mk_s = jnp.transpose(cache_mem_k[l].reshape(bs, N_MEM, MEM_WIDTH), (0, 2, 1))
    mv_s = jnp.transpose(cache_mem_v[l].reshape(bs, N_MEM, MEM_WIDTH), (0, 2, 1))
    o_me_s = _mem_attn(zq.reshape(n_all // ts, ts, MEM_WIDTH), n_p // ts, bs, 1, mk_s, mv_s, tt=ts, bb=ATTN_BB,
                       feature_major=True)

    wrt_pad = jnp.zeros((d, LANES), F32).at[:, :N_EXPERTS].set(w_router[l])
    brt_pad = jnp.zeros((1, LANES), F32).at[0, :N_EXPERTS].set(b_router[l])
    x2, h2, route, route_t, counts = _merge(
        x_pair, (o_rw_p.reshape(n_p, RWKV_WIDTH), o_rw_s.reshape(n_s, RWKV_WIDTH)), (o_gm_p, o_gm_s),
        (o_me_p.reshape(n_p, MEM_WIDTH), o_me_s.reshape(n_s, MEM_WIDTH)), sg,
        w_br_rwkv[l].astype(BF16), w_br_gmlp[l].astype(BF16), w_br_mem[l].astype(BF16), w_out[l].astype(BF16),
        row(g_norm_ffn[l]), wrt_pad, brt_pad)

    n_assign = n_all * TOP_K
    n_blocks = -(-(n_assign + N_EXPERTS * (MOE_BM - 1)) // MOE_BM)
    dest, blk_e, n_used, nxt_e, parity, zlist = _slot_tables(route_t, counts[0], n_blocks)
    xs = _dispatch(dest, zlist, h2.reshape(n_all, ROW_SUB, LANES), n_blocks)
    yb = _moe(blk_e, n_used, nxt_e, parity, xs.reshape(n_blocks * MOE_BM * ROW_SUB, LANES), w_exp1[l],
              b_exp1[l].reshape(N_EXPERTS, 1, 2 * D_FF), w_exp2[l], b_exp2[l].reshape(N_EXPERTS, 1, d), n_blocks)
    y_p, y_s = _combine(dest, x2, route, row(g_norm_final), yb, n_p)

    mk_out = mk_p.reshape(1, bp, N_MEM, MEM_HEADS, MEM_WIDTH // MEM_HEADS)
    mv_out = mv_p.reshape(1, bp, N_MEM, MEM_HEADS, MEM_WIDTH // MEM_HEADS)
    return (y_p.reshape(bp, tp, d), y_s.reshape(bs, ts, d), shift_p.reshape(1, bp, SHIFT_WIDTH), s_p[None], mk_out,
            mv_out, shift_s.reshape(1, bs, SHIFT_WIDTH), s_s[None],
            v_rows_s.reshape(1, bs, ts, GMLP_WIDTH))
```

```python
import functools
import math

import jax
import jax.numpy as jnp
from jax import lax
from jax.experimental import pallas as pl
from jax.experimental.pallas import tpu as pltpu

F32 = jnp.float32
BF16 = jnp.bfloat16

D_MODEL = 1024
RWKV_HEADS = 8
HEAD_DIM = 64
RWKV_WIDTH = RWKV_HEADS * HEAD_DIM
DECAY_LORA = 64
A_LORA = 64
GATE_LORA = 128
GMLP_GROUPS = 4
GMLP_WIDTH = 256
CHUNK = 128
MEM_HEADS = 4
MEM_WIDTH = 256
N_MEM = 256
N_EXPERTS = 32
TOP_K = 4
D_FF = 1024
SWIGLU_ALPHA = 1.702
SWIGLU_LIMIT = 7.0
RMS_EPS = 1e-5
LN_EPS = 1e-5
GN_EPS = 64e-5
SHIFT_WIDTH = 3 * RWKV_WIDTH + DECAY_LORA + A_LORA + GATE_LORA
OFF_GMLP = SHIFT_WIDTH
OFF_QMEM = OFF_GMLP + 2 * GMLP_WIDTH
OFF_GATE = OFF_QMEM + MEM_WIDTH
IN_WIDTH = OFF_GATE + 3 * D_MODEL
LANES = 128
ROW_SUB = D_MODEL // LANES

PROJ_TM = 512
MERGE_TM = 512
RWKV_TT = 512
RWKV_C = 64
RWKV_GROUP = 8
RWKV_SAMPLE_SEQS = 16
GMLP_TILE = 4
ATTN_TT = 1024
ATTN_BB = 8
MOE_BM = 512
DISPATCH_TM = 1024
COMBINE_TM = 128

ROUTE_E = 0
ROUTE_RANK = 4
ROUTE_GATE = 8

NN = ((1,), (0,))
NT = ((1,), (1,))
TN = ((0,), (0,))

VMEM_LIMIT = 56 * 1024 * 1024


def _params(sem, vmem=VMEM_LIMIT):
    return pltpu.CompilerParams(dimension_semantics=sem, vmem_limit_bytes=vmem)


def _bdot(a, b, dims=NN):
    return lax.dot_general(a.astype(BF16), b.astype(BF16), (dims, ((), ())), preferred_element_type=F32)


def _split(x):
    hi = x.astype(BF16)
    lo = (x - hi.astype(F32)).astype(BF16)
    return hi, lo


def _dot3(a, b, dims=NN):
    dn = (dims, ((), ()))
    ah, al = _split(a)
    bh, bl = _split(b)
    r = lax.dot_general(ah, bh, dn, preferred_element_type=F32)
    r = r + lax.dot_general(al, bh, dn, preferred_element_type=F32)
    return r + lax.dot_general(ah, bl, dn, preferred_element_type=F32)


def _rms(x, g):
    return x * lax.rsqrt(jnp.mean(x * x, axis=-1, keepdims=True) + RMS_EPS) * g


def _store_row_tiles(ref, x):
    m = x.shape[0]
    for j in range(ROW_SUB):
        ref[pl.ds(j, m, stride=ROW_SUB), :] = x[:, j * LANES:(j + 1) * LANES]


def _load_row_tiles(ref, m):
    return jnp.concatenate([ref[pl.ds(j, m, stride=ROW_SUB), :] for j in range(ROW_SUB)], axis=1)


def _sigmoid(x):
    return 0.5 * jnp.tanh(0.5 * x) + 0.5


def _in_proj_body(xp_ref, xs_ref, g_ref, w_ref, zr_ref, zg_ref, zq_ref, sg_ref, *, np_tiles):
    x = jnp.where(pl.program_id(0) < np_tiles, xp_ref[...], xs_ref[...])
    h = _rms(x, g_ref[...]).astype(BF16)
    zr_ref[...] = jnp.dot(h, w_ref[:, 0:OFF_GMLP], preferred_element_type=F32)
    zg_ref[...] = jnp.dot(h, w_ref[:, OFF_GMLP:OFF_QMEM], preferred_element_type=F32)
    zq_ref[...] = jnp.dot(h, w_ref[:, OFF_QMEM:OFF_GATE], preferred_element_type=F32).astype(zq_ref.dtype)
    gates = jnp.dot(h, w_ref[:, OFF_GATE:IN_WIDTH], preferred_element_type=F32)
    sg_ref[...] = _sigmoid(gates).astype(BF16)


def _in_proj(x_p, x_s, g, w_bf16):
    n_p = x_p.shape[0]
    n = n_p + x_s.shape[0]
    tm = PROJ_TM
    np_tiles = n_p // tm
    row = lambda i: (i, 0)
    const = lambda i: (0, 0)
    return pl.pallas_call(
        functools.partial(_in_proj_body, np_tiles=np_tiles),
        grid=(n // tm,),
        in_specs=[
            pl.BlockSpec((tm, D_MODEL), lambda i: (jnp.minimum(i, np_tiles - 1), 0)),
            pl.BlockSpec((tm, D_MODEL), lambda i: (jnp.maximum(i - np_tiles, 0), 0)),
            pl.BlockSpec((1, D_MODEL), const),
            pl.BlockSpec((D_MODEL, IN_WIDTH), const, pipeline_mode=pl.Buffered(1)),
        ],
        out_specs=[
            pl.BlockSpec((tm, SHIFT_WIDTH), row),
            pl.BlockSpec((tm, 2 * GMLP_WIDTH), row),
            pl.BlockSpec((tm, MEM_WIDTH), row),
            pl.BlockSpec((tm, 3 * D_MODEL), row),
        ],
        out_shape=[
            jax.ShapeDtypeStruct((n, SHIFT_WIDTH), F32),
            jax.ShapeDtypeStruct((n, 2 * GMLP_WIDTH), F32),
            jax.ShapeDtypeStruct((n, MEM_WIDTH), BF16),
            jax.ShapeDtypeStruct((n, 3 * D_MODEL), BF16),
        ],
        compiler_params=_params(("parallel",)),
        name="in_proj",
    )(x_p, x_s, g, w_bf16)


def _mem_kv_body(x_ref, g_ref, w_ref, k_ref, v_ref):
    h = _rms(x_ref[...], g_ref[...]).astype(BF16)
    k_ref[...] = jnp.dot(h, w_ref[:, 0:MEM_WIDTH], preferred_element_type=F32)
    v_ref[...] = jnp.dot(h, w_ref[:, MEM_WIDTH:2 * MEM_WIDTH], preferred_element_type=F32)


def _mem_kv(mem, g, w_bf16):
    n = mem.shape[0]
    tm = PROJ_TM
    return pl.pallas_call(
        _mem_kv_body,
        grid=(n // tm,),
        in_specs=[
            pl.BlockSpec((tm, D_MODEL), lambda i: (i, 0)),
            pl.BlockSpec((1, D_MODEL), lambda i: (0, 0)),
            pl.BlockSpec((D_MODEL, 2 * MEM_WIDTH), lambda i: (0, 0)),
        ],
        out_specs=[pl.BlockSpec((tm, MEM_WIDTH), lambda i: (i, 0))] * 2,
        out_shape=[jax.ShapeDtypeStruct((n, MEM_WIDTH), F32)] * 2,
        compiler_params=_params(("parallel",)),
        name="mem_kv",
    )(mem, g, w_bf16)


EXP_M05 = math.exp(-0.5)


def _unit_lower_inverse(lows, c):
    rows = lax.broadcasted_iota(jnp.int32, (c, c), 0)
    cols = lax.broadcasted_iota(jnp.int32, (c, c), 1)
    eye = (rows == cols).astype(F32)
    invs = [eye - low for low in lows]
    powers = lows
    for _ in range(int(math.log2(c)) - 1):
        powers = [_bdot(pw, pw) for pw in powers]
        invs = [inv + _bdot(inv, pw) for inv, pw in zip(invs, powers)]
    return invs


def _rwkv_body(zr_ref, shift_ref, s0_ref, mu_ref, w0_ref, wd_ref, a0_ref, wa_ref, wg_ref, kk_ref, ka_ref,
               rk_ref, lng_ref, lnb_ref, hsum_ref,
               o_ref, sout_ref, shout_ref,
               carry_ref, state_ref, r_s, k_s, v_s, kk_s, b_s, ld_s, y_s, bon_s, g_s,
               rw_s, y0_s, gm_s, h0_s, *, tt, c, nseq):
    i = pl.program_id(1)
    rw = RWKV_WIDTH
    nh = RWKV_HEADS

    @pl.when(i == 0)
    def _():
        for q in range(nseq):
            state_ref[q * nh:(q + 1) * nh] = s0_ref[q]

    z = zr_ref[0]
    z_prev = pltpu.roll(z, 1, 0)
    row = lax.broadcasted_iota(jnp.int32, z.shape, 0)
    if nseq == 1:
        @pl.when(i == 0)
        def _():
            carry_ref[...] = shift_ref[0]

        z_prev = jnp.where(row == 0, carry_ref[...], z_prev)
        carry_ref[...] = z[tt - 1:tt, :]
    else:
        first_rows = jnp.concatenate([jnp.broadcast_to(shift_ref[q], (c, SHIFT_WIDTH)) for q in range(nseq)], axis=0)
        z_prev = jnp.where(row % c == 0, first_rows, z_prev)
    zs = z + mu_ref[...] * (z_prev - z)
    r = zs[:, 0:rw]
    k = zs[:, rw:2 * rw]
    v = zs[:, 2 * rw:3 * rw]
    zw = zs[:, 3 * rw:3 * rw + DECAY_LORA]
    za = zs[:, 3 * rw + DECAY_LORA:3 * rw + DECAY_LORA + A_LORA]
    zg = zs[:, 3 * rw + DECAY_LORA + A_LORA:SHIFT_WIDTH]
    xw = w0_ref[...] + _bdot(jnp.tanh(zw), wd_ref[...])
    ld_s[...] = -EXP_M05 * _sigmoid(xw)
    a = _sigmoid(a0_ref[...] + _bdot(za, wa_ref[...]))
    g_s[...] = _bdot(_sigmoid(zg), wg_ref[...])
    kk = k * kk_ref[...]
    k = k * (1.0 + (a - 1.0) * ka_ref[...])
    r_s[...] = r
    k_s[...] = k
    v_s[...] = v

    def head_sum(t):
        return jnp.dot(t.astype(BF16), hsum_ref[...], preferred_element_type=F32)

    kk = kk / jnp.maximum(jnp.sqrt(head_sum(kk * kk)), 1e-12)
    kk_s[...] = kk
    b_s[...] = kk * a
    bon_s[...] = head_sum(r * k * rk_ref[...]) * v

    rows_c = lax.broadcasted_iota(jnp.int32, (c, c), 0)
    cols_c = lax.broadcasted_iota(jnp.int32, (c, c), 1)
    strict = rows_c > cols_c
    incl = rows_c >= cols_c
    tril_ones = jnp.where(incl, 1.0, 0.0).astype(BF16)
    rows_2c = lax.broadcasted_iota(jnp.int32, (c, 2 * c), 0)
    cols_2c = lax.broadcasted_iota(jnp.int32, (c, 2 * c), 1)
    incl2 = rows_2c >= jnp.where(cols_2c >= c, cols_2c - c, cols_2c)
    rows_k = lax.broadcasted_iota(jnp.int32, (HEAD_DIM, HEAD_DIM), 0)
    cols_k = lax.broadcasted_iota(jnp.int32, (HEAD_DIM, HEAD_DIM), 1)
    eye_k = (rows_k == cols_k).astype(F32)
    zeros_cv = jnp.zeros((c, HEAD_DIM), F32)
    heads = [slice(h * HEAD_DIM, (h + 1) * HEAD_DIM) for h in range(RWKV_HEADS)]

    n_chunks = tt // c
    group = nseq if nseq > 1 else math.gcd(n_chunks, RWKV_GROUP)

    def chunk_rows(ci):
        return pl.ds(ci * c, c) if isinstance(ci, int) else pl.ds(pl.multiple_of(ci * c, c), c)

    def scaled(ci):
        rows = chunk_rows(ci)
        ld = ld_s[rows, :]
        ld_hi, ld_lo = _split(ld)
        cum = (jnp.dot(tril_ones, ld_hi, preferred_element_type=F32)
               + jnp.dot(tril_ones, ld_lo, preferred_element_type=F32))
        e_inc = jnp.exp(cum)
        e_neg = jnp.exp(-cum)
        kt = k_s[rows, :] * e_neg
        bt = b_s[rows, :] * e_neg
        g_end = e_inc[c - 1:c, :]
        return dict(ci=ci, rt=r_s[rows, :] * e_inc, kkt=kk_s[rows, :] * jnp.exp(cum - ld), kt=kt, bt=bt,
                    g_end=g_end, bc=bt * g_end, kc=kt * g_end, vv=v_s[rows, :])

    def phase1(cj, carry):
        chains = [(ch, h, s) for ch in [scaled(cj * group + g) for g in range(group)]
                  for h, s in enumerate(heads)]
        x = [jnp.concatenate([ch["kkt"][:, s], ch["rt"][:, s]], axis=0) for ch, _, s in chains]
        zz = [jnp.concatenate([ch["bt"][:, s], ch["kt"][:, s]], axis=0) for ch, _, s in chains]
        amat = [_bdot(xh, zh, NT) for xh, zh in zip(x, zz)]
        l_b = [jnp.where(strict, am[0:c, 0:c], 0.0) for am in amat]
        l_k = [jnp.where(strict, am[0:c, c:2 * c], 0.0) for am in amat]
        a_r = [jnp.where(incl2, am[c:2 * c, :], 0.0) for am in amat]
        tinv = _unit_lower_inverse(l_b, c)
        lkv = [_bdot(lk, ch["vv"][:, s]) for lk, (ch, _, s) in zip(l_k, chains)]
        wu = [-_bdot(t, jnp.concatenate([ch["kkt"][:, s], lv], axis=1))
              for t, lv, (ch, _, s) in zip(tinv, lkv, chains)]
        m = [jnp.concatenate([w, jnp.concatenate([zeros_cv, ch["vv"][:, s]], axis=1)], axis=0)
             for w, (ch, _, s) in zip(wu, chains)]
        am2 = [_bdot(ar, mh) for ar, mh in zip(a_r, m)]
        gh = [_bdot(mh, jnp.concatenate([ch["bc"][:, s], ch["kc"][:, s]], axis=0), TN)
              for mh, (ch, _, s) in zip(m, chains)]
        for q, (ch, h, s) in enumerate(chains):
            idx = ch["ci"] * RWKV_HEADS + h
            rw_s[idx] = ch["rt"][:, s] + am2[q][:, 0:HEAD_DIM]
            y0_s[idx] = am2[q][:, HEAD_DIM:2 * HEAD_DIM]
            gm_s[idx] = gh[q][0:HEAD_DIM, :] + eye_k * ch["g_end"][:, s]
            h0_s[idx] = gh[q][HEAD_DIM:2 * HEAD_DIM, :]
        return carry

    if n_chunks == group:
        phase1(0, 0)
    else:
        lax.fori_loop(0, n_chunks // group, phase1, 0)

    if nseq == 1:
        def phase2(ci, carry):
            rows = chunk_rows(ci)
            for h, s in enumerate(heads):
                idx = ci * nh + h
                s_h = state_ref[h]
                y_s[rows, s] = _bdot(rw_s[idx], s_h, NT) + y0_s[idx]
                state_ref[h] = _bdot(s_h, gm_s[idx]) + h0_s[idx]
            return carry

        lax.fori_loop(0, n_chunks, phase2, 0)
    else:
        pairs = [(q, h, s) for q in range(nseq) for h, s in enumerate(heads)]
        states = [state_ref[q * nh + h] for q, h, _ in pairs]
        ys = [_bdot(rw_s[q * nh + h], st, NT) + y0_s[q * nh + h] for (q, h, _), st in zip(pairs, states)]
        new = [_bdot(st, gm_s[q * nh + h]) + h0_s[q * nh + h] for (q, h, _), st in zip(pairs, states)]
        for (q, h, s), yq, nq in zip(pairs, ys, new):
            y_s[chunk_rows(q), s] = yq
            state_ref[q * nh + h] = nq
    y = y_s[...]
    yc = y - head_sum(y) * (1.0 / HEAD_DIM)
    var = head_sum(yc * yc) * (1.0 / HEAD_DIM)
    yn = yc * lax.rsqrt(var + GN_EPS)
    o_ref[0] = ((yn * lng_ref[...] + lnb_ref[...] + bon_s[...]) * g_s[...]).astype(o_ref.dtype)

    @pl.when(i == pl.num_programs(1) - 1)
    def _():
        for q in range(nseq):
            sout_ref[q] = state_ref[q * nh:(q + 1) * nh]
            last = tt - 1 if nseq == 1 else (q + 1) * c - 1
            shout_ref[q] = zr_ref[0, last:last + 1, :]


def _rwkv(zr3, blk_off, b, nt, shift_prev, s_prev, p, tt, c, nseq):
    vec = lambda n: pl.BlockSpec((1, n), lambda bi, ti: (0, 0))
    mat = lambda m, n: pl.BlockSpec((m, n), lambda bi, ti: (0, 0))
    rw = RWKV_WIDTH
    scr = lambda: pltpu.VMEM((tt, rw), F32)
    nch = (tt // c) * RWKV_HEADS
    return pl.pallas_call(
        functools.partial(_rwkv_body, tt=tt, c=c, nseq=nseq),
        grid=(b // nseq, nt),
        in_specs=[
            pl.BlockSpec((1, tt, SHIFT_WIDTH), lambda bi, ti: (blk_off + bi * nt + ti, 0, 0)),
            pl.BlockSpec((nseq, 1, SHIFT_WIDTH), lambda bi, ti: (bi, 0, 0)),
            pl.BlockSpec((nseq, RWKV_HEADS, HEAD_DIM, HEAD_DIM), lambda bi, ti: (bi, 0, 0, 0)),
            vec(SHIFT_WIDTH), vec(rw), mat(DECAY_LORA, rw), vec(rw), mat(A_LORA, rw), mat(GATE_LORA, rw),
            vec(rw), vec(rw), vec(rw), vec(rw), vec(rw), mat(rw, rw),
        ],
        out_specs=[
            pl.BlockSpec((1, tt, rw), lambda bi, ti: (bi, ti, 0)),
            pl.BlockSpec((nseq, RWKV_HEADS, HEAD_DIM, HEAD_DIM), lambda bi, ti: (bi, 0, 0, 0)),
            pl.BlockSpec((nseq, 1, SHIFT_WIDTH), lambda bi, ti: (bi, 0, 0)),
        ],
        out_shape=[
            jax.ShapeDtypeStruct((b // nseq, nt * tt, rw), BF16),
            jax.ShapeDtypeStruct((b, RWKV_HEADS, HEAD_DIM, HEAD_DIM), F32),
            jax.ShapeDtypeStruct((b, 1, SHIFT_WIDTH), F32),
        ],
        scratch_shapes=[
            pltpu.VMEM((1, SHIFT_WIDTH), F32),
            pltpu.VMEM((nseq * RWKV_HEADS, HEAD_DIM, HEAD_DIM), F32),
            scr(), scr(), scr(), scr(), scr(), scr(), scr(), scr(), scr(),
            pltpu.VMEM((nch, c, HEAD_DIM), F32),
            pltpu.VMEM((nch, c, HEAD_DIM), F32),
            pltpu.VMEM((nch, HEAD_DIM, HEAD_DIM), F32),
            pltpu.VMEM((nch, HEAD_DIM, HEAD_DIM), F32),
        ],
        compiler_params=_params(("parallel", "arbitrary")),
        name="rwkv",
    )(zr3, shift_prev, s_prev, p["mu"], p["w0"], p["wd"], p["a0"], p["wa"], p["wg"], p["kk"], p["ka"],
      p["rk"], p["lng"], p["lnb"], p["hsum"])


def _gmlp_body(z_ref, lng_ref, lnb_ref, ws_ref, bs_ref, o_ref, v_ref):
    z = z_ref[...]
    ge = 0.5 * z * (1.0 + lax.erf(z * (1.0 / math.sqrt(2.0))))
    u = ge[:, 0:GMLP_WIDTH]
    v = ge[:, GMLP_WIDTH:2 * GMLP_WIDTH]
    mean = jnp.mean(v, axis=-1, keepdims=True)
    vc = v - mean
    var = jnp.mean(vc * vc, axis=-1, keepdims=True)
    vn = vc * lax.rsqrt(var + LN_EPS) * lng_ref[...] + lnb_ref[...]
    v_ref[...] = vn
    gd = GMLP_WIDTH // GMLP_GROUPS
    for q in range(GMLP_TILE):
        rows = slice(q * CHUNK, (q + 1) * CHUNK)
        for g in range(GMLP_GROUPS):
            gs = slice(g * gd, (g + 1) * gd)
            sv = jnp.dot(ws_ref[g], vn[rows, gs].astype(BF16), preferred_element_type=F32) + bs_ref[:, g:g + 1]
            o_ref[rows, gs] = (u[rows, gs] * sv).astype(o_ref.dtype)


def _gmlp(zg, tile_off, n_rows, lng, lnb, ws_bf16, bs):
    tile = GMLP_TILE * CHUNK
    return pl.pallas_call(
        _gmlp_body,
        grid=(n_rows // tile,),
        in_specs=[
            pl.BlockSpec((tile, 2 * GMLP_WIDTH), lambda i: (tile_off + i, 0)),
            pl.BlockSpec((1, GMLP_WIDTH), lambda i: (0, 0)),
            pl.BlockSpec((1, GMLP_WIDTH), lambda i: (0, 0)),
            pl.BlockSpec((GMLP_GROUPS, CHUNK, CHUNK), lambda i: (0, 0, 0)),
            pl.BlockSpec((CHUNK, GMLP_GROUPS), lambda i: (0, 0)),
        ],
        out_specs=[
            pl.BlockSpec((tile, GMLP_WIDTH), lambda i: (i, 0)),
            pl.BlockSpec((tile, GMLP_WIDTH), lambda i: (i, 0)),
        ],
        out_shape=[
            jax.ShapeDtypeStruct((n_rows, GMLP_WIDTH), BF16),
            jax.ShapeDtypeStruct((n_rows, GMLP_WIDTH), F32),
        ],
        compiler_params=_params(("parallel",)),
        name="gmlp",
    )(zg, lng, lnb, ws_bf16, bs)


def _mem_attn_body(q_ref, k_ref, v_ref, o_ref, *, bb, feature_major):
    hd = MEM_WIDTH // MEM_HEADS
    pairs = [(s, slice(h * hd, (h + 1) * hd)) for s in range(bb) for h in range(MEM_HEADS)]
    if feature_major:
        scores = [_bdot(q_ref[s, :, hs], k_ref[s, hs, :], NN) * (hd ** -0.5) for s, hs in pairs]
    else:
        scores = [_bdot(q_ref[s, :, hs], k_ref[s, :, hs], NT) * (hd ** -0.5) for s, hs in pairs]
    probs = []
    for sc in scores:
        pr = jnp.exp(sc - jnp.max(sc, axis=-1, keepdims=True))
        probs.append(pr / jnp.sum(pr, axis=-1, keepdims=True))
    for (s, hs), pr in zip(pairs, probs):
        if feature_major:
            o_ref[s, :, hs] = _bdot(pr, v_ref[s, hs, :], NT).astype(o_ref.dtype)
        else:
            o_ref[s, :, hs] = _bdot(pr, v_ref[s, :, hs]).astype(o_ref.dtype)


def _mem_attn(q3, blk_off, b, nt, mk, mv, tt, bb, feature_major):
    return pl.pallas_call(
        functools.partial(_mem_attn_body, bb=bb, feature_major=feature_major),
        grid=(b // bb, nt),
        in_specs=[
            pl.BlockSpec((bb, tt, MEM_WIDTH), lambda bi, ti: (blk_off // bb + bi * nt + ti, 0, 0)),
            pl.BlockSpec((bb, N_MEM, MEM_WIDTH), lambda bi, ti: (bi, 0, 0)),
            pl.BlockSpec((bb, N_MEM, MEM_WIDTH), lambda bi, ti: (bi, 0, 0)),
        ],
        out_specs=pl.BlockSpec((bb, tt, MEM_WIDTH), lambda bi, ti: (bi, ti, 0)),
        out_shape=jax.ShapeDtypeStruct((b, nt * tt, MEM_WIDTH), BF16),
        compiler_params=_params(("parallel", "parallel")),
        name="mem_attn",
    )(q3, mk, mv)


def _merge_body(x_p, x_s, orw_p, orw_s, ogm_p, ogm_s, ome_p, ome_s, sg_ref, wbr_ref, wbg_ref, wbm_ref, wout_ref,
                gffn_ref, wrt_ref, brt_ref, x2_ref, h2_ref, route_ref, route_t_ref, cnt_ref, count_s, *, np_tiles):
    i = pl.program_id(0)
    d = D_MODEL
    tm = x_p.shape[0]
    is_p = i < np_tiles
    x = jnp.where(is_p, x_p[...], x_s[...])
    orw = jnp.where(is_p, orw_p[...], orw_s[...])
    ogm = jnp.where(is_p, ogm_p[...], ogm_s[...])
    ome = jnp.where(is_p, ome_p[...], ome_s[...])
    merged = sg_ref[:, 0:d].astype(F32) * _bdot(orw, wbr_ref[...])
    merged = merged + sg_ref[:, d:2 * d].astype(F32) * _bdot(ogm, wbg_ref[...])
    merged = merged + sg_ref[:, 2 * d:3 * d].astype(F32) * _bdot(ome, wbm_ref[...])
    x2 = x + _bdot(merged, wout_ref[...])
    x2_ref[...] = x2
    h2 = _rms(x2, gffn_ref[...])
    _store_row_tiles(h2_ref, h2)

    @pl.when(i == 0)
    def _():
        count_s[...] = jnp.zeros_like(count_s)

    lane = lax.broadcasted_iota(jnp.int32, (tm, LANES), 1)
    logits = jnp.where(lane < N_EXPERTS, _dot3(h2, wrt_ref[...]) + brt_ref[...], -jnp.inf)
    lane_f = lane.astype(F32)
    tops, hots, idxs = [], [], []
    for _ in range(TOP_K):
        top = jnp.max(logits, axis=-1, keepdims=True)
        idx = jnp.min(jnp.where(logits == top, lane_f, float(LANES)), axis=-1, keepdims=True)
        hot = lane_f == idx
        logits = jnp.where(hot, -jnp.inf, logits)
        tops.append(top)
        hots.append(hot)
        idxs.append(idx)
    weights = [jnp.exp(t - tops[0]) for t in tops]
    denom = weights[0] + weights[1] + weights[2] + weights[3]
    onehot = jnp.zeros((tm, LANES), F32)
    for hot in hots:
        onehot = onehot + jnp.where(hot, 1.0, 0.0)
    rows_t = lax.broadcasted_iota(jnp.int32, (tm, tm), 0)
    cols_t = lax.broadcasted_iota(jnp.int32, (tm, tm), 1)
    ahead = jnp.where(rows_t > cols_t, 1.0, 0.0).astype(BF16)
    prefix = jnp.dot(ahead, onehot.astype(BF16), preferred_element_type=F32) + count_s[...]
    route = jnp.zeros((tm, LANES), F32)
    for kk in range(TOP_K):
        rank = jnp.sum(jnp.where(hots[kk], prefix, 0.0), axis=-1, keepdims=True)
        route = jnp.where(lane == ROUTE_E + kk, idxs[kk], route)
        route = jnp.where(lane == ROUTE_RANK + kk, rank, route)
        route = jnp.where(lane == ROUTE_GATE + kk, weights[kk] / denom, route)
    route_ref[...] = route
    route_t_ref[...] = route.T
    count_s[...] = count_s[...] + jnp.sum(onehot, axis=0, keepdims=True)
    cnt_ref[...] = jnp.broadcast_to(count_s[...], cnt_ref.shape)


def _merge(x, o_rw, o_gm, o_me, sg, wbr, wbg, wbm, wout, gffn, wrt_pad, brt_pad):
    n_p = x[0].shape[0]
    n = n_p + x[1].shape[0]
    tm = MERGE_TM
    d = D_MODEL
    np_tiles = n_p // tm
    row = lambda i: (i, 0)
    const = lambda i: (0, 0)
    first =lambda i: (jnp.minimum(i, np_tiles - 1), 0)
    second = lambda i: (jnp.maximum(i - np_tiles, 0), 0)

    def pair(width):
        return [pl.BlockSpec((tm, width), first), pl.BlockSpec((tm, width), second)]

    return pl.pallas_call(
        functools.partial(_merge_body, np_tiles=np_tiles),
        grid=(n // tm,),
        in_specs=pair(d) + pair(RWKV_WIDTH) + pair(GMLP_WIDTH) + pair(MEM_WIDTH) + [
            pl.BlockSpec((tm, 3 * d), row),
            pl.BlockSpec((RWKV_WIDTH, d), const),
            pl.BlockSpec((GMLP_WIDTH, d), const),
            pl.BlockSpec((MEM_WIDTH, d), const),
            pl.BlockSpec((d, d), const),
            pl.BlockSpec((1, d), const),
            pl.BlockSpec((d, LANES), const),
            pl.BlockSpec((1, LANES), const),
        ],
        out_specs=[
            pl.BlockSpec((tm, d), row),
            pl.BlockSpec((tm * ROW_SUB, LANES), row),
            pl.BlockSpec((tm, LANES), row),
            pl.BlockSpec((LANES, tm), lambda i: (0, i)),
            pl.BlockSpec((8, LANES), const),
        ],
        out_shape=[
            jax.ShapeDtypeStruct((n, d), F32),
            jax.ShapeDtypeStruct((n * ROW_SUB, LANES), F32),
            jax.ShapeDtypeStruct((n, LANES), F32),
            jax.ShapeDtypeStruct((LANES, n), F32),
            jax.ShapeDtypeStruct((8, LANES), F32),
        ],
        scratch_shapes=[pltpu.VMEM((1, LANES), F32)],
        compiler_params=_params(("arbitrary",)),
        name="merge",
    )(x[0], x[1], o_rw[0], o_rw[1], o_gm[0], o_gm[1], o_me[0], o_me[1], sg, wbr, wbg, wbm, wout, gffn, wrt_pad,
      brt_pad)


N_ZERO_BLOCKS = 2 * N_EXPERTS


def _dispatch_body(dest_ref, zlist_ref, h_hbm, xs_hbm, zero_s, hbuf, sem_z, sem_in, sem):
    i = pl.program_id(0)
    tm = DISPATCH_TM
    bm = MOE_BM

    @pl.when(i == 0)
    def _():
        zero_s[...] = jnp.zeros_like(zero_s)

        def zero_copy(q):
            start = pl.multiple_of(zlist_ref[q] * bm, bm)
            return pltpu.make_async_copy(zero_s, xs_hbm.at[pl.ds(start, bm)], sem_z)

        def start(q, carry):
            @pl.when(zlist_ref[q] >= 0)
            def _():
                zero_copy(q).start()
            return carry

        def wait(q, carry):
            @pl.when(zlist_ref[q] >= 0)
            def _():
                zero_copy(q).wait()
            return carry

        lax.fori_loop(0, N_ZERO_BLOCKS, start, 0)
        lax.fori_loop(0, N_ZERO_BLOCKS, wait, 0)

    base = i * tm
    n_tok = pl.num_programs(0) * tm

    n_steps = pl.num_programs(0)
    parity = lax.rem(i, 2)
    buf = lax.rem(i, 3)

    def fetch(tile, which):
        return pltpu.make_async_copy(h_hbm.at[pl.ds(pl.multiple_of(tile * tm, tm), tm)], hbuf.at[which],
                                     sem_in.at[which])

    @pl.when(i == 0)
    def _():
        fetch(0, 0).start()

    fetch(i, buf).wait()

    @pl.when(i + 1 < n_steps)
    def _():
        fetch(i + 1, lax.rem(i + 1, 3)).start()

    def body(r, carry):
        for kk in range(TOP_K):
            slot = dest_ref[kk * n_tok + base + r]
            pltpu.make_async_copy(hbuf.at[buf, r], xs_hbm.at[slot], sem.at[parity]).start(priority=kk % 2)
        return carry

    lax.fori_loop(0, tm, body, 0, unroll=8)

    def drain(which):
        for kk in range(TOP_K):
            pltpu.make_async_copy(hbuf.at[0], xs_hbm.at[pl.ds(0, tm)], sem.at[which]).wait()

    @pl.when(i > 0)
    def _():
        drain(1 - parity)

    @pl.when(i == n_steps - 1)
    def _():
        drain(parity)


def _dispatch(dest, zlist, h, n_blocks):
    n = h.shape[0]
    tm = DISPATCH_TM
    grid_spec = pltpu.PrefetchScalarGridSpec(
        num_scalar_prefetch=2,
        grid=(n // tm,),
        in_specs=[pl.BlockSpec(memory_space=pl.ANY)],
        out_specs=pl.BlockSpec(memory_space=pl.ANY),
        scratch_shapes=[
            pltpu.VMEM((MOE_BM, ROW_SUB, LANES), F32),
            pltpu.VMEM((3, tm, ROW_SUB, LANES), F32),
            pltpu.SemaphoreType.DMA(()),
            pltpu.SemaphoreType.DMA((3,)),
            pltpu.SemaphoreType.DMA((2,)),
        ],
    )
    return pl.pallas_call(
        _dispatch_body,
        grid_spec=grid_spec,
        out_shape=jax.ShapeDtypeStruct((n_blocks * MOE_BM, ROW_SUB, LANES), F32),
        compiler_params=_params(("arbitrary",)),
        name="moe_dispatch",
    )(dest, zlist, h)


CAST_ROWS = 64


def _moe_body(be_ref, nused_ref, nxt_ref, par_ref, x_ref, w1_hbm, b1_ref, w2_hbm, b2_ref, o_ref,
              w1f, w2f, w1b, w2b, sem):
    j = pl.program_id(0)
    n_used = nused_ref[0]

    def weight_copies(expert, slot):
        return (pltpu.make_async_copy(w1_hbm.at[expert], w1f.at[slot], sem.at[slot, 0]),
                pltpu.make_async_copy(w2_hbm.at[expert], w2f.at[slot], sem.at[slot, 1]))

    @pl.when(j == 0)
    def _():
        for cp in weight_copies(be_ref[0], 0):
            cp.start()

    @pl.when(j < n_used)
    def _():
        slot = par_ref[be_ref[j]]
        changed = jnp.logical_or(j == 0, be_ref[j] != be_ref[jnp.maximum(j - 1, 0)])

        @pl.when(changed)
        def _():
            for cp in weight_copies(be_ref[j], slot):
                cp.wait()

            nxt = nxt_ref[be_ref[j]]

            @pl.when(nxt >= 0)
            def _():
                for cp in weight_copies(nxt, 1 - slot):
                    cp.start()

            def cast1(q, carry):
                rows = pl.ds(pl.multiple_of(q * CAST_ROWS, CAST_ROWS), CAST_ROWS)
                w1b[rows, :] = w1f[slot, rows, :].astype(BF16)
                return carry

            def cast2(q, carry):
                rows = pl.ds(pl.multiple_of(q * CAST_ROWS, CAST_ROWS), CAST_ROWS)
                w2b[rows, :] = w2f[slot, rows, :].astype(BF16)
                return carry

            lax.fori_loop(0, D_MODEL // CAST_ROWS, cast1, 0)
            lax.fori_loop(0, D_FF // CAST_ROWS, cast2, 0)

        x = _load_row_tiles(x_ref, MOE_BM).astype(BF16)
        z = jnp.dot(x, w1b[...], preferred_element_type=F32) + b1_ref[0]
        zg = jnp.minimum(z[:, 0:D_FF], SWIGLU_LIMIT)
        zl = jnp.clip(z[:, D_FF:2 * D_FF], -SWIGLU_LIMIT, SWIGLU_LIMIT)
        act = zg * _sigmoid(SWIGLU_ALPHA * zg) * (zl + 1.0)
        _store_row_tiles(o_ref, jnp.dot(act.astype(BF16), w2b[...], preferred_element_type=F32) + b2_ref[0])

    @pl.when(j >= n_used)
    def _():
        o_ref[...] = jnp.zeros_like(o_ref)


def _moe(blk_e, n_used, nxt_e, parity, xs, w1, b1, w2, b2, n_blocks):
    bm = MOE_BM
    d = D_MODEL
    grid_spec = pltpu.PrefetchScalarGridSpec(
        num_scalar_prefetch=4,
        grid=(n_blocks,),
        in_specs=[
            pl.BlockSpec((bm * ROW_SUB, LANES), lambda j, be, nu, nx, pa: (j, 0)),
            pl.BlockSpec(memory_space=pl.ANY),
            pl.BlockSpec((1, 1, 2 * D_FF), lambda j, be, nu, nx, pa: (be[j], 0, 0)),
            pl.BlockSpec(memory_space=pl.ANY),
            pl.BlockSpec((1, 1, d), lambda j, be, nu, nx, pa: (be[j], 0, 0)),
        ],
        out_specs=pl.BlockSpec((bm * ROW_SUB, LANES), lambda j, be, nu, nx, pa: (j, 0)),
        scratch_shapes=[
            pltpu.VMEM((2, d, 2 * D_FF), F32),
            pltpu.VMEM((2, D_FF, d), F32),
            pltpu.VMEM((d, 2 * D_FF), BF16),
            pltpu.VMEM((D_FF, d), BF16),
            pltpu.SemaphoreType.DMA((2, 2)),
        ],
    )
    return pl.pallas_call(
        _moe_body,
        grid_spec=grid_spec,
        out_shape=jax.ShapeDtypeStruct((n_blocks * bm * ROW_SUB, LANES), F32),
        compiler_params=_params(("arbitrary",)),
        name="moe_ffn",
    )(blk_e, n_used, nxt_e, parity, xs, w1, b1, w2, b2)


def _combine_body(dest_ref, x2_ref, route_ref, gfin_ref, yb_hbm, op_ref, os_ref, buf0, buf1, sem, *, np_steps):
    i = pl.program_id(0)
    n_steps = pl.num_programs(0)
    tm = COMBINE_TM
    n_tok = n_steps * (2 * tm)
    bufs = (buf0, buf1)

    def issue(tile, which):
        base = tile * tm
        for r in range(tm):
            for kk in range(TOP_K):
                slot = dest_ref[kk * n_tok + base + r]
                src = pl.ds(pl.multiple_of(slot * ROW_SUB, ROW_SUB), ROW_SUB)
                pltpu.make_async_copy(yb_hbm.at[src], bufs[which].at[kk, r * ROW_SUB:(r + 1) * ROW_SUB],
                                      sem.at[which]).start(priority=kk % 2)

    def wait(which):
        for kk in range(TOP_K):
            pltpu.make_async_copy(yb_hbm.at[pl.ds(0, tm * ROW_SUB)], bufs[which].at[kk], sem.at[which]).wait()

    def reduce(which, half):
        rows = slice(half * tm, (half + 1) * tm)
        acc = x2_ref[rows, :]
        for kk in range(TOP_K):
            gate = route_ref[rows, ROUTE_GATE + kk:ROUTE_GATE + kk + 1]
            acc = acc + gate * _load_row_tiles(bufs[which].at[kk], tm)
        return _rms(acc, gfin_ref[...])

    @pl.when(i == 0)
    def _():
        issue(0, 0)

    wait(0)
    issue(2 * i + 1, 1)
    y0 = reduce(0, 0)
    wait(1)
    issue(jnp.minimum(2 * i + 2, 2 * n_steps - 1), 0)
    y1 = reduce(1, 1)

    @pl.when(i < np_steps)
    def _():
        op_ref[0:tm, :] = y0
        op_ref[tm:2 * tm, :] = y1

    @pl.when(i >= np_steps)
    def _():
        os_ref[0:tm, :] = y0
        os_ref[tm:2 * tm, :] = y1

    @pl.when(i == n_steps - 1)
    def _():
        wait(0)


def _combine(dest, x2, route, gfin, yb, n_p):
    n, d = x2.shape
    tm = COMBINE_TM
    step = 2 * tm
    np_steps = n_p // step
    grid_spec = pltpu.PrefetchScalarGridSpec(
        num_scalar_prefetch=1,
        grid=(n // step,),
        in_specs=[
            pl.BlockSpec((step, d), lambda i, dest: (i, 0)),
            pl.BlockSpec((step, LANES), lambda i, dest: (i, 0)),
            pl.BlockSpec((1, d), lambda i, dest: (0, 0)),
            pl.BlockSpec(memory_space=pl.ANY),
        ],
        out_specs=[
            pl.BlockSpec((step, d), lambda i, dest: (jnp.minimum(i, np_steps - 1), 0)),
            pl.BlockSpec((step, d), lambda i, dest: (jnp.maximum(i - np_steps, 0), 0)),
        ],
        scratch_shapes=[
            pltpu.VMEM((TOP_K, tm * ROW_SUB, LANES), F32),
            pltpu.VMEM((TOP_K, tm * ROW_SUB, LANES), F32),
            pltpu.SemaphoreType.DMA((2,)),
        ],
    )
    return pl.pallas_call(
        functools.partial(_combine_body, np_steps=np_steps),
        grid_spec=grid_spec,
        out_shape=[jax.ShapeDtypeStruct((n_p, d), F32), jax.ShapeDtypeStruct((n - n_p, d), F32)],
        compiler_params=_params(("arbitrary",)),
        name="moe_combine",
    )(dest, x2, route, gfin, yb)


def _slot_tables(route_t, counts_row, n_blocks):
    bm = MOE_BM
    e = route_t[ROUTE_E:ROUTE_E + TOP_K].astype(jnp.int32)
    rank = route_t[ROUTE_RANK:ROUTE_RANK + TOP_K].astype(jnp.int32)
    counts = counts_row[:N_EXPERTS].astype(jnp.int32)
    padded = (counts + bm - 1) // bm * bm
    pad_end = jnp.cumsum(padded)
    pad_start = pad_end - padded
    experts = jnp.arange(N_EXPERTS, dtype=jnp.int32)[:, None, None]
    dest = (jnp.sum(jnp.where(e[None] == experts, pad_start[:, None, None], 0), axis=0) + rank).reshape(-1)
    blk_start = jnp.arange(n_blocks, dtype=jnp.int32) * bm
    blk_e = jnp.minimum(jnp.sum(pad_end[None, :] <= blk_start[:, None], axis=1), N_EXPERTS - 1).astype(jnp.int32)
    n_used = pad_end[-1] // bm
    last_blk = jnp.where(padded > 0, pad_end // bm - 1, -1)
    trailing = n_used + jnp.arange(N_EXPERTS, dtype=jnp.int32)
    trailing = jnp.where(trailing < n_blocks, trailing, -1)
    zlist = jnp.concatenate([last_blk, trailing]).astype(jnp.int32)
    ids = jnp.arange(N_EXPERTS, dtype=jnp.int32)
    present = counts > 0
    later = jnp.where(jnp.logical_and(ids[None, :] > ids[:, None], present[None, :]), ids[None, :], N_EXPERTS)
    next_present = jnp.min(later, axis=1)
    next_present = jnp.where(next_present < N_EXPERTS, next_present, -1)
    nxt_e = next_present.astype(jnp.int32)
    parity = ((jnp.cumsum(present.astype(jnp.int32)) - 1) % 2).astype(jnp.int32)
    return dest.astype(jnp.int32), blk_e, n_used.astype(jnp.int32).reshape(1), nxt_e, parity, zlist


def kernel(x_prompt, x_sample, mem_prompt, state_shift, state_wkv, cache_mem_k, cache_mem_v, g_norm_mix, w_in, mu_shift, w0, w_decay_up, a0, w_a_up, w_g_up, k_k, k_a, r_k, ln_x_g, ln_x_b, gmlp_ln_g, gmlp_ln_b, w_spatial, b_spatial, g_norm_mem, w_mem_kv, w_br_rwkv, w_br_gmlp, w_br_mem, w_out, g_norm_ffn, w_router, b_router, w_exp1, b_exp1, w_exp2, b_exp2, g_norm_final):
    bp, tp, d = x_prompt.shape
    bs, ts, _ = x_sample.shape
    n_p, n_s = bp * tp, bs * ts
    n_all = n_p + n_s
    l = 0
    row = lambda a: a.reshape(1, -1)

    x_pair = (x_prompt.reshape(n_p, d), x_sample.reshape(n_s, d))
    zr, zg, zq, sg = _in_proj(x_pair[0], x_pair[1], row(g_norm_mix[l]), w_in[l].astype(BF16))

    mk_p, mv_p = _mem_kv(mem_prompt.reshape(bp * N_MEM, d), row(g_norm_mem[l]), w_mem_kv[l].astype(BF16))

    rp = dict(mu=row(mu_shift[l]), w0=row(w0[l]), wd=w_decay_up[l].astype(BF16), a0=row(a0[l]),
              wa=w_a_up[l].astype(BF16), wg=w_g_up[l].astype(BF16), kk=row(k_k[l]), ka=row(k_a[l]),
              rk=row(r_k[l]), lng=row(ln_x_g[l]), lnb=row(ln_x_b[l]))
    head_of = jnp.arange(RWKV_WIDTH, dtype=jnp.int32) // HEAD_DIM
    rp["hsum"] = (head_of[:, None] == head_of[None, :]).astype(BF16)
    o_rw_p, s_p, shift_p = _rwkv(zr.reshape(n_all // RWKV_TT, RWKV_TT, SHIFT_WIDTH), 0, bp, tp // RWKV_TT,
                        jnp.zeros((bp, 1, SHIFT_WIDTH), F32), jnp.zeros((bp, RWKV_HEADS, HEAD_DIM, HEAD_DIM), F32),
                        rp, tt=RWKV_TT, c=RWKV_C, nseq=1)
    tile_s = RWKV_SAMPLE_SEQS * ts
    o_rw_s, s_s, shift_s = _rwkv(zr.reshape(n_all // tile_s, tile_s, SHIFT_WIDTH), n_p // tile_s, bs, 1,
                        state_shift[l].reshape(bs, 1, SHIFT_WIDTH), state_wkv[l], rp, tt=tile_s, c=ts,
                        nseq=RWKV_SAMPLE_SEQS)

    tri = jnp.tril(jnp.ones((CHUNK, CHUNK), bool))
    ws_p = jnp.where(tri, w_spatial[l], 0.0).astype(BF16)
    bs_p = b_spatial[l].T
    reps = CHUNK // ts
    tri_s = jnp.tril(jnp.ones((ts, ts), bool))
    ws_small = jnp.where(tri_s, w_spatial[l][:, :ts, :ts], 0.0)
    eye = jnp.eye(reps, dtype=F32)
    ws_s = jnp.einsum("ab,gij->gaibj", eye, ws_small).reshape(GMLP_GROUPS, CHUNK, CHUNK).astype(BF16)
    bs_s = jnp.tile(b_spatial[l][:, :ts], (1, reps)).T
    lng, lnb = row(gmlp_ln_g[l]), row(gmlp_ln_b[l])
    o_gm_p, _ = _gmlp(zg, 0, n_p, lng, lnb, ws_p, bs_p)
    o_gm_s, v_rows_s = _gmlp(zg, n_p // (GMLP_TILE * CHUNK), n_s, lng, lnb, ws_s, bs_s)

    o_me_p = _mem_attn(zq.reshape(n_all // ATTN_TT, ATTN_TT, MEM_WIDTH), 0, bp, tp // ATTN_TT,
                       mk_p.reshape(bp, N_MEM, MEM_WIDTH), mv_p.reshape(bp, N_MEM, MEM_WIDTH), tt=ATTN_TT, bb=1,
                       feature_major=False)
    mk_s = jnp.transpose(cache_mem_k[l].reshape(bs, N_MEM, MEM_WIDTH), (0, 2, 1))
    mv_s = jnp.transpose(cache_mem_v[l].reshape(bs, N_MEM, MEM_WIDTH), (0, 2, 1))
    o_me_s = _mem_attn(zq.reshape(n_all // ts, ts, MEM_WIDTH), n_p // ts, bs, 1, mk_s, mv_s, tt=ts, bb=ATTN_BB,
                       feature_major=True)

    wrt_pad = jnp.zeros((d, LANES), F32).at[:, :N_EXPERTS].set(w_router[l])
    brt_pad = jnp.zeros((1, LANES), F32).at[0, :N_EXPERTS].set(b_router[l])
    x2, h2, route, route_t, counts = _merge(
        x_pair, (o_rw_p.reshape(n_p, RWKV_WIDTH), o_rw_s.reshape(n_s, RWKV_WIDTH)), (o_gm_p, o_gm_s),
        (o_me_p.reshape(n_p, MEM_WIDTH), o_me_s.reshape(n_s, MEM_WIDTH)), sg,
        w_br_rwkv[l].astype(BF16), w_br_gmlp[l].astype(BF16), w_br_mem[l].astype(BF16), w_out[l].astype(BF16),
        row(g_norm_ffn[l]), wrt_pad, brt_pad)

    n_assign = n_all * TOP_K
    n_blocks = -(-(n_assign + N_EXPERTS * (MOE_BM - 1)) // MOE_BM)
    dest, blk_e, n_used, nxt_e, parity, zlist = _slot_tables(route_t, counts[0], n_blocks)
    xs = _dispatch(dest, zlist, h2.reshape(n_all, ROW_SUB, LANES), n_blocks)
    yb = _moe(blk_e, n_used, nxt_e, parity, xs.reshape(n_blocks * MOE_BM * ROW_SUB, LANES), w_exp1[l],
              b_exp1[l].reshape(N_EXPERTS, 1, 2 * D_FF), w_exp2[l], b_exp2[l].reshape(N_EXPERTS, 1, d), n_blocks)
    y_p, y_s = _combine(dest, x2, route, row(g_norm_final), yb, n_p)

    mk_out = mk_p.reshape(1, bp, N_MEM, MEM_HEADS, MEM_WIDTH // MEM_HEADS)
    mv_out = mv_p.reshape(1, bp, N_MEM, MEM_HEADS, MEM_WIDTH // MEM_HEADS)
    return (y_p.reshape(bp, tp, d), y_s.reshape(bs, ts, d), shift_p.reshape(1, bp, SHIFT_WIDTH), s_p[None], mk_out,
            mv_out, shift_s.reshape(1, bs, SHIFT_WIDTH), s_s[None],
            v_rows_s.reshape(1, bs, ts, GMLP_WIDTH))
```

```python
import functools
import math

import jax
import jax.numpy as jnp
from jax import lax
from jax.experimental import pallas as pl
from jax.experimental.pallas import tpu as pltpu

F32 = jnp.float32
BF16 = jnp.bfloat16

D_MODEL = 1024
RWKV_HEADS = 8
HEAD_DIM = 64
RWKV_WIDTH = RWKV_HEADS * HEAD_DIM
DECAY_LORA = 64
A_LORA = 64
GATE_LORA = 128
GMLP_GROUPS = 4
GMLP_WIDTH = 256
CHUNK = 128
MEM_HEADS = 4
MEM_WIDTH = 256
N_MEM = 256
N_EXPERTS = 32
TOP_K = 4
D_FF = 1024
SWIGLU_ALPHA = 1.702
SWIGLU_LIMIT = 7.0
RMS_EPS = 1e-5
LN_EPS = 1e-5
GN_EPS = 64e-5
SHIFT_WIDTH = 3 * RWKV_WIDTH + DECAY_LORA + A_LORA + GATE_LORA
OFF_GMLP = SHIFT_WIDTH
OFF_QMEM = OFF_GMLP + 2 * GMLP_WIDTH
OFF_GATE = OFF_QMEM + MEM_WIDTH
IN_WIDTH = OFF_GATE + 3 * D_MODEL
LANES = 128
ROW_SUB = D_MODEL // LANES

PROJ_TM = 512
MERGE_TM = 512
RWKV_TT = 512
RWKV_C = 64
RWKV_GROUP = 8
RWKV_SAMPLE_SEQS = 16
GMLP_TILE = 4
ATTN_TT = 1024
ATTN_BB = 8
MOE_BM = 512
DISPATCH_TM = 1024
COMBINE_TM = 128

ROUTE_E = 0
ROUTE_RANK = 4
ROUTE_GATE = 8

NN = ((1,), (0,))
NT = ((1,), (1,))
TN = ((0,), (0,))

VMEM_LIMIT = 56 * 1024 * 1024


def _params(sem, vmem=VMEM_LIMIT):
    return pltpu.CompilerParams(dimension_semantics=sem, vmem_limit_bytes=vmem)


def _bdot(a, b, dims=NN):
    return lax.dot_general(a.astype(BF16), b.astype(BF16), (dims, ((), ())), preferred_element_type=F32)


def _split(x):
    hi = x.astype(BF16)
    lo = (x - hi.astype(F32)).astype(BF16)
    return hi, lo


def _dot3(a, b, dims=NN):
    dn = (dims, ((), ()))
    ah, al = _split(a)
    bh, bl = _split(b)
    r = lax.dot_general(ah, bh, dn, preferred_element_type=F32)
    r = r + lax.dot_general(al, bh, dn, preferred_element_type=F32)
    return r + lax.dot_general(ah, bl, dn, preferred_element_type=F32)


def _rms(x, g):
    return x * lax.rsqrt(jnp.mean(x * x, axis=-1, keepdims=True) + RMS_EPS) * g


def _store_row_tiles(ref, x):
    m = x.shape[0]
    for j in range(ROW_SUB):
        ref[pl.ds(j, m, stride=ROW_SUB), :] = x[:, j * LANES:(j + 1) * LANES]


def _load_row_tiles(ref, m):
    return jnp.concatenate([ref[pl.ds(j, m, stride=ROW_SUB), :] for j in range(ROW_SUB)], axis=1)


def _sigmoid(x):
    return 0.5 * jnp.tanh(0.5 * x) + 0.5


def _in_proj_body(xp_ref, xs_ref, g_ref, w_ref, zr_ref, zg_ref, zq_ref, sg_ref, *, np_tiles):
    x = jnp.where(pl.program_id(0) < np_tiles, xp_ref[...], xs_ref[...])
    h = _rms(x, g_ref[...]).astype(BF16)
    zr_ref[...] = jnp.dot(h, w_ref[:, 0:OFF_GMLP], preferred_element_type=F32)
    zg_ref[...] = jnp.dot(h, w_ref[:, OFF_GMLP:OFF_QMEM], preferred_element_type=F32)
    zq_ref[...] = jnp.dot(h, w_ref[:, OFF_QMEM:OFF_GATE], preferred_element_type=F32).astype(zq_ref.dtype)
    gates = jnp.dot(h, w_ref[:, OFF_GATE:IN_WIDTH], preferred_element_type=F32)
    sg_ref[...] = _sigmoid(gates).astype(BF16)


def _in_proj(x_p, x_s, g, w_bf16):
    n_p = x_p.shape[0]
    n = n_p + x_s.shape[0]
    tm = PROJ_TM
    np_tiles = n_p // tm
    row = lambda i: (i, 0)
    const = lambda i: (0, 0)
    return pl.pallas_call(
        functools.partial(_in_proj_body, np_tiles=np_tiles),
        grid=(n // tm,),
        in_specs=[
            pl.BlockSpec((tm, D_MODEL), lambda i: (jnp.minimum(i, np_tiles - 1), 0)),
            pl.BlockSpec((tm, D_MODEL), lambda i: (jnp.maximum(i - np_tiles, 0), 0)),
            pl.BlockSpec((1, D_MODEL), const),
            pl.BlockSpec((D_MODEL, IN_WIDTH), const, pipeline_mode=pl.Buffered(1)),
        ],
        out_specs=[
            pl.BlockSpec((tm, SHIFT_WIDTH), row),
            pl.BlockSpec((tm, 2 * GMLP_WIDTH), row),
            pl.BlockSpec((tm, MEM_WIDTH), row),
            pl.BlockSpec((tm, 3 * D_MODEL), row),
        ],
        out_shape=[
            jax.ShapeDtypeStruct((n, SHIFT_WIDTH), F32),
            jax.ShapeDtypeStruct((n, 2 * GMLP_WIDTH), F32),
            jax.ShapeDtypeStruct((n, MEM_WIDTH), BF16),
            jax.ShapeDtypeStruct((n, 3 * D_MODEL), BF16),
        ],
        compiler_params=_params(("parallel",)),
        name="in_proj",
    )(x_p, x_s, g, w_bf16)


def _mem_kv_body(x_ref, g_ref, w_ref, k_ref, v_ref):
    h = _rms(x_ref[...], g_ref[...]).astype(BF16)
    k_ref[...] = jnp.dot(h, w_ref[:, 0:MEM_WIDTH], preferred_element_type=F32)
    v_ref[...] = jnp.dot(h, w_ref[:, MEM_WIDTH:2 * MEM_WIDTH], preferred_element_type=F32)


def _mem_kv(mem, g, w_bf16):
    n = mem.shape[0]
    tm = PROJ_TM
    return pl.pallas_call(
        _mem_kv_body,
        grid=(n // tm,),
        in_specs=[
            pl.BlockSpec((tm, D_MODEL), lambda i: (i, 0)),
            pl.BlockSpec((1, D_MODEL), lambda i: (0, 0)),
            pl.BlockSpec((D_MODEL, 2 * MEM_WIDTH), lambda i: (0, 0)),
        ],
        out_specs=[pl.BlockSpec((tm, MEM_WIDTH), lambda i: (i, 0))] * 2,
        out_shape=[jax.ShapeDtypeStruct((n, MEM_WIDTH), F32)] * 2,
        compiler_params=_params(("parallel",)),
        name="mem_kv",
    )(mem, g, w_bf16)


EXP_M05 = math.exp(-0.5)


def _unit_lower_inverse(lows, c):
    rows = lax.broadcasted_iota(jnp.int32, (c, c), 0)
    cols = lax.broadcasted_iota(jnp.int32, (c, c), 1)
    eye = (rows == cols).astype(F32)
    invs = [eye - low for low in lows]
    powers = lows
    for _ in range(int(math.log2(c)) - 1):
        powers = [_bdot(pw, pw) for pw in powers]
        invs = [inv + _bdot(inv, pw) for inv, pw in zip(invs, powers)]
    return invs


def _rwkv_body(zr_ref, shift_ref, s0_ref, mu_ref, w0_ref, wd_ref, a0_ref, wa_ref, wg_ref, kk_ref, ka_ref,
               rk_ref, lng_ref, lnb_ref, hsum_ref,
               o_ref, sout_ref, shout_ref,
               carry_ref, state_ref, r_s, k_s, v_s, kk_s, b_s, ld_s, y_s, bon_s, g_s,
               rw_s, y0_s, gm_s, h0_s, *, tt, c, nseq):
    i = pl.program_id(1)
    rw = RWKV_WIDTH
    nh = RWKV_HEADS

    @pl.when(i == 0)
    def _():
        for q in range(nseq):
            state_ref[q * nh:(q + 1) * nh] = s0_ref[q]

    z = zr_ref[0]
    z_prev = pltpu.roll(z, 1, 0)
    row = lax.broadcasted_iota(jnp.int32, z.shape, 0)
    if nseq == 1:
        @pl.when(i == 0)
        def _():
            carry_ref[...] = shift_ref[0]

        z_prev = jnp.where(row == 0, carry_ref[...], z_prev)
        carry_ref[...] = z[tt - 1:tt, :]
    else:
        first_rows = jnp.concatenate([jnp.broadcast_to(shift_ref[q], (c, SHIFT_WIDTH)) for q in range(nseq)], axis=0)
        z_prev = jnp.where(row % c == 0, first_rows, z_prev)
    zs = z + mu_ref[...] * (z_prev - z)
    r = zs[:, 0:rw]
    k = zs[:, rw:2 * rw]
    v = zs[:, 2 * rw:3 * rw]
    zw = zs[:, 3 * rw:3 * rw + DECAY_LORA]
    za = zs[:, 3 * rw + DECAY_LORA:3 * rw + DECAY_LORA + A_LORA]
    zg = zs[:, 3 * rw + DECAY_LORA + A_LORA:SHIFT_WIDTH]
    xw = w0_ref[...] + _bdot(jnp.tanh(zw), wd_ref[...])
    ld_s[...] = -EXP_M05 * _sigmoid(xw)
    a = _sigmoid(a0_ref[...] + _bdot(za, wa_ref[...]))
    g_s[...] = _bdot(_sigmoid(zg), wg_ref[...])
    kk = k * kk_ref[...]
    k = k * (1.0 + (a - 1.0) * ka_ref[...])
    r_s[...] = r
    k_s[...] = k
    v_s[...] = v

    def head_sum(t):
        return jnp.dot(t.astype(BF16), hsum_ref[...], preferred_element_type=F32)

    kk = kk / jnp.maximum(jnp.sqrt(head_sum(kk * kk)), 1e-12)
    kk_s[...] = kk
    b_s[...] = kk * a
    bon_s[...] = head_sum(r * k * rk_ref[...]) * v

    rows_c = lax.broadcasted_iota(jnp.int32, (c, c), 0)
    cols_c = lax.broadcasted_iota(jnp.int32, (c, c), 1)
    strict = rows_c > cols_c
    incl = rows_c >= cols_c
    tril_ones = jnp.where(incl, 1.0, 0.0).astype(BF16)
    rows_2c = lax.broadcasted_iota(jnp.int32, (c, 2 * c), 0)
    cols_2c = lax.broadcasted_iota(jnp.int32, (c, 2 * c), 1)
    incl2 = rows_2c >= jnp.where(cols_2c >= c, cols_2c - c, cols_2c)
    rows_k = lax.broadcasted_iota(jnp.int32, (HEAD_DIM, HEAD_DIM), 0)
    cols_k = lax.broadcasted_iota(jnp.int32, (HEAD_DIM, HEAD_DIM), 1)
    eye_k = (rows_k == cols_k).astype(F32)
    zeros_cv = jnp.zeros((c, HEAD_DIM), F32)
    heads = [slice(h * HEAD_DIM, (h + 1) * HEAD_DIM) for h in range(RWKV_HEADS)]

    n_chunks = tt // c
    group = nseq if nseq > 1 else math.gcd(n_chunks, RWKV_GROUP)

    def chunk_rows(ci):
        return pl.ds(ci * c, c) if isinstance(ci, int) else pl.ds(pl.multiple_of(ci * c, c), c)

    def scaled(ci):
        rows = chunk_rows(ci)
        ld = ld_s[rows, :]
        ld_hi, ld_lo = _split(ld)
        cum = (jnp.dot(tril_ones, ld_hi, preferred_element_type=F32)
               + jnp.dot(tril_ones, ld_lo, preferred_element_type=F32))
        e_inc = jnp.exp(cum)
        e_neg = jnp.exp(-cum)
        kt = k_s[rows, :] * e_neg
        bt = b_s[rows, :] * e_neg
        g_end = e_inc[c - 1:c, :]
        return dict(ci=ci, rt=r_s[rows, :] * e_inc, kkt=kk_s[rows, :] * jnp.exp(cum - ld), kt=kt, bt=bt,
                    g_end=g_end, bc=bt * g_end, kc=kt * g_end, vv=v_s[rows, :])

    def phase1(cj, carry):
        chains = [(ch, h, s) for ch in [scaled(cj * group + g) for g in range(group)]
                  for h, s in enumerate(heads)]
        x = [jnp.concatenate([ch["kkt"][:, s], ch["rt"][:, s]], axis=0) for ch, _, s in chains]
        zz = [jnp.concatenate([ch["bt"][:, s], ch["kt"][:, s]], axis=0) for ch, _, s in chains]
        amat = [_bdot(xh, zh, NT) for xh, zh in zip(x, zz)]
        l_b = [jnp.where(strict, am[0:c, 0:c], 0.0) for am in amat]
        l_k = [jnp.where(strict, am[0:c, c:2 * c], 0.0) for am in amat]
        a_r = [jnp.where(incl2, am[c:2 * c, :], 0.0) for am in amat]
        tinv = _unit_lower_inverse(l_b, c)
        lkv = [_bdot(lk, ch["vv"][:, s]) for lk, (ch, _, s) in zip(l_k, chains)]
        wu = [-_bdot(t, jnp.concatenate([ch["kkt"][:, s], lv], axis=1))
              for t, lv, (ch, _, s) in zip(tinv, lkv, chains)]
        m = [jnp.concatenate([w, jnp.concatenate([zeros_cv, ch["vv"][:, s]], axis=1)], axis=0)
             for w, (ch, _, s) in zip(wu, chains)]
        am2 = [_bdot(ar, mh) for ar, mh in zip(a_r, m)]
        gh = [_bdot(mh, jnp.concatenate([ch["bc"][:, s], ch["kc"][:, s]], axis=0), TN)
              for mh, (ch, _, s) in zip(m, chains)]
        for q, (ch, h, s) in enumerate(chains):
            idx = ch["ci"] * RWKV_HEADS + h
            rw_s[idx] = ch["rt"][:, s] + am2[q][:, 0:HEAD_DIM]
            y0_s[idx] = am2[q][:, HEAD_DIM:2 * HEAD_DIM]
            gm_s[idx] = gh[q][0:HEAD_DIM, :] + eye_k * ch["g_end"][:, s]
            h0_s[idx] = gh[q][HEAD_DIM:2 * HEAD_DIM, :]
        return carry

    if n_chunks == group:
        phase1(0, 0)
    else:
        lax.fori_loop(0, n_chunks // group, phase1, 0)

    if nseq == 1:
        def phase2(ci, carry):
            rows = chunk_rows(ci)
            for h, s in enumerate(heads):
                idx = ci * nh + h
                s_h = state_ref[h]
                y_s[rows, s] = _bdot(rw_s[idx], s_h, NT) + y0_s[idx]
                state_ref[h] = _bdot(s_h, gm_s[idx]) + h0_s[idx]
            return carry

        lax.fori_loop(0, n_chunks, phase2, 0)
    else:
        pairs = [(q, h, s) for q in range(nseq) for h, s in enumerate(heads)]
        states = [state_ref[q * nh + h] for q, h, _ in pairs]
        ys = [_bdot(rw_s[q * nh + h], st, NT) + y0_s[q * nh + h] for (q, h, _), st in zip(pairs, states)]
        new = [_bdot(st, gm_s[q * nh + h]) + h0_s[q * nh + h] for (q, h, _), st in zip(pairs, states)]
        for (q, h, s), yq, nq in zip(pairs, ys, new):
            y_s[chunk_rows(q), s] = yq
            state_ref[q * nh + h] = nq
    y = y_s[...]
    yc = y - head_sum(y) * (1.0 / HEAD_DIM)
    var = head_sum(yc * yc) * (1.0 / HEAD_DIM)
    yn = yc * lax.rsqrt(var + GN_EPS)
    o_ref[0] = ((yn * lng_ref[...] + lnb_ref[...] + bon_s[...]) * g_s[...]).astype(o_ref.dtype)

    @pl.when(i == pl.num_programs(1) - 1)
    def _():
        for q in range(nseq):
            sout_ref[q] = state_ref[q * nh:(q + 1) * nh]
            last = tt - 1 if nseq == 1 else (q + 1) * c - 1
            shout_ref[q] = zr_ref[0, last:last + 1, :]


def _rwkv(zr3, blk_off, b, nt, shift_prev, s_prev, p, tt, c, nseq):
    vec = lambda n: pl.BlockSpec((1, n), lambda bi, ti: (0, 0))
    mat = lambda m, n: pl.BlockSpec((m, n), lambda bi, ti: (0, 0))
    rw = RWKV_WIDTH
    scr = lambda: pltpu.VMEM((tt, rw), F32)
    nch = (tt // c) * RWKV_HEADS
    return pl.pallas_call(
        functools.partial(_rwkv_body, tt=tt, c=c, nseq=nseq),
        grid=(b // nseq, nt),
        in_specs=[
            pl.BlockSpec((1, tt, SHIFT_WIDTH), lambda bi, ti: (blk_off + bi * nt + ti, 0, 0)),
            pl.BlockSpec((nseq, 1, SHIFT_WIDTH), lambda bi, ti: (bi, 0, 0)),
            pl.BlockSpec((nseq, RWKV_HEADS, HEAD_DIM, HEAD_DIM), lambda bi, ti: (bi, 0, 0, 0)),
            vec(SHIFT_WIDTH), vec(rw), mat(DECAY_LORA, rw), vec(rw), mat(A_LORA, rw), mat(GATE_LORA, rw),
            vec(rw), vec(rw), vec(rw), vec(rw), vec(rw), mat(rw, rw),
        ],
        out_specs=[
            pl.BlockSpec((1, tt, rw), lambda bi, ti: (bi, ti, 0)),
            pl.BlockSpec((nseq, RWKV_HEADS, HEAD_DIM, HEAD_DIM), lambda bi, ti: (bi, 0, 0, 0)),
            pl.BlockSpec((nseq, 1, SHIFT_WIDTH), lambda bi, ti: (bi, 0, 0)),
        ],
        out_shape=[
            jax.ShapeDtypeStruct((b // nseq, nt * tt, rw), BF16),
            jax.ShapeDtypeStruct((b, RWKV_HEADS, HEAD_DIM, HEAD_DIM), F32),
            jax.ShapeDtypeStruct((b, 1, SHIFT_WIDTH), F32),
        ],
        scratch_shapes=[
            pltpu.VMEM((1, SHIFT_WIDTH), F32),
            pltpu.VMEM((nseq * RWKV_HEADS, HEAD_DIM, HEAD_DIM), F32),
            scr(), scr(), scr(), scr(), scr(), scr(), scr(), scr(), scr(),
            pltpu.VMEM((nch, c, HEAD_DIM), F32),
            pltpu.VMEM((nch, c, HEAD_DIM), F32),
            pltpu.VMEM((nch, HEAD_DIM, HEAD_DIM), F32),
            pltpu.VMEM((nch, HEAD_DIM, HEAD_DIM), F32),
        ],
        compiler_params=_params(("parallel", "arbitrary")),
        name="rwkv",
    )(zr3, shift_prev, s_prev, p["mu"], p["w0"], p["wd"], p["a0"], p["wa"], p["wg"], p["kk"], p["ka"],
      p["rk"], p["lng"], p["lnb"], p["hsum"])


def _gmlp_body(z_ref, lng_ref, lnb_ref, ws_ref, bs_ref, o_ref, v_ref):
    z = z_ref[...]
    ge = 0.5 * z * (1.0 + lax.erf(z * (1.0 / math.sqrt(2.0))))
    u = ge[:, 0:GMLP_WIDTH]
    v = ge[:, GMLP_WIDTH:2 * GMLP_WIDTH]
    mean = jnp.mean(v, axis=-1, keepdims=True)
    vc = v - mean
    var = jnp.mean(vc * vc, axis=-1, keepdims=True)
    vn = vc * lax.rsqrt(var + LN_EPS) * lng_ref[...] + lnb_ref[...]
    v_ref[...] = vn
    gd = GMLP_WIDTH // GMLP_GROUPS
    for q in range(GMLP_TILE):
        rows = slice(q * CHUNK, (q + 1) * CHUNK)
        for g in range(GMLP_GROUPS):
            gs = slice(g * gd, (g + 1) * gd)
            sv = jnp.dot(ws_ref[g], vn[rows, gs].astype(BF16), preferred_element_type=F32) + bs_ref[:, g:g + 1]
            o_ref[rows, gs] = (u[rows, gs] * sv).astype(o_ref.dtype)


def _gmlp(zg, tile_off, n_rows, lng, lnb, ws_bf16, bs):
    tile = GMLP_TILE * CHUNK
    return pl.pallas_call(
        _gmlp_body,
        grid=(n_rows // tile,),
        in_specs=[
            pl.BlockSpec((tile, 2 * GMLP_WIDTH), lambda i: (tile_off + i, 0)),
            pl.BlockSpec((1, GMLP_WIDTH), lambda i: (0, 0)),
            pl.BlockSpec((1, GMLP_WIDTH), lambda i: (0, 0)),
            pl.BlockSpec((GMLP_GROUPS, CHUNK, CHUNK), lambda i: (0, 0, 0)),
            pl.BlockSpec((CHUNK, GMLP_GROUPS), lambda i: (0, 0)),
        ],
        out_specs=[
            pl.BlockSpec((tile, GMLP_WIDTH), lambda i: (i, 0)),
            pl.BlockSpec((tile, GMLP_WIDTH), lambda i: (i, 0)),
        ],
        out_shape=[
            jax.ShapeDtypeStruct((n_rows, GMLP_WIDTH), BF16),
            jax.ShapeDtypeStruct((n_rows, GMLP_WIDTH), F32),
        ],
        compiler_params=_params(("parallel",)),
        name="gmlp",
    )(zg, lng, lnb, ws_bf16, bs)


def _mem_attn_body(q_ref, k_ref, v_ref, o_ref, *, bb, feature_major):
    hd = MEM_WIDTH // MEM_HEADS
    pairs = [(s, slice(h * hd, (h + 1) * hd)) for s in range(bb) for h in range(MEM_HEADS)]
    if feature_major:
        scores = [_bdot(q_ref[s, :, hs], k_ref[s, hs, :], NN) * (hd ** -0.5) for s, hs in pairs]
    else:
        scores = [_bdot(q_ref[s, :, hs], k_ref[s, :, hs], NT) * (hd ** -0.5) for s, hs in pairs]
    probs = []
    for sc in scores:
        pr = jnp.exp(sc - jnp.max(sc, axis=-1, keepdims=True))
        probs.append(pr / jnp.sum(pr, axis=-1, keepdims=True))
    for (s, hs), pr in zip(pairs, probs):
        if feature_major:
            o_ref[s, :, hs] = _bdot(pr, v_ref[s, hs, :], NT).astype(o_ref.dtype)
        else:
            o_ref[s, :, hs] = _bdot(pr, v_ref[s, :, hs]).astype(o_ref.dtype)


def _mem_attn(q3, blk_off, b, nt, mk, mv, tt, bb, feature_major):
    return pl.pallas_call(
        functools.partial(_mem_attn_body, bb=bb, feature_major=feature_major),
        grid=(b // bb, nt),
        in_specs=[
            pl.BlockSpec((bb, tt, MEM_WIDTH), lambda bi, ti: (blk_off // bb + bi * nt + ti, 0, 0)),
            pl.BlockSpec((bb, N_MEM, MEM_WIDTH), lambda bi, ti: (bi, 0, 0)),
            pl.BlockSpec((bb, N_MEM, MEM_WIDTH), lambda bi, ti: (bi, 0, 0)),
        ],
        out_specs=pl.BlockSpec((bb, tt, MEM_WIDTH), lambda bi, ti: (bi, ti, 0)),
        out_shape=jax.ShapeDtypeStruct((b, nt * tt, MEM_WIDTH), BF16),
        compiler_params=_params(("parallel", "parallel")),
        name="mem_attn",
    )(q3, mk, mv)


def _merge_body(x_p, x_s, orw_p, orw_s, ogm_p, ogm_s, ome_p, ome_s, sg_ref, wbr_ref, wbg_ref, wbm_ref, wout_ref,
                gffn_ref, wrt_ref, brt_ref, x2_ref, h2_ref, route_ref, route_t_ref, cnt_ref, count_s, *, np_tiles):
    i = pl.program_id(0)
    d = D_MODEL
    tm = x_p.shape[0]
    is_p = i < np_tiles
    x = jnp.where(is_p, x_p[...], x_s[...])
    orw = jnp.where(is_p, orw_p[...], orw_s[...])
    ogm = jnp.where(is_p, ogm_p[...], ogm_s[...])
    ome = jnp.where(is_p, ome_p[...], ome_s[...])
    merged = sg_ref[:, 0:d].astype(F32) * _bdot(orw, wbr_ref[...])
    merged = merged + sg_ref[:, d:2 * d].astype(F32) * _bdot(ogm, wbg_ref[...])
    merged = merged + sg_ref[:, 2 * d:3 * d].astype(F32) * _bdot(ome, wbm_ref[...])
    x2 = x + _bdot(merged, wout_ref[...])
    x2_ref[...] = x2
    h2 = _rms(x2, gffn_ref[...])
    _store_row_tiles(h2_ref, h2)

    @pl.when(i == 0)
    def _():
        count_s[...] = jnp.zeros_like(count_s)

    lane = lax.broadcasted_iota(jnp.int32, (tm, LANES), 1)
    logits = jnp.where(lane < N_EXPERTS, _dot3(h2, wrt_ref[...]) + brt_ref[...], -jnp.inf)
    lane_f = lane.astype(F32)
    tops, hots, idxs = [], [], []
    for _ in range(TOP_K):
        top = jnp.max(logits, axis=-1, keepdims=True)
        idx = jnp.min(jnp.where(logits == top, lane_f, float(LANES)), axis=-1, keepdims=True)
        hot = lane_f == idx
        logits = jnp.where(hot, -jnp.inf, logits)
        tops.append(top)
        hots.append(hot)
        idxs.append(idx)
    weights = [jnp.exp(t - tops[0]) for t in tops]
    denom = weights[0] + weights[1] + weights[2] + weights[3]
    onehot = jnp.zeros((tm, LANES), F32)
    for hot in hots:
        onehot = onehot + jnp.where(hot, 1.0, 0.0)
    rows_t = lax.broadcasted_iota(jnp.int32, (tm, tm), 0)
    cols_t = lax.broadcasted_iota(jnp.int32, (tm, tm), 1)
    ahead = jnp.where(rows_t > cols_t, 1.0, 0.0).astype(BF16)
    prefix = jnp.dot(ahead, onehot.astype(BF16), preferred_element_type=F32) + count_s[...]
    route = jnp.zeros((tm, LANES), F32)
    for kk in range(TOP_K):
        rank = jnp.sum(jnp.where(hots[kk], prefix, 0.0), axis=-1, keepdims=True)
        route = jnp.where(lane == ROUTE_E + kk, idxs[kk], route)
        route = jnp.where(lane == ROUTE_RANK + kk, rank, route)
        route = jnp.where(lane == ROUTE_GATE + kk, weights[kk] / denom, route)
    route_ref[...] = route
    route_t_ref[...] = route.T
    count_s[...] = count_s[...] + jnp.sum(onehot, axis=0, keepdims=True)
    cnt_ref[...] = jnp.broadcast_to(count_s[...], cnt_ref.shape)


def _merge(x, o_rw, o_gm, o_me, sg, wbr, wbg, wbm, wout, gffn, wrt_pad, brt_pad):
    n_p = x[0].shape[0]
    n = n_p + x[1].shape[0]
    tm = MERGE_TM
    d = D_MODEL
    np_tiles = n_p // tm
    row = lambda i: (i, 0)
    const = lambda i: (0, 0)
    first =lambda i: (jnp.minimum(i, np_tiles - 1), 0)
    second = lambda i: (jnp.maximum(i - np_tiles, 0), 0)

    def pair(width):
        return [pl.BlockSpec((tm, width), first), pl.BlockSpec((tm, width), second)]

    return pl.pallas_call(
        functools.partial(_merge_body, np_tiles=np_tiles),
        grid=(n // tm,),
        in_specs=pair(d) + pair(RWKV_WIDTH) + pair(GMLP_WIDTH) + pair(MEM_WIDTH) + [
            pl.BlockSpec((tm, 3 * d), row),
            pl.BlockSpec((RWKV_WIDTH, d), const),
            pl.BlockSpec((GMLP_WIDTH, d), const),
            pl.BlockSpec((MEM_WIDTH, d), const),
            pl.BlockSpec((d, d), const),
            pl.BlockSpec((1, d), const),
            pl.BlockSpec((d, LANES), const),
            pl.BlockSpec((1, LANES), const),
        ],
        out_specs=[
            pl.BlockSpec((tm, d), row),
            pl.BlockSpec((tm * ROW_SUB, LANES), row),
            pl.BlockSpec((tm, LANES), row),
            pl.BlockSpec((LANES, tm), lambda i: (0, i)),
            pl.BlockSpec((8, LANES), const),
        ],
        out_shape=[
            jax.ShapeDtypeStruct((n, d), F32),
            jax.ShapeDtypeStruct((n * ROW_SUB, LANES), F32),
            jax.ShapeDtypeStruct((n, LANES), F32),
            jax.ShapeDtypeStruct((LANES, n), F32),
            jax.ShapeDtypeStruct((8, LANES), F32),
        ],
        scratch_shapes=[pltpu.VMEM((1, LANES), F32)],
        compiler_params=_params(("arbitrary",)),
        name="merge",
    )(x[0], x[1], o_rw[0], o_rw[1], o_gm[0], o_gm[1], o_me[0], o_me[1], sg, wbr, wbg, wbm, wout, gffn, wrt_pad,
      brt_pad)


N_ZERO_BLOCKS = 2 * N_EXPERTS


def _dispatch_body(dest_ref, zlist_ref, h_hbm, xs_hbm, zero_s, hbuf, sem_z, sem_in, sem):
    i = pl.program_id(0)
    tm = DISPATCH_TM
    bm = MOE_BM

    @pl.when(i == 0)
    def _():
        zero_s[...] = jnp.zeros_like(zero_s)

        def zero_copy(q):
            start = pl.multiple_of(zlist_ref[q] * bm, bm)
            return pltpu.make_async_copy(zero_s, xs_hbm.at[pl.ds(start, bm)], sem_z)

        def start(q, carry):
            @pl.when(zlist_ref[q] >= 0)
            def _():
                zero_copy(q).start()
            return carry

        def wait(q, carry):
            @pl.when(zlist_ref[q] >= 0)
            def _():
                zero_copy(q).wait()
            return carry

        lax.fori_loop(0, N_ZERO_BLOCKS, start, 0)
        lax.fori_loop(0, N_ZERO_BLOCKS, wait, 0)

    base = i * tm
    n_tok = pl.num_programs(0) * tm

    n_steps = pl.num_programs(0)
    parity = lax.rem(i, 2)
    buf = lax.rem(i, 3)

    def fetch(tile, which):
        return pltpu.make_async_copy(h_hbm.at[pl.ds(pl.multiple_of(tile * tm, tm), tm)], hbuf.at[which],
                                     sem_in.at[which])

    @pl.when(i == 0)
    def _():
        fetch(0, 0).start()

    fetch(i, buf).wait()

    @pl.when(i + 1 < n_steps)
    def _():
        fetch(i + 1, lax.rem(i + 1, 3)).start()

    def body(r, carry):
        for kk in range(TOP_K):
            slot = dest_ref[kk * n_tok + base + r]
            pltpu.make_async_copy(hbuf.at[buf, r], xs_hbm.at[slot], sem.at[parity]).start(priority=kk % 2)
        return carry

    lax.fori_loop(0, tm, body, 0, unroll=8)

    def drain(which):
        for kk in range(TOP_K):
            pltpu.make_async_copy(hbuf.at[0], xs_hbm.at[pl.ds(0, tm)], sem.at[which]).wait()

    @pl.when(i > 0)
    def _():
        drain(1 - parity)

    @pl.when(i == n_steps - 1)
    def _():
        drain(parity)


def _dispatch(dest, zlist, h, n_blocks):
    n = h.shape[0]
    tm = DISPATCH_TM
    grid_spec = pltpu.PrefetchScalarGridSpec(
        num_scalar_prefetch=2,
        grid=(n // tm,),
        in_specs=[pl.BlockSpec(memory_space=pl.ANY)],
        out_specs=pl.BlockSpec(memory_space=pl.ANY),
        scratch_shapes=[
            pltpu.VMEM((MOE_BM, ROW_SUB, LANES), F32),
            pltpu.VMEM((3, tm, ROW_SUB, LANES), F32),
            pltpu.SemaphoreType.DMA(()),
            pltpu.SemaphoreType.DMA((3,)),
            pltpu.SemaphoreType.DMA((2,)),
        ],
    )
    return pl.pallas_call(
        _dispatch_body,
        grid_spec=grid_spec,
        out_shape=jax.ShapeDtypeStruct((n_blocks * MOE_BM, ROW_SUB, LANES), F32),
        compiler_params=_params(("arbitrary",)),
        name="moe_dispatch",
    )(dest, zlist, h)


CAST_ROWS = 64


def _moe_body(be_ref, nused_ref, nxt_ref, par_ref, x_ref, w1_hbm, b1_ref, w2_hbm, b2_ref, o_ref,
              w1f, w2f, w1b, w2b, sem):
    j = pl.program_id(0)
    n_used = nused_ref[0]

    def weight_copies(expert, slot):
        return (pltpu.make_async_copy(w1_hbm.at[expert], w1f.at[slot], sem.at[slot, 0]),
                pltpu.make_async_copy(w2_hbm.at[expert], w2f.at[slot], sem.at[slot, 1]))

    @pl.when(j == 0)
    def _():
        for cp in weight_copies(be_ref[0], 0):
            cp.start()

    @pl.when(j < n_used)
    def _():
        slot = par_ref[be_ref[j]]
        changed = jnp.logical_or(j == 0, be_ref[j] != be_ref[jnp.maximum(j - 1, 0)])

        @pl.when(changed)
        def _():
            for cp in weight_copies(be_ref[j], slot):
                cp.wait()

            nxt = nxt_ref[be_ref[j]]

            @pl.when(nxt >= 0)
            def _():
                for cp in weight_copies(nxt, 1 - slot):
                    cp.start()

            def cast1(q, carry):
                rows = pl.ds(pl.multiple_of(q * CAST_ROWS, CAST_ROWS), CAST_ROWS)
                w1b[rows, :] = w1f[slot, rows, :].astype(BF16)
                return carry

            def cast2(q, carry):
                rows = pl.ds(pl.multiple_of(q * CAST_ROWS, CAST_ROWS), CAST_ROWS)
                w2b[rows, :] = w2f[slot, rows, :].astype(BF16)
                return carry

            lax.fori_loop(0, D_MODEL // CAST_ROWS, cast1, 0)
            lax.fori_loop(0, D_FF // CAST_ROWS, cast2, 0)

        x = _load_row_tiles(x_ref, MOE_BM).astype(BF16)
        z = jnp.dot(x, w1b[...], preferred_element_type=F32) + b1_ref[0]
        zg = jnp.minimum(z[:, 0:D_FF], SWIGLU_LIMIT)
        zl = jnp.clip(z[:, D_FF:2 * D_FF], -SWIGLU_LIMIT, SWIGLU_LIMIT)
        act = zg * _sigmoid(SWIGLU_ALPHA * zg) * (zl + 1.0)
        _store_row_tiles(o_ref, jnp.dot(act.astype(BF16), w2b[...], preferred_element_type=F32) + b2_ref[0])

    @pl.when(j >= n_used)
    def _():
        o_ref[...] = jnp.zeros_like(o_ref)


def _moe(blk_e, n_used, nxt_e, parity, xs, w1, b1, w2, b2, n_blocks):
    bm = MOE_BM
    d = D_MODEL
    grid_spec = pltpu.PrefetchScalarGridSpec(
        num_scalar_prefetch=4,
        grid=(n_blocks,),
        in_specs=[
            pl.BlockSpec((bm * ROW_SUB, LANES), lambda j, be, nu, nx, pa: (j, 0)),
            pl.BlockSpec(memory_space=pl.ANY),
            pl.BlockSpec((1, 1, 2 * D_FF), lambda j, be, nu, nx, pa: (be[j], 0, 0)),
            pl.BlockSpec(memory_space=pl.ANY),
            pl.BlockSpec((1, 1, d), lambda j, be, nu, nx, pa: (be[j], 0, 0)),
        ],
        out_specs=pl.BlockSpec((bm * ROW_SUB, LANES), lambda j, be, nu, nx, pa: (j, 0)),
        scratch_shapes=[
            pltpu.VMEM((2, d, 2 * D_FF), F32),
            pltpu.VMEM((2, D_FF, d), F32),
            pltpu.VMEM((d, 2 * D_FF), BF16),
            pltpu.VMEM((D_FF, d), BF16),
            pltpu.SemaphoreType.DMA((2, 2)),
        ],
    )
    return pl.pallas_call(
        _moe_body,
        grid_spec=grid_spec,
        out_shape=jax.ShapeDtypeStruct((n_blocks * bm * ROW_SUB, LANES), F32),
        compiler_params=_params(("arbitrary",)),
        name="moe_ffn",
    )(blk_e, n_used, nxt_e, parity, xs, w1, b1, w2, b2)


def _combine_body(dest_ref, x2_ref, route_ref, gfin_ref, yb_hbm, op_ref, os_ref, buf0, buf1, sem, *, np_steps):
    i = pl.program_id(0)
    n_steps = pl.num_programs(0)
    tm = COMBINE_TM
    n_tok = n_steps * (2 * tm)
    bufs = (buf0, buf1)

    def issue(tile, which):
        base = tile * tm
        for r in range(tm):
            for kk in range(TOP_K):
                slot = dest_ref[kk * n_tok + base + r]
                src = pl.ds(pl.multiple_of(slot * ROW_SUB, ROW_SUB), ROW_SUB)
                pltpu.make_async_copy(yb_hbm.at[src], bufs[which].at[kk, r * ROW_SUB:(r + 1) * ROW_SUB],
                                      sem.at[which]).start(priority=kk % 2)

    def wait(which):
        for kk in range(TOP_K):
            pltpu.make_async_copy(yb_hbm.at[pl.ds(0, tm * ROW_SUB)], bufs[which].at[kk], sem.at[which]).wait()

    def reduce(which, half):
        rows = slice(half * tm, (half + 1) * tm)
        acc = x2_ref[rows, :]
        for kk in range(TOP_K):
            gate = route_ref[rows, ROUTE_GATE + kk:ROUTE_GATE + kk + 1]
            acc = acc + gate * _load_row_tiles(bufs[which].at[kk], tm)
        return _rms(acc, gfin_ref[...])

    @pl.when(i == 0)
    def _():
        issue(0, 0)

    wait(0)
    y0 = reduce(0, 0)
    issue(2 * i + 1, 1)
    wait(1)
    issue(jnp.minimum(2 * i + 2, 2 * n_steps - 1), 0)
    y1 = reduce(1, 1)

    @pl.when(i < np_steps)
    def _():
        op_ref[0:tm, :] = y0
        op_ref[tm:2 * tm, :] = y1

    @pl.when(i >= np_steps)
    def _():
        os_ref[0:tm, :] = y0
        os_ref[tm:2 * tm, :] = y1

    @pl.when(i == n_steps - 1)
    def _():
        wait(0)


def _combine(dest, x2, route, gfin, yb, n_p):
    n, d = x2.shape
    tm = COMBINE_TM
    step = 2 * tm
    np_steps = n_p // step
    grid_spec = pltpu.PrefetchScalarGridSpec(
        num_scalar_prefetch=1,
        grid=(n // step,),
        in_specs=[
            pl.BlockSpec((step, d), lambda i, dest: (i, 0)),
            pl.BlockSpec((step, LANES), lambda i, dest: (i, 0)),
            pl.BlockSpec((1, d), lambda i, dest: (0, 0)),
            pl.BlockSpec(memory_space=pl.ANY),
        ],
        out_specs=[
            pl.BlockSpec((step, d), lambda i, dest: (jnp.minimum(i, np_steps - 1), 0)),
            pl.BlockSpec((step, d), lambda i, dest: (jnp.maximum(i - np_steps, 0), 0)),
        ],
        scratch_shapes=[
            pltpu.VMEM((TOP_K, tm * ROW_SUB, LANES), F32),
            pltpu.VMEM((TOP_K, tm * ROW_SUB, LANES), F32),
            pltpu.SemaphoreType.DMA((2,)),
        ],
    )
    return pl.pallas_call(
        functools.partial(_combine_body, np_steps=np_steps),
        grid_spec=grid_spec,
        out_shape=[jax.ShapeDtypeStruct((n_p, d), F32), jax.ShapeDtypeStruct((n - n_p, d), F32)],
        compiler_params=_params(("arbitrary",)),
        name="moe_combine",
    )(dest, x2, route, gfin, yb)


def _slot_tables(route_t, counts_row, n_blocks):
    bm = MOE_BM
    e = route_t[ROUTE_E:ROUTE_E + TOP_K].astype(jnp.int32)
    rank = route_t[ROUTE_RANK:ROUTE_RANK + TOP_K].astype(jnp.int32)
    counts = counts_row[:N_EXPERTS].astype(jnp.int32)
    padded = (counts + bm - 1) // bm * bm
    pad_end = jnp.cumsum(padded)
    pad_start = pad_end - padded
    experts = jnp.arange(N_EXPERTS, dtype=jnp.int32)[:, None, None]
    dest = (jnp.sum(jnp.where(e[None] == experts, pad_start[:, None, None], 0), axis=0) + rank).reshape(-1)
    blk_start = jnp.arange(n_blocks, dtype=jnp.int32) * bm
    blk_e = jnp.minimum(jnp.sum(pad_end[None, :] <= blk_start[:, None], axis=1), N_EXPERTS - 1).astype(jnp.int32)
    n_used = pad_end[-1] // bm
    last_blk = jnp.where(padded > 0, pad_end // bm - 1, -1)
    trailing = n_used + jnp.arange(N_EXPERTS, dtype=jnp.int32)
    trailing = jnp.where(trailing < n_blocks, trailing, -1)
    zlist = jnp.concatenate([last_blk, trailing]).astype(jnp.int32)
    ids = jnp.arange(N_EXPERTS, dtype=jnp.int32)
    present = counts > 0
    later = jnp.where(jnp.logical_and(ids[None, :] > ids[:, None], present[None, :]), ids[None, :], N_EXPERTS)
    next_present = jnp.min(later, axis=1)
    next_present = jnp.where(next_present < N_EXPERTS, next_present, -1)
    nxt_e = next_present.astype(jnp.int32)
    parity = ((jnp.cumsum(present.astype(jnp.int32)) - 1) % 2).astype(jnp.int32)
    return dest.astype(jnp.int32), blk_e, n_used.astype(jnp.int32).reshape(1), nxt_e, parity, zlist


def kernel(x_prompt, x_sample, mem_prompt, state_shift, state_wkv, cache_mem_k, cache_mem_v, g_norm_mix, w_in, mu_shift, w0, w_decay_up, a0, w_a_up, w_g_up, k_k, k_a, r_k, ln_x_g, ln_x_b, gmlp_ln_g, gmlp_ln_b, w_spatial, b_spatial, g_norm_mem, w_mem_kv, w_br_rwkv, w_br_gmlp, w_br_mem, w_out, g_norm_ffn, w_router, b_router, w_exp1, b_exp1, w_exp2, b_exp2, g_norm_final):
    bp, tp, d = x_prompt.shape
    bs, ts, _ = x_sample.shape
    n_p, n_s = bp * tp, bs * ts
    n_all = n_p + n_s
    l = 0
    row = lambda a: a.reshape(1, -1)

    x_pair = (x_prompt.reshape(n_p, d), x_sample.reshape(n_s, d))
    zr, zg, zq, sg = _in_proj(x_pair[0], x_pair[1], row(g_norm_mix[l]), w_in[l].astype(BF16))

    mk_p, mv_p = _mem_kv(mem_prompt.reshape(bp * N_MEM, d), row(g_norm_mem[l]), w_mem_kv[l].astype(BF16))

    rp = dict(mu=row(mu_shift[l]), w0=row(w0[l]), wd=w_decay_up[l].astype(BF16), a0=row(a0[l]),
              wa=w_a_up[l].astype(BF16), wg=w_g_up[l].astype(BF16), kk=row(k_k[l]), ka=row(k_a[l]),
              rk=row(r_k[l]), lng=row(ln_x_g[l]), lnb=row(ln_x_b[l]))
    head_of = jnp.arange(RWKV_WIDTH, dtype=jnp.int32) // HEAD_DIM
    rp["hsum"] = (head_of[:, None] == head_of[None, :]).astype(BF16)
    o_rw_p, s_p, shift_p = _rwkv(zr.reshape(n_all // RWKV_TT, RWKV_TT, SHIFT_WIDTH), 0, bp, tp // RWKV_TT,
                        jnp.zeros((bp, 1, SHIFT_WIDTH), F32), jnp.zeros((bp, RWKV_HEADS, HEAD_DIM, HEAD_DIM), F32),
                        rp, tt=RWKV_TT, c=RWKV_C, nseq=1)
    tile_s = RWKV_SAMPLE_SEQS * ts
    o_rw_s, s_s, shift_s = _rwkv(zr.reshape(n_all // tile_s, tile_s, SHIFT_WIDTH), n_p // tile_s, bs, 1,
                        state_shift[l].reshape(bs, 1, SHIFT_WIDTH), state_wkv[l], rp, tt=tile_s, c=ts,
                        nseq=RWKV_SAMPLE_SEQS)

    tri = jnp.tril(jnp.ones((CHUNK, CHUNK), bool))
    ws_p = jnp.where(tri, w_spatial[l], 0.0).astype(BF16)
    bs_p = b_spatial[l].T
    reps = CHUNK // ts
    tri_s = jnp.tril(jnp.ones((ts, ts), bool))
    ws_small = jnp.where(tri_s, w_spatial[l][:, :ts, :ts], 0.0)
    eye = jnp.eye(reps, dtype=F32)
    ws_s = jnp.einsum("ab,gij->gaibj", eye, ws_small).reshape(GMLP_GROUPS, CHUNK, CHUNK).astype(BF16)
    bs_s = jnp.tile(b_spatial[l][:, :ts], (1, reps)).T
    lng, lnb = row(gmlp_ln_g[l]), row(gmlp_ln_b[l])
    o_gm_p, _ = _gmlp(zg, 0, n_p, lng, lnb, ws_p, bs_p)
    o_gm_s, v_rows_s = _gmlp(zg, n_p // (GMLP_TILE * CHUNK), n_s, lng, lnb, ws_s, bs_s)

    o_me_p = _mem_attn(zq.reshape(n_all // ATTN_TT, ATTN_TT, MEM_WIDTH), 0, bp, tp // ATTN_TT,
                       mk_p.reshape(bp, N_MEM, MEM_WIDTH), mv_p.reshape(bp, N_MEM, MEM_WIDTH), tt=ATTN_TT, bb=1,
                       feature_major=False)
    mk_s = jnp.transpose(cache_mem_k[l].reshape(bs, N_MEM, MEM_WIDTH), (0, 2, 1))
    mv_s = jnp.transpose(cache_mem_v[l].reshape(bs, N_MEM, MEM_WIDTH), (0, 2, 1))
    o_me_s = _mem_attn(zq.reshape(n_all // ts, ts, MEM_WIDTH), n_p // ts, bs, 1, mk_s, mv_s, tt=ts, bb=ATTN_BB,
                       feature_major=True)

    wrt_pad = jnp.zeros((d, LANES), F32).at[:, :N_EXPERTS].set(w_router[l])
    brt_pad = jnp.zeros((1, LANES), F32).at[0, :N_EXPERTS].set(b_router[l])
    x2, h2, route, route_t, counts = _merge(
        x_pair, (o_rw_p.reshape(n_p, RWKV_WIDTH), o_rw_s.reshape(n_s, RWKV_WIDTH)), (o_gm_p, o_gm_s),
        (o_me_p.reshape(n_p, MEM_WIDTH), o_me_s.reshape(n_s, MEM_WIDTH)), sg,
        w_br_rwkv[l].astype(BF16), w_br_gmlp[l].astype(BF16), w_br_mem[l].astype(BF16), w_out[l].astype(BF16),
        row(g_norm_ffn[l]), wrt_pad, brt_pad)

    n_assign = n_all * TOP_K
    n_blocks = -(-(n_assign + N_EXPERTS * (MOE_BM - 1)) // MOE_BM)
    dest, blk_e, n_used, nxt_e, parity, zlist = _slot_tables(route_t, counts[0], n_blocks)
    xs = _dispatch(dest, zlist, h2.reshape(n_all, ROW_SUB, LANES), n_blocks)
    yb = _moe(blk_e, n_used, nxt_e, parity, xs.reshape(n_blocks * MOE_BM * ROW_SUB, LANES), w_exp1[l],
              b_exp1[l].reshape(N_EXPERTS, 1, 2 * D_FF), w_exp2[l], b_exp2[l].reshape(N_EXPERTS, 1, d), n_blocks)
    y_p, y_s = _combine(dest, x2, route, row(g_norm_final), yb, n_p)

    mk_out = mk_p.reshape(1, bp, N_MEM, MEM_HEADS, MEM_WIDTH // MEM_HEADS)
    mv_out = mv_p.reshape(1, bp, N_MEM, MEM_HEADS, MEM_WIDTH // MEM_HEADS)
    return (y_p.reshape(bp, tp, d), y_s.reshape(bs, ts, d), shift_p.reshape(1, bp, SHIFT_WIDTH), s_p[None], mk_out,
            mv_out, shift_s.reshape(1, bs, SHIFT_WIDTH), s_s[None],
            v_rows_s.reshape(1, bs, ts, GMLP_WIDTH))
```
